```python
import jax, jax.numpy as jnp
from jax import lax
import numpy as np

D_MODEL = 1024
BATCH = 8
SEQ = 8192
DEPTH = 1

N_MEM = 256
EPS = 1e-6
FOX_HEADS = 8
FOX_HEAD_DIM = 64
FOX_WIDTH = FOX_HEADS * FOX_HEAD_DIM
Q_BLOCK = 128
GDN_HEADS = 4
GDN_HEAD_DIM = 128
GDN_WIDTH = GDN_HEADS * GDN_HEAD_DIM
CONV_WIDTH = 4
CHUNK = 64
MIX_WIDTH = FOX_WIDTH + GDN_WIDTH
IN_PROJ_SIZES = (FOX_WIDTH, FOX_WIDTH, FOX_WIDTH, FOX_HEADS,
                 GDN_WIDTH, GDN_WIDTH, GDN_WIDTH, GDN_WIDTH, GDN_HEADS, GDN_HEADS)
IN_PROJ_WIDTH = sum(IN_PROJ_SIZES)
SPLIT_POINTS = tuple(np.cumsum(IN_PROJ_SIZES)[:-1].tolist())
MEM_HEADS = 4
MEM_HEAD_DIM = D_MODEL // MEM_HEADS
D_FF = 2816
FFN_RESIDUAL_WEIGHT = 0.5

kernel_name = 'hymba_fox_gdn_macaron_sandwich_memory'


def rms_norm(x, gain):
    xf = x.astype(jnp.float32)
    y = xf * lax.rsqrt(jnp.mean(xf * xf, axis=-1, keepdims=True) + EPS)
    return (y * gain.astype(jnp.float32)).astype(x.dtype)


def l2_norm(x):
    return x * lax.rsqrt(jnp.sum(x * x, axis=-1, keepdims=True) + EPS)


def swiglu(x, w_gate, w_up, w_down):
    return (jax.nn.silu(x @ w_gate) * (x @ w_up)) @ w_down


def forgetting_attention(q, k, v, f_logit):
    B, S, H, Dh = q.shape
    nb = S // Q_BLOCK
    F = jnp.cumsum(jax.nn.log_sigmoid(f_logit.astype(jnp.float32)), axis=1).transpose(0, 2, 1)
    qb = (q * (Dh ** -0.5)).reshape(B, nb, Q_BLOCK, H, Dh).transpose(1, 0, 3, 2, 4)
    Fq = F.reshape(B, H, nb, Q_BLOCK).transpose(2, 0, 1, 3)
    q_pos = jnp.arange(S).reshape(nb, Q_BLOCK)
    k_pos = jnp.arange(S)

    def block(args):
        q_blk, F_blk, pos_blk = args
        s = jnp.einsum('bhqd,bshd->bhqs', q_blk, k, preferred_element_type=jnp.float32)
        s = s + F_blk[..., None] - F[:, :, None, :]
        s = jnp.where(pos_blk[:, None] >= k_pos[None, :], s, -jnp.inf)
        p = jax.nn.softmax(s, axis=-1).astype(v.dtype)
        return jnp.einsum('bhqs,bshd->bqhd', p, v)

    out = lax.map(block, (qb, Fq, q_pos))
    return out.transpose(1, 0, 2, 3, 4).reshape(B, S, H * Dh)


def causal_conv_silu(x, w):
    S = x.shape[1]
    xp = jnp.pad(x, ((0, 0), (CONV_WIDTH - 1, 0), (0, 0)))
    y = xp[:, 0:S] * w[0]
    for i in range(1, CONV_WIDTH):
        y = y + xp[:, i:i + S] * w[i]
    return jax.nn.silu(y)


def gated_delta_rule_chunked(q, k, v, g, beta):
    B, S, H, Dk = q.shape
    Dv = v.shape[-1]
    n = S // CHUNK

    def chunks(t):
        return t.reshape(B, n, CHUNK, H, -1).transpose(0, 3, 1, 2, 4)

    q = chunks(q) * (Dk ** -0.5)
    k = chunks(k)
    v = chunks(v)
    beta = beta.reshape(B, n, CHUNK, H).transpose(0, 3, 1, 2)
    g = jnp.cumsum(g.reshape(B, n, CHUNK, H).transpose(0, 3, 1, 2), axis=-1)
    k_beta = k * beta[..., None]
    v_beta = v * beta[..., None]
    incl = jnp.tril(jnp.ones((CHUNK, CHUNK), dtype=bool))
    strict = jnp.tril(jnp.ones((CHUNK, CHUNK), dtype=bool), -1)
    decay = jnp.exp(jnp.where(incl, g[..., :, None] - g[..., None, :], -jnp.inf))
    L = jnp.where(strict, jnp.einsum('bhncd,bhnsd->bhncs', k_beta, k) * decay, 0.0)
    A = L + jnp.eye(CHUNK, dtype=L.dtype)
    u = lax.linalg.triangular_solve(A, v_beta, left_side=True, lower=True)
    w = lax.linalg.triangular_solve(A, k_beta * jnp.exp(g)[..., None], left_side=True, lower=True)
    attn_intra = jnp.einsum('bhncd,bhnsd->bhncs', q, k) * decay
    q_decay = q * jnp.exp(g)[..., None]
    g_last = g[..., -1]
    k_tail = k * jnp.exp(g_last[..., None] - g)[..., None]

    def step(state, xs):
        w_c, u_c, qd_c, a_c, kt_c, gl_c = xs
        v_new = u_c - jnp.einsum('bhcd,bhde->bhce', w_c, state)
        o = jnp.einsum('bhcd,bhde->bhce', qd_c, state) + jnp.einsum('bhcs,bhse->bhce', a_c, v_new)
        state = state * jnp.exp(gl_c)[..., None, None] + jnp.einsum('bhcd,bhce->bhde', kt_c, v_new)
        return state, o

    xs = (jnp.moveaxis(w, 2, 0), jnp.moveaxis(u, 2, 0), jnp.moveaxis(q_decay, 2, 0),
          jnp.moveaxis(attn_intra, 2, 0), jnp.moveaxis(k_tail, 2, 0), jnp.moveaxis(g_last, 2, 0))
    state0 = jnp.zeros((B, H, Dk, Dv), jnp.float32)
    _, o = lax.scan(step, state0, xs)
    return o.transpose(1, 0, 3, 2, 4).reshape(B, S, H, Dv)


def hybrid_mixer(u, w_in, fox_f_bias, gdn_conv_w, gdn_a_log, gdn_dt_bias, gdn_out_norm):
    B, S, _ = u.shape
    proj = u @ w_in
    fq, fk, fv, ff, gq, gk, gv, gz, gb, ga = jnp.split(proj, SPLIT_POINTS, axis=-1)
    fox = forgetting_attention(fq.reshape(B, S, FOX_HEADS, FOX_HEAD_DIM),
                               fk.reshape(B, S, FOX_HEADS, FOX_HEAD_DIM),
                               fv.reshape(B, S, FOX_HEADS, FOX_HEAD_DIM),
                               ff + fox_f_bias)
    qkv = causal_conv_silu(jnp.concatenate([gq, gk, gv], axis=-1).astype(jnp.float32),
                           gdn_conv_w.astype(jnp.float32))
    cq, ck, cv = jnp.split(qkv, (GDN_WIDTH, 2 * GDN_WIDTH), axis=-1)
    q = l2_norm(cq.reshape(B, S, GDN_HEADS, GDN_HEAD_DIM))
    k = l2_norm(ck.reshape(B, S, GDN_HEADS, GDN_HEAD_DIM))
    v = cv.reshape(B, S, GDN_HEADS, GDN_HEAD_DIM)
    beta = jax.nn.sigmoid(gb.astype(jnp.float32))
    g = -jnp.exp(gdn_a_log.astype(jnp.float32)) * jax.nn.softplus(
        ga.astype(jnp.float32) + gdn_dt_bias.astype(jnp.float32))
    o = gated_delta_rule_chunked(q, k, v, g, beta)
    o = rms_norm(o, gdn_out_norm) * jax.nn.silu(
        gz.astype(jnp.float32).reshape(B, S, GDN_HEADS, GDN_HEAD_DIM))
    gdn = o.reshape(B, S, GDN_WIDTH).astype(u.dtype)
    return jnp.concatenate([fox.astype(u.dtype), gdn], axis=-1)


def memory_cross_attention(h, m, w_q, w_kv, w_o):
    B, S, _ = h.shape
    q = (h @ w_q).reshape(B, S, MEM_HEADS, MEM_HEAD_DIM)
    k, v = jnp.split(m @ w_kv, 2, axis=-1)
    k = k.reshape(B, -1, MEM_HEADS, MEM_HEAD_DIM)
    v = v.reshape(B, -1, MEM_HEADS, MEM_HEAD_DIM)
    s = jnp.einsum('bqhd,bmhd->bhqm', q, k, preferred_element_type=jnp.float32) * (MEM_HEAD_DIM ** -0.5)
    p = jax.nn.softmax(s, axis=-1).astype(v.dtype)
    o = jnp.einsum('bhqm,bmhd->bqhd', p, v).reshape(B, S, D_MODEL)
    return o @ w_o


def _fwd_setup_inputs(seed: int = 0) -> dict:
    key = jax.random.key(seed)
    ks = iter(jax.random.split(key, 40))

    def normal(shape, scale):
        return jax.random.normal(next(ks), shape, jnp.float32) * scale

    def gain(width):
        return 1.0 + normal((DEPTH, width), 0.02)

    L = DEPTH
    dt = jnp.exp(jax.random.uniform(next(ks), (L, GDN_HEADS), jnp.float32,
                                    minval=float(np.log(1e-3)), maxval=float(np.log(1e-1))))
    return {
        'x': normal((BATCH, SEQ, D_MODEL), 1.0),
        'mem': normal((BATCH, N_MEM, D_MODEL), 1.0),
        'ffn1_pre_norm': gain(D_MODEL),
        'ffn1_w_gate': normal((L, D_MODEL, D_FF), D_MODEL ** -0.5),
        'ffn1_w_up': normal((L, D_MODEL, D_FF), D_MODEL ** -0.5),
        'ffn1_w_down': normal((L, D_FF, D_MODEL), D_FF ** -0.5),
        'ffn1_post_norm': gain(D_MODEL),
        'mix_pre_norm': gain(D_MODEL),
        'w_in': normal((L, D_MODEL, IN_PROJ_WIDTH), D_MODEL ** -0.5),
        'fox_f_bias': jax.random.uniform(next(ks), (L, FOX_HEADS), jnp.float32, minval=1.0, maxval=4.0),
        'gdn_conv_w': normal((L, CONV_WIDTH, 3 * GDN_WIDTH), CONV_WIDTH ** -0.5),
        'gdn_a_log': jnp.log(jax.random.uniform(next(ks), (L, GDN_HEADS), jnp.float32, minval=1.0, maxval=16.0)),
        'gdn_dt_bias': dt + jnp.log(-jnp.expm1(-dt)),
        'gdn_out_norm': gain(GDN_HEAD_DIM),
        'w_out': normal((L, MIX_WIDTH, D_MODEL), MIX_WIDTH ** -0.5),
        'mix_post_norm': gain(D_MODEL),
        'mem_pre_norm': gain(D_MODEL),
        'mem_kv_norm': gain(D_MODEL),
        'mem_w_q': normal((L, D_MODEL, D_MODEL), D_MODEL ** -0.5),
        'mem_w_kv': normal((L, D_MODEL, 2 * D_MODEL), D_MODEL ** -0.5),
        'mem_w_o': normal((L, D_MODEL, D_MODEL), D_MODEL ** -0.5),
        'mem_post_norm': gain(D_MODEL),
        'ffn2_pre_norm': gain(D_MODEL),
        'ffn2_w_gate': normal((L, D_MODEL, D_FF), D_MODEL ** -0.5),
        'ffn2_w_up': normal((L, D_MODEL, D_FF), D_MODEL ** -0.5),
        'ffn2_w_down': normal((L, D_FF, D_MODEL), D_FF ** -0.5),
        'ffn2_post_norm': gain(D_MODEL),
    }


def _fwd_reference(x, mem, ffn1_pre_norm, ffn1_w_gate, ffn1_w_up, ffn1_w_down, ffn1_post_norm,
              mix_pre_norm, w_in, fox_f_bias, gdn_conv_w, gdn_a_log, gdn_dt_bias, gdn_out_norm,
              w_out, mix_post_norm, mem_pre_norm, mem_kv_norm, mem_w_q, mem_w_kv, mem_w_o,
              mem_post_norm, ffn2_pre_norm, ffn2_w_gate, ffn2_w_up, ffn2_w_down, ffn2_post_norm):
    h = x
    for l in range(DEPTH):
        f = swiglu(rms_norm(h, ffn1_pre_norm[l]), ffn1_w_gate[l], ffn1_w_up[l], ffn1_w_down[l])
        h = h + FFN_RESIDUAL_WEIGHT * rms_norm(f, ffn1_post_norm[l])
        mixed = hybrid_mixer(rms_norm(h, mix_pre_norm[l]), w_in[l], fox_f_bias[l], gdn_conv_w[l],
                             gdn_a_log[l], gdn_dt_bias[l], gdn_out_norm[l])
        h = h + rms_norm(mixed @ w_out[l], mix_post_norm[l])
        c = memory_cross_attention(rms_norm(h, mem_pre_norm[l]), rms_norm(mem, mem_kv_norm[l]),
                                   mem_w_q[l], mem_w_kv[l], mem_w_o[l])
        h = h + rms_norm(c, mem_post_norm[l])
        f = swiglu(rms_norm(h, ffn2_pre_norm[l]), ffn2_w_gate[l], ffn2_w_up[l], ffn2_w_down[l])
        h = h + FFN_RESIDUAL_WEIGHT * rms_norm(f, ffn2_post_norm[l])
    return h


import jax as _jax
import jax.numpy as _jnp

TWIN_FORMAT = 'train_step'
FWD_PARAMS = ['x', 'mem', 'ffn1_pre_norm', 'ffn1_w_gate', 'ffn1_w_up', 'ffn1_w_down', 'ffn1_post_norm', 'mix_pre_norm', 'w_in', 'fox_f_bias', 'gdn_conv_w', 'gdn_a_log', 'gdn_dt_bias', 'gdn_out_norm', 'w_out', 'mix_post_norm', 'mem_pre_norm', 'mem_kv_norm', 'mem_w_q', 'mem_w_kv', 'mem_w_o', 'mem_post_norm', 'ffn2_pre_norm', 'ffn2_w_gate', 'ffn2_w_up', 'ffn2_w_down', 'ffn2_post_norm']
TWIN_WEIGHTS = ['ffn1_pre_norm', 'ffn1_w_gate', 'ffn1_w_up', 'ffn1_w_down', 'ffn1_post_norm', 'mix_pre_norm', 'w_in', 'fox_f_bias', 'gdn_conv_w', 'gdn_a_log', 'gdn_dt_bias', 'gdn_out_norm', 'w_out', 'mix_post_norm', 'mem_pre_norm', 'mem_kv_norm', 'mem_w_q', 'mem_w_kv', 'mem_w_o', 'mem_post_norm', 'ffn2_pre_norm', 'ffn2_w_gate', 'ffn2_w_up', 'ffn2_w_down', 'ffn2_post_norm']
TWIN_DIFF_INPUT = 'x'
TWIN_INPUTS = ['x', 'mem', 'ffn1_pre_norm', 'ffn1_w_gate', 'ffn1_w_up', 'ffn1_w_down', 'ffn1_post_norm', 'mix_pre_norm', 'w_in', 'fox_f_bias', 'gdn_conv_w', 'gdn_a_log', 'gdn_dt_bias', 'gdn_out_norm', 'w_out', 'mix_post_norm', 'mem_pre_norm', 'mem_kv_norm', 'mem_w_q', 'mem_w_kv', 'mem_w_o', 'mem_post_norm', 'ffn2_pre_norm', 'ffn2_w_gate', 'ffn2_w_up', 'ffn2_w_down', 'ffn2_post_norm', 'loss_target', 'm_ffn1_pre_norm', 'm_ffn1_w_gate', 'm_ffn1_w_up', 'm_ffn1_w_down', 'm_ffn1_post_norm', 'm_mix_pre_norm', 'm_w_in', 'm_fox_f_bias', 'm_gdn_conv_w', 'm_gdn_a_log', 'm_gdn_dt_bias', 'm_gdn_out_norm', 'm_w_out', 'm_mix_post_norm', 'm_mem_pre_norm', 'm_mem_kv_norm', 'm_mem_w_q', 'm_mem_w_kv', 'm_mem_w_o', 'm_mem_post_norm', 'm_ffn2_pre_norm', 'm_ffn2_w_gate', 'm_ffn2_w_up', 'm_ffn2_w_down', 'm_ffn2_post_norm', 'v_ffn1_pre_norm', 'v_ffn1_w_gate', 'v_ffn1_w_up', 'v_ffn1_w_down', 'v_ffn1_post_norm', 'v_mix_pre_norm', 'v_w_in', 'v_fox_f_bias', 'v_gdn_conv_w', 'v_gdn_a_log', 'v_gdn_dt_bias', 'v_gdn_out_norm', 'v_w_out', 'v_mix_post_norm', 'v_mem_pre_norm', 'v_mem_kv_norm', 'v_mem_w_q', 'v_mem_w_kv', 'v_mem_w_o', 'v_mem_post_norm', 'v_ffn2_pre_norm', 'v_ffn2_w_gate', 'v_ffn2_w_up', 'v_ffn2_w_down', 'v_ffn2_post_norm']
TWIN_OUTPUTS = ['loss', 'grad_x', 'grad_ffn1_pre_norm', 'grad_ffn1_w_gate', 'grad_ffn1_w_up', 'grad_ffn1_w_down', 'grad_ffn1_post_norm', 'grad_mix_pre_norm', 'grad_w_in', 'grad_fox_f_bias', 'grad_gdn_conv_w', 'grad_gdn_a_log', 'grad_gdn_dt_bias', 'grad_gdn_out_norm', 'grad_w_out', 'grad_mix_post_norm', 'grad_mem_pre_norm', 'grad_mem_kv_norm', 'grad_mem_w_q', 'grad_mem_w_kv', 'grad_mem_w_o', 'grad_mem_post_norm', 'grad_ffn2_pre_norm', 'grad_ffn2_w_gate', 'grad_ffn2_w_up', 'grad_ffn2_w_down', 'grad_ffn2_post_norm', 'delta_ffn1_pre_norm', 'delta_ffn1_w_gate', 'delta_ffn1_w_up', 'delta_ffn1_w_down', 'delta_ffn1_post_norm', 'delta_mix_pre_norm', 'delta_w_in', 'delta_fox_f_bias', 'delta_gdn_conv_w', 'delta_gdn_a_log', 'delta_gdn_dt_bias', 'delta_gdn_out_norm', 'delta_w_out', 'delta_mix_post_norm', 'delta_mem_pre_norm', 'delta_mem_kv_norm', 'delta_mem_w_q', 'delta_mem_w_kv', 'delta_mem_w_o', 'delta_mem_post_norm', 'delta_ffn2_pre_norm', 'delta_ffn2_w_gate', 'delta_ffn2_w_up', 'delta_ffn2_w_down', 'delta_ffn2_post_norm', 'new_m_ffn1_pre_norm', 'new_m_ffn1_w_gate', 'new_m_ffn1_w_up', 'new_m_ffn1_w_down', 'new_m_ffn1_post_norm', 'new_m_mix_pre_norm', 'new_m_w_in', 'new_m_fox_f_bias', 'new_m_gdn_conv_w', 'new_m_gdn_a_log', 'new_m_gdn_dt_bias', 'new_m_gdn_out_norm', 'new_m_w_out', 'new_m_mix_post_norm', 'new_m_mem_pre_norm', 'new_m_mem_kv_norm', 'new_m_mem_w_q', 'new_m_mem_w_kv', 'new_m_mem_w_o', 'new_m_mem_post_norm', 'new_m_ffn2_pre_norm', 'new_m_ffn2_w_gate', 'new_m_ffn2_w_up', 'new_m_ffn2_w_down', 'new_m_ffn2_post_norm', 'new_v_ffn1_pre_norm', 'new_v_ffn1_w_gate', 'new_v_ffn1_w_up', 'new_v_ffn1_w_down', 'new_v_ffn1_post_norm', 'new_v_mix_pre_norm', 'new_v_w_in', 'new_v_fox_f_bias', 'new_v_gdn_conv_w', 'new_v_gdn_a_log', 'new_v_gdn_dt_bias', 'new_v_gdn_out_norm', 'new_v_w_out', 'new_v_mix_post_norm', 'new_v_mem_pre_norm', 'new_v_mem_kv_norm', 'new_v_mem_w_q', 'new_v_mem_w_kv', 'new_v_mem_w_o', 'new_v_mem_post_norm', 'new_v_ffn2_pre_norm', 'new_v_ffn2_w_gate', 'new_v_ffn2_w_up', 'new_v_ffn2_w_down', 'new_v_ffn2_post_norm']
TWIN_LEAF_KINDS = {'loss': 'loss', 'grad_x': 'grad_x', 'grad_ffn1_pre_norm': 'grad_w', 'grad_ffn1_w_gate': 'grad_w', 'grad_ffn1_w_up': 'grad_w', 'grad_ffn1_w_down': 'grad_w', 'grad_ffn1_post_norm': 'grad_w', 'grad_mix_pre_norm': 'grad_w', 'grad_w_in': 'grad_w', 'grad_fox_f_bias': 'grad_w', 'grad_gdn_conv_w': 'grad_w', 'grad_gdn_a_log': 'grad_w', 'grad_gdn_dt_bias': 'grad_w', 'grad_gdn_out_norm': 'grad_w', 'grad_w_out': 'grad_w', 'grad_mix_post_norm': 'grad_w', 'grad_mem_pre_norm': 'grad_w', 'grad_mem_kv_norm': 'grad_w', 'grad_mem_w_q': 'grad_w', 'grad_mem_w_kv': 'grad_w', 'grad_mem_w_o': 'grad_w', 'grad_mem_post_norm': 'grad_w', 'grad_ffn2_pre_norm': 'grad_w', 'grad_ffn2_w_gate': 'grad_w', 'grad_ffn2_w_up': 'grad_w', 'grad_ffn2_w_down': 'grad_w', 'grad_ffn2_post_norm': 'grad_w', 'delta_ffn1_pre_norm': 'delta_w', 'delta_ffn1_w_gate': 'delta_w', 'delta_ffn1_w_up': 'delta_w', 'delta_ffn1_w_down': 'delta_w', 'delta_ffn1_post_norm': 'delta_w', 'delta_mix_pre_norm': 'delta_w', 'delta_w_in': 'delta_w', 'delta_fox_f_bias': 'delta_w', 'delta_gdn_conv_w': 'delta_w', 'delta_gdn_a_log': 'delta_w', 'delta_gdn_dt_bias': 'delta_w', 'delta_gdn_out_norm': 'delta_w', 'delta_w_out': 'delta_w', 'delta_mix_post_norm': 'delta_w', 'delta_mem_pre_norm': 'delta_w', 'delta_mem_kv_norm': 'delta_w', 'delta_mem_w_q': 'delta_w', 'delta_mem_w_kv': 'delta_w', 'delta_mem_w_o': 'delta_w', 'delta_mem_post_norm': 'delta_w', 'delta_ffn2_pre_norm': 'delta_w', 'delta_ffn2_w_gate': 'delta_w', 'delta_ffn2_w_up': 'delta_w', 'delta_ffn2_w_down': 'delta_w', 'delta_ffn2_post_norm': 'delta_w', 'new_m_ffn1_pre_norm': 'new_m', 'new_m_ffn1_w_gate': 'new_m', 'new_m_ffn1_w_up': 'new_m', 'new_m_ffn1_w_down': 'new_m', 'new_m_ffn1_post_norm': 'new_m', 'new_m_mix_pre_norm': 'new_m', 'new_m_w_in': 'new_m', 'new_m_fox_f_bias': 'new_m', 'new_m_gdn_conv_w': 'new_m', 'new_m_gdn_a_log': 'new_m', 'new_m_gdn_dt_bias': 'new_m', 'new_m_gdn_out_norm': 'new_m', 'new_m_w_out': 'new_m', 'new_m_mix_post_norm': 'new_m', 'new_m_mem_pre_norm': 'new_m', 'new_m_mem_kv_norm': 'new_m', 'new_m_mem_w_q': 'new_m', 'new_m_mem_w_kv': 'new_m', 'new_m_mem_w_o': 'new_m', 'new_m_mem_post_norm': 'new_m', 'new_m_ffn2_pre_norm': 'new_m', 'new_m_ffn2_w_gate': 'new_m', 'new_m_ffn2_w_up': 'new_m', 'new_m_ffn2_w_down': 'new_m', 'new_m_ffn2_post_norm': 'new_m', 'new_v_ffn1_pre_norm': 'new_v', 'new_v_ffn1_w_gate': 'new_v', 'new_v_ffn1_w_up': 'new_v', 'new_v_ffn1_w_down': 'new_v', 'new_v_ffn1_post_norm': 'new_v', 'new_v_mix_pre_norm': 'new_v', 'new_v_w_in': 'new_v', 'new_v_fox_f_bias': 'new_v', 'new_v_gdn_conv_w': 'new_v', 'new_v_gdn_a_log': 'new_v', 'new_v_gdn_dt_bias': 'new_v', 'new_v_gdn_out_norm': 'new_v', 'new_v_w_out': 'new_v', 'new_v_mix_post_norm': 'new_v', 'new_v_mem_pre_norm': 'new_v', 'new_v_mem_kv_norm': 'new_v', 'new_v_mem_w_q': 'new_v', 'new_v_mem_w_kv': 'new_v', 'new_v_mem_w_o': 'new_v', 'new_v_mem_post_norm': 'new_v', 'new_v_ffn2_pre_norm': 'new_v', 'new_v_ffn2_w_gate': 'new_v', 'new_v_ffn2_w_up': 'new_v', 'new_v_ffn2_w_down': 'new_v', 'new_v_ffn2_post_norm': 'new_v'}


def _forward(args):
    return _fwd_reference(*[args[k] for k in FWD_PARAMS])


def _output_shape():
    out = _jax.eval_shape(lambda: _forward(_fwd_setup_inputs(0)))
    return out.shape, out.dtype

N_MICROBATCH = 1
ADAM_LR = 0.001
ADAM_B1 = 0.9
ADAM_B2 = 0.999
ADAM_EPS = 1e-08
ADAM_WD = 0.01
ADAM_STEP = 10
PER_EXAMPLE_BATCH_AXIS = {'x': 0, 'mem': 0, 'loss_target': 0}
SHARED_INPUTS = []
_WEIGHT_DTYPES = {'ffn1_pre_norm': _jnp.float32, 'ffn1_w_gate': _jnp.float32, 'ffn1_w_up': _jnp.float32, 'ffn1_w_down': _jnp.float32, 'ffn1_post_norm': _jnp.float32, 'mix_pre_norm': _jnp.float32, 'w_in': _jnp.float32, 'fox_f_bias': _jnp.float32, 'gdn_conv_w': _jnp.float32, 'gdn_a_log': _jnp.float32, 'gdn_dt_bias': _jnp.float32, 'gdn_out_norm': _jnp.float32, 'w_out': _jnp.float32, 'mix_post_norm': _jnp.float32, 'mem_pre_norm': _jnp.float32, 'mem_kv_norm': _jnp.float32, 'mem_w_q': _jnp.float32, 'mem_w_kv': _jnp.float32, 'mem_w_o': _jnp.float32, 'mem_post_norm': _jnp.float32, 'ffn2_pre_norm': _jnp.float32, 'ffn2_w_gate': _jnp.float32, 'ffn2_w_up': _jnp.float32, 'ffn2_w_down': _jnp.float32, 'ffn2_post_norm': _jnp.float32}
MOMENT_SCALE = {'ffn1_pre_norm': 1.009273e+00, 'ffn1_w_gate': 3.732953e-01, 'ffn1_w_up': 3.810218e-01, 'ffn1_w_down': 6.383137e-01, 'ffn1_post_norm': 1.507720e+01, 'mix_pre_norm': 1.366935e+00, 'w_in': 6.411963e-01, 'fox_f_bias': 6.861921e+00, 'gdn_conv_w': 1.465802e+00, 'gdn_a_log': 6.814098e+00, 'gdn_dt_bias': 6.526776e+00, 'gdn_out_norm': 8.574505e+00, 'w_out': 2.685488e+00, 'mix_post_norm': 6.403240e+01, 'mem_pre_norm': 1.028545e+00, 'mem_kv_norm': 3.374490e+00, 'mem_w_q': 1.008061e+00, 'mem_w_kv': 2.314176e+00, 'mem_w_o': 3.441062e+00, 'mem_post_norm': 6.552654e+01, 'ffn2_pre_norm': 1.277244e+00, 'ffn2_w_gate': 4.338146e-01, 'ffn2_w_up': 7.620278e-01, 'ffn2_w_down': 1.281597e+00, 'ffn2_post_norm': 1.596826e+01}


def _to_microbatches(a, axis):
    t = _jnp.moveaxis(a, axis, 0)
    t = t.reshape((N_MICROBATCH, t.shape[0] // N_MICROBATCH) + t.shape[1:])
    return _jnp.moveaxis(t, 1, axis + 1)


def setup_inputs(seed: int = 0) -> dict:
    inp = _fwd_setup_inputs(seed)
    key = _jax.random.fold_in(_jax.random.key(seed), 7919)
    shape, _ = _output_shape()
    out = dict(inp)
    out["loss_target"] = _jax.random.normal(_jax.random.fold_in(key, 0), shape, _jnp.float32)
    for i, name in enumerate(TWIN_WEIGHTS):
        w = inp[name].astype(_jnp.float32)
        if MOMENT_SCALE is None:
            s = _jnp.sqrt(_jnp.mean(_jnp.square(w)) + 1e-30)
        else:
            s = MOMENT_SCALE[name]
        km, kv = _jax.random.split(_jax.random.fold_in(key, i + 1))
        out[name] = w
        out["m_" + name] = s * _jax.random.normal(km, w.shape, _jnp.float32)
        out["v_" + name] = (s * s) * _jax.random.uniform(kv, w.shape, _jnp.float32, 0.5, 1.5)
    if N_MICROBATCH > 1:
        for name, axis in PER_EXAMPLE_BATCH_AXIS.items():
            out[name] = _to_microbatches(out[name], axis)
    return {'x': out['x'], 'mem': out['mem'], 'ffn1_pre_norm': out['ffn1_pre_norm'], 'ffn1_w_gate': out['ffn1_w_gate'], 'ffn1_w_up': out['ffn1_w_up'], 'ffn1_w_down': out['ffn1_w_down'], 'ffn1_post_norm': out['ffn1_post_norm'], 'mix_pre_norm': out['mix_pre_norm'], 'w_in': out['w_in'], 'fox_f_bias': out['fox_f_bias'], 'gdn_conv_w': out['gdn_conv_w'], 'gdn_a_log': out['gdn_a_log'], 'gdn_dt_bias': out['gdn_dt_bias'], 'gdn_out_norm': out['gdn_out_norm'], 'w_out': out['w_out'], 'mix_post_norm': out['mix_post_norm'], 'mem_pre_norm': out['mem_pre_norm'], 'mem_kv_norm': out['mem_kv_norm'], 'mem_w_q': out['mem_w_q'], 'mem_w_kv': out['mem_w_kv'], 'mem_w_o': out['mem_w_o'], 'mem_post_norm': out['mem_post_norm'], 'ffn2_pre_norm': out['ffn2_pre_norm'], 'ffn2_w_gate': out['ffn2_w_gate'], 'ffn2_w_up': out['ffn2_w_up'], 'ffn2_w_down': out['ffn2_w_down'], 'ffn2_post_norm': out['ffn2_post_norm'], 'loss_target': out['loss_target'], 'm_ffn1_pre_norm': out['m_ffn1_pre_norm'], 'm_ffn1_w_gate': out['m_ffn1_w_gate'], 'm_ffn1_w_up': out['m_ffn1_w_up'], 'm_ffn1_w_down': out['m_ffn1_w_down'], 'm_ffn1_post_norm': out['m_ffn1_post_norm'], 'm_mix_pre_norm': out['m_mix_pre_norm'], 'm_w_in': out['m_w_in'], 'm_fox_f_bias': out['m_fox_f_bias'], 'm_gdn_conv_w': out['m_gdn_conv_w'], 'm_gdn_a_log': out['m_gdn_a_log'], 'm_gdn_dt_bias': out['m_gdn_dt_bias'], 'm_gdn_out_norm': out['m_gdn_out_norm'], 'm_w_out': out['m_w_out'], 'm_mix_post_norm': out['m_mix_post_norm'], 'm_mem_pre_norm': out['m_mem_pre_norm'], 'm_mem_kv_norm': out['m_mem_kv_norm'], 'm_mem_w_q': out['m_mem_w_q'], 'm_mem_w_kv': out['m_mem_w_kv'], 'm_mem_w_o': out['m_mem_w_o'], 'm_mem_post_norm': out['m_mem_post_norm'], 'm_ffn2_pre_norm': out['m_ffn2_pre_norm'], 'm_ffn2_w_gate': out['m_ffn2_w_gate'], 'm_ffn2_w_up': out['m_ffn2_w_up'], 'm_ffn2_w_down': out['m_ffn2_w_down'], 'm_ffn2_post_norm': out['m_ffn2_post_norm'], 'v_ffn1_pre_norm': out['v_ffn1_pre_norm'], 'v_ffn1_w_gate': out['v_ffn1_w_gate'], 'v_ffn1_w_up': out['v_ffn1_w_up'], 'v_ffn1_w_down': out['v_ffn1_w_down'], 'v_ffn1_post_norm': out['v_ffn1_post_norm'], 'v_mix_pre_norm': out['v_mix_pre_norm'], 'v_w_in': out['v_w_in'], 'v_fox_f_bias': out['v_fox_f_bias'], 'v_gdn_conv_w': out['v_gdn_conv_w'], 'v_gdn_a_log': out['v_gdn_a_log'], 'v_gdn_dt_bias': out['v_gdn_dt_bias'], 'v_gdn_out_norm': out['v_gdn_out_norm'], 'v_w_out': out['v_w_out'], 'v_mix_post_norm': out['v_mix_post_norm'], 'v_mem_pre_norm': out['v_mem_pre_norm'], 'v_mem_kv_norm': out['v_mem_kv_norm'], 'v_mem_w_q': out['v_mem_w_q'], 'v_mem_w_kv': out['v_mem_w_kv'], 'v_mem_w_o': out['v_mem_w_o'], 'v_mem_post_norm': out['v_mem_post_norm'], 'v_ffn2_pre_norm': out['v_ffn2_pre_norm'], 'v_ffn2_w_gate': out['v_ffn2_w_gate'], 'v_ffn2_w_up': out['v_ffn2_w_up'], 'v_ffn2_w_down': out['v_ffn2_w_down'], 'v_ffn2_post_norm': out['v_ffn2_post_norm']}


def _loss(weights, diff, rest, loss_target):
    with _jax.named_scope("forward"):
        args = {**rest, TWIN_DIFF_INPUT: diff, **{k: w.astype(_WEIGHT_DTYPES[k]) for k, w in weights.items()}}
        y = _forward(args)
    with _jax.named_scope("loss_head"):
        err = _jnp.square(y.astype(_jnp.float32) - loss_target)
        return 0.5 * _jnp.sum(_jnp.mean(err, axis=-1)) if err.ndim else 0.5 * err


def _adamw(w, g, m, v):
    m = ADAM_B1 * m + (1.0 - ADAM_B1) * g
    v = ADAM_B2 * v + (1.0 - ADAM_B2) * _jnp.square(g)
    m_hat = m / (1.0 - ADAM_B1 ** ADAM_STEP)
    v_hat = v / (1.0 - ADAM_B2 ** ADAM_STEP)
    delta = -ADAM_LR * (m_hat / (_jnp.sqrt(v_hat) + ADAM_EPS) + ADAM_WD * w)
    return delta, m, v


def reference(x, mem, ffn1_pre_norm, ffn1_w_gate, ffn1_w_up, ffn1_w_down, ffn1_post_norm, mix_pre_norm, w_in, fox_f_bias, gdn_conv_w, gdn_a_log, gdn_dt_bias, gdn_out_norm, w_out, mix_post_norm, mem_pre_norm, mem_kv_norm, mem_w_q, mem_w_kv, mem_w_o, mem_post_norm, ffn2_pre_norm, ffn2_w_gate, ffn2_w_up, ffn2_w_down, ffn2_post_norm, loss_target, m_ffn1_pre_norm, m_ffn1_w_gate, m_ffn1_w_up, m_ffn1_w_down, m_ffn1_post_norm, m_mix_pre_norm, m_w_in, m_fox_f_bias, m_gdn_conv_w, m_gdn_a_log, m_gdn_dt_bias, m_gdn_out_norm, m_w_out, m_mix_post_norm, m_mem_pre_norm, m_mem_kv_norm, m_mem_w_q, m_mem_w_kv, m_mem_w_o, m_mem_post_norm, m_ffn2_pre_norm, m_ffn2_w_gate, m_ffn2_w_up, m_ffn2_w_down, m_ffn2_post_norm, v_ffn1_pre_norm, v_ffn1_w_gate, v_ffn1_w_up, v_ffn1_w_down, v_ffn1_post_norm, v_mix_pre_norm, v_w_in, v_fox_f_bias, v_gdn_conv_w, v_gdn_a_log, v_gdn_dt_bias, v_gdn_out_norm, v_w_out, v_mix_post_norm, v_mem_pre_norm, v_mem_kv_norm, v_mem_w_q, v_mem_w_kv, v_mem_w_o, v_mem_post_norm, v_ffn2_pre_norm, v_ffn2_w_gate, v_ffn2_w_up, v_ffn2_w_down, v_ffn2_post_norm):
    given = dict(x=x, mem=mem, ffn1_pre_norm=ffn1_pre_norm, ffn1_w_gate=ffn1_w_gate, ffn1_w_up=ffn1_w_up, ffn1_w_down=ffn1_w_down, ffn1_post_norm=ffn1_post_norm, mix_pre_norm=mix_pre_norm, w_in=w_in, fox_f_bias=fox_f_bias, gdn_conv_w=gdn_conv_w, gdn_a_log=gdn_a_log, gdn_dt_bias=gdn_dt_bias, gdn_out_norm=gdn_out_norm, w_out=w_out, mix_post_norm=mix_post_norm, mem_pre_norm=mem_pre_norm, mem_kv_norm=mem_kv_norm, mem_w_q=mem_w_q, mem_w_kv=mem_w_kv, mem_w_o=mem_w_o, mem_post_norm=mem_post_norm, ffn2_pre_norm=ffn2_pre_norm, ffn2_w_gate=ffn2_w_gate, ffn2_w_up=ffn2_w_up, ffn2_w_down=ffn2_w_down, ffn2_post_norm=ffn2_post_norm, loss_target=loss_target, m_ffn1_pre_norm=m_ffn1_pre_norm, m_ffn1_w_gate=m_ffn1_w_gate, m_ffn1_w_up=m_ffn1_w_up, m_ffn1_w_down=m_ffn1_w_down, m_ffn1_post_norm=m_ffn1_post_norm, m_mix_pre_norm=m_mix_pre_norm, m_w_in=m_w_in, m_fox_f_bias=m_fox_f_bias, m_gdn_conv_w=m_gdn_conv_w, m_gdn_a_log=m_gdn_a_log, m_gdn_dt_bias=m_gdn_dt_bias, m_gdn_out_norm=m_gdn_out_norm, m_w_out=m_w_out, m_mix_post_norm=m_mix_post_norm, m_mem_pre_norm=m_mem_pre_norm, m_mem_kv_norm=m_mem_kv_norm, m_mem_w_q=m_mem_w_q, m_mem_w_kv=m_mem_w_kv, m_mem_w_o=m_mem_w_o, m_mem_post_norm=m_mem_post_norm, m_ffn2_pre_norm=m_ffn2_pre_norm, m_ffn2_w_gate=m_ffn2_w_gate, m_ffn2_w_up=m_ffn2_w_up, m_ffn2_w_down=m_ffn2_w_down, m_ffn2_post_norm=m_ffn2_post_norm, v_ffn1_pre_norm=v_ffn1_pre_norm, v_ffn1_w_gate=v_ffn1_w_gate, v_ffn1_w_up=v_ffn1_w_up, v_ffn1_w_down=v_ffn1_w_down, v_ffn1_post_norm=v_ffn1_post_norm, v_mix_pre_norm=v_mix_pre_norm, v_w_in=v_w_in, v_fox_f_bias=v_fox_f_bias, v_gdn_conv_w=v_gdn_conv_w, v_gdn_a_log=v_gdn_a_log, v_gdn_dt_bias=v_gdn_dt_bias, v_gdn_out_norm=v_gdn_out_norm, v_w_out=v_w_out, v_mix_post_norm=v_mix_post_norm, v_mem_pre_norm=v_mem_pre_norm, v_mem_kv_norm=v_mem_kv_norm, v_mem_w_q=v_mem_w_q, v_mem_w_kv=v_mem_w_kv, v_mem_w_o=v_mem_w_o, v_mem_post_norm=v_mem_post_norm, v_ffn2_pre_norm=v_ffn2_pre_norm, v_ffn2_w_gate=v_ffn2_w_gate, v_ffn2_w_up=v_ffn2_w_up, v_ffn2_w_down=v_ffn2_w_down, v_ffn2_post_norm=v_ffn2_post_norm)
    weights = {n: given[n] for n in TWIN_WEIGHTS}
    shared = {n: given[n] for n in SHARED_INPUTS}
    per_example = {n: given[n] for n in ['x', 'mem']}
    grad_fn = _jax.value_and_grad(_loss, argnums=(0, 1))

    def one_microbatch(ex, loss_target):
        ex = dict(ex)
        diff = ex.pop(TWIN_DIFF_INPUT)
        return grad_fn(weights, diff, {**shared, **ex}, loss_target)

    if N_MICROBATCH == 1:
        loss, (grad_w, grad_x) = one_microbatch(per_example, given["loss_target"])
    else:
        def body(carry, xs):
            loss_sum, grad_sum = carry
            l_k, (gw_k, gx_k) = one_microbatch(xs[0], xs[1])
            with _jax.named_scope("update"):
                return (loss_sum + l_k, _jax.tree.map(_jnp.add, grad_sum, gw_k)), gx_k

        init = (_jnp.zeros((), _jnp.float32), _jax.tree.map(_jnp.zeros_like, weights))
        (loss, grad_w), grad_x = _jax.lax.scan(body, init, (per_example, given["loss_target"]))
    with _jax.named_scope("update"):
        delta_w, new_m, new_v = {}, {}, {}
        for n in TWIN_WEIGHTS:
            delta_w[n], new_m[n], new_v[n] = _adamw(weights[n], grad_w[n], given["m_" + n], given["v_" + n])
    return (loss, grad_x, *[grad_w[n] for n in TWIN_WEIGHTS], *[delta_w[n] for n in TWIN_WEIGHTS],
            *[new_m[n] for n in TWIN_WEIGHTS], *[new_v[n] for n in TWIN_WEIGHTS])
```

```python
import functools

import jax
import jax.numpy as jnp
from jax import lax
from jax.experimental import pallas as pl
from jax.experimental.pallas import tpu as pltpu

F32 = jnp.float32
BF = jnp.bfloat16
HI = lax.Precision.HIGHEST
MESH = pl.DeviceIdType.MESH

N_DEV = 8
EPS = 1e-6
D_MODEL = 1024
D_FF = 2816
FF_SHARD = D_FF // N_DEV
FF_SHARD_PAD = 384
D_FF_PAD = FF_SHARD_PAD * N_DEV
FOX_HEADS, FOX_DH = 8, 64
GDN_HEADS, GDN_DH = 4, 128
GDN_CHUNK = 64
CONV_W = 4
MEM_HEADS, MEM_DH = 4, 256
IN_W = 3600
IN_SHARD = IN_W // N_DEV
PROJ_W = 4096
SMALL_F, SMALL_B, SMALL_A = 0, 8, 12

ADAM_LR, ADAM_B1, ADAM_B2, ADAM_EPS, ADAM_WD, ADAM_STEP = 0.001, 0.9, 0.999, 1e-08, 0.01, 10

VMEM_LIMIT = 56 * 1024 * 1024


def _params(sem=None):
    return pltpu.CompilerParams(dimension_semantics=sem, vmem_limit_bytes=VMEM_LIMIT)


def _tile(n, pref, unit=128):
    if n <= pref:
        return n
    t = (pref // unit) * unit
    while t > unit and n % t:
        t -= unit
    assert n % t == 0, (n, pref)
    return t


@functools.partial(jax.custom_vjp, nondiff_argnums=(2, 3))
def bdot(a, b, ca, cb):
    return lax.dot_general(a.astype(BF), b.astype(BF), (((ca,), (cb,)), ((), ())), preferred_element_type=F32)


def _bdot_fwd(a, b, ca, cb):
    return bdot(a, b, ca, cb), (a, b)


def _bdot_bwd(ca, cb, res, g):
    a, b = res
    da = bdot(g, b, 1, 1 - cb) if ca == 1 else bdot(b, g, 1 - cb, 1)
    db = bdot(a, g, 1 - ca, 0) if cb == 0 else bdot(g, a, 0, 1 - ca)
    return da, db


bdot.defvjp(_bdot_fwd, _bdot_bwd)


def hdot(a, b):
    return jnp.dot(a, b, precision=HI, preferred_element_type=F32)


def _iota2(shape, dim):
    return lax.broadcasted_iota(jnp.int32, shape, dim)


def _sigmoid(x):
    return 1.0 / (1.0 + jnp.exp(-x))


def _silu(x):
    return x * _sigmoid(x)


def _softplus(x):
    return jnp.maximum(x, 0.0) + jnp.log(1.0 + jnp.exp(-jnp.abs(x)))


def _rms(x, gain):
    return x * lax.rsqrt(jnp.mean(x * x, axis=-1, keepdims=True) + EPS) * gain


def mm(a, b, *, name, ta=False, tb=False, out_dtype=F32, tm=512, tn=512, tk=1024):
    m, k = (a.shape[1], a.shape[0]) if ta else a.shape
    n = b.shape[0] if tb else b.shape[1]
    assert k == (b.shape[1] if tb else b.shape[0]), (a.shape, b.shape, ta, tb)
    tm, tn, tk = _tile(m, tm), _tile(n, tn), _tile(k, tk)
    nk = k // tk
    dims = (((0 if ta else 1,), (1 if tb else 0,)), ((), ()))

    def kern(a_ref, b_ref, o_ref, acc_ref):
        kk = pl.program_id(2)

        @pl.when(kk == 0)
        def _():
            acc_ref[...] = jnp.zeros_like(acc_ref)

        acc_ref[...] += lax.dot_general(a_ref[...].astype(BF), b_ref[...].astype(BF), dims,
                                        preferred_element_type=F32)

        @pl.when(kk == nk - 1)
        def _():
            o_ref[...] = acc_ref[...].astype(o_ref.dtype)

    a_spec = pl.BlockSpec((tk, tm), lambda i, j, kk: (kk, i)) if ta else pl.BlockSpec((tm, tk), lambda i, j, kk: (i, kk))
    b_spec = pl.BlockSpec((tn, tk), lambda i, j, kk: (j, kk)) if tb else pl.BlockSpec((tk, tn), lambda i, j, kk: (kk, j))
    return pl.pallas_call(
        kern, name=name, grid=(m // tm, n // tn, nk),
        in_specs=[a_spec, b_spec],
        out_specs=pl.BlockSpec((tm, tn), lambda i, j, kk: (i, j)),
        out_shape=jax.ShapeDtypeStruct((m, n), out_dtype),
        scratch_shapes=[pltpu.VMEM((tm, tn), F32)],
        compiler_params=_params(("parallel", "parallel", "arbitrary")),
    )(a, b)


def _row_spec(item, rows):
    if not isinstance(item, tuple):
        return item, pl.BlockSpec((rows, item.shape[1]), lambda i: (i, 0))
    if len(item) == 3:
        arr, w, c = item
        return arr, pl.BlockSpec((rows, w), lambda i: (i, c))
    arr, w, c, lead = item
    return arr, pl.BlockSpec((None, rows, w), lambda i: (lead, i, c))


def _whole_spec(item):
    if not isinstance(item, tuple):
        return item, pl.BlockSpec(item.shape, lambda i: (0,) * item.ndim)
    arr, w, c = item
    return arr, pl.BlockSpec((arr.shape[0], w), lambda i: (0, c))


def rowcall(body, tiled, whole, outs, accs=(), *, rows, total, name):
    rows = min(rows, total)
    assert total % rows == 0
    t_arr, t_spec = zip(*[_row_spec(t, rows) for t in tiled])
    w_arr, w_spec = zip(*[_whole_spec(w) for w in whole]) if whole else ((), ())
    nt, nw, no, na = len(t_arr), len(w_arr), len(outs), len(accs)

    def kern(*refs):
        vals = [r[...] for r in refs[:nt + nw]]
        res = body(*vals)
        if not isinstance(res, (tuple, list)):
            res = (res,)
        assert len(res) == no + na, (name, len(res), no, na)
        for r, v in zip(refs[nt + nw:nt + nw + no], res[:no]):
            r[...] = v.astype(r.dtype)
        if na:
            acc_refs = refs[nt + nw + no:]

            @pl.when(pl.program_id(0) == 0)
            def _():
                for r in acc_refs:
                    r[...] = jnp.zeros_like(r)

            for r, v in zip(acc_refs, res[no:]):
                r[...] += v

    out_shape = [jax.ShapeDtypeStruct((total, w), d) for w, d in outs] + [jax.ShapeDtypeStruct(s, F32) for s in accs]
    out_specs = [pl.BlockSpec((rows, w), lambda i: (i, 0)) for w, _ in outs] + \
                [pl.BlockSpec(s, lambda i: (0, 0)) for s in accs]
    res = pl.pallas_call(
        kern, name=name, grid=(total // rows,),
        in_specs=list(t_spec) + list(w_spec), out_specs=out_specs, out_shape=out_shape,
        compiler_params=_params(("arbitrary",) if na else ("parallel",)),
    )(*t_arr, *w_arr)
    return res


def _colsum(x):
    return jnp.sum(x, axis=0, keepdims=True)


def _gdn_chunk(q, k, v, z, gb, bb, state, gain):
    c = GDN_CHUNK
    r64, c64 = _iota2((c, c), 0), _iota2((c, c), 1)
    incl = r64 >= c64
    strict = r64 > c64
    ltri = incl.astype(F32)
    utri = (r64 <= c64).astype(F32)
    eye = (r64 == c64).astype(F32)
    pick = (_iota2((GDN_DH, c), 0) == _iota2((GDN_DH, c), 1)).astype(F32)
    last = (_iota2((c, GDN_DH), 0) == c - 1).astype(F32)

    qn = q * lax.rsqrt(jnp.sum(q * q, axis=-1, keepdims=True) + EPS) * (GDN_DH ** -0.5)
    kn = k * lax.rsqrt(jnp.sum(k * k, axis=-1, keepdims=True) + EPS)
    gc = hdot(ltri, gb)
    g64 = hdot(gb, pick)
    gcol = hdot(ltri, g64)
    grow = hdot(jnp.ones((c, c), F32), g64 * utri)
    dec = jnp.exp(jnp.where(incl, gcol - grow, -1e30))
    kb = kn * bb
    vb = v * bb
    lmat = jnp.where(strict, bdot(kb, kn, 1, 1) * dec, 0.0)
    p = -lmat
    tinv = eye + p
    for _ in range(5):
        p = hdot(p, p)
        tinv = tinv + hdot(tinv, p)
    u = hdot(tinv, vb)
    w = hdot(tinv, kb * jnp.exp(gc))
    attn = bdot(qn, kn, 1, 1) * dec
    qd = qn * jnp.exp(gc)
    gl = jnp.sum(gc * last, axis=0, keepdims=True)
    kt = kn * jnp.exp(gl - gc)
    v_new = u - bdot(w, state, 1, 0)
    o = bdot(qd, state, 1, 0) + bdot(attn, v_new, 1, 0)
    new_state = state * jnp.exp(gl) + bdot(kt, v_new, 0, 0)
    o = _rms(o, gain) * _silu(z)
    return o, new_state


GDN_ROWS = 512


def gdn_fwd(cqkv, proj, gbb, gain):
    s = cqkv.shape[0]
    nb, cpb = s // GDN_ROWS, GDN_ROWS // GDN_CHUNK
    h4 = GDN_HEADS

    def kern(q_ref, k_ref, v_ref, z_ref, g_ref, b_ref, gain_ref, o_ref, st_ref, state):
        @pl.when(pl.program_id(1) == 0)
        def _():
            state[...] = jnp.zeros_like(state)

        gain_v = gain_ref[...]

        def step(ci, carry):
            sl = pl.ds(pl.multiple_of(ci * GDN_CHUNK, GDN_CHUNK), GDN_CHUNK)
            st = state[...]
            st_ref[ci] = st
            o, new = _gdn_chunk(q_ref[sl, :], k_ref[sl, :], v_ref[sl, :], z_ref[sl, :], g_ref[sl, :], b_ref[sl, :],
                                st, gain_v)
            o_ref[sl, :] = o
            state[...] = new
            return carry

        lax.fori_loop(0, cpb, step, 0)

    def col(base):
        return pl.BlockSpec((GDN_ROWS, GDN_DH), lambda h, i: (i, base + h))

    return pl.pallas_call(
        kern, name="gdn_fwd", grid=(h4, nb),
        in_specs=[col(0), col(h4), col(2 * h4), col(6 * h4), col(0), col(h4),
                  pl.BlockSpec((1, GDN_DH), lambda h, i: (0, 0))],
        out_specs=[pl.BlockSpec((GDN_ROWS, GDN_DH), lambda h, i: (i, h)),
                   pl.BlockSpec((None, cpb, GDN_DH, GDN_DH), lambda h, i: (h, i, 0, 0))],
        out_shape=[jax.ShapeDtypeStruct((s, h4 * GDN_DH), F32),
                   jax.ShapeDtypeStruct((h4, s // GDN_CHUNK, GDN_DH, GDN_DH), F32)],
        scratch_shapes=[pltpu.VMEM((GDN_DH, GDN_DH), F32)],
        compiler_params=_params(("arbitrary", "arbitrary")),
    )(cqkv, cqkv, cqkv, proj, gbb, gbb, gain)


def gdn_bwd(cqkv, proj, gbb, gain, states, d_mixed):
    s = cqkv.shape[0]
    nb, cpb = s // GDN_ROWS, GDN_ROWS // GDN_CHUNK
    h4 = GDN_HEADS

    def kern(q_ref, k_ref, v_ref, z_ref, g_ref, b_ref, gain_ref, st_ref, do_ref,
             dq_ref, dk_ref, dv_ref, dz_ref, dg_ref, db_ref, dgain_ref, dstate):
        first = jnp.logical_and(pl.program_id(0) == 0, pl.program_id(1) == 0)

        @pl.when(first)
        def _():
            dgain_ref[...] = jnp.zeros_like(dgain_ref)

        @pl.when(pl.program_id(1) == 0)
        def _():
            dstate[...] = jnp.zeros_like(dstate)

        gain_v = gain_ref[...]

        def step(t, carry):
            ci = cpb - 1 - t
            sl = pl.ds(pl.multiple_of(ci * GDN_CHUNK, GDN_CHUNK), GDN_CHUNK)
            prim = (q_ref[sl, :], k_ref[sl, :], v_ref[sl, :], z_ref[sl, :], g_ref[sl, :], b_ref[sl, :],
                    st_ref[ci], gain_v)
            _, vjp = jax.vjp(_gdn_chunk, *prim)
            dq, dk, dv, dz, dg, db, dst, dgain = vjp((do_ref[sl, :], dstate[...]))
            dq_ref[sl, :] = dq
            dk_ref[sl, :] = dk
            dv_ref[sl, :] = dv
            dz_ref[sl, :] = dz
            dg_ref[sl, :] = dg
            db_ref[sl, :] = db
            dstate[...] = dst
            dgain_ref[...] += dgain
            return carry

        lax.fori_loop(0, cpb, step, 0)

    def col(base):
        return pl.BlockSpec((GDN_ROWS, GDN_DH), lambda h, i: (nb - 1 - i, base + h))

    tiled = jax.ShapeDtypeStruct((s, h4 * GDN_DH), F32)
    return pl.pallas_call(
        kern, name="gdn_bwd", grid=(h4, nb),
        in_specs=[col(0), col(h4), col(2 * h4), col(6 * h4), col(0), col(h4),
                  pl.BlockSpec((1, GDN_DH), lambda h, i: (0, 0)),
                  pl.BlockSpec((None, cpb, GDN_DH, GDN_DH), lambda h, i: (h, nb - 1 - i, 0, 0)),
                  col(h4)],
        out_specs=[col(0)] * 6 + [pl.BlockSpec((1, GDN_DH), lambda h, i: (0, 0))],
        out_shape=[tiled] * 6 + [jax.ShapeDtypeStruct((1, GDN_DH), F32)],
        scratch_shapes=[pltpu.VMEM((GDN_DH, GDN_DH), F32)],
        compiler_params=_params(("arbitrary", "arbitrary")),
    )(cqkv, cqkv, cqkv, proj, gbb, gbb, gain, states, d_mixed)


def _gdn_gates(small, prm):
    w = GDN_HEADS * GDN_DH
    lane, head = _iota2((128, w), 0), _iota2((128, w), 1) // GDN_DH
    sel_b = (lane == SMALL_B + head).astype(F32)
    sel_a = (lane == SMALL_A + head).astype(F32)
    prow = _iota2((8, 128), 0)
    a_log = jnp.sum(prm * (prow == 0).astype(F32), axis=0, keepdims=True)
    dt_b = jnp.sum(prm * (prow == 1).astype(F32), axis=0, keepdims=True)
    beta = _sigmoid(hdot(small, sel_b))
    g = hdot(-jnp.exp(a_log) * _softplus(small + dt_b), sel_a)
    return g, beta


CONV_ROWS = 1024
CONV_COLS = 128
CONV_BLOCK0 = 1536 // CONV_COLS


def _shift_down(prev8, cur, s):
    ext = jnp.concatenate([prev8, cur], axis=0)
    return pltpu.roll(ext, s, 0)[8:]


def _shift_up(cur, next8, s):
    n = cur.shape[0]
    ext = jnp.concatenate([cur, next8], axis=0)
    return pltpu.roll(ext, n + 8 - s, 0)[:n]


def _conv_pre(x_ref, w, ci, nchunk):
    r0 = pl.multiple_of(ci * CONV_ROWS, CONV_ROWS)
    cur = x_ref[pl.ds(r0, CONV_ROWS), :]
    prev = x_ref[pl.ds(pl.multiple_of(jnp.maximum(r0 - 8, 0), 8), 8), :]
    prev = jnp.where(ci > 0, prev, 0.0)
    shifted = [cur] + [_shift_down(prev, cur, s) for s in range(1, CONV_W)]
    pre = w[CONV_W - 1:CONV_W, :] * cur
    for s in range(1, CONV_W):
        pre = pre + w[CONV_W - 1 - s:CONV_W - s, :] * shifted[s]
    return r0, pre, shifted


def conv_fwd(proj, conv_w8):
    s = proj.shape[0]
    nchunk = s // CONV_ROWS
    ncol = 3 * GDN_HEADS * GDN_DH // CONV_COLS

    def kern(x_ref, w_ref, y_ref):
        w = w_ref[...]

        def step(ci, carry):
            r0, pre, _ = _conv_pre(x_ref, w, ci, nchunk)
            y_ref[pl.ds(r0, CONV_ROWS), :] = _silu(pre)
            return carry

        lax.fori_loop(0, nchunk, step, 0)

    return pl.pallas_call(
        kern, name="conv_fwd", grid=(ncol,),
        in_specs=[pl.BlockSpec((s, CONV_COLS), lambda j: (0, CONV_BLOCK0 + j)),
                  pl.BlockSpec((8, CONV_COLS), lambda j: (0, j))],
        out_specs=pl.BlockSpec((s, CONV_COLS), lambda j: (0, j)),
        out_shape=jax.ShapeDtypeStruct((s, ncol * CONV_COLS), F32),
        compiler_params=_params(("parallel",)),
    )(proj, conv_w8)


def conv_bwd(proj, conv_w8, dy_parts):
    s = proj.shape[0]
    nchunk = s // CONV_ROWS
    per = GDN_HEADS * GDN_DH // CONV_COLS
    outs = []
    for part, dy in enumerate(dy_parts):
        def kern(x_ref, w_ref, dy_ref, dx_ref, dw_ref, dpre_ref):
            w = w_ref[...]
            rows8 = _iota2((8, CONV_COLS), 0)

            def step1(ci, dw):
                r0, pre, shifted = _conv_pre(x_ref, w, ci, nchunk)
                sg = _sigmoid(pre)
                dpre = dy_ref[pl.ds(r0, CONV_ROWS), :] * sg * (1.0 + pre * (1.0 - sg))
                dpre_ref[pl.ds(r0, CONV_ROWS), :] = dpre
                for sh in range(CONV_W):
                    dw = dw + jnp.where(rows8 == CONV_W - 1 - sh, _colsum(dpre * shifted[sh]), 0.0)
                return dw

            dw_ref[...] = lax.fori_loop(0, nchunk, step1, jnp.zeros((8, CONV_COLS), F32))

            def step2(ci, carry):
                r0 = pl.multiple_of(ci * CONV_ROWS, CONV_ROWS)
                cur = dpre_ref[pl.ds(r0, CONV_ROWS), :]
                nxt = dpre_ref[pl.ds(pl.multiple_of(jnp.minimum(r0 + CONV_ROWS, s - 8), 8), 8), :]
                nxt = jnp.where(ci < nchunk - 1, nxt, 0.0)
                dx = w[CONV_W - 1:CONV_W, :] * cur
                for sh in range(1, CONV_W):
                    dx = dx + w[CONV_W - 1 - sh:CONV_W - sh, :] * _shift_up(cur, nxt, sh)
                dx_ref[pl.ds(r0, CONV_ROWS), :] = dx
                return carry

            lax.fori_loop(0, nchunk, step2, 0)

        outs.append(pl.pallas_call(
            kern, name=f"conv_bwd{part}", grid=(per,),
            in_specs=[pl.BlockSpec((s, CONV_COLS), lambda j, part=part: (0, CONV_BLOCK0 + part * per + j)),
                      pl.BlockSpec((8, CONV_COLS), lambda j, part=part: (0, part * per + j)),
                      pl.BlockSpec((s, CONV_COLS), lambda j: (0, j))],
            out_specs=[pl.BlockSpec((s, CONV_COLS), lambda j: (0, j)),
                       pl.BlockSpec((8, CONV_COLS), lambda j: (0, j))],
            out_shape=[jax.ShapeDtypeStruct((s, per * CONV_COLS), F32),
                       jax.ShapeDtypeStruct((8, per * CONV_COLS), F32)],
            scratch_shapes=[pltpu.VMEM((s, CONV_COLS), F32)],
            compiler_params=_params(("parallel",)),
        )(proj, conv_w8, dy))
    dx = jnp.concatenate([o[0] for o in outs], axis=1)
    dw = jnp.concatenate([o[1] for o in outs], axis=1)
    return dx, dw


FOXF_ROWS = 512
SMALL_BLOCK128 = 3584 // 128


def _log_sigmoid(x):
    return jnp.minimum(x, 0.0) - jnp.log(1.0 + jnp.exp(-jnp.abs(x)))


def fox_f_fwd(proj, bias_row):
    s = proj.shape[0]
    n = s // FOXF_ROWS

    def kern(x_ref, b_ref, f_ref, carry):
        @pl.when(pl.program_id(0) == 0)
        def _():
            carry[...] = jnp.zeros_like(carry)

        heads = _iota2((FOXF_ROWS, 128), 1) < FOX_HEADS
        lf = jnp.where(heads, _log_sigmoid(x_ref[...] + b_ref[...]), 0.0)
        ltri = (_iota2((FOXF_ROWS, FOXF_ROWS), 0) >= _iota2((FOXF_ROWS, FOXF_ROWS), 1)).astype(F32)
        c = hdot(ltri, lf) + carry[...]
        f_ref[...] = c
        carry[...] = c[FOXF_ROWS - 1:FOXF_ROWS, :]

    return pl.pallas_call(
        kern, name="fox_f_fwd", grid=(n,),
        in_specs=[pl.BlockSpec((FOXF_ROWS, 128), lambda i: (i, SMALL_BLOCK128)),
                  pl.BlockSpec((1, 128), lambda i: (0, 0))],
        out_specs=pl.BlockSpec((FOXF_ROWS, 128), lambda i: (i, 0)),
        out_shape=jax.ShapeDtypeStruct((s, 128), F32),
        scratch_shapes=[pltpu.VMEM((1, 128), F32)],
        compiler_params=_params(("arbitrary",)),
    )(proj, bias_row)


def fox_f_bwd(proj, bias_row, d_f):
    s = proj.shape[0]
    n = s // FOXF_ROWS

    def kern(x_ref, b_ref, df_ref, dx_ref, db_ref, carry):
        @pl.when(pl.program_id(0) == 0)
        def _():
            carry[...] = jnp.zeros_like(carry)
            db_ref[...] = jnp.zeros_like(db_ref)

        heads = _iota2((FOXF_ROWS, 128), 1) < FOX_HEADS
        utri = (_iota2((FOXF_ROWS, FOXF_ROWS), 0) <= _iota2((FOXF_ROWS, FOXF_ROWS), 1)).astype(F32)
        rc = hdot(utri, df_ref[...]) + carry[...]
        carry[...] = rc[0:1, :]
        dx = jnp.where(heads, rc * _sigmoid(-(x_ref[...] + b_ref[...])), 0.0)
        dx_ref[...] = dx
        db_ref[...] += _colsum(dx)

    return pl.pallas_call(
        kern, name="fox_f_bwd", grid=(n,),
        in_specs=[pl.BlockSpec((FOXF_ROWS, 128), lambda i: (n - 1 - i, SMALL_BLOCK128)),
                  pl.BlockSpec((1, 128), lambda i: (0, 0)),
                  pl.BlockSpec((FOXF_ROWS, 128), lambda i: (n - 1 - i, 0))],
        out_specs=[pl.BlockSpec((FOXF_ROWS, 128), lambda i: (n - 1 - i, 0)),
                   pl.BlockSpec((1, 128), lambda i: (0, 0))],
        out_shape=[jax.ShapeDtypeStruct((s, 128), F32), jax.ShapeDtypeStruct((1, 128), F32)],
        scratch_shapes=[pltpu.VMEM((1, 128), F32)],
        compiler_params=_params(("arbitrary",)),
    )(proj, bias_row, d_f)


FOX_T = 512
FOX_SCALE = FOX_DH ** -0.5
NEG = -1e30


def fox_fwd(q, k, v, f_col, f_row):
    h, s, d = q.shape
    t = min(FOX_T, s)
    n = s // t

    def kern(q_ref, k_ref, v_ref, fc_ref, fr_ref, o_ref, lse_ref):
        i = pl.program_id(1)
        qv = q_ref[...]
        fq = fc_ref[...]
        row = i * t + _iota2((t, t), 0)

        def step(j, carry):
            m, l, acc = carry
            sl = pl.ds(pl.multiple_of(j * t, t), t)
            sc = lax.dot_general(qv, k_ref[sl, :], (((1,), (1,)), ((), ())), preferred_element_type=F32)
            sc = sc * FOX_SCALE + fq - fr_ref[j]
            sc = jnp.where(row >= j * t + _iota2((t, t), 1), sc, NEG)
            m_new = jnp.maximum(m, jnp.max(sc, axis=1, keepdims=True))
            p = jnp.exp(sc - m_new)
            alpha = jnp.exp(m - m_new)
            l = alpha * l + jnp.sum(p, axis=1, keepdims=True)
            acc = alpha * acc + jnp.dot(p.astype(BF), v_ref[sl, :], preferred_element_type=F32)
            return m_new, l, acc

        init = (jnp.full((t, 1), NEG, F32), jnp.zeros((t, 1), F32), jnp.zeros((t, d), F32))
        m, l, acc = lax.fori_loop(0, i + 1, step, init)
        o_ref[...] = acc / l
        lse_ref[...] = m + jnp.log(l)

    return pl.pallas_call(
        kern, name="fox_fwd", grid=(h, n),
        in_specs=[pl.BlockSpec((None, t, d), lambda hh, i: (hh, i, 0)),
                  pl.BlockSpec((None, s, d), lambda hh, i: (hh, 0, 0)),
                  pl.BlockSpec((None, s, d), lambda hh, i: (hh, 0, 0)),
                  pl.BlockSpec((None, t, 1), lambda hh, i: (hh, i, 0)),
                  pl.BlockSpec((None, n, 1, t), lambda hh, i: (hh, 0, 0, 0))],
        out_specs=[pl.BlockSpec((None, t, d), lambda hh, i: (hh, i, 0)),
                   pl.BlockSpec((None, t, 1), lambda hh, i: (hh, i, 0))],
        out_shape=[jax.ShapeDtypeStruct((h, s, d), F32), jax.ShapeDtypeStruct((h, s, 1), F32)],
        compiler_params=_params(("parallel", "parallel")),
    )(q, k, v, f_col, f_row)


def fox_bwd(q, k, v, do, f_col, f_row, lse_row, delta_row):
    h, s, d = q.shape
    t = min(FOX_T, s)
    n = s // t

    def kern(k_ref, v_ref, fk_ref, q_ref, do_ref, fr_ref, lse_ref, dl_ref, dq_ref, dk_ref, dv_ref, df_ref, dfq_ref):
        j = pl.program_id(1)

        @pl.when(j == 0)
        def _():
            dq_ref[...] = jnp.zeros_like(dq_ref)
            dfq_ref[...] = jnp.zeros_like(dfq_ref)

        kv, vv, fk = k_ref[...], v_ref[...], fk_ref[...]
        key = j * t + _iota2((t, t), 0)
        nt = (((1,), (1,)), ((), ()))

        def step(i, carry):
            dk, dv, df = carry
            sl = pl.ds(pl.multiple_of(i * t, t), t)
            qi, doi = q_ref[sl, :], do_ref[sl, :]
            st = lax.dot_general(kv, qi, nt, preferred_element_type=F32) * FOX_SCALE + fr_ref[i] - fk
            st = jnp.where(key <= i * t + _iota2((t, t), 1), st, NEG)
            pt = jnp.exp(st - lse_ref[i])
            dpt = lax.dot_general(vv, doi, nt, preferred_element_type=F32)
            dst = pt * (dpt - dl_ref[i])
            dsb = dst.astype(BF)
            dv = dv + jnp.dot(pt.astype(BF), doi, preferred_element_type=F32)
            dk = dk + jnp.dot(dsb, qi, preferred_element_type=F32)
            dq_ref[sl, :] += lax.dot_general(dsb, kv, (((0,), (0,)), ((), ())), preferred_element_type=F32) * FOX_SCALE
            df = df - jnp.sum(dst, axis=1, keepdims=True)
            dfq_ref[i] += jnp.sum(dst, axis=0, keepdims=True)
            return dk, dv, df

        init = (jnp.zeros((t, d), F32), jnp.zeros((t, d), F32), jnp.zeros((t, 1), F32))
        dk, dv, df = lax.fori_loop(j, n, step, init)
        dk_ref[...] = dk * FOX_SCALE
        dv_ref[...] = dv
        df_ref[...] = df

    tile = pl.BlockSpec((None, t, d), lambda hh, j: (hh, j, 0))
    whole = pl.BlockSpec((None, s, d), lambda hh, j: (hh, 0, 0))
    col = pl.BlockSpec((None, t, 1), lambda hh, j: (hh, j, 0))
    rows = pl.BlockSpec((None, n, 1, t), lambda hh, j: (hh, 0, 0, 0))
    return pl.pallas_call(
        kern, name="fox_bwd", grid=(h, n),
        in_specs=[tile, tile, col, whole, whole, rows, rows, rows],
        out_specs=[whole, tile, tile, col, rows],
        out_shape=[jax.ShapeDtypeStruct((h, s, d), F32)] * 3 + [jax.ShapeDtypeStruct((h, s, 1), F32),
                                                                 jax.ShapeDtypeStruct((h, n, 1, t), F32)],
        compiler_params=_params(("parallel", "arbitrary")),
    )(k, v, f_col, q, do, f_row, lse_row, delta_row)


def _xattn_head(q, k, v):
    sc = bdot(q, k, 1, 1) * (MEM_DH ** -0.5)
    e = jnp.exp(sc - lax.stop_gradient(jnp.max(sc, axis=-1, keepdims=True)))
    p = e / jnp.sum(e, axis=-1, keepdims=True)
    return bdot(p, v, 1, 0)


def xattn_fwd(q, kv):
    s = q.shape[0]
    hh = MEM_HEADS

    def body(*vals):
        qs, ks, vs = vals[:hh], vals[hh:2 * hh], vals[2 * hh:]
        return jnp.concatenate([_xattn_head(qs[a], ks[a], vs[a]) for a in range(hh)], axis=1)

    return rowcall(body, [(q, MEM_DH, a) for a in range(hh)],
                   [(kv, MEM_DH, a) for a in range(2 * hh)],
                   [(hh * MEM_DH, BF)], rows=512, total=s, name="xattn_fwd")[0]


def xattn_bwd(q, kv, d_o):
    s = q.shape[0]
    hh = MEM_HEADS

    def body(*vals):
        qs, dos = vals[:hh], vals[hh:2 * hh]
        ks, vs = vals[2 * hh:3 * hh], vals[3 * hh:]
        dqs, dks, dvs = [], [], []
        for a in range(hh):
            _, vjp = jax.vjp(_xattn_head, qs[a], ks[a], vs[a])
            dq, dk, dv = vjp(dos[a])
            dqs.append(dq)
            dks.append(dk)
            dvs.append(dv)
        return jnp.concatenate(dqs, axis=1), jnp.concatenate(dks + dvs, axis=1)

    return rowcall(body, [(q, MEM_DH, a) for a in range(hh)] + [(d_o, MEM_DH, a) for a in range(hh)],
                   [(kv, MEM_DH, a) for a in range(2 * hh)],
                   [(hh * MEM_DH, BF)], [kv.shape], rows=512, total=s, name="xattn_bwd")


def exchange(items, name):
    n = len(items)
    npeer = N_DEV - 1

    def body(*refs):
        ins, outs = refs[:n], refs[n:2 * n]
        send, recv, loc = refs[2 * n:]
        x, y, c = lax.axis_index("x"), lax.axis_index("y"), lax.axis_index("c")
        me = 4 * x + 2 * y + c

        def peer(p):
            px = 1 - x if p & 4 else x
            py = 1 - y if p & 2 else y
            pc = 1 - c if p & 1 else c
            return (px, py, pc), 4 * px + 2 * py + pc

        local, remote = [], []
        for w, (mode, _) in enumerate(items):
            own = ins[w] if mode == "gather" else ins[w].at[me]
            cp = pltpu.make_async_copy(own, outs[w].at[me], loc.at[w])
            cp.start()
            local.append(cp)
        for p in range(1, N_DEV):
            dev, idx = peer(p)
            for w, (mode, _) in enumerate(items):
                src = ins[w] if mode == "gather" else ins[w].at[idx]
                k = w * npeer + p - 1
                out_cp = pltpu.make_async_remote_copy(src_ref=src, dst_ref=outs[w].at[me], send_sem=send.at[k],
                                                      recv_sem=recv.at[k], device_id=dev, device_id_type=MESH)
                out_cp.start()
                in_cp = pltpu.make_async_remote_copy(src_ref=src, dst_ref=outs[w].at[idx], send_sem=send.at[k],
                                                     recv_sem=recv.at[k], device_id=dev, device_id_type=MESH)
                remote.append((out_cp, in_cp))
        for out_cp, in_cp in remote:
            in_cp.wait_recv()
            out_cp.wait_send()
        for cp in local:
            cp.wait()

    out_shape = []
    for mode, a in items:
        shp = (N_DEV,) + tuple(a.shape) if mode == "gather" else tuple(a.shape)
        out_shape.append(jax.ShapeDtypeStruct(shp, a.dtype))
    hbm = pl.BlockSpec(memory_space=pl.ANY)
    return pl.pallas_call(
        body, name=name, in_specs=[hbm] * n, out_specs=[hbm] * n, out_shape=out_shape,
        scratch_shapes=[pltpu.SemaphoreType.DMA((n * npeer,)), pltpu.SemaphoreType.DMA((n * npeer,)),
                        pltpu.SemaphoreType.DMA((n,))],
        compiler_params=pltpu.CompilerParams(has_side_effects=True),
    )(*[a for _, a in items])


def adamw(w, m, v, contribs, name):
    r, c = w.shape
    nc = len(contribs)
    rows = r // 4 if r % 32 == 0 else r
    c1, c2 = 1.0 - ADAM_B1 ** ADAM_STEP, 1.0 - ADAM_B2 ** ADAM_STEP

    def body(wv, mv, vv, *gs):
        g = gs[0]
        for extra in gs[1:]:
            g = g + extra
        m_new = ADAM_B1 * mv + (1.0 - ADAM_B1) * g
        v_new = ADAM_B2 * vv + (1.0 - ADAM_B2) * (g * g)
        delta = -ADAM_LR * ((m_new / c1) / (jnp.sqrt(v_new / c2) + ADAM_EPS) + ADAM_WD * wv)
        return g, delta, m_new, v_new

    assert nc >= 1
    return rowcall(body, [w, m, v] + list(contribs), [], [(c, F32)] * 4, rows=rows, total=r, name=name)


WEIGHTS = ['ffn1_pre_norm', 'ffn1_w_gate', 'ffn1_w_up', 'ffn1_w_down', 'ffn1_post_norm', 'mix_pre_norm', 'w_in',
           'fox_f_bias', 'gdn_conv_w', 'gdn_a_log', 'gdn_dt_bias', 'gdn_out_norm', 'w_out', 'mix_post_norm',
           'mem_pre_norm', 'mem_kv_norm', 'mem_w_q', 'mem_w_kv', 'mem_w_o', 'mem_post_norm', 'ffn2_pre_norm',
           'ffn2_w_gate', 'ffn2_w_up', 'ffn2_w_down', 'ffn2_post_norm']
GAINS = ['ffn1_pre_norm', 'ffn1_post_norm', 'mix_pre_norm', 'mix_post_norm', 'mem_pre_norm', 'mem_kv_norm',
         'mem_post_norm', 'ffn2_pre_norm', 'ffn2_post_norm']
BIG = ['ffn1_w_gate', 'ffn1_w_up', 'ffn1_w_down', 'w_in', 'w_out', 'mem_w_q', 'mem_w_kv', 'mem_w_o',
       'ffn2_w_gate', 'ffn2_w_up', 'ffn2_w_down']
PACK_ROWS = 24
ROW_MISC = len(GAINS)
ROW_CONV = ROW_MISC + 1
COL_FBIAS, COL_ALOG, COL_DTB, COL_ONORM, COL_LOSS = 0, 8, 12, 128, 256
CONV_CH = 3 * GDN_HEADS * GDN_DH


def _pad_to(a, shape):
    return jnp.pad(a, [(0, t - s) for s, t in zip(a.shape, shape)])


def _pack(get, conv=None, loss=None):
    rows = [get(nm) for nm in GAINS]
    misc = jnp.concatenate([get('fox_f_bias'), get('gdn_a_log'), get('gdn_dt_bias'),
                            jnp.zeros((1, COL_ONORM - COL_DTB - 4), F32), get('gdn_out_norm'),
                            jnp.zeros((1, 1), F32) if loss is None else loss.reshape(1, 1)], axis=1)
    rows.append(_pad_to(misc, (1, D_MODEL)))
    rows.append(jnp.zeros((6, D_MODEL), F32) if conv is None else conv.reshape(6, D_MODEL))
    return _pad_to(jnp.concatenate(rows, axis=0), (PACK_ROWS, D_MODEL))


def _unpack(p):
    out = {nm: p[i:i + 1] for i, nm in enumerate(GAINS)}
    misc = p[ROW_MISC:ROW_MISC + 1]
    out['fox_f_bias'] = misc[:, COL_FBIAS:COL_FBIAS + FOX_HEADS]
    out['gdn_a_log'] = misc[:, COL_ALOG:COL_ALOG + GDN_HEADS]
    out['gdn_dt_bias'] = misc[:, COL_DTB:COL_DTB + GDN_HEADS]
    out['gdn_out_norm'] = misc[:, COL_ONORM:COL_ONORM + GDN_DH]
    return out


def _to_heads(a):
    return a.reshape(a.shape[0], FOX_HEADS, FOX_DH).transpose(1, 0, 2)


def _from_heads(a):
    return a.transpose(1, 0, 2).reshape(a.shape[1], FOX_HEADS * FOX_DH)


def _ffn_fwd(h, pre, wgu, wd, tag):
    s = h.shape[0]
    u, = rowcall(_rms, [h], [pre], [(D_MODEL, BF)], rows=512, total=s, name=tag + "_pre")
    gu = mm(u, wgu, name=tag + "_gate_up")
    act, = rowcall(lambda a, b: _silu(a) * b, [(gu, D_FF_PAD, 0), (gu, D_FF_PAD, 1)], [], [(D_FF_PAD, BF)],
                   rows=256, total=s, name=tag + "_act")
    f = mm(act, wd, name=tag + "_down")
    return u, gu, act, f


def _half_rms(a, g):
    return 0.5 * _rms(a, g)


def _ffn_bwd(dh_out, h, pre, post, wgu, wd, saved, tag):
    u, gu, act, f = saved
    s = h.shape[0]

    def b_post(dh, fv, pg):
        return jax.vjp(_half_rms, fv, pg)[1](dh)

    df, dpost = rowcall(b_post, [dh_out, f], [post], [(D_MODEL, BF)], [(1, D_MODEL)], rows=512, total=s,
                        name=tag + "_bwd_post")
    dact = mm(df, wd, tb=True, name=tag + "_bwd_dact")
    dwd = mm(act, df, ta=True, name=tag + "_bwd_dwd")

    def b_act(a, b, da):
        dg, du = jax.vjp(lambda g_, u_: _silu(g_) * u_, a, b)[1](da)
        return jnp.concatenate([dg, du], axis=1)

    dgu, = rowcall(b_act, [(gu, D_FF_PAD, 0), (gu, D_FF_PAD, 1), dact], [], [(2 * D_FF_PAD, BF)], rows=256, total=s,
                   name=tag + "_bwd_act")
    du = mm(dgu, wgu, tb=True, name=tag + "_bwd_du")
    dwgu = mm(u, dgu, ta=True, name=tag + "_bwd_dwgu")

    def b_pre(dh, duv, hv, pg):
        dx, dpre = jax.vjp(_rms, hv, pg)[1](duv)
        return dh + dx, dpre

    dh, dpre = rowcall(b_pre, [dh_out, du, h], [pre], [(D_MODEL, F32)], [(1, D_MODEL)], rows=512, total=s,
                       name=tag + "_bwd_pre")
    return dh, dwgu, dwd, dpre, dpost


def _residual_rms(h, a, g):
    return h + _rms(a, g)


def _bwd_residual(dh, a, g):
    return jax.vjp(_rms, a, g)[1](dh)


def _step(a):
    x, mem = a['x'][0], a['mem'][0]
    s = x.shape[0]
    me = 4 * lax.axis_index("x") + 2 * lax.axis_index("y") + lax.axis_index("c")
    w2 = {nm: a[nm][0] for nm in WEIGHTS}
    m2 = {nm: a['m_' + nm][0] for nm in WEIGHTS}
    v2 = {nm: a['v_' + nm][0] for nm in WEIGHTS}
    small = {nm: w2[nm][None] for nm in WEIGHTS if nm not in BIG and nm != 'gdn_conv_w'}

    def ff_cols(w):
        return _pad_to(w, (D_MODEL, FF_SHARD_PAD)).astype(BF)

    def ff_rows(w):
        return _pad_to(w, (FF_SHARD_PAD, D_MODEL)).astype(BF)

    send = {'ffn1_w_gate': ff_cols(w2['ffn1_w_gate']), 'ffn1_w_up': ff_cols(w2['ffn1_w_up']),
            'ffn1_w_down': ff_rows(w2['ffn1_w_down']),
            'ffn2_w_gate': ff_cols(w2['ffn2_w_gate']), 'ffn2_w_up': ff_cols(w2['ffn2_w_up']),
            'ffn2_w_down': ff_rows(w2['ffn2_w_down']),
            'w_in': w2['w_in'].astype(BF), 'w_out': w2['w_out'].astype(BF), 'mem_w_q': w2['mem_w_q'].astype(BF),
            'mem_w_kv': w2['mem_w_kv'].astype(BF), 'mem_w_o': w2['mem_w_o'].astype(BF),
            'gdn_conv_w': _pad_to(w2['gdn_conv_w'], (8, CONV_CH // N_DEV))}
    names = list(send)
    got = dict(zip(names, exchange([("gather", send[nm]) for nm in names], "gather_weights")))

    def cols(g):
        return g.transpose(1, 0, 2).reshape(g.shape[1], -1)

    def rows(g):
        return g.reshape(-1, g.shape[2])

    wgu1 = jnp.concatenate([cols(got['ffn1_w_gate']), cols(got['ffn1_w_up'])], axis=1)
    wgu2 = jnp.concatenate([cols(got['ffn2_w_gate']), cols(got['ffn2_w_up'])], axis=1)
    wd1, wd2 = rows(got['ffn1_w_down']), rows(got['ffn2_w_down'])
    w_in = cols(got['w_in'])
    sp = [0, 512, 1024, 1536, 1544, 2056, 2568, 3080, 3592, 3596, 3600]
    fq, fk, fv, ff, gq, gk, gv, gz, gb, ga = [w_in[:, sp[i]:sp[i + 1]] for i in range(10)]
    w_proj = jnp.concatenate([fq, fk, fv, gq, gk, gv, gz, ff, gb, ga,
                              jnp.zeros((D_MODEL, PROJ_W - 3584 - 16), BF)], axis=1)
    w_out, w_q, w_o = rows(got['w_out']), rows(got['mem_w_q']), rows(got['mem_w_o'])
    w_kv = cols(got['mem_w_kv'])
    conv_w8 = cols(got['gdn_conv_w'])

    bias_row = _pad_to(small['fox_f_bias'], (1, 128))
    gate_prm = _pad_to(jnp.concatenate([_pad_to(small['gdn_a_log'], (1, 128 - SMALL_A)),
                                        _pad_to(small['gdn_dt_bias'], (1, 128 - SMALL_A))], axis=0),
                       (8, 128 - SMALL_A))
    gate_prm = jnp.pad(gate_prm, ((0, 0), (SMALL_A, 0)))
    onorm = small['gdn_out_norm']

    sv1 = _ffn_fwd(x, small['ffn1_pre_norm'], wgu1, wd1, "ffn1")
    h1, = rowcall(lambda h, f, g: h + _half_rms(f, g), [x, sv1[3]], [small['ffn1_post_norm']], [(D_MODEL, F32)],
                  rows=512, total=s, name="ffn1_out")

    u2, = rowcall(_rms, [h1], [small['mix_pre_norm']], [(D_MODEL, BF)], rows=512, total=s, name="mix_pre")
    proj = mm(u2, w_proj, name="mix_proj")
    nt = s // min(FOX_T, s)
    tt = min(FOX_T, s)
    f_cum = fox_f_fwd(proj, bias_row)
    f_heads = f_cum[:, :FOX_HEADS].T
    f_col, f_row = f_heads[:, :, None], f_heads.reshape(FOX_HEADS, nt, 1, tt)
    qh, kh, vh = [_to_heads(proj[:, i * 512:(i + 1) * 512]).astype(BF) for i in range(3)]
    fox_o, lse = fox_fwd(qh, kh, vh, f_col, f_row)
    fox_flat = _from_heads(fox_o)
    cqkv = conv_fwd(proj, conv_w8)
    g_l, b_l = rowcall(_gdn_gates, [(proj, 128, SMALL_BLOCK128)], [gate_prm], [(512, F32), (512, F32)],
                       rows=512, total=s, name="gdn_gates")
    gbb = jnp.concatenate([g_l, b_l], axis=1)
    gdn_o, states = gdn_fwd(cqkv, proj, gbb, onorm)
    mixed = jnp.concatenate([fox_flat, gdn_o], axis=1).astype(BF)
    mo = mm(mixed, w_out, name="mix_out")
    h2, = rowcall(_residual_rms, [h1, mo], [small['mix_post_norm']], [(D_MODEL, F32)], rows=512, total=s,
                  name="mix_res")

    hq, = rowcall(_rms, [h2], [small['mem_pre_norm']], [(D_MODEL, BF)], rows=512, total=s, name="mem_pre")
    mn, = rowcall(_rms, [mem], [small['mem_kv_norm']], [(D_MODEL, BF)], rows=256, total=mem.shape[0], name="mem_kvn")
    q_mem = mm(hq, w_q, name="mem_q")
    kv_mem = mm(mn, w_kv, name="mem_kv")
    o_mem = xattn_fwd(q_mem, kv_mem)
    c_mem = mm(o_mem, w_o, name="mem_o")
    h3, = rowcall(_residual_rms, [h2, c_mem], [small['mem_post_norm']], [(D_MODEL, F32)], rows=512, total=s,
                  name="mem_res")

    sv2 = _ffn_fwd(h3, small['ffn2_pre_norm'], wgu2, wd2, "ffn2")

    def b_loss(h, f, tgt, g):
        err = h + _half_rms(f, g) - tgt
        part = 0.5 * jnp.sum(jnp.mean(err * err, axis=-1, keepdims=True), axis=0, keepdims=True)
        return err * (1.0 / D_MODEL), jnp.broadcast_to(part, (1, 128))

    dy, loss_acc = rowcall(b_loss, [h3, sv2[3], a['loss_target'][0]], [small['ffn2_post_norm']], [(D_MODEL, F32)],
                           [(1, 128)], rows=512, total=s, name="loss")

    grads = {}
    dh3, dwgu2, dwd2, grads['ffn2_pre_norm'], grads['ffn2_post_norm'] = _ffn_bwd(
        dy, h3, small['ffn2_pre_norm'], small['ffn2_post_norm'], wgu2, wd2, sv2, "ffn2")

    dc, grads['mem_post_norm'] = rowcall(_bwd_residual, [dh3, c_mem], [small['mem_post_norm']], [(D_MODEL, BF)],
                                         [(1, D_MODEL)], rows=512, total=s, name="mem_bwd_res")
    d_o = mm(dc, w_o, tb=True, name="mem_bwd_do")
    dw_o = mm(o_mem, dc, ta=True, name="mem_bwd_dwo")
    dq_mem, dkv = xattn_bwd(q_mem, kv_mem, d_o)
    dhq = mm(dq_mem, w_q, tb=True, name="mem_bwd_dhq")
    dw_q = mm(hq, dq_mem, ta=True, name="mem_bwd_dwq")
    dmn = mm(dkv, w_kv, tb=True, name="mem_bwd_dmn")
    dw_kv = mm(mn, dkv, ta=True, name="mem_bwd_dwkv")
    _, grads['mem_kv_norm'] = rowcall(lambda d, mv, g: jax.vjp(_rms, mv, g)[1](d), [dmn, mem],
                                      [small['mem_kv_norm']], [(D_MODEL, F32)], [(1, D_MODEL)], rows=256,
                                      total=mem.shape[0], name="mem_bwd_kvn")

    def b_pre(dh, duv, hv, pg):
        dx, dpre = jax.vjp(_rms, hv, pg)[1](duv)
        return dh + dx, dpre

    dh2, grads['mem_pre_norm'] = rowcall(b_pre, [dh3, dhq, h2], [small['mem_pre_norm']], [(D_MODEL, F32)],
                                         [(1, D_MODEL)], rows=512, total=s, name="mem_bwd_pre")

    dmo, grads['mix_post_norm'] = rowcall(_bwd_residual, [dh2, mo], [small['mix_post_norm']], [(D_MODEL, BF)],
                                          [(1, D_MODEL)], rows=512, total=s, name="mix_bwd_res")
    d_mixed = mm(dmo, w_out, tb=True, name="mix_bwd_dmixed")
    dw_out = mm(mixed, dmo, ta=True, name="mix_bwd_dwout")
    def b_delta(do, o):
        sel = (_iota2((512, 128), 0) // FOX_DH == _iota2((512, 128), 1)).astype(F32)
        return hdot(do * o, sel)

    delta, = rowcall(b_delta, [(d_mixed, 512, 0), fox_flat], [], [(128, F32)], rows=512, total=s, name="fox_delta")
    delta_row = delta[:, :FOX_HEADS].T.reshape(FOX_HEADS, nt, 1, tt)
    lse_row = lse[:, :, 0].reshape(FOX_HEADS, nt, 1, tt)
    doh = _to_heads(d_mixed[:, :512]).astype(BF)
    dqh, dkh, dvh, dfk, dfq = fox_bwd(qh, kh, vh, doh, f_col, f_row, lse_row, delta_row)
    d_f = _pad_to((dfk[:, :, 0] + dfq.reshape(FOX_HEADS, s)).T, (s, 128))
    dsmall_f, dbias = fox_f_bwd(proj, bias_row, d_f)
    grads['fox_f_bias'] = dbias[:, :FOX_HEADS]
    dcq, dck, dcv, dz, dg_l, db_l, grads['gdn_out_norm'] = gdn_bwd(cqkv, proj, gbb, onorm, states, d_mixed)

    def b_gates(sm, dsf, dg, db, prm):
        dsm, dprm = jax.vjp(_gdn_gates, sm, prm)[1]((dg, db))
        return dsm + dsf, dprm

    dsmall, dprm = rowcall(b_gates, [(proj, 128, SMALL_BLOCK128), dsmall_f, dg_l, db_l], [gate_prm], [(128, F32)],
                           [(8, 128)], rows=512, total=s, name="gdn_bwd_gates")
    grads['gdn_a_log'] = dprm[0:1, SMALL_A:SMALL_A + GDN_HEADS]
    grads['gdn_dt_bias'] = dprm[1:2, SMALL_A:SMALL_A + GDN_HEADS]
    dqkv_pre, dconv8 = conv_bwd(proj, conv_w8, (dcq, dck, dcv))
    dproj = jnp.concatenate([_from_heads(dqh), _from_heads(dkh), _from_heads(dvh), dqkv_pre, dz, dsmall,
                             jnp.zeros((s, PROJ_W - 3584 - 128), F32)], axis=1).astype(BF)
    du2 = mm(dproj, w_proj, tb=True, name="mix_bwd_du")
    dw_proj = mm(u2, dproj, ta=True, name="mix_bwd_dwproj")
    dh1, grads['mix_pre_norm'] = rowcall(b_pre, [dh2, du2, h1], [small['mix_pre_norm']], [(D_MODEL, F32)],
                                         [(1, D_MODEL)], rows=512, total=s, name="mix_bwd_pre")

    grad_x, dwgu1, dwd1, grads['ffn1_pre_norm'], grads['ffn1_post_norm'] = _ffn_bwd(
        dh1, x, small['ffn1_pre_norm'], small['ffn1_post_norm'], wgu1, wd1, sv1, "ffn1")

    def ff_cols_shards(dw):
        return dw.reshape(D_MODEL, N_DEV, FF_SHARD_PAD)[:, :, :FF_SHARD].transpose(1, 0, 2)

    def ff_rows_shards(dw):
        return dw.reshape(N_DEV, FF_SHARD_PAD, D_MODEL)[:, :FF_SHARD]

    def col_shards(dw):
        return dw.reshape(dw.shape[0], N_DEV, -1).transpose(1, 0, 2)

    def row_shards(dw):
        return dw.reshape(N_DEV, -1, dw.shape[1])

    dw_in = jnp.concatenate([dw_proj[:, :1536], dw_proj[:, 3584:3592], dw_proj[:, 1536:3584],
                             dw_proj[:, 3592:3600]], axis=1)
    part = {'ffn1_w_gate': ff_cols_shards(dwgu1[:, :D_FF_PAD]), 'ffn1_w_up': ff_cols_shards(dwgu1[:, D_FF_PAD:]),
            'ffn1_w_down': ff_rows_shards(dwd1),
            'ffn2_w_gate': ff_cols_shards(dwgu2[:, :D_FF_PAD]), 'ffn2_w_up': ff_cols_shards(dwgu2[:, D_FF_PAD:]),
            'ffn2_w_down': ff_rows_shards(dwd2),
            'w_in': col_shards(dw_in), 'w_out': row_shards(dw_out), 'mem_w_q': row_shards(dw_q),
            'mem_w_kv': col_shards(dw_kv), 'mem_w_o': row_shards(dw_o)}
    gpack = _pack(lambda nm: grads[nm], conv=dconv8[:CONV_W], loss=loss_acc[:, :1])
    red = exchange([("scatter", part[nm]) for nm in BIG] + [("gather", gpack)], "reduce_grads")
    recv = dict(zip(BIG, red[:-1]))
    gsum_parts = red[-1]

    out_g, out_d, out_m, out_v = {}, {}, {}, {}
    for nm in BIG:
        r = recv[nm]
        res = adamw(w2[nm], m2[nm], v2[nm], [(r, r.shape[2], 0, d) for d in range(N_DEV)], "adamw_" + nm)
        out_g[nm], out_d[nm], out_m[nm], out_v[nm] = res
    wp = _pack(lambda nm: small[nm])
    mp = _pack(lambda nm: m2[nm][None])
    vp = _pack(lambda nm: v2[nm][None])
    pg, pd, pm, pv = adamw(wp, mp, vp, [(gsum_parts, D_MODEL, 0, d) for d in range(N_DEV)], "adamw_small")
    for dst, p in ((out_g, pg), (out_d, pd), (out_m, pm), (out_v, pv)):
        dst.update({k: val[0] for k, val in _unpack(p).items()})
    loss = pg[ROW_MISC, COL_LOSS]
    conv_g = lax.dynamic_slice_in_dim(pg[ROW_CONV:ROW_CONV + 6].reshape(CONV_W, CONV_CH), me * (CONV_CH // N_DEV),
                                      CONV_CH // N_DEV, axis=1)
    res = adamw(w2['gdn_conv_w'], m2['gdn_conv_w'], v2['gdn_conv_w'], [conv_g], "adamw_conv")
    out_g['gdn_conv_w'], out_d['gdn_conv_w'], out_m['gdn_conv_w'], out_v['gdn_conv_w'] = res

    def lead(t):
        return t[None]

    return (loss, grad_x[None], *[lead(out_g[nm]) for nm in WEIGHTS], *[lead(out_d[nm]) for nm in WEIGHTS],
            *[lead(out_m[nm]) for nm in WEIGHTS], *[lead(out_v[nm]) for nm in WEIGHTS])


def kernel(x, mem, ffn1_pre_norm, ffn1_w_gate, ffn1_w_up, ffn1_w_down, ffn1_post_norm, mix_pre_norm, w_in, fox_f_bias, gdn_conv_w, gdn_a_log, gdn_dt_bias, gdn_out_norm, w_out, mix_post_norm, mem_pre_norm, mem_kv_norm, mem_w_q, mem_w_kv, mem_w_o, mem_post_norm, ffn2_pre_norm, ffn2_w_gate, ffn2_w_up, ffn2_w_down, ffn2_post_norm, loss_target, m_ffn1_pre_norm, m_ffn1_w_gate, m_ffn1_w_up, m_ffn1_w_down, m_ffn1_post_norm, m_mix_pre_norm, m_w_in, m_fox_f_bias, m_gdn_conv_w, m_gdn_a_log, m_gdn_dt_bias, m_gdn_out_norm, m_w_out, m_mix_post_norm, m_mem_pre_norm, m_mem_kv_norm, m_mem_w_q, m_mem_w_kv, m_mem_w_o, m_mem_post_norm, m_ffn2_pre_norm, m_ffn2_w_gate, m_ffn2_w_up, m_ffn2_w_down, m_ffn2_post_norm, v_ffn1_pre_norm, v_ffn1_w_gate, v_ffn1_w_up, v_ffn1_w_down, v_ffn1_post_norm, v_mix_pre_norm, v_w_in, v_fox_f_bias, v_gdn_conv_w, v_gdn_a_log, v_gdn_dt_bias, v_gdn_out_norm, v_w_out, v_mix_post_norm, v_mem_pre_norm, v_mem_kv_norm, v_mem_w_q, v_mem_w_kv, v_mem_w_o, v_mem_post_norm, v_ffn2_pre_norm, v_ffn2_w_gate, v_ffn2_w_up, v_ffn2_w_down, v_ffn2_post_norm):
    return _step(dict(locals()))
```

```python
import functools

import jax
import jax.numpy as jnp
from jax import lax
from jax.experimental import pallas as pl
from jax.experimental.pallas import tpu as pltpu

F32 = jnp.float32
BF = jnp.bfloat16
HI = lax.Precision.HIGHEST
MESH = pl.DeviceIdType.MESH

N_DEV = 8
EPS = 1e-6
D_MODEL = 1024
D_FF = 2816
FF_SHARD = D_FF // N_DEV
FF_SHARD_PAD = 384
D_FF_PAD = FF_SHARD_PAD * N_DEV
FOX_HEADS, FOX_DH = 8, 64
GDN_HEADS, GDN_DH = 4, 128
GDN_CHUNK = 64
CONV_W = 4
MEM_HEADS, MEM_DH = 4, 256
IN_W = 3600
IN_SHARD = IN_W // N_DEV
PROJ_W = 4096
SMALL_F, SMALL_B, SMALL_A = 0, 8, 12

ADAM_LR, ADAM_B1, ADAM_B2, ADAM_EPS, ADAM_WD, ADAM_STEP = 0.001, 0.9, 0.999, 1e-08, 0.01, 10

VMEM_LIMIT = 56 * 1024 * 1024


def _params(sem=None):
    return pltpu.CompilerParams(dimension_semantics=sem, vmem_limit_bytes=VMEM_LIMIT)


def _tile(n, pref, unit=128):
    if n <= pref:
        return n
    t = (pref // unit) * unit
    while t > unit and n % t:
        t -= unit
    assert n % t == 0, (n, pref)
    return t


@functools.partial(jax.custom_vjp, nondiff_argnums=(2, 3))
def bdot(a, b, ca, cb):
    return lax.dot_general(a.astype(BF), b.astype(BF), (((ca,), (cb,)), ((), ())), preferred_element_type=F32)


def _bdot_fwd(a, b, ca, cb):
    return bdot(a, b, ca, cb), (a, b)


def _bdot_bwd(ca, cb, res, g):
    a, b = res
    da = bdot(g, b, 1, 1 - cb) if ca == 1 else bdot(b, g, 1 - cb, 1)
    db = bdot(a, g, 1 - ca, 0) if cb == 0 else bdot(g, a, 0, 1 - ca)
    return da, db


bdot.defvjp(_bdot_fwd, _bdot_bwd)


def hdot(a, b):
    return jnp.dot(a, b, precision=HI, preferred_element_type=F32)


def _iota2(shape, dim):
    return lax.broadcasted_iota(jnp.int32, shape, dim)


def _sigmoid(x):
    return 1.0 / (1.0 + jnp.exp(-x))


def _silu(x):
    return x * _sigmoid(x)


def _softplus(x):
    return jnp.maximum(x, 0.0) + jnp.log(1.0 + jnp.exp(-jnp.abs(x)))


def _rms(x, gain):
    return x * lax.rsqrt(jnp.mean(x * x, axis=-1, keepdims=True) + EPS) * gain


def mm(a, b, *, name, ta=False, tb=False, out_dtype=F32, tm=1024, tn=1024, tk=512):
    m, k = (a.shape[1], a.shape[0]) if ta else a.shape
    n = b.shape[0] if tb else b.shape[1]
    assert k == (b.shape[1] if tb else b.shape[0]), (a.shape, b.shape, ta, tb)
    tm, tn, tk = _tile(m, tm), _tile(n, tn), _tile(k, tk)
    nk = k // tk
    dims = (((0 if ta else 1,), (1 if tb else 0,)), ((), ()))

    def kern(a_ref, b_ref, o_ref, *scratch):
        def part():
            return lax.dot_general(a_ref[...].astype(BF), b_ref[...].astype(BF), dims, preferred_element_type=F32)

        if nk == 1:
            o_ref[...] = part().astype(o_ref.dtype)
            return
        acc_ref, = scratch
        kk = pl.program_id(2)

        @pl.when(kk == 0)
        def _():
            acc_ref[...] = part()

        @pl.when(kk > 0)
        def _():
            acc_ref[...] += part()

        @pl.when(kk == nk - 1)
        def _():
            o_ref[...] = acc_ref[...].astype(o_ref.dtype)

    a_spec = pl.BlockSpec((tk, tm), lambda i, j, kk: (kk, i)) if ta else pl.BlockSpec((tm, tk), lambda i, j, kk: (i, kk))
    b_spec = pl.BlockSpec((tn, tk), lambda i, j, kk: (j, kk)) if tb else pl.BlockSpec((tk, tn), lambda i, j, kk: (kk, j))
    return pl.pallas_call(
        kern, name=name, grid=(m // tm, n // tn, nk),
        in_specs=[a_spec, b_spec],
        out_specs=pl.BlockSpec((tm, tn), lambda i, j, kk: (i, j)),
        out_shape=jax.ShapeDtypeStruct((m, n), out_dtype),
        scratch_shapes=[pltpu.VMEM((tm, tn), F32)] if nk > 1 else [],
        compiler_params=_params(("parallel", "parallel", "arbitrary")),
    )(a, b)


def _row_spec(item, rows):
    if not isinstance(item, tuple):
        return item, pl.BlockSpec((rows, item.shape[1]), lambda i: (i, 0))
    if len(item) == 3:
        arr, w, c = item
        return arr, pl.BlockSpec((rows, w), lambda i: (i, c))
    arr, w, c, lead = item
    return arr, pl.BlockSpec((None, rows, w), lambda i: (lead, i, c))


def _whole_spec(item):
    if not isinstance(item, tuple):
        return item, pl.BlockSpec(item.shape, lambda i: (0,) * item.ndim)
    arr, w, c = item
    return arr, pl.BlockSpec((arr.shape[0], w), lambda i: (0, c))


def rowcall(body, tiled, whole, outs, accs=(), *, rows, total, name):
    rows = min(rows, total)
    assert total % rows == 0
    t_arr, t_spec = zip(*[_row_spec(t, rows) for t in tiled])
    w_arr, w_spec = zip(*[_whole_spec(w) for w in whole]) if whole else ((), ())
    nt, nw, no, na = len(t_arr), len(w_arr), len(outs), len(accs)

    def kern(*refs):
        vals = [r[...] for r in refs[:nt + nw]]
        res = body(*vals)
        if not isinstance(res, (tuple, list)):
            res = (res,)
        assert len(res) == no + na, (name, len(res), no, na)
        for r, v in zip(refs[nt + nw:nt + nw + no], res[:no]):
            r[...] = v.astype(r.dtype)
        if na:
            acc_refs = refs[nt + nw + no:]

            @pl.when(pl.program_id(0) == 0)
            def _():
                for r in acc_refs:
                    r[...] = jnp.zeros_like(r)

            for r, v in zip(acc_refs, res[no:]):
                r[...] += v

    out_shape = [jax.ShapeDtypeStruct((total, w), d) for w, d in outs] + [jax.ShapeDtypeStruct(s, F32) for s in accs]
    out_specs = [pl.BlockSpec((rows, w), lambda i: (i, 0)) for w, _ in outs] + \
                [pl.BlockSpec(s, lambda i: (0, 0)) for s in accs]
    res = pl.pallas_call(
        kern, name=name, grid=(total // rows,),
        in_specs=list(t_spec) + list(w_spec), out_specs=out_specs, out_shape=out_shape,
        compiler_params=_params(("arbitrary",) if na else ("parallel",)),
    )(*t_arr, *w_arr)
    return res


def _colsum(x):
    return jnp.sum(x, axis=0, keepdims=True)


def _gdn_chunk(q, k, v, z, gb, bb, state, gain):
    c = GDN_CHUNK
    nh = len(q)
    hs = range(nh)
    r64, c64 = _iota2((c, c), 0), _iota2((c, c), 1)
    incl = r64 >= c64
    strict = r64 > c64
    ltri = incl.astype(F32)
    utri = (r64 <= c64).astype(F32)
    eye = (r64 == c64).astype(F32)
    ones = jnp.ones((c, c), F32)
    pick = (_iota2((GDN_DH, c), 0) == _iota2((GDN_DH, c), 1)).astype(F32)
    last = (_iota2((c, GDN_DH), 0) == c - 1).astype(F32)

    qn = [q[h] * lax.rsqrt(jnp.sum(q[h] * q[h], axis=-1, keepdims=True) + EPS) * (GDN_DH ** -0.5) for h in hs]
    kn = [k[h] * lax.rsqrt(jnp.sum(k[h] * k[h], axis=-1, keepdims=True) + EPS) for h in hs]
    gc = [hdot(ltri, gb[h]) for h in hs]
    g64 = [hdot(gb[h], pick) for h in hs]
    gcol = [hdot(ltri, g64[h]) for h in hs]
    grow = [hdot(ones, g64[h] * utri) for h in hs]
    dec = [jnp.exp(jnp.where(incl, gcol[h] - grow[h], -1e30)) for h in hs]
    kb = [kn[h] * bb[h] for h in hs]
    vb = [v[h] * bb[h] for h in hs]
    kk = [bdot(kb[h], kn[h], 1, 1) for h in hs]
    p = [-jnp.where(strict, kk[h] * dec[h], 0.0) for h in hs]
    tinv = [eye + p[h] for h in hs]
    for _ in range(5):
        p = [hdot(p[h], p[h]) for h in hs]
        tinv = [tinv[h] + hdot(tinv[h], p[h]) for h in hs]
    egc = [jnp.exp(gc[h]) for h in hs]
    u = [hdot(tinv[h], vb[h]) for h in hs]
    w = [hdot(tinv[h], kb[h] * egc[h]) for h in hs]
    attn = [bdot(qn[h], kn[h], 1, 1) * dec[h] for h in hs]
    qd = [qn[h] * egc[h] for h in hs]
    gl = [jnp.sum(gc[h] * last, axis=0, keepdims=True) for h in hs]
    kt = [kn[h] * jnp.exp(gl[h] - gc[h]) for h in hs]
    ws = [bdot(w[h], state[h], 1, 0) for h in hs]
    qs = [bdot(qd[h], state[h], 1, 0) for h in hs]
    v_new = [u[h] - ws[h] for h in hs]
    av = [bdot(attn[h], v_new[h], 1, 0) for h in hs]
    kv = [bdot(kt[h], v_new[h], 0, 0) for h in hs]
    new_state = tuple(state[h] * jnp.exp(gl[h]) + kv[h] for h in hs)
    o = tuple(_rms(qs[h] + av[h], gain) * _silu(z[h]) for h in hs)
    return o, new_state


GDN_ROWS = 512
GDN_W = GDN_HEADS * GDN_DH


def gdn_fwd(cqkv, proj, gbb, gain):
    s = cqkv.shape[0]
    nb, cpb = s // GDN_ROWS, GDN_ROWS // GDN_CHUNK
    h4 = GDN_HEADS

    def kern(qkv_ref, z_ref, gb_ref, gain_ref, o_ref, st_ref, state):
        @pl.when(pl.program_id(0) == 0)
        def _():
            state[...] = jnp.zeros_like(state)

        gain_v = gain_ref[...]

        def step(ci, carry):
            sl = pl.ds(pl.multiple_of(ci * GDN_CHUNK, GDN_CHUNK), GDN_CHUNK)
            ins = []
            for h in range(h4):
                ln = lambda base, h=h: slice(base + h * GDN_DH, base + (h + 1) * GDN_DH)
                ins.append((qkv_ref[sl, ln(0)], qkv_ref[sl, ln(GDN_W)], qkv_ref[sl, ln(2 * GDN_W)], z_ref[sl, ln(0)],
                            gb_ref[sl, ln(0)], gb_ref[sl, ln(GDN_W)], state[h]))
            cols = [tuple(col) for col in zip(*ins)]
            o, new = _gdn_chunk(*cols[:7], gain_v)
            for h in range(h4):
                st_ref[h, ci] = ins[h][6]
                o_ref[sl, h * GDN_DH:(h + 1) * GDN_DH] = o[h]
                state[h] = new[h]
            return carry

        lax.fori_loop(0, cpb, step, 0)

    return pl.pallas_call(
        kern, name="gdn_fwd", grid=(nb,),
        in_specs=[pl.BlockSpec((GDN_ROWS, 3 * GDN_W), lambda i: (i, 0)),
                  pl.BlockSpec((GDN_ROWS, GDN_W), lambda i: (i, 6)),
                  pl.BlockSpec((GDN_ROWS, 2 * GDN_W), lambda i: (i, 0)),
                  pl.BlockSpec((1, GDN_DH), lambda i: (0, 0))],
        out_specs=[pl.BlockSpec((GDN_ROWS, GDN_W), lambda i: (i, 0)),
                   pl.BlockSpec((h4, cpb, GDN_DH, GDN_DH), lambda i: (0, i, 0, 0))],
        out_shape=[jax.ShapeDtypeStruct((s, GDN_W), F32),
                   jax.ShapeDtypeStruct((h4, s // GDN_CHUNK, GDN_DH, GDN_DH), F32)],
        scratch_shapes=[pltpu.VMEM((h4, GDN_DH, GDN_DH), F32)],
        compiler_params=_params(("arbitrary",)),
    )(cqkv, proj, gbb, gain)


def gdn_bwd(cqkv, proj, gbb, gain, states, d_mixed):
    s = cqkv.shape[0]
    nb, cpb = s // GDN_ROWS, GDN_ROWS // GDN_CHUNK
    h4 = GDN_HEADS

    def kern(qkv_ref, z_ref, gb_ref, gain_ref, st_ref, do_ref, dqkv_ref, dz_ref, dgb_ref, dgain_ref, dstate):
        @pl.when(pl.program_id(0) == 0)
        def _():
            dgain_ref[...] = jnp.zeros_like(dgain_ref)
            dstate[...] = jnp.zeros_like(dstate)

        gain_v = gain_ref[...]

        def step(t, carry):
            ci = cpb - 1 - t
            sl = pl.ds(pl.multiple_of(ci * GDN_CHUNK, GDN_CHUNK), GDN_CHUNK)
            prim, cot, dst_in = [], [], []
            for h in range(h4):
                ln = lambda base, h=h: slice(base + h * GDN_DH, base + (h + 1) * GDN_DH)
                prim.append((qkv_ref[sl, ln(0)], qkv_ref[sl, ln(GDN_W)], qkv_ref[sl, ln(2 * GDN_W)], z_ref[sl, ln(0)],
                             gb_ref[sl, ln(0)], gb_ref[sl, ln(GDN_W)], st_ref[h, ci]))
                cot.append(do_ref[sl, ln(0)])
                dst_in.append(dstate[h])
            cols = [tuple(col) for col in zip(*prim)]
            vjp = jax.vjp(_gdn_chunk, *cols, gain_v)[1]
            dq, dk, dv, dz, dg, db, dst, dgn = vjp((tuple(cot), tuple(dst_in)))
            for h in range(h4):
                ln = lambda base, h=h: slice(base + h * GDN_DH, base + (h + 1) * GDN_DH)
                dqkv_ref[sl, ln(0)] = dq[h]
                dqkv_ref[sl, ln(GDN_W)] = dk[h]
                dqkv_ref[sl, ln(2 * GDN_W)] = dv[h]
                dz_ref[sl, ln(0)] = dz[h]
                dgb_ref[sl, ln(0)] = dg[h]
                dgb_ref[sl, ln(GDN_W)] = db[h]
                dstate[h] = dst[h]
            dgain_ref[...] += dgn
            return carry

        lax.fori_loop(0, cpb, step, 0)

    def rev(width, cblock=0):
        return pl.BlockSpec((GDN_ROWS, width), lambda i: (nb - 1 - i, cblock))

    return pl.pallas_call(
        kern, name="gdn_bwd", grid=(nb,),
        in_specs=[rev(3 * GDN_W), rev(GDN_W, 6), rev(2 * GDN_W), pl.BlockSpec((1, GDN_DH), lambda i: (0, 0)),
                  pl.BlockSpec((h4, cpb, GDN_DH, GDN_DH), lambda i: (0, nb - 1 - i, 0, 0)), rev(GDN_W, 1)],
        out_specs=[rev(3 * GDN_W), rev(GDN_W), rev(2 * GDN_W), pl.BlockSpec((1, GDN_DH), lambda i: (0, 0))],
        out_shape=[jax.ShapeDtypeStruct((s, 3 * GDN_W), F32), jax.ShapeDtypeStruct((s, GDN_W), F32),
                   jax.ShapeDtypeStruct((s, 2 * GDN_W), F32), jax.ShapeDtypeStruct((1, GDN_DH), F32)],
        scratch_shapes=[pltpu.VMEM((h4, GDN_DH, GDN_DH), F32)],
        compiler_params=_params(("arbitrary",)),
    )(cqkv, proj, gbb, gain, states, d_mixed)


def _gdn_gates(small, prm):
    w = GDN_HEADS * GDN_DH
    lane, head = _iota2((128, w), 0), _iota2((128, w), 1) // GDN_DH
    sel_b = (lane == SMALL_B + head).astype(F32)
    sel_a = (lane == SMALL_A + head).astype(F32)
    prow = _iota2((8, 128), 0)
    a_log = jnp.sum(prm * (prow == 0).astype(F32), axis=0, keepdims=True)
    dt_b = jnp.sum(prm * (prow == 1).astype(F32), axis=0, keepdims=True)
    beta = _sigmoid(hdot(small, sel_b))
    g = hdot(-jnp.exp(a_log) * _softplus(small + dt_b), sel_a)
    return g, beta


CONV_ROWS = 1024
CONV_COLS = 128
CONV_BLOCK0 = 1536 // CONV_COLS


def _shift_down(prev8, cur, s):
    ext = jnp.concatenate([prev8, cur], axis=0)
    return pltpu.roll(ext, s, 0)[8:]


def _shift_up(cur, next8, s):
    n = cur.shape[0]
    ext = jnp.concatenate([cur, next8], axis=0)
    return pltpu.roll(ext, n + 8 - s, 0)[:n]


def _conv_pre(x_ref, w, ci, nchunk):
    r0 = pl.multiple_of(ci * CONV_ROWS, CONV_ROWS)
    cur = x_ref[pl.ds(r0, CONV_ROWS), :]
    prev = x_ref[pl.ds(pl.multiple_of(jnp.maximum(r0 - 8, 0), 8), 8), :]
    prev = jnp.where(ci > 0, prev, 0.0)
    shifted = [cur] + [_shift_down(prev, cur, s) for s in range(1, CONV_W)]
    pre = w[CONV_W - 1:CONV_W, :] * cur
    for s in range(1, CONV_W):
        pre = pre + w[CONV_W - 1 - s:CONV_W - s, :] * shifted[s]
    return r0, pre, shifted


def conv_fwd(proj, conv_w8):
    s = proj.shape[0]
    nchunk = s // CONV_ROWS
    ncol = 3 * GDN_HEADS * GDN_DH // CONV_COLS

    def kern(x_ref, w_ref, y_ref):
        w = w_ref[...]

        def step(ci, carry):
            r0, pre, _ = _conv_pre(x_ref, w, ci, nchunk)
            y_ref[pl.ds(r0, CONV_ROWS), :] = _silu(pre)
            return carry

        lax.fori_loop(0, nchunk, step, 0)

    return pl.pallas_call(
        kern, name="conv_fwd", grid=(ncol,),
        in_specs=[pl.BlockSpec((s, CONV_COLS), lambda j: (0, CONV_BLOCK0 + j)),
                  pl.BlockSpec((8, CONV_COLS), lambda j: (0, j))],
        out_specs=pl.BlockSpec((s, CONV_COLS), lambda j: (0, j)),
        out_shape=jax.ShapeDtypeStruct((s, ncol * CONV_COLS), F32),
        compiler_params=_params(("parallel",)),
    )(proj, conv_w8)


def conv_bwd(proj, conv_w8, dy):
    s = proj.shape[0]
    nchunk = s // CONV_ROWS
    per = 3 * GDN_HEADS * GDN_DH // CONV_COLS
    outs = []
    for part in range(1):
        def kern(x_ref, w_ref, dy_ref, dx_ref, dw_ref, dpre_ref):
            w = w_ref[...]
            rows8 = _iota2((8, CONV_COLS), 0)

            def step1(ci, dw):
                r0, pre, shifted = _conv_pre(x_ref, w, ci, nchunk)
                sg = _sigmoid(pre)
                dpre = dy_ref[pl.ds(r0, CONV_ROWS), :] * sg * (1.0 + pre * (1.0 - sg))
                dpre_ref[pl.ds(r0, CONV_ROWS), :] = dpre
                for sh in range(CONV_W):
                    dw = dw + jnp.where(rows8 == CONV_W - 1 - sh, _colsum(dpre * shifted[sh]), 0.0)
                return dw

            dw_ref[...] = lax.fori_loop(0, nchunk, step1, jnp.zeros((8, CONV_COLS), F32))

            def step2(ci, carry):
                r0 = pl.multiple_of(ci * CONV_ROWS, CONV_ROWS)
                cur = dpre_ref[pl.ds(r0, CONV_ROWS), :]
                nxt = dpre_ref[pl.ds(pl.multiple_of(jnp.minimum(r0 + CONV_ROWS, s - 8), 8), 8), :]
                nxt = jnp.where(ci < nchunk - 1, nxt, 0.0)
                dx = w[CONV_W - 1:CONV_W, :] * cur
                for sh in range(1, CONV_W):
                    dx = dx + w[CONV_W - 1 - sh:CONV_W - sh, :] * _shift_up(cur, nxt, sh)
                dx_ref[pl.ds(r0, CONV_ROWS), :] = dx
                return carry

            lax.fori_loop(0, nchunk, step2, 0)

        outs.append(pl.pallas_call(
            kern, name=f"conv_bwd{part}", grid=(per,),
            in_specs=[pl.BlockSpec((s, CONV_COLS), lambda j, part=part: (0, CONV_BLOCK0 + part * per + j)),
                      pl.BlockSpec((8, CONV_COLS), lambda j, part=part: (0, part * per + j)),
                      pl.BlockSpec((s, CONV_COLS), lambda j: (0, j))],
            out_specs=[pl.BlockSpec((s, CONV_COLS), lambda j: (0, j)),
                       pl.BlockSpec((8, CONV_COLS), lambda j: (0, j))],
            out_shape=[jax.ShapeDtypeStruct((s, per * CONV_COLS), F32),
                       jax.ShapeDtypeStruct((8, per * CONV_COLS), F32)],
            scratch_shapes=[pltpu.VMEM((s, CONV_COLS), F32)],
            compiler_params=_params(("parallel",)),
        )(proj, conv_w8, dy))
    dx = jnp.concatenate([o[0] for o in outs], axis=1)
    dw = jnp.concatenate([o[1] for o in outs], axis=1)
    return dx, dw


FOXF_ROWS = 512
SMALL_BLOCK128 = 3584 // 128


def _log_sigmoid(x):
    return jnp.minimum(x, 0.0) - jnp.log(1.0 + jnp.exp(-jnp.abs(x)))


def fox_f_fwd(proj, bias_row):
    s = proj.shape[0]
    n = s // FOXF_ROWS

    def kern(x_ref, b_ref, f_ref, carry):
        @pl.when(pl.program_id(0) == 0)
        def _():
            carry[...] = jnp.zeros_like(carry)

        heads = _iota2((FOXF_ROWS, 128), 1) < FOX_HEADS
        lf = jnp.where(heads, _log_sigmoid(x_ref[...] + b_ref[...]), 0.0)
        ltri = (_iota2((FOXF_ROWS, FOXF_ROWS), 0) >= _iota2((FOXF_ROWS, FOXF_ROWS), 1)).astype(F32)
        c = hdot(ltri, lf) + carry[...]
        f_ref[...] = c
        carry[...] = c[FOXF_ROWS - 1:FOXF_ROWS, :]

    return pl.pallas_call(
        kern, name="fox_f_fwd", grid=(n,),
        in_specs=[pl.BlockSpec((FOXF_ROWS, 128), lambda i: (i, SMALL_BLOCK128)),
                  pl.BlockSpec((1, 128), lambda i: (0, 0))],
        out_specs=pl.BlockSpec((FOXF_ROWS, 128), lambda i: (i, 0)),
        out_shape=jax.ShapeDtypeStruct((s, 128), F32),
        scratch_shapes=[pltpu.VMEM((1, 128), F32)],
        compiler_params=_params(("arbitrary",)),
    )(proj, bias_row)


def fox_f_bwd(proj, bias_row, d_f):
    s = proj.shape[0]
    n = s // FOXF_ROWS

    def kern(x_ref, b_ref, df_ref, dx_ref, db_ref, carry):
        @pl.when(pl.program_id(0) == 0)
        def _():
            carry[...] = jnp.zeros_like(carry)
            db_ref[...] = jnp.zeros_like(db_ref)

        heads = _iota2((FOXF_ROWS, 128), 1) < FOX_HEADS
        utri = (_iota2((FOXF_ROWS, FOXF_ROWS), 0) <= _iota2((FOXF_ROWS, FOXF_ROWS), 1)).astype(F32)
        rc = hdot(utri, df_ref[...]) + carry[...]
        carry[...] = rc[0:1, :]
        dx = jnp.where(heads, rc * _sigmoid(-(x_ref[...] + b_ref[...])), 0.0)
        dx_ref[...] = dx
        db_ref[...] += _colsum(dx)

    return pl.pallas_call(
        kern, name="fox_f_bwd", grid=(n,),
        in_specs=[pl.BlockSpec((FOXF_ROWS, 128), lambda i: (n - 1 - i, SMALL_BLOCK128)),
                  pl.BlockSpec((1, 128), lambda i: (0, 0)),
                  pl.BlockSpec((FOXF_ROWS, 128), lambda i: (n - 1 - i, 0))],
        out_specs=[pl.BlockSpec((FOXF_ROWS, 128), lambda i: (n - 1 - i, 0)),
                   pl.BlockSpec((1, 128), lambda i: (0, 0))],
        out_shape=[jax.ShapeDtypeStruct((s, 128), F32), jax.ShapeDtypeStruct((1, 128), F32)],
        scratch_shapes=[pltpu.VMEM((1, 128), F32)],
        compiler_params=_params(("arbitrary",)),
    )(proj, bias_row, d_f)


FOX_T = 512
FOX_SCALE = FOX_DH ** -0.5
NEG = -1e30


def fox_fwd(q, k, v, f_col, f_row):
    h, s, d = q.shape
    t = min(FOX_T, s)
    n = s // t

    def kern(q_ref, k_ref, v_ref, fc_ref, fr_ref, o_ref, lse_ref):
        i = pl.program_id(1)
        qv = q_ref[...]
        fq = fc_ref[...]
        row = i * t + _iota2((t, t), 0)

        def step(j, carry):
            m, l, acc = carry
            sl = pl.ds(pl.multiple_of(j * t, t), t)
            sc = lax.dot_general(qv, k_ref[sl, :], (((1,), (1,)), ((), ())), preferred_element_type=F32)
            sc = sc * FOX_SCALE + fq - fr_ref[j]
            sc = jnp.where(row >= j * t + _iota2((t, t), 1), sc, NEG)
            m_new = jnp.maximum(m, jnp.max(sc, axis=1, keepdims=True))
            p = jnp.exp(sc - m_new)
            alpha = jnp.exp(m - m_new)
            l = alpha * l + jnp.sum(p, axis=1, keepdims=True)
            acc = alpha * acc + jnp.dot(p.astype(BF), v_ref[sl, :], preferred_element_type=F32)
            return m_new, l, acc

        init = (jnp.full((t, 1), NEG, F32), jnp.zeros((t, 1), F32), jnp.zeros((t, d), F32))
        m, l, acc = lax.fori_loop(0, i + 1, step, init)
        o_ref[...] = acc / l
        lse_ref[...] = m + jnp.log(l)

    return pl.pallas_call(
        kern, name="fox_fwd", grid=(h, n),
        in_specs=[pl.BlockSpec((None, t, d), lambda hh, i: (hh, i, 0)),
                  pl.BlockSpec((None, s, d), lambda hh, i: (hh, 0, 0)),
                  pl.BlockSpec((None, s, d), lambda hh, i: (hh, 0, 0)),
                  pl.BlockSpec((None, t, 1), lambda hh, i: (hh, i, 0)),
                  pl.BlockSpec((None, n, 1, t), lambda hh, i: (hh, 0, 0, 0))],
        out_specs=[pl.BlockSpec((None, t, d), lambda hh, i: (hh, i, 0)),
                   pl.BlockSpec((None, t, 1), lambda hh, i: (hh, i, 0))],
        out_shape=[jax.ShapeDtypeStruct((h, s, d), F32), jax.ShapeDtypeStruct((h, s, 1), F32)],
        compiler_params=_params(("parallel", "parallel")),
    )(q, k, v, f_col, f_row)


def fox_bwd(q, k, v, do, f_col, f_row, lse_row, delta_row):
    h, s, d = q.shape
    t = min(FOX_T, s)
    n = s // t

    def kern(k_ref, v_ref, fk_ref, q_ref, do_ref, fr_ref, lse_ref, dl_ref, dq_ref, dk_ref, dv_ref, df_ref, dfq_ref):
        j = pl.program_id(1)

        @pl.when(j == 0)
        def _():
            dq_ref[...] = jnp.zeros_like(dq_ref)
            dfq_ref[...] = jnp.zeros_like(dfq_ref)

        kv, vv, fk = k_ref[...], v_ref[...], fk_ref[...]
        key = j * t + _iota2((t, t), 0)
        nt = (((1,), (1,)), ((), ()))

        def step(i, carry):
            dk, dv, df = carry
            sl = pl.ds(pl.multiple_of(i * t, t), t)
            qi, doi = q_ref[sl, :], do_ref[sl, :]
            st = lax.dot_general(kv, qi, nt, preferred_element_type=F32) * FOX_SCALE + fr_ref[i] - fk
            st = jnp.where(key <= i * t + _iota2((t, t), 1), st, NEG)
            pt = jnp.exp(st - lse_ref[i])
            dpt = lax.dot_general(vv, doi, nt, preferred_element_type=F32)
            dst = pt * (dpt - dl_ref[i])
            dsb = dst.astype(BF)
            dv = dv + jnp.dot(pt.astype(BF), doi, preferred_element_type=F32)
            dk = dk + jnp.dot(dsb, qi, preferred_element_type=F32)
            dq_ref[sl, :] += lax.dot_general(dsb, kv, (((0,), (0,)), ((), ())), preferred_element_type=F32) * FOX_SCALE
            df = df - jnp.sum(dst, axis=1, keepdims=True)
            dfq_ref[i] += jnp.sum(dst, axis=0, keepdims=True)
            return dk, dv, df

        init = (jnp.zeros((t, d), F32), jnp.zeros((t, d), F32), jnp.zeros((t, 1), F32))
        dk, dv, df = lax.fori_loop(j, n, step, init)
        dk_ref[...] = dk * FOX_SCALE
        dv_ref[...] = dv
        df_ref[...] = df

    tile = pl.BlockSpec((None, t, d), lambda hh, j: (hh, j, 0))
    whole = pl.BlockSpec((None, s, d), lambda hh, j: (hh, 0, 0))
    col = pl.BlockSpec((None, t, 1), lambda hh, j: (hh, j, 0))
    rows = pl.BlockSpec((None, n, 1, t), lambda hh, j: (hh, 0, 0, 0))
    return pl.pallas_call(
        kern, name="fox_bwd", grid=(h, n),
        in_specs=[tile, tile, col, whole, whole, rows, rows, rows],
        out_specs=[whole, tile, tile, col, rows],
        out_shape=[jax.ShapeDtypeStruct((h, s, d), F32)] * 3 + [jax.ShapeDtypeStruct((h, s, 1), F32),
                                                                 jax.ShapeDtypeStruct((h, n, 1, t), F32)],
        compiler_params=_params(("parallel", "arbitrary")),
    )(k, v, f_col, q, do, f_row, lse_row, delta_row)


def _xattn_head(q, k, v):
    sc = bdot(q, k, 1, 1) * (MEM_DH ** -0.5)
    e = jnp.exp(sc - lax.stop_gradient(jnp.max(sc, axis=-1, keepdims=True)))
    p = e / jnp.sum(e, axis=-1, keepdims=True)
    return bdot(p, v, 1, 0)


def xattn_fwd(q, kv):
    s = q.shape[0]
    hh = MEM_HEADS

    def body(*vals):
        qs, ks, vs = vals[:hh], vals[hh:2 * hh], vals[2 * hh:]
        return jnp.concatenate([_xattn_head(qs[a], ks[a], vs[a]) for a in range(hh)], axis=1)

    return rowcall(body, [(q, MEM_DH, a) for a in range(hh)],
                   [(kv, MEM_DH, a) for a in range(2 * hh)],
                   [(hh * MEM_DH, BF)], rows=512, total=s, name="xattn_fwd")[0]


def xattn_bwd(q, kv, d_o):
    s = q.shape[0]
    hh = MEM_HEADS

    def body(*vals):
        qs, dos = vals[:hh], vals[hh:2 * hh]
        ks, vs = vals[2 * hh:3 * hh], vals[3 * hh:]
        dqs, dks, dvs = [], [], []
        for a in range(hh):
            _, vjp = jax.vjp(_xattn_head, qs[a], ks[a], vs[a])
            dq, dk, dv = vjp(dos[a])
            dqs.append(dq)
            dks.append(dk)
            dvs.append(dv)
        return jnp.concatenate(dqs, axis=1), jnp.concatenate(dks + dvs, axis=1)

    return rowcall(body, [(q, MEM_DH, a) for a in range(hh)] + [(d_o, MEM_DH, a) for a in range(hh)],
                   [(kv, MEM_DH, a) for a in range(2 * hh)],
                   [(hh * MEM_DH, BF)], [kv.shape], rows=512, total=s, name="xattn_bwd")


def exchange(items, name):
    n = len(items)
    npeer = N_DEV - 1

    def body(*refs):
        ins, outs = refs[:n], refs[n:2 * n]
        send, recv, loc = refs[2 * n:]
        x, y, c = lax.axis_index("x"), lax.axis_index("y"), lax.axis_index("c")
        me = 4 * x + 2 * y + c

        def peer(p):
            px = 1 - x if p & 4 else x
            py = 1 - y if p & 2 else y
            pc = 1 - c if p & 1 else c
            return (px, py, pc), 4 * px + 2 * py + pc

        local, remote = [], []
        for w, (mode, _) in enumerate(items):
            own = ins[w] if mode == "gather" else ins[w].at[me]
            cp = pltpu.make_async_copy(own, outs[w].at[me], loc.at[w])
            cp.start()
            local.append(cp)
        for p in range(1, N_DEV):
            dev, idx = peer(p)
            for w, (mode, _) in enumerate(items):
                src = ins[w] if mode == "gather" else ins[w].at[idx]
                k = w * npeer + p - 1
                out_cp = pltpu.make_async_remote_copy(src_ref=src, dst_ref=outs[w].at[me], send_sem=send.at[k],
                                                      recv_sem=recv.at[k], device_id=dev, device_id_type=MESH)
                out_cp.start()
                in_cp = pltpu.make_async_remote_copy(src_ref=src, dst_ref=outs[w].at[idx], send_sem=send.at[k],
                                                     recv_sem=recv.at[k], device_id=dev, device_id_type=MESH)
                remote.append((out_cp, in_cp))
        for out_cp, in_cp in remote:
            in_cp.wait_recv()
            out_cp.wait_send()
        for cp in local:
            cp.wait()

    out_shape = []
    for mode, a in items:
        shp = (N_DEV,) + tuple(a.shape) if mode == "gather" else tuple(a.shape)
        out_shape.append(jax.ShapeDtypeStruct(shp, a.dtype))
    hbm = pl.BlockSpec(memory_space=pl.ANY)
    return pl.pallas_call(
        body, name=name, in_specs=[hbm] * n, out_specs=[hbm] * n, out_shape=out_shape,
        scratch_shapes=[pltpu.SemaphoreType.DMA((n * npeer,)), pltpu.SemaphoreType.DMA((n * npeer,)),
                        pltpu.SemaphoreType.DMA((n,))],
        compiler_params=pltpu.CompilerParams(has_side_effects=True),
    )(*[a for _, a in items])


def adamw(w, m, v, contribs, name):
    r, c = w.shape
    nc = len(contribs)
    rows = next((r // d for d in (4, 2) if r % d == 0 and (r // d) % 16 == 0), r)
    c1, c2 = 1.0 - ADAM_B1 ** ADAM_STEP, 1.0 - ADAM_B2 ** ADAM_STEP

    def body(wv, mv, vv, *gs):
        g = gs[0].astype(F32)
        for extra in gs[1:]:
            g = g + extra.astype(F32)
        m_new = ADAM_B1 * mv + (1.0 - ADAM_B1) * g
        v_new = ADAM_B2 * vv + (1.0 - ADAM_B2) * (g * g)
        delta = -ADAM_LR * ((m_new / c1) / (jnp.sqrt(v_new / c2) + ADAM_EPS) + ADAM_WD * wv)
        return g, delta, m_new, v_new

    assert nc >= 1
    return rowcall(body, [w, m, v] + list(contribs), [], [(c, F32)] * 4, rows=rows, total=r, name=name)


WEIGHTS = ['ffn1_pre_norm', 'ffn1_w_gate', 'ffn1_w_up', 'ffn1_w_down', 'ffn1_post_norm', 'mix_pre_norm', 'w_in',
           'fox_f_bias', 'gdn_conv_w', 'gdn_a_log', 'gdn_dt_bias', 'gdn_out_norm', 'w_out', 'mix_post_norm',
           'mem_pre_norm', 'mem_kv_norm', 'mem_w_q', 'mem_w_kv', 'mem_w_o', 'mem_post_norm', 'ffn2_pre_norm',
           'ffn2_w_gate', 'ffn2_w_up', 'ffn2_w_down', 'ffn2_post_norm']
GAINS = ['ffn1_pre_norm', 'ffn1_post_norm', 'mix_pre_norm', 'mix_post_norm', 'mem_pre_norm', 'mem_kv_norm',
         'mem_post_norm', 'ffn2_pre_norm', 'ffn2_post_norm']
BIG = ['ffn1_w_gate', 'ffn1_w_up', 'ffn1_w_down', 'w_in', 'w_out', 'mem_w_q', 'mem_w_kv', 'mem_w_o',
       'ffn2_w_gate', 'ffn2_w_up', 'ffn2_w_down']
PACK_ROWS = 24
ROW_MISC = len(GAINS)
ROW_CONV = ROW_MISC + 1
COL_FBIAS, COL_ALOG, COL_DTB, COL_ONORM, COL_LOSS = 0, 8, 12, 128, 256
CONV_CH = 3 * GDN_HEADS * GDN_DH


def _pad_to(a, shape):
    return jnp.pad(a, [(0, t - s) for s, t in zip(a.shape, shape)])


def _pack(get, conv=None, loss=None):
    rows = [get(nm) for nm in GAINS]
    misc = jnp.concatenate([get('fox_f_bias'), get('gdn_a_log'), get('gdn_dt_bias'),
                            jnp.zeros((1, COL_ONORM - COL_DTB - 4), F32), get('gdn_out_norm'),
                            jnp.zeros((1, 1), F32) if loss is None else loss.reshape(1, 1)], axis=1)
    rows.append(_pad_to(misc, (1, D_MODEL)))
    rows.append(jnp.zeros((6, D_MODEL), F32) if conv is None else conv.reshape(6, D_MODEL))
    return _pad_to(jnp.concatenate(rows, axis=0), (PACK_ROWS, D_MODEL))


def _unpack(p):
    out = {nm: p[i:i + 1] for i, nm in enumerate(GAINS)}
    misc = p[ROW_MISC:ROW_MISC + 1]
    out['fox_f_bias'] = misc[:, COL_FBIAS:COL_FBIAS + FOX_HEADS]
    out['gdn_a_log'] = misc[:, COL_ALOG:COL_ALOG + GDN_HEADS]
    out['gdn_dt_bias'] = misc[:, COL_DTB:COL_DTB + GDN_HEADS]
    out['gdn_out_norm'] = misc[:, COL_ONORM:COL_ONORM + GDN_DH]
    return out


def _to_heads(a):
    return a.reshape(a.shape[0], FOX_HEADS, FOX_DH).transpose(1, 0, 2)


def _from_heads(a):
    return a.transpose(1, 0, 2).reshape(a.shape[1], FOX_HEADS * FOX_DH)


def _ffn_fwd(h, pre, wgu, wd, tag):
    s = h.shape[0]
    u, = rowcall(_rms, [h], [pre], [(D_MODEL, BF)], rows=512, total=s, name=tag + "_pre")
    gu = mm(u, wgu, name=tag + "_gate_up")
    act, = rowcall(lambda a, b: _silu(a) * b, [(gu, D_FF_PAD, 0), (gu, D_FF_PAD, 1)], [], [(D_FF_PAD, BF)],
                   rows=256, total=s, name=tag + "_act")
    f = mm(act, wd, name=tag + "_down")
    return u, gu, act, f


def _half_rms(a, g):
    return 0.5 * _rms(a, g)


def _ffn_bwd(dh_out, h, pre, post, wgu, wd, saved, tag):
    u, gu, act, f = saved
    s = h.shape[0]

    def b_post(dh, fv, pg):
        return jax.vjp(_half_rms, fv, pg)[1](dh)

    df, dpost = rowcall(b_post, [dh_out, f], [post], [(D_MODEL, BF)], [(1, D_MODEL)], rows=512, total=s,
                        name=tag + "_bwd_post")
    dact = mm(df, wd, tb=True, name=tag + "_bwd_dact")
    dwd = mm(act, df, ta=True, name=tag + "_bwd_dwd")

    def b_act(a, b, da):
        dg, du = jax.vjp(lambda g_, u_: _silu(g_) * u_, a, b)[1](da)
        return jnp.concatenate([dg, du], axis=1)

    dgu, = rowcall(b_act, [(gu, D_FF_PAD, 0), (gu, D_FF_PAD, 1), dact], [], [(2 * D_FF_PAD, BF)], rows=256, total=s,
                   name=tag + "_bwd_act")
    du = mm(dgu, wgu, tb=True, name=tag + "_bwd_du")
    dwgu = mm(u, dgu, ta=True, name=tag + "_bwd_dwgu")

    def b_pre(dh, duv, hv, pg):
        dx, dpre = jax.vjp(_rms, hv, pg)[1](duv)
        return dh + dx, dpre

    dh, dpre = rowcall(b_pre, [dh_out, du, h], [pre], [(D_MODEL, F32)], [(1, D_MODEL)], rows=512, total=s,
                       name=tag + "_bwd_pre")
    return dh, dwgu, dwd, dpre, dpost


def _residual_rms(h, a, g):
    return h + _rms(a, g)


def _bwd_residual(dh, a, g):
    return jax.vjp(_rms, a, g)[1](dh)


def _step(a):
    x, mem = a['x'][0], a['mem'][0]
    s = x.shape[0]
    me = 4 * lax.axis_index("x") + 2 * lax.axis_index("y") + lax.axis_index("c")
    w2 = {nm: a[nm][0] for nm in WEIGHTS}
    m2 = {nm: a['m_' + nm][0] for nm in WEIGHTS}
    v2 = {nm: a['v_' + nm][0] for nm in WEIGHTS}
    small = {nm: w2[nm][None] for nm in WEIGHTS if nm not in BIG and nm != 'gdn_conv_w'}

    def ff_cols(w):
        return _pad_to(w, (D_MODEL, FF_SHARD_PAD)).astype(BF)

    def ff_rows(w):
        return _pad_to(w, (FF_SHARD_PAD, D_MODEL)).astype(BF)

    send = {'ffn1_w_gate': ff_cols(w2['ffn1_w_gate']), 'ffn1_w_up': ff_cols(w2['ffn1_w_up']),
            'ffn1_w_down': ff_rows(w2['ffn1_w_down']),
            'ffn2_w_gate': ff_cols(w2['ffn2_w_gate']), 'ffn2_w_up': ff_cols(w2['ffn2_w_up']),
            'ffn2_w_down': ff_rows(w2['ffn2_w_down']),
            'w_in': w2['w_in'].astype(BF), 'w_out': w2['w_out'].astype(BF), 'mem_w_q': w2['mem_w_q'].astype(BF),
            'mem_w_kv': w2['mem_w_kv'].astype(BF), 'mem_w_o': w2['mem_w_o'].astype(BF),
            'gdn_conv_w': _pad_to(w2['gdn_conv_w'], (8, CONV_CH // N_DEV))}
    names = list(send)
    got = dict(zip(names, exchange([("gather", send[nm]) for nm in names], "gather_weights")))

    def cols(g):
        return g.transpose(1, 0, 2).reshape(g.shape[1], -1)

    def rows(g):
        return g.reshape(-1, g.shape[2])

    wgu1 = jnp.concatenate([cols(got['ffn1_w_gate']), cols(got['ffn1_w_up'])], axis=1)
    wgu2 = jnp.concatenate([cols(got['ffn2_w_gate']), cols(got['ffn2_w_up'])], axis=1)
    wd1, wd2 = rows(got['ffn1_w_down']), rows(got['ffn2_w_down'])
    w_in = cols(got['w_in'])
    sp = [0, 512, 1024, 1536, 1544, 2056, 2568, 3080, 3592, 3596, 3600]
    fq, fk, fv, ff, gq, gk, gv, gz, gb, ga = [w_in[:, sp[i]:sp[i + 1]] for i in range(10)]
    w_proj = jnp.concatenate([fq, fk, fv, gq, gk, gv, gz, ff, gb, ga,
                              jnp.zeros((D_MODEL, PROJ_W - 3584 - 16), BF)], axis=1)
    w_out, w_q, w_o = rows(got['w_out']), rows(got['mem_w_q']), rows(got['mem_w_o'])
    w_kv = cols(got['mem_w_kv'])
    conv_w8 = cols(got['gdn_conv_w'])

    bias_row = _pad_to(small['fox_f_bias'], (1, 128))
    gate_prm = _pad_to(jnp.concatenate([_pad_to(small['gdn_a_log'], (1, 128 - SMALL_A)),
                                        _pad_to(small['gdn_dt_bias'], (1, 128 - SMALL_A))], axis=0),
                       (8, 128 - SMALL_A))
    gate_prm = jnp.pad(gate_prm, ((0, 0), (SMALL_A, 0)))
    onorm = small['gdn_out_norm']

    sv1 = _ffn_fwd(x, small['ffn1_pre_norm'], wgu1, wd1, "ffn1")
    h1, = rowcall(lambda h, f, g: h + _half_rms(f, g), [x, sv1[3]], [small['ffn1_post_norm']], [(D_MODEL, F32)],
                  rows=512, total=s, name="ffn1_out")

    u2, = rowcall(_rms, [h1], [small['mix_pre_norm']], [(D_MODEL, BF)], rows=512, total=s, name="mix_pre")
    proj = mm(u2, w_proj, name="mix_proj")
    nt = s // min(FOX_T, s)
    tt = min(FOX_T, s)
    f_cum = fox_f_fwd(proj, bias_row)
    f_heads = f_cum[:, :FOX_HEADS].T
    f_col, f_row = f_heads[:, :, None], f_heads.reshape(FOX_HEADS, nt, 1, tt)
    qh, kh, vh = [_to_heads(proj[:, i * 512:(i + 1) * 512]).astype(BF) for i in range(3)]
    fox_o, lse = fox_fwd(qh, kh, vh, f_col, f_row)
    fox_flat = _from_heads(fox_o)
    cqkv = conv_fwd(proj, conv_w8)
    g_l, b_l = rowcall(_gdn_gates, [(proj, 128, SMALL_BLOCK128)], [gate_prm], [(512, F32), (512, F32)],
                       rows=512, total=s, name="gdn_gates")
    gbb = jnp.concatenate([g_l, b_l], axis=1)
    gdn_o, states = gdn_fwd(cqkv, proj, gbb, onorm)
    mixed = jnp.concatenate([fox_flat, gdn_o], axis=1).astype(BF)
    mo = mm(mixed, w_out, name="mix_out")
    h2, = rowcall(_residual_rms, [h1, mo], [small['mix_post_norm']], [(D_MODEL, F32)], rows=512, total=s,
                  name="mix_res")

    hq, = rowcall(_rms, [h2], [small['mem_pre_norm']], [(D_MODEL, BF)], rows=512, total=s, name="mem_pre")
    mn, = rowcall(_rms, [mem], [small['mem_kv_norm']], [(D_MODEL, BF)], rows=256, total=mem.shape[0], name="mem_kvn")
    q_mem = mm(hq, w_q, name="mem_q")
    kv_mem = mm(mn, w_kv, name="mem_kv")
    o_mem = xattn_fwd(q_mem, kv_mem)
    c_mem = mm(o_mem, w_o, name="mem_o")
    h3, = rowcall(_residual_rms, [h2, c_mem], [small['mem_post_norm']], [(D_MODEL, F32)], rows=512, total=s,
                  name="mem_res")

    sv2 = _ffn_fwd(h3, small['ffn2_pre_norm'], wgu2, wd2, "ffn2")

    def b_loss(h, f, tgt, g):
        err = h + _half_rms(f, g) - tgt
        part = 0.5 * jnp.sum(jnp.mean(err * err, axis=-1, keepdims=True), axis=0, keepdims=True)
        return err * (1.0 / D_MODEL), jnp.broadcast_to(part, (1, 128))

    dy, loss_acc = rowcall(b_loss, [h3, sv2[3], a['loss_target'][0]], [small['ffn2_post_norm']], [(D_MODEL, F32)],
                           [(1, 128)], rows=512, total=s, name="loss")

    grads = {}
    dh3, dwgu2, dwd2, grads['ffn2_pre_norm'], grads['ffn2_post_norm'] = _ffn_bwd(
        dy, h3, small['ffn2_pre_norm'], small['ffn2_post_norm'], wgu2, wd2, sv2, "ffn2")

    dc, grads['mem_post_norm'] = rowcall(_bwd_residual, [dh3, c_mem], [small['mem_post_norm']], [(D_MODEL, BF)],
                                         [(1, D_MODEL)], rows=512, total=s, name="mem_bwd_res")
    d_o = mm(dc, w_o, tb=True, name="mem_bwd_do")
    dw_o = mm(o_mem, dc, ta=True, name="mem_bwd_dwo")
    dq_mem, dkv = xattn_bwd(q_mem, kv_mem, d_o)
    dhq = mm(dq_mem, w_q, tb=True, name="mem_bwd_dhq")
    dw_q = mm(hq, dq_mem, ta=True, name="mem_bwd_dwq")
    dmn = mm(dkv, w_kv, tb=True, name="mem_bwd_dmn")
    dw_kv = mm(mn, dkv, ta=True, name="mem_bwd_dwkv")
    _, grads['mem_kv_norm'] = rowcall(lambda d, mv, g: jax.vjp(_rms, mv, g)[1](d), [dmn, mem],
                                      [small['mem_kv_norm']], [(D_MODEL, F32)], [(1, D_MODEL)], rows=256,
                                      total=mem.shape[0], name="mem_bwd_kvn")

    def b_pre(dh, duv, hv, pg):
        dx, dpre = jax.vjp(_rms, hv, pg)[1](duv)
        return dh + dx, dpre

    dh2, grads['mem_pre_norm'] = rowcall(b_pre, [dh3, dhq, h2], [small['mem_pre_norm']], [(D_MODEL, F32)],
                                         [(1, D_MODEL)], rows=512, total=s, name="mem_bwd_pre")

    dmo, grads['mix_post_norm'] = rowcall(_bwd_residual, [dh2, mo], [small['mix_post_norm']], [(D_MODEL, BF)],
                                          [(1, D_MODEL)], rows=512, total=s, name="mix_bwd_res")
    d_mixed = mm(dmo, w_out, tb=True, name="mix_bwd_dmixed")
    dw_out = mm(mixed, dmo, ta=True, name="mix_bwd_dwout")
    def b_delta(do, o):
        sel = (_iota2((512, 128), 0) // FOX_DH == _iota2((512, 128), 1)).astype(F32)
        return hdot(do * o, sel)

    delta, = rowcall(b_delta, [(d_mixed, 512, 0), fox_flat], [], [(128, F32)], rows=512, total=s, name="fox_delta")
    delta_row = delta[:, :FOX_HEADS].T.reshape(FOX_HEADS, nt, 1, tt)
    lse_row = lse[:, :, 0].reshape(FOX_HEADS, nt, 1, tt)
    doh = _to_heads(d_mixed[:, :512]).astype(BF)
    dqh, dkh, dvh, dfk, dfq = fox_bwd(qh, kh, vh, doh, f_col, f_row, lse_row, delta_row)
    d_f = _pad_to((dfk[:, :, 0] + dfq.reshape(FOX_HEADS, s)).T, (s, 128))
    dsmall_f, dbias = fox_f_bwd(proj, bias_row, d_f)
    grads['fox_f_bias'] = dbias[:, :FOX_HEADS]
    dcqkv, dz, dgb, grads['gdn_out_norm'] = gdn_bwd(cqkv, proj, gbb, onorm, states, d_mixed)

    def b_gates(sm, dsf, dg, db, prm):
        dsm, dprm = jax.vjp(_gdn_gates, sm, prm)[1]((dg, db))
        return dsm + dsf, dprm

    dsmall, dprm = rowcall(b_gates, [(proj, 128, SMALL_BLOCK128), dsmall_f, (dgb, 512, 0), (dgb, 512, 1)], [gate_prm],
                           [(128, F32)],
                           [(8, 128)], rows=512, total=s, name="gdn_bwd_gates")
    grads['gdn_a_log'] = dprm[0:1, SMALL_A:SMALL_A + GDN_HEADS]
    grads['gdn_dt_bias'] = dprm[1:2, SMALL_A:SMALL_A + GDN_HEADS]
    dqkv_pre, dconv8 = conv_bwd(proj, conv_w8, dcqkv)
    dproj = jnp.concatenate([_from_heads(dqh), _from_heads(dkh), _from_heads(dvh), dqkv_pre, dz, dsmall,
                             jnp.zeros((s, PROJ_W - 3584 - 128), F32)], axis=1).astype(BF)
    du2 = mm(dproj, w_proj, tb=True, name="mix_bwd_du")
    dw_proj = mm(u2, dproj, ta=True, name="mix_bwd_dwproj")
    dh1, grads['mix_pre_norm'] = rowcall(b_pre, [dh2, du2, h1], [small['mix_pre_norm']], [(D_MODEL, F32)],
                                         [(1, D_MODEL)], rows=512, total=s, name="mix_bwd_pre")

    grad_x, dwgu1, dwd1, grads['ffn1_pre_norm'], grads['ffn1_post_norm'] = _ffn_bwd(
        dh1, x, small['ffn1_pre_norm'], small['ffn1_post_norm'], wgu1, wd1, sv1, "ffn1")

    def ff_cols_shards(dw):
        return dw.reshape(D_MODEL, N_DEV, FF_SHARD_PAD)[:, :, :FF_SHARD].transpose(1, 0, 2)

    def ff_rows_shards(dw):
        return dw.reshape(N_DEV, FF_SHARD_PAD, D_MODEL)[:, :FF_SHARD]

    def col_shards(dw):
        return dw.reshape(dw.shape[0], N_DEV, -1).transpose(1, 0, 2)

    def row_shards(dw):
        return dw.reshape(N_DEV, -1, dw.shape[1])

    dw_in = jnp.concatenate([dw_proj[:, :1536], dw_proj[:, 3584:3592], dw_proj[:, 1536:3584],
                             dw_proj[:, 3592:3600]], axis=1)
    part = {'ffn1_w_gate': ff_cols_shards(dwgu1[:, :D_FF_PAD]), 'ffn1_w_up': ff_cols_shards(dwgu1[:, D_FF_PAD:]),
            'ffn1_w_down': ff_rows_shards(dwd1),
            'ffn2_w_gate': ff_cols_shards(dwgu2[:, :D_FF_PAD]), 'ffn2_w_up': ff_cols_shards(dwgu2[:, D_FF_PAD:]),
            'ffn2_w_down': ff_rows_shards(dwd2),
            'w_in': col_shards(dw_in), 'w_out': row_shards(dw_out), 'mem_w_q': row_shards(dw_q),
            'mem_w_kv': col_shards(dw_kv), 'mem_w_o': row_shards(dw_o)}
    gpack = _pack(lambda nm: grads[nm], conv=dconv8[:CONV_W], loss=loss_acc[:, :1])
    red = exchange([("scatter", part[nm].astype(BF)) for nm in BIG] + [("gather", gpack)], "reduce_grads")
    recv = dict(zip(BIG, red[:-1]))
    gsum_parts = red[-1]

    out_g, out_d, out_m, out_v = {}, {}, {}, {}
    for nm in BIG:
        r = recv[nm]
        res = adamw(w2[nm], m2[nm], v2[nm], [(r, r.shape[2], 0, d) for d in range(N_DEV)], "adamw_" + nm)
        out_g[nm], out_d[nm], out_m[nm], out_v[nm] = res
    wp = _pack(lambda nm: small[nm])
    mp = _pack(lambda nm: m2[nm][None])
    vp = _pack(lambda nm: v2[nm][None])
    pg, pd, pm, pv = adamw(wp, mp, vp, [(gsum_parts, D_MODEL, 0, d) for d in range(N_DEV)], "adamw_small")
    for dst, p in ((out_g, pg), (out_d, pd), (out_m, pm), (out_v, pv)):
        dst.update({k: val[0] for k, val in _unpack(p).items()})
    loss = pg[ROW_MISC, COL_LOSS]
    conv_g = lax.dynamic_slice_in_dim(pg[ROW_CONV:ROW_CONV + 6].reshape(CONV_W, CONV_CH), me * (CONV_CH // N_DEV),
                                      CONV_CH // N_DEV, axis=1)
    res = adamw(w2['gdn_conv_w'], m2['gdn_conv_w'], v2['gdn_conv_w'], [conv_g], "adamw_conv")
    out_g['gdn_conv_w'], out_d['gdn_conv_w'], out_m['gdn_conv_w'], out_v['gdn_conv_w'] = res

    def lead(t):
        return t[None]

    return (loss, grad_x[None], *[lead(out_g[nm]) for nm in WEIGHTS], *[lead(out_d[nm]) for nm in WEIGHTS],
            *[lead(out_m[nm]) for nm in WEIGHTS], *[lead(out_v[nm]) for nm in WEIGHTS])


def kernel(x, mem, ffn1_pre_norm, ffn1_w_gate, ffn1_w_up, ffn1_w_down, ffn1_post_norm, mix_pre_norm, w_in, fox_f_bias, gdn_conv_w, gdn_a_log, gdn_dt_bias, gdn_out_norm, w_out, mix_post_norm, mem_pre_norm, mem_kv_norm, mem_w_q, mem_w_kv, mem_w_o, mem_post_norm, ffn2_pre_norm, ffn2_w_gate, ffn2_w_up, ffn2_w_down, ffn2_post_norm, loss_target, m_ffn1_pre_norm, m_ffn1_w_gate, m_ffn1_w_up, m_ffn1_w_down, m_ffn1_post_norm, m_mix_pre_norm, m_w_in, m_fox_f_bias, m_gdn_conv_w, m_gdn_a_log, m_gdn_dt_bias, m_gdn_out_norm, m_w_out, m_mix_post_norm, m_mem_pre_norm, m_mem_kv_norm, m_mem_w_q, m_mem_w_kv, m_mem_w_o, m_mem_post_norm, m_ffn2_pre_norm, m_ffn2_w_gate, m_ffn2_w_up, m_ffn2_w_down, m_ffn2_post_norm, v_ffn1_pre_norm, v_ffn1_w_gate, v_ffn1_w_up, v_ffn1_w_down, v_ffn1_post_norm, v_mix_pre_norm, v_w_in, v_fox_f_bias, v_gdn_conv_w, v_gdn_a_log, v_gdn_dt_bias, v_gdn_out_norm, v_w_out, v_mix_post_norm, v_mem_pre_norm, v_mem_kv_norm, v_mem_w_q, v_mem_w_kv, v_mem_w_o, v_mem_post_norm, v_ffn2_pre_norm, v_ffn2_w_gate, v_ffn2_w_up, v_ffn2_w_down, v_ffn2_post_norm):
    return _step(dict(locals()))
```

```python
import functools

import jax
import jax.numpy as jnp
from jax import lax
from jax.experimental import pallas as pl
from jax.experimental.pallas import tpu as pltpu

F32 = jnp.float32
BF = jnp.bfloat16
HI = lax.Precision.HIGHEST
MESH = pl.DeviceIdType.MESH

N_DEV = 8
EPS = 1e-6
D_MODEL = 1024
D_FF = 2816
FF_SHARD = D_FF // N_DEV
FF_SHARD_PAD = 384
D_FF_PAD = FF_SHARD_PAD * N_DEV
FOX_HEADS, FOX_DH = 8, 64
GDN_HEADS, GDN_DH = 4, 128
GDN_CHUNK = 64
CONV_W = 4
MEM_HEADS, MEM_DH = 4, 256
IN_W = 3600
IN_SHARD = IN_W // N_DEV
PROJ_W = 4096
SMALL_F, SMALL_B, SMALL_A = 0, 8, 12

ADAM_LR, ADAM_B1, ADAM_B2, ADAM_EPS, ADAM_WD, ADAM_STEP = 0.001, 0.9, 0.999, 1e-08, 0.01, 10

VMEM_LIMIT = 56 * 1024 * 1024


def _params(sem=None):
    return pltpu.CompilerParams(dimension_semantics=sem, vmem_limit_bytes=VMEM_LIMIT)


def _tile(n, pref, unit=128):
    if n <= pref:
        return n
    t = (pref // unit) * unit
    while t > unit and n % t:
        t -= unit
    assert n % t == 0, (n, pref)
    return t


@functools.partial(jax.custom_vjp, nondiff_argnums=(2, 3))
def bdot(a, b, ca, cb):
    return lax.dot_general(a.astype(BF), b.astype(BF), (((ca,), (cb,)), ((), ())), preferred_element_type=F32)


def _bdot_fwd(a, b, ca, cb):
    return bdot(a, b, ca, cb), (a, b)


def _bdot_bwd(ca, cb, res, g):
    a, b = res
    da = bdot(g, b, 1, 1 - cb) if ca == 1 else bdot(b, g, 1 - cb, 1)
    db = bdot(a, g, 1 - ca, 0) if cb == 0 else bdot(g, a, 0, 1 - ca)
    return da, db


bdot.defvjp(_bdot_fwd, _bdot_bwd)


def hdot(a, b):
    return jnp.dot(a, b, precision=HI, preferred_element_type=F32)


def _iota2(shape, dim):
    return lax.broadcasted_iota(jnp.int32, shape, dim)


def _sigmoid(x):
    return 1.0 / (1.0 + jnp.exp(-x))


def _silu(x):
    return x * _sigmoid(x)


def _softplus(x):
    return jnp.maximum(x, 0.0) + jnp.log(1.0 + jnp.exp(-jnp.abs(x)))


def _rms(x, gain):
    return x * lax.rsqrt(jnp.mean(x * x, axis=-1, keepdims=True) + EPS) * gain


def mm(a, b, *, name, ta=False, tb=False, out_dtype=F32, tm=1024, tn=1024, tk=512):
    m, k = (a.shape[1], a.shape[0]) if ta else a.shape
    n = b.shape[0] if tb else b.shape[1]
    assert k == (b.shape[1] if tb else b.shape[0]), (a.shape, b.shape, ta, tb)
    tm, tn, tk = _tile(m, tm), _tile(n, tn), _tile(k, tk)
    nk = k // tk
    dims = (((0 if ta else 1,), (1 if tb else 0,)), ((), ()))

    def kern(a_ref, b_ref, o_ref, *scratch):
        def part():
            return lax.dot_general(a_ref[...].astype(BF), b_ref[...].astype(BF), dims, preferred_element_type=F32)

        if nk == 1:
            o_ref[...] = part().astype(o_ref.dtype)
            return
        acc_ref, = scratch
        kk = pl.program_id(2)

        @pl.when(kk == 0)
        def _():
            acc_ref[...] = part()

        @pl.when(kk > 0)
        def _():
            acc_ref[...] += part()

        @pl.when(kk == nk - 1)
        def _():
            o_ref[...] = acc_ref[...].astype(o_ref.dtype)

    a_spec = pl.BlockSpec((tk, tm), lambda i, j, kk: (kk, i)) if ta else pl.BlockSpec((tm, tk), lambda i, j, kk: (i, kk))
    b_spec = pl.BlockSpec((tn, tk), lambda i, j, kk: (j, kk)) if tb else pl.BlockSpec((tk, tn), lambda i, j, kk: (kk, j))
    return pl.pallas_call(
        kern, name=name, grid=(m // tm, n // tn, nk),
        in_specs=[a_spec, b_spec],
        out_specs=pl.BlockSpec((tm, tn), lambda i, j, kk: (i, j)),
        out_shape=jax.ShapeDtypeStruct((m, n), out_dtype),
        scratch_shapes=[pltpu.VMEM((tm, tn), F32)] if nk > 1 else [],
        compiler_params=_params(("parallel", "parallel", "arbitrary")),
    )(a, b)


def _row_spec(item, rows):
    if not isinstance(item, tuple):
        return item, pl.BlockSpec((rows, item.shape[1]), lambda i: (i, 0))
    if len(item) == 3:
        arr, w, c = item
        return arr, pl.BlockSpec((rows, w), lambda i: (i, c))
    arr, w, c, lead = item
    return arr, pl.BlockSpec((None, rows, w), lambda i: (lead, i, c))


def _whole_spec(item):
    if not isinstance(item, tuple):
        return item, pl.BlockSpec(item.shape, lambda i: (0,) * item.ndim)
    arr, w, c = item
    return arr, pl.BlockSpec((arr.shape[0], w), lambda i: (0, c))


def rowcall(body, tiled, whole, outs, accs=(), *, rows, total, name):
    rows = min(rows, total)
    assert total % rows == 0
    t_arr, t_spec = zip(*[_row_spec(t, rows) for t in tiled])
    w_arr, w_spec = zip(*[_whole_spec(w) for w in whole]) if whole else ((), ())
    nt, nw, no, na = len(t_arr), len(w_arr), len(outs), len(accs)

    def kern(*refs):
        vals = [r[...] for r in refs[:nt + nw]]
        res = body(*vals)
        if not isinstance(res, (tuple, list)):
            res = (res,)
        assert len(res) == no + na, (name, len(res), no, na)
        for r, v in zip(refs[nt + nw:nt + nw + no], res[:no]):
            r[...] = v.astype(r.dtype)
        if na:
            acc_refs = refs[nt + nw + no:]

            @pl.when(pl.program_id(0) == 0)
            def _():
                for r in acc_refs:
                    r[...] = jnp.zeros_like(r)

            for r, v in zip(acc_refs, res[no:]):
                r[...] += v

    out_shape = [jax.ShapeDtypeStruct((total, w), d) for w, d in outs] + [jax.ShapeDtypeStruct(s, F32) for s in accs]
    out_specs = [pl.BlockSpec((rows, w), lambda i: (i, 0)) for w, _ in outs] + \
                [pl.BlockSpec(s, lambda i: (0, 0)) for s in accs]
    res = pl.pallas_call(
        kern, name=name, grid=(total // rows,),
        in_specs=list(t_spec) + list(w_spec), out_specs=out_specs, out_shape=out_shape,
        compiler_params=_params(("arbitrary",) if na else ("parallel",)),
    )(*t_arr, *w_arr)
    return res


def _colsum(x):
    return jnp.sum(x, axis=0, keepdims=True)


def _gdn_chunk(q, k, v, z, gb, bb, state, gain):
    c = GDN_CHUNK
    nh = len(q)
    hs = range(nh)
    r64, c64 = _iota2((c, c), 0), _iota2((c, c), 1)
    incl = r64 >= c64
    strict = r64 > c64
    ltri = incl.astype(F32)
    utri = (r64 <= c64).astype(F32)
    eye = (r64 == c64).astype(F32)
    ones = jnp.ones((c, c), F32)
    pick = (_iota2((GDN_DH, c), 0) == _iota2((GDN_DH, c), 1)).astype(F32)
    last = (_iota2((c, GDN_DH), 0) == c - 1).astype(F32)

    qn = [q[h] * lax.rsqrt(jnp.sum(q[h] * q[h], axis=-1, keepdims=True) + EPS) * (GDN_DH ** -0.5) for h in hs]
    kn = [k[h] * lax.rsqrt(jnp.sum(k[h] * k[h], axis=-1, keepdims=True) + EPS) for h in hs]
    gc = [hdot(ltri, gb[h]) for h in hs]
    g64 = [hdot(gb[h], pick) for h in hs]
    gcol = [hdot(ltri, g64[h]) for h in hs]
    grow = [hdot(ones, g64[h] * utri) for h in hs]
    dec = [jnp.exp(jnp.where(incl, gcol[h] - grow[h], -1e30)) for h in hs]
    kb = [kn[h] * bb[h] for h in hs]
    vb = [v[h] * bb[h] for h in hs]
    kk = [bdot(kb[h], kn[h], 1, 1) for h in hs]
    p = [-jnp.where(strict, kk[h] * dec[h], 0.0) for h in hs]
    tinv = [eye + p[h] for h in hs]
    for _ in range(5):
        p = [hdot(p[h], p[h]) for h in hs]
        tinv = [tinv[h] + hdot(tinv[h], p[h]) for h in hs]
    egc = [jnp.exp(gc[h]) for h in hs]
    u = [hdot(tinv[h], vb[h]) for h in hs]
    w = [hdot(tinv[h], kb[h] * egc[h]) for h in hs]
    attn = [bdot(qn[h], kn[h], 1, 1) * dec[h] for h in hs]
    qd = [qn[h] * egc[h] for h in hs]
    gl = [jnp.sum(gc[h] * last, axis=0, keepdims=True) for h in hs]
    kt = [kn[h] * jnp.exp(gl[h] - gc[h]) for h in hs]
    ws = [bdot(w[h], state[h], 1, 0) for h in hs]
    qs = [bdot(qd[h], state[h], 1, 0) for h in hs]
    v_new = [u[h] - ws[h] for h in hs]
    av = [bdot(attn[h], v_new[h], 1, 0) for h in hs]
    kv = [bdot(kt[h], v_new[h], 0, 0) for h in hs]
    new_state = tuple(state[h] * jnp.exp(gl[h]) + kv[h] for h in hs)
    o = tuple(_rms(qs[h] + av[h], gain) * _silu(z[h]) for h in hs)
    return o, new_state


GDN_ROWS = 512
GDN_W = GDN_HEADS * GDN_DH


def gdn_fwd(cqkv, proj, gbb, gain):
    s = cqkv.shape[0]
    nb, cpb = s // GDN_ROWS, GDN_ROWS // GDN_CHUNK
    h4 = GDN_HEADS

    def kern(qkv_ref, z_ref, gb_ref, gain_ref, o_ref, st_ref, state):
        @pl.when(pl.program_id(0) == 0)
        def _():
            state[...] = jnp.zeros_like(state)

        gain_v = gain_ref[...]

        def step(ci, carry):
            sl = pl.ds(pl.multiple_of(ci * GDN_CHUNK, GDN_CHUNK), GDN_CHUNK)
            ins = []
            for h in range(h4):
                ln = lambda base, h=h: slice(base + h * GDN_DH, base + (h + 1) * GDN_DH)
                ins.append((qkv_ref[sl, ln(0)], qkv_ref[sl, ln(GDN_W)], qkv_ref[sl, ln(2 * GDN_W)], z_ref[sl, ln(0)],
                            gb_ref[sl, ln(0)], gb_ref[sl, ln(GDN_W)], state[h]))
            cols = [tuple(col) for col in zip(*ins)]
            o, new = _gdn_chunk(*cols[:7], gain_v)
            for h in range(h4):
                st_ref[h, ci] = ins[h][6]
                o_ref[sl, h * GDN_DH:(h + 1) * GDN_DH] = o[h]
                state[h] = new[h]
            return carry

        lax.fori_loop(0, cpb, step, 0)

    return pl.pallas_call(
        kern, name="gdn_fwd", grid=(nb,),
        in_specs=[pl.BlockSpec((GDN_ROWS, 3 * GDN_W), lambda i: (i, 0)),
                  pl.BlockSpec((GDN_ROWS, GDN_W), lambda i: (i, 6)),
                  pl.BlockSpec((GDN_ROWS, 2 * GDN_W), lambda i: (i, 0)),
                  pl.BlockSpec((1, GDN_DH), lambda i: (0, 0))],
        out_specs=[pl.BlockSpec((GDN_ROWS, GDN_W), lambda i: (i, 0)),
                   pl.BlockSpec((h4, cpb, GDN_DH, GDN_DH), lambda i: (0, i, 0, 0))],
        out_shape=[jax.ShapeDtypeStruct((s, GDN_W), F32),
                   jax.ShapeDtypeStruct((h4, s // GDN_CHUNK, GDN_DH, GDN_DH), F32)],
        scratch_shapes=[pltpu.VMEM((h4, GDN_DH, GDN_DH), F32)],
        compiler_params=_params(("arbitrary",)),
    )(cqkv, proj, gbb, gain)


def gdn_bwd(cqkv, proj, gbb, gain, states, d_mixed):
    s = cqkv.shape[0]
    nb, cpb = s // GDN_ROWS, GDN_ROWS // GDN_CHUNK
    h4 = GDN_HEADS

    def kern(qkv_ref, z_ref, gb_ref, gain_ref, st_ref, do_ref, dqkv_ref, dz_ref, dgb_ref, dgain_ref, dstate):
        @pl.when(pl.program_id(0) == 0)
        def _():
            dgain_ref[...] = jnp.zeros_like(dgain_ref)
            dstate[...] = jnp.zeros_like(dstate)

        gain_v = gain_ref[...]

        def step(t, carry):
            ci = cpb - 1 - t
            sl = pl.ds(pl.multiple_of(ci * GDN_CHUNK, GDN_CHUNK), GDN_CHUNK)
            prim, cot, dst_in = [], [], []
            for h in range(h4):
                ln = lambda base, h=h: slice(base + h * GDN_DH, base + (h + 1) * GDN_DH)
                prim.append((qkv_ref[sl, ln(0)], qkv_ref[sl, ln(GDN_W)], qkv_ref[sl, ln(2 * GDN_W)], z_ref[sl, ln(0)],
                             gb_ref[sl, ln(0)], gb_ref[sl, ln(GDN_W)], st_ref[h, ci]))
                cot.append(do_ref[sl, ln(0)])
                dst_in.append(dstate[h])
            cols = [tuple(col) for col in zip(*prim)]
            vjp = jax.vjp(_gdn_chunk, *cols, gain_v)[1]
            dq, dk, dv, dz, dg, db, dst, dgn = vjp((tuple(cot), tuple(dst_in)))
            for h in range(h4):
                ln = lambda base, h=h: slice(base + h * GDN_DH, base + (h + 1) * GDN_DH)
                dqkv_ref[sl, ln(0)] = dq[h]
                dqkv_ref[sl, ln(GDN_W)] = dk[h]
                dqkv_ref[sl, ln(2 * GDN_W)] = dv[h]
                dz_ref[sl, ln(0)] = dz[h]
                dgb_ref[sl, ln(0)] = dg[h]
                dgb_ref[sl, ln(GDN_W)] = db[h]
                dstate[h] = dst[h]
            dgain_ref[...] += dgn
            return carry

        lax.fori_loop(0, cpb, step, 0)

    def rev(width, cblock=0):
        return pl.BlockSpec((GDN_ROWS, width), lambda i: (nb - 1 - i, cblock))

    return pl.pallas_call(
        kern, name="gdn_bwd", grid=(nb,),
        in_specs=[rev(3 * GDN_W), rev(GDN_W, 6), rev(2 * GDN_W), pl.BlockSpec((1, GDN_DH), lambda i: (0, 0)),
                  pl.BlockSpec((h4, cpb, GDN_DH, GDN_DH), lambda i: (0, nb - 1 - i, 0, 0)), rev(GDN_W, 1)],
        out_specs=[rev(3 * GDN_W), rev(GDN_W), rev(2 * GDN_W), pl.BlockSpec((1, GDN_DH), lambda i: (0, 0))],
        out_shape=[jax.ShapeDtypeStruct((s, 3 * GDN_W), F32), jax.ShapeDtypeStruct((s, GDN_W), F32),
                   jax.ShapeDtypeStruct((s, 2 * GDN_W), F32), jax.ShapeDtypeStruct((1, GDN_DH), F32)],
        scratch_shapes=[pltpu.VMEM((h4, GDN_DH, GDN_DH), F32)],
        compiler_params=_params(("arbitrary",)),
    )(cqkv, proj, gbb, gain, states, d_mixed)


def _gdn_gates(small, prm):
    w = GDN_HEADS * GDN_DH
    lane, head = _iota2((128, w), 0), _iota2((128, w), 1) // GDN_DH
    sel_b = (lane == SMALL_B + head).astype(F32)
    sel_a = (lane == SMALL_A + head).astype(F32)
    prow = _iota2((8, 128), 0)
    a_log = jnp.sum(prm * (prow == 0).astype(F32), axis=0, keepdims=True)
    dt_b = jnp.sum(prm * (prow == 1).astype(F32), axis=0, keepdims=True)
    beta = _sigmoid(hdot(small, sel_b))
    g = hdot(-jnp.exp(a_log) * _softplus(small + dt_b), sel_a)
    return g, beta


CONV_ROWS = 1024
CONV_COLS = 128
CONV_BLOCK0 = 1536 // CONV_COLS


def _shift_down(prev8, cur, s):
    ext = jnp.concatenate([prev8, cur], axis=0)
    return pltpu.roll(ext, s, 0)[8:]


def _shift_up(cur, next8, s):
    n = cur.shape[0]
    ext = jnp.concatenate([cur, next8], axis=0)
    return pltpu.roll(ext, n + 8 - s, 0)[:n]


def _conv_pre(x_ref, w, ci, nchunk):
    r0 = pl.multiple_of(ci * CONV_ROWS, CONV_ROWS)
    cur = x_ref[pl.ds(r0, CONV_ROWS), :]
    prev = x_ref[pl.ds(pl.multiple_of(jnp.maximum(r0 - 8, 0), 8), 8), :]
    prev = jnp.where(ci > 0, prev, 0.0)
    shifted = [cur] + [_shift_down(prev, cur, s) for s in range(1, CONV_W)]
    pre = w[CONV_W - 1:CONV_W, :] * cur
    for s in range(1, CONV_W):
        pre = pre + w[CONV_W - 1 - s:CONV_W - s, :] * shifted[s]
    return r0, pre, shifted


def conv_fwd(proj, conv_w8):
    s = proj.shape[0]
    nchunk = s // CONV_ROWS
    ncol = 3 * GDN_HEADS * GDN_DH // CONV_COLS

    def kern(x_ref, w_ref, y_ref):
        w = w_ref[...]

        def step(ci, carry):
            r0, pre, _ = _conv_pre(x_ref, w, ci, nchunk)
            y_ref[pl.ds(r0, CONV_ROWS), :] = _silu(pre)
            return carry

        lax.fori_loop(0, nchunk, step, 0)

    return pl.pallas_call(
        kern, name="conv_fwd", grid=(ncol,),
        in_specs=[pl.BlockSpec((s, CONV_COLS), lambda j: (0, CONV_BLOCK0 + j)),
                  pl.BlockSpec((8, CONV_COLS), lambda j: (0, j))],
        out_specs=pl.BlockSpec((s, CONV_COLS), lambda j: (0, j)),
        out_shape=jax.ShapeDtypeStruct((s, ncol * CONV_COLS), F32),
        compiler_params=_params(("parallel",)),
    )(proj, conv_w8)


def conv_bwd(proj, conv_w8, dy):
    s = proj.shape[0]
    nchunk = s // CONV_ROWS
    per = 3 * GDN_HEADS * GDN_DH // CONV_COLS
    outs = []
    for part in range(1):
        def kern(x_ref, w_ref, dy_ref, dx_ref, dw_ref, dpre_ref):
            w = w_ref[...]
            rows8 = _iota2((8, CONV_COLS), 0)

            def step1(ci, dw):
                r0, pre, shifted = _conv_pre(x_ref, w, ci, nchunk)
                sg = _sigmoid(pre)
                dpre = dy_ref[pl.ds(r0, CONV_ROWS), :] * sg * (1.0 + pre * (1.0 - sg))
                dpre_ref[pl.ds(r0, CONV_ROWS), :] = dpre
                for sh in range(CONV_W):
                    dw = dw + jnp.where(rows8 == CONV_W - 1 - sh, _colsum(dpre * shifted[sh]), 0.0)
                return dw

            dw_ref[...] = lax.fori_loop(0, nchunk, step1, jnp.zeros((8, CONV_COLS), F32))

            def step2(ci, carry):
                r0 = pl.multiple_of(ci * CONV_ROWS, CONV_ROWS)
                cur = dpre_ref[pl.ds(r0, CONV_ROWS), :]
                nxt = dpre_ref[pl.ds(pl.multiple_of(jnp.minimum(r0 + CONV_ROWS, s - 8), 8), 8), :]
                nxt = jnp.where(ci < nchunk - 1, nxt, 0.0)
                dx = w[CONV_W - 1:CONV_W, :] * cur
                for sh in range(1, CONV_W):
                    dx = dx + w[CONV_W - 1 - sh:CONV_W - sh, :] * _shift_up(cur, nxt, sh)
                dx_ref[pl.ds(r0, CONV_ROWS), :] = dx
                return carry

            lax.fori_loop(0, nchunk, step2, 0)

        outs.append(pl.pallas_call(
            kern, name=f"conv_bwd{part}", grid=(per,),
            in_specs=[pl.BlockSpec((s, CONV_COLS), lambda j, part=part: (0, CONV_BLOCK0 + part * per + j)),
                      pl.BlockSpec((8, CONV_COLS), lambda j, part=part: (0, part * per + j)),
                      pl.BlockSpec((s, CONV_COLS), lambda j: (0, j))],
            out_specs=[pl.BlockSpec((s, CONV_COLS), lambda j: (0, j)),
                       pl.BlockSpec((8, CONV_COLS), lambda j: (0, j))],
            out_shape=[jax.ShapeDtypeStruct((s, per * CONV_COLS), F32),
                       jax.ShapeDtypeStruct((8, per * CONV_COLS), F32)],
            scratch_shapes=[pltpu.VMEM((s, CONV_COLS), F32)],
            compiler_params=_params(("parallel",)),
        )(proj, conv_w8, dy))
    dx = jnp.concatenate([o[0] for o in outs], axis=1)
    dw = jnp.concatenate([o[1] for o in outs], axis=1)
    return dx, dw


FOXF_ROWS = 512
SMALL_BLOCK128 = 3584 // 128


def _log_sigmoid(x):
    return jnp.minimum(x, 0.0) - jnp.log(1.0 + jnp.exp(-jnp.abs(x)))


def fox_f_fwd(proj, bias_row):
    s = proj.shape[0]
    n = s // FOXF_ROWS

    def kern(x_ref, b_ref, f_ref, carry):
        @pl.when(pl.program_id(0) == 0)
        def _():
            carry[...] = jnp.zeros_like(carry)

        heads = _iota2((FOXF_ROWS, 128), 1) < FOX_HEADS
        lf = jnp.where(heads, _log_sigmoid(x_ref[...] + b_ref[...]), 0.0)
        ltri = (_iota2((FOXF_ROWS, FOXF_ROWS), 0) >= _iota2((FOXF_ROWS, FOXF_ROWS), 1)).astype(F32)
        c = hdot(ltri, lf) + carry[...]
        f_ref[...] = c
        carry[...] = c[FOXF_ROWS - 1:FOXF_ROWS, :]

    return pl.pallas_call(
        kern, name="fox_f_fwd", grid=(n,),
        in_specs=[pl.BlockSpec((FOXF_ROWS, 128), lambda i: (i, SMALL_BLOCK128)),
                  pl.BlockSpec((1, 128), lambda i: (0, 0))],
        out_specs=pl.BlockSpec((FOXF_ROWS, 128), lambda i: (i, 0)),
        out_shape=jax.ShapeDtypeStruct((s, 128), F32),
        scratch_shapes=[pltpu.VMEM((1, 128), F32)],
        compiler_params=_params(("arbitrary",)),
    )(proj, bias_row)


def fox_f_bwd(proj, bias_row, d_f):
    s = proj.shape[0]
    n = s // FOXF_ROWS

    def kern(x_ref, b_ref, df_ref, dx_ref, db_ref, carry):
        @pl.when(pl.program_id(0) == 0)
        def _():
            carry[...] = jnp.zeros_like(carry)
            db_ref[...] = jnp.zeros_like(db_ref)

        heads = _iota2((FOXF_ROWS, 128), 1) < FOX_HEADS
        utri = (_iota2((FOXF_ROWS, FOXF_ROWS), 0) <= _iota2((FOXF_ROWS, FOXF_ROWS), 1)).astype(F32)
        rc = hdot(utri, df_ref[...]) + carry[...]
        carry[...] = rc[0:1, :]
        dx = jnp.where(heads, rc * _sigmoid(-(x_ref[...] + b_ref[...])), 0.0)
        dx_ref[...] = dx
        db_ref[...] += _colsum(dx)

    return pl.pallas_call(
        kern, name="fox_f_bwd", grid=(n,),
        in_specs=[pl.BlockSpec((FOXF_ROWS, 128), lambda i: (n - 1 - i, SMALL_BLOCK128)),
                  pl.BlockSpec((1, 128), lambda i: (0, 0)),
                  pl.BlockSpec((FOXF_ROWS, 128), lambda i: (n - 1 - i, 0))],
        out_specs=[pl.BlockSpec((FOXF_ROWS, 128), lambda i: (n - 1 - i, 0)),
                   pl.BlockSpec((1, 128), lambda i: (0, 0))],
        out_shape=[jax.ShapeDtypeStruct((s, 128), F32), jax.ShapeDtypeStruct((1, 128), F32)],
        scratch_shapes=[pltpu.VMEM((1, 128), F32)],
        compiler_params=_params(("arbitrary",)),
    )(proj, bias_row, d_f)


FOX_T = 512
FOX_SCALE = FOX_DH ** -0.5
FOX_LANES = 128
NEG = -1e30
_NT = (((1,), (1,)), ((), ()))


def _split3(x):
    def bf(v):
        return lax.reduce_precision(v, exponent_bits=8, mantissa_bits=7)

    hi = bf(x)
    mid = bf(x - hi)
    lo = bf(x - hi - mid)
    return jnp.stack([hi, mid, lo], axis=-1).astype(BF)


def _fox_operand(feat, bias, ones_first):
    h, s, _ = feat.shape
    ones = jnp.ones((h, s, 3), BF)
    cols = [ones, _split3(bias)] if ones_first else [_split3(bias), ones]
    pad = jnp.zeros((h, s, FOX_LANES - FOX_DH - 6), BF)
    return jnp.concatenate([feat.astype(BF)] + cols + [pad], axis=-1)


def fox_fwd(qa, ka, va):
    h, s, d = qa.shape
    t = min(FOX_T, s)
    n = s // t

    def kern(q_ref, k_ref, v_ref, o_ref, lse_ref):
        i = pl.program_id(1)
        qv = q_ref[...]

        def step(j, carry, masked):
            m, acc = carry
            sl = pl.ds(pl.multiple_of(j * t, t), t)
            sc = lax.dot_general(qv, k_ref[sl, :], _NT, preferred_element_type=F32)
            if masked:
                sc = jnp.where(_iota2((t, t), 0) >= _iota2((t, t), 1), sc, NEG)
            m_new = jnp.maximum(m, jnp.max(sc, axis=1, keepdims=True))
            p = jnp.exp(sc - m_new).astype(BF)
            acc = jnp.exp(m - m_new) * acc + jnp.dot(p, v_ref[sl, :], preferred_element_type=F32)
            return m_new, acc

        init = (jnp.full((t, 1), NEG, F32), jnp.zeros((t, d), F32))
        carry = lax.fori_loop(0, i, lambda j, c: step(j, c, False), init)
        m, acc = step(i, carry, True)
        l = jnp.sum(jnp.where(_iota2((t, d), 1) == FOX_DH, acc, 0.0), axis=1, keepdims=True)
        o_ref[...] = acc / l
        lse_ref[...] = m + jnp.log(l)

    return pl.pallas_call(
        kern, name="fox_fwd", grid=(h, n),
        in_specs=[pl.BlockSpec((None, t, d), lambda hh, i: (hh, i, 0)),
                  pl.BlockSpec((None, s, d), lambda hh, i: (hh, 0, 0)),
                  pl.BlockSpec((None, s, d), lambda hh, i: (hh, 0, 0))],
        out_specs=[pl.BlockSpec((None, t, d), lambda hh, i: (hh, i, 0)),
                   pl.BlockSpec((None, t, 1), lambda hh, i: (hh, i, 0))],
        out_shape=[jax.ShapeDtypeStruct((h, s, d), F32), jax.ShapeDtypeStruct((h, s, 1), F32)],
        compiler_params=_params(("parallel", "parallel")),
    )(qa, ka, va)


def fox_bwd(ka, va, qb, dob):
    h, s, d = ka.shape
    t = min(FOX_T, s)
    n = s // t

    def kern(k_ref, v_ref, q_ref, do_ref, dq_ref, dk_ref, dv_ref):
        j = pl.program_id(1)

        @pl.when(j == 0)
        def _():
            dq_ref[...] = jnp.zeros_like(dq_ref)

        kv, vv = k_ref[...], v_ref[...]

        def step(i, carry, masked):
            dk, dv = carry
            sl = pl.ds(pl.multiple_of(i * t, t), t)
            qi, doi = q_ref[sl, :], do_ref[sl, :]
            st = lax.dot_general(kv, qi, _NT, preferred_element_type=F32)
            if masked:
                st = jnp.where(_iota2((t, t), 0) <= _iota2((t, t), 1), st, NEG)
            pt = jnp.exp(st)
            dsb = (pt * lax.dot_general(vv, doi, _NT, preferred_element_type=F32)).astype(BF)
            dv = dv + jnp.dot(pt.astype(BF), doi, preferred_element_type=F32)
            dk = dk + jnp.dot(dsb, qi, preferred_element_type=F32)
            dq_ref[sl, :] += lax.dot_general(dsb, kv, (((0,), (0,)), ((), ())), preferred_element_type=F32)
            return dk, dv

        init = (jnp.zeros((t, d), F32), jnp.zeros((t, d), F32))
        carry = step(j, init, True)
        dk, dv = lax.fori_loop(j + 1, n, lambda i, c: step(i, c, False), carry)
        dk_ref[...] = dk
        dv_ref[...] = dv

    tile = pl.BlockSpec((None, t, d), lambda hh, j: (hh, j, 0))
    whole = pl.BlockSpec((None, s, d), lambda hh, j: (hh, 0, 0))
    return pl.pallas_call(
        kern, name="fox_bwd", grid=(h, n),
        in_specs=[tile, tile, whole, whole],
        out_specs=[whole, tile, tile],
        out_shape=[jax.ShapeDtypeStruct((h, s, d), F32)] * 3,
        compiler_params=_params(("parallel", "arbitrary")),
    )(ka, va, qb, dob)


def _xattn_head(q, k, v):
    sc = bdot(q, k, 1, 1) * (MEM_DH ** -0.5)
    e = jnp.exp(sc - lax.stop_gradient(jnp.max(sc, axis=-1, keepdims=True)))
    p = e / jnp.sum(e, axis=-1, keepdims=True)
    return bdot(p, v, 1, 0)


def xattn_fwd(q, kv):
    s = q.shape[0]
    hh = MEM_HEADS

    def body(*vals):
        qs, ks, vs = vals[:hh], vals[hh:2 * hh], vals[2 * hh:]
        return jnp.concatenate([_xattn_head(qs[a], ks[a], vs[a]) for a in range(hh)], axis=1)

    return rowcall(body, [(q, MEM_DH, a) for a in range(hh)],
                   [(kv, MEM_DH, a) for a in range(2 * hh)],
                   [(hh * MEM_DH, BF)], rows=512, total=s, name="xattn_fwd")[0]


def xattn_bwd(q, kv, d_o):
    s = q.shape[0]
    hh = MEM_HEADS

    def body(*vals):
        qs, dos = vals[:hh], vals[hh:2 * hh]
        ks, vs = vals[2 * hh:3 * hh], vals[3 * hh:]
        dqs, dks, dvs = [], [], []
        for a in range(hh):
            _, vjp = jax.vjp(_xattn_head, qs[a], ks[a], vs[a])
            dq, dk, dv = vjp(dos[a])
            dqs.append(dq)
            dks.append(dk)
            dvs.append(dv)
        return jnp.concatenate(dqs, axis=1), jnp.concatenate(dks + dvs, axis=1)

    return rowcall(body, [(q, MEM_DH, a) for a in range(hh)] + [(d_o, MEM_DH, a) for a in range(hh)],
                   [(kv, MEM_DH, a) for a in range(2 * hh)],
                   [(hh * MEM_DH, BF)], [kv.shape], rows=512, total=s, name="xattn_bwd")


def exchange(items, name):
    n = len(items)
    npeer = N_DEV - 1

    def body(*refs):
        ins, outs = refs[:n], refs[n:2 * n]
        send, recv, loc = refs[2 * n:]
        x, y, c = lax.axis_index("x"), lax.axis_index("y"), lax.axis_index("c")
        me = 4 * x + 2 * y + c

        def peer(p):
            px = 1 - x if p & 4 else x
            py = 1 - y if p & 2 else y
            pc = 1 - c if p & 1 else c
            return (px, py, pc), 4 * px + 2 * py + pc

        local, remote = [], []
        for w, (mode, _) in enumerate(items):
            own = ins[w] if mode == "gather" else ins[w].at[me]
            cp = pltpu.make_async_copy(own, outs[w].at[me], loc.at[w])
            cp.start()
            local.append(cp)
        for p in range(1, N_DEV):
            dev, idx = peer(p)
            for w, (mode, _) in enumerate(items):
                src = ins[w] if mode == "gather" else ins[w].at[idx]
                k = w * npeer + p - 1
                out_cp = pltpu.make_async_remote_copy(src_ref=src, dst_ref=outs[w].at[me], send_sem=send.at[k],
                                                      recv_sem=recv.at[k], device_id=dev, device_id_type=MESH)
                out_cp.start()
                in_cp = pltpu.make_async_remote_copy(src_ref=src, dst_ref=outs[w].at[idx], send_sem=send.at[k],
                                                     recv_sem=recv.at[k], device_id=dev, device_id_type=MESH)
                remote.append((out_cp, in_cp))
        for out_cp, in_cp in remote:
            in_cp.wait_recv()
            out_cp.wait_send()
        for cp in local:
            cp.wait()

    out_shape = []
    for mode, a in items:
        shp = (N_DEV,) + tuple(a.shape) if mode == "gather" else tuple(a.shape)
        out_shape.append(jax.ShapeDtypeStruct(shp, a.dtype))
    hbm = pl.BlockSpec(memory_space=pl.ANY)
    return pl.pallas_call(
        body, name=name, in_specs=[hbm] * n, out_specs=[hbm] * n, out_shape=out_shape,
        scratch_shapes=[pltpu.SemaphoreType.DMA((n * npeer,)), pltpu.SemaphoreType.DMA((n * npeer,)),
                        pltpu.SemaphoreType.DMA((n,))],
        compiler_params=pltpu.CompilerParams(has_side_effects=True),
    )(*[a for _, a in items])


def adamw(w, m, v, contribs, name):
    r, c = w.shape
    nc = len(contribs)
    rows = next((r // d for d in (4, 2) if r % d == 0 and (r // d) % 16 == 0), r)
    c1, c2 = 1.0 - ADAM_B1 ** ADAM_STEP, 1.0 - ADAM_B2 ** ADAM_STEP

    def body(wv, mv, vv, *gs):
        g = gs[0].astype(F32)
        for extra in gs[1:]:
            g = g + extra.astype(F32)
        m_new = ADAM_B1 * mv + (1.0 - ADAM_B1) * g
        v_new = ADAM_B2 * vv + (1.0 - ADAM_B2) * (g * g)
        delta = -ADAM_LR * ((m_new / c1) / (jnp.sqrt(v_new / c2) + ADAM_EPS) + ADAM_WD * wv)
        return g, delta, m_new, v_new

    assert nc >= 1
    return rowcall(body, [w, m, v] + list(contribs), [], [(c, F32)] * 4, rows=rows, total=r, name=name)


WEIGHTS = ['ffn1_pre_norm', 'ffn1_w_gate', 'ffn1_w_up', 'ffn1_w_down', 'ffn1_post_norm', 'mix_pre_norm', 'w_in',
           'fox_f_bias', 'gdn_conv_w', 'gdn_a_log', 'gdn_dt_bias', 'gdn_out_norm', 'w_out', 'mix_post_norm',
           'mem_pre_norm', 'mem_kv_norm', 'mem_w_q', 'mem_w_kv', 'mem_w_o', 'mem_post_norm', 'ffn2_pre_norm',
           'ffn2_w_gate', 'ffn2_w_up', 'ffn2_w_down', 'ffn2_post_norm']
GAINS = ['ffn1_pre_norm', 'ffn1_post_norm', 'mix_pre_norm', 'mix_post_norm', 'mem_pre_norm', 'mem_kv_norm',
         'mem_post_norm', 'ffn2_pre_norm', 'ffn2_post_norm']
BIG = ['ffn1_w_gate', 'ffn1_w_up', 'ffn1_w_down', 'w_in', 'w_out', 'mem_w_q', 'mem_w_kv', 'mem_w_o',
       'ffn2_w_gate', 'ffn2_w_up', 'ffn2_w_down']
PACK_ROWS = 24
ROW_MISC = len(GAINS)
ROW_CONV = ROW_MISC + 1
COL_FBIAS, COL_ALOG, COL_DTB, COL_ONORM, COL_LOSS = 0, 8, 12, 128, 256
CONV_CH = 3 * GDN_HEADS * GDN_DH


def _pad_to(a, shape):
    return jnp.pad(a, [(0, t - s) for s, t in zip(a.shape, shape)])


def _pack(get, conv=None, loss=None):
    rows = [get(nm) for nm in GAINS]
    misc = jnp.concatenate([get('fox_f_bias'), get('gdn_a_log'), get('gdn_dt_bias'),
                            jnp.zeros((1, COL_ONORM - COL_DTB - 4), F32), get('gdn_out_norm'),
                            jnp.zeros((1, 1), F32) if loss is None else loss.reshape(1, 1)], axis=1)
    rows.append(_pad_to(misc, (1, D_MODEL)))
    rows.append(jnp.zeros((6, D_MODEL), F32) if conv is None else conv.reshape(6, D_MODEL))
    return _pad_to(jnp.concatenate(rows, axis=0), (PACK_ROWS, D_MODEL))


def _unpack(p):
    out = {nm: p[i:i + 1] for i, nm in enumerate(GAINS)}
    misc = p[ROW_MISC:ROW_MISC + 1]
    out['fox_f_bias'] = misc[:, COL_FBIAS:COL_FBIAS + FOX_HEADS]
    out['gdn_a_log'] = misc[:, COL_ALOG:COL_ALOG + GDN_HEADS]
    out['gdn_dt_bias'] = misc[:, COL_DTB:COL_DTB + GDN_HEADS]
    out['gdn_out_norm'] = misc[:, COL_ONORM:COL_ONORM + GDN_DH]
    return out


def _to_heads(a):
    return a.reshape(a.shape[0], FOX_HEADS, FOX_DH).transpose(1, 0, 2)


def _from_heads(a):
    return a.transpose(1, 0, 2).reshape(a.shape[1], FOX_HEADS * FOX_DH)


def _ffn_fwd(h, pre, wgu, wd, tag):
    s = h.shape[0]
    u, = rowcall(_rms, [h], [pre], [(D_MODEL, BF)], rows=512, total=s, name=tag + "_pre")
    gu = mm(u, wgu, name=tag + "_gate_up")
    act, = rowcall(lambda a, b: _silu(a) * b, [(gu, D_FF_PAD, 0), (gu, D_FF_PAD, 1)], [], [(D_FF_PAD, BF)],
                   rows=256, total=s, name=tag + "_act")
    f = mm(act, wd, name=tag + "_down")
    return u, gu, act, f


def _half_rms(a, g):
    return 0.5 * _rms(a, g)


def _ffn_bwd(dh_out, h, pre, post, wgu, wd, saved, tag):
    u, gu, act, f = saved
    s = h.shape[0]

    def b_post(dh, fv, pg):
        return jax.vjp(_half_rms, fv, pg)[1](dh)

    df, dpost = rowcall(b_post, [dh_out, f], [post], [(D_MODEL, BF)], [(1, D_MODEL)], rows=512, total=s,
                        name=tag + "_bwd_post")
    dact = mm(df, wd, tb=True, name=tag + "_bwd_dact")
    dwd = mm(act, df, ta=True, name=tag + "_bwd_dwd")

    def b_act(a, b, da):
        dg, du = jax.vjp(lambda g_, u_: _silu(g_) * u_, a, b)[1](da)
        return jnp.concatenate([dg, du], axis=1)

    dgu, = rowcall(b_act, [(gu, D_FF_PAD, 0), (gu, D_FF_PAD, 1), dact], [], [(2 * D_FF_PAD, BF)], rows=256, total=s,
                   name=tag + "_bwd_act")
    du = mm(dgu, wgu, tb=True, name=tag + "_bwd_du")
    dwgu = mm(u, dgu, ta=True, name=tag + "_bwd_dwgu")

    def b_pre(dh, duv, hv, pg):
        dx, dpre = jax.vjp(_rms, hv, pg)[1](duv)
        return dh + dx, dpre

    dh, dpre = rowcall(b_pre, [dh_out, du, h], [pre], [(D_MODEL, F32)], [(1, D_MODEL)], rows=512, total=s,
                       name=tag + "_bwd_pre")
    return dh, dwgu, dwd, dpre, dpost


def _residual_rms(h, a, g):
    return h + _rms(a, g)


def _bwd_residual(dh, a, g):
    return jax.vjp(_rms, a, g)[1](dh)


def _step(a):
    x, mem = a['x'][0], a['mem'][0]
    s = x.shape[0]
    me = 4 * lax.axis_index("x") + 2 * lax.axis_index("y") + lax.axis_index("c")
    w2 = {nm: a[nm][0] for nm in WEIGHTS}
    m2 = {nm: a['m_' + nm][0] for nm in WEIGHTS}
    v2 = {nm: a['v_' + nm][0] for nm in WEIGHTS}
    small = {nm: w2[nm][None] for nm in WEIGHTS if nm not in BIG and nm != 'gdn_conv_w'}

    def ff_cols(w):
        return _pad_to(w, (D_MODEL, FF_SHARD_PAD)).astype(BF)

    def ff_rows(w):
        return _pad_to(w, (FF_SHARD_PAD, D_MODEL)).astype(BF)

    send = {'ffn1_w_gate': ff_cols(w2['ffn1_w_gate']), 'ffn1_w_up': ff_cols(w2['ffn1_w_up']),
            'ffn1_w_down': ff_rows(w2['ffn1_w_down']),
            'ffn2_w_gate': ff_cols(w2['ffn2_w_gate']), 'ffn2_w_up': ff_cols(w2['ffn2_w_up']),
            'ffn2_w_down': ff_rows(w2['ffn2_w_down']),
            'w_in': w2['w_in'].astype(BF), 'w_out': w2['w_out'].astype(BF), 'mem_w_q': w2['mem_w_q'].astype(BF),
            'mem_w_kv': w2['mem_w_kv'].astype(BF), 'mem_w_o': w2['mem_w_o'].astype(BF),
            'gdn_conv_w': _pad_to(w2['gdn_conv_w'], (8, CONV_CH // N_DEV))}
    names = list(send)
    got = dict(zip(names, exchange([("gather", send[nm]) for nm in names], "gather_weights")))

    def cols(g):
        return g.transpose(1, 0, 2).reshape(g.shape[1], -1)

    def rows(g):
        return g.reshape(-1, g.shape[2])

    wgu1 = jnp.concatenate([cols(got['ffn1_w_gate']), cols(got['ffn1_w_up'])], axis=1)
    wgu2 = jnp.concatenate([cols(got['ffn2_w_gate']), cols(got['ffn2_w_up'])], axis=1)
    wd1, wd2 = rows(got['ffn1_w_down']), rows(got['ffn2_w_down'])
    w_in = cols(got['w_in'])
    sp = [0, 512, 1024, 1536, 1544, 2056, 2568, 3080, 3592, 3596, 3600]
    fq, fk, fv, ff, gq, gk, gv, gz, gb, ga = [w_in[:, sp[i]:sp[i + 1]] for i in range(10)]
    w_proj = jnp.concatenate([fq, fk, fv, gq, gk, gv, gz, ff, gb, ga,
                              jnp.zeros((D_MODEL, PROJ_W - 3584 - 16), BF)], axis=1)
    w_out, w_q, w_o = rows(got['w_out']), rows(got['mem_w_q']), rows(got['mem_w_o'])
    w_kv = cols(got['mem_w_kv'])
    conv_w8 = cols(got['gdn_conv_w'])

    bias_row = _pad_to(small['fox_f_bias'], (1, 128))
    gate_prm = _pad_to(jnp.concatenate([_pad_to(small['gdn_a_log'], (1, 128 - SMALL_A)),
                                        _pad_to(small['gdn_dt_bias'], (1, 128 - SMALL_A))], axis=0),
                       (8, 128 - SMALL_A))
    gate_prm = jnp.pad(gate_prm, ((0, 0), (SMALL_A, 0)))
    onorm = small['gdn_out_norm']

    sv1 = _ffn_fwd(x, small['ffn1_pre_norm'], wgu1, wd1, "ffn1")
    h1, = rowcall(lambda h, f, g: h + _half_rms(f, g), [x, sv1[3]], [small['ffn1_post_norm']], [(D_MODEL, F32)],
                  rows=512, total=s, name="ffn1_out")

    u2, = rowcall(_rms, [h1], [small['mix_pre_norm']], [(D_MODEL, BF)], rows=512, total=s, name="mix_pre")
    proj = mm(u2, w_proj, name="mix_proj")
    f_cum = fox_f_fwd(proj, bias_row)
    f_heads = f_cum[:, :FOX_HEADS].T
    qh, kh, vh = [_to_heads(proj[:, i * 512:(i + 1) * 512]) for i in range(3)]
    zero_bias = jnp.zeros_like(f_heads)
    qa = _fox_operand(qh * FOX_SCALE, f_heads, False)
    ka = _fox_operand(kh, -f_heads, True)
    va = _fox_operand(vh, zero_bias, True)
    fox_o, lse = fox_fwd(qa, ka, va)
    fox_flat = _from_heads(fox_o[:, :, :FOX_DH])
    cqkv = conv_fwd(proj, conv_w8)
    g_l, b_l = rowcall(_gdn_gates, [(proj, 128, SMALL_BLOCK128)], [gate_prm], [(512, F32), (512, F32)],
                       rows=512, total=s, name="gdn_gates")
    gbb = jnp.concatenate([g_l, b_l], axis=1)
    gdn_o, states = gdn_fwd(cqkv, proj, gbb, onorm)
    mixed = jnp.concatenate([fox_flat, gdn_o], axis=1).astype(BF)
    mo = mm(mixed, w_out, name="mix_out")
    h2, = rowcall(_residual_rms, [h1, mo], [small['mix_post_norm']], [(D_MODEL, F32)], rows=512, total=s,
                  name="mix_res")

    hq, = rowcall(_rms, [h2], [small['mem_pre_norm']], [(D_MODEL, BF)], rows=512, total=s, name="mem_pre")
    mn, = rowcall(_rms, [mem], [small['mem_kv_norm']], [(D_MODEL, BF)], rows=256, total=mem.shape[0], name="mem_kvn")
    q_mem = mm(hq, w_q, name="mem_q")
    kv_mem = mm(mn, w_kv, name="mem_kv")
    o_mem = xattn_fwd(q_mem, kv_mem)
    c_mem = mm(o_mem, w_o, name="mem_o")
    h3, = rowcall(_residual_rms, [h2, c_mem], [small['mem_post_norm']], [(D_MODEL, F32)], rows=512, total=s,
                  name="mem_res")

    sv2 = _ffn_fwd(h3, small['ffn2_pre_norm'], wgu2, wd2, "ffn2")

    def b_loss(h, f, tgt, g):
        err = h + _half_rms(f, g) - tgt
        part = 0.5 * jnp.sum(jnp.mean(err * err, axis=-1, keepdims=True), axis=0, keepdims=True)
        return err * (1.0 / D_MODEL), jnp.broadcast_to(part, (1, 128))

    dy, loss_acc = rowcall(b_loss, [h3, sv2[3], a['loss_target'][0]], [small['ffn2_post_norm']], [(D_MODEL, F32)],
                           [(1, 128)], rows=512, total=s, name="loss")

    grads = {}
    dh3, dwgu2, dwd2, grads['ffn2_pre_norm'], grads['ffn2_post_norm'] = _ffn_bwd(
        dy, h3, small['ffn2_pre_norm'], small['ffn2_post_norm'], wgu2, wd2, sv2, "ffn2")

    dc, grads['mem_post_norm'] = rowcall(_bwd_residual, [dh3, c_mem], [small['mem_post_norm']], [(D_MODEL, BF)],
                                         [(1, D_MODEL)], rows=512, total=s, name="mem_bwd_res")
    d_o = mm(dc, w_o, tb=True, name="mem_bwd_do")
    dw_o = mm(o_mem, dc, ta=True, name="mem_bwd_dwo")
    dq_mem, dkv = xattn_bwd(q_mem, kv_mem, d_o)
    dhq = mm(dq_mem, w_q, tb=True, name="mem_bwd_dhq")
    dw_q = mm(hq, dq_mem, ta=True, name="mem_bwd_dwq")
    dmn = mm(dkv, w_kv, tb=True, name="mem_bwd_dmn")
    dw_kv = mm(mn, dkv, ta=True, name="mem_bwd_dwkv")
    _, grads['mem_kv_norm'] = rowcall(lambda d, mv, g: jax.vjp(_rms, mv, g)[1](d), [dmn, mem],
                                      [small['mem_kv_norm']], [(D_MODEL, F32)], [(1, D_MODEL)], rows=256,
                                      total=mem.shape[0], name="mem_bwd_kvn")

    def b_pre(dh, duv, hv, pg):
        dx, dpre = jax.vjp(_rms, hv, pg)[1](duv)
        return dh + dx, dpre

    dh2, grads['mem_pre_norm'] = rowcall(b_pre, [dh3, dhq, h2], [small['mem_pre_norm']], [(D_MODEL, F32)],
                                         [(1, D_MODEL)], rows=512, total=s, name="mem_bwd_pre")

    dmo, grads['mix_post_norm'] = rowcall(_bwd_residual, [dh2, mo], [small['mix_post_norm']], [(D_MODEL, BF)],
                                          [(1, D_MODEL)], rows=512, total=s, name="mix_bwd_res")
    d_mixed = mm(dmo, w_out, tb=True, name="mix_bwd_dmixed")
    dw_out = mm(mixed, dmo, ta=True, name="mix_bwd_dwout")
    def b_delta(do, o):
        sel = (_iota2((512, 128), 0) // FOX_DH == _iota2((512, 128), 1)).astype(F32)
        return hdot(do * o, sel)

    delta, = rowcall(b_delta, [(d_mixed, 512, 0), fox_flat], [], [(128, F32)], rows=512, total=s, name="fox_delta")
    qb = _fox_operand(qh * FOX_SCALE, f_heads - lse[:, :, 0], False)
    dob = _fox_operand(_to_heads(d_mixed[:, :512]), -delta[:, :FOX_HEADS].T, False)
    dqa, dka, dva = fox_bwd(ka, va, qb, dob)
    dqh, dkh, dvh = dqa[:, :, :FOX_DH] * FOX_SCALE, dka[:, :, :FOX_DH], dva[:, :, :FOX_DH]
    d_f = _pad_to((dqa[:, :, FOX_DH] - dka[:, :, FOX_DH + 3]).T, (s, 128))
    dsmall_f, dbias = fox_f_bwd(proj, bias_row, d_f)
    grads['fox_f_bias'] = dbias[:, :FOX_HEADS]
    dcqkv, dz, dgb, grads['gdn_out_norm'] = gdn_bwd(cqkv, proj, gbb, onorm, states, d_mixed)

    def b_gates(sm, dsf, dg, db, prm):
        dsm, dprm = jax.vjp(_gdn_gates, sm, prm)[1]((dg, db))
        return dsm + dsf, dprm

    dsmall, dprm = rowcall(b_gates, [(proj, 128, SMALL_BLOCK128), dsmall_f, (dgb, 512, 0), (dgb, 512, 1)], [gate_prm],
                           [(128, F32)],
                           [(8, 128)], rows=512, total=s, name="gdn_bwd_gates")
    grads['gdn_a_log'] = dprm[0:1, SMALL_A:SMALL_A + GDN_HEADS]
    grads['gdn_dt_bias'] = dprm[1:2, SMALL_A:SMALL_A + GDN_HEADS]
    dqkv_pre, dconv8 = conv_bwd(proj, conv_w8, dcqkv)
    dproj = jnp.concatenate([_from_heads(dqh), _from_heads(dkh), _from_heads(dvh), dqkv_pre, dz, dsmall,
                             jnp.zeros((s, PROJ_W - 3584 - 128), F32)], axis=1).astype(BF)
    du2 = mm(dproj, w_proj, tb=True, name="mix_bwd_du")
    dw_proj = mm(u2, dproj, ta=True, name="mix_bwd_dwproj")
    dh1, grads['mix_pre_norm'] = rowcall(b_pre, [dh2, du2, h1], [small['mix_pre_norm']], [(D_MODEL, F32)],
                                         [(1, D_MODEL)], rows=512, total=s, name="mix_bwd_pre")

    grad_x, dwgu1, dwd1, grads['ffn1_pre_norm'], grads['ffn1_post_norm'] = _ffn_bwd(
        dh1, x, small['ffn1_pre_norm'], small['ffn1_post_norm'], wgu1, wd1, sv1, "ffn1")

    def ff_cols_shards(dw):
        return dw.reshape(D_MODEL, N_DEV, FF_SHARD_PAD)[:, :, :FF_SHARD].transpose(1, 0, 2)

    def ff_rows_shards(dw):
        return dw.reshape(N_DEV, FF_SHARD_PAD, D_MODEL)[:, :FF_SHARD]

    def col_shards(dw):
        return dw.reshape(dw.shape[0], N_DEV, -1).transpose(1, 0, 2)

    def row_shards(dw):
        return dw.reshape(N_DEV, -1, dw.shape[1])

    dw_in = jnp.concatenate([dw_proj[:, :1536], dw_proj[:, 3584:3592], dw_proj[:, 1536:3584],
                             dw_proj[:, 3592:3600]], axis=1)
    part = {'ffn1_w_gate': ff_cols_shards(dwgu1[:, :D_FF_PAD]), 'ffn1_w_up': ff_cols_shards(dwgu1[:, D_FF_PAD:]),
            'ffn1_w_down': ff_rows_shards(dwd1),
            'ffn2_w_gate': ff_cols_shards(dwgu2[:, :D_FF_PAD]), 'ffn2_w_up': ff_cols_shards(dwgu2[:, D_FF_PAD:]),
            'ffn2_w_down': ff_rows_shards(dwd2),
            'w_in': col_shards(dw_in), 'w_out': row_shards(dw_out), 'mem_w_q': row_shards(dw_q),
            'mem_w_kv': col_shards(dw_kv), 'mem_w_o': row_shards(dw_o)}
    gpack = _pack(lambda nm: grads[nm], conv=dconv8[:CONV_W], loss=loss_acc[:, :1])
    red = exchange([("scatter", part[nm].astype(BF)) for nm in BIG] + [("gather", gpack)], "reduce_grads")
    recv = dict(zip(BIG, red[:-1]))
    gsum_parts = red[-1]

    out_g, out_d, out_m, out_v = {}, {}, {}, {}
    for nm in BIG:
        r = recv[nm]
        res = adamw(w2[nm], m2[nm], v2[nm], [(r, r.shape[2], 0, d) for d in range(N_DEV)], "adamw_" + nm)
        out_g[nm], out_d[nm], out_m[nm], out_v[nm] = res
    wp = _pack(lambda nm: small[nm])
    mp = _pack(lambda nm: m2[nm][None])
    vp = _pack(lambda nm: v2[nm][None])
    pg, pd, pm, pv = adamw(wp, mp, vp, [(gsum_parts, D_MODEL, 0, d) for d in range(N_DEV)], "adamw_small")
    for dst, p in ((out_g, pg), (out_d, pd), (out_m, pm), (out_v, pv)):
        dst.update({k: val[0] for k, val in _unpack(p).items()})
    loss = pg[ROW_MISC, COL_LOSS]
    conv_g = lax.dynamic_slice_in_dim(pg[ROW_CONV:ROW_CONV + 6].reshape(CONV_W, CONV_CH), me * (CONV_CH // N_DEV),
                                      CONV_CH // N_DEV, axis=1)
    res = adamw(w2['gdn_conv_w'], m2['gdn_conv_w'], v2['gdn_conv_w'], [conv_g], "adamw_conv")
    out_g['gdn_conv_w'], out_d['gdn_conv_w'], out_m['gdn_conv_w'], out_v['gdn_conv_w'] = res

    def lead(t):
        return t[None]

    return (loss, grad_x[None], *[lead(out_g[nm]) for nm in WEIGHTS], *[lead(out_d[nm]) for nm in WEIGHTS],
            *[lead(out_m[nm]) for nm in WEIGHTS], *[lead(out_v[nm]) for nm in WEIGHTS])


def kernel(x, mem, ffn1_pre_norm, ffn1_w_gate, ffn1_w_up, ffn1_w_down, ffn1_post_norm, mix_pre_norm, w_in, fox_f_bias, gdn_conv_w, gdn_a_log, gdn_dt_bias, gdn_out_norm, w_out, mix_post_norm, mem_pre_norm, mem_kv_norm, mem_w_q, mem_w_kv, mem_w_o, mem_post_norm, ffn2_pre_norm, ffn2_w_gate, ffn2_w_up, ffn2_w_down, ffn2_post_norm, loss_target, m_ffn1_pre_norm, m_ffn1_w_gate, m_ffn1_w_up, m_ffn1_w_down, m_ffn1_post_norm, m_mix_pre_norm, m_w_in, m_fox_f_bias, m_gdn_conv_w, m_gdn_a_log, m_gdn_dt_bias, m_gdn_out_norm, m_w_out, m_mix_post_norm, m_mem_pre_norm, m_mem_kv_norm, m_mem_w_q, m_mem_w_kv, m_mem_w_o, m_mem_post_norm, m_ffn2_pre_norm, m_ffn2_w_gate, m_ffn2_w_up, m_ffn2_w_down, m_ffn2_post_norm, v_ffn1_pre_norm, v_ffn1_w_gate, v_ffn1_w_up, v_ffn1_w_down, v_ffn1_post_norm, v_mix_pre_norm, v_w_in, v_fox_f_bias, v_gdn_conv_w, v_gdn_a_log, v_gdn_dt_bias, v_gdn_out_norm, v_w_out, v_mix_post_norm, v_mem_pre_norm, v_mem_kv_norm, v_mem_w_q, v_mem_w_kv, v_mem_w_o, v_mem_post_norm, v_ffn2_pre_norm, v_ffn2_w_gate, v_ffn2_w_up, v_ffn2_w_down, v_ffn2_post_norm):
    return _step(dict(locals()))
```

```python
import functools

import jax
import jax.numpy as jnp
from jax import lax
from jax.experimental import pallas as pl
from jax.experimental.pallas import tpu as pltpu

F32 = jnp.float32
BF = jnp.bfloat16
HI = lax.Precision.HIGHEST
MESH = pl.DeviceIdType.MESH

N_DEV = 8
EPS = 1e-6
D_MODEL = 1024
D_FF = 2816
FF_SHARD = D_FF // N_DEV
FF_SHARD_PAD = 384
D_FF_PAD = FF_SHARD_PAD * N_DEV
FOX_HEADS, FOX_DH = 8, 64
GDN_HEADS, GDN_DH = 4, 128
GDN_CHUNK = 64
CONV_W = 4
MEM_HEADS, MEM_DH = 4, 256
IN_W = 3600
IN_SHARD = IN_W // N_DEV
PROJ_W = 4096
SMALL_F, SMALL_B, SMALL_A = 0, 8, 12

ADAM_LR, ADAM_B1, ADAM_B2, ADAM_EPS, ADAM_WD, ADAM_STEP = 0.001, 0.9, 0.999, 1e-08, 0.01, 10

VMEM_LIMIT = 56 * 1024 * 1024


def _params(sem=None):
    return pltpu.CompilerParams(dimension_semantics=sem, vmem_limit_bytes=VMEM_LIMIT)


def _tile(n, pref, unit=128):
    if n <= pref:
        return n
    t = (pref // unit) * unit
    while t > unit and n % t:
        t -= unit
    assert n % t == 0, (n, pref)
    return t


@functools.partial(jax.custom_vjp, nondiff_argnums=(2, 3))
def bdot(a, b, ca, cb):
    return lax.dot_general(a.astype(BF), b.astype(BF), (((ca,), (cb,)), ((), ())), preferred_element_type=F32)


def _bdot_fwd(a, b, ca, cb):
    return bdot(a, b, ca, cb), (a, b)


def _bdot_bwd(ca, cb, res, g):
    a, b = res
    da = bdot(g, b, 1, 1 - cb) if ca == 1 else bdot(b, g, 1 - cb, 1)
    db = bdot(a, g, 1 - ca, 0) if cb == 0 else bdot(g, a, 0, 1 - ca)
    return da, db


bdot.defvjp(_bdot_fwd, _bdot_bwd)


def hdot(a, b):
    return jnp.dot(a, b, precision=HI, preferred_element_type=F32)


def _iota2(shape, dim):
    return lax.broadcasted_iota(jnp.int32, shape, dim)


def _sigmoid(x):
    return 1.0 / (1.0 + jnp.exp(-x))


def _silu(x):
    return x * _sigmoid(x)


def _softplus(x):
    return jnp.maximum(x, 0.0) + jnp.log(1.0 + jnp.exp(-jnp.abs(x)))


def _rms(x, gain):
    return x * lax.rsqrt(jnp.mean(x * x, axis=-1, keepdims=True) + EPS) * gain


def mm(a, b, *, name, ta=False, tb=False, out_dtype=F32, tm=1024, tn=1024, tk=512):
    m, k = (a.shape[1], a.shape[0]) if ta else a.shape
    n = b.shape[0] if tb else b.shape[1]
    assert k == (b.shape[1] if tb else b.shape[0]), (a.shape, b.shape, ta, tb)
    tm, tn, tk = _tile(m, tm), _tile(n, tn), _tile(k, tk)
    nk = k // tk
    dims = (((0 if ta else 1,), (1 if tb else 0,)), ((), ()))

    def kern(a_ref, b_ref, o_ref, *scratch):
        def part():
            return lax.dot_general(a_ref[...].astype(BF), b_ref[...].astype(BF), dims, preferred_element_type=F32)

        if nk == 1:
            o_ref[...] = part().astype(o_ref.dtype)
            return
        acc_ref, = scratch
        kk = pl.program_id(2)

        @pl.when(kk == 0)
        def _():
            acc_ref[...] = part()

        @pl.when(kk > 0)
        def _():
            acc_ref[...] += part()

        @pl.when(kk == nk - 1)
        def _():
            o_ref[...] = acc_ref[...].astype(o_ref.dtype)

    a_spec = pl.BlockSpec((tk, tm), lambda i, j, kk: (kk, i)) if ta else pl.BlockSpec((tm, tk), lambda i, j, kk: (i, kk))
    b_spec = pl.BlockSpec((tn, tk), lambda i, j, kk: (j, kk)) if tb else pl.BlockSpec((tk, tn), lambda i, j, kk: (kk, j))
    return pl.pallas_call(
        kern, name=name, grid=(m // tm, n // tn, nk),
        in_specs=[a_spec, b_spec],
        out_specs=pl.BlockSpec((tm, tn), lambda i, j, kk: (i, j)),
        out_shape=jax.ShapeDtypeStruct((m, n), out_dtype),
        scratch_shapes=[pltpu.VMEM((tm, tn), F32)] if nk > 1 else [],
        compiler_params=_params(("parallel", "parallel", "arbitrary")),
    )(a, b)


def _row_spec(item, rows):
    if not isinstance(item, tuple):
        return item, pl.BlockSpec((rows, item.shape[1]), lambda i: (i, 0))
    if len(item) == 3:
        arr, w, c = item
        return arr, pl.BlockSpec((rows, w), lambda i: (i, c))
    arr, w, c, lead = item
    return arr, pl.BlockSpec((None, rows, w), lambda i: (lead, i, c))


def _whole_spec(item):
    if not isinstance(item, tuple):
        return item, pl.BlockSpec(item.shape, lambda i: (0,) * item.ndim)
    arr, w, c = item
    return arr, pl.BlockSpec((arr.shape[0], w), lambda i: (0, c))


def rowcall(body, tiled, whole, outs, accs=(), *, rows, total, name):
    rows = min(rows, total)
    assert total % rows == 0
    t_arr, t_spec = zip(*[_row_spec(t, rows) for t in tiled])
    w_arr, w_spec = zip(*[_whole_spec(w) for w in whole]) if whole else ((), ())
    nt, nw, no, na = len(t_arr), len(w_arr), len(outs), len(accs)

    def kern(*refs):
        vals = [r[...] for r in refs[:nt + nw]]
        res = body(*vals)
        if not isinstance(res, (tuple, list)):
            res = (res,)
        assert len(res) == no + na, (name, len(res), no, na)
        for r, v in zip(refs[nt + nw:nt + nw + no], res[:no]):
            r[...] = v.astype(r.dtype)
        if na:
            acc_refs = refs[nt + nw + no:]

            @pl.when(pl.program_id(0) == 0)
            def _():
                for r in acc_refs:
                    r[...] = jnp.zeros_like(r)

            for r, v in zip(acc_refs, res[no:]):
                r[...] += v

    out_shape = [jax.ShapeDtypeStruct((total, w), d) for w, d in outs] + [jax.ShapeDtypeStruct(s, F32) for s in accs]
    out_specs = [pl.BlockSpec((rows, w), lambda i: (i, 0)) for w, _ in outs] + \
                [pl.BlockSpec(s, lambda i: (0, 0)) for s in accs]
    res = pl.pallas_call(
        kern, name=name, grid=(total // rows,),
        in_specs=list(t_spec) + list(w_spec), out_specs=out_specs, out_shape=out_shape,
        compiler_params=_params(("arbitrary",) if na else ("parallel",)),
    )(*t_arr, *w_arr)
    return res


def _colsum(x):
    return jnp.sum(x, axis=0, keepdims=True)


def _gdn_chunk(q, k, v, z, gb, bb, state, gain):
    c = GDN_CHUNK
    nh = len(q)
    hs = range(nh)
    r64, c64 = _iota2((c, c), 0), _iota2((c, c), 1)
    incl = r64 >= c64
    strict = r64 > c64
    ltri = incl.astype(F32)
    utri = (r64 <= c64).astype(F32)
    eye = (r64 == c64).astype(F32)
    ones = jnp.ones((c, c), F32)
    pick = (_iota2((GDN_DH, c), 0) == _iota2((GDN_DH, c), 1)).astype(F32)
    last = (_iota2((c, GDN_DH), 0) == c - 1).astype(F32)

    qn = [q[h] * lax.rsqrt(jnp.sum(q[h] * q[h], axis=-1, keepdims=True) + EPS) * (GDN_DH ** -0.5) for h in hs]
    kn = [k[h] * lax.rsqrt(jnp.sum(k[h] * k[h], axis=-1, keepdims=True) + EPS) for h in hs]
    gc = [hdot(ltri, gb[h]) for h in hs]
    g64 = [hdot(gb[h], pick) for h in hs]
    gcol = [hdot(ltri, g64[h]) for h in hs]
    grow = [hdot(ones, g64[h] * utri) for h in hs]
    dec = [jnp.exp(jnp.where(incl, gcol[h] - grow[h], -1e30)) for h in hs]
    kb = [kn[h] * bb[h] for h in hs]
    vb = [v[h] * bb[h] for h in hs]
    kk = [bdot(kb[h], kn[h], 1, 1) for h in hs]
    p = [-jnp.where(strict, kk[h] * dec[h], 0.0) for h in hs]
    tinv = [eye + p[h] for h in hs]
    for _ in range(5):
        p = [hdot(p[h], p[h]) for h in hs]
        tinv = [tinv[h] + hdot(tinv[h], p[h]) for h in hs]
    egc = [jnp.exp(gc[h]) for h in hs]
    u = [hdot(tinv[h], vb[h]) for h in hs]
    w = [hdot(tinv[h], kb[h] * egc[h]) for h in hs]
    attn = [bdot(qn[h], kn[h], 1, 1) * dec[h] for h in hs]
    qd = [qn[h] * egc[h] for h in hs]
    gl = [jnp.sum(gc[h] * last, axis=0, keepdims=True) for h in hs]
    kt = [kn[h] * jnp.exp(gl[h] - gc[h]) for h in hs]
    ws = [bdot(w[h], state[h], 1, 0) for h in hs]
    qs = [bdot(qd[h], state[h], 1, 0) for h in hs]
    v_new = [u[h] - ws[h] for h in hs]
    av = [bdot(attn[h], v_new[h], 1, 0) for h in hs]
    kv = [bdot(kt[h], v_new[h], 0, 0) for h in hs]
    new_state = tuple(state[h] * jnp.exp(gl[h]) + kv[h] for h in hs)
    o = tuple(_rms(qs[h] + av[h], gain) * _silu(z[h]) for h in hs)
    return o, new_state


GDN_ROWS = 512
GDN_W = GDN_HEADS * GDN_DH


def gdn_fwd(cqkv, proj, gbb, gain):
    s = cqkv.shape[0]
    nb, cpb = s // GDN_ROWS, GDN_ROWS // GDN_CHUNK
    h4 = GDN_HEADS

    def kern(qkv_ref, z_ref, gb_ref, gain_ref, o_ref, st_ref, state):
        @pl.when(pl.program_id(0) == 0)
        def _():
            state[...] = jnp.zeros_like(state)

        gain_v = gain_ref[...]

        def step(ci, carry):
            sl = pl.ds(pl.multiple_of(ci * GDN_CHUNK, GDN_CHUNK), GDN_CHUNK)
            ins = []
            for h in range(h4):
                ln = lambda base, h=h: slice(base + h * GDN_DH, base + (h + 1) * GDN_DH)
                ins.append((qkv_ref[sl, ln(0)], qkv_ref[sl, ln(GDN_W)], qkv_ref[sl, ln(2 * GDN_W)], z_ref[sl, ln(0)],
                            gb_ref[sl, ln(0)], gb_ref[sl, ln(GDN_W)], state[h]))
            cols = [tuple(col) for col in zip(*ins)]
            o, new = _gdn_chunk(*cols[:7], gain_v)
            for h in range(h4):
                st_ref[h, ci] = ins[h][6]
                o_ref[sl, h * GDN_DH:(h + 1) * GDN_DH] = o[h]
                state[h] = new[h]
            return carry

        lax.fori_loop(0, cpb, step, 0)

    return pl.pallas_call(
        kern, name="gdn_fwd", grid=(nb,),
        in_specs=[pl.BlockSpec((GDN_ROWS, 3 * GDN_W), lambda i: (i, 0)),
                  pl.BlockSpec((GDN_ROWS, GDN_W), lambda i: (i, 6)),
                  pl.BlockSpec((GDN_ROWS, 2 * GDN_W), lambda i: (i, 0)),
                  pl.BlockSpec((1, GDN_DH), lambda i: (0, 0))],
        out_specs=[pl.BlockSpec((GDN_ROWS, GDN_W), lambda i: (i, 0)),
                   pl.BlockSpec((h4, cpb, GDN_DH, GDN_DH), lambda i: (0, i, 0, 0))],
        out_shape=[jax.ShapeDtypeStruct((s, GDN_W), F32),
                   jax.ShapeDtypeStruct((h4, s // GDN_CHUNK, GDN_DH, GDN_DH), F32)],
        scratch_shapes=[pltpu.VMEM((h4, GDN_DH, GDN_DH), F32)],
        compiler_params=_params(("arbitrary",)),
    )(cqkv, proj, gbb, gain)


def gdn_bwd(cqkv, proj, gbb, gain, states, d_mixed):
    s = cqkv.shape[0]
    nb, cpb = s // GDN_ROWS, GDN_ROWS // GDN_CHUNK
    h4 = GDN_HEADS

    def kern(qkv_ref, z_ref, gb_ref, gain_ref, st_ref, do_ref, dqkv_ref, dz_ref, dgb_ref, dgain_ref, dstate):
        @pl.when(pl.program_id(0) == 0)
        def _():
            dgain_ref[...] = jnp.zeros_like(dgain_ref)
            dstate[...] = jnp.zeros_like(dstate)

        gain_v = gain_ref[...]

        def step(t, carry):
            ci = cpb - 1 - t
            sl = pl.ds(pl.multiple_of(ci * GDN_CHUNK, GDN_CHUNK), GDN_CHUNK)
            prim, cot, dst_in = [], [], []
            for h in range(h4):
                ln = lambda base, h=h: slice(base + h * GDN_DH, base + (h + 1) * GDN_DH)
                prim.append((qkv_ref[sl, ln(0)], qkv_ref[sl, ln(GDN_W)], qkv_ref[sl, ln(2 * GDN_W)], z_ref[sl, ln(0)],
                             gb_ref[sl, ln(0)], gb_ref[sl, ln(GDN_W)], st_ref[h, ci]))
                cot.append(do_ref[sl, ln(0)])
                dst_in.append(dstate[h])
            cols = [tuple(col) for col in zip(*prim)]
            vjp = jax.vjp(_gdn_chunk, *cols, gain_v)[1]
            dq, dk, dv, dz, dg, db, dst, dgn = vjp((tuple(cot), tuple(dst_in)))
            for h in range(h4):
                ln = lambda base, h=h: slice(base + h * GDN_DH, base + (h + 1) * GDN_DH)
                dqkv_ref[sl, ln(0)] = dq[h]
                dqkv_ref[sl, ln(GDN_W)] = dk[h]
                dqkv_ref[sl, ln(2 * GDN_W)] = dv[h]
                dz_ref[sl, ln(0)] = dz[h]
                dgb_ref[sl, ln(0)] = dg[h]
                dgb_ref[sl, ln(GDN_W)] = db[h]
                dstate[h] = dst[h]
            dgain_ref[...] += dgn
            return carry

        lax.fori_loop(0, cpb, step, 0)

    def rev(width, cblock=0):
        return pl.BlockSpec((GDN_ROWS, width), lambda i: (nb - 1 - i, cblock))

    return pl.pallas_call(
        kern, name="gdn_bwd", grid=(nb,),
        in_specs=[rev(3 * GDN_W), rev(GDN_W, 6), rev(2 * GDN_W), pl.BlockSpec((1, GDN_DH), lambda i: (0, 0)),
                  pl.BlockSpec((h4, cpb, GDN_DH, GDN_DH), lambda i: (0, nb - 1 - i, 0, 0)), rev(GDN_W, 1)],
        out_specs=[rev(3 * GDN_W), rev(GDN_W), rev(2 * GDN_W), pl.BlockSpec((1, GDN_DH), lambda i: (0, 0))],
        out_shape=[jax.ShapeDtypeStruct((s, 3 * GDN_W), F32), jax.ShapeDtypeStruct((s, GDN_W), F32),
                   jax.ShapeDtypeStruct((s, 2 * GDN_W), F32), jax.ShapeDtypeStruct((1, GDN_DH), F32)],
        scratch_shapes=[pltpu.VMEM((h4, GDN_DH, GDN_DH), F32)],
        compiler_params=_params(("arbitrary",)),
    )(cqkv, proj, gbb, gain, states, d_mixed)


def _gdn_gates(small, prm):
    w = GDN_HEADS * GDN_DH
    lane, head = _iota2((128, w), 0), _iota2((128, w), 1) // GDN_DH
    sel_b = (lane == SMALL_B + head).astype(F32)
    sel_a = (lane == SMALL_A + head).astype(F32)
    prow = _iota2((8, 128), 0)
    a_log = jnp.sum(prm * (prow == 0).astype(F32), axis=0, keepdims=True)
    dt_b = jnp.sum(prm * (prow == 1).astype(F32), axis=0, keepdims=True)
    beta = _sigmoid(hdot(small, sel_b))
    g = hdot(-jnp.exp(a_log) * _softplus(small + dt_b), sel_a)
    return g, beta


CONV_ROWS = 1024
CONV_COLS = 128
CONV_BLOCK0 = 1536 // CONV_COLS


def _shift_down(prev8, cur, s):
    ext = jnp.concatenate([prev8, cur], axis=0)
    return pltpu.roll(ext, s, 0)[8:]


def _shift_up(cur, next8, s):
    n = cur.shape[0]
    ext = jnp.concatenate([cur, next8], axis=0)
    return pltpu.roll(ext, n + 8 - s, 0)[:n]


def _conv_pre(x_ref, w, ci, nchunk):
    r0 = pl.multiple_of(ci * CONV_ROWS, CONV_ROWS)
    cur = x_ref[pl.ds(r0, CONV_ROWS), :]
    prev = x_ref[pl.ds(pl.multiple_of(jnp.maximum(r0 - 8, 0), 8), 8), :]
    prev = jnp.where(ci > 0, prev, 0.0)
    shifted = [cur] + [_shift_down(prev, cur, s) for s in range(1, CONV_W)]
    pre = w[CONV_W - 1:CONV_W, :] * cur
    for s in range(1, CONV_W):
        pre = pre + w[CONV_W - 1 - s:CONV_W - s, :] * shifted[s]
    return r0, pre, shifted


def conv_fwd(proj, conv_w8):
    s = proj.shape[0]
    nchunk = s // CONV_ROWS
    ncol = 3 * GDN_HEADS * GDN_DH // CONV_COLS

    def kern(x_ref, w_ref, y_ref):
        w = w_ref[...]

        def step(ci, carry):
            r0, pre, _ = _conv_pre(x_ref, w, ci, nchunk)
            y_ref[pl.ds(r0, CONV_ROWS), :] = _silu(pre)
            return carry

        lax.fori_loop(0, nchunk, step, 0)

    return pl.pallas_call(
        kern, name="conv_fwd", grid=(ncol,),
        in_specs=[pl.BlockSpec((s, CONV_COLS), lambda j: (0, CONV_BLOCK0 + j)),
                  pl.BlockSpec((8, CONV_COLS), lambda j: (0, j))],
        out_specs=pl.BlockSpec((s, CONV_COLS), lambda j: (0, j)),
        out_shape=jax.ShapeDtypeStruct((s, ncol * CONV_COLS), F32),
        compiler_params=_params(("parallel",)),
    )(proj, conv_w8)


def conv_bwd(proj, conv_w8, dy):
    s = proj.shape[0]
    nchunk = s // CONV_ROWS
    per = 3 * GDN_HEADS * GDN_DH // CONV_COLS
    outs = []
    for part in range(1):
        def kern(x_ref, w_ref, dy_ref, dx_ref, dw_ref, dpre_ref):
            w = w_ref[...]
            rows8 = _iota2((8, CONV_COLS), 0)

            def step1(ci, dw):
                r0, pre, shifted = _conv_pre(x_ref, w, ci, nchunk)
                sg = _sigmoid(pre)
                dpre = dy_ref[pl.ds(r0, CONV_ROWS), :] * sg * (1.0 + pre * (1.0 - sg))
                dpre_ref[pl.ds(r0, CONV_ROWS), :] = dpre
                for sh in range(CONV_W):
                    dw = dw + jnp.where(rows8 == CONV_W - 1 - sh, _colsum(dpre * shifted[sh]), 0.0)
                return dw

            dw_ref[...] = lax.fori_loop(0, nchunk, step1, jnp.zeros((8, CONV_COLS), F32))

            def step2(ci, carry):
                r0 = pl.multiple_of(ci * CONV_ROWS, CONV_ROWS)
                cur = dpre_ref[pl.ds(r0, CONV_ROWS), :]
                nxt = dpre_ref[pl.ds(pl.multiple_of(jnp.minimum(r0 + CONV_ROWS, s - 8), 8), 8), :]
                nxt = jnp.where(ci < nchunk - 1, nxt, 0.0)
                dx = w[CONV_W - 1:CONV_W, :] * cur
                for sh in range(1, CONV_W):
                    dx = dx + w[CONV_W - 1 - sh:CONV_W - sh, :] * _shift_up(cur, nxt, sh)
                dx_ref[pl.ds(r0, CONV_ROWS), :] = dx
                return carry

            lax.fori_loop(0, nchunk, step2, 0)

        outs.append(pl.pallas_call(
            kern, name=f"conv_bwd{part}", grid=(per,),
            in_specs=[pl.BlockSpec((s, CONV_COLS), lambda j, part=part: (0, CONV_BLOCK0 + part * per + j)),
                      pl.BlockSpec((8, CONV_COLS), lambda j, part=part: (0, part * per + j)),
                      pl.BlockSpec((s, CONV_COLS), lambda j: (0, j))],
            out_specs=[pl.BlockSpec((s, CONV_COLS), lambda j: (0, j)),
                       pl.BlockSpec((8, CONV_COLS), lambda j: (0, j))],
            out_shape=[jax.ShapeDtypeStruct((s, per * CONV_COLS), F32),
                       jax.ShapeDtypeStruct((8, per * CONV_COLS), F32)],
            scratch_shapes=[pltpu.VMEM((s, CONV_COLS), F32)],
            compiler_params=_params(("parallel",)),
        )(proj, conv_w8, dy))
    dx = jnp.concatenate([o[0] for o in outs], axis=1)
    dw = jnp.concatenate([o[1] for o in outs], axis=1)
    return dx, dw


FOXF_ROWS = 512
SMALL_BLOCK128 = 3584 // 128


def _log_sigmoid(x):
    return jnp.minimum(x, 0.0) - jnp.log(1.0 + jnp.exp(-jnp.abs(x)))


def fox_f_fwd(proj, bias_row):
    s = proj.shape[0]
    n = s // FOXF_ROWS

    def kern(x_ref, b_ref, f_ref, carry):
        @pl.when(pl.program_id(0) == 0)
        def _():
            carry[...] = jnp.zeros_like(carry)

        heads = _iota2((FOXF_ROWS, 128), 1) < FOX_HEADS
        lf = jnp.where(heads, _log_sigmoid(x_ref[...] + b_ref[...]), 0.0)
        ltri = (_iota2((FOXF_ROWS, FOXF_ROWS), 0) >= _iota2((FOXF_ROWS, FOXF_ROWS), 1)).astype(F32)
        c = hdot(ltri, lf) + carry[...]
        f_ref[...] = c
        carry[...] = c[FOXF_ROWS - 1:FOXF_ROWS, :]

    return pl.pallas_call(
        kern, name="fox_f_fwd", grid=(n,),
        in_specs=[pl.BlockSpec((FOXF_ROWS, 128), lambda i: (i, SMALL_BLOCK128)),
                  pl.BlockSpec((1, 128), lambda i: (0, 0))],
        out_specs=pl.BlockSpec((FOXF_ROWS, 128), lambda i: (i, 0)),
        out_shape=jax.ShapeDtypeStruct((s, 128), F32),
        scratch_shapes=[pltpu.VMEM((1, 128), F32)],
        compiler_params=_params(("arbitrary",)),
    )(proj, bias_row)


def fox_f_bwd(proj, bias_row, d_f):
    s = proj.shape[0]
    n = s // FOXF_ROWS

    def kern(x_ref, b_ref, df_ref, dx_ref, db_ref, carry):
        @pl.when(pl.program_id(0) == 0)
        def _():
            carry[...] = jnp.zeros_like(carry)
            db_ref[...] = jnp.zeros_like(db_ref)

        heads = _iota2((FOXF_ROWS, 128), 1) < FOX_HEADS
        utri = (_iota2((FOXF_ROWS, FOXF_ROWS), 0) <= _iota2((FOXF_ROWS, FOXF_ROWS), 1)).astype(F32)
        rc = hdot(utri, df_ref[...]) + carry[...]
        carry[...] = rc[0:1, :]
        dx = jnp.where(heads, rc * _sigmoid(-(x_ref[...] + b_ref[...])), 0.0)
        dx_ref[...] = dx
        db_ref[...] += _colsum(dx)

    return pl.pallas_call(
        kern, name="fox_f_bwd", grid=(n,),
        in_specs=[pl.BlockSpec((FOXF_ROWS, 128), lambda i: (n - 1 - i, SMALL_BLOCK128)),
                  pl.BlockSpec((1, 128), lambda i: (0, 0)),
                  pl.BlockSpec((FOXF_ROWS, 128), lambda i: (n - 1 - i, 0))],
        out_specs=[pl.BlockSpec((FOXF_ROWS, 128), lambda i: (n - 1 - i, 0)),
                   pl.BlockSpec((1, 128), lambda i: (0, 0))],
        out_shape=[jax.ShapeDtypeStruct((s, 128), F32), jax.ShapeDtypeStruct((1, 128), F32)],
        scratch_shapes=[pltpu.VMEM((1, 128), F32)],
        compiler_params=_params(("arbitrary",)),
    )(proj, bias_row, d_f)


FOX_T = 512
FOX_SCALE = FOX_DH ** -0.5
FOX_PAIRS = FOX_HEADS // 2
XL = 8
NEG = -1e30
_NT = (((1,), (1,)), ((), ()))


def _split3(x):
    def bf(v):
        return lax.reduce_precision(v, exponent_bits=8, mantissa_bits=7)

    hi = bf(x)
    mid = bf(x - hi)
    lo = bf(x - hi - mid)
    return jnp.stack([hi, mid, lo], axis=-1)


def _fox_extras(s, first, second):
    def part(v):
        if v is None:
            return jnp.zeros((s, FOX_HEADS, 3), F32)
        if isinstance(v, float):
            return jnp.full((s, FOX_HEADS, 3), v, F32)
        return _split3(v)

    cols = jnp.concatenate([part(first), part(second), jnp.zeros((s, FOX_HEADS, XL - 6), F32)], axis=-1)
    cols = _pad_to(cols.reshape(s, FOX_PAIRS, 2 * XL), (s, FOX_PAIRS, 128))
    return cols.transpose(1, 0, 2).astype(BF)


def _pair_masks(rows):
    lane = _iota2((rows, 256), 1)
    head = jnp.where(lane < 128, lane // FOX_DH, (lane - 128) // XL)
    return head == 0, head == 1


def fox_fwd(qkv, xq, xk, xv):
    s = qkv.shape[0]
    t = min(FOX_T, s)
    n = s // t

    def kern(q_ref, k_ref, v_ref, xq_ref, xk_ref, xv_ref, o_ref, lse_ref):
        i = pl.program_id(1)
        masks = _pair_masks(t)
        q_all = jnp.concatenate([q_ref[...] * FOX_SCALE, xq_ref[...]], axis=1)
        q_ops = [jnp.where(mk, q_all, 0).astype(BF) for mk in masks]

        def step(j, carry, masked):
            sl = pl.ds(pl.multiple_of(j * t, t), t)
            k_op = jnp.concatenate([k_ref[sl, :], xk_ref[sl, :]], axis=1)
            v_op = jnp.concatenate([v_ref[sl, :], xv_ref[sl, :]], axis=1)
            sc = [lax.dot_general(q_ops[e], k_op, _NT, preferred_element_type=F32) for e in range(2)]
            if masked:
                keep = _iota2((t, t), 0) >= _iota2((t, t), 1)
                sc = [jnp.where(keep, x, NEG) for x in sc]
            m_new = [jnp.maximum(carry[e][0], jnp.max(sc[e], axis=1, keepdims=True)) for e in range(2)]
            p = [jnp.exp(sc[e] - m_new[e]).astype(BF) for e in range(2)]
            pv = [jnp.dot(p[e], v_op, preferred_element_type=F32) for e in range(2)]
            return tuple((m_new[e], jnp.exp(carry[e][0] - m_new[e]) * carry[e][1] + pv[e]) for e in range(2))

        init = tuple((jnp.full((t, 1), NEG, F32), jnp.zeros((t, 256), F32)) for _ in range(2))
        carry = lax.fori_loop(0, i, lambda j, c: step(j, c, False), init)
        carry = step(i, carry, True)
        lane = _iota2((t, 256), 1)
        outs, lses = [], []
        for e in range(2):
            m, acc = carry[e]
            l = jnp.sum(jnp.where(lane == 128, acc, 0.0), axis=1, keepdims=True)
            outs.append(acc[:, :128] / l)
            lses.append(m + jnp.log(l))
        lane128 = _iota2((t, 128), 1)
        o_ref[...] = jnp.where(lane128 < FOX_DH, outs[0], outs[1])
        lse_ref[...] = jnp.where(lane128 == 0, lses[0], jnp.where(lane128 == 1, lses[1], 0.0))

    pr = FOX_PAIRS
    return pl.pallas_call(
        kern, name="fox_fwd", grid=(pr, n),
        in_specs=[pl.BlockSpec((t, 128), lambda p, i: (i, p)),
                  pl.BlockSpec((s, 128), lambda p, i: (0, pr + p)),
                  pl.BlockSpec((s, 128), lambda p, i: (0, 2 * pr + p)),
                  pl.BlockSpec((None, t, 128), lambda p, i: (p, i, 0)),
                  pl.BlockSpec((None, s, 128), lambda p, i: (p, 0, 0)),
                  pl.BlockSpec((None, s, 128), lambda p, i: (p, 0, 0))],
        out_specs=[pl.BlockSpec((t, 128), lambda p, i: (i, p)),
                   pl.BlockSpec((None, t, 128), lambda p, i: (p, i, 0))],
        out_shape=[jax.ShapeDtypeStruct((s, FOX_HEADS * FOX_DH), F32), jax.ShapeDtypeStruct((pr, s, 128), F32)],
        compiler_params=_params(("parallel", "parallel")),
    )(qkv, qkv, qkv, xq, xk, xv)


def fox_bwd(qkv, d_o, xk, xv, xqb, xdo):
    s = qkv.shape[0]
    t = min(FOX_T, s)
    n = s // t

    def kern(k_ref, v_ref, xk_ref, xv_ref, q_ref, do_ref, xq_ref, xd_ref, dq_ref, dk_ref, dv_ref):
        j = pl.program_id(1)

        @pl.when(j == 0)
        def _():
            dq_ref[...] = jnp.zeros_like(dq_ref)

        masks = _pair_masks(t)
        k_op = jnp.concatenate([k_ref[...], xk_ref[...]], axis=1)
        v_op = jnp.concatenate([v_ref[...], xv_ref[...]], axis=1)
        k_ops = [jnp.where(mk, k_op, 0).astype(BF) for mk in masks]

        def step(i, carry, masked):
            dk, dv = carry
            sl = pl.ds(pl.multiple_of(i * t, t), t)
            q_all = jnp.concatenate([q_ref[sl, :] * FOX_SCALE, xq_ref[sl, :]], axis=1)
            do_all = jnp.concatenate([do_ref[sl, :], xd_ref[sl, :]], axis=1)
            q_ops = [jnp.where(mk, q_all, 0).astype(BF) for mk in masks]
            do_ops = [jnp.where(mk, do_all, 0).astype(BF) for mk in masks]
            st = [lax.dot_general(k_op, q_ops[e], _NT, preferred_element_type=F32) for e in range(2)]
            dp = [lax.dot_general(v_op, do_ops[e], _NT, preferred_element_type=F32) for e in range(2)]
            if masked:
                keep = _iota2((t, t), 0) <= _iota2((t, t), 1)
                st = [jnp.where(keep, x, NEG) for x in st]
            pt = [jnp.exp(x) for x in st]
            dsb = [(pt[e] * dp[e]).astype(BF) for e in range(2)]
            for e in range(2):
                dv = dv + jnp.dot(pt[e].astype(BF), do_ops[e][:, :128], preferred_element_type=F32)
                dk = dk + jnp.dot(dsb[e], q_ops[e], preferred_element_type=F32)
            dq_ref[sl, :] += sum(lax.dot_general(dsb[e], k_ops[e], (((0,), (0,)), ((), ())),
                                                 preferred_element_type=F32) for e in range(2))
            return dk, dv

        init = (jnp.zeros((t, 256), F32), jnp.zeros((t, 128), F32))
        carry = step(j, init, True)
        dk, dv = lax.fori_loop(j + 1, n, lambda i, c: step(i, c, False), carry)
        dk_ref[...] = dk
        dv_ref[...] = dv

    pr = FOX_PAIRS
    return pl.pallas_call(
        kern, name="fox_bwd", grid=(pr, n),
        in_specs=[pl.BlockSpec((t, 128), lambda p, j: (j, pr + p)),
                  pl.BlockSpec((t, 128), lambda p, j: (j, 2 * pr + p)),
                  pl.BlockSpec((None, t, 128), lambda p, j: (p, j, 0)),
                  pl.BlockSpec((None, t, 128), lambda p, j: (p, j, 0)),
                  pl.BlockSpec((s, 128), lambda p, j: (0, p)),
                  pl.BlockSpec((s, 128), lambda p, j: (0, p)),
                  pl.BlockSpec((None, s, 128), lambda p, j: (p, 0, 0)),
                  pl.BlockSpec((None, s, 128), lambda p, j: (p, 0, 0))],
        out_specs=[pl.BlockSpec((s, 256), lambda p, j: (0, p)),
                   pl.BlockSpec((t, 256), lambda p, j: (j, p)),
                   pl.BlockSpec((t, 128), lambda p, j: (j, p))],
        out_shape=[jax.ShapeDtypeStruct((s, pr * 256), F32), jax.ShapeDtypeStruct((s, pr * 256), F32),
                   jax.ShapeDtypeStruct((s, FOX_HEADS * FOX_DH), F32)],
        compiler_params=_params(("parallel", "arbitrary")),
    )(qkv, qkv, xk, xv, qkv, d_o, xqb, xdo)


def _xattn_head(q, k, v):
    sc = bdot(q, k, 1, 1) * (MEM_DH ** -0.5)
    e = jnp.exp(sc - lax.stop_gradient(jnp.max(sc, axis=-1, keepdims=True)))
    p = e / jnp.sum(e, axis=-1, keepdims=True)
    return bdot(p, v, 1, 0)


def xattn_fwd(q, kv):
    s = q.shape[0]
    hh = MEM_HEADS

    def body(*vals):
        qs, ks, vs = vals[:hh], vals[hh:2 * hh], vals[2 * hh:]
        return jnp.concatenate([_xattn_head(qs[a], ks[a], vs[a]) for a in range(hh)], axis=1)

    return rowcall(body, [(q, MEM_DH, a) for a in range(hh)],
                   [(kv, MEM_DH, a) for a in range(2 * hh)],
                   [(hh * MEM_DH, BF)], rows=512, total=s, name="xattn_fwd")[0]


def xattn_bwd(q, kv, d_o):
    s = q.shape[0]
    hh = MEM_HEADS

    def body(*vals):
        qs, dos = vals[:hh], vals[hh:2 * hh]
        ks, vs = vals[2 * hh:3 * hh], vals[3 * hh:]
        dqs, dks, dvs = [], [], []
        for a in range(hh):
            _, vjp = jax.vjp(_xattn_head, qs[a], ks[a], vs[a])
            dq, dk, dv = vjp(dos[a])
            dqs.append(dq)
            dks.append(dk)
            dvs.append(dv)
        return jnp.concatenate(dqs, axis=1), jnp.concatenate(dks + dvs, axis=1)

    return rowcall(body, [(q, MEM_DH, a) for a in range(hh)] + [(d_o, MEM_DH, a) for a in range(hh)],
                   [(kv, MEM_DH, a) for a in range(2 * hh)],
                   [(hh * MEM_DH, BF)], [kv.shape], rows=512, total=s, name="xattn_bwd")


def exchange(items, name):
    n = len(items)
    npeer = N_DEV - 1

    def body(*refs):
        ins, outs = refs[:n], refs[n:2 * n]
        send, recv, loc = refs[2 * n:]
        x, y, c = lax.axis_index("x"), lax.axis_index("y"), lax.axis_index("c")
        me = 4 * x + 2 * y + c

        def peer(p):
            px = 1 - x if p & 4 else x
            py = 1 - y if p & 2 else y
            pc = 1 - c if p & 1 else c
            return (px, py, pc), 4 * px + 2 * py + pc

        local, remote = [], []
        for w, (mode, _) in enumerate(items):
            own = ins[w] if mode == "gather" else ins[w].at[me]
            cp = pltpu.make_async_copy(own, outs[w].at[me], loc.at[w])
            cp.start()
            local.append(cp)
        for p in range(1, N_DEV):
            dev, idx = peer(p)
            for w, (mode, _) in enumerate(items):
                src = ins[w] if mode == "gather" else ins[w].at[idx]
                k = w * npeer + p - 1
                out_cp = pltpu.make_async_remote_copy(src_ref=src, dst_ref=outs[w].at[me], send_sem=send.at[k],
                                                      recv_sem=recv.at[k], device_id=dev, device_id_type=MESH)
                out_cp.start()
                in_cp = pltpu.make_async_remote_copy(src_ref=src, dst_ref=outs[w].at[idx], send_sem=send.at[k],
                                                     recv_sem=recv.at[k], device_id=dev, device_id_type=MESH)
                remote.append((out_cp, in_cp))
        for out_cp, in_cp in remote:
            in_cp.wait_recv()
            out_cp.wait_send()
        for cp in local:
            cp.wait()

    out_shape = []
    for mode, a in items:
        shp = (N_DEV,) + tuple(a.shape) if mode == "gather" else tuple(a.shape)
        out_shape.append(jax.ShapeDtypeStruct(shp, a.dtype))
    hbm = pl.BlockSpec(memory_space=pl.ANY)
    return pl.pallas_call(
        body, name=name, in_specs=[hbm] * n, out_specs=[hbm] * n, out_shape=out_shape,
        scratch_shapes=[pltpu.SemaphoreType.DMA((n * npeer,)), pltpu.SemaphoreType.DMA((n * npeer,)),
                        pltpu.SemaphoreType.DMA((n,))],
        compiler_params=pltpu.CompilerParams(has_side_effects=True),
    )(*[a for _, a in items])


def adamw(w, m, v, contribs, name):
    r, c = w.shape
    nc = len(contribs)
    rows = next((r // d for d in (4, 2) if r % d == 0 and (r // d) % 16 == 0), r)
    c1, c2 = 1.0 - ADAM_B1 ** ADAM_STEP, 1.0 - ADAM_B2 ** ADAM_STEP

    def body(wv, mv, vv, *gs):
        g = gs[0].astype(F32)
        for extra in gs[1:]:
            g = g + extra.astype(F32)
        m_new = ADAM_B1 * mv + (1.0 - ADAM_B1) * g
        v_new = ADAM_B2 * vv + (1.0 - ADAM_B2) * (g * g)
        delta = -ADAM_LR * ((m_new / c1) / (jnp.sqrt(v_new / c2) + ADAM_EPS) + ADAM_WD * wv)
        return g, delta, m_new, v_new

    assert nc >= 1
    return rowcall(body, [w, m, v] + list(contribs), [], [(c, F32)] * 4, rows=rows, total=r, name=name)


WEIGHTS = ['ffn1_pre_norm', 'ffn1_w_gate', 'ffn1_w_up', 'ffn1_w_down', 'ffn1_post_norm', 'mix_pre_norm', 'w_in',
           'fox_f_bias', 'gdn_conv_w', 'gdn_a_log', 'gdn_dt_bias', 'gdn_out_norm', 'w_out', 'mix_post_norm',
           'mem_pre_norm', 'mem_kv_norm', 'mem_w_q', 'mem_w_kv', 'mem_w_o', 'mem_post_norm', 'ffn2_pre_norm',
           'ffn2_w_gate', 'ffn2_w_up', 'ffn2_w_down', 'ffn2_post_norm']
GAINS = ['ffn1_pre_norm', 'ffn1_post_norm', 'mix_pre_norm', 'mix_post_norm', 'mem_pre_norm', 'mem_kv_norm',
         'mem_post_norm', 'ffn2_pre_norm', 'ffn2_post_norm']
BIG = ['ffn1_w_gate', 'ffn1_w_up', 'ffn1_w_down', 'w_in', 'w_out', 'mem_w_q', 'mem_w_kv', 'mem_w_o',
       'ffn2_w_gate', 'ffn2_w_up', 'ffn2_w_down']
PACK_ROWS = 24
ROW_MISC = len(GAINS)
ROW_CONV = ROW_MISC + 1
COL_FBIAS, COL_ALOG, COL_DTB, COL_ONORM, COL_LOSS = 0, 8, 12, 128, 256
CONV_CH = 3 * GDN_HEADS * GDN_DH


def _pad_to(a, shape):
    return jnp.pad(a, [(0, t - s) for s, t in zip(a.shape, shape)])


def _pack(get, conv=None, loss=None):
    rows = [get(nm) for nm in GAINS]
    misc = jnp.concatenate([get('fox_f_bias'), get('gdn_a_log'), get('gdn_dt_bias'),
                            jnp.zeros((1, COL_ONORM - COL_DTB - 4), F32), get('gdn_out_norm'),
                            jnp.zeros((1, 1), F32) if loss is None else loss.reshape(1, 1)], axis=1)
    rows.append(_pad_to(misc, (1, D_MODEL)))
    rows.append(jnp.zeros((6, D_MODEL), F32) if conv is None else conv.reshape(6, D_MODEL))
    return _pad_to(jnp.concatenate(rows, axis=0), (PACK_ROWS, D_MODEL))


def _unpack(p):
    out = {nm: p[i:i + 1] for i, nm in enumerate(GAINS)}
    misc = p[ROW_MISC:ROW_MISC + 1]
    out['fox_f_bias'] = misc[:, COL_FBIAS:COL_FBIAS + FOX_HEADS]
    out['gdn_a_log'] = misc[:, COL_ALOG:COL_ALOG + GDN_HEADS]
    out['gdn_dt_bias'] = misc[:, COL_DTB:COL_DTB + GDN_HEADS]
    out['gdn_out_norm'] = misc[:, COL_ONORM:COL_ONORM + GDN_DH]
    return out


def _ffn_fwd(h, pre, wgu, wd, tag):
    s = h.shape[0]
    u, = rowcall(_rms, [h], [pre], [(D_MODEL, BF)], rows=512, total=s, name=tag + "_pre")
    gu = mm(u, wgu, name=tag + "_gate_up")
    act, = rowcall(lambda a, b: _silu(a) * b, [(gu, D_FF_PAD, 0), (gu, D_FF_PAD, 1)], [], [(D_FF_PAD, BF)],
                   rows=256, total=s, name=tag + "_act")
    f = mm(act, wd, name=tag + "_down")
    return u, gu, act, f


def _half_rms(a, g):
    return 0.5 * _rms(a, g)


def _ffn_bwd(dh_out, h, pre, post, wgu, wd, saved, tag):
    u, gu, act, f = saved
    s = h.shape[0]

    def b_post(dh, fv, pg):
        return jax.vjp(_half_rms, fv, pg)[1](dh)

    df, dpost = rowcall(b_post, [dh_out, f], [post], [(D_MODEL, BF)], [(1, D_MODEL)], rows=512, total=s,
                        name=tag + "_bwd_post")
    dact = mm(df, wd, tb=True, name=tag + "_bwd_dact")
    dwd = mm(act, df, ta=True, name=tag + "_bwd_dwd")

    def b_act(a, b, da):
        dg, du = jax.vjp(lambda g_, u_: _silu(g_) * u_, a, b)[1](da)
        return jnp.concatenate([dg, du], axis=1)

    dgu, = rowcall(b_act, [(gu, D_FF_PAD, 0), (gu, D_FF_PAD, 1), dact], [], [(2 * D_FF_PAD, BF)], rows=256, total=s,
                   name=tag + "_bwd_act")
    du = mm(dgu, wgu, tb=True, name=tag + "_bwd_du")
    dwgu = mm(u, dgu, ta=True, name=tag + "_bwd_dwgu")

    def b_pre(dh, duv, hv, pg):
        dx, dpre = jax.vjp(_rms, hv, pg)[1](duv)
        return dh + dx, dpre

    dh, dpre = rowcall(b_pre, [dh_out, du, h], [pre], [(D_MODEL, F32)], [(1, D_MODEL)], rows=512, total=s,
                       name=tag + "_bwd_pre")
    return dh, dwgu, dwd, dpre, dpost


def _residual_rms(h, a, g):
    return h + _rms(a, g)


def _bwd_residual(dh, a, g):
    return jax.vjp(_rms, a, g)[1](dh)


def _step(a):
    x, mem = a['x'][0], a['mem'][0]
    s = x.shape[0]
    me = 4 * lax.axis_index("x") + 2 * lax.axis_index("y") + lax.axis_index("c")
    w2 = {nm: a[nm][0] for nm in WEIGHTS}
    m2 = {nm: a['m_' + nm][0] for nm in WEIGHTS}
    v2 = {nm: a['v_' + nm][0] for nm in WEIGHTS}
    small = {nm: w2[nm][None] for nm in WEIGHTS if nm not in BIG and nm != 'gdn_conv_w'}

    def ff_cols(w):
        return _pad_to(w, (D_MODEL, FF_SHARD_PAD)).astype(BF)

    def ff_rows(w):
        return _pad_to(w, (FF_SHARD_PAD, D_MODEL)).astype(BF)

    send = {'ffn1_w_gate': ff_cols(w2['ffn1_w_gate']), 'ffn1_w_up': ff_cols(w2['ffn1_w_up']),
            'ffn1_w_down': ff_rows(w2['ffn1_w_down']),
            'ffn2_w_gate': ff_cols(w2['ffn2_w_gate']), 'ffn2_w_up': ff_cols(w2['ffn2_w_up']),
            'ffn2_w_down': ff_rows(w2['ffn2_w_down']),
            'w_in': w2['w_in'].astype(BF), 'w_out': w2['w_out'].astype(BF), 'mem_w_q': w2['mem_w_q'].astype(BF),
            'mem_w_kv': w2['mem_w_kv'].astype(BF), 'mem_w_o': w2['mem_w_o'].astype(BF),
            'gdn_conv_w': _pad_to(w2['gdn_conv_w'], (8, CONV_CH // N_DEV))}
    names = list(send)
    got = dict(zip(names, exchange([("gather", send[nm]) for nm in names], "gather_weights")))

    def cols(g):
        return g.transpose(1, 0, 2).reshape(g.shape[1], -1)

    def rows(g):
        return g.reshape(-1, g.shape[2])

    wgu1 = jnp.concatenate([cols(got['ffn1_w_gate']), cols(got['ffn1_w_up'])], axis=1)
    wgu2 = jnp.concatenate([cols(got['ffn2_w_gate']), cols(got['ffn2_w_up'])], axis=1)
    wd1, wd2 = rows(got['ffn1_w_down']), rows(got['ffn2_w_down'])
    w_in = cols(got['w_in'])
    sp = [0, 512, 1024, 1536, 1544, 2056, 2568, 3080, 3592, 3596, 3600]
    fq, fk, fv, ff, gq, gk, gv, gz, gb, ga = [w_in[:, sp[i]:sp[i + 1]] for i in range(10)]
    w_proj = jnp.concatenate([fq, fk, fv, gq, gk, gv, gz, ff, gb, ga,
                              jnp.zeros((D_MODEL, PROJ_W - 3584 - 16), BF)], axis=1)
    w_out, w_q, w_o = rows(got['w_out']), rows(got['mem_w_q']), rows(got['mem_w_o'])
    w_kv = cols(got['mem_w_kv'])
    conv_w8 = cols(got['gdn_conv_w'])

    bias_row = _pad_to(small['fox_f_bias'], (1, 128))
    gate_prm = _pad_to(jnp.concatenate([_pad_to(small['gdn_a_log'], (1, 128 - SMALL_A)),
                                        _pad_to(small['gdn_dt_bias'], (1, 128 - SMALL_A))], axis=0),
                       (8, 128 - SMALL_A))
    gate_prm = jnp.pad(gate_prm, ((0, 0), (SMALL_A, 0)))
    onorm = small['gdn_out_norm']

    sv1 = _ffn_fwd(x, small['ffn1_pre_norm'], wgu1, wd1, "ffn1")
    h1, = rowcall(lambda h, f, g: h + _half_rms(f, g), [x, sv1[3]], [small['ffn1_post_norm']], [(D_MODEL, F32)],
                  rows=512, total=s, name="ffn1_out")

    u2, = rowcall(_rms, [h1], [small['mix_pre_norm']], [(D_MODEL, BF)], rows=512, total=s, name="mix_pre")
    proj = mm(u2, w_proj, name="mix_proj")
    f_cum = fox_f_fwd(proj, bias_row)
    f_heads = f_cum[:, :FOX_HEADS]
    qkv_bf = proj[:, :3 * FOX_HEADS * FOX_DH].astype(BF)
    xk, xv = _fox_extras(s, 1.0, -f_heads), _fox_extras(s, 1.0, None)
    fox_flat, lse = fox_fwd(qkv_bf, _fox_extras(s, f_heads, 1.0), xk, xv)
    lse_heads = lse[:, :, :2].transpose(1, 0, 2).reshape(s, FOX_HEADS)
    cqkv = conv_fwd(proj, conv_w8)
    g_l, b_l = rowcall(_gdn_gates, [(proj, 128, SMALL_BLOCK128)], [gate_prm], [(512, F32), (512, F32)],
                       rows=512, total=s, name="gdn_gates")
    gbb = jnp.concatenate([g_l, b_l], axis=1)
    gdn_o, states = gdn_fwd(cqkv, proj, gbb, onorm)
    mixed = jnp.concatenate([fox_flat, gdn_o], axis=1).astype(BF)
    mo = mm(mixed, w_out, name="mix_out")
    h2, = rowcall(_residual_rms, [h1, mo], [small['mix_post_norm']], [(D_MODEL, F32)], rows=512, total=s,
                  name="mix_res")

    hq, = rowcall(_rms, [h2], [small['mem_pre_norm']], [(D_MODEL, BF)], rows=512, total=s, name="mem_pre")
    mn, = rowcall(_rms, [mem], [small['mem_kv_norm']], [(D_MODEL, BF)], rows=256, total=mem.shape[0], name="mem_kvn")
    q_mem = mm(hq, w_q, name="mem_q")
    kv_mem = mm(mn, w_kv, name="mem_kv")
    o_mem = xattn_fwd(q_mem, kv_mem)
    c_mem = mm(o_mem, w_o, name="mem_o")
    h3, = rowcall(_residual_rms, [h2, c_mem], [small['mem_post_norm']], [(D_MODEL, F32)], rows=512, total=s,
                  name="mem_res")

    sv2 = _ffn_fwd(h3, small['ffn2_pre_norm'], wgu2, wd2, "ffn2")

    def b_loss(h, f, tgt, g):
        err = h + _half_rms(f, g) - tgt
        part = 0.5 * jnp.sum(jnp.mean(err * err, axis=-1, keepdims=True), axis=0, keepdims=True)
        return err * (1.0 / D_MODEL), jnp.broadcast_to(part, (1, 128))

    dy, loss_acc = rowcall(b_loss, [h3, sv2[3], a['loss_target'][0]], [small['ffn2_post_norm']], [(D_MODEL, F32)],
                           [(1, 128)], rows=512, total=s, name="loss")

    grads = {}
    dh3, dwgu2, dwd2, grads['ffn2_pre_norm'], grads['ffn2_post_norm'] = _ffn_bwd(
        dy, h3, small['ffn2_pre_norm'], small['ffn2_post_norm'], wgu2, wd2, sv2, "ffn2")

    dc, grads['mem_post_norm'] = rowcall(_bwd_residual, [dh3, c_mem], [small['mem_post_norm']], [(D_MODEL, BF)],
                                         [(1, D_MODEL)], rows=512, total=s, name="mem_bwd_res")
    d_o = mm(dc, w_o, tb=True, name="mem_bwd_do")
    dw_o = mm(o_mem, dc, ta=True, name="mem_bwd_dwo")
    dq_mem, dkv = xattn_bwd(q_mem, kv_mem, d_o)
    dhq = mm(dq_mem, w_q, tb=True, name="mem_bwd_dhq")
    dw_q = mm(hq, dq_mem, ta=True, name="mem_bwd_dwq")
    dmn = mm(dkv, w_kv, tb=True, name="mem_bwd_dmn")
    dw_kv = mm(mn, dkv, ta=True, name="mem_bwd_dwkv")
    _, grads['mem_kv_norm'] = rowcall(lambda d, mv, g: jax.vjp(_rms, mv, g)[1](d), [dmn, mem],
                                      [small['mem_kv_norm']], [(D_MODEL, F32)], [(1, D_MODEL)], rows=256,
                                      total=mem.shape[0], name="mem_bwd_kvn")

    def b_pre(dh, duv, hv, pg):
        dx, dpre = jax.vjp(_rms, hv, pg)[1](duv)
        return dh + dx, dpre

    dh2, grads['mem_pre_norm'] = rowcall(b_pre, [dh3, dhq, h2], [small['mem_pre_norm']], [(D_MODEL, F32)],
                                         [(1, D_MODEL)], rows=512, total=s, name="mem_bwd_pre")

    dmo, grads['mix_post_norm'] = rowcall(_bwd_residual, [dh2, mo], [small['mix_post_norm']], [(D_MODEL, BF)],
                                          [(1, D_MODEL)], rows=512, total=s, name="mix_bwd_res")
    d_mixed = mm(dmo, w_out, tb=True, name="mix_bwd_dmixed")
    dw_out = mm(mixed, dmo, ta=True, name="mix_bwd_dwout")
    def b_delta(do, o):
        sel = (_iota2((512, 128), 0) // FOX_DH == _iota2((512, 128), 1)).astype(F32)
        return hdot(do * o, sel)

    delta, = rowcall(b_delta, [(d_mixed, 512, 0), fox_flat], [], [(128, F32)], rows=512, total=s, name="fox_delta")
    dqf, dkf, dvf = fox_bwd(qkv_bf, d_mixed[:, :512].astype(BF), xk, xv, _fox_extras(s, f_heads - lse_heads, 1.0),
                            _fox_extras(s, -delta[:, :FOX_HEADS], None))
    dqf, dkf = dqf.reshape(s, FOX_PAIRS, 256), dkf.reshape(s, FOX_PAIRS, 256)
    dfox_q = dqf[:, :, :128].reshape(s, 512) * FOX_SCALE
    dfox_k = dkf[:, :, :128].reshape(s, 512)
    df_q = dqf[:, :, 128:128 + 2 * XL].reshape(s, FOX_HEADS, XL)[:, :, 0]
    df_k = dkf[:, :, 128:128 + 2 * XL].reshape(s, FOX_HEADS, XL)[:, :, 3]
    d_f = _pad_to(df_q - df_k, (s, 128))
    dsmall_f, dbias = fox_f_bwd(proj, bias_row, d_f)
    grads['fox_f_bias'] = dbias[:, :FOX_HEADS]
    dcqkv, dz, dgb, grads['gdn_out_norm'] = gdn_bwd(cqkv, proj, gbb, onorm, states, d_mixed)

    def b_gates(sm, dsf, dg, db, prm):
        dsm, dprm = jax.vjp(_gdn_gates, sm, prm)[1]((dg, db))
        return dsm + dsf, dprm

    dsmall, dprm = rowcall(b_gates, [(proj, 128, SMALL_BLOCK128), dsmall_f, (dgb, 512, 0), (dgb, 512, 1)], [gate_prm],
                           [(128, F32)],
                           [(8, 128)], rows=512, total=s, name="gdn_bwd_gates")
    grads['gdn_a_log'] = dprm[0:1, SMALL_A:SMALL_A + GDN_HEADS]
    grads['gdn_dt_bias'] = dprm[1:2, SMALL_A:SMALL_A + GDN_HEADS]
    dqkv_pre, dconv8 = conv_bwd(proj, conv_w8, dcqkv)
    dproj = jnp.concatenate([dfox_q, dfox_k, dvf, dqkv_pre, dz, dsmall,
                             jnp.zeros((s, PROJ_W - 3584 - 128), F32)], axis=1).astype(BF)
    du2 = mm(dproj, w_proj, tb=True, name="mix_bwd_du")
    dw_proj = mm(u2, dproj, ta=True, name="mix_bwd_dwproj")
    dh1, grads['mix_pre_norm'] = rowcall(b_pre, [dh2, du2, h1], [small['mix_pre_norm']], [(D_MODEL, F32)],
                                         [(1, D_MODEL)], rows=512, total=s, name="mix_bwd_pre")

    grad_x, dwgu1, dwd1, grads['ffn1_pre_norm'], grads['ffn1_post_norm'] = _ffn_bwd(
        dh1, x, small['ffn1_pre_norm'], small['ffn1_post_norm'], wgu1, wd1, sv1, "ffn1")

    def ff_cols_shards(dw):
        return dw.reshape(D_MODEL, N_DEV, FF_SHARD_PAD)[:, :, :FF_SHARD].transpose(1, 0, 2)

    def ff_rows_shards(dw):
        return dw.reshape(N_DEV, FF_SHARD_PAD, D_MODEL)[:, :FF_SHARD]

    def col_shards(dw):
        return dw.reshape(dw.shape[0], N_DEV, -1).transpose(1, 0, 2)

    def row_shards(dw):
        return dw.reshape(N_DEV, -1, dw.shape[1])

    dw_in = jnp.concatenate([dw_proj[:, :1536], dw_proj[:, 3584:3592], dw_proj[:, 1536:3584],
                             dw_proj[:, 3592:3600]], axis=1)
    part = {'ffn1_w_gate': ff_cols_shards(dwgu1[:, :D_FF_PAD]), 'ffn1_w_up': ff_cols_shards(dwgu1[:, D_FF_PAD:]),
            'ffn1_w_down': ff_rows_shards(dwd1),
            'ffn2_w_gate': ff_cols_shards(dwgu2[:, :D_FF_PAD]), 'ffn2_w_up': ff_cols_shards(dwgu2[:, D_FF_PAD:]),
            'ffn2_w_down': ff_rows_shards(dwd2),
            'w_in': col_shards(dw_in), 'w_out': row_shards(dw_out), 'mem_w_q': row_shards(dw_q),
            'mem_w_kv': col_shards(dw_kv), 'mem_w_o': row_shards(dw_o)}
    gpack = _pack(lambda nm: grads[nm], conv=dconv8[:CONV_W], loss=loss_acc[:, :1])
    red = exchange([("scatter", part[nm].astype(BF)) for nm in BIG] + [("gather", gpack)], "reduce_grads")
    recv = dict(zip(BIG, red[:-1]))
    gsum_parts = red[-1]

    out_g, out_d, out_m, out_v = {}, {}, {}, {}
    for nm in BIG:
        r = recv[nm]
        res = adamw(w2[nm], m2[nm], v2[nm], [(r, r.shape[2], 0, d) for d in range(N_DEV)], "adamw_" + nm)
        out_g[nm], out_d[nm], out_m[nm], out_v[nm] = res
    wp = _pack(lambda nm: small[nm])
    mp = _pack(lambda nm: m2[nm][None])
    vp = _pack(lambda nm: v2[nm][None])
    pg, pd, pm, pv = adamw(wp, mp, vp, [(gsum_parts, D_MODEL, 0, d) for d in range(N_DEV)], "adamw_small")
    for dst, p in ((out_g, pg), (out_d, pd), (out_m, pm), (out_v, pv)):
        dst.update({k: val[0] for k, val in _unpack(p).items()})
    loss = pg[ROW_MISC, COL_LOSS]
    conv_g = lax.dynamic_slice_in_dim(pg[ROW_CONV:ROW_CONV + 6].reshape(CONV_W, CONV_CH), me * (CONV_CH // N_DEV),
                                      CONV_CH // N_DEV, axis=1)
    res = adamw(w2['gdn_conv_w'], m2['gdn_conv_w'], v2['gdn_conv_w'], [conv_g], "adamw_conv")
    out_g['gdn_conv_w'], out_d['gdn_conv_w'], out_m['gdn_conv_w'], out_v['gdn_conv_w'] = res

    def lead(t):
        return t[None]

    return (loss, grad_x[None], *[lead(out_g[nm]) for nm in WEIGHTS], *[lead(out_d[nm]) for nm in WEIGHTS],
            *[lead(out_m[nm]) for nm in WEIGHTS], *[lead(out_v[nm]) for nm in WEIGHTS])


def kernel(x, mem, ffn1_pre_norm, ffn1_w_gate, ffn1_w_up, ffn1_w_down, ffn1_post_norm, mix_pre_norm, w_in, fox_f_bias, gdn_conv_w, gdn_a_log, gdn_dt_bias, gdn_out_norm, w_out, mix_post_norm, mem_pre_norm, mem_kv_norm, mem_w_q, mem_w_kv, mem_w_o, mem_post_norm, ffn2_pre_norm, ffn2_w_gate, ffn2_w_up, ffn2_w_down, ffn2_post_norm, loss_target, m_ffn1_pre_norm, m_ffn1_w_gate, m_ffn1_w_up, m_ffn1_w_down, m_ffn1_post_norm, m_mix_pre_norm, m_w_in, m_fox_f_bias, m_gdn_conv_w, m_gdn_a_log, m_gdn_dt_bias, m_gdn_out_norm, m_w_out, m_mix_post_norm, m_mem_pre_norm, m_mem_kv_norm, m_mem_w_q, m_mem_w_kv, m_mem_w_o, m_mem_post_norm, m_ffn2_pre_norm, m_ffn2_w_gate, m_ffn2_w_up, m_ffn2_w_down, m_ffn2_post_norm, v_ffn1_pre_norm, v_ffn1_w_gate, v_ffn1_w_up, v_ffn1_w_down, v_ffn1_post_norm, v_mix_pre_norm, v_w_in, v_fox_f_bias, v_gdn_conv_w, v_gdn_a_log, v_gdn_dt_bias, v_gdn_out_norm, v_w_out, v_mix_post_norm, v_mem_pre_norm, v_mem_kv_norm, v_mem_w_q, v_mem_w_kv, v_mem_w_o, v_mem_post_norm, v_ffn2_pre_norm, v_ffn2_w_gate, v_ffn2_w_up, v_ffn2_w_down, v_ffn2_post_norm):
    return _step(dict(locals()))
```

```python
import functools

import jax
import jax.numpy as jnp
from jax import lax
from jax.experimental import pallas as pl
from jax.experimental.pallas import tpu as pltpu

F32 = jnp.float32
BF = jnp.bfloat16
HI = lax.Precision.HIGHEST
MESH = pl.DeviceIdType.MESH

N_DEV = 8
EPS = 1e-6
D_MODEL = 1024
D_FF = 2816
FF_SHARD = D_FF // N_DEV
FF_SHARD_PAD = 384
D_FF_PAD = FF_SHARD_PAD * N_DEV
FOX_HEADS, FOX_DH = 8, 64
GDN_HEADS, GDN_DH = 4, 128
GDN_CHUNK = 64
CONV_W = 4
MEM_HEADS, MEM_DH = 4, 256
IN_W = 3600
IN_SHARD = IN_W // N_DEV
IN_SHARD_PAD = 512
PROJ_W = 4096
SMALL_F, SMALL_B, SMALL_A = 0, 8, 12

ADAM_LR, ADAM_B1, ADAM_B2, ADAM_EPS, ADAM_WD, ADAM_STEP = 0.001, 0.9, 0.999, 1e-08, 0.01, 10

VMEM_LIMIT = 56 * 1024 * 1024


def _params(sem=None):
    return pltpu.CompilerParams(dimension_semantics=sem, vmem_limit_bytes=VMEM_LIMIT)


def _tile(n, pref, unit=128):
    if n <= pref:
        return n
    t = (pref // unit) * unit
    while t > unit and n % t:
        t -= unit
    assert n % t == 0, (n, pref)
    return t


@functools.partial(jax.custom_vjp, nondiff_argnums=(2, 3))
def bdot(a, b, ca, cb):
    return lax.dot_general(a.astype(BF), b.astype(BF), (((ca,), (cb,)), ((), ())), preferred_element_type=F32)


def _bdot_fwd(a, b, ca, cb):
    return bdot(a, b, ca, cb), (a, b)


def _bdot_bwd(ca, cb, res, g):
    a, b = res
    da = bdot(g, b, 1, 1 - cb) if ca == 1 else bdot(b, g, 1 - cb, 1)
    db = bdot(a, g, 1 - ca, 0) if cb == 0 else bdot(g, a, 0, 1 - ca)
    return da, db


bdot.defvjp(_bdot_fwd, _bdot_bwd)


def hdot(a, b):
    return jnp.dot(a, b, precision=HI, preferred_element_type=F32)


def _iota2(shape, dim):
    return lax.broadcasted_iota(jnp.int32, shape, dim)


def _sigmoid(x):
    return 1.0 / (1.0 + jnp.exp(-x))


def _silu(x):
    return x * _sigmoid(x)


def _softplus(x):
    return jnp.maximum(x, 0.0) + jnp.log(1.0 + jnp.exp(-jnp.abs(x)))


def _rms(x, gain):
    return x * lax.rsqrt(jnp.mean(x * x, axis=-1, keepdims=True) + EPS) * gain


def mm(a, b, *, name, ta=False, tb=False, out_dtype=F32, tm=1024, tn=1024, tk=512):
    m, k = (a.shape[1], a.shape[0]) if ta else a.shape
    n = b.shape[0] if tb else b.shape[1]
    assert k == (b.shape[1] if tb else b.shape[0]), (a.shape, b.shape, ta, tb)
    tm, tn, tk = _tile(m, tm), _tile(n, tn), _tile(k, tk)
    nk = k // tk
    dims = (((0 if ta else 1,), (1 if tb else 0,)), ((), ()))

    def kern(a_ref, b_ref, o_ref, *scratch):
        def part():
            return lax.dot_general(a_ref[...].astype(BF), b_ref[...].astype(BF), dims, preferred_element_type=F32)

        if nk == 1:
            o_ref[...] = part().astype(o_ref.dtype)
            return
        acc_ref, = scratch
        kk = pl.program_id(2)

        @pl.when(kk == 0)
        def _():
            acc_ref[...] = part()

        @pl.when(kk > 0)
        def _():
            acc_ref[...] += part()

        @pl.when(kk == nk - 1)
        def _():
            o_ref[...] = acc_ref[...].astype(o_ref.dtype)

    a_spec = pl.BlockSpec((tk, tm), lambda i, j, kk: (kk, i)) if ta else pl.BlockSpec((tm, tk), lambda i, j, kk: (i, kk))
    b_spec = pl.BlockSpec((tn, tk), lambda i, j, kk: (j, kk)) if tb else pl.BlockSpec((tk, tn), lambda i, j, kk: (kk, j))
    return pl.pallas_call(
        kern, name=name, grid=(m // tm, n // tn, nk),
        in_specs=[a_spec, b_spec],
        out_specs=pl.BlockSpec((tm, tn), lambda i, j, kk: (i, j)),
        out_shape=jax.ShapeDtypeStruct((m, n), out_dtype),
        scratch_shapes=[pltpu.VMEM((tm, tn), F32)] if nk > 1 else [],
        compiler_params=_params(("parallel", "parallel", "arbitrary")),
    )(a, b)


def _row_spec(item, rows):
    if not isinstance(item, tuple):
        return item, pl.BlockSpec((rows, item.shape[1]), lambda i: (i, 0))
    if len(item) == 3:
        arr, w, c = item
        return arr, pl.BlockSpec((rows, w), lambda i: (i, c))
    arr, w, c, lead = item
    return arr, pl.BlockSpec((None, rows, w), lambda i: (lead, i, c))


def _whole_spec(item):
    if not isinstance(item, tuple):
        return item, pl.BlockSpec(item.shape, lambda i: (0,) * item.ndim)
    arr, w, c = item
    return arr, pl.BlockSpec((arr.shape[0], w), lambda i: (0, c))


def rowcall(body, tiled, whole, outs, accs=(), *, rows, total, name):
    rows = min(rows, total)
    assert total % rows == 0
    t_arr, t_spec = zip(*[_row_spec(t, rows) for t in tiled])
    w_arr, w_spec = zip(*[_whole_spec(w) for w in whole]) if whole else ((), ())
    nt, nw, no, na = len(t_arr), len(w_arr), len(outs), len(accs)

    def kern(*refs):
        vals = [r[...] for r in refs[:nt + nw]]
        res = body(*vals)
        if not isinstance(res, (tuple, list)):
            res = (res,)
        assert len(res) == no + na, (name, len(res), no, na)
        for r, v in zip(refs[nt + nw:nt + nw + no], res[:no]):
            r[...] = v.astype(r.dtype)
        if na:
            acc_refs = refs[nt + nw + no:]

            @pl.when(pl.program_id(0) == 0)
            def _():
                for r in acc_refs:
                    r[...] = jnp.zeros_like(r)

            for r, v in zip(acc_refs, res[no:]):
                r[...] += v

    out_shape = [jax.ShapeDtypeStruct((total, w), d) for w, d in outs] + [jax.ShapeDtypeStruct(s, F32) for s in accs]
    out_specs = [pl.BlockSpec((rows, w), lambda i: (i, 0)) for w, _ in outs] + \
                [pl.BlockSpec(s, lambda i: (0, 0)) for s in accs]
    res = pl.pallas_call(
        kern, name=name, grid=(total // rows,),
        in_specs=list(t_spec) + list(w_spec), out_specs=out_specs, out_shape=out_shape,
        compiler_params=_params(("arbitrary",) if na else ("parallel",)),
    )(*t_arr, *w_arr)
    return res


def _colsum(x):
    return jnp.sum(x, axis=0, keepdims=True)


def _gdn_chunk(q, k, v, z, gb, bb, state, gain):
    c = GDN_CHUNK
    nh = len(q)
    hs = range(nh)
    r64, c64 = _iota2((c, c), 0), _iota2((c, c), 1)
    incl = r64 >= c64
    strict = r64 > c64
    ltri = incl.astype(F32)
    utri = (r64 <= c64).astype(F32)
    eye = (r64 == c64).astype(F32)
    ones = jnp.ones((c, c), F32)
    pick = (_iota2((GDN_DH, c), 0) == _iota2((GDN_DH, c), 1)).astype(F32)
    last = (_iota2((c, GDN_DH), 0) == c - 1).astype(F32)

    qn = [q[h] * lax.rsqrt(jnp.sum(q[h] * q[h], axis=-1, keepdims=True) + EPS) * (GDN_DH ** -0.5) for h in hs]
    kn = [k[h] * lax.rsqrt(jnp.sum(k[h] * k[h], axis=-1, keepdims=True) + EPS) for h in hs]
    gc = [hdot(ltri, gb[h]) for h in hs]
    g64 = [hdot(gb[h], pick) for h in hs]
    gcol = [hdot(ltri, g64[h]) for h in hs]
    grow = [hdot(ones, g64[h] * utri) for h in hs]
    dec = [jnp.exp(jnp.where(incl, gcol[h] - grow[h], -1e30)) for h in hs]
    kb = [kn[h] * bb[h] for h in hs]
    vb = [v[h] * bb[h] for h in hs]
    kk = [bdot(kb[h], kn[h], 1, 1) for h in hs]
    p = [-jnp.where(strict, kk[h] * dec[h], 0.0) for h in hs]
    tinv = [eye + p[h] for h in hs]
    for _ in range(5):
        p = [hdot(p[h], p[h]) for h in hs]
        tinv = [tinv[h] + hdot(tinv[h], p[h]) for h in hs]
    egc = [jnp.exp(gc[h]) for h in hs]
    u = [hdot(tinv[h], vb[h]) for h in hs]
    w = [hdot(tinv[h], kb[h] * egc[h]) for h in hs]
    attn = [bdot(qn[h], kn[h], 1, 1) * dec[h] for h in hs]
    qd = [qn[h] * egc[h] for h in hs]
    gl = [jnp.sum(gc[h] * last, axis=0, keepdims=True) for h in hs]
    kt = [kn[h] * jnp.exp(gl[h] - gc[h]) for h in hs]
    ws = [bdot(w[h], state[h], 1, 0) for h in hs]
    qs = [bdot(qd[h], state[h], 1, 0) for h in hs]
    v_new = [u[h] - ws[h] for h in hs]
    av = [bdot(attn[h], v_new[h], 1, 0) for h in hs]
    kv = [bdot(kt[h], v_new[h], 0, 0) for h in hs]
    new_state = tuple(state[h] * jnp.exp(gl[h]) + kv[h] for h in hs)
    o = tuple(_rms(qs[h] + av[h], gain) * _silu(z[h]) for h in hs)
    return o, new_state


GDN_ROWS = 512
GDN_W = GDN_HEADS * GDN_DH


def gdn_fwd(cqkv, proj, gbb, gain):
    s = cqkv.shape[0]
    nb, cpb = s // GDN_ROWS, GDN_ROWS // GDN_CHUNK
    h4 = GDN_HEADS

    def kern(qkv_ref, z_ref, gb_ref, gain_ref, o_ref, st_ref, state):
        @pl.when(pl.program_id(0) == 0)
        def _():
            state[...] = jnp.zeros_like(state)

        gain_v = gain_ref[...]

        def step(ci, carry):
            sl = pl.ds(pl.multiple_of(ci * GDN_CHUNK, GDN_CHUNK), GDN_CHUNK)
            ins = []
            for h in range(h4):
                ln = lambda base, h=h: slice(base + h * GDN_DH, base + (h + 1) * GDN_DH)
                ins.append((qkv_ref[sl, ln(0)], qkv_ref[sl, ln(GDN_W)], qkv_ref[sl, ln(2 * GDN_W)], z_ref[sl, ln(0)],
                            gb_ref[sl, ln(0)], gb_ref[sl, ln(GDN_W)], state[h]))
            cols = [tuple(col) for col in zip(*ins)]
            o, new = _gdn_chunk(*cols[:7], gain_v)
            for h in range(h4):
                st_ref[h, ci] = ins[h][6]
                o_ref[sl, h * GDN_DH:(h + 1) * GDN_DH] = o[h]
                state[h] = new[h]
            return carry

        lax.fori_loop(0, cpb, step, 0)

    return pl.pallas_call(
        kern, name="gdn_fwd", grid=(nb,),
        in_specs=[pl.BlockSpec((GDN_ROWS, 3 * GDN_W), lambda i: (i, 0)),
                  pl.BlockSpec((GDN_ROWS, GDN_W), lambda i: (i, 6)),
                  pl.BlockSpec((GDN_ROWS, 2 * GDN_W), lambda i: (i, 0)),
                  pl.BlockSpec((1, GDN_DH), lambda i: (0, 0))],
        out_specs=[pl.BlockSpec((GDN_ROWS, GDN_W), lambda i: (i, 0)),
                   pl.BlockSpec((h4, cpb, GDN_DH, GDN_DH), lambda i: (0, i, 0, 0))],
        out_shape=[jax.ShapeDtypeStruct((s, GDN_W), F32),
                   jax.ShapeDtypeStruct((h4, s // GDN_CHUNK, GDN_DH, GDN_DH), F32)],
        scratch_shapes=[pltpu.VMEM((h4, GDN_DH, GDN_DH), F32)],
        compiler_params=_params(("arbitrary",)),
    )(cqkv, proj, gbb, gain)


def gdn_bwd(cqkv, proj, gbb, gain, states, d_mixed):
    s = cqkv.shape[0]
    nb, cpb = s // GDN_ROWS, GDN_ROWS // GDN_CHUNK
    h4 = GDN_HEADS

    def kern(qkv_ref, z_ref, gb_ref, gain_ref, st_ref, do_ref, dqkv_ref, dz_ref, dgb_ref, dgain_ref, dstate):
        @pl.when(pl.program_id(0) == 0)
        def _():
            dgain_ref[...] = jnp.zeros_like(dgain_ref)
            dstate[...] = jnp.zeros_like(dstate)

        gain_v = gain_ref[...]

        def step(t, carry):
            ci = cpb - 1 - t
            sl = pl.ds(pl.multiple_of(ci * GDN_CHUNK, GDN_CHUNK), GDN_CHUNK)
            prim, cot, dst_in = [], [], []
            for h in range(h4):
                ln = lambda base, h=h: slice(base + h * GDN_DH, base + (h + 1) * GDN_DH)
                prim.append((qkv_ref[sl, ln(0)], qkv_ref[sl, ln(GDN_W)], qkv_ref[sl, ln(2 * GDN_W)], z_ref[sl, ln(0)],
                             gb_ref[sl, ln(0)], gb_ref[sl, ln(GDN_W)], st_ref[h, ci]))
                cot.append(do_ref[sl, ln(0)])
                dst_in.append(dstate[h])
            cols = [tuple(col) for col in zip(*prim)]
            vjp = jax.vjp(_gdn_chunk, *cols, gain_v)[1]
            dq, dk, dv, dz, dg, db, dst, dgn = vjp((tuple(cot), tuple(dst_in)))
            for h in range(h4):
                ln = lambda base, h=h: slice(base + h * GDN_DH, base + (h + 1) * GDN_DH)
                dqkv_ref[sl, ln(0)] = dq[h]
                dqkv_ref[sl, ln(GDN_W)] = dk[h]
                dqkv_ref[sl, ln(2 * GDN_W)] = dv[h]
                dz_ref[sl, ln(0)] = dz[h]
                dgb_ref[sl, ln(0)] = dg[h]
                dgb_ref[sl, ln(GDN_W)] = db[h]
                dstate[h] = dst[h]
            dgain_ref[...] += dgn
            return carry

        lax.fori_loop(0, cpb, step, 0)

    def rev(width, cblock=0):
        return pl.BlockSpec((GDN_ROWS, width), lambda i: (nb - 1 - i, cblock))

    return pl.pallas_call(
        kern, name="gdn_bwd", grid=(nb,),
        in_specs=[rev(3 * GDN_W), rev(GDN_W, 6), rev(2 * GDN_W), pl.BlockSpec((1, GDN_DH), lambda i: (0, 0)),
                  pl.BlockSpec((h4, cpb, GDN_DH, GDN_DH), lambda i: (0, nb - 1 - i, 0, 0)), rev(GDN_W, 1)],
        out_specs=[rev(3 * GDN_W), rev(GDN_W), rev(2 * GDN_W), pl.BlockSpec((1, GDN_DH), lambda i: (0, 0))],
        out_shape=[jax.ShapeDtypeStruct((s, 3 * GDN_W), F32), jax.ShapeDtypeStruct((s, GDN_W), F32),
                   jax.ShapeDtypeStruct((s, 2 * GDN_W), F32), jax.ShapeDtypeStruct((1, GDN_DH), F32)],
        scratch_shapes=[pltpu.VMEM((h4, GDN_DH, GDN_DH), F32)],
        compiler_params=_params(("arbitrary",)),
    )(cqkv, proj, gbb, gain, states, d_mixed)


def _gdn_gates(small, prm):
    w = GDN_HEADS * GDN_DH
    lane, head = _iota2((128, w), 0), _iota2((128, w), 1) // GDN_DH
    sel_b = (lane == SMALL_B + head).astype(F32)
    sel_a = (lane == SMALL_A + head).astype(F32)
    prow = _iota2((8, 128), 0)
    a_log = jnp.sum(prm * (prow == 0).astype(F32), axis=0, keepdims=True)
    dt_b = jnp.sum(prm * (prow == 1).astype(F32), axis=0, keepdims=True)
    beta = _sigmoid(hdot(small, sel_b))
    g = hdot(-jnp.exp(a_log) * _softplus(small + dt_b), sel_a)
    return g, beta


CONV_ROWS = 1024
CONV_COLS = 128
CONV_BLOCK0 = 1536 // CONV_COLS


def _shift_down(prev8, cur, s):
    ext = jnp.concatenate([prev8, cur], axis=0)
    return pltpu.roll(ext, s, 0)[8:]


def _shift_up(cur, next8, s):
    n = cur.shape[0]
    ext = jnp.concatenate([cur, next8], axis=0)
    return pltpu.roll(ext, n + 8 - s, 0)[:n]


def _conv_pre(x_ref, w, ci, nchunk):
    r0 = pl.multiple_of(ci * CONV_ROWS, CONV_ROWS)
    cur = x_ref[pl.ds(r0, CONV_ROWS), :]
    prev = x_ref[pl.ds(pl.multiple_of(jnp.maximum(r0 - 8, 0), 8), 8), :]
    prev = jnp.where(ci > 0, prev, 0.0)
    shifted = [cur] + [_shift_down(prev, cur, s) for s in range(1, CONV_W)]
    pre = w[CONV_W - 1:CONV_W, :] * cur
    for s in range(1, CONV_W):
        pre = pre + w[CONV_W - 1 - s:CONV_W - s, :] * shifted[s]
    return r0, pre, shifted


def conv_fwd(proj, conv_w8):
    s = proj.shape[0]
    nchunk = s // CONV_ROWS
    ncol = 3 * GDN_HEADS * GDN_DH // CONV_COLS

    def kern(x_ref, w_ref, y_ref):
        w = w_ref[...]

        def step(ci, carry):
            r0, pre, _ = _conv_pre(x_ref, w, ci, nchunk)
            y_ref[pl.ds(r0, CONV_ROWS), :] = _silu(pre)
            return carry

        lax.fori_loop(0, nchunk, step, 0)

    return pl.pallas_call(
        kern, name="conv_fwd", grid=(ncol,),
        in_specs=[pl.BlockSpec((s, CONV_COLS), lambda j: (0, CONV_BLOCK0 + j)),
                  pl.BlockSpec((8, CONV_COLS), lambda j: (0, j))],
        out_specs=pl.BlockSpec((s, CONV_COLS), lambda j: (0, j)),
        out_shape=jax.ShapeDtypeStruct((s, ncol * CONV_COLS), F32),
        compiler_params=_params(("parallel",)),
    )(proj, conv_w8)


def conv_bwd(proj, conv_w8, dy):
    s = proj.shape[0]
    nchunk = s // CONV_ROWS
    per = 3 * GDN_HEADS * GDN_DH // CONV_COLS
    outs = []
    for part in range(1):
        def kern(x_ref, w_ref, dy_ref, dx_ref, dw_ref, dpre_ref):
            w = w_ref[...]
            rows8 = _iota2((8, CONV_COLS), 0)

            def step1(ci, dw):
                r0, pre, shifted = _conv_pre(x_ref, w, ci, nchunk)
                sg = _sigmoid(pre)
                dpre = dy_ref[pl.ds(r0, CONV_ROWS), :] * sg * (1.0 + pre * (1.0 - sg))
                dpre_ref[pl.ds(r0, CONV_ROWS), :] = dpre
                for sh in range(CONV_W):
                    dw = dw + jnp.where(rows8 == CONV_W - 1 - sh, _colsum(dpre * shifted[sh]), 0.0)
                return dw

            dw_ref[...] = lax.fori_loop(0, nchunk, step1, jnp.zeros((8, CONV_COLS), F32))

            def step2(ci, carry):
                r0 = pl.multiple_of(ci * CONV_ROWS, CONV_ROWS)
                cur = dpre_ref[pl.ds(r0, CONV_ROWS), :]
                nxt = dpre_ref[pl.ds(pl.multiple_of(jnp.minimum(r0 + CONV_ROWS, s - 8), 8), 8), :]
                nxt = jnp.where(ci < nchunk - 1, nxt, 0.0)
                dx = w[CONV_W - 1:CONV_W, :] * cur
                for sh in range(1, CONV_W):
                    dx = dx + w[CONV_W - 1 - sh:CONV_W - sh, :] * _shift_up(cur, nxt, sh)
                dx_ref[pl.ds(r0, CONV_ROWS), :] = dx
                return carry

            lax.fori_loop(0, nchunk, step2, 0)

        outs.append(pl.pallas_call(
            kern, name=f"conv_bwd{part}", grid=(per,),
            in_specs=[pl.BlockSpec((s, CONV_COLS), lambda j, part=part: (0, CONV_BLOCK0 + part * per + j)),
                      pl.BlockSpec((8, CONV_COLS), lambda j, part=part: (0, part * per + j)),
                      pl.BlockSpec((s, CONV_COLS), lambda j: (0, j))],
            out_specs=[pl.BlockSpec((s, CONV_COLS), lambda j: (0, j)),
                       pl.BlockSpec((8, CONV_COLS), lambda j: (0, j))],
            out_shape=[jax.ShapeDtypeStruct((s, per * CONV_COLS), F32),
                       jax.ShapeDtypeStruct((8, per * CONV_COLS), F32)],
            scratch_shapes=[pltpu.VMEM((s, CONV_COLS), F32)],
            compiler_params=_params(("parallel",)),
        )(proj, conv_w8, dy))
    dx = jnp.concatenate([o[0] for o in outs], axis=1)
    dw = jnp.concatenate([o[1] for o in outs], axis=1)
    return dx, dw


FOXF_ROWS = 512
SMALL_BLOCK128 = 3584 // 128


def _log_sigmoid(x):
    return jnp.minimum(x, 0.0) - jnp.log(1.0 + jnp.exp(-jnp.abs(x)))


def fox_f_fwd(proj, bias_row):
    s = proj.shape[0]
    n = s // FOXF_ROWS

    def kern(x_ref, b_ref, f_ref, carry):
        @pl.when(pl.program_id(0) == 0)
        def _():
            carry[...] = jnp.zeros_like(carry)

        heads = _iota2((FOXF_ROWS, 128), 1) < FOX_HEADS
        lf = jnp.where(heads, _log_sigmoid(x_ref[...] + b_ref[...]), 0.0)
        ltri = (_iota2((FOXF_ROWS, FOXF_ROWS), 0) >= _iota2((FOXF_ROWS, FOXF_ROWS), 1)).astype(F32)
        c = hdot(ltri, lf) + carry[...]
        f_ref[...] = c
        carry[...] = c[FOXF_ROWS - 1:FOXF_ROWS, :]

    return pl.pallas_call(
        kern, name="fox_f_fwd", grid=(n,),
        in_specs=[pl.BlockSpec((FOXF_ROWS, 128), lambda i: (i, SMALL_BLOCK128)),
                  pl.BlockSpec((1, 128), lambda i: (0, 0))],
        out_specs=pl.BlockSpec((FOXF_ROWS, 128), lambda i: (i, 0)),
        out_shape=jax.ShapeDtypeStruct((s, 128), F32),
        scratch_shapes=[pltpu.VMEM((1, 128), F32)],
        compiler_params=_params(("arbitrary",)),
    )(proj, bias_row)


def fox_f_bwd(proj, bias_row, d_f):
    s = proj.shape[0]
    n = s // FOXF_ROWS

    def kern(x_ref, b_ref, df_ref, dx_ref, db_ref, carry):
        @pl.when(pl.program_id(0) == 0)
        def _():
            carry[...] = jnp.zeros_like(carry)
            db_ref[...] = jnp.zeros_like(db_ref)

        heads = _iota2((FOXF_ROWS, 128), 1) < FOX_HEADS
        utri = (_iota2((FOXF_ROWS, FOXF_ROWS), 0) <= _iota2((FOXF_ROWS, FOXF_ROWS), 1)).astype(F32)
        rc = hdot(utri, df_ref[...]) + carry[...]
        carry[...] = rc[0:1, :]
        dx = jnp.where(heads, rc * _sigmoid(-(x_ref[...] + b_ref[...])), 0.0)
        dx_ref[...] = dx
        db_ref[...] += _colsum(dx)

    return pl.pallas_call(
        kern, name="fox_f_bwd", grid=(n,),
        in_specs=[pl.BlockSpec((FOXF_ROWS, 128), lambda i: (n - 1 - i, SMALL_BLOCK128)),
                  pl.BlockSpec((1, 128), lambda i: (0, 0)),
                  pl.BlockSpec((FOXF_ROWS, 128), lambda i: (n - 1 - i, 0))],
        out_specs=[pl.BlockSpec((FOXF_ROWS, 128), lambda i: (n - 1 - i, 0)),
                   pl.BlockSpec((1, 128), lambda i: (0, 0))],
        out_shape=[jax.ShapeDtypeStruct((s, 128), F32), jax.ShapeDtypeStruct((1, 128), F32)],
        scratch_shapes=[pltpu.VMEM((1, 128), F32)],
        compiler_params=_params(("arbitrary",)),
    )(proj, bias_row, d_f)


FOX_T = 512
FOX_SCALE = FOX_DH ** -0.5
FOX_PAIRS = FOX_HEADS // 2
XL = 8
NEG = -1e30
_NT = (((1,), (1,)), ((), ()))


def _split3(x):
    def bf(v):
        return lax.reduce_precision(v, exponent_bits=8, mantissa_bits=7)

    hi = bf(x)
    mid = bf(x - hi)
    lo = bf(x - hi - mid)
    return jnp.stack([hi, mid, lo], axis=-1)


def _fox_extras(s, first, second):
    def part(v):
        if v is None:
            return jnp.zeros((s, FOX_HEADS, 3), F32)
        if isinstance(v, float):
            return jnp.full((s, FOX_HEADS, 3), v, F32)
        return _split3(v)

    cols = jnp.concatenate([part(first), part(second), jnp.zeros((s, FOX_HEADS, XL - 6), F32)], axis=-1)
    cols = _pad_to(cols.reshape(s, FOX_PAIRS, 2 * XL), (s, FOX_PAIRS, 128))
    return cols.transpose(1, 0, 2).astype(BF)


def _pair_masks(rows):
    lane = _iota2((rows, 256), 1)
    head = jnp.where(lane < 128, lane // FOX_DH, (lane - 128) // XL)
    return head == 0, head == 1


def fox_fwd(qkv, xq, xk, xv):
    s = qkv.shape[0]
    t = min(FOX_T, s)
    n = s // t

    def kern(q_ref, k_ref, v_ref, xq_ref, xk_ref, xv_ref, o_ref, lse_ref):
        i = pl.program_id(1)
        masks = _pair_masks(t)
        q_all = jnp.concatenate([q_ref[...] * FOX_SCALE, xq_ref[...]], axis=1)
        q_ops = [jnp.where(mk, q_all, 0).astype(BF) for mk in masks]

        def step(j, carry, masked):
            sl = pl.ds(pl.multiple_of(j * t, t), t)
            k_op = jnp.concatenate([k_ref[sl, :], xk_ref[sl, :]], axis=1)
            v_op = jnp.concatenate([v_ref[sl, :], xv_ref[sl, :]], axis=1)
            sc = [lax.dot_general(q_ops[e], k_op, _NT, preferred_element_type=F32) for e in range(2)]
            if masked:
                keep = _iota2((t, t), 0) >= _iota2((t, t), 1)
                sc = [jnp.where(keep, x, NEG) for x in sc]
            m_new = [jnp.maximum(carry[e][0], jnp.max(sc[e], axis=1, keepdims=True)) for e in range(2)]
            p = [jnp.exp(sc[e] - m_new[e]).astype(BF) for e in range(2)]
            pv = [jnp.dot(p[e], v_op, preferred_element_type=F32) for e in range(2)]
            return tuple((m_new[e], jnp.exp(carry[e][0] - m_new[e]) * carry[e][1] + pv[e]) for e in range(2))

        init = tuple((jnp.full((t, 1), NEG, F32), jnp.zeros((t, 256), F32)) for _ in range(2))
        carry = lax.fori_loop(0, i, lambda j, c: step(j, c, False), init)
        carry = step(i, carry, True)
        lane = _iota2((t, 256), 1)
        outs, lses = [], []
        for e in range(2):
            m, acc = carry[e]
            l = jnp.sum(jnp.where(lane == 128, acc, 0.0), axis=1, keepdims=True)
            outs.append(acc[:, :128] / l)
            lses.append(m + jnp.log(l))
        lane128 = _iota2((t, 128), 1)
        o_ref[...] = jnp.where(lane128 < FOX_DH, outs[0], outs[1])
        lse_ref[...] = jnp.where(lane128 == 0, lses[0], jnp.where(lane128 == 1, lses[1], 0.0))

    pr = FOX_PAIRS
    return pl.pallas_call(
        kern, name="fox_fwd", grid=(pr, n),
        in_specs=[pl.BlockSpec((t, 128), lambda p, i: (i, p)),
                  pl.BlockSpec((s, 128), lambda p, i: (0, pr + p)),
                  pl.BlockSpec((s, 128), lambda p, i: (0, 2 * pr + p)),
                  pl.BlockSpec((None, t, 128), lambda p, i: (p, i, 0)),
                  pl.BlockSpec((None, s, 128), lambda p, i: (p, 0, 0)),
                  pl.BlockSpec((None, s, 128), lambda p, i: (p, 0, 0))],
        out_specs=[pl.BlockSpec((t, 128), lambda p, i: (i, p)),
                   pl.BlockSpec((None, t, 128), lambda p, i: (p, i, 0))],
        out_shape=[jax.ShapeDtypeStruct((s, FOX_HEADS * FOX_DH), F32), jax.ShapeDtypeStruct((pr, s, 128), F32)],
        compiler_params=_params(("parallel", "parallel")),
    )(qkv, qkv, qkv, xq, xk, xv)


def fox_bwd(qkv, d_o, xk, xv, xqb, xdo):
    s = qkv.shape[0]
    t = min(FOX_T, s)
    n = s // t

    def kern(k_ref, v_ref, xk_ref, xv_ref, q_ref, do_ref, xq_ref, xd_ref, dq_ref, dk_ref, dv_ref):
        j = pl.program_id(1)

        @pl.when(j == 0)
        def _():
            dq_ref[...] = jnp.zeros_like(dq_ref)

        masks = _pair_masks(t)
        k_op = jnp.concatenate([k_ref[...], xk_ref[...]], axis=1)
        v_op = jnp.concatenate([v_ref[...], xv_ref[...]], axis=1)
        k_ops = [jnp.where(mk, k_op, 0).astype(BF) for mk in masks]

        def step(i, carry, masked):
            dk, dv = carry
            sl = pl.ds(pl.multiple_of(i * t, t), t)
            q_all = jnp.concatenate([q_ref[sl, :] * FOX_SCALE, xq_ref[sl, :]], axis=1)
            do_all = jnp.concatenate([do_ref[sl, :], xd_ref[sl, :]], axis=1)
            q_ops = [jnp.where(mk, q_all, 0).astype(BF) for mk in masks]
            do_ops = [jnp.where(mk, do_all, 0).astype(BF) for mk in masks]
            st = [lax.dot_general(k_op, q_ops[e], _NT, preferred_element_type=F32) for e in range(2)]
            dp = [lax.dot_general(v_op, do_ops[e], _NT, preferred_element_type=F32) for e in range(2)]
            if masked:
                keep = _iota2((t, t), 0) <= _iota2((t, t), 1)
                st = [jnp.where(keep, x, NEG) for x in st]
            pt = [jnp.exp(x) for x in st]
            dsb = [(pt[e] * dp[e]).astype(BF) for e in range(2)]
            for e in range(2):
                dv = dv + jnp.dot(pt[e].astype(BF), do_ops[e][:, :128], preferred_element_type=F32)
                dk = dk + jnp.dot(dsb[e], q_ops[e], preferred_element_type=F32)
            dq_ref[sl, :] += sum(lax.dot_general(dsb[e], k_ops[e], (((0,), (0,)), ((), ())),
                                                 preferred_element_type=F32) for e in range(2))
            return dk, dv

        init = (jnp.zeros((t, 256), F32), jnp.zeros((t, 128), F32))
        carry = step(j, init, True)
        dk, dv = lax.fori_loop(j + 1, n, lambda i, c: step(i, c, False), carry)
        dk_ref[...] = dk
        dv_ref[...] = dv

    pr = FOX_PAIRS
    return pl.pallas_call(
        kern, name="fox_bwd", grid=(pr, n),
        in_specs=[pl.BlockSpec((t, 128), lambda p, j: (j, pr + p)),
                  pl.BlockSpec((t, 128), lambda p, j: (j, 2 * pr + p)),
                  pl.BlockSpec((None, t, 128), lambda p, j: (p, j, 0)),
                  pl.BlockSpec((None, t, 128), lambda p, j: (p, j, 0)),
                  pl.BlockSpec((s, 128), lambda p, j: (0, p)),
                  pl.BlockSpec((s, 128), lambda p, j: (0, p)),
                  pl.BlockSpec((None, s, 128), lambda p, j: (p, 0, 0)),
                  pl.BlockSpec((None, s, 128), lambda p, j: (p, 0, 0))],
        out_specs=[pl.BlockSpec((s, 256), lambda p, j: (0, p)),
                   pl.BlockSpec((t, 256), lambda p, j: (j, p)),
                   pl.BlockSpec((t, 128), lambda p, j: (j, p))],
        out_shape=[jax.ShapeDtypeStruct((s, pr * 256), F32), jax.ShapeDtypeStruct((s, pr * 256), F32),
                   jax.ShapeDtypeStruct((s, FOX_HEADS * FOX_DH), F32)],
        compiler_params=_params(("parallel", "arbitrary")),
    )(qkv, qkv, xk, xv, qkv, d_o, xqb, xdo)


def _xattn_head(q, k, v):
    sc = bdot(q, k, 1, 1) * (MEM_DH ** -0.5)
    e = jnp.exp(sc - lax.stop_gradient(jnp.max(sc, axis=-1, keepdims=True)))
    p = e / jnp.sum(e, axis=-1, keepdims=True)
    return bdot(p, v, 1, 0)


def xattn_fwd(q, kv):
    s = q.shape[0]
    hh = MEM_HEADS

    def body(*vals):
        qs, ks, vs = vals[:hh], vals[hh:2 * hh], vals[2 * hh:]
        return jnp.concatenate([_xattn_head(qs[a], ks[a], vs[a]) for a in range(hh)], axis=1)

    return rowcall(body, [(q, MEM_DH, a) for a in range(hh)],
                   [(kv, MEM_DH, a) for a in range(2 * hh)],
                   [(hh * MEM_DH, BF)], rows=512, total=s, name="xattn_fwd")[0]


def xattn_bwd(q, kv, d_o):
    s = q.shape[0]
    hh = MEM_HEADS

    def body(*vals):
        qs, dos = vals[:hh], vals[hh:2 * hh]
        ks, vs = vals[2 * hh:3 * hh], vals[3 * hh:]
        dqs, dks, dvs = [], [], []
        for a in range(hh):
            _, vjp = jax.vjp(_xattn_head, qs[a], ks[a], vs[a])
            dq, dk, dv = vjp(dos[a])
            dqs.append(dq)
            dks.append(dk)
            dvs.append(dv)
        return jnp.concatenate(dqs, axis=1), jnp.concatenate(dks + dvs, axis=1)

    return rowcall(body, [(q, MEM_DH, a) for a in range(hh)] + [(d_o, MEM_DH, a) for a in range(hh)],
                   [(kv, MEM_DH, a) for a in range(2 * hh)],
                   [(hh * MEM_DH, BF)], [kv.shape], rows=512, total=s, name="xattn_bwd")


def _slab(ref, axis, start, size):
    if axis is None:
        return ref
    if axis == "lead":
        return ref.at[start]
    idx = pl.ds(pl.multiple_of(start, 128 if axis == 1 else 16), size)
    return ref.at[idx] if axis == 0 else ref.at[:, idx]


def exchange(inputs, outputs, transfers, name):
    ni, no, nt = len(inputs), len(outputs), len(transfers)
    npeer = N_DEV - 1

    def body(*refs):
        ins, outs = refs[:ni], refs[ni:ni + no]
        send, recv, loc = refs[ni + no:]
        x, y, c = lax.axis_index("x"), lax.axis_index("y"), lax.axis_index("c")
        me = 4 * x + 2 * y + c

        def peer(p):
            px = 1 - x if p & 4 else x
            py = 1 - y if p & 2 else y
            pc = 1 - c if p & 1 else c
            return (px, py, pc), 4 * px + 2 * py + pc

        def view(ref, spec, who):
            axis, off, stride, size = spec
            return _slab(ref, axis, off + who * stride, size)

        local, remote = [], []
        for w, (ii, src, oi, dst) in enumerate(transfers):
            cp = pltpu.make_async_copy(view(ins[ii], src, me), view(outs[oi], dst, me), loc.at[w])
            cp.start()
            local.append(cp)
        for p in range(1, N_DEV):
            dev, idx = peer(p)
            for w, (ii, src, oi, dst) in enumerate(transfers):
                k = w * npeer + p - 1
                out_cp = pltpu.make_async_remote_copy(
                    src_ref=view(ins[ii], src, idx), dst_ref=view(outs[oi], dst, me), send_sem=send.at[k],
                    recv_sem=recv.at[k], device_id=dev, device_id_type=MESH)
                out_cp.start()
                in_cp = pltpu.make_async_remote_copy(
                    src_ref=view(ins[ii], src, idx), dst_ref=view(outs[oi], dst, idx), send_sem=send.at[k],
                    recv_sem=recv.at[k], device_id=dev, device_id_type=MESH)
                remote.append((out_cp, in_cp))
        for out_cp, in_cp in remote:
            in_cp.wait_recv()
            out_cp.wait_send()
        for cp in local:
            cp.wait()

    hbm = pl.BlockSpec(memory_space=pl.ANY)
    return pl.pallas_call(
        body, name=name, in_specs=[hbm] * ni, out_specs=[hbm] * no, out_shape=list(outputs),
        scratch_shapes=[pltpu.SemaphoreType.DMA((nt * npeer,)), pltpu.SemaphoreType.DMA((nt * npeer,)),
                        pltpu.SemaphoreType.DMA((nt,))],
        compiler_params=pltpu.CompilerParams(has_side_effects=True),
    )(*inputs)


def adamw(w, m, v, contribs, name):
    r, c = w.shape
    nc = len(contribs)
    rows = next((r // d for d in (4, 2) if r % d == 0 and (r // d) % 16 == 0), r)
    c1, c2 = 1.0 - ADAM_B1 ** ADAM_STEP, 1.0 - ADAM_B2 ** ADAM_STEP

    def body(wv, mv, vv, *gs):
        g = gs[0].astype(F32)
        for extra in gs[1:]:
            g = g + extra.astype(F32)
        g = g[:, :c]
        m_new = ADAM_B1 * mv + (1.0 - ADAM_B1) * g
        v_new = ADAM_B2 * vv + (1.0 - ADAM_B2) * (g * g)
        delta = -ADAM_LR * ((m_new / c1) / (jnp.sqrt(v_new / c2) + ADAM_EPS) + ADAM_WD * wv)
        return g, delta, m_new, v_new

    assert nc >= 1
    return rowcall(body, [w, m, v] + list(contribs), [], [(c, F32)] * 4, rows=rows, total=r, name=name)


WEIGHTS = ['ffn1_pre_norm', 'ffn1_w_gate', 'ffn1_w_up', 'ffn1_w_down', 'ffn1_post_norm', 'mix_pre_norm', 'w_in',
           'fox_f_bias', 'gdn_conv_w', 'gdn_a_log', 'gdn_dt_bias', 'gdn_out_norm', 'w_out', 'mix_post_norm',
           'mem_pre_norm', 'mem_kv_norm', 'mem_w_q', 'mem_w_kv', 'mem_w_o', 'mem_post_norm', 'ffn2_pre_norm',
           'ffn2_w_gate', 'ffn2_w_up', 'ffn2_w_down', 'ffn2_post_norm']
GAINS = ['ffn1_pre_norm', 'ffn1_post_norm', 'mix_pre_norm', 'mix_post_norm', 'mem_pre_norm', 'mem_kv_norm',
         'mem_post_norm', 'ffn2_pre_norm', 'ffn2_post_norm']
BIG = ['ffn1_w_gate', 'ffn1_w_up', 'ffn1_w_down', 'w_in', 'w_out', 'mem_w_q', 'mem_w_kv', 'mem_w_o',
       'ffn2_w_gate', 'ffn2_w_up', 'ffn2_w_down']
PACK_ROWS = 24
ROW_MISC = len(GAINS)
ROW_CONV = ROW_MISC + 1
COL_FBIAS, COL_ALOG, COL_DTB, COL_ONORM, COL_LOSS = 0, 8, 12, 128, 256
CONV_CH = 3 * GDN_HEADS * GDN_DH


def _pad_to(a, shape):
    return jnp.pad(a, [(0, t - s) for s, t in zip(a.shape, shape)])


def _pack(get, conv=None, loss=None):
    rows = [get(nm) for nm in GAINS]
    misc = jnp.concatenate([get('fox_f_bias'), get('gdn_a_log'), get('gdn_dt_bias'),
                            jnp.zeros((1, COL_ONORM - COL_DTB - 4), F32), get('gdn_out_norm'),
                            jnp.zeros((1, 1), F32) if loss is None else loss.reshape(1, 1)], axis=1)
    rows.append(_pad_to(misc, (1, D_MODEL)))
    rows.append(jnp.zeros((6, D_MODEL), F32) if conv is None else conv.reshape(6, D_MODEL))
    return _pad_to(jnp.concatenate(rows, axis=0), (PACK_ROWS, D_MODEL))


def _unpack(p):
    out = {nm: p[i:i + 1] for i, nm in enumerate(GAINS)}
    misc = p[ROW_MISC:ROW_MISC + 1]
    out['fox_f_bias'] = misc[:, COL_FBIAS:COL_FBIAS + FOX_HEADS]
    out['gdn_a_log'] = misc[:, COL_ALOG:COL_ALOG + GDN_HEADS]
    out['gdn_dt_bias'] = misc[:, COL_DTB:COL_DTB + GDN_HEADS]
    out['gdn_out_norm'] = misc[:, COL_ONORM:COL_ONORM + GDN_DH]
    return out


def _ffn_fwd(h, pre, wgu, wd, tag):
    s = h.shape[0]
    u, = rowcall(_rms, [h], [pre], [(D_MODEL, BF)], rows=512, total=s, name=tag + "_pre")
    gu = mm(u, wgu, name=tag + "_gate_up")
    act, = rowcall(lambda a, b: _silu(a) * b, [(gu, D_FF_PAD, 0), (gu, D_FF_PAD, 1)], [], [(D_FF_PAD, BF)],
                   rows=256, total=s, name=tag + "_act")
    f = mm(act, wd, name=tag + "_down")
    return u, gu, act, f


def _half_rms(a, g):
    return 0.5 * _rms(a, g)


def _ffn_bwd(dh_out, h, pre, post, wgu, wd, saved, tag):
    u, gu, act, f = saved
    s = h.shape[0]

    def b_post(dh, fv, pg):
        return jax.vjp(_half_rms, fv, pg)[1](dh)

    df, dpost = rowcall(b_post, [dh_out, f], [post], [(D_MODEL, BF)], [(1, D_MODEL)], rows=512, total=s,
                        name=tag + "_bwd_post")
    dact = mm(df, wd, tb=True, name=tag + "_bwd_dact")
    dwd = mm(act, df, ta=True, out_dtype=BF, name=tag + "_bwd_dwd")

    def b_act(a, b, da):
        dg, du = jax.vjp(lambda g_, u_: _silu(g_) * u_, a, b)[1](da)
        return jnp.concatenate([dg, du], axis=1)

    dgu, = rowcall(b_act, [(gu, D_FF_PAD, 0), (gu, D_FF_PAD, 1), dact], [], [(2 * D_FF_PAD, BF)], rows=256, total=s,
                   name=tag + "_bwd_act")
    du = mm(dgu, wgu, tb=True, name=tag + "_bwd_du")
    dwgu = mm(u, dgu, ta=True, out_dtype=BF, name=tag + "_bwd_dwgu")

    def b_pre(dh, duv, hv, pg):
        dx, dpre = jax.vjp(_rms, hv, pg)[1](duv)
        return dh + dx, dpre

    dh, dpre = rowcall(b_pre, [dh_out, du, h], [pre], [(D_MODEL, F32)], [(1, D_MODEL)], rows=512, total=s,
                       name=tag + "_bwd_pre")
    return dh, dwgu, dwd, dpre, dpost


def _residual_rms(h, a, g):
    return h + _rms(a, g)


def _bwd_residual(dh, a, g):
    return jax.vjp(_rms, a, g)[1](dh)


def _step(a):
    x, mem = a['x'][0], a['mem'][0]
    s = x.shape[0]
    me = 4 * lax.axis_index("x") + 2 * lax.axis_index("y") + lax.axis_index("c")
    w2 = {nm: a[nm][0] for nm in WEIGHTS}
    m2 = {nm: a['m_' + nm][0] for nm in WEIGHTS}
    v2 = {nm: a['v_' + nm][0] for nm in WEIGHTS}
    small = {nm: w2[nm][None] for nm in WEIGHTS if nm not in BIG and nm != 'gdn_conv_w'}

    def ff_cols(w):
        return _pad_to(w, (D_MODEL, FF_SHARD_PAD)).astype(BF)

    def ff_rows(w):
        return _pad_to(w, (FF_SHARD_PAD, D_MODEL)).astype(BF)

    whole = (None, 0, 0, 0)
    conv_pad = 256
    g_in = [ff_cols(w2['ffn1_w_gate']), ff_cols(w2['ffn1_w_up']), ff_rows(w2['ffn1_w_down']),
            ff_cols(w2['ffn2_w_gate']), ff_cols(w2['ffn2_w_up']), ff_rows(w2['ffn2_w_down']),
            _pad_to(w2['w_in'], (D_MODEL, IN_SHARD_PAD)).astype(BF), w2['w_out'].astype(BF),
            w2['mem_w_q'].astype(BF), w2['mem_w_kv'].astype(BF), w2['mem_w_o'].astype(BF),
            _pad_to(w2['gdn_conv_w'], (8, conv_pad))]
    g_out = [jax.ShapeDtypeStruct((D_MODEL, 2 * D_FF_PAD), BF), jax.ShapeDtypeStruct((D_FF_PAD, D_MODEL), BF),
             jax.ShapeDtypeStruct((D_MODEL, 2 * D_FF_PAD), BF), jax.ShapeDtypeStruct((D_FF_PAD, D_MODEL), BF),
             jax.ShapeDtypeStruct((D_MODEL, N_DEV * IN_SHARD_PAD), BF), jax.ShapeDtypeStruct((D_MODEL, D_MODEL), BF),
             jax.ShapeDtypeStruct((D_MODEL, D_MODEL), BF), jax.ShapeDtypeStruct((D_MODEL, 2 * D_MODEL), BF),
             jax.ShapeDtypeStruct((D_MODEL, D_MODEL), BF), jax.ShapeDtypeStruct((8, N_DEV * conv_pad), F32)]
    sp_, dm = FF_SHARD_PAD, D_MODEL // N_DEV
    g_tr = [(0, whole, 0, (1, 0, sp_, sp_)), (1, whole, 0, (1, D_FF_PAD, sp_, sp_)), (2, whole, 1, (0, 0, sp_, sp_)),
            (3, whole, 2, (1, 0, sp_, sp_)), (4, whole, 2, (1, D_FF_PAD, sp_, sp_)), (5, whole, 3, (0, 0, sp_, sp_)),
            (6, whole, 4, (1, 0, IN_SHARD_PAD, IN_SHARD_PAD)), (7, whole, 5, (0, 0, dm, dm)),
            (8, whole, 6, (0, 0, dm, dm)), (9, whole, 7, (1, 0, 2 * dm, 2 * dm)), (10, whole, 8, (0, 0, dm, dm)),
            (11, whole, 9, (1, 0, conv_pad, conv_pad))]
    wgu1, wd1, wgu2, wd2, w_in_g, w_out, w_q, w_kv, w_o, conv_g = exchange(g_in, g_out, g_tr, "gather_weights")
    w_in = jnp.concatenate([w_in_g[:, j * IN_SHARD_PAD:j * IN_SHARD_PAD + IN_SHARD] for j in range(N_DEV)],
                           axis=1)
    sp = [0, 512, 1024, 1536, 1544, 2056, 2568, 3080, 3592, 3596, 3600]
    fq, fk, fv, ff, gq, gk, gv, gz, gb, ga = [w_in[:, sp[i]:sp[i + 1]] for i in range(10)]
    w_proj = jnp.concatenate([fq, fk, fv, gq, gk, gv, gz, ff, gb, ga,
                              jnp.zeros((D_MODEL, PROJ_W - 3584 - 16), BF)], axis=1)
    conv_w8 = conv_g.reshape(8, N_DEV, conv_pad)[:, :, :CONV_CH // N_DEV].reshape(8, CONV_CH)

    bias_row = _pad_to(small['fox_f_bias'], (1, 128))
    gate_prm = _pad_to(jnp.concatenate([_pad_to(small['gdn_a_log'], (1, 128 - SMALL_A)),
                                        _pad_to(small['gdn_dt_bias'], (1, 128 - SMALL_A))], axis=0),
                       (8, 128 - SMALL_A))
    gate_prm = jnp.pad(gate_prm, ((0, 0), (SMALL_A, 0)))
    onorm = small['gdn_out_norm']

    sv1 = _ffn_fwd(x, small['ffn1_pre_norm'], wgu1, wd1, "ffn1")
    h1, = rowcall(lambda h, f, g: h + _half_rms(f, g), [x, sv1[3]], [small['ffn1_post_norm']], [(D_MODEL, F32)],
                  rows=512, total=s, name="ffn1_out")

    u2, = rowcall(_rms, [h1], [small['mix_pre_norm']], [(D_MODEL, BF)], rows=512, total=s, name="mix_pre")
    proj = mm(u2, w_proj, name="mix_proj")
    f_cum = fox_f_fwd(proj, bias_row)
    f_heads = f_cum[:, :FOX_HEADS]
    qkv_bf = proj[:, :3 * FOX_HEADS * FOX_DH].astype(BF)
    xk, xv = _fox_extras(s, 1.0, -f_heads), _fox_extras(s, 1.0, None)
    fox_flat, lse = fox_fwd(qkv_bf, _fox_extras(s, f_heads, 1.0), xk, xv)
    lse_heads = lse[:, :, :2].transpose(1, 0, 2).reshape(s, FOX_HEADS)
    cqkv = conv_fwd(proj, conv_w8)
    g_l, b_l = rowcall(_gdn_gates, [(proj, 128, SMALL_BLOCK128)], [gate_prm], [(512, F32), (512, F32)],
                       rows=512, total=s, name="gdn_gates")
    gbb = jnp.concatenate([g_l, b_l], axis=1)
    gdn_o, states = gdn_fwd(cqkv, proj, gbb, onorm)
    mixed = jnp.concatenate([fox_flat, gdn_o], axis=1).astype(BF)
    mo = mm(mixed, w_out, name="mix_out")
    h2, = rowcall(_residual_rms, [h1, mo], [small['mix_post_norm']], [(D_MODEL, F32)], rows=512, total=s,
                  name="mix_res")

    hq, = rowcall(_rms, [h2], [small['mem_pre_norm']], [(D_MODEL, BF)], rows=512, total=s, name="mem_pre")
    mn, = rowcall(_rms, [mem], [small['mem_kv_norm']], [(D_MODEL, BF)], rows=256, total=mem.shape[0], name="mem_kvn")
    q_mem = mm(hq, w_q, name="mem_q")
    kv_mem = mm(mn, w_kv, name="mem_kv")
    o_mem = xattn_fwd(q_mem, kv_mem)
    c_mem = mm(o_mem, w_o, name="mem_o")
    h3, = rowcall(_residual_rms, [h2, c_mem], [small['mem_post_norm']], [(D_MODEL, F32)], rows=512, total=s,
                  name="mem_res")

    sv2 = _ffn_fwd(h3, small['ffn2_pre_norm'], wgu2, wd2, "ffn2")

    def b_loss(h, f, tgt, g):
        err = h + _half_rms(f, g) - tgt
        part = 0.5 * jnp.sum(jnp.mean(err * err, axis=-1, keepdims=True), axis=0, keepdims=True)
        return err * (1.0 / D_MODEL), jnp.broadcast_to(part, (1, 128))

    dy, loss_acc = rowcall(b_loss, [h3, sv2[3], a['loss_target'][0]], [small['ffn2_post_norm']], [(D_MODEL, F32)],
                           [(1, 128)], rows=512, total=s, name="loss")

    grads = {}
    dh3, dwgu2, dwd2, grads['ffn2_pre_norm'], grads['ffn2_post_norm'] = _ffn_bwd(
        dy, h3, small['ffn2_pre_norm'], small['ffn2_post_norm'], wgu2, wd2, sv2, "ffn2")

    dc, grads['mem_post_norm'] = rowcall(_bwd_residual, [dh3, c_mem], [small['mem_post_norm']], [(D_MODEL, BF)],
                                         [(1, D_MODEL)], rows=512, total=s, name="mem_bwd_res")
    d_o = mm(dc, w_o, tb=True, name="mem_bwd_do")
    dw_o = mm(o_mem, dc, ta=True, out_dtype=BF, name="mem_bwd_dwo")
    dq_mem, dkv = xattn_bwd(q_mem, kv_mem, d_o)
    dhq = mm(dq_mem, w_q, tb=True, name="mem_bwd_dhq")
    dw_q = mm(hq, dq_mem, ta=True, out_dtype=BF, name="mem_bwd_dwq")
    dmn = mm(dkv, w_kv, tb=True, name="mem_bwd_dmn")
    dw_kv = mm(mn, dkv, ta=True, out_dtype=BF, name="mem_bwd_dwkv")
    _, grads['mem_kv_norm'] = rowcall(lambda d, mv, g: jax.vjp(_rms, mv, g)[1](d), [dmn, mem],
                                      [small['mem_kv_norm']], [(D_MODEL, F32)], [(1, D_MODEL)], rows=256,
                                      total=mem.shape[0], name="mem_bwd_kvn")

    def b_pre(dh, duv, hv, pg):
        dx, dpre = jax.vjp(_rms, hv, pg)[1](duv)
        return dh + dx, dpre

    dh2, grads['mem_pre_norm'] = rowcall(b_pre, [dh3, dhq, h2], [small['mem_pre_norm']], [(D_MODEL, F32)],
                                         [(1, D_MODEL)], rows=512, total=s, name="mem_bwd_pre")

    dmo, grads['mix_post_norm'] = rowcall(_bwd_residual, [dh2, mo], [small['mix_post_norm']], [(D_MODEL, BF)],
                                          [(1, D_MODEL)], rows=512, total=s, name="mix_bwd_res")
    d_mixed = mm(dmo, w_out, tb=True, name="mix_bwd_dmixed")
    dw_out = mm(mixed, dmo, ta=True, out_dtype=BF, name="mix_bwd_dwout")
    def b_delta(do, o):
        sel = (_iota2((512, 128), 0) // FOX_DH == _iota2((512, 128), 1)).astype(F32)
        return hdot(do * o, sel)

    delta, = rowcall(b_delta, [(d_mixed, 512, 0), fox_flat], [], [(128, F32)], rows=512, total=s, name="fox_delta")
    dqf, dkf, dvf = fox_bwd(qkv_bf, d_mixed[:, :512].astype(BF), xk, xv, _fox_extras(s, f_heads - lse_heads, 1.0),
                            _fox_extras(s, -delta[:, :FOX_HEADS], None))
    dqf, dkf = dqf.reshape(s, FOX_PAIRS, 256), dkf.reshape(s, FOX_PAIRS, 256)
    dfox_q = dqf[:, :, :128].reshape(s, 512) * FOX_SCALE
    dfox_k = dkf[:, :, :128].reshape(s, 512)
    df_q = dqf[:, :, 128:128 + 2 * XL].reshape(s, FOX_HEADS, XL)[:, :, 0]
    df_k = dkf[:, :, 128:128 + 2 * XL].reshape(s, FOX_HEADS, XL)[:, :, 3]
    d_f = _pad_to(df_q - df_k, (s, 128))
    dsmall_f, dbias = fox_f_bwd(proj, bias_row, d_f)
    grads['fox_f_bias'] = dbias[:, :FOX_HEADS]
    dcqkv, dz, dgb, grads['gdn_out_norm'] = gdn_bwd(cqkv, proj, gbb, onorm, states, d_mixed)

    def b_gates(sm, dsf, dg, db, prm):
        dsm, dprm = jax.vjp(_gdn_gates, sm, prm)[1]((dg, db))
        return dsm + dsf, dprm

    dsmall, dprm = rowcall(b_gates, [(proj, 128, SMALL_BLOCK128), dsmall_f, (dgb, 512, 0), (dgb, 512, 1)], [gate_prm],
                           [(128, F32)],
                           [(8, 128)], rows=512, total=s, name="gdn_bwd_gates")
    grads['gdn_a_log'] = dprm[0:1, SMALL_A:SMALL_A + GDN_HEADS]
    grads['gdn_dt_bias'] = dprm[1:2, SMALL_A:SMALL_A + GDN_HEADS]
    dqkv_pre, dconv8 = conv_bwd(proj, conv_w8, dcqkv)
    dproj = jnp.concatenate([dfox_q, dfox_k, dvf, dqkv_pre, dz, dsmall,
                             jnp.zeros((s, PROJ_W - 3584 - 128), F32)], axis=1).astype(BF)
    du2 = mm(dproj, w_proj, tb=True, name="mix_bwd_du")
    dw_proj = mm(u2, dproj, ta=True, out_dtype=BF, name="mix_bwd_dwproj")
    dh1, grads['mix_pre_norm'] = rowcall(b_pre, [dh2, du2, h1], [small['mix_pre_norm']], [(D_MODEL, F32)],
                                         [(1, D_MODEL)], rows=512, total=s, name="mix_bwd_pre")

    grad_x, dwgu1, dwd1, grads['ffn1_pre_norm'], grads['ffn1_post_norm'] = _ffn_bwd(
        dh1, x, small['ffn1_pre_norm'], small['ffn1_post_norm'], wgu1, wd1, sv1, "ffn1")

    dw_in = jnp.concatenate([dw_proj[:, :1536], dw_proj[:, 3584:3592], dw_proj[:, 1536:3584],
                             dw_proj[:, 3592:3600]], axis=1)
    gap = jnp.zeros((D_MODEL, IN_SHARD_PAD - IN_SHARD), BF)
    dw_in = jnp.concatenate([piece for j in range(N_DEV) for piece in (dw_in[:, j * IN_SHARD:(j + 1) * IN_SHARD], gap)],
                            axis=1)
    gpack = _pack(lambda nm: grads[nm], conv=dconv8[:CONV_W], loss=loss_acc[:, :1])
    r_in = [dwgu1, dwd1, dwgu2, dwd2, dw_in, dw_out, dw_q, dw_kv, dw_o, gpack]
    lead = ("lead", 0, 1, 0)

    def land(r, c, dt=BF):
        return jax.ShapeDtypeStruct((N_DEV, r, c), dt)

    r_out = [land(D_MODEL, sp_), land(D_MODEL, sp_), land(FF_SHARD, D_MODEL),
             land(D_MODEL, sp_), land(D_MODEL, sp_), land(FF_SHARD, D_MODEL),
             land(D_MODEL, IN_SHARD_PAD), land(dm, D_MODEL), land(dm, D_MODEL), land(D_MODEL, 2 * dm),
             land(dm, D_MODEL), land(PACK_ROWS, D_MODEL, F32)]
    r_tr = [(0, (1, 0, sp_, sp_), 0, lead), (0, (1, D_FF_PAD, sp_, sp_), 1, lead), (1, (0, 0, sp_, FF_SHARD), 2, lead),
            (2, (1, 0, sp_, sp_), 3, lead), (2, (1, D_FF_PAD, sp_, sp_), 4, lead), (3, (0, 0, sp_, FF_SHARD), 5, lead),
            (4, (1, 0, IN_SHARD_PAD, IN_SHARD_PAD), 6, lead), (5, (0, 0, dm, dm), 7, lead), (6, (0, 0, dm, dm), 8, lead),
            (7, (1, 0, 2 * dm, 2 * dm), 9, lead), (8, (0, 0, dm, dm), 10, lead), (9, whole, 11, lead)]
    red = exchange(r_in, r_out, r_tr, "reduce_grads")
    recv = dict(zip(['ffn1_w_gate', 'ffn1_w_up', 'ffn1_w_down', 'ffn2_w_gate', 'ffn2_w_up', 'ffn2_w_down', 'w_in',
                     'w_out', 'mem_w_q', 'mem_w_kv', 'mem_w_o'], red[:-1]))
    gsum_parts = red[-1]

    out_g, out_d, out_m, out_v = {}, {}, {}, {}
    for nm in BIG:
        r = recv[nm]
        res = adamw(w2[nm], m2[nm], v2[nm], [(r, r.shape[2], 0, d) for d in range(N_DEV)], "adamw_" + nm)
        out_g[nm], out_d[nm], out_m[nm], out_v[nm] = res
    wp = _pack(lambda nm: small[nm])
    mp = _pack(lambda nm: m2[nm][None])
    vp = _pack(lambda nm: v2[nm][None])
    pg, pd, pm, pv = adamw(wp, mp, vp, [(gsum_parts, D_MODEL, 0, d) for d in range(N_DEV)], "adamw_small")
    for dst, p in ((out_g, pg), (out_d, pd), (out_m, pm), (out_v, pv)):
        dst.update({k: val[0] for k, val in _unpack(p).items()})
    loss = pg[ROW_MISC, COL_LOSS]
    conv_g = lax.dynamic_slice_in_dim(pg[ROW_CONV:ROW_CONV + 6].reshape(CONV_W, CONV_CH), me * (CONV_CH // N_DEV),
                                      CONV_CH // N_DEV, axis=1)
    res = adamw(w2['gdn_conv_w'], m2['gdn_conv_w'], v2['gdn_conv_w'], [conv_g], "adamw_conv")
    out_g['gdn_conv_w'], out_d['gdn_conv_w'], out_m['gdn_conv_w'], out_v['gdn_conv_w'] = res

    def depth(t):
        return t[None]

    return (loss, grad_x[None], *[depth(out_g[nm]) for nm in WEIGHTS], *[depth(out_d[nm]) for nm in WEIGHTS],
            *[depth(out_m[nm]) for nm in WEIGHTS], *[depth(out_v[nm]) for nm in WEIGHTS])


def kernel(x, mem, ffn1_pre_norm, ffn1_w_gate, ffn1_w_up, ffn1_w_down, ffn1_post_norm, mix_pre_norm, w_in, fox_f_bias, gdn_conv_w, gdn_a_log, gdn_dt_bias, gdn_out_norm, w_out, mix_post_norm, mem_pre_norm, mem_kv_norm, mem_w_q, mem_w_kv, mem_w_o, mem_post_norm, ffn2_pre_norm, ffn2_w_gate, ffn2_w_up, ffn2_w_down, ffn2_post_norm, loss_target, m_ffn1_pre_norm, m_ffn1_w_gate, m_ffn1_w_up, m_ffn1_w_down, m_ffn1_post_norm, m_mix_pre_norm, m_w_in, m_fox_f_bias, m_gdn_conv_w, m_gdn_a_log, m_gdn_dt_bias, m_gdn_out_norm, m_w_out, m_mix_post_norm, m_mem_pre_norm, m_mem_kv_norm, m_mem_w_q, m_mem_w_kv, m_mem_w_o, m_mem_post_norm, m_ffn2_pre_norm, m_ffn2_w_gate, m_ffn2_w_up, m_ffn2_w_down, m_ffn2_post_norm, v_ffn1_pre_norm, v_ffn1_w_gate, v_ffn1_w_up, v_ffn1_w_down, v_ffn1_post_norm, v_mix_pre_norm, v_w_in, v_fox_f_bias, v_gdn_conv_w, v_gdn_a_log, v_gdn_dt_bias, v_gdn_out_norm, v_w_out, v_mix_post_norm, v_mem_pre_norm, v_mem_kv_norm, v_mem_w_q, v_mem_w_kv, v_mem_w_o, v_mem_post_norm, v_ffn2_pre_norm, v_ffn2_w_gate, v_ffn2_w_up, v_ffn2_w_down, v_ffn2_post_norm):
    return _step(dict(locals()))
```

```python
import functools

import jax
import jax.numpy as jnp
from jax import lax
from jax.experimental import pallas as pl
from jax.experimental.pallas import tpu as pltpu

F32 = jnp.float32
BF = jnp.bfloat16
HI = lax.Precision.HIGHEST
MESH = pl.DeviceIdType.MESH

N_DEV = 8
EPS = 1e-6
D_MODEL = 1024
D_FF = 2816
FF_SHARD = D_FF // N_DEV
FF_SHARD_PAD = 384
D_FF_PAD = FF_SHARD_PAD * N_DEV
FOX_HEADS, FOX_DH = 8, 64
GDN_HEADS, GDN_DH = 4, 128
GDN_CHUNK = 64
CONV_W = 4
MEM_HEADS, MEM_DH = 4, 256
IN_W = 3600
IN_SHARD = IN_W // N_DEV
IN_SHARD_PAD = 512
PROJ_W = 4096
SMALL_F, SMALL_B, SMALL_A = 0, 8, 12

ADAM_LR, ADAM_B1, ADAM_B2, ADAM_EPS, ADAM_WD, ADAM_STEP = 0.001, 0.9, 0.999, 1e-08, 0.01, 10

VMEM_LIMIT = 56 * 1024 * 1024


def _params(sem=None):
    return pltpu.CompilerParams(dimension_semantics=sem, vmem_limit_bytes=VMEM_LIMIT)


def _tile(n, pref, unit=128):
    if n <= pref:
        return n
    t = (pref // unit) * unit
    while t > unit and n % t:
        t -= unit
    assert n % t == 0, (n, pref)
    return t


@functools.partial(jax.custom_vjp, nondiff_argnums=(2, 3))
def bdot(a, b, ca, cb):
    return lax.dot_general(a.astype(BF), b.astype(BF), (((ca,), (cb,)), ((), ())), preferred_element_type=F32)


def _bdot_fwd(a, b, ca, cb):
    return bdot(a, b, ca, cb), (a, b)


def _bdot_bwd(ca, cb, res, g):
    a, b = res
    da = bdot(g, b, 1, 1 - cb) if ca == 1 else bdot(b, g, 1 - cb, 1)
    db = bdot(a, g, 1 - ca, 0) if cb == 0 else bdot(g, a, 0, 1 - ca)
    return da, db


bdot.defvjp(_bdot_fwd, _bdot_bwd)


def hdot(a, b):
    return jnp.dot(a, b, precision=HI, preferred_element_type=F32)


def _iota2(shape, dim):
    return lax.broadcasted_iota(jnp.int32, shape, dim)


def _sigmoid(x):
    return 1.0 / (1.0 + jnp.exp(-x))


def _silu(x):
    return x * _sigmoid(x)


def _softplus(x):
    return jnp.maximum(x, 0.0) + jnp.log(1.0 + jnp.exp(-jnp.abs(x)))


def _rms(x, gain):
    return x * lax.rsqrt(jnp.mean(x * x, axis=-1, keepdims=True) + EPS) * gain


def mm(a, b, *, name, ta=False, tb=False, out_dtype=F32, tm=1024, tn=1024, tk=512):
    m, k = (a.shape[1], a.shape[0]) if ta else a.shape
    n = b.shape[0] if tb else b.shape[1]
    assert k == (b.shape[1] if tb else b.shape[0]), (a.shape, b.shape, ta, tb)
    tm, tn, tk = _tile(m, tm), _tile(n, tn), _tile(k, tk)
    nk = k // tk
    dims = (((0 if ta else 1,), (1 if tb else 0,)), ((), ()))

    def kern(a_ref, b_ref, o_ref, *scratch):
        def part():
            return lax.dot_general(a_ref[...].astype(BF), b_ref[...].astype(BF), dims, preferred_element_type=F32)

        if nk == 1:
            o_ref[...] = part().astype(o_ref.dtype)
            return
        acc_ref, = scratch
        kk = pl.program_id(2)

        @pl.when(kk == 0)
        def _():
            acc_ref[...] = part()

        @pl.when(kk > 0)
        def _():
            acc_ref[...] += part()

        @pl.when(kk == nk - 1)
        def _():
            o_ref[...] = acc_ref[...].astype(o_ref.dtype)

    a_spec = pl.BlockSpec((tk, tm), lambda i, j, kk: (kk, i)) if ta else pl.BlockSpec((tm, tk), lambda i, j, kk: (i, kk))
    b_spec = pl.BlockSpec((tn, tk), lambda i, j, kk: (j, kk)) if tb else pl.BlockSpec((tk, tn), lambda i, j, kk: (kk, j))
    return pl.pallas_call(
        kern, name=name, grid=(m // tm, n // tn, nk),
        in_specs=[a_spec, b_spec],
        out_specs=pl.BlockSpec((tm, tn), lambda i, j, kk: (i, j)),
        out_shape=jax.ShapeDtypeStruct((m, n), out_dtype),
        scratch_shapes=[pltpu.VMEM((tm, tn), F32)] if nk > 1 else [],
        compiler_params=_params(("parallel", "parallel", "arbitrary")),
    )(a, b)


def _row_spec(item, rows):
    if not isinstance(item, tuple):
        return item, pl.BlockSpec((rows, item.shape[1]), lambda i: (i, 0))
    if len(item) == 3:
        arr, w, c = item
        return arr, pl.BlockSpec((rows, w), lambda i: (i, c))
    arr, w, c, lead = item
    return arr, pl.BlockSpec((None, rows, w), lambda i: (lead, i, c))


def _whole_spec(item):
    if not isinstance(item, tuple):
        return item, pl.BlockSpec(item.shape, lambda i: (0,) * item.ndim)
    arr, w, c = item
    return arr, pl.BlockSpec((arr.shape[0], w), lambda i: (0, c))


def rowcall(body, tiled, whole, outs, accs=(), *, rows, total, name):
    rows = min(rows, total)
    assert total % rows == 0
    t_arr, t_spec = zip(*[_row_spec(t, rows) for t in tiled])
    w_arr, w_spec = zip(*[_whole_spec(w) for w in whole]) if whole else ((), ())
    nt, nw, no, na = len(t_arr), len(w_arr), len(outs), len(accs)

    def kern(*refs):
        vals = [r[...] for r in refs[:nt + nw]]
        res = body(*vals)
        if not isinstance(res, (tuple, list)):
            res = (res,)
        assert len(res) == no + na, (name, len(res), no, na)
        for r, v in zip(refs[nt + nw:nt + nw + no], res[:no]):
            r[...] = v.astype(r.dtype)
        if na:
            acc_refs = refs[nt + nw + no:]

            @pl.when(pl.program_id(0) == 0)
            def _():
                for r in acc_refs:
                    r[...] = jnp.zeros_like(r)

            for r, v in zip(acc_refs, res[no:]):
                r[...] += v

    out_shape = [jax.ShapeDtypeStruct((total, w), d) for w, d in outs] + [jax.ShapeDtypeStruct(s, F32) for s in accs]
    out_specs = [pl.BlockSpec((rows, w), lambda i: (i, 0)) for w, _ in outs] + \
                [pl.BlockSpec(s, lambda i: (0, 0)) for s in accs]
    res = pl.pallas_call(
        kern, name=name, grid=(total // rows,),
        in_specs=list(t_spec) + list(w_spec), out_specs=out_specs, out_shape=out_shape,
        compiler_params=_params(("arbitrary",) if na else ("parallel",)),
    )(*t_arr, *w_arr)
    return res


def _colsum(x):
    return jnp.sum(x, axis=0, keepdims=True)


def _gdn_chunk(q, k, v, z, gb, bb, state, gain):
    c = GDN_CHUNK
    nh = len(q)
    hs = range(nh)
    r64, c64 = _iota2((c, c), 0), _iota2((c, c), 1)
    incl = r64 >= c64
    strict = r64 > c64
    ltri = incl.astype(F32)
    utri = (r64 <= c64).astype(F32)
    eye = (r64 == c64).astype(F32)
    ones = jnp.ones((c, c), F32)
    pick = (_iota2((GDN_DH, c), 0) == _iota2((GDN_DH, c), 1)).astype(F32)
    last = (_iota2((c, GDN_DH), 0) == c - 1).astype(F32)

    qn = [q[h] * lax.rsqrt(jnp.sum(q[h] * q[h], axis=-1, keepdims=True) + EPS) * (GDN_DH ** -0.5) for h in hs]
    kn = [k[h] * lax.rsqrt(jnp.sum(k[h] * k[h], axis=-1, keepdims=True) + EPS) for h in hs]
    gc = [hdot(ltri, gb[h]) for h in hs]
    g64 = [hdot(gb[h], pick) for h in hs]
    gcol = [hdot(ltri, g64[h]) for h in hs]
    grow = [hdot(ones, g64[h] * utri) for h in hs]
    dec = [jnp.exp(jnp.where(incl, gcol[h] - grow[h], -1e30)) for h in hs]
    kb = [kn[h] * bb[h] for h in hs]
    vb = [v[h] * bb[h] for h in hs]
    kk = [bdot(kb[h], kn[h], 1, 1) for h in hs]
    p = [-jnp.where(strict, kk[h] * dec[h], 0.0) for h in hs]
    tinv = [eye + p[h] for h in hs]
    for _ in range(5):
        p = [hdot(p[h], p[h]) for h in hs]
        tinv = [tinv[h] + hdot(tinv[h], p[h]) for h in hs]
    egc = [jnp.exp(gc[h]) for h in hs]
    u = [hdot(tinv[h], vb[h]) for h in hs]
    w = [hdot(tinv[h], kb[h] * egc[h]) for h in hs]
    attn = [bdot(qn[h], kn[h], 1, 1) * dec[h] for h in hs]
    qd = [qn[h] * egc[h] for h in hs]
    gl = [jnp.sum(gc[h] * last, axis=0, keepdims=True) for h in hs]
    kt = [kn[h] * jnp.exp(gl[h] - gc[h]) for h in hs]
    ws = [bdot(w[h], state[h], 1, 0) for h in hs]
    qs = [bdot(qd[h], state[h], 1, 0) for h in hs]
    v_new = [u[h] - ws[h] for h in hs]
    av = [bdot(attn[h], v_new[h], 1, 0) for h in hs]
    kv = [bdot(kt[h], v_new[h], 0, 0) for h in hs]
    new_state = tuple(state[h] * jnp.exp(gl[h]) + kv[h] for h in hs)
    o = tuple(_rms(qs[h] + av[h], gain) * _silu(z[h]) for h in hs)
    return o, new_state


GDN_ROWS = 512
GDN_W = GDN_HEADS * GDN_DH


def gdn_fwd(cqkv, proj, gbb, gain):
    s = cqkv.shape[0]
    nb, cpb = s // GDN_ROWS, GDN_ROWS // GDN_CHUNK
    h4 = GDN_HEADS

    def kern(qkv_ref, z_ref, gb_ref, gain_ref, o_ref, st_ref, state):
        @pl.when(pl.program_id(0) == 0)
        def _():
            state[...] = jnp.zeros_like(state)

        gain_v = gain_ref[...]

        def step(ci, carry):
            sl = pl.ds(pl.multiple_of(ci * GDN_CHUNK, GDN_CHUNK), GDN_CHUNK)
            ins = []
            for h in range(h4):
                ln = lambda base, h=h: slice(base + h * GDN_DH, base + (h + 1) * GDN_DH)
                ins.append((qkv_ref[sl, ln(0)], qkv_ref[sl, ln(GDN_W)], qkv_ref[sl, ln(2 * GDN_W)], z_ref[sl, ln(0)],
                            gb_ref[sl, ln(0)], gb_ref[sl, ln(GDN_W)], state[h]))
            cols = [tuple(col) for col in zip(*ins)]
            o, new = _gdn_chunk(*cols[:7], gain_v)
            for h in range(h4):
                st_ref[h, ci] = ins[h][6]
                o_ref[sl, h * GDN_DH:(h + 1) * GDN_DH] = o[h]
                state[h] = new[h]
            return carry

        lax.fori_loop(0, cpb, step, 0)

    return pl.pallas_call(
        kern, name="gdn_fwd", grid=(nb,),
        in_specs=[pl.BlockSpec((GDN_ROWS, 3 * GDN_W), lambda i: (i, 0)),
                  pl.BlockSpec((GDN_ROWS, GDN_W), lambda i: (i, 6)),
                  pl.BlockSpec((GDN_ROWS, 2 * GDN_W), lambda i: (i, 0)),
                  pl.BlockSpec((1, GDN_DH), lambda i: (0, 0))],
        out_specs=[pl.BlockSpec((GDN_ROWS, GDN_W), lambda i: (i, 0)),
                   pl.BlockSpec((h4, cpb, GDN_DH, GDN_DH), lambda i: (0, i, 0, 0))],
        out_shape=[jax.ShapeDtypeStruct((s, GDN_W), F32),
                   jax.ShapeDtypeStruct((h4, s // GDN_CHUNK, GDN_DH, GDN_DH), F32)],
        scratch_shapes=[pltpu.VMEM((h4, GDN_DH, GDN_DH), F32)],
        compiler_params=_params(("arbitrary",)),
    )(cqkv, proj, gbb, gain)


def gdn_bwd(cqkv, proj, gbb, gain, states, d_mixed):
    s = cqkv.shape[0]
    nb, cpb = s // GDN_ROWS, GDN_ROWS // GDN_CHUNK
    h4 = GDN_HEADS

    def kern(qkv_ref, z_ref, gb_ref, gain_ref, st_ref, do_ref, dqkv_ref, dz_ref, dgb_ref, dgain_ref, dstate):
        @pl.when(pl.program_id(0) == 0)
        def _():
            dgain_ref[...] = jnp.zeros_like(dgain_ref)
            dstate[...] = jnp.zeros_like(dstate)

        gain_v = gain_ref[...]

        def step(t, carry):
            ci = cpb - 1 - t
            sl = pl.ds(pl.multiple_of(ci * GDN_CHUNK, GDN_CHUNK), GDN_CHUNK)
            prim, cot, dst_in = [], [], []
            for h in range(h4):
                ln = lambda base, h=h: slice(base + h * GDN_DH, base + (h + 1) * GDN_DH)
                prim.append((qkv_ref[sl, ln(0)], qkv_ref[sl, ln(GDN_W)], qkv_ref[sl, ln(2 * GDN_W)], z_ref[sl, ln(0)],
                             gb_ref[sl, ln(0)], gb_ref[sl, ln(GDN_W)], st_ref[h, ci]))
                cot.append(do_ref[sl, ln(0)])
                dst_in.append(dstate[h])
            cols = [tuple(col) for col in zip(*prim)]
            vjp = jax.vjp(_gdn_chunk, *cols, gain_v)[1]
            dq, dk, dv, dz, dg, db, dst, dgn = vjp((tuple(cot), tuple(dst_in)))
            for h in range(h4):
                ln = lambda base, h=h: slice(base + h * GDN_DH, base + (h + 1) * GDN_DH)
                dqkv_ref[sl, ln(0)] = dq[h]
                dqkv_ref[sl, ln(GDN_W)] = dk[h]
                dqkv_ref[sl, ln(2 * GDN_W)] = dv[h]
                dz_ref[sl, ln(0)] = dz[h]
                dgb_ref[sl, ln(0)] = dg[h]
                dgb_ref[sl, ln(GDN_W)] = db[h]
                dstate[h] = dst[h]
            dgain_ref[...] += dgn
            return carry

        lax.fori_loop(0, cpb, step, 0)

    def rev(width, cblock=0):
        return pl.BlockSpec((GDN_ROWS, width), lambda i: (nb - 1 - i, cblock))

    return pl.pallas_call(
        kern, name="gdn_bwd", grid=(nb,),
        in_specs=[rev(3 * GDN_W), rev(GDN_W, 6), rev(2 * GDN_W), pl.BlockSpec((1, GDN_DH), lambda i: (0, 0)),
                  pl.BlockSpec((h4, cpb, GDN_DH, GDN_DH), lambda i: (0, nb - 1 - i, 0, 0)), rev(GDN_W, 1)],
        out_specs=[rev(3 * GDN_W), rev(GDN_W), rev(2 * GDN_W), pl.BlockSpec((1, GDN_DH), lambda i: (0, 0))],
        out_shape=[jax.ShapeDtypeStruct((s, 3 * GDN_W), F32), jax.ShapeDtypeStruct((s, GDN_W), F32),
                   jax.ShapeDtypeStruct((s, 2 * GDN_W), F32), jax.ShapeDtypeStruct((1, GDN_DH), F32)],
        scratch_shapes=[pltpu.VMEM((h4, GDN_DH, GDN_DH), F32)],
        compiler_params=_params(("arbitrary",)),
    )(cqkv, proj, gbb, gain, states, d_mixed)


def _gdn_gates(small, prm):
    w = GDN_HEADS * GDN_DH
    lane, head = _iota2((128, w), 0), _iota2((128, w), 1) // GDN_DH
    sel_b = (lane == SMALL_B + head).astype(F32)
    sel_a = (lane == SMALL_A + head).astype(F32)
    prow = _iota2((8, 128), 0)
    a_log = jnp.sum(prm * (prow == 0).astype(F32), axis=0, keepdims=True)
    dt_b = jnp.sum(prm * (prow == 1).astype(F32), axis=0, keepdims=True)
    beta = _sigmoid(hdot(small, sel_b))
    g = hdot(-jnp.exp(a_log) * _softplus(small + dt_b), sel_a)
    return g, beta


CONV_ROWS = 1024
CONV_COLS = 128
CONV_BLOCK0 = 1536 // CONV_COLS


def _shift_down(prev8, cur, s):
    ext = jnp.concatenate([prev8, cur], axis=0)
    return pltpu.roll(ext, s, 0)[8:]


def _shift_up(cur, next8, s):
    n = cur.shape[0]
    ext = jnp.concatenate([cur, next8], axis=0)
    return pltpu.roll(ext, n + 8 - s, 0)[:n]


def _conv_pre(x_ref, w, ci, nchunk):
    r0 = pl.multiple_of(ci * CONV_ROWS, CONV_ROWS)
    cur = x_ref[pl.ds(r0, CONV_ROWS), :]
    prev = x_ref[pl.ds(pl.multiple_of(jnp.maximum(r0 - 8, 0), 8), 8), :]
    prev = jnp.where(ci > 0, prev, 0.0)
    shifted = [cur] + [_shift_down(prev, cur, s) for s in range(1, CONV_W)]
    pre = w[CONV_W - 1:CONV_W, :] * cur
    for s in range(1, CONV_W):
        pre = pre + w[CONV_W - 1 - s:CONV_W - s, :] * shifted[s]
    return r0, pre, shifted


def conv_fwd(proj, conv_w8):
    s = proj.shape[0]
    nchunk = s // CONV_ROWS
    ncol = 3 * GDN_HEADS * GDN_DH // CONV_COLS

    def kern(x_ref, w_ref, y_ref):
        w = w_ref[...]

        def step(ci, carry):
            r0, pre, _ = _conv_pre(x_ref, w, ci, nchunk)
            y_ref[pl.ds(r0, CONV_ROWS), :] = _silu(pre)
            return carry

        lax.fori_loop(0, nchunk, step, 0)

    return pl.pallas_call(
        kern, name="conv_fwd", grid=(ncol,),
        in_specs=[pl.BlockSpec((s, CONV_COLS), lambda j: (0, CONV_BLOCK0 + j)),
                  pl.BlockSpec((8, CONV_COLS), lambda j: (0, j))],
        out_specs=pl.BlockSpec((s, CONV_COLS), lambda j: (0, j)),
        out_shape=jax.ShapeDtypeStruct((s, ncol * CONV_COLS), F32),
        compiler_params=_params(("parallel",)),
    )(proj, conv_w8)


def conv_bwd(proj, conv_w8, dy):
    s = proj.shape[0]
    nchunk = s // CONV_ROWS
    per = 3 * GDN_HEADS * GDN_DH // CONV_COLS
    outs = []
    for part in range(1):
        def kern(x_ref, w_ref, dy_ref, dx_ref, dw_ref, dpre_ref):
            w = w_ref[...]
            rows8 = _iota2((8, CONV_COLS), 0)

            def step1(ci, dw):
                r0, pre, shifted = _conv_pre(x_ref, w, ci, nchunk)
                sg = _sigmoid(pre)
                dpre = dy_ref[pl.ds(r0, CONV_ROWS), :] * sg * (1.0 + pre * (1.0 - sg))
                dpre_ref[pl.ds(r0, CONV_ROWS), :] = dpre
                for sh in range(CONV_W):
                    dw = dw + jnp.where(rows8 == CONV_W - 1 - sh, _colsum(dpre * shifted[sh]), 0.0)
                return dw

            dw_ref[...] = lax.fori_loop(0, nchunk, step1, jnp.zeros((8, CONV_COLS), F32))

            def step2(ci, carry):
                r0 = pl.multiple_of(ci * CONV_ROWS, CONV_ROWS)
                cur = dpre_ref[pl.ds(r0, CONV_ROWS), :]
                nxt = dpre_ref[pl.ds(pl.multiple_of(jnp.minimum(r0 + CONV_ROWS, s - 8), 8), 8), :]
                nxt = jnp.where(ci < nchunk - 1, nxt, 0.0)
                dx = w[CONV_W - 1:CONV_W, :] * cur
                for sh in range(1, CONV_W):
                    dx = dx + w[CONV_W - 1 - sh:CONV_W - sh, :] * _shift_up(cur, nxt, sh)
                dx_ref[pl.ds(r0, CONV_ROWS), :] = dx
                return carry

            lax.fori_loop(0, nchunk, step2, 0)

        outs.append(pl.pallas_call(
            kern, name=f"conv_bwd{part}", grid=(per,),
            in_specs=[pl.BlockSpec((s, CONV_COLS), lambda j, part=part: (0, CONV_BLOCK0 + part * per + j)),
                      pl.BlockSpec((8, CONV_COLS), lambda j, part=part: (0, part * per + j)),
                      pl.BlockSpec((s, CONV_COLS), lambda j: (0, j))],
            out_specs=[pl.BlockSpec((s, CONV_COLS), lambda j: (0, j)),
                       pl.BlockSpec((8, CONV_COLS), lambda j: (0, j))],
            out_shape=[jax.ShapeDtypeStruct((s, per * CONV_COLS), F32),
                       jax.ShapeDtypeStruct((8, per * CONV_COLS), F32)],
            scratch_shapes=[pltpu.VMEM((s, CONV_COLS), F32)],
            compiler_params=_params(("parallel",)),
        )(proj, conv_w8, dy))
    dx = jnp.concatenate([o[0] for o in outs], axis=1)
    dw = jnp.concatenate([o[1] for o in outs], axis=1)
    return dx, dw


FOXF_ROWS = 512
SMALL_BLOCK128 = 3584 // 128


def _log_sigmoid(x):
    return jnp.minimum(x, 0.0) - jnp.log(1.0 + jnp.exp(-jnp.abs(x)))


def fox_f_fwd(proj, bias_row):
    s = proj.shape[0]
    n = s // FOXF_ROWS

    def kern(x_ref, b_ref, f_ref, carry):
        @pl.when(pl.program_id(0) == 0)
        def _():
            carry[...] = jnp.zeros_like(carry)

        heads = _iota2((FOXF_ROWS, 128), 1) < FOX_HEADS
        lf = jnp.where(heads, _log_sigmoid(x_ref[...] + b_ref[...]), 0.0)
        ltri = (_iota2((FOXF_ROWS, FOXF_ROWS), 0) >= _iota2((FOXF_ROWS, FOXF_ROWS), 1)).astype(F32)
        c = hdot(ltri, lf) + carry[...]
        f_ref[...] = c
        carry[...] = c[FOXF_ROWS - 1:FOXF_ROWS, :]

    return pl.pallas_call(
        kern, name="fox_f_fwd", grid=(n,),
        in_specs=[pl.BlockSpec((FOXF_ROWS, 128), lambda i: (i, SMALL_BLOCK128)),
                  pl.BlockSpec((1, 128), lambda i: (0, 0))],
        out_specs=pl.BlockSpec((FOXF_ROWS, 128), lambda i: (i, 0)),
        out_shape=jax.ShapeDtypeStruct((s, 128), F32),
        scratch_shapes=[pltpu.VMEM((1, 128), F32)],
        compiler_params=_params(("arbitrary",)),
    )(proj, bias_row)


def fox_f_bwd(proj, bias_row, d_f):
    s = proj.shape[0]
    n = s // FOXF_ROWS

    def kern(x_ref, b_ref, df_ref, dx_ref, db_ref, carry):
        @pl.when(pl.program_id(0) == 0)
        def _():
            carry[...] = jnp.zeros_like(carry)
            db_ref[...] = jnp.zeros_like(db_ref)

        heads = _iota2((FOXF_ROWS, 128), 1) < FOX_HEADS
        utri = (_iota2((FOXF_ROWS, FOXF_ROWS), 0) <= _iota2((FOXF_ROWS, FOXF_ROWS), 1)).astype(F32)
        rc = hdot(utri, df_ref[...]) + carry[...]
        carry[...] = rc[0:1, :]
        dx = jnp.where(heads, rc * _sigmoid(-(x_ref[...] + b_ref[...])), 0.0)
        dx_ref[...] = dx
        db_ref[...] += _colsum(dx)

    return pl.pallas_call(
        kern, name="fox_f_bwd", grid=(n,),
        in_specs=[pl.BlockSpec((FOXF_ROWS, 128), lambda i: (n - 1 - i, SMALL_BLOCK128)),
                  pl.BlockSpec((1, 128), lambda i: (0, 0)),
                  pl.BlockSpec((FOXF_ROWS, 128), lambda i: (n - 1 - i, 0))],
        out_specs=[pl.BlockSpec((FOXF_ROWS, 128), lambda i: (n - 1 - i, 0)),
                   pl.BlockSpec((1, 128), lambda i: (0, 0))],
        out_shape=[jax.ShapeDtypeStruct((s, 128), F32), jax.ShapeDtypeStruct((1, 128), F32)],
        scratch_shapes=[pltpu.VMEM((1, 128), F32)],
        compiler_params=_params(("arbitrary",)),
    )(proj, bias_row, d_f)


FOX_T = 512
FOX_SCALE = FOX_DH ** -0.5
FOX_PAIRS = FOX_HEADS // 2
XL = 8
NEG = -1e30
_NT = (((1,), (1,)), ((), ()))


def _split3(x):
    def bf(v):
        return lax.reduce_precision(v, exponent_bits=8, mantissa_bits=7)

    hi = bf(x)
    mid = bf(x - hi)
    lo = bf(x - hi - mid)
    return jnp.stack([hi, mid, lo], axis=-1)


def _fox_extras(s, first, second):
    def part(v):
        if v is None:
            return jnp.zeros((s, FOX_HEADS, 3), F32)
        if isinstance(v, float):
            return jnp.full((s, FOX_HEADS, 3), v, F32)
        return _split3(v)

    cols = jnp.concatenate([part(first), part(second), jnp.zeros((s, FOX_HEADS, XL - 6), F32)], axis=-1)
    cols = _pad_to(cols.reshape(s, FOX_PAIRS, 2 * XL), (s, FOX_PAIRS, 128))
    return cols.transpose(1, 0, 2).astype(BF)


def _pair_masks(rows):
    lane = _iota2((rows, 256), 1)
    head = jnp.where(lane < 128, lane // FOX_DH, (lane - 128) // XL)
    return head == 0, head == 1


def fox_fwd(qkv, xq, xk, xv):
    s = qkv.shape[0]
    t = min(FOX_T, s)
    n = s // t

    def kern(q_ref, k_ref, v_ref, xq_ref, xk_ref, xv_ref, o_ref, lse_ref):
        i = pl.program_id(1)
        masks = _pair_masks(t)
        q_all = jnp.concatenate([q_ref[...] * FOX_SCALE, xq_ref[...]], axis=1)
        q_ops = [jnp.where(mk, q_all, 0).astype(BF) for mk in masks]

        def step(j, carry, masked):
            sl = pl.ds(pl.multiple_of(j * t, t), t)
            k_op = jnp.concatenate([k_ref[sl, :], xk_ref[sl, :]], axis=1)
            v_op = jnp.concatenate([v_ref[sl, :], xv_ref[sl, :]], axis=1)
            sc = [lax.dot_general(q_ops[e], k_op, _NT, preferred_element_type=F32) for e in range(2)]
            if masked:
                keep = _iota2((t, t), 0) >= _iota2((t, t), 1)
                sc = [jnp.where(keep, x, NEG) for x in sc]
            m_new = [jnp.maximum(carry[e][0], jnp.max(sc[e], axis=1, keepdims=True)) for e in range(2)]
            p = [jnp.exp(sc[e] - m_new[e]).astype(BF) for e in range(2)]
            pv = [jnp.dot(p[e], v_op, preferred_element_type=F32) for e in range(2)]
            return tuple((m_new[e], jnp.exp(carry[e][0] - m_new[e]) * carry[e][1] + pv[e]) for e in range(2))

        init = tuple((jnp.full((t, 1), NEG, F32), jnp.zeros((t, 256), F32)) for _ in range(2))
        carry = lax.fori_loop(0, i, lambda j, c: step(j, c, False), init)
        carry = step(i, carry, True)
        lane = _iota2((t, 256), 1)
        outs, lses = [], []
        for e in range(2):
            m, acc = carry[e]
            l = jnp.sum(jnp.where(lane == 128, acc, 0.0), axis=1, keepdims=True)
            outs.append(acc[:, :128] / l)
            lses.append(m + jnp.log(l))
        lane128 = _iota2((t, 128), 1)
        o_ref[...] = jnp.where(lane128 < FOX_DH, outs[0], outs[1])
        lse_ref[...] = jnp.where(lane128 == 0, lses[0], jnp.where(lane128 == 1, lses[1], 0.0))

    pr = FOX_PAIRS
    return pl.pallas_call(
        kern, name="fox_fwd", grid=(pr, n),
        in_specs=[pl.BlockSpec((t, 128), lambda p, i: (i, p)),
                  pl.BlockSpec((s, 128), lambda p, i: (0, pr + p)),
                  pl.BlockSpec((s, 128), lambda p, i: (0, 2 * pr + p)),
                  pl.BlockSpec((None, t, 128), lambda p, i: (p, i, 0)),
                  pl.BlockSpec((None, s, 128), lambda p, i: (p, 0, 0)),
                  pl.BlockSpec((None, s, 128), lambda p, i: (p, 0, 0))],
        out_specs=[pl.BlockSpec((t, 128), lambda p, i: (i, p)),
                   pl.BlockSpec((None, t, 128), lambda p, i: (p, i, 0))],
        out_shape=[jax.ShapeDtypeStruct((s, FOX_HEADS * FOX_DH), F32), jax.ShapeDtypeStruct((pr, s, 128), F32)],
        compiler_params=_params(("parallel", "parallel")),
    )(qkv, qkv, qkv, xq, xk, xv)


def fox_bwd(qkv, d_o, xk, xv, xqb, xdo):
    s = qkv.shape[0]
    t = min(FOX_T, s)
    n = s // t

    def kern(k_ref, v_ref, xk_ref, xv_ref, q_ref, do_ref, xq_ref, xd_ref, dq_ref, dk_ref, dv_ref):
        j = pl.program_id(1)

        @pl.when(j == 0)
        def _():
            dq_ref[...] = jnp.zeros_like(dq_ref)

        masks = _pair_masks(t)
        k_op = jnp.concatenate([k_ref[...], xk_ref[...]], axis=1)
        v_op = jnp.concatenate([v_ref[...], xv_ref[...]], axis=1)
        k_ops = [jnp.where(mk, k_op, 0).astype(BF) for mk in masks]

        def step(i, carry, masked):
            dk, dv = carry
            sl = pl.ds(pl.multiple_of(i * t, t), t)
            q_all = jnp.concatenate([q_ref[sl, :] * FOX_SCALE, xq_ref[sl, :]], axis=1)
            do_all = jnp.concatenate([do_ref[sl, :], xd_ref[sl, :]], axis=1)
            q_ops = [jnp.where(mk, q_all, 0).astype(BF) for mk in masks]
            do_ops = [jnp.where(mk, do_all, 0).astype(BF) for mk in masks]
            st = [lax.dot_general(k_op, q_ops[e], _NT, preferred_element_type=F32) for e in range(2)]
            dp = [lax.dot_general(v_op, do_ops[e], _NT, preferred_element_type=F32) for e in range(2)]
            if masked:
                keep = _iota2((t, t), 0) <= _iota2((t, t), 1)
                st = [jnp.where(keep, x, NEG) for x in st]
            pt = [jnp.exp(x) for x in st]
            dsb = [(pt[e] * dp[e]).astype(BF) for e in range(2)]
            for e in range(2):
                dv = dv + jnp.dot(pt[e].astype(BF), do_ops[e][:, :128], preferred_element_type=F32)
                dk = dk + jnp.dot(dsb[e], q_ops[e], preferred_element_type=F32)
            dq_ref[sl, :] += sum(lax.dot_general(dsb[e], k_ops[e], (((0,), (0,)), ((), ())),
                                                 preferred_element_type=F32) for e in range(2))
            return dk, dv

        init = (jnp.zeros((t, 256), F32), jnp.zeros((t, 128), F32))
        carry = step(j, init, True)
        dk, dv = lax.fori_loop(j + 1, n, lambda i, c: step(i, c, False), carry)
        dk_ref[...] = dk
        dv_ref[...] = dv

    pr = FOX_PAIRS
    return pl.pallas_call(
        kern, name="fox_bwd", grid=(pr, n),
        in_specs=[pl.BlockSpec((t, 128), lambda p, j: (j, pr + p)),
                  pl.BlockSpec((t, 128), lambda p, j: (j, 2 * pr + p)),
                  pl.BlockSpec((None, t, 128), lambda p, j: (p, j, 0)),
                  pl.BlockSpec((None, t, 128), lambda p, j: (p, j, 0)),
                  pl.BlockSpec((s, 128), lambda p, j: (0, p)),
                  pl.BlockSpec((s, 128), lambda p, j: (0, p)),
                  pl.BlockSpec((None, s, 128), lambda p, j: (p, 0, 0)),
                  pl.BlockSpec((None, s, 128), lambda p, j: (p, 0, 0))],
        out_specs=[pl.BlockSpec((s, 256), lambda p, j: (0, p)),
                   pl.BlockSpec((t, 256), lambda p, j: (j, p)),
                   pl.BlockSpec((t, 128), lambda p, j: (j, p))],
        out_shape=[jax.ShapeDtypeStruct((s, pr * 256), F32), jax.ShapeDtypeStruct((s, pr * 256), F32),
                   jax.ShapeDtypeStruct((s, FOX_HEADS * FOX_DH), F32)],
        compiler_params=_params(("parallel", "arbitrary")),
    )(qkv, qkv, xk, xv, qkv, d_o, xqb, xdo)


def _xattn_head(q, k, v):
    sc = bdot(q, k, 1, 1) * (MEM_DH ** -0.5)
    e = jnp.exp(sc - lax.stop_gradient(jnp.max(sc, axis=-1, keepdims=True)))
    p = e / jnp.sum(e, axis=-1, keepdims=True)
    return bdot(p, v, 1, 0)


def xattn_fwd(q, kv):
    s = q.shape[0]
    hh = MEM_HEADS

    def body(*vals):
        qs, ks, vs = vals[:hh], vals[hh:2 * hh], vals[2 * hh:]
        return jnp.concatenate([_xattn_head(qs[a], ks[a], vs[a]) for a in range(hh)], axis=1)

    return rowcall(body, [(q, MEM_DH, a) for a in range(hh)],
                   [(kv, MEM_DH, a) for a in range(2 * hh)],
                   [(hh * MEM_DH, BF)], rows=512, total=s, name="xattn_fwd")[0]


def xattn_bwd(q, kv, d_o):
    s = q.shape[0]
    hh = MEM_HEADS

    def body(*vals):
        qs, dos = vals[:hh], vals[hh:2 * hh]
        ks, vs = vals[2 * hh:3 * hh], vals[3 * hh:]
        dqs, dks, dvs = [], [], []
        for a in range(hh):
            _, vjp = jax.vjp(_xattn_head, qs[a], ks[a], vs[a])
            dq, dk, dv = vjp(dos[a])
            dqs.append(dq)
            dks.append(dk)
            dvs.append(dv)
        return jnp.concatenate(dqs, axis=1), jnp.concatenate(dks + dvs, axis=1)

    return rowcall(body, [(q, MEM_DH, a) for a in range(hh)] + [(d_o, MEM_DH, a) for a in range(hh)],
                   [(kv, MEM_DH, a) for a in range(2 * hh)],
                   [(hh * MEM_DH, BF)], [kv.shape], rows=512, total=s, name="xattn_bwd")


def _slab(ref, axis, start, size):
    if axis is None:
        return ref
    if axis == "lead":
        return ref.at[start]
    idx = pl.ds(pl.multiple_of(start, 128 if axis == 1 else 16), size)
    return ref.at[idx] if axis == 0 else ref.at[:, idx]


def exchange(inputs, outputs, transfers, name):
    ni, no, nt = len(inputs), len(outputs), len(transfers)
    npeer = N_DEV - 1

    def body(*refs):
        ins, outs = refs[:ni], refs[ni:ni + no]
        send, recv, loc = refs[ni + no:]
        x, y, c = lax.axis_index("x"), lax.axis_index("y"), lax.axis_index("c")
        me = 4 * x + 2 * y + c

        def peer(p):
            px = 1 - x if p & 4 else x
            py = 1 - y if p & 2 else y
            pc = 1 - c if p & 1 else c
            return (px, py, pc), 4 * px + 2 * py + pc

        def view(ref, spec, who):
            axis, off, stride, size = spec
            return _slab(ref, axis, off + who * stride, size)

        local, remote = [], []
        for w, (ii, src, oi, dst) in enumerate(transfers):
            cp = pltpu.make_async_copy(view(ins[ii], src, me), view(outs[oi], dst, me), loc.at[w])
            cp.start()
            local.append(cp)
        for p in range(1, N_DEV):
            dev, idx = peer(p)
            for w, (ii, src, oi, dst) in enumerate(transfers):
                k = w * npeer + p - 1
                out_cp = pltpu.make_async_remote_copy(
                    src_ref=view(ins[ii], src, idx), dst_ref=view(outs[oi], dst, me), send_sem=send.at[k],
                    recv_sem=recv.at[k], device_id=dev, device_id_type=MESH)
                out_cp.start()
                in_cp = pltpu.make_async_remote_copy(
                    src_ref=view(ins[ii], src, idx), dst_ref=view(outs[oi], dst, idx), send_sem=send.at[k],
                    recv_sem=recv.at[k], device_id=dev, device_id_type=MESH)
                remote.append((out_cp, in_cp))
        for out_cp, in_cp in remote:
            in_cp.wait_recv()
            out_cp.wait_send()
        for cp in local:
            cp.wait()

    hbm = pl.BlockSpec(memory_space=pl.ANY)
    return pl.pallas_call(
        body, name=name, in_specs=[hbm] * ni, out_specs=[hbm] * no, out_shape=list(outputs),
        scratch_shapes=[pltpu.SemaphoreType.DMA((nt * npeer,)), pltpu.SemaphoreType.DMA((nt * npeer,)),
                        pltpu.SemaphoreType.DMA((nt,))],
        compiler_params=pltpu.CompilerParams(has_side_effects=True),
    )(*inputs)


def _peer(p):
    x, y, c = lax.axis_index("x"), lax.axis_index("y"), lax.axis_index("c")
    px = 1 - x if p & 4 else x
    py = 1 - y if p & 2 else y
    pc = 1 - c if p & 1 else c
    return (px, py, pc), 4 * px + 2 * py + pc


def _view(ref, spec, who):
    axis, off, stride, size = spec
    return _slab(ref, axis, off + who * stride, size)


def place_local(inputs, outputs, transfers, name):
    ni, no, nt = len(inputs), len(outputs), len(transfers)

    def body(*refs):
        ins, outs, loc = refs[:ni], refs[ni:ni + no], refs[ni + no]
        me = 4 * lax.axis_index("x") + 2 * lax.axis_index("y") + lax.axis_index("c")
        copies = [pltpu.make_async_copy(_view(ins[ii], src, me), _view(outs[oi], dst, me), loc.at[w])
                  for w, (ii, src, oi, dst) in enumerate(transfers)]
        for cp in copies:
            cp.start()
        for cp in copies:
            cp.wait()

    hbm = pl.BlockSpec(memory_space=pl.ANY)
    return pl.pallas_call(
        body, name=name, in_specs=[hbm] * ni, out_specs=[hbm] * no, out_shape=list(outputs),
        scratch_shapes=[pltpu.SemaphoreType.DMA((nt,))],
    )(*inputs)


_HBM = pl.BlockSpec(memory_space=pltpu.HBM)
_SEM = pl.BlockSpec(memory_space=pltpu.SEMAPHORE)
_EFFECT = pltpu.SideEffectType.DATAFLOW_SIDE_EFFECTING


def _remote_copies(ins, lands, transfers, send, recv):
    npeer = N_DEV - 1
    me = 4 * lax.axis_index("x") + 2 * lax.axis_index("y") + lax.axis_index("c")
    pairs = []
    for p in range(1, N_DEV):
        dev, idx = _peer(p)
        for w, (ii, src, oi, dst) in enumerate(transfers):
            k = w * npeer + p - 1
            common = dict(src_ref=_view(ins[ii], src, idx), send_sem=send.at[k], recv_sem=recv.at[k],
                          device_id=dev, device_id_type=MESH)
            pairs.append((pltpu.make_async_remote_copy(dst_ref=_view(lands[oi], dst, me), **common),
                          pltpu.make_async_remote_copy(dst_ref=_view(lands[oi], dst, idx), **common)))
    return pairs


def exchange_start(inputs, lands, transfers, name):
    ni, nl, nsem = len(inputs), len(lands), len(transfers) * (N_DEV - 1)

    def body(*refs):
        ins, lnd = refs[:ni], refs[ni:ni + nl]
        send, recv = refs[ni + nl], refs[ni + nl + 1]
        token = refs[-1]
        for out_cp, _ in _remote_copies(ins, lnd, transfers, send, recv):
            out_cp.start()
        token[...] = jnp.zeros_like(token)

    args = [pltpu.with_memory_space_constraint(a, pltpu.HBM) for a in list(inputs) + list(lands)]
    res = pl.pallas_call(
        body, name=name,
        out_shape=(pltpu.SemaphoreType.DMA((nsem,)), pltpu.SemaphoreType.DMA((nsem,)),
                   *[pltpu.HBM(a.shape, a.dtype) for a in args], jax.ShapeDtypeStruct((8, 128), F32)),
        in_specs=[_HBM] * (ni + nl),
        out_specs=(_SEM, _SEM, *[_HBM] * (ni + nl), pl.BlockSpec(memory_space=pltpu.VMEM)),
        input_output_aliases={k: k + 2 for k in range(ni + nl)},
        compiler_params=pltpu.CompilerParams(has_side_effects=_EFFECT),
    )(*args)
    return res[0], res[1], list(res[2:2 + ni]), list(res[2 + ni:2 + ni + nl]), res[-1]


def exchange_wait(send, recv, inputs, lands, after, transfers, name):
    ni, nl = len(inputs), len(lands)

    def body(*refs):
        ins, lnd = refs[:ni], refs[ni:ni + nl]
        send_r, recv_r = refs[ni + nl], refs[ni + nl + 1]
        for out_cp, in_cp in _remote_copies(ins, lnd, transfers, send_r, recv_r):
            out_cp.wait_send()
            in_cp.wait_recv()

    res = pl.pallas_call(
        body, name=name,
        out_shape=tuple(pltpu.HBM(a.shape, a.dtype) for a in list(inputs) + list(lands)),
        in_specs=[_HBM] * (ni + nl) + [_SEM, _SEM, pl.BlockSpec(memory_space=pl.ANY)],
        out_specs=tuple([_HBM] * (ni + nl)),
        input_output_aliases={k: k for k in range(ni + nl)},
        compiler_params=pltpu.CompilerParams(has_side_effects=_EFFECT),
    )(*inputs, *lands, send, recv, after)
    return list(res[ni:])


def adamw(w, m, v, contribs, name):
    r, c = w.shape
    nc = len(contribs)
    rows = next((r // d for d in (4, 2) if r % d == 0 and (r // d) % 16 == 0), r)
    c1, c2 = 1.0 - ADAM_B1 ** ADAM_STEP, 1.0 - ADAM_B2 ** ADAM_STEP

    def body(wv, mv, vv, *gs):
        g = gs[0].astype(F32)
        for extra in gs[1:]:
            g = g + extra.astype(F32)
        g = g[:, :c]
        m_new = ADAM_B1 * mv + (1.0 - ADAM_B1) * g
        v_new = ADAM_B2 * vv + (1.0 - ADAM_B2) * (g * g)
        delta = -ADAM_LR * ((m_new / c1) / (jnp.sqrt(v_new / c2) + ADAM_EPS) + ADAM_WD * wv)
        return g, delta, m_new, v_new

    assert nc >= 1
    return rowcall(body, [w, m, v] + list(contribs), [], [(c, F32)] * 4, rows=rows, total=r, name=name)


WEIGHTS = ['ffn1_pre_norm', 'ffn1_w_gate', 'ffn1_w_up', 'ffn1_w_down', 'ffn1_post_norm', 'mix_pre_norm', 'w_in',
           'fox_f_bias', 'gdn_conv_w', 'gdn_a_log', 'gdn_dt_bias', 'gdn_out_norm', 'w_out', 'mix_post_norm',
           'mem_pre_norm', 'mem_kv_norm', 'mem_w_q', 'mem_w_kv', 'mem_w_o', 'mem_post_norm', 'ffn2_pre_norm',
           'ffn2_w_gate', 'ffn2_w_up', 'ffn2_w_down', 'ffn2_post_norm']
GAINS = ['ffn1_pre_norm', 'ffn1_post_norm', 'mix_pre_norm', 'mix_post_norm', 'mem_pre_norm', 'mem_kv_norm',
         'mem_post_norm', 'ffn2_pre_norm', 'ffn2_post_norm']
BIG = ['ffn1_w_gate', 'ffn1_w_up', 'ffn1_w_down', 'w_in', 'w_out', 'mem_w_q', 'mem_w_kv', 'mem_w_o',
       'ffn2_w_gate', 'ffn2_w_up', 'ffn2_w_down']
PACK_ROWS = 24
ROW_MISC = len(GAINS)
ROW_CONV = ROW_MISC + 1
COL_FBIAS, COL_ALOG, COL_DTB, COL_ONORM, COL_LOSS = 0, 8, 12, 128, 256
CONV_CH = 3 * GDN_HEADS * GDN_DH


def _pad_to(a, shape):
    return jnp.pad(a, [(0, t - s) for s, t in zip(a.shape, shape)])


def _pack(get, conv=None, loss=None):
    rows = [get(nm) for nm in GAINS]
    misc = jnp.concatenate([get('fox_f_bias'), get('gdn_a_log'), get('gdn_dt_bias'),
                            jnp.zeros((1, COL_ONORM - COL_DTB - 4), F32), get('gdn_out_norm'),
                            jnp.zeros((1, 1), F32) if loss is None else loss.reshape(1, 1)], axis=1)
    rows.append(_pad_to(misc, (1, D_MODEL)))
    rows.append(jnp.zeros((6, D_MODEL), F32) if conv is None else conv.reshape(6, D_MODEL))
    return _pad_to(jnp.concatenate(rows, axis=0), (PACK_ROWS, D_MODEL))


def _unpack(p):
    out = {nm: p[i:i + 1] for i, nm in enumerate(GAINS)}
    misc = p[ROW_MISC:ROW_MISC + 1]
    out['fox_f_bias'] = misc[:, COL_FBIAS:COL_FBIAS + FOX_HEADS]
    out['gdn_a_log'] = misc[:, COL_ALOG:COL_ALOG + GDN_HEADS]
    out['gdn_dt_bias'] = misc[:, COL_DTB:COL_DTB + GDN_HEADS]
    out['gdn_out_norm'] = misc[:, COL_ONORM:COL_ONORM + GDN_DH]
    return out


def _ffn_fwd(h, pre, wgu, wd, tag):
    s = h.shape[0]
    u, = rowcall(_rms, [h], [pre], [(D_MODEL, BF)], rows=512, total=s, name=tag + "_pre")
    gu = mm(u, wgu, name=tag + "_gate_up")
    act, = rowcall(lambda a, b: _silu(a) * b, [(gu, D_FF_PAD, 0), (gu, D_FF_PAD, 1)], [], [(D_FF_PAD, BF)],
                   rows=256, total=s, name=tag + "_act")
    f = mm(act, wd, name=tag + "_down")
    return u, gu, act, f


def _half_rms(a, g):
    return 0.5 * _rms(a, g)


def _ffn_bwd(dh_out, h, pre, post, wgu, wd, saved, tag):
    u, gu, act, f = saved
    s = h.shape[0]

    def b_post(dh, fv, pg):
        return jax.vjp(_half_rms, fv, pg)[1](dh)

    df, dpost = rowcall(b_post, [dh_out, f], [post], [(D_MODEL, BF)], [(1, D_MODEL)], rows=512, total=s,
                        name=tag + "_bwd_post")
    dact = mm(df, wd, tb=True, name=tag + "_bwd_dact")
    dwd = mm(act, df, ta=True, out_dtype=BF, name=tag + "_bwd_dwd")

    def b_act(a, b, da):
        dg, du = jax.vjp(lambda g_, u_: _silu(g_) * u_, a, b)[1](da)
        return jnp.concatenate([dg, du], axis=1)

    dgu, = rowcall(b_act, [(gu, D_FF_PAD, 0), (gu, D_FF_PAD, 1), dact], [], [(2 * D_FF_PAD, BF)], rows=256, total=s,
                   name=tag + "_bwd_act")
    du = mm(dgu, wgu, tb=True, name=tag + "_bwd_du")
    dwgu = mm(u, dgu, ta=True, out_dtype=BF, name=tag + "_bwd_dwgu")

    def b_pre(dh, duv, hv, pg):
        dx, dpre = jax.vjp(_rms, hv, pg)[1](duv)
        return dh + dx, dpre

    dh, dpre = rowcall(b_pre, [dh_out, du, h], [pre], [(D_MODEL, F32)], [(1, D_MODEL)], rows=512, total=s,
                       name=tag + "_bwd_pre")
    return dh, dwgu, dwd, dpre, dpost


def _residual_rms(h, a, g):
    return h + _rms(a, g)


def _bwd_residual(dh, a, g):
    return jax.vjp(_rms, a, g)[1](dh)


def _step(a):
    x, mem = a['x'][0], a['mem'][0]
    s = x.shape[0]
    me = 4 * lax.axis_index("x") + 2 * lax.axis_index("y") + lax.axis_index("c")
    w2 = {nm: a[nm][0] for nm in WEIGHTS}
    m2 = {nm: a['m_' + nm][0] for nm in WEIGHTS}
    v2 = {nm: a['v_' + nm][0] for nm in WEIGHTS}
    small = {nm: w2[nm][None] for nm in WEIGHTS if nm not in BIG and nm != 'gdn_conv_w'}

    def ff_cols(w):
        return _pad_to(w, (D_MODEL, FF_SHARD_PAD)).astype(BF)

    def ff_rows(w):
        return _pad_to(w, (FF_SHARD_PAD, D_MODEL)).astype(BF)

    whole = (None, 0, 0, 0)
    conv_pad = 256
    g_in = [ff_cols(w2['ffn1_w_gate']), ff_cols(w2['ffn1_w_up']), ff_rows(w2['ffn1_w_down']),
            ff_cols(w2['ffn2_w_gate']), ff_cols(w2['ffn2_w_up']), ff_rows(w2['ffn2_w_down']),
            _pad_to(w2['w_in'], (D_MODEL, IN_SHARD_PAD)).astype(BF), w2['w_out'].astype(BF),
            w2['mem_w_q'].astype(BF), w2['mem_w_kv'].astype(BF), w2['mem_w_o'].astype(BF),
            _pad_to(w2['gdn_conv_w'], (8, conv_pad))]
    g_out = [jax.ShapeDtypeStruct((D_MODEL, 2 * D_FF_PAD), BF), jax.ShapeDtypeStruct((D_FF_PAD, D_MODEL), BF),
             jax.ShapeDtypeStruct((D_MODEL, 2 * D_FF_PAD), BF), jax.ShapeDtypeStruct((D_FF_PAD, D_MODEL), BF),
             jax.ShapeDtypeStruct((D_MODEL, N_DEV * IN_SHARD_PAD), BF), jax.ShapeDtypeStruct((D_MODEL, D_MODEL), BF),
             jax.ShapeDtypeStruct((D_MODEL, D_MODEL), BF), jax.ShapeDtypeStruct((D_MODEL, 2 * D_MODEL), BF),
             jax.ShapeDtypeStruct((D_MODEL, D_MODEL), BF), jax.ShapeDtypeStruct((8, N_DEV * conv_pad), F32)]
    sp_, dm = FF_SHARD_PAD, D_MODEL // N_DEV
    g_tr = [(0, whole, 0, (1, 0, sp_, sp_)), (1, whole, 0, (1, D_FF_PAD, sp_, sp_)), (2, whole, 1, (0, 0, sp_, sp_)),
            (3, whole, 2, (1, 0, sp_, sp_)), (4, whole, 2, (1, D_FF_PAD, sp_, sp_)), (5, whole, 3, (0, 0, sp_, sp_)),
            (6, whole, 4, (1, 0, IN_SHARD_PAD, IN_SHARD_PAD)), (7, whole, 5, (0, 0, dm, dm)),
            (8, whole, 6, (0, 0, dm, dm)), (9, whole, 7, (1, 0, 2 * dm, 2 * dm)), (10, whole, 8, (0, 0, dm, dm)),
            (11, whole, 9, (1, 0, conv_pad, conv_pad))]
    wgu1, wd1 = exchange(g_in[:3], g_out[:2], g_tr[:3], "gather_ffn1")
    rest_in, rest_out = g_in[3:], g_out[2:]
    rest_tr = [(ii - 3, src, oi - 2, dst) for ii, src, oi, dst in g_tr[3:]]
    rest_land = place_local(rest_in, rest_out, rest_tr, "gather_rest_local")
    g_send, g_recv, g_src, g_land, g_token = exchange_start(rest_in, rest_land, rest_tr, "gather_rest_start")
    bias_row = _pad_to(small['fox_f_bias'], (1, 128))
    gate_prm = _pad_to(jnp.concatenate([_pad_to(small['gdn_a_log'], (1, 128 - SMALL_A)),
                                        _pad_to(small['gdn_dt_bias'], (1, 128 - SMALL_A))], axis=0),
                       (8, 128 - SMALL_A))
    gate_prm = jnp.pad(gate_prm, ((0, 0), (SMALL_A, 0)))
    onorm = small['gdn_out_norm']

    sv1 = _ffn_fwd(x, small['ffn1_pre_norm'] + g_token[0, 0], wgu1, wd1, "ffn1")
    h1, = rowcall(lambda h, f, g: h + _half_rms(f, g), [x, sv1[3]], [small['ffn1_post_norm']], [(D_MODEL, F32)],
                  rows=512, total=s, name="ffn1_out")
    wgu2, wd2, w_in_g, w_out, w_q, w_kv, w_o, conv_g = exchange_wait(g_send, g_recv, g_src, g_land, h1, rest_tr,
                                                                   "gather_rest_wait")
    w_in = jnp.concatenate([w_in_g[:, j * IN_SHARD_PAD:j * IN_SHARD_PAD + IN_SHARD] for j in range(N_DEV)],
                           axis=1)
    sp = [0, 512, 1024, 1536, 1544, 2056, 2568, 3080, 3592, 3596, 3600]
    fq, fk, fv, ff, gq, gk, gv, gz, gb, ga = [w_in[:, sp[i]:sp[i + 1]] for i in range(10)]
    w_proj = jnp.concatenate([fq, fk, fv, gq, gk, gv, gz, ff, gb, ga,
                              jnp.zeros((D_MODEL, PROJ_W - 3584 - 16), BF)], axis=1)
    conv_w8 = conv_g.reshape(8, N_DEV, conv_pad)[:, :, :CONV_CH // N_DEV].reshape(8, CONV_CH)


    u2, = rowcall(_rms, [h1], [small['mix_pre_norm']], [(D_MODEL, BF)], rows=512, total=s, name="mix_pre")
    proj = mm(u2, w_proj, name="mix_proj")
    f_cum = fox_f_fwd(proj, bias_row)
    f_heads = f_cum[:, :FOX_HEADS]
    qkv_bf = proj[:, :3 * FOX_HEADS * FOX_DH].astype(BF)
    xk, xv = _fox_extras(s, 1.0, -f_heads), _fox_extras(s, 1.0, None)
    fox_flat, lse = fox_fwd(qkv_bf, _fox_extras(s, f_heads, 1.0), xk, xv)
    lse_heads = lse[:, :, :2].transpose(1, 0, 2).reshape(s, FOX_HEADS)
    cqkv = conv_fwd(proj, conv_w8)
    g_l, b_l = rowcall(_gdn_gates, [(proj, 128, SMALL_BLOCK128)], [gate_prm], [(512, F32), (512, F32)],
                       rows=512, total=s, name="gdn_gates")
    gbb = jnp.concatenate([g_l, b_l], axis=1)
    gdn_o, states = gdn_fwd(cqkv, proj, gbb, onorm)
    mixed = jnp.concatenate([fox_flat, gdn_o], axis=1).astype(BF)
    mo = mm(mixed, w_out, name="mix_out")
    h2, = rowcall(_residual_rms, [h1, mo], [small['mix_post_norm']], [(D_MODEL, F32)], rows=512, total=s,
                  name="mix_res")

    hq, = rowcall(_rms, [h2], [small['mem_pre_norm']], [(D_MODEL, BF)], rows=512, total=s, name="mem_pre")
    mn, = rowcall(_rms, [mem], [small['mem_kv_norm']], [(D_MODEL, BF)], rows=256, total=mem.shape[0], name="mem_kvn")
    q_mem = mm(hq, w_q, name="mem_q")
    kv_mem = mm(mn, w_kv, name="mem_kv")
    o_mem = xattn_fwd(q_mem, kv_mem)
    c_mem = mm(o_mem, w_o, name="mem_o")
    h3, = rowcall(_residual_rms, [h2, c_mem], [small['mem_post_norm']], [(D_MODEL, F32)], rows=512, total=s,
                  name="mem_res")

    sv2 = _ffn_fwd(h3, small['ffn2_pre_norm'], wgu2, wd2, "ffn2")

    def b_loss(h, f, tgt, g):
        err = h + _half_rms(f, g) - tgt
        part = 0.5 * jnp.sum(jnp.mean(err * err, axis=-1, keepdims=True), axis=0, keepdims=True)
        return err * (1.0 / D_MODEL), jnp.broadcast_to(part, (1, 128))

    dy, loss_acc = rowcall(b_loss, [h3, sv2[3], a['loss_target'][0]], [small['ffn2_post_norm']], [(D_MODEL, F32)],
                           [(1, 128)], rows=512, total=s, name="loss")

    grads = {}
    dh3, dwgu2, dwd2, grads['ffn2_pre_norm'], grads['ffn2_post_norm'] = _ffn_bwd(
        dy, h3, small['ffn2_pre_norm'], small['ffn2_post_norm'], wgu2, wd2, sv2, "ffn2")

    lead = ("lead", 0, 1, 0)

    def land(r, c, dt=BF):
        return jax.ShapeDtypeStruct((N_DEV, r, c), dt)

    ffn_tr = [(0, (1, 0, sp_, sp_), 0, lead), (0, (1, D_FF_PAD, sp_, sp_), 1, lead), (1, (0, 0, sp_, FF_SHARD), 2, lead)]
    ffn_land = [land(D_MODEL, sp_), land(D_MODEL, sp_), land(FF_SHARD, D_MODEL)]
    a_land = place_local([dwgu2, dwd2], ffn_land, ffn_tr, "reduce_ffn2_local")
    a_send, a_recv, a_src, a_land, a_token = exchange_start([dwgu2, dwd2], a_land, ffn_tr, "reduce_ffn2_start")

    dc, grads['mem_post_norm'] = rowcall(_bwd_residual, [dh3, c_mem], [small['mem_post_norm'] + a_token[0, 0]],
                                         [(D_MODEL, BF)],
                                         [(1, D_MODEL)], rows=512, total=s, name="mem_bwd_res")
    d_o = mm(dc, w_o, tb=True, name="mem_bwd_do")
    dw_o = mm(o_mem, dc, ta=True, out_dtype=BF, name="mem_bwd_dwo")
    dq_mem, dkv = xattn_bwd(q_mem, kv_mem, d_o)
    dhq = mm(dq_mem, w_q, tb=True, name="mem_bwd_dhq")
    dw_q = mm(hq, dq_mem, ta=True, out_dtype=BF, name="mem_bwd_dwq")
    dmn = mm(dkv, w_kv, tb=True, name="mem_bwd_dmn")
    dw_kv = mm(mn, dkv, ta=True, out_dtype=BF, name="mem_bwd_dwkv")
    _, grads['mem_kv_norm'] = rowcall(lambda d, mv, g: jax.vjp(_rms, mv, g)[1](d), [dmn, mem],
                                      [small['mem_kv_norm']], [(D_MODEL, F32)], [(1, D_MODEL)], rows=256,
                                      total=mem.shape[0], name="mem_bwd_kvn")

    def b_pre(dh, duv, hv, pg):
        dx, dpre = jax.vjp(_rms, hv, pg)[1](duv)
        return dh + dx, dpre

    dh2, grads['mem_pre_norm'] = rowcall(b_pre, [dh3, dhq, h2], [small['mem_pre_norm']], [(D_MODEL, F32)],
                                         [(1, D_MODEL)], rows=512, total=s, name="mem_bwd_pre")

    dmo, grads['mix_post_norm'] = rowcall(_bwd_residual, [dh2, mo], [small['mix_post_norm']], [(D_MODEL, BF)],
                                          [(1, D_MODEL)], rows=512, total=s, name="mix_bwd_res")
    d_mixed = mm(dmo, w_out, tb=True, name="mix_bwd_dmixed")
    dw_out = mm(mixed, dmo, ta=True, out_dtype=BF, name="mix_bwd_dwout")
    def b_delta(do, o):
        sel = (_iota2((512, 128), 0) // FOX_DH == _iota2((512, 128), 1)).astype(F32)
        return hdot(do * o, sel)

    delta, = rowcall(b_delta, [(d_mixed, 512, 0), fox_flat], [], [(128, F32)], rows=512, total=s, name="fox_delta")
    dqf, dkf, dvf = fox_bwd(qkv_bf, d_mixed[:, :512].astype(BF), xk, xv, _fox_extras(s, f_heads - lse_heads, 1.0),
                            _fox_extras(s, -delta[:, :FOX_HEADS], None))
    dqf, dkf = dqf.reshape(s, FOX_PAIRS, 256), dkf.reshape(s, FOX_PAIRS, 256)
    dfox_q = dqf[:, :, :128].reshape(s, 512) * FOX_SCALE
    dfox_k = dkf[:, :, :128].reshape(s, 512)
    df_q = dqf[:, :, 128:128 + 2 * XL].reshape(s, FOX_HEADS, XL)[:, :, 0]
    df_k = dkf[:, :, 128:128 + 2 * XL].reshape(s, FOX_HEADS, XL)[:, :, 3]
    d_f = _pad_to(df_q - df_k, (s, 128))
    dsmall_f, dbias = fox_f_bwd(proj, bias_row, d_f)
    grads['fox_f_bias'] = dbias[:, :FOX_HEADS]
    dcqkv, dz, dgb, grads['gdn_out_norm'] = gdn_bwd(cqkv, proj, gbb, onorm, states, d_mixed)

    def b_gates(sm, dsf, dg, db, prm):
        dsm, dprm = jax.vjp(_gdn_gates, sm, prm)[1]((dg, db))
        return dsm + dsf, dprm

    dsmall, dprm = rowcall(b_gates, [(proj, 128, SMALL_BLOCK128), dsmall_f, (dgb, 512, 0), (dgb, 512, 1)], [gate_prm],
                           [(128, F32)],
                           [(8, 128)], rows=512, total=s, name="gdn_bwd_gates")
    grads['gdn_a_log'] = dprm[0:1, SMALL_A:SMALL_A + GDN_HEADS]
    grads['gdn_dt_bias'] = dprm[1:2, SMALL_A:SMALL_A + GDN_HEADS]
    dqkv_pre, dconv8 = conv_bwd(proj, conv_w8, dcqkv)
    dproj = jnp.concatenate([dfox_q, dfox_k, dvf, dqkv_pre, dz, dsmall,
                             jnp.zeros((s, PROJ_W - 3584 - 128), F32)], axis=1).astype(BF)
    du2 = mm(dproj, w_proj, tb=True, name="mix_bwd_du")
    dw_proj = mm(u2, dproj, ta=True, out_dtype=BF, name="mix_bwd_dwproj")
    dh1, grads['mix_pre_norm'] = rowcall(b_pre, [dh2, du2, h1], [small['mix_pre_norm']], [(D_MODEL, F32)],
                                         [(1, D_MODEL)], rows=512, total=s, name="mix_bwd_pre")

    dw_in = jnp.concatenate([dw_proj[:, :1536], dw_proj[:, 3584:3592], dw_proj[:, 1536:3584],
                             dw_proj[:, 3592:3600]], axis=1)
    gap = jnp.zeros((D_MODEL, IN_SHARD_PAD - IN_SHARD), BF)
    dw_in = jnp.concatenate([piece for j in range(N_DEV) for piece in (dw_in[:, j * IN_SHARD:(j + 1) * IN_SHARD], gap)],
                            axis=1)
    b_in = [dw_in, dw_out, dw_q, dw_kv, dw_o]
    b_tr = [(0, (1, 0, IN_SHARD_PAD, IN_SHARD_PAD), 0, lead), (1, (0, 0, dm, dm), 1, lead), (2, (0, 0, dm, dm), 2, lead),
            (3, (1, 0, 2 * dm, 2 * dm), 3, lead), (4, (0, 0, dm, dm), 4, lead)]
    b_shapes = [land(D_MODEL, IN_SHARD_PAD), land(dm, D_MODEL), land(dm, D_MODEL), land(D_MODEL, 2 * dm),
                land(dm, D_MODEL)]
    b_land = place_local(b_in, b_shapes, b_tr, "reduce_mix_local")
    b_send, b_recv, b_src, b_land, b_token = exchange_start(b_in, b_land, b_tr, "reduce_mix_start")

    grad_x, dwgu1, dwd1, grads['ffn1_pre_norm'], grads['ffn1_post_norm'] = _ffn_bwd(
        dh1, x, small['ffn1_pre_norm'], small['ffn1_post_norm'] + b_token[0, 0], wgu1, wd1, sv1, "ffn1")

    gpack = _pack(lambda nm: grads[nm], conv=dconv8[:CONV_W], loss=loss_acc[:, :1])
    c_out = exchange([dwgu1, dwd1, gpack], ffn_land + [land(PACK_ROWS, D_MODEL, F32)],
                     ffn_tr + [(2, whole, 3, lead)], "reduce_ffn1")
    gsum_parts = c_out[3]
    a_got = exchange_wait(a_send, a_recv, a_src, a_land, gsum_parts, ffn_tr, "reduce_ffn2_wait")
    b_got = exchange_wait(b_send, b_recv, b_src, b_land, gsum_parts, b_tr, "reduce_mix_wait")
    recv = dict(zip(['ffn1_w_gate', 'ffn1_w_up', 'ffn1_w_down', 'ffn2_w_gate', 'ffn2_w_up', 'ffn2_w_down', 'w_in',
                     'w_out', 'mem_w_q', 'mem_w_kv', 'mem_w_o'], list(c_out[:3]) + a_got + b_got))

    out_g, out_d, out_m, out_v = {}, {}, {}, {}
    for nm in BIG:
        r = recv[nm]
        res = adamw(w2[nm], m2[nm], v2[nm], [(r, r.shape[2], 0, d) for d in range(N_DEV)], "adamw_" + nm)
        out_g[nm], out_d[nm], out_m[nm], out_v[nm] = res
    wp = _pack(lambda nm: small[nm])
    mp = _pack(lambda nm: m2[nm][None])
    vp = _pack(lambda nm: v2[nm][None])
    pg, pd, pm, pv = adamw(wp, mp, vp, [(gsum_parts, D_MODEL, 0, d) for d in range(N_DEV)], "adamw_small")
    for dst, p in ((out_g, pg), (out_d, pd), (out_m, pm), (out_v, pv)):
        dst.update({k: val[0] for k, val in _unpack(p).items()})
    loss = pg[ROW_MISC, COL_LOSS]
    conv_g = lax.dynamic_slice_in_dim(pg[ROW_CONV:ROW_CONV + 6].reshape(CONV_W, CONV_CH), me * (CONV_CH // N_DEV),
                                      CONV_CH // N_DEV, axis=1)
    res = adamw(w2['gdn_conv_w'], m2['gdn_conv_w'], v2['gdn_conv_w'], [conv_g], "adamw_conv")
    out_g['gdn_conv_w'], out_d['gdn_conv_w'], out_m['gdn_conv_w'], out_v['gdn_conv_w'] = res

    def depth(t):
        return t[None]

    return (loss, grad_x[None], *[depth(out_g[nm]) for nm in WEIGHTS], *[depth(out_d[nm]) for nm in WEIGHTS],
            *[depth(out_m[nm]) for nm in WEIGHTS], *[depth(out_v[nm]) for nm in WEIGHTS])


def kernel(x, mem, ffn1_pre_norm, ffn1_w_gate, ffn1_w_up, ffn1_w_down, ffn1_post_norm, mix_pre_norm, w_in, fox_f_bias, gdn_conv_w, gdn_a_log, gdn_dt_bias, gdn_out_norm, w_out, mix_post_norm, mem_pre_norm, mem_kv_norm, mem_w_q, mem_w_kv, mem_w_o, mem_post_norm, ffn2_pre_norm, ffn2_w_gate, ffn2_w_up, ffn2_w_down, ffn2_post_norm, loss_target, m_ffn1_pre_norm, m_ffn1_w_gate, m_ffn1_w_up, m_ffn1_w_down, m_ffn1_post_norm, m_mix_pre_norm, m_w_in, m_fox_f_bias, m_gdn_conv_w, m_gdn_a_log, m_gdn_dt_bias, m_gdn_out_norm, m_w_out, m_mix_post_norm, m_mem_pre_norm, m_mem_kv_norm, m_mem_w_q, m_mem_w_kv, m_mem_w_o, m_mem_post_norm, m_ffn2_pre_norm, m_ffn2_w_gate, m_ffn2_w_up, m_ffn2_w_down, m_ffn2_post_norm, v_ffn1_pre_norm, v_ffn1_w_gate, v_ffn1_w_up, v_ffn1_w_down, v_ffn1_post_norm, v_mix_pre_norm, v_w_in, v_fox_f_bias, v_gdn_conv_w, v_gdn_a_log, v_gdn_dt_bias, v_gdn_out_norm, v_w_out, v_mix_post_norm, v_mem_pre_norm, v_mem_kv_norm, v_mem_w_q, v_mem_w_kv, v_mem_w_o, v_mem_post_norm, v_ffn2_pre_norm, v_ffn2_w_gate, v_ffn2_w_up, v_ffn2_w_down, v_ffn2_post_norm):
    return _step(dict(locals()))
```

```python
import functools

import jax
import jax.numpy as jnp
from jax import lax
from jax.experimental import pallas as pl
from jax.experimental.pallas import tpu as pltpu

F32 = jnp.float32
BF = jnp.bfloat16
HI = lax.Precision.HIGHEST
MESH = pl.DeviceIdType.MESH

N_DEV = 8
EPS = 1e-6
D_MODEL = 1024
D_FF = 2816
FF_SHARD = D_FF // N_DEV
FF_SHARD_PAD = 384
D_FF_PAD = FF_SHARD_PAD * N_DEV
FOX_HEADS, FOX_DH = 8, 64
GDN_HEADS, GDN_DH = 4, 128
GDN_CHUNK = 64
CONV_W = 4
MEM_HEADS, MEM_DH = 4, 256
IN_W = 3600
IN_SHARD = IN_W // N_DEV
IN_SHARD_PAD = 512
PROJ_W = 4096
SMALL_F, SMALL_B, SMALL_A = 0, 8, 12

ADAM_LR, ADAM_B1, ADAM_B2, ADAM_EPS, ADAM_WD, ADAM_STEP = 0.001, 0.9, 0.999, 1e-08, 0.01, 10

VMEM_LIMIT = 56 * 1024 * 1024


def _params(sem=None):
    return pltpu.CompilerParams(dimension_semantics=sem, vmem_limit_bytes=VMEM_LIMIT)


def _tile(n, pref, unit=128):
    if n <= pref:
        return n
    t = (pref // unit) * unit
    while t > unit and n % t:
        t -= unit
    assert n % t == 0, (n, pref)
    return t


@functools.partial(jax.custom_vjp, nondiff_argnums=(2, 3))
def bdot(a, b, ca, cb):
    return lax.dot_general(a.astype(BF), b.astype(BF), (((ca,), (cb,)), ((), ())), preferred_element_type=F32)


def _bdot_fwd(a, b, ca, cb):
    return bdot(a, b, ca, cb), (a, b)


def _bdot_bwd(ca, cb, res, g):
    a, b = res
    da = bdot(g, b, 1, 1 - cb) if ca == 1 else bdot(b, g, 1 - cb, 1)
    db = bdot(a, g, 1 - ca, 0) if cb == 0 else bdot(g, a, 0, 1 - ca)
    return da, db


bdot.defvjp(_bdot_fwd, _bdot_bwd)


def hdot(a, b):
    return jnp.dot(a, b, precision=HI, preferred_element_type=F32)


def mdot(a, b):
    return jnp.dot(a, b, precision=lax.Precision.HIGH, preferred_element_type=F32)


def _iota2(shape, dim):
    return lax.broadcasted_iota(jnp.int32, shape, dim)


def _sigmoid(x):
    return 1.0 / (1.0 + jnp.exp(-x))


def _silu(x):
    return x * _sigmoid(x)


def _softplus(x):
    return jnp.maximum(x, 0.0) + jnp.log(1.0 + jnp.exp(-jnp.abs(x)))


def _rms(x, gain):
    return x * lax.rsqrt(jnp.mean(x * x, axis=-1, keepdims=True) + EPS) * gain


def mm(a, b, *, name, ta=False, tb=False, out_dtype=F32, tm=1024, tn=1024, tk=1024):
    m, k = (a.shape[1], a.shape[0]) if ta else a.shape
    n = b.shape[0] if tb else b.shape[1]
    assert k == (b.shape[1] if tb else b.shape[0]), (a.shape, b.shape, ta, tb)
    tm, tn, tk = _tile(m, tm), _tile(n, tn), _tile(k, tk)
    nk = k // tk
    dims = (((0 if ta else 1,), (1 if tb else 0,)), ((), ()))

    def kern(a_ref, b_ref, o_ref, *scratch):
        def part():
            return lax.dot_general(a_ref[...].astype(BF), b_ref[...].astype(BF), dims, preferred_element_type=F32)

        if nk == 1:
            o_ref[...] = part().astype(o_ref.dtype)
            return
        acc_ref, = scratch
        kk = pl.program_id(2)

        @pl.when(kk == 0)
        def _():
            acc_ref[...] = part()

        @pl.when(kk > 0)
        def _():
            acc_ref[...] += part()

        @pl.when(kk == nk - 1)
        def _():
            o_ref[...] = acc_ref[...].astype(o_ref.dtype)

    a_spec = pl.BlockSpec((tk, tm), lambda i, j, kk: (kk, i)) if ta else pl.BlockSpec((tm, tk), lambda i, j, kk: (i, kk))
    b_spec = pl.BlockSpec((tn, tk), lambda i, j, kk: (j, kk)) if tb else pl.BlockSpec((tk, tn), lambda i, j, kk: (kk, j))
    return pl.pallas_call(
        kern, name=name, grid=(m // tm, n // tn, nk),
        in_specs=[a_spec, b_spec],
        out_specs=pl.BlockSpec((tm, tn), lambda i, j, kk: (i, j)),
        out_shape=jax.ShapeDtypeStruct((m, n), out_dtype),
        scratch_shapes=[pltpu.VMEM((tm, tn), F32)] if nk > 1 else [],
        compiler_params=_params(("parallel", "parallel", "arbitrary")),
    )(a, b)


def _row_spec(item, rows):
    if not isinstance(item, tuple):
        return item, pl.BlockSpec((rows, item.shape[1]), lambda i: (i, 0))
    if len(item) == 3:
        arr, w, c = item
        return arr, pl.BlockSpec((rows, w), lambda i: (i, c))
    arr, w, c, lead = item
    return arr, pl.BlockSpec((None, rows, w), lambda i: (lead, i, c))


def _whole_spec(item):
    if not isinstance(item, tuple):
        return item, pl.BlockSpec(item.shape, lambda i: (0,) * item.ndim)
    arr, w, c = item
    return arr, pl.BlockSpec((arr.shape[0], w), lambda i: (0, c))


def rowcall(body, tiled, whole, outs, accs=(), *, rows, total, name):
    rows = min(rows, total)
    assert total % rows == 0
    t_arr, t_spec = zip(*[_row_spec(t, rows) for t in tiled])
    w_arr, w_spec = zip(*[_whole_spec(w) for w in whole]) if whole else ((), ())
    nt, nw, no, na = len(t_arr), len(w_arr), len(outs), len(accs)

    def kern(*refs):
        vals = [r[...] for r in refs[:nt + nw]]
        res = body(*vals)
        if not isinstance(res, (tuple, list)):
            res = (res,)
        assert len(res) == no + na, (name, len(res), no, na)
        for r, v in zip(refs[nt + nw:nt + nw + no], res[:no]):
            r[...] = v.astype(r.dtype)
        if na:
            acc_refs = refs[nt + nw + no:]

            @pl.when(pl.program_id(0) == 0)
            def _():
                for r in acc_refs:
                    r[...] = jnp.zeros_like(r)

            for r, v in zip(acc_refs, res[no:]):
                r[...] += v

    out_shape = [jax.ShapeDtypeStruct((total, w), d) for w, d in outs] + [jax.ShapeDtypeStruct(s, F32) for s in accs]
    out_specs = [pl.BlockSpec((rows, w), lambda i: (i, 0)) for w, _ in outs] + \
                [pl.BlockSpec(s, lambda i: (0, 0)) for s in accs]
    res = pl.pallas_call(
        kern, name=name, grid=(total // rows,),
        in_specs=list(t_spec) + list(w_spec), out_specs=out_specs, out_shape=out_shape,
        compiler_params=_params(("arbitrary",) if na else ("parallel",)),
    )(*t_arr, *w_arr)
    return res


def _colsum(x):
    return jnp.sum(x, axis=0, keepdims=True)


def _gdn_chunk(q, k, v, z, gb, bb, state, gain):
    c = GDN_CHUNK
    nh = len(q)
    hs = range(nh)
    r64, c64 = _iota2((c, c), 0), _iota2((c, c), 1)
    incl = r64 >= c64
    strict = r64 > c64
    ltri = incl.astype(F32)
    utri = (r64 <= c64).astype(F32)
    eye = (r64 == c64).astype(F32)
    ones = jnp.ones((c, c), F32)
    pick = (_iota2((GDN_DH, c), 0) == _iota2((GDN_DH, c), 1)).astype(F32)
    last = (_iota2((c, GDN_DH), 0) == c - 1).astype(F32)

    qn = [q[h] * lax.rsqrt(jnp.sum(q[h] * q[h], axis=-1, keepdims=True) + EPS) * (GDN_DH ** -0.5) for h in hs]
    kn = [k[h] * lax.rsqrt(jnp.sum(k[h] * k[h], axis=-1, keepdims=True) + EPS) for h in hs]
    gc = [mdot(ltri, gb[h]) for h in hs]
    g64 = [mdot(gb[h], pick) for h in hs]
    gcol = [mdot(ltri, g64[h]) for h in hs]
    grow = [mdot(ones, g64[h] * utri) for h in hs]
    dec = [jnp.exp(jnp.where(incl, gcol[h] - grow[h], -1e30)) for h in hs]
    kb = [kn[h] * bb[h] for h in hs]
    vb = [v[h] * bb[h] for h in hs]
    kk = [bdot(kb[h], kn[h], 1, 1) for h in hs]
    p = [-jnp.where(strict, kk[h] * dec[h], 0.0) for h in hs]
    tinv = [eye + p[h] for h in hs]
    for _ in range(5):
        p = [mdot(p[h], p[h]) for h in hs]
        tinv = [tinv[h] + mdot(tinv[h], p[h]) for h in hs]
    egc = [jnp.exp(gc[h]) for h in hs]
    u = [mdot(tinv[h], vb[h]) for h in hs]
    w = [mdot(tinv[h], kb[h] * egc[h]) for h in hs]
    attn = [bdot(qn[h], kn[h], 1, 1) * dec[h] for h in hs]
    qd = [qn[h] * egc[h] for h in hs]
    gl = [jnp.sum(gc[h] * last, axis=0, keepdims=True) for h in hs]
    kt = [kn[h] * jnp.exp(gl[h] - gc[h]) for h in hs]
    ws = [bdot(w[h], state[h], 1, 0) for h in hs]
    qs = [bdot(qd[h], state[h], 1, 0) for h in hs]
    v_new = [u[h] - ws[h] for h in hs]
    av = [bdot(attn[h], v_new[h], 1, 0) for h in hs]
    kv = [bdot(kt[h], v_new[h], 0, 0) for h in hs]
    new_state = tuple(state[h] * jnp.exp(gl[h]) + kv[h] for h in hs)
    o = tuple(_rms(qs[h] + av[h], gain) * _silu(z[h]) for h in hs)
    return o, new_state


GDN_ROWS = 512
GDN_W = GDN_HEADS * GDN_DH


def gdn_fwd(cqkv, proj, gbb, gain):
    s = cqkv.shape[0]
    nb, cpb = s // GDN_ROWS, GDN_ROWS // GDN_CHUNK
    h4 = GDN_HEADS

    def kern(qkv_ref, z_ref, gb_ref, gain_ref, o_ref, st_ref, state):
        @pl.when(pl.program_id(0) == 0)
        def _():
            state[...] = jnp.zeros_like(state)

        gain_v = gain_ref[...]

        def step(ci, carry):
            sl = pl.ds(pl.multiple_of(ci * GDN_CHUNK, GDN_CHUNK), GDN_CHUNK)
            ins = []
            for h in range(h4):
                ln = lambda base, h=h: slice(base + h * GDN_DH, base + (h + 1) * GDN_DH)
                ins.append((qkv_ref[sl, ln(0)], qkv_ref[sl, ln(GDN_W)], qkv_ref[sl, ln(2 * GDN_W)], z_ref[sl, ln(0)],
                            gb_ref[sl, ln(0)], gb_ref[sl, ln(GDN_W)], state[h]))
            cols = [tuple(col) for col in zip(*ins)]
            o, new = _gdn_chunk(*cols[:7], gain_v)
            for h in range(h4):
                st_ref[h, ci] = ins[h][6]
                o_ref[sl, h * GDN_DH:(h + 1) * GDN_DH] = o[h]
                state[h] = new[h]
            return carry

        lax.fori_loop(0, cpb, step, 0)

    return pl.pallas_call(
        kern, name="gdn_fwd", grid=(nb,),
        in_specs=[pl.BlockSpec((GDN_ROWS, 3 * GDN_W), lambda i: (i, 0)),
                  pl.BlockSpec((GDN_ROWS, GDN_W), lambda i: (i, 6)),
                  pl.BlockSpec((GDN_ROWS, 2 * GDN_W), lambda i: (i, 0)),
                  pl.BlockSpec((1, GDN_DH), lambda i: (0, 0))],
        out_specs=[pl.BlockSpec((GDN_ROWS, GDN_W), lambda i: (i, 0)),
                   pl.BlockSpec((h4, cpb, GDN_DH, GDN_DH), lambda i: (0, i, 0, 0))],
        out_shape=[jax.ShapeDtypeStruct((s, GDN_W), F32),
                   jax.ShapeDtypeStruct((h4, s // GDN_CHUNK, GDN_DH, GDN_DH), F32)],
        scratch_shapes=[pltpu.VMEM((h4, GDN_DH, GDN_DH), F32)],
        compiler_params=_params(("arbitrary",)),
    )(cqkv, proj, gbb, gain)


def gdn_bwd(cqkv, proj, gbb, gain, states, d_mixed):
    s = cqkv.shape[0]
    nb, cpb = s // GDN_ROWS, GDN_ROWS // GDN_CHUNK
    h4 = GDN_HEADS

    def kern(qkv_ref, z_ref, gb_ref, gain_ref, st_ref, do_ref, dqkv_ref, dz_ref, dgb_ref, dgain_ref, dstate):
        @pl.when(pl.program_id(0) == 0)
        def _():
            dgain_ref[...] = jnp.zeros_like(dgain_ref)
            dstate[...] = jnp.zeros_like(dstate)

        gain_v = gain_ref[...]

        def step(t, carry):
            ci = cpb - 1 - t
            sl = pl.ds(pl.multiple_of(ci * GDN_CHUNK, GDN_CHUNK), GDN_CHUNK)
            prim, cot, dst_in = [], [], []
            for h in range(h4):
                ln = lambda base, h=h: slice(base + h * GDN_DH, base + (h + 1) * GDN_DH)
                prim.append((qkv_ref[sl, ln(0)], qkv_ref[sl, ln(GDN_W)], qkv_ref[sl, ln(2 * GDN_W)], z_ref[sl, ln(0)],
                             gb_ref[sl, ln(0)], gb_ref[sl, ln(GDN_W)], st_ref[h, ci]))
                cot.append(do_ref[sl, ln(0)])
                dst_in.append(dstate[h])
            cols = [tuple(col) for col in zip(*prim)]
            vjp = jax.vjp(_gdn_chunk, *cols, gain_v)[1]
            dq, dk, dv, dz, dg, db, dst, dgn = vjp((tuple(cot), tuple(dst_in)))
            for h in range(h4):
                ln = lambda base, h=h: slice(base + h * GDN_DH, base + (h + 1) * GDN_DH)
                dqkv_ref[sl, ln(0)] = dq[h]
                dqkv_ref[sl, ln(GDN_W)] = dk[h]
                dqkv_ref[sl, ln(2 * GDN_W)] = dv[h]
                dz_ref[sl, ln(0)] = dz[h]
                dgb_ref[sl, ln(0)] = dg[h]
                dgb_ref[sl, ln(GDN_W)] = db[h]
                dstate[h] = dst[h]
            dgain_ref[...] += dgn
            return carry

        lax.fori_loop(0, cpb, step, 0)

    def rev(width, cblock=0):
        return pl.BlockSpec((GDN_ROWS, width), lambda i: (nb - 1 - i, cblock))

    return pl.pallas_call(
        kern, name="gdn_bwd", grid=(nb,),
        in_specs=[rev(3 * GDN_W), rev(GDN_W, 6), rev(2 * GDN_W), pl.BlockSpec((1, GDN_DH), lambda i: (0, 0)),
                  pl.BlockSpec((h4, cpb, GDN_DH, GDN_DH), lambda i: (0, nb - 1 - i, 0, 0)), rev(GDN_W, 1)],
        out_specs=[rev(3 * GDN_W), rev(GDN_W), rev(2 * GDN_W), pl.BlockSpec((1, GDN_DH), lambda i: (0, 0))],
        out_shape=[jax.ShapeDtypeStruct((s, 3 * GDN_W), F32), jax.ShapeDtypeStruct((s, GDN_W), F32),
                   jax.ShapeDtypeStruct((s, 2 * GDN_W), F32), jax.ShapeDtypeStruct((1, GDN_DH), F32)],
        scratch_shapes=[pltpu.VMEM((h4, GDN_DH, GDN_DH), F32)],
        compiler_params=_params(("arbitrary",)),
    )(cqkv, proj, gbb, gain, states, d_mixed)


def _gdn_gates(small, prm):
    w = GDN_HEADS * GDN_DH
    lane, head = _iota2((128, w), 0), _iota2((128, w), 1) // GDN_DH
    sel_b = (lane == SMALL_B + head).astype(F32)
    sel_a = (lane == SMALL_A + head).astype(F32)
    prow = _iota2((8, 128), 0)
    a_log = jnp.sum(prm * (prow == 0).astype(F32), axis=0, keepdims=True)
    dt_b = jnp.sum(prm * (prow == 1).astype(F32), axis=0, keepdims=True)
    beta = _sigmoid(hdot(small, sel_b))
    g = hdot(-jnp.exp(a_log) * _softplus(small + dt_b), sel_a)
    return g, beta


CONV_ROWS = 1024
CONV_COLS = 128
CONV_BLOCK0 = 1536 // CONV_COLS


def _shift_down(prev8, cur, s):
    ext = jnp.concatenate([prev8, cur], axis=0)
    return pltpu.roll(ext, s, 0)[8:]


def _shift_up(cur, next8, s):
    n = cur.shape[0]
    ext = jnp.concatenate([cur, next8], axis=0)
    return pltpu.roll(ext, n + 8 - s, 0)[:n]


def _conv_pre(x_ref, w, ci, nchunk):
    r0 = pl.multiple_of(ci * CONV_ROWS, CONV_ROWS)
    cur = x_ref[pl.ds(r0, CONV_ROWS), :]
    prev = x_ref[pl.ds(pl.multiple_of(jnp.maximum(r0 - 8, 0), 8), 8), :]
    prev = jnp.where(ci > 0, prev, 0.0)
    shifted = [cur] + [_shift_down(prev, cur, s) for s in range(1, CONV_W)]
    pre = w[CONV_W - 1:CONV_W, :] * cur
    for s in range(1, CONV_W):
        pre = pre + w[CONV_W - 1 - s:CONV_W - s, :] * shifted[s]
    return r0, pre, shifted


def conv_fwd(proj, conv_w8):
    s = proj.shape[0]
    nchunk = s // CONV_ROWS
    ncol = 3 * GDN_HEADS * GDN_DH // CONV_COLS

    def kern(x_ref, w_ref, y_ref):
        w = w_ref[...]

        def step(ci, carry):
            r0, pre, _ = _conv_pre(x_ref, w, ci, nchunk)
            y_ref[pl.ds(r0, CONV_ROWS), :] = _silu(pre)
            return carry

        lax.fori_loop(0, nchunk, step, 0)

    return pl.pallas_call(
        kern, name="conv_fwd", grid=(ncol,),
        in_specs=[pl.BlockSpec((s, CONV_COLS), lambda j: (0, CONV_BLOCK0 + j)),
                  pl.BlockSpec((8, CONV_COLS), lambda j: (0, j))],
        out_specs=pl.BlockSpec((s, CONV_COLS), lambda j: (0, j)),
        out_shape=jax.ShapeDtypeStruct((s, ncol * CONV_COLS), F32),
        compiler_params=_params(("parallel",)),
    )(proj, conv_w8)


def conv_bwd(proj, conv_w8, dy):
    s = proj.shape[0]
    nchunk = s // CONV_ROWS
    per = 3 * GDN_HEADS * GDN_DH // CONV_COLS
    outs = []
    for part in range(1):
        def kern(x_ref, w_ref, dy_ref, dx_ref, dw_ref, dpre_ref):
            w = w_ref[...]
            rows8 = _iota2((8, CONV_COLS), 0)

            def step1(ci, dw):
                r0, pre, shifted = _conv_pre(x_ref, w, ci, nchunk)
                sg = _sigmoid(pre)
                dpre = dy_ref[pl.ds(r0, CONV_ROWS), :] * sg * (1.0 + pre * (1.0 - sg))
                dpre_ref[pl.ds(r0, CONV_ROWS), :] = dpre
                for sh in range(CONV_W):
                    dw = dw + jnp.where(rows8 == CONV_W - 1 - sh, _colsum(dpre * shifted[sh]), 0.0)
                return dw

            dw_ref[...] = lax.fori_loop(0, nchunk, step1, jnp.zeros((8, CONV_COLS), F32))

            def step2(ci, carry):
                r0 = pl.multiple_of(ci * CONV_ROWS, CONV_ROWS)
                cur = dpre_ref[pl.ds(r0, CONV_ROWS), :]
                nxt = dpre_ref[pl.ds(pl.multiple_of(jnp.minimum(r0 + CONV_ROWS, s - 8), 8), 8), :]
                nxt = jnp.where(ci < nchunk - 1, nxt, 0.0)
                dx = w[CONV_W - 1:CONV_W, :] * cur
                for sh in range(1, CONV_W):
                    dx = dx + w[CONV_W - 1 - sh:CONV_W - sh, :] * _shift_up(cur, nxt, sh)
                dx_ref[pl.ds(r0, CONV_ROWS), :] = dx
                return carry

            lax.fori_loop(0, nchunk, step2, 0)

        outs.append(pl.pallas_call(
            kern, name=f"conv_bwd{part}", grid=(per,),
            in_specs=[pl.BlockSpec((s, CONV_COLS), lambda j, part=part: (0, CONV_BLOCK0 + part * per + j)),
                      pl.BlockSpec((8, CONV_COLS), lambda j, part=part: (0, part * per + j)),
                      pl.BlockSpec((s, CONV_COLS), lambda j: (0, j))],
            out_specs=[pl.BlockSpec((s, CONV_COLS), lambda j: (0, j)),
                       pl.BlockSpec((8, CONV_COLS), lambda j: (0, j))],
            out_shape=[jax.ShapeDtypeStruct((s, per * CONV_COLS), F32),
                       jax.ShapeDtypeStruct((8, per * CONV_COLS), F32)],
            scratch_shapes=[pltpu.VMEM((s, CONV_COLS), F32)],
            compiler_params=_params(("parallel",)),
        )(proj, conv_w8, dy))
    dx = jnp.concatenate([o[0] for o in outs], axis=1)
    dw = jnp.concatenate([o[1] for o in outs], axis=1)
    return dx, dw


FOXF_ROWS = 512
SMALL_BLOCK128 = 3584 // 128


def _log_sigmoid(x):
    return jnp.minimum(x, 0.0) - jnp.log(1.0 + jnp.exp(-jnp.abs(x)))


def fox_f_fwd(proj, bias_row):
    s = proj.shape[0]
    n = s // FOXF_ROWS

    def kern(x_ref, b_ref, f_ref, carry):
        @pl.when(pl.program_id(0) == 0)
        def _():
            carry[...] = jnp.zeros_like(carry)

        heads = _iota2((FOXF_ROWS, 128), 1) < FOX_HEADS
        lf = jnp.where(heads, _log_sigmoid(x_ref[...] + b_ref[...]), 0.0)
        ltri = (_iota2((FOXF_ROWS, FOXF_ROWS), 0) >= _iota2((FOXF_ROWS, FOXF_ROWS), 1)).astype(F32)
        c = hdot(ltri, lf) + carry[...]
        f_ref[...] = c
        carry[...] = c[FOXF_ROWS - 1:FOXF_ROWS, :]

    return pl.pallas_call(
        kern, name="fox_f_fwd", grid=(n,),
        in_specs=[pl.BlockSpec((FOXF_ROWS, 128), lambda i: (i, SMALL_BLOCK128)),
                  pl.BlockSpec((1, 128), lambda i: (0, 0))],
        out_specs=pl.BlockSpec((FOXF_ROWS, 128), lambda i: (i, 0)),
        out_shape=jax.ShapeDtypeStruct((s, 128), F32),
        scratch_shapes=[pltpu.VMEM((1, 128), F32)],
        compiler_params=_params(("arbitrary",)),
    )(proj, bias_row)


def fox_f_bwd(proj, bias_row, d_f):
    s = proj.shape[0]
    n = s // FOXF_ROWS

    def kern(x_ref, b_ref, df_ref, dx_ref, db_ref, carry):
        @pl.when(pl.program_id(0) == 0)
        def _():
            carry[...] = jnp.zeros_like(carry)
            db_ref[...] = jnp.zeros_like(db_ref)

        heads = _iota2((FOXF_ROWS, 128), 1) < FOX_HEADS
        utri = (_iota2((FOXF_ROWS, FOXF_ROWS), 0) <= _iota2((FOXF_ROWS, FOXF_ROWS), 1)).astype(F32)
        rc = hdot(utri, df_ref[...]) + carry[...]
        carry[...] = rc[0:1, :]
        dx = jnp.where(heads, rc * _sigmoid(-(x_ref[...] + b_ref[...])), 0.0)
        dx_ref[...] = dx
        db_ref[...] += _colsum(dx)

    return pl.pallas_call(
        kern, name="fox_f_bwd", grid=(n,),
        in_specs=[pl.BlockSpec((FOXF_ROWS, 128), lambda i: (n - 1 - i, SMALL_BLOCK128)),
                  pl.BlockSpec((1, 128), lambda i: (0, 0)),
                  pl.BlockSpec((FOXF_ROWS, 128), lambda i: (n - 1 - i, 0))],
        out_specs=[pl.BlockSpec((FOXF_ROWS, 128), lambda i: (n - 1 - i, 0)),
                   pl.BlockSpec((1, 128), lambda i: (0, 0))],
        out_shape=[jax.ShapeDtypeStruct((s, 128), F32), jax.ShapeDtypeStruct((1, 128), F32)],
        scratch_shapes=[pltpu.VMEM((1, 128), F32)],
        compiler_params=_params(("arbitrary",)),
    )(proj, bias_row, d_f)


FOX_T = 512
FOX_SCALE = FOX_DH ** -0.5
FOX_PAIRS = FOX_HEADS // 2
XL = 8
NEG = -1e30
_NT = (((1,), (1,)), ((), ()))


def _split3(x):
    def bf(v):
        return lax.reduce_precision(v, exponent_bits=8, mantissa_bits=7)

    hi = bf(x)
    mid = bf(x - hi)
    lo = bf(x - hi - mid)
    return jnp.stack([hi, mid, lo], axis=-1)


def _fox_extras(s, first, second):
    def part(v):
        if v is None:
            return jnp.zeros((s, FOX_HEADS, 3), F32)
        if isinstance(v, float):
            return jnp.full((s, FOX_HEADS, 3), v, F32)
        return _split3(v)

    cols = jnp.concatenate([part(first), part(second), jnp.zeros((s, FOX_HEADS, XL - 6), F32)], axis=-1)
    cols = _pad_to(cols.reshape(s, FOX_PAIRS, 2 * XL), (s, FOX_PAIRS, 128))
    return cols.transpose(1, 0, 2).astype(BF)


def _pair_masks(rows):
    lane = _iota2((rows, 256), 1)
    head = jnp.where(lane < 128, lane // FOX_DH, (lane - 128) // XL)
    return head == 0, head == 1


def fox_fwd(qkv, xq, xk, xv):
    s = qkv.shape[0]
    t = min(FOX_T, s)
    n = s // t

    def kern(q_ref, k_ref, v_ref, xq_ref, xk_ref, xv_ref, o_ref, lse_ref):
        i = pl.program_id(1)
        masks = _pair_masks(t)
        q_all = jnp.concatenate([q_ref[...] * FOX_SCALE, xq_ref[...]], axis=1)
        q_ops = [jnp.where(mk, q_all, 0).astype(BF) for mk in masks]

        def step(j, carry, masked):
            sl = pl.ds(pl.multiple_of(j * t, t), t)
            k_op = jnp.concatenate([k_ref[sl, :], xk_ref[sl, :]], axis=1)
            v_op = jnp.concatenate([v_ref[sl, :], xv_ref[sl, :]], axis=1)
            sc = [lax.dot_general(q_ops[e], k_op, _NT, preferred_element_type=F32) for e in range(2)]
            if masked:
                keep = _iota2((t, t), 0) >= _iota2((t, t), 1)
                sc = [jnp.where(keep, x, NEG) for x in sc]
            m_new = [jnp.maximum(carry[e][0], jnp.max(sc[e], axis=1, keepdims=True)) for e in range(2)]
            p = [jnp.exp(sc[e] - m_new[e]).astype(BF) for e in range(2)]
            pv = [jnp.dot(p[e], v_op, preferred_element_type=F32) for e in range(2)]
            return tuple((m_new[e], jnp.exp(carry[e][0] - m_new[e]) * carry[e][1] + pv[e]) for e in range(2))

        init = tuple((jnp.full((t, 1), NEG, F32), jnp.zeros((t, 256), F32)) for _ in range(2))
        carry = lax.fori_loop(0, i, lambda j, c: step(j, c, False), init)
        carry = step(i, carry, True)
        lane = _iota2((t, 256), 1)
        outs, lses = [], []
        for e in range(2):
            m, acc = carry[e]
            l = jnp.sum(jnp.where(lane == 128, acc, 0.0), axis=1, keepdims=True)
            outs.append(acc[:, :128] / l)
            lses.append(m + jnp.log(l))
        lane128 = _iota2((t, 128), 1)
        o_ref[...] = jnp.where(lane128 < FOX_DH, outs[0], outs[1])
        lse_ref[...] = jnp.where(lane128 == 0, lses[0], jnp.where(lane128 == 1, lses[1], 0.0))

    pr = FOX_PAIRS
    return pl.pallas_call(
        kern, name="fox_fwd", grid=(pr, n),
        in_specs=[pl.BlockSpec((t, 128), lambda p, i: (i, p)),
                  pl.BlockSpec((s, 128), lambda p, i: (0, pr + p)),
                  pl.BlockSpec((s, 128), lambda p, i: (0, 2 * pr + p)),
                  pl.BlockSpec((None, t, 128), lambda p, i: (p, i, 0)),
                  pl.BlockSpec((None, s, 128), lambda p, i: (p, 0, 0)),
                  pl.BlockSpec((None, s, 128), lambda p, i: (p, 0, 0))],
        out_specs=[pl.BlockSpec((t, 128), lambda p, i: (i, p)),
                   pl.BlockSpec((None, t, 128), lambda p, i: (p, i, 0))],
        out_shape=[jax.ShapeDtypeStruct((s, FOX_HEADS * FOX_DH), F32), jax.ShapeDtypeStruct((pr, s, 128), F32)],
        compiler_params=_params(("parallel", "parallel")),
    )(qkv, qkv, qkv, xq, xk, xv)


def fox_bwd(qkv, d_o, xk, xv, xqb, xdo):
    s = qkv.shape[0]
    t = min(FOX_T, s)
    n = s // t

    def kern(k_ref, v_ref, xk_ref, xv_ref, q_ref, do_ref, xq_ref, xd_ref, dq_ref, dk_ref, dv_ref):
        j = pl.program_id(1)

        @pl.when(j == 0)
        def _():
            dq_ref[...] = jnp.zeros_like(dq_ref)

        masks = _pair_masks(t)
        k_op = jnp.concatenate([k_ref[...], xk_ref[...]], axis=1)
        v_op = jnp.concatenate([v_ref[...], xv_ref[...]], axis=1)
        k_ops = [jnp.where(mk, k_op, 0).astype(BF) for mk in masks]

        def step(i, carry, masked):
            dk, dv = carry
            sl = pl.ds(pl.multiple_of(i * t, t), t)
            q_all = jnp.concatenate([q_ref[sl, :] * FOX_SCALE, xq_ref[sl, :]], axis=1)
            do_all = jnp.concatenate([do_ref[sl, :], xd_ref[sl, :]], axis=1)
            q_ops = [jnp.where(mk, q_all, 0).astype(BF) for mk in masks]
            do_ops = [jnp.where(mk, do_all, 0).astype(BF) for mk in masks]
            st = [lax.dot_general(k_op, q_ops[e], _NT, preferred_element_type=F32) for e in range(2)]
            dp = [lax.dot_general(v_op, do_ops[e], _NT, preferred_element_type=F32) for e in range(2)]
            if masked:
                keep = _iota2((t, t), 0) <= _iota2((t, t), 1)
                st = [jnp.where(keep, x, NEG) for x in st]
            pt = [jnp.exp(x) for x in st]
            dsb = [(pt[e] * dp[e]).astype(BF) for e in range(2)]
            for e in range(2):
                dv = dv + jnp.dot(pt[e].astype(BF), do_ops[e][:, :128], preferred_element_type=F32)
                dk = dk + jnp.dot(dsb[e], q_ops[e], preferred_element_type=F32)
            dq_ref[sl, :] += sum(lax.dot_general(dsb[e], k_ops[e], (((0,), (0,)), ((), ())),
                                                 preferred_element_type=F32) for e in range(2))
            return dk, dv

        init = (jnp.zeros((t, 256), F32), jnp.zeros((t, 128), F32))
        carry = step(j, init, True)
        dk, dv = lax.fori_loop(j + 1, n, lambda i, c: step(i, c, False), carry)
        dk_ref[...] = dk
        dv_ref[...] = dv

    pr = FOX_PAIRS
    return pl.pallas_call(
        kern, name="fox_bwd", grid=(pr, n),
        in_specs=[pl.BlockSpec((t, 128), lambda p, j: (j, pr + p)),
                  pl.BlockSpec((t, 128), lambda p, j: (j, 2 * pr + p)),
                  pl.BlockSpec((None, t, 128), lambda p, j: (p, j, 0)),
                  pl.BlockSpec((None, t, 128), lambda p, j: (p, j, 0)),
                  pl.BlockSpec((s, 128), lambda p, j: (0, p)),
                  pl.BlockSpec((s, 128), lambda p, j: (0, p)),
                  pl.BlockSpec((None, s, 128), lambda p, j: (p, 0, 0)),
                  pl.BlockSpec((None, s, 128), lambda p, j: (p, 0, 0))],
        out_specs=[pl.BlockSpec((s, 256), lambda p, j: (0, p)),
                   pl.BlockSpec((t, 256), lambda p, j: (j, p)),
                   pl.BlockSpec((t, 128), lambda p, j: (j, p))],
        out_shape=[jax.ShapeDtypeStruct((s, pr * 256), F32), jax.ShapeDtypeStruct((s, pr * 256), F32),
                   jax.ShapeDtypeStruct((s, FOX_HEADS * FOX_DH), F32)],
        compiler_params=_params(("parallel", "arbitrary")),
    )(qkv, qkv, xk, xv, qkv, d_o, xqb, xdo)


def _xattn_head(q, k, v):
    sc = bdot(q, k, 1, 1) * (MEM_DH ** -0.5)
    e = jnp.exp(sc - lax.stop_gradient(jnp.max(sc, axis=-1, keepdims=True)))
    p = e / jnp.sum(e, axis=-1, keepdims=True)
    return bdot(p, v, 1, 0)


def xattn_fwd(q, kv):
    s = q.shape[0]
    hh = MEM_HEADS

    def body(*vals):
        qs, ks, vs = vals[:hh], vals[hh:2 * hh], vals[2 * hh:]
        return jnp.concatenate([_xattn_head(qs[a], ks[a], vs[a]) for a in range(hh)], axis=1)

    return rowcall(body, [(q, MEM_DH, a) for a in range(hh)],
                   [(kv, MEM_DH, a) for a in range(2 * hh)],
                   [(hh * MEM_DH, BF)], rows=512, total=s, name="xattn_fwd")[0]


def xattn_bwd(q, kv, d_o):
    s = q.shape[0]
    hh = MEM_HEADS

    def body(*vals):
        qs, dos = vals[:hh], vals[hh:2 * hh]
        ks, vs = vals[2 * hh:3 * hh], vals[3 * hh:]
        dqs, dks, dvs = [], [], []
        for a in range(hh):
            _, vjp = jax.vjp(_xattn_head, qs[a], ks[a], vs[a])
            dq, dk, dv = vjp(dos[a])
            dqs.append(dq)
            dks.append(dk)
            dvs.append(dv)
        return jnp.concatenate(dqs, axis=1), jnp.concatenate(dks + dvs, axis=1)

    return rowcall(body, [(q, MEM_DH, a) for a in range(hh)] + [(d_o, MEM_DH, a) for a in range(hh)],
                   [(kv, MEM_DH, a) for a in range(2 * hh)],
                   [(hh * MEM_DH, BF)], [kv.shape], rows=512, total=s, name="xattn_bwd")


def _slab(ref, axis, start, size):
    if axis is None:
        return ref
    if axis == "lead":
        return ref.at[start]
    idx = pl.ds(pl.multiple_of(start, 128 if axis == 1 else 16), size)
    return ref.at[idx] if axis == 0 else ref.at[:, idx]


def exchange(inputs, outputs, transfers, name):
    ni, no, nt = len(inputs), len(outputs), len(transfers)
    npeer = N_DEV - 1

    def body(*refs):
        ins, outs = refs[:ni], refs[ni:ni + no]
        send, recv, loc = refs[ni + no:]
        x, y, c = lax.axis_index("x"), lax.axis_index("y"), lax.axis_index("c")
        me = 4 * x + 2 * y + c

        def peer(p):
            px = 1 - x if p & 4 else x
            py = 1 - y if p & 2 else y
            pc = 1 - c if p & 1 else c
            return (px, py, pc), 4 * px + 2 * py + pc

        def view(ref, spec, who):
            axis, off, stride, size = spec
            return _slab(ref, axis, off + who * stride, size)

        local, remote = [], []
        for w, (ii, src, oi, dst) in enumerate(transfers):
            cp = pltpu.make_async_copy(view(ins[ii], src, me), view(outs[oi], dst, me), loc.at[w])
            cp.start()
            local.append(cp)
        for p in range(1, N_DEV):
            dev, idx = peer(p)
            for w, (ii, src, oi, dst) in enumerate(transfers):
                k = w * npeer + p - 1
                out_cp = pltpu.make_async_remote_copy(
                    src_ref=view(ins[ii], src, idx), dst_ref=view(outs[oi], dst, me), send_sem=send.at[k],
                    recv_sem=recv.at[k], device_id=dev, device_id_type=MESH)
                out_cp.start()
                in_cp = pltpu.make_async_remote_copy(
                    src_ref=view(ins[ii], src, idx), dst_ref=view(outs[oi], dst, idx), send_sem=send.at[k],
                    recv_sem=recv.at[k], device_id=dev, device_id_type=MESH)
                remote.append((out_cp, in_cp))
        for out_cp, in_cp in remote:
            in_cp.wait_recv()
            out_cp.wait_send()
        for cp in local:
            cp.wait()

    hbm = pl.BlockSpec(memory_space=pl.ANY)
    return pl.pallas_call(
        body, name=name, in_specs=[hbm] * ni, out_specs=[hbm] * no, out_shape=list(outputs),
        scratch_shapes=[pltpu.SemaphoreType.DMA((nt * npeer,)), pltpu.SemaphoreType.DMA((nt * npeer,)),
                        pltpu.SemaphoreType.DMA((nt,))],
        compiler_params=pltpu.CompilerParams(has_side_effects=True),
    )(*inputs)


def _peer(p):
    x, y, c = lax.axis_index("x"), lax.axis_index("y"), lax.axis_index("c")
    px = 1 - x if p & 4 else x
    py = 1 - y if p & 2 else y
    pc = 1 - c if p & 1 else c
    return (px, py, pc), 4 * px + 2 * py + pc


def _view(ref, spec, who):
    axis, off, stride, size = spec
    return _slab(ref, axis, off + who * stride, size)


def place_own(inputs, outputs, transfers):
    me = 4 * lax.axis_index("x") + 2 * lax.axis_index("y") + lax.axis_index("c")
    lands = [jnp.zeros(o.shape, o.dtype) for o in outputs]
    for ii, src, oi, dst in transfers:
        axis, off, stride, size = src
        own = inputs[ii] if axis is None else lax.dynamic_slice_in_dim(inputs[ii], off + me * stride, size, axis)
        axis, off, stride, size = dst
        if axis == "lead":
            lands[oi] = lax.dynamic_update_slice_in_dim(lands[oi], own[None], me, 0)
        else:
            lands[oi] = lax.dynamic_update_slice_in_dim(lands[oi], own, off + me * stride, axis)
    return lands


_HBM = pl.BlockSpec(memory_space=pltpu.HBM)
_SEM = pl.BlockSpec(memory_space=pltpu.SEMAPHORE)
_EFFECT = pltpu.SideEffectType.DATAFLOW_SIDE_EFFECTING


def _remote_copies(ins, lands, transfers, send, recv):
    npeer = N_DEV - 1
    me = 4 * lax.axis_index("x") + 2 * lax.axis_index("y") + lax.axis_index("c")
    pairs = []
    for p in range(1, N_DEV):
        dev, idx = _peer(p)
        for w, (ii, src, oi, dst) in enumerate(transfers):
            k = w * npeer + p - 1
            common = dict(src_ref=_view(ins[ii], src, idx), send_sem=send.at[k], recv_sem=recv.at[k],
                          device_id=dev, device_id_type=MESH)
            pairs.append((pltpu.make_async_remote_copy(dst_ref=_view(lands[oi], dst, me), **common),
                          pltpu.make_async_remote_copy(dst_ref=_view(lands[oi], dst, idx), **common)))
    return pairs


def exchange_start(inputs, lands, transfers, after, name):
    ni, nl, nsem = len(inputs), len(lands), len(transfers) * (N_DEV - 1)

    def body(*refs):
        ins, lnd = refs[:ni], refs[ni:ni + nl]
        send, recv = refs[ni + nl + 1], refs[ni + nl + 2]
        token = refs[-1]
        for out_cp, _ in _remote_copies(ins, lnd, transfers, send, recv):
            out_cp.start()
        token[...] = jnp.zeros_like(token)

    args = [pltpu.with_memory_space_constraint(a, pltpu.HBM) for a in list(inputs) + list(lands)]
    res = pl.pallas_call(
        body, name=name,
        out_shape=(pltpu.SemaphoreType.DMA((nsem,)), pltpu.SemaphoreType.DMA((nsem,)),
                   *[pltpu.HBM(a.shape, a.dtype) for a in args], jax.ShapeDtypeStruct((8, 128), F32)),
        in_specs=[_HBM] * (ni + nl) + [pl.BlockSpec(memory_space=pl.ANY)],
        out_specs=(_SEM, _SEM, *[_HBM] * (ni + nl), pl.BlockSpec(memory_space=pltpu.VMEM)),
        input_output_aliases={k: k + 2 for k in range(ni + nl)},
        compiler_params=pltpu.CompilerParams(has_side_effects=_EFFECT),
    )(*args, after)
    return res[0], res[1], list(res[2:2 + ni]), list(res[2 + ni:2 + ni + nl]), res[-1]


def exchange_wait(send, recv, inputs, lands, after, transfers, name):
    ni, nl = len(inputs), len(lands)

    def body(*refs):
        ins, lnd = refs[:ni], refs[ni:ni + nl]
        send_r, recv_r = refs[ni + nl], refs[ni + nl + 1]
        for out_cp, in_cp in _remote_copies(ins, lnd, transfers, send_r, recv_r):
            out_cp.wait_send()
            in_cp.wait_recv()

    res = pl.pallas_call(
        body, name=name,
        out_shape=tuple(pltpu.HBM(a.shape, a.dtype) for a in list(inputs) + list(lands)),
        in_specs=[_HBM] * (ni + nl) + [_SEM, _SEM, pl.BlockSpec(memory_space=pl.ANY)],
        out_specs=tuple([_HBM] * (ni + nl)),
        input_output_aliases={k: k for k in range(ni + nl)},
        compiler_params=pltpu.CompilerParams(has_side_effects=_EFFECT),
    )(*inputs, *lands, send, recv, after)
    return list(res[ni:])


def adamw(w, m, v, contribs, name):
    r, c = w.shape
    nc = len(contribs)
    rows = next((r // d for d in (4, 2) if r % d == 0 and (r // d) % 16 == 0), r)
    c1, c2 = 1.0 - ADAM_B1 ** ADAM_STEP, 1.0 - ADAM_B2 ** ADAM_STEP

    def body(wv, mv, vv, *gs):
        g = gs[0].astype(F32)
        for extra in gs[1:]:
            g = g + extra.astype(F32)
        g = g[:, :c]
        m_new = ADAM_B1 * mv + (1.0 - ADAM_B1) * g
        v_new = ADAM_B2 * vv + (1.0 - ADAM_B2) * (g * g)
        delta = -ADAM_LR * ((m_new / c1) / (jnp.sqrt(v_new / c2) + ADAM_EPS) + ADAM_WD * wv)
        return g, delta, m_new, v_new

    assert nc >= 1
    return rowcall(body, [w, m, v] + list(contribs), [], [(c, F32)] * 4, rows=rows, total=r, name=name)


WEIGHTS = ['ffn1_pre_norm', 'ffn1_w_gate', 'ffn1_w_up', 'ffn1_w_down', 'ffn1_post_norm', 'mix_pre_norm', 'w_in',
           'fox_f_bias', 'gdn_conv_w', 'gdn_a_log', 'gdn_dt_bias', 'gdn_out_norm', 'w_out', 'mix_post_norm',
           'mem_pre_norm', 'mem_kv_norm', 'mem_w_q', 'mem_w_kv', 'mem_w_o', 'mem_post_norm', 'ffn2_pre_norm',
           'ffn2_w_gate', 'ffn2_w_up', 'ffn2_w_down', 'ffn2_post_norm']
GAINS = ['ffn1_pre_norm', 'ffn1_post_norm', 'mix_pre_norm', 'mix_post_norm', 'mem_pre_norm', 'mem_kv_norm',
         'mem_post_norm', 'ffn2_pre_norm', 'ffn2_post_norm']
BIG = ['ffn1_w_gate', 'ffn1_w_up', 'ffn1_w_down', 'w_in', 'w_out', 'mem_w_q', 'mem_w_kv', 'mem_w_o',
       'ffn2_w_gate', 'ffn2_w_up', 'ffn2_w_down']
PACK_ROWS = 24
ROW_MISC = len(GAINS)
ROW_CONV = ROW_MISC + 1
COL_FBIAS, COL_ALOG, COL_DTB, COL_ONORM, COL_LOSS = 0, 8, 12, 128, 256
CONV_CH = 3 * GDN_HEADS * GDN_DH


def _pad_to(a, shape):
    return jnp.pad(a, [(0, t - s) for s, t in zip(a.shape, shape)])


def _pack(get, conv=None, loss=None):
    rows = [get(nm) for nm in GAINS]
    misc = jnp.concatenate([get('fox_f_bias'), get('gdn_a_log'), get('gdn_dt_bias'),
                            jnp.zeros((1, COL_ONORM - COL_DTB - 4), F32), get('gdn_out_norm'),
                            jnp.zeros((1, 1), F32) if loss is None else loss.reshape(1, 1)], axis=1)
    rows.append(_pad_to(misc, (1, D_MODEL)))
    rows.append(jnp.zeros((6, D_MODEL), F32) if conv is None else conv.reshape(6, D_MODEL))
    return _pad_to(jnp.concatenate(rows, axis=0), (PACK_ROWS, D_MODEL))


def _unpack(p):
    out = {nm: p[i:i + 1] for i, nm in enumerate(GAINS)}
    misc = p[ROW_MISC:ROW_MISC + 1]
    out['fox_f_bias'] = misc[:, COL_FBIAS:COL_FBIAS + FOX_HEADS]
    out['gdn_a_log'] = misc[:, COL_ALOG:COL_ALOG + GDN_HEADS]
    out['gdn_dt_bias'] = misc[:, COL_DTB:COL_DTB + GDN_HEADS]
    out['gdn_out_norm'] = misc[:, COL_ONORM:COL_ONORM + GDN_DH]
    return out


def _ffn_fwd(h, pre, wgu, wd, tag):
    s = h.shape[0]
    u, = rowcall(_rms, [h], [pre], [(D_MODEL, BF)], rows=512, total=s, name=tag + "_pre")
    gu = mm(u, wgu, out_dtype=BF, name=tag + "_gate_up")
    act, = rowcall(lambda a, b: _silu(a.astype(F32)) * b.astype(F32), [(gu, D_FF_PAD, 0), (gu, D_FF_PAD, 1)], [],
                   [(D_FF_PAD, BF)],
                   rows=256, total=s, name=tag + "_act")
    f = mm(act, wd, name=tag + "_down")
    return u, gu, act, f


def _half_rms(a, g):
    return 0.5 * _rms(a, g)


def _ffn_bwd(dh_out, h, pre, post, wgu, wd, saved, tag):
    u, gu, act, f = saved
    s = h.shape[0]

    def b_post(dh, fv, pg):
        return jax.vjp(_half_rms, fv, pg)[1](dh)

    df, dpost = rowcall(b_post, [dh_out, f], [post], [(D_MODEL, BF)], [(1, D_MODEL)], rows=512, total=s,
                        name=tag + "_bwd_post")
    dact = mm(df, wd, tb=True, out_dtype=BF, name=tag + "_bwd_dact")
    dwd = mm(act, df, ta=True, out_dtype=BF, name=tag + "_bwd_dwd")

    def b_act(a, b, da):
        dg, du = jax.vjp(lambda g_, u_: _silu(g_) * u_, a.astype(F32), b.astype(F32))[1](da.astype(F32))
        return jnp.concatenate([dg, du], axis=1)

    dgu, = rowcall(b_act, [(gu, D_FF_PAD, 0), (gu, D_FF_PAD, 1), dact], [], [(2 * D_FF_PAD, BF)], rows=256, total=s,
                   name=tag + "_bwd_act")
    du = mm(dgu, wgu, tb=True, name=tag + "_bwd_du")
    dwgu = mm(u, dgu, ta=True, out_dtype=BF, name=tag + "_bwd_dwgu")

    def b_pre(dh, duv, hv, pg):
        dx, dpre = jax.vjp(_rms, hv, pg)[1](duv)
        return dh + dx, dpre

    dh, dpre = rowcall(b_pre, [dh_out, du, h], [pre], [(D_MODEL, F32)], [(1, D_MODEL)], rows=512, total=s,
                       name=tag + "_bwd_pre")
    return dh, dwgu, dwd, dpre, dpost


def _residual_rms(h, a, g):
    return h + _rms(a, g)


def _bwd_residual(dh, a, g):
    return jax.vjp(_rms, a, g)[1](dh)


def _step(a):
    x, mem = a['x'][0], a['mem'][0]
    s = x.shape[0]
    me = 4 * lax.axis_index("x") + 2 * lax.axis_index("y") + lax.axis_index("c")
    w2 = {nm: a[nm][0] for nm in WEIGHTS}
    m2 = {nm: a['m_' + nm][0] for nm in WEIGHTS}
    v2 = {nm: a['v_' + nm][0] for nm in WEIGHTS}
    small = {nm: w2[nm][None] for nm in WEIGHTS if nm not in BIG and nm != 'gdn_conv_w'}

    def ff_cols(w):
        return _pad_to(w, (D_MODEL, FF_SHARD_PAD)).astype(BF)

    def ff_rows(w):
        return _pad_to(w, (FF_SHARD_PAD, D_MODEL)).astype(BF)

    whole = (None, 0, 0, 0)
    conv_pad = 256
    g_in = [ff_cols(w2['ffn1_w_gate']), ff_cols(w2['ffn1_w_up']), ff_rows(w2['ffn1_w_down']),
            ff_cols(w2['ffn2_w_gate']), ff_cols(w2['ffn2_w_up']), ff_rows(w2['ffn2_w_down']),
            _pad_to(w2['w_in'], (D_MODEL, IN_SHARD_PAD)).astype(BF), w2['w_out'].astype(BF),
            w2['mem_w_q'].astype(BF), w2['mem_w_kv'].astype(BF), w2['mem_w_o'].astype(BF),
            _pad_to(w2['gdn_conv_w'], (8, conv_pad))]
    g_out = [jax.ShapeDtypeStruct((D_MODEL, 2 * D_FF_PAD), BF), jax.ShapeDtypeStruct((D_FF_PAD, D_MODEL), BF),
             jax.ShapeDtypeStruct((D_MODEL, 2 * D_FF_PAD), BF), jax.ShapeDtypeStruct((D_FF_PAD, D_MODEL), BF),
             jax.ShapeDtypeStruct((D_MODEL, N_DEV * IN_SHARD_PAD), BF), jax.ShapeDtypeStruct((D_MODEL, D_MODEL), BF),
             jax.ShapeDtypeStruct((D_MODEL, D_MODEL), BF), jax.ShapeDtypeStruct((D_MODEL, 2 * D_MODEL), BF),
             jax.ShapeDtypeStruct((D_MODEL, D_MODEL), BF), jax.ShapeDtypeStruct((8, N_DEV * conv_pad), F32)]
    sp_, dm = FF_SHARD_PAD, D_MODEL // N_DEV
    g_tr = [(0, whole, 0, (1, 0, sp_, sp_)), (1, whole, 0, (1, D_FF_PAD, sp_, sp_)), (2, whole, 1, (0, 0, sp_, sp_)),
            (3, whole, 2, (1, 0, sp_, sp_)), (4, whole, 2, (1, D_FF_PAD, sp_, sp_)), (5, whole, 3, (0, 0, sp_, sp_)),
            (6, whole, 4, (1, 0, IN_SHARD_PAD, IN_SHARD_PAD)), (7, whole, 5, (0, 0, dm, dm)),
            (8, whole, 6, (0, 0, dm, dm)), (9, whole, 7, (1, 0, 2 * dm, 2 * dm)), (10, whole, 8, (0, 0, dm, dm)),
            (11, whole, 9, (1, 0, conv_pad, conv_pad))]
    wgu1, wd1 = exchange(g_in[:3], g_out[:2], g_tr[:3], "gather_ffn1")
    rest_in, rest_out = g_in[3:], g_out[2:]
    rest_tr = [(ii - 3, src, oi - 2, dst) for ii, src, oi, dst in g_tr[3:]]
    rest_land = place_own(rest_in, rest_out, rest_tr)
    g_send, g_recv, g_src, g_land, g_token = exchange_start(rest_in, rest_land, rest_tr, wd1, "gather_rest_start")
    bias_row = _pad_to(small['fox_f_bias'], (1, 128))
    gate_prm = _pad_to(jnp.concatenate([_pad_to(small['gdn_a_log'], (1, 128 - SMALL_A)),
                                        _pad_to(small['gdn_dt_bias'], (1, 128 - SMALL_A))], axis=0),
                       (8, 128 - SMALL_A))
    gate_prm = jnp.pad(gate_prm, ((0, 0), (SMALL_A, 0)))
    onorm = small['gdn_out_norm']

    sv1 = _ffn_fwd(x, small['ffn1_pre_norm'] + g_token[0, 0], wgu1, wd1, "ffn1")
    h1, = rowcall(lambda h, f, g: h + _half_rms(f, g), [x, sv1[3]], [small['ffn1_post_norm']], [(D_MODEL, F32)],
                  rows=512, total=s, name="ffn1_out")
    wgu2, wd2, w_in_g, w_out, w_q, w_kv, w_o, conv_g = exchange_wait(g_send, g_recv, g_src, g_land, h1, rest_tr,
                                                                   "gather_rest_wait")
    w_in = jnp.concatenate([w_in_g[:, j * IN_SHARD_PAD:j * IN_SHARD_PAD + IN_SHARD] for j in range(N_DEV)],
                           axis=1)
    sp = [0, 512, 1024, 1536, 1544, 2056, 2568, 3080, 3592, 3596, 3600]
    fq, fk, fv, ff, gq, gk, gv, gz, gb, ga = [w_in[:, sp[i]:sp[i + 1]] for i in range(10)]
    w_proj = jnp.concatenate([fq, fk, fv, gq, gk, gv, gz, ff, gb, ga,
                              jnp.zeros((D_MODEL, PROJ_W - 3584 - 16), BF)], axis=1)
    conv_w8 = conv_g.reshape(8, N_DEV, conv_pad)[:, :, :CONV_CH // N_DEV].reshape(8, CONV_CH)


    u2, = rowcall(_rms, [h1], [small['mix_pre_norm']], [(D_MODEL, BF)], rows=512, total=s, name="mix_pre")
    proj = mm(u2, w_proj, name="mix_proj")
    f_cum = fox_f_fwd(proj, bias_row)
    f_heads = f_cum[:, :FOX_HEADS]
    qkv_bf = proj[:, :3 * FOX_HEADS * FOX_DH].astype(BF)
    xk, xv = _fox_extras(s, 1.0, -f_heads), _fox_extras(s, 1.0, None)
    fox_flat, lse = fox_fwd(qkv_bf, _fox_extras(s, f_heads, 1.0), xk, xv)
    lse_heads = lse[:, :, :2].transpose(1, 0, 2).reshape(s, FOX_HEADS)
    cqkv = conv_fwd(proj, conv_w8)
    g_l, b_l = rowcall(_gdn_gates, [(proj, 128, SMALL_BLOCK128)], [gate_prm], [(512, F32), (512, F32)],
                       rows=512, total=s, name="gdn_gates")
    gbb = jnp.concatenate([g_l, b_l], axis=1)
    gdn_o, states = gdn_fwd(cqkv, proj, gbb, onorm)
    mixed = jnp.concatenate([fox_flat, gdn_o], axis=1).astype(BF)
    mo = mm(mixed, w_out, name="mix_out")
    h2, = rowcall(_residual_rms, [h1, mo], [small['mix_post_norm']], [(D_MODEL, F32)], rows=512, total=s,
                  name="mix_res")

    hq, = rowcall(_rms, [h2], [small['mem_pre_norm']], [(D_MODEL, BF)], rows=512, total=s, name="mem_pre")
    mn, = rowcall(_rms, [mem], [small['mem_kv_norm']], [(D_MODEL, BF)], rows=256, total=mem.shape[0], name="mem_kvn")
    q_mem = mm(hq, w_q, name="mem_q")
    kv_mem = mm(mn, w_kv, name="mem_kv")
    o_mem = xattn_fwd(q_mem, kv_mem)
    c_mem = mm(o_mem, w_o, name="mem_o")
    h3, = rowcall(_residual_rms, [h2, c_mem], [small['mem_post_norm']], [(D_MODEL, F32)], rows=512, total=s,
                  name="mem_res")

    sv2 = _ffn_fwd(h3, small['ffn2_pre_norm'], wgu2, wd2, "ffn2")

    def b_loss(h, f, tgt, g):
        err = h + _half_rms(f, g) - tgt
        part = 0.5 * jnp.sum(jnp.mean(err * err, axis=-1, keepdims=True), axis=0, keepdims=True)
        return err * (1.0 / D_MODEL), jnp.broadcast_to(part, (1, 128))

    dy, loss_acc = rowcall(b_loss, [h3, sv2[3], a['loss_target'][0]], [small['ffn2_post_norm']], [(D_MODEL, F32)],
                           [(1, 128)], rows=512, total=s, name="loss")

    grads = {}
    dh3, dwgu2, dwd2, grads['ffn2_pre_norm'], grads['ffn2_post_norm'] = _ffn_bwd(
        dy, h3, small['ffn2_pre_norm'], small['ffn2_post_norm'], wgu2, wd2, sv2, "ffn2")

    lead = ("lead", 0, 1, 0)

    def land(r, c, dt=BF):
        return jax.ShapeDtypeStruct((N_DEV, r, c), dt)

    ffn_tr = [(0, (1, 0, sp_, sp_), 0, lead), (0, (1, D_FF_PAD, sp_, sp_), 1, lead), (1, (0, 0, sp_, FF_SHARD), 2, lead)]
    ffn_land = [land(D_MODEL, sp_), land(D_MODEL, sp_), land(FF_SHARD, D_MODEL)]
    a_land = place_own([dwgu2, dwd2], ffn_land, ffn_tr)
    a_send, a_recv, a_src, a_land, a_token = exchange_start([dwgu2, dwd2], a_land, ffn_tr, dh3, "reduce_ffn2_start")

    dc, grads['mem_post_norm'] = rowcall(_bwd_residual, [dh3, c_mem], [small['mem_post_norm'] + a_token[0, 0]],
                                         [(D_MODEL, BF)],
                                         [(1, D_MODEL)], rows=512, total=s, name="mem_bwd_res")
    d_o = mm(dc, w_o, tb=True, name="mem_bwd_do")
    dw_o = mm(o_mem, dc, ta=True, out_dtype=BF, name="mem_bwd_dwo")
    dq_mem, dkv = xattn_bwd(q_mem, kv_mem, d_o)
    dhq = mm(dq_mem, w_q, tb=True, name="mem_bwd_dhq")
    dw_q = mm(hq, dq_mem, ta=True, out_dtype=BF, name="mem_bwd_dwq")
    dmn = mm(dkv, w_kv, tb=True, name="mem_bwd_dmn")
    dw_kv = mm(mn, dkv, ta=True, out_dtype=BF, name="mem_bwd_dwkv")
    _, grads['mem_kv_norm'] = rowcall(lambda d, mv, g: jax.vjp(_rms, mv, g)[1](d), [dmn, mem],
                                      [small['mem_kv_norm']], [(D_MODEL, F32)], [(1, D_MODEL)], rows=256,
                                      total=mem.shape[0], name="mem_bwd_kvn")

    def b_pre(dh, duv, hv, pg):
        dx, dpre = jax.vjp(_rms, hv, pg)[1](duv)
        return dh + dx, dpre

    dh2, grads['mem_pre_norm'] = rowcall(b_pre, [dh3, dhq, h2], [small['mem_pre_norm']], [(D_MODEL, F32)],
                                         [(1, D_MODEL)], rows=512, total=s, name="mem_bwd_pre")

    dmo, grads['mix_post_norm'] = rowcall(_bwd_residual, [dh2, mo], [small['mix_post_norm']], [(D_MODEL, BF)],
                                          [(1, D_MODEL)], rows=512, total=s, name="mix_bwd_res")
    d_mixed = mm(dmo, w_out, tb=True, name="mix_bwd_dmixed")
    dw_out = mm(mixed, dmo, ta=True, out_dtype=BF, name="mix_bwd_dwout")
    def b_delta(do, o):
        sel = (_iota2((512, 128), 0) // FOX_DH == _iota2((512, 128), 1)).astype(F32)
        return hdot(do * o, sel)

    delta, = rowcall(b_delta, [(d_mixed, 512, 0), fox_flat], [], [(128, F32)], rows=512, total=s, name="fox_delta")
    dqf, dkf, dvf = fox_bwd(qkv_bf, d_mixed[:, :512].astype(BF), xk, xv, _fox_extras(s, f_heads - lse_heads, 1.0),
                            _fox_extras(s, -delta[:, :FOX_HEADS], None))
    dqf, dkf = dqf.reshape(s, FOX_PAIRS, 256), dkf.reshape(s, FOX_PAIRS, 256)
    dfox_q = dqf[:, :, :128].reshape(s, 512) * FOX_SCALE
    dfox_k = dkf[:, :, :128].reshape(s, 512)
    df_q = dqf[:, :, 128:128 + 2 * XL].reshape(s, FOX_HEADS, XL)[:, :, 0]
    df_k = dkf[:, :, 128:128 + 2 * XL].reshape(s, FOX_HEADS, XL)[:, :, 3]
    d_f = _pad_to(df_q - df_k, (s, 128))
    dsmall_f, dbias = fox_f_bwd(proj, bias_row, d_f)
    grads['fox_f_bias'] = dbias[:, :FOX_HEADS]
    dcqkv, dz, dgb, grads['gdn_out_norm'] = gdn_bwd(cqkv, proj, gbb, onorm, states, d_mixed)

    def b_gates(sm, dsf, dg, db, prm):
        dsm, dprm = jax.vjp(_gdn_gates, sm, prm)[1]((dg, db))
        return dsm + dsf, dprm

    dsmall, dprm = rowcall(b_gates, [(proj, 128, SMALL_BLOCK128), dsmall_f, (dgb, 512, 0), (dgb, 512, 1)], [gate_prm],
                           [(128, F32)],
                           [(8, 128)], rows=512, total=s, name="gdn_bwd_gates")
    grads['gdn_a_log'] = dprm[0:1, SMALL_A:SMALL_A + GDN_HEADS]
    grads['gdn_dt_bias'] = dprm[1:2, SMALL_A:SMALL_A + GDN_HEADS]
    dqkv_pre, dconv8 = conv_bwd(proj, conv_w8, dcqkv)
    dproj = jnp.concatenate([dfox_q, dfox_k, dvf, dqkv_pre, dz, dsmall,
                             jnp.zeros((s, PROJ_W - 3584 - 128), F32)], axis=1).astype(BF)
    du2 = mm(dproj, w_proj, tb=True, name="mix_bwd_du")
    dw_proj = mm(u2, dproj, ta=True, out_dtype=BF, name="mix_bwd_dwproj")
    dh1, grads['mix_pre_norm'] = rowcall(b_pre, [dh2, du2, h1], [small['mix_pre_norm']], [(D_MODEL, F32)],
                                         [(1, D_MODEL)], rows=512, total=s, name="mix_bwd_pre")

    dw_in = jnp.concatenate([dw_proj[:, :1536], dw_proj[:, 3584:3592], dw_proj[:, 1536:3584],
                             dw_proj[:, 3592:3600]], axis=1)
    gap = jnp.zeros((D_MODEL, IN_SHARD_PAD - IN_SHARD), BF)
    dw_in = jnp.concatenate([piece for j in range(N_DEV) for piece in (dw_in[:, j * IN_SHARD:(j + 1) * IN_SHARD], gap)],
                            axis=1)
    b_in = [dw_in, dw_out, dw_q, dw_kv, dw_o]
    b_tr = [(0, (1, 0, IN_SHARD_PAD, IN_SHARD_PAD), 0, lead), (1, (0, 0, dm, dm), 1, lead), (2, (0, 0, dm, dm), 2, lead),
            (3, (1, 0, 2 * dm, 2 * dm), 3, lead), (4, (0, 0, dm, dm), 4, lead)]
    b_shapes = [land(D_MODEL, IN_SHARD_PAD), land(dm, D_MODEL), land(dm, D_MODEL), land(D_MODEL, 2 * dm),
                land(dm, D_MODEL)]
    b_land = place_own(b_in, b_shapes, b_tr)
    b_send, b_recv, b_src, b_land, b_token = exchange_start(b_in, b_land, b_tr, dh1, "reduce_mix_start")

    grad_x, dwgu1, dwd1, grads['ffn1_pre_norm'], grads['ffn1_post_norm'] = _ffn_bwd(
        dh1, x, small['ffn1_pre_norm'], small['ffn1_post_norm'] + b_token[0, 0], wgu1, wd1, sv1, "ffn1")

    gpack = _pack(lambda nm: grads[nm], conv=dconv8[:CONV_W], loss=loss_acc[:, :1])
    c_out = exchange([dwgu1, dwd1, gpack], ffn_land + [land(PACK_ROWS, D_MODEL, F32)],
                     ffn_tr + [(2, whole, 3, lead)], "reduce_ffn1")
    gsum_parts = c_out[3]
    a_got = exchange_wait(a_send, a_recv, a_src, a_land, gsum_parts, ffn_tr, "reduce_ffn2_wait")
    b_got = exchange_wait(b_send, b_recv, b_src, b_land, gsum_parts, b_tr, "reduce_mix_wait")
    recv = dict(zip(['ffn1_w_gate', 'ffn1_w_up', 'ffn1_w_down', 'ffn2_w_gate', 'ffn2_w_up', 'ffn2_w_down', 'w_in',
                     'w_out', 'mem_w_q', 'mem_w_kv', 'mem_w_o'], list(c_out[:3]) + a_got + b_got))

    out_g, out_d, out_m, out_v = {}, {}, {}, {}
    for nm in BIG:
        r = recv[nm]
        res = adamw(w2[nm], m2[nm], v2[nm], [(r, r.shape[2], 0, d) for d in range(N_DEV)], "adamw_" + nm)
        out_g[nm], out_d[nm], out_m[nm], out_v[nm] = res
    wp = _pack(lambda nm: small[nm])
    mp = _pack(lambda nm: m2[nm][None])
    vp = _pack(lambda nm: v2[nm][None])
    pg, pd, pm, pv = adamw(wp, mp, vp, [(gsum_parts, D_MODEL, 0, d) for d in range(N_DEV)], "adamw_small")
    for dst, p in ((out_g, pg), (out_d, pd), (out_m, pm), (out_v, pv)):
        dst.update({k: val[0] for k, val in _unpack(p).items()})
    loss = pg[ROW_MISC, COL_LOSS]
    conv_g = lax.dynamic_slice_in_dim(pg[ROW_CONV:ROW_CONV + 6].reshape(CONV_W, CONV_CH), me * (CONV_CH // N_DEV),
                                      CONV_CH // N_DEV, axis=1)
    res = adamw(w2['gdn_conv_w'], m2['gdn_conv_w'], v2['gdn_conv_w'], [conv_g], "adamw_conv")
    out_g['gdn_conv_w'], out_d['gdn_conv_w'], out_m['gdn_conv_w'], out_v['gdn_conv_w'] = res

    def depth(t):
        return t[None]

    return (loss, grad_x[None], *[depth(out_g[nm]) for nm in WEIGHTS], *[depth(out_d[nm]) for nm in WEIGHTS],
            *[depth(out_m[nm]) for nm in WEIGHTS], *[depth(out_v[nm]) for nm in WEIGHTS])


def kernel(x, mem, ffn1_pre_norm, ffn1_w_gate, ffn1_w_up, ffn1_w_down, ffn1_post_norm, mix_pre_norm, w_in, fox_f_bias, gdn_conv_w, gdn_a_log, gdn_dt_bias, gdn_out_norm, w_out, mix_post_norm, mem_pre_norm, mem_kv_norm, mem_w_q, mem_w_kv, mem_w_o, mem_post_norm, ffn2_pre_norm, ffn2_w_gate, ffn2_w_up, ffn2_w_down, ffn2_post_norm, loss_target, m_ffn1_pre_norm, m_ffn1_w_gate, m_ffn1_w_up, m_ffn1_w_down, m_ffn1_post_norm, m_mix_pre_norm, m_w_in, m_fox_f_bias, m_gdn_conv_w, m_gdn_a_log, m_gdn_dt_bias, m_gdn_out_norm, m_w_out, m_mix_post_norm, m_mem_pre_norm, m_mem_kv_norm, m_mem_w_q, m_mem_w_kv, m_mem_w_o, m_mem_post_norm, m_ffn2_pre_norm, m_ffn2_w_gate, m_ffn2_w_up, m_ffn2_w_down, m_ffn2_post_norm, v_ffn1_pre_norm, v_ffn1_w_gate, v_ffn1_w_up, v_ffn1_w_down, v_ffn1_post_norm, v_mix_pre_norm, v_w_in, v_fox_f_bias, v_gdn_conv_w, v_gdn_a_log, v_gdn_dt_bias, v_gdn_out_norm, v_w_out, v_mix_post_norm, v_mem_pre_norm, v_mem_kv_norm, v_mem_w_q, v_mem_w_kv, v_mem_w_o, v_mem_post_norm, v_ffn2_pre_norm, v_ffn2_w_gate, v_ffn2_w_up, v_ffn2_w_down, v_ffn2_post_norm):
    return _step(dict(locals()))
```

```python
import functools

import jax
import jax.numpy as jnp
from jax import lax
from jax.experimental import pallas as pl
from jax.experimental.pallas import tpu as pltpu

F32 = jnp.float32
BF = jnp.bfloat16
HI = lax.Precision.HIGHEST
MESH = pl.DeviceIdType.MESH

N_DEV = 8
EPS = 1e-6
D_MODEL = 1024
D_FF = 2816
FF_SHARD = D_FF // N_DEV
FF_SHARD_PAD = 384
D_FF_PAD = FF_SHARD_PAD * N_DEV
FOX_HEADS, FOX_DH = 8, 64
GDN_HEADS, GDN_DH = 4, 128
GDN_CHUNK = 64
CONV_W = 4
MEM_HEADS, MEM_DH = 4, 256
IN_W = 3600
IN_SHARD = IN_W // N_DEV
IN_SHARD_PAD = 512
PROJ_W = 4096
SMALL_F, SMALL_B, SMALL_A = 0, 8, 12

ADAM_LR, ADAM_B1, ADAM_B2, ADAM_EPS, ADAM_WD, ADAM_STEP = 0.001, 0.9, 0.999, 1e-08, 0.01, 10

VMEM_LIMIT = 56 * 1024 * 1024


def _params(sem=None):
    return pltpu.CompilerParams(dimension_semantics=sem, vmem_limit_bytes=VMEM_LIMIT)


def _tile(n, pref, unit=128):
    if n <= pref:
        return n
    t = (pref // unit) * unit
    while t > unit and n % t:
        t -= unit
    assert n % t == 0, (n, pref)
    return t


@functools.partial(jax.custom_vjp, nondiff_argnums=(2, 3))
def bdot(a, b, ca, cb):
    return lax.dot_general(a.astype(BF), b.astype(BF), (((ca,), (cb,)), ((), ())), preferred_element_type=F32)


def _bdot_fwd(a, b, ca, cb):
    return bdot(a, b, ca, cb), (a, b)


def _bdot_bwd(ca, cb, res, g):
    a, b = res
    da = bdot(g, b, 1, 1 - cb) if ca == 1 else bdot(b, g, 1 - cb, 1)
    db = bdot(a, g, 1 - ca, 0) if cb == 0 else bdot(g, a, 0, 1 - ca)
    return da, db


bdot.defvjp(_bdot_fwd, _bdot_bwd)


def hdot(a, b):
    return jnp.dot(a, b, precision=HI, preferred_element_type=F32)


def mdot(a, b):
    return jnp.dot(a, b, precision=lax.Precision.HIGH, preferred_element_type=F32)


def _iota2(shape, dim):
    return lax.broadcasted_iota(jnp.int32, shape, dim)


def _sigmoid(x):
    return 1.0 / (1.0 + jnp.exp(-x))


def _silu(x):
    return x * _sigmoid(x)


def _softplus(x):
    return jnp.maximum(x, 0.0) + jnp.log(1.0 + jnp.exp(-jnp.abs(x)))


def _rms(x, gain):
    return x * lax.rsqrt(jnp.mean(x * x, axis=-1, keepdims=True) + EPS) * gain


def mm(a, b, *, name, ta=False, tb=False, out_dtype=F32, tm=1024, tn=1024, tk=1024):
    m, k = (a.shape[1], a.shape[0]) if ta else a.shape
    n = b.shape[0] if tb else b.shape[1]
    assert k == (b.shape[1] if tb else b.shape[0]), (a.shape, b.shape, ta, tb)
    tm, tn, tk = _tile(m, tm), _tile(n, tn), _tile(k, tk)
    nk = k // tk
    dims = (((0 if ta else 1,), (1 if tb else 0,)), ((), ()))

    def kern(a_ref, b_ref, o_ref, *scratch):
        def part():
            return lax.dot_general(a_ref[...].astype(BF), b_ref[...].astype(BF), dims, preferred_element_type=F32)

        if nk == 1:
            o_ref[...] = part().astype(o_ref.dtype)
            return
        acc_ref, = scratch
        kk = pl.program_id(2)

        @pl.when(kk == 0)
        def _():
            acc_ref[...] = part()

        @pl.when(kk > 0)
        def _():
            acc_ref[...] += part()

        @pl.when(kk == nk - 1)
        def _():
            o_ref[...] = acc_ref[...].astype(o_ref.dtype)

    a_spec = pl.BlockSpec((tk, tm), lambda i, j, kk: (kk, i)) if ta else pl.BlockSpec((tm, tk), lambda i, j, kk: (i, kk))
    b_spec = pl.BlockSpec((tn, tk), lambda i, j, kk: (j, kk)) if tb else pl.BlockSpec((tk, tn), lambda i, j, kk: (kk, j))
    return pl.pallas_call(
        kern, name=name, grid=(m // tm, n // tn, nk),
        in_specs=[a_spec, b_spec],
        out_specs=pl.BlockSpec((tm, tn), lambda i, j, kk: (i, j)),
        out_shape=jax.ShapeDtypeStruct((m, n), out_dtype),
        scratch_shapes=[pltpu.VMEM((tm, tn), F32)] if nk > 1 else [],
        compiler_params=_params(("parallel", "parallel", "arbitrary")),
    )(a, b)


def _row_spec(item, rows):
    if not isinstance(item, tuple):
        return item, pl.BlockSpec((rows, item.shape[1]), lambda i: (i, 0))
    if len(item) == 3:
        arr, w, c = item
        return arr, pl.BlockSpec((rows, w), lambda i: (i, c))
    arr, w, c, lead = item
    return arr, pl.BlockSpec((None, rows, w), lambda i: (lead, i, c))


def _whole_spec(item):
    if not isinstance(item, tuple):
        return item, pl.BlockSpec(item.shape, lambda i: (0,) * item.ndim)
    arr, w, c = item
    return arr, pl.BlockSpec((arr.shape[0], w), lambda i: (0, c))


def rowcall(body, tiled, whole, outs, accs=(), *, rows, total, name):
    rows = min(rows, total)
    assert total % rows == 0
    t_arr, t_spec = zip(*[_row_spec(t, rows) for t in tiled])
    w_arr, w_spec = zip(*[_whole_spec(w) for w in whole]) if whole else ((), ())
    nt, nw, no, na = len(t_arr), len(w_arr), len(outs), len(accs)

    def kern(*refs):
        vals = [r[...] for r in refs[:nt + nw]]
        res = body(*vals)
        if not isinstance(res, (tuple, list)):
            res = (res,)
        assert len(res) == no + na, (name, len(res), no, na)
        for r, v in zip(refs[nt + nw:nt + nw + no], res[:no]):
            r[...] = v.astype(r.dtype)
        if na:
            acc_refs = refs[nt + nw + no:]

            @pl.when(pl.program_id(0) == 0)
            def _():
                for r in acc_refs:
                    r[...] = jnp.zeros_like(r)

            for r, v in zip(acc_refs, res[no:]):
                r[...] += v

    out_shape = [jax.ShapeDtypeStruct((total, w), d) for w, d in outs] + [jax.ShapeDtypeStruct(s, F32) for s in accs]
    out_specs = [pl.BlockSpec((rows, w), lambda i: (i, 0)) for w, _ in outs] + \
                [pl.BlockSpec(s, lambda i: (0, 0)) for s in accs]
    res = pl.pallas_call(
        kern, name=name, grid=(total // rows,),
        in_specs=list(t_spec) + list(w_spec), out_specs=out_specs, out_shape=out_shape,
        compiler_params=_params(("arbitrary",) if na else ("parallel",)),
    )(*t_arr, *w_arr)
    return res


def _colsum(x):
    return jnp.sum(x, axis=0, keepdims=True)


def _gdn_chunk(q, k, v, z, gb, bb, state, gain):
    c = GDN_CHUNK
    nh = len(q)
    hs = range(nh)
    r64, c64 = _iota2((c, c), 0), _iota2((c, c), 1)
    incl = r64 >= c64
    strict = r64 > c64
    ltri = incl.astype(F32)
    utri = (r64 <= c64).astype(F32)
    eye = (r64 == c64).astype(F32)
    ones = jnp.ones((c, c), F32)
    pick = (_iota2((GDN_DH, c), 0) == _iota2((GDN_DH, c), 1)).astype(F32)
    last = (_iota2((c, GDN_DH), 0) == c - 1).astype(F32)

    qn = [q[h] * lax.rsqrt(jnp.sum(q[h] * q[h], axis=-1, keepdims=True) + EPS) * (GDN_DH ** -0.5) for h in hs]
    kn = [k[h] * lax.rsqrt(jnp.sum(k[h] * k[h], axis=-1, keepdims=True) + EPS) for h in hs]
    gc = [mdot(ltri, gb[h]) for h in hs]
    g64 = [mdot(gb[h], pick) for h in hs]
    gcol = [mdot(ltri, g64[h]) for h in hs]
    grow = [mdot(ones, g64[h] * utri) for h in hs]
    dec = [jnp.exp(jnp.where(incl, gcol[h] - grow[h], -1e30)) for h in hs]
    kb = [kn[h] * bb[h] for h in hs]
    vb = [v[h] * bb[h] for h in hs]
    kk = [bdot(kb[h], kn[h], 1, 1) for h in hs]
    p = [-jnp.where(strict, kk[h] * dec[h], 0.0) for h in hs]
    tinv = [eye + p[h] for h in hs]
    for _ in range(5):
        p = [mdot(p[h], p[h]) for h in hs]
        tinv = [tinv[h] + mdot(tinv[h], p[h]) for h in hs]
    egc = [jnp.exp(gc[h]) for h in hs]
    u = [mdot(tinv[h], vb[h]) for h in hs]
    w = [mdot(tinv[h], kb[h] * egc[h]) for h in hs]
    attn = [bdot(qn[h], kn[h], 1, 1) * dec[h] for h in hs]
    qd = [qn[h] * egc[h] for h in hs]
    gl = [jnp.sum(gc[h] * last, axis=0, keepdims=True) for h in hs]
    kt = [kn[h] * jnp.exp(gl[h] - gc[h]) for h in hs]
    ws = [bdot(w[h], state[h], 1, 0) for h in hs]
    qs = [bdot(qd[h], state[h], 1, 0) for h in hs]
    v_new = [u[h] - ws[h] for h in hs]
    av = [bdot(attn[h], v_new[h], 1, 0) for h in hs]
    kv = [bdot(kt[h], v_new[h], 0, 0) for h in hs]
    new_state = tuple(state[h] * jnp.exp(gl[h]) + kv[h] for h in hs)
    o = tuple(_rms(qs[h] + av[h], gain) * _silu(z[h]) for h in hs)
    return o, new_state


GDN_ROWS = 512
GDN_W = GDN_HEADS * GDN_DH


def gdn_fwd(cqkv, proj, gbb, gain):
    s = cqkv.shape[0]
    nb, cpb = s // GDN_ROWS, GDN_ROWS // GDN_CHUNK
    h4 = GDN_HEADS

    def kern(qkv_ref, z_ref, gb_ref, gain_ref, o_ref, st_ref, state):
        @pl.when(pl.program_id(0) == 0)
        def _():
            state[...] = jnp.zeros_like(state)

        gain_v = gain_ref[...]

        def step(ci, carry):
            sl = pl.ds(pl.multiple_of(ci * GDN_CHUNK, GDN_CHUNK), GDN_CHUNK)
            ins = []
            for h in range(h4):
                ln = lambda base, h=h: slice(base + h * GDN_DH, base + (h + 1) * GDN_DH)
                ins.append((qkv_ref[sl, ln(0)], qkv_ref[sl, ln(GDN_W)], qkv_ref[sl, ln(2 * GDN_W)], z_ref[sl, ln(0)],
                            gb_ref[sl, ln(0)], gb_ref[sl, ln(GDN_W)], state[h]))
            cols = [tuple(col) for col in zip(*ins)]
            o, new = _gdn_chunk(*cols[:7], gain_v)
            for h in range(h4):
                st_ref[h, ci] = ins[h][6]
                o_ref[sl, h * GDN_DH:(h + 1) * GDN_DH] = o[h]
                state[h] = new[h]
            return carry

        lax.fori_loop(0, cpb, step, 0)

    return pl.pallas_call(
        kern, name="gdn_fwd", grid=(nb,),
        in_specs=[pl.BlockSpec((GDN_ROWS, 3 * GDN_W), lambda i: (i, 0)),
                  pl.BlockSpec((GDN_ROWS, GDN_W), lambda i: (i, 6)),
                  pl.BlockSpec((GDN_ROWS, 2 * GDN_W), lambda i: (i, 0)),
                  pl.BlockSpec((1, GDN_DH), lambda i: (0, 0))],
        out_specs=[pl.BlockSpec((GDN_ROWS, GDN_W), lambda i: (i, 0)),
                   pl.BlockSpec((h4, cpb, GDN_DH, GDN_DH), lambda i: (0, i, 0, 0))],
        out_shape=[jax.ShapeDtypeStruct((s, GDN_W), F32),
                   jax.ShapeDtypeStruct((h4, s // GDN_CHUNK, GDN_DH, GDN_DH), F32)],
        scratch_shapes=[pltpu.VMEM((h4, GDN_DH, GDN_DH), F32)],
        compiler_params=_params(("arbitrary",)),
    )(cqkv, proj, gbb, gain)


def gdn_bwd(cqkv, proj, gbb, gain, states, d_mixed):
    s = cqkv.shape[0]
    nb, cpb = s // GDN_ROWS, GDN_ROWS // GDN_CHUNK
    h4 = GDN_HEADS

    def kern(qkv_ref, z_ref, gb_ref, gain_ref, st_ref, do_ref, dqkv_ref, dz_ref, dgb_ref, dgain_ref, dstate):
        @pl.when(pl.program_id(0) == 0)
        def _():
            dgain_ref[...] = jnp.zeros_like(dgain_ref)
            dstate[...] = jnp.zeros_like(dstate)

        gain_v = gain_ref[...]

        def step(t, carry):
            ci = cpb - 1 - t
            sl = pl.ds(pl.multiple_of(ci * GDN_CHUNK, GDN_CHUNK), GDN_CHUNK)
            prim, cot, dst_in = [], [], []
            for h in range(h4):
                ln = lambda base, h=h: slice(base + h * GDN_DH, base + (h + 1) * GDN_DH)
                prim.append((qkv_ref[sl, ln(0)], qkv_ref[sl, ln(GDN_W)], qkv_ref[sl, ln(2 * GDN_W)], z_ref[sl, ln(0)],
                             gb_ref[sl, ln(0)], gb_ref[sl, ln(GDN_W)], st_ref[h, ci]))
                cot.append(do_ref[sl, ln(0)])
                dst_in.append(dstate[h])
            cols = [tuple(col) for col in zip(*prim)]
            vjp = jax.vjp(_gdn_chunk, *cols, gain_v)[1]
            dq, dk, dv, dz, dg, db, dst, dgn = vjp((tuple(cot), tuple(dst_in)))
            for h in range(h4):
                ln = lambda base, h=h: slice(base + h * GDN_DH, base + (h + 1) * GDN_DH)
                dqkv_ref[sl, ln(0)] = dq[h]
                dqkv_ref[sl, ln(GDN_W)] = dk[h]
                dqkv_ref[sl, ln(2 * GDN_W)] = dv[h]
                dz_ref[sl, ln(0)] = dz[h]
                dgb_ref[sl, ln(0)] = dg[h]
                dgb_ref[sl, ln(GDN_W)] = db[h]
                dstate[h] = dst[h]
            dgain_ref[...] += dgn
            return carry

        lax.fori_loop(0, cpb, step, 0)

    def rev(width, cblock=0):
        return pl.BlockSpec((GDN_ROWS, width), lambda i: (nb - 1 - i, cblock))

    return pl.pallas_call(
        kern, name="gdn_bwd", grid=(nb,),
        in_specs=[rev(3 * GDN_W), rev(GDN_W, 6), rev(2 * GDN_W), pl.BlockSpec((1, GDN_DH), lambda i: (0, 0)),
                  pl.BlockSpec((h4, cpb, GDN_DH, GDN_DH), lambda i: (0, nb - 1 - i, 0, 0)), rev(GDN_W, 1)],
        out_specs=[rev(3 * GDN_W), rev(GDN_W), rev(2 * GDN_W), pl.BlockSpec((1, GDN_DH), lambda i: (0, 0))],
        out_shape=[jax.ShapeDtypeStruct((s, 3 * GDN_W), F32), jax.ShapeDtypeStruct((s, GDN_W), F32),
                   jax.ShapeDtypeStruct((s, 2 * GDN_W), F32), jax.ShapeDtypeStruct((1, GDN_DH), F32)],
        scratch_shapes=[pltpu.VMEM((h4, GDN_DH, GDN_DH), F32)],
        compiler_params=_params(("arbitrary",)),
    )(cqkv, proj, gbb, gain, states, d_mixed)


def _gdn_gates(small, prm):
    w = GDN_HEADS * GDN_DH
    lane, head = _iota2((128, w), 0), _iota2((128, w), 1) // GDN_DH
    sel_b = (lane == SMALL_B + head).astype(F32)
    sel_a = (lane == SMALL_A + head).astype(F32)
    prow = _iota2((8, 128), 0)
    a_log = jnp.sum(prm * (prow == 0).astype(F32), axis=0, keepdims=True)
    dt_b = jnp.sum(prm * (prow == 1).astype(F32), axis=0, keepdims=True)
    beta = _sigmoid(hdot(small, sel_b))
    g = hdot(-jnp.exp(a_log) * _softplus(small + dt_b), sel_a)
    return g, beta


CONV_ROWS = 1024
CONV_COLS = 128
CONV_BLOCK0 = 1536 // CONV_COLS


def _shift_down(prev8, cur, s):
    ext = jnp.concatenate([prev8, cur], axis=0)
    return pltpu.roll(ext, s, 0)[8:]


def _shift_up(cur, next8, s):
    n = cur.shape[0]
    ext = jnp.concatenate([cur, next8], axis=0)
    return pltpu.roll(ext, n + 8 - s, 0)[:n]


def _conv_pre(x_ref, w, ci, nchunk):
    r0 = pl.multiple_of(ci * CONV_ROWS, CONV_ROWS)
    cur = x_ref[pl.ds(r0, CONV_ROWS), :]
    prev = x_ref[pl.ds(pl.multiple_of(jnp.maximum(r0 - 8, 0), 8), 8), :]
    prev = jnp.where(ci > 0, prev, 0.0)
    shifted = [cur] + [_shift_down(prev, cur, s) for s in range(1, CONV_W)]
    pre = w[CONV_W - 1:CONV_W, :] * cur
    for s in range(1, CONV_W):
        pre = pre + w[CONV_W - 1 - s:CONV_W - s, :] * shifted[s]
    return r0, pre, shifted


def conv_fwd(proj, conv_w8):
    s = proj.shape[0]
    nchunk = s // CONV_ROWS
    ncol = 3 * GDN_HEADS * GDN_DH // CONV_COLS

    def kern(x_ref, w_ref, y_ref):
        w = w_ref[...]

        def step(ci, carry):
            r0, pre, _ = _conv_pre(x_ref, w, ci, nchunk)
            y_ref[pl.ds(r0, CONV_ROWS), :] = _silu(pre)
            return carry

        lax.fori_loop(0, nchunk, step, 0)

    return pl.pallas_call(
        kern, name="conv_fwd", grid=(ncol,),
        in_specs=[pl.BlockSpec((s, CONV_COLS), lambda j: (0, CONV_BLOCK0 + j)),
                  pl.BlockSpec((8, CONV_COLS), lambda j: (0, j))],
        out_specs=pl.BlockSpec((s, CONV_COLS), lambda j: (0, j)),
        out_shape=jax.ShapeDtypeStruct((s, ncol * CONV_COLS), F32),
        compiler_params=_params(("parallel",)),
    )(proj, conv_w8)


def conv_bwd(proj, conv_w8, dy):
    s = proj.shape[0]
    nchunk = s // CONV_ROWS
    per = 3 * GDN_HEADS * GDN_DH // CONV_COLS
    outs = []
    for part in range(1):
        def kern(x_ref, w_ref, dy_ref, dx_ref, dw_ref, dpre_ref):
            w = w_ref[...]
            rows8 = _iota2((8, CONV_COLS), 0)

            def step1(ci, dw):
                r0, pre, shifted = _conv_pre(x_ref, w, ci, nchunk)
                sg = _sigmoid(pre)
                dpre = dy_ref[pl.ds(r0, CONV_ROWS), :] * sg * (1.0 + pre * (1.0 - sg))
                dpre_ref[pl.ds(r0, CONV_ROWS), :] = dpre
                for sh in range(CONV_W):
                    dw = dw + jnp.where(rows8 == CONV_W - 1 - sh, _colsum(dpre * shifted[sh]), 0.0)
                return dw

            dw_ref[...] = lax.fori_loop(0, nchunk, step1, jnp.zeros((8, CONV_COLS), F32))

            def step2(ci, carry):
                r0 = pl.multiple_of(ci * CONV_ROWS, CONV_ROWS)
                cur = dpre_ref[pl.ds(r0, CONV_ROWS), :]
                nxt = dpre_ref[pl.ds(pl.multiple_of(jnp.minimum(r0 + CONV_ROWS, s - 8), 8), 8), :]
                nxt = jnp.where(ci < nchunk - 1, nxt, 0.0)
                dx = w[CONV_W - 1:CONV_W, :] * cur
                for sh in range(1, CONV_W):
                    dx = dx + w[CONV_W - 1 - sh:CONV_W - sh, :] * _shift_up(cur, nxt, sh)
                dx_ref[pl.ds(r0, CONV_ROWS), :] = dx
                return carry

            lax.fori_loop(0, nchunk, step2, 0)

        outs.append(pl.pallas_call(
            kern, name=f"conv_bwd{part}", grid=(per,),
            in_specs=[pl.BlockSpec((s, CONV_COLS), lambda j, part=part: (0, CONV_BLOCK0 + part * per + j)),
                      pl.BlockSpec((8, CONV_COLS), lambda j, part=part: (0, part * per + j)),
                      pl.BlockSpec((s, CONV_COLS), lambda j: (0, j))],
            out_specs=[pl.BlockSpec((s, CONV_COLS), lambda j: (0, j)),
                       pl.BlockSpec((8, CONV_COLS), lambda j: (0, j))],
            out_shape=[jax.ShapeDtypeStruct((s, per * CONV_COLS), F32),
                       jax.ShapeDtypeStruct((8, per * CONV_COLS), F32)],
            scratch_shapes=[pltpu.VMEM((s, CONV_COLS), F32)],
            compiler_params=_params(("parallel",)),
        )(proj, conv_w8, dy))
    dx = jnp.concatenate([o[0] for o in outs], axis=1)
    dw = jnp.concatenate([o[1] for o in outs], axis=1)
    return dx, dw


FOXF_ROWS = 512
SMALL_BLOCK128 = 3584 // 128


def _log_sigmoid(x):
    return jnp.minimum(x, 0.0) - jnp.log(1.0 + jnp.exp(-jnp.abs(x)))


def fox_f_fwd(proj, bias_row):
    s = proj.shape[0]
    n = s // FOXF_ROWS

    def kern(x_ref, b_ref, f_ref, carry):
        @pl.when(pl.program_id(0) == 0)
        def _():
            carry[...] = jnp.zeros_like(carry)

        heads = _iota2((FOXF_ROWS, 128), 1) < FOX_HEADS
        lf = jnp.where(heads, _log_sigmoid(x_ref[...] + b_ref[...]), 0.0)
        ltri = (_iota2((FOXF_ROWS, FOXF_ROWS), 0) >= _iota2((FOXF_ROWS, FOXF_ROWS), 1)).astype(F32)
        c = hdot(ltri, lf) + carry[...]
        f_ref[...] = c
        carry[...] = c[FOXF_ROWS - 1:FOXF_ROWS, :]

    return pl.pallas_call(
        kern, name="fox_f_fwd", grid=(n,),
        in_specs=[pl.BlockSpec((FOXF_ROWS, 128), lambda i: (i, SMALL_BLOCK128)),
                  pl.BlockSpec((1, 128), lambda i: (0, 0))],
        out_specs=pl.BlockSpec((FOXF_ROWS, 128), lambda i: (i, 0)),
        out_shape=jax.ShapeDtypeStruct((s, 128), F32),
        scratch_shapes=[pltpu.VMEM((1, 128), F32)],
        compiler_params=_params(("arbitrary",)),
    )(proj, bias_row)


def fox_f_bwd(proj, bias_row, d_f):
    s = proj.shape[0]
    n = s // FOXF_ROWS

    def kern(x_ref, b_ref, df_ref, dx_ref, db_ref, carry):
        @pl.when(pl.program_id(0) == 0)
        def _():
            carry[...] = jnp.zeros_like(carry)
            db_ref[...] = jnp.zeros_like(db_ref)

        heads = _iota2((FOXF_ROWS, 128), 1) < FOX_HEADS
        utri = (_iota2((FOXF_ROWS, FOXF_ROWS), 0) <= _iota2((FOXF_ROWS, FOXF_ROWS), 1)).astype(F32)
        rc = hdot(utri, df_ref[...]) + carry[...]
        carry[...] = rc[0:1, :]
        dx = jnp.where(heads, rc * _sigmoid(-(x_ref[...] + b_ref[...])), 0.0)
        dx_ref[...] = dx
        db_ref[...] += _colsum(dx)

    return pl.pallas_call(
        kern, name="fox_f_bwd", grid=(n,),
        in_specs=[pl.BlockSpec((FOXF_ROWS, 128), lambda i: (n - 1 - i, SMALL_BLOCK128)),
                  pl.BlockSpec((1, 128), lambda i: (0, 0)),
                  pl.BlockSpec((FOXF_ROWS, 128), lambda i: (n - 1 - i, 0))],
        out_specs=[pl.BlockSpec((FOXF_ROWS, 128), lambda i: (n - 1 - i, 0)),
                   pl.BlockSpec((1, 128), lambda i: (0, 0))],
        out_shape=[jax.ShapeDtypeStruct((s, 128), F32), jax.ShapeDtypeStruct((1, 128), F32)],
        scratch_shapes=[pltpu.VMEM((1, 128), F32)],
        compiler_params=_params(("arbitrary",)),
    )(proj, bias_row, d_f)


FOX_T = 512
FOX_SCALE = FOX_DH ** -0.5
FOX_PAIRS = FOX_HEADS // 2
NEG = -1e30
_NT = (((1,), (1,)), ((), ()))


def _split3(x):
    def bf(v):
        return lax.reduce_precision(v, exponent_bits=8, mantissa_bits=7)

    hi = bf(x)
    mid = bf(x - hi)
    lo = bf(x - hi - mid)
    return jnp.stack([hi, mid, lo], axis=-1)


def _fox_extras(s, first, second):
    def part(v):
        if v is None:
            return jnp.zeros((s, FOX_HEADS, 3), F32)
        if isinstance(v, float):
            return jnp.full((s, FOX_HEADS, 3), v, F32)
        return _split3(v)

    cols = jnp.concatenate([part(first), part(second)], axis=-1)
    cols = _pad_to(cols, (s, FOX_HEADS, FOX_DH)).reshape(s, FOX_PAIRS, 2, FOX_DH)
    cols = cols[:, :, ::-1, :].reshape(s, FOX_PAIRS, 2 * FOX_DH)
    return cols.transpose(1, 0, 2).astype(BF)


def _head_masks(rows):
    lane = _iota2((rows, 2 * FOX_DH), 1)
    return lane < FOX_DH, lane >= FOX_DH


def _extra_lane(e, slot):
    return (FOX_DH if e == 0 else 0) + slot


def fox_fwd(qkv, xq, xk, xv):
    s = qkv.shape[0]
    t = min(FOX_T, s)
    n = s // t

    def kern(q_ref, k_ref, v_ref, xq_ref, xk_ref, xv_ref, o_ref, lse_ref):
        i = pl.program_id(1)
        masks = _head_masks(t)
        q_pair, x_pair = q_ref[...] * FOX_SCALE, xq_ref[...]
        q_ops = [jnp.where(mk, q_pair, x_pair) for mk in masks]

        def step(j, carry, masked):
            sl = pl.ds(pl.multiple_of(j * t, t), t)
            k_pair, xk_pair, v_pair, xv_pair = k_ref[sl, :], xk_ref[sl, :], v_ref[sl, :], xv_ref[sl, :]
            k_ops = [jnp.where(mk, k_pair, xk_pair) for mk in masks]
            v_ops = [jnp.where(mk, v_pair, xv_pair) for mk in masks]
            sc = [lax.dot_general(q_ops[e], k_ops[e], _NT, preferred_element_type=F32) for e in range(2)]
            if masked:
                keep = _iota2((t, t), 0) >= _iota2((t, t), 1)
                sc = [jnp.where(keep, x, NEG) for x in sc]
            m_new = [jnp.maximum(carry[e][0], jnp.max(sc[e], axis=1, keepdims=True)) for e in range(2)]
            p = [jnp.exp(sc[e] - m_new[e]).astype(BF) for e in range(2)]
            pv = [jnp.dot(p[e], v_ops[e], preferred_element_type=F32) for e in range(2)]
            return tuple((m_new[e], jnp.exp(carry[e][0] - m_new[e]) * carry[e][1] + pv[e]) for e in range(2))

        init = tuple((jnp.full((t, 1), NEG, F32), jnp.zeros((t, 2 * FOX_DH), F32)) for _ in range(2))
        carry = lax.fori_loop(0, i, lambda j, c: step(j, c, False), init)
        carry = step(i, carry, True)
        lane = _iota2((t, 2 * FOX_DH), 1)
        outs, lses = [], []
        for e in range(2):
            m, acc = carry[e]
            l = jnp.sum(jnp.where(lane == _extra_lane(e, 0), acc, 0.0), axis=1, keepdims=True)
            outs.append(acc / l)
            lses.append(m + jnp.log(l))
        o_ref[...] = jnp.where(masks[0], outs[0], outs[1])
        lse_ref[...] = jnp.where(lane == 0, lses[0], jnp.where(lane == 1, lses[1], 0.0))

    pr = FOX_PAIRS
    return pl.pallas_call(
        kern, name="fox_fwd", grid=(pr, n),
        in_specs=[pl.BlockSpec((t, 128), lambda p, i: (i, p)),
                  pl.BlockSpec((s, 128), lambda p, i: (0, pr + p)),
                  pl.BlockSpec((s, 128), lambda p, i: (0, 2 * pr + p)),
                  pl.BlockSpec((None, t, 128), lambda p, i: (p, i, 0)),
                  pl.BlockSpec((None, s, 128), lambda p, i: (p, 0, 0)),
                  pl.BlockSpec((None, s, 128), lambda p, i: (p, 0, 0))],
        out_specs=[pl.BlockSpec((t, 128), lambda p, i: (i, p)),
                   pl.BlockSpec((None, t, 128), lambda p, i: (p, i, 0))],
        out_shape=[jax.ShapeDtypeStruct((s, FOX_HEADS * FOX_DH), F32), jax.ShapeDtypeStruct((pr, s, 128), F32)],
        compiler_params=_params(("parallel", "parallel")),
    )(qkv, qkv, qkv, xq, xk, xv)


def fox_bwd(qkv, d_o, xk, xv, xqb, xdo):
    s = qkv.shape[0]
    t = min(FOX_T, s)
    n = s // t
    w = 2 * FOX_DH

    def kern(k_ref, v_ref, xk_ref, xv_ref, q_ref, do_ref, xq_ref, xd_ref, dq_ref, dk_ref, dv_ref):
        j = pl.program_id(1)

        @pl.when(j == 0)
        def _():
            dq_ref[...] = jnp.zeros_like(dq_ref)

        masks = _head_masks(t)
        k_ops = [jnp.where(mk, k_ref[...], xk_ref[...]) for mk in masks]
        v_ops = [jnp.where(mk, v_ref[...], xv_ref[...]) for mk in masks]

        def step(i, carry, masked):
            dk, dv = carry
            sl = pl.ds(pl.multiple_of(i * t, t), t)
            q_pair, xq_pair, do_pair, xd_pair = q_ref[sl, :] * FOX_SCALE, xq_ref[sl, :], do_ref[sl, :], xd_ref[sl, :]
            q_ops = [jnp.where(mk, q_pair, xq_pair) for mk in masks]
            do_ops = [jnp.where(mk, do_pair, xd_pair) for mk in masks]
            do_own = [jnp.where(mk, do_pair, 0).astype(BF) for mk in masks]
            st = [lax.dot_general(k_ops[e], q_ops[e], _NT, preferred_element_type=F32) for e in range(2)]
            dp = [lax.dot_general(v_ops[e], do_ops[e], _NT, preferred_element_type=F32) for e in range(2)]
            if masked:
                keep = _iota2((t, t), 0) <= _iota2((t, t), 1)
                st = [jnp.where(keep, x, NEG) for x in st]
            pt = [jnp.exp(x) for x in st]
            dsb = [(pt[e] * dp[e]).astype(BF) for e in range(2)]
            dv = dv + sum(jnp.dot(pt[e].astype(BF), do_own[e], preferred_element_type=F32) for e in range(2))
            dk = tuple(dk[e] + jnp.dot(dsb[e], q_ops[e], preferred_element_type=F32) for e in range(2))
            for e in range(2):
                dq_ref[sl, e * w:(e + 1) * w] += lax.dot_general(dsb[e], k_ops[e], (((0,), (0,)), ((), ())),
                                                                 preferred_element_type=F32)
            return dk, dv

        init = ((jnp.zeros((t, w), F32), jnp.zeros((t, w), F32)), jnp.zeros((t, w), F32))
        carry = step(j, init, True)
        dk, dv = lax.fori_loop(j + 1, n, lambda i, c: step(i, c, False), carry)
        dk_ref[...] = jnp.concatenate(dk, axis=1)
        dv_ref[...] = dv

    pr = FOX_PAIRS
    return pl.pallas_call(
        kern, name="fox_bwd", grid=(pr, n),
        in_specs=[pl.BlockSpec((t, 128), lambda p, j: (j, pr + p)),
                  pl.BlockSpec((t, 128), lambda p, j: (j, 2 * pr + p)),
                  pl.BlockSpec((None, t, 128), lambda p, j: (p, j, 0)),
                  pl.BlockSpec((None, t, 128), lambda p, j: (p, j, 0)),
                  pl.BlockSpec((s, 128), lambda p, j: (0, p)),
                  pl.BlockSpec((s, 128), lambda p, j: (0, p)),
                  pl.BlockSpec((None, s, 128), lambda p, j: (p, 0, 0)),
                  pl.BlockSpec((None, s, 128), lambda p, j: (p, 0, 0))],
        out_specs=[pl.BlockSpec((s, 256), lambda p, j: (0, p)),
                   pl.BlockSpec((t, 256), lambda p, j: (j, p)),
                   pl.BlockSpec((t, 128), lambda p, j: (j, p))],
        out_shape=[jax.ShapeDtypeStruct((s, pr * 256), F32), jax.ShapeDtypeStruct((s, pr * 256), F32),
                   jax.ShapeDtypeStruct((s, FOX_HEADS * FOX_DH), F32)],
        compiler_params=_params(("parallel", "arbitrary")),
    )(qkv, qkv, xk, xv, qkv, d_o, xqb, xdo)


def _xattn_head(q, k, v):
    sc = bdot(q, k, 1, 1) * (MEM_DH ** -0.5)
    e = jnp.exp(sc - lax.stop_gradient(jnp.max(sc, axis=-1, keepdims=True)))
    p = e / jnp.sum(e, axis=-1, keepdims=True)
    return bdot(p, v, 1, 0)


def xattn_fwd(q, kv):
    s = q.shape[0]
    hh = MEM_HEADS

    def body(*vals):
        qs, ks, vs = vals[:hh], vals[hh:2 * hh], vals[2 * hh:]
        return jnp.concatenate([_xattn_head(qs[a], ks[a], vs[a]) for a in range(hh)], axis=1)

    return rowcall(body, [(q, MEM_DH, a) for a in range(hh)],
                   [(kv, MEM_DH, a) for a in range(2 * hh)],
                   [(hh * MEM_DH, BF)], rows=512, total=s, name="xattn_fwd")[0]


def xattn_bwd(q, kv, d_o):
    s = q.shape[0]
    hh = MEM_HEADS

    def body(*vals):
        qs, dos = vals[:hh], vals[hh:2 * hh]
        ks, vs = vals[2 * hh:3 * hh], vals[3 * hh:]
        dqs, dks, dvs = [], [], []
        for a in range(hh):
            _, vjp = jax.vjp(_xattn_head, qs[a], ks[a], vs[a])
            dq, dk, dv = vjp(dos[a])
            dqs.append(dq)
            dks.append(dk)
            dvs.append(dv)
        return jnp.concatenate(dqs, axis=1), jnp.concatenate(dks + dvs, axis=1)

    return rowcall(body, [(q, MEM_DH, a) for a in range(hh)] + [(d_o, MEM_DH, a) for a in range(hh)],
                   [(kv, MEM_DH, a) for a in range(2 * hh)],
                   [(hh * MEM_DH, BF)], [kv.shape], rows=512, total=s, name="xattn_bwd")


def _slab(ref, axis, start, size):
    if axis is None:
        return ref
    if axis == "lead":
        return ref.at[start]
    idx = pl.ds(pl.multiple_of(start, 128 if axis == 1 else 16), size)
    return ref.at[idx] if axis == 0 else ref.at[:, idx]


def exchange(inputs, outputs, transfers, name):
    ni, no, nt = len(inputs), len(outputs), len(transfers)
    npeer = N_DEV - 1

    def body(*refs):
        ins, outs = refs[:ni], refs[ni:ni + no]
        send, recv, loc = refs[ni + no:]
        x, y, c = lax.axis_index("x"), lax.axis_index("y"), lax.axis_index("c")
        me = 4 * x + 2 * y + c

        def peer(p):
            px = 1 - x if p & 4 else x
            py = 1 - y if p & 2 else y
            pc = 1 - c if p & 1 else c
            return (px, py, pc), 4 * px + 2 * py + pc

        def view(ref, spec, who):
            axis, off, stride, size = spec
            return _slab(ref, axis, off + who * stride, size)

        local, remote = [], []
        for w, (ii, src, oi, dst) in enumerate(transfers):
            cp = pltpu.make_async_copy(view(ins[ii], src, me), view(outs[oi], dst, me), loc.at[w])
            cp.start()
            local.append(cp)
        for p in range(1, N_DEV):
            dev, idx = peer(p)
            for w, (ii, src, oi, dst) in enumerate(transfers):
                k = w * npeer + p - 1
                out_cp = pltpu.make_async_remote_copy(
                    src_ref=view(ins[ii], src, idx), dst_ref=view(outs[oi], dst, me), send_sem=send.at[k],
                    recv_sem=recv.at[k], device_id=dev, device_id_type=MESH)
                out_cp.start()
                in_cp = pltpu.make_async_remote_copy(
                    src_ref=view(ins[ii], src, idx), dst_ref=view(outs[oi], dst, idx), send_sem=send.at[k],
                    recv_sem=recv.at[k], device_id=dev, device_id_type=MESH)
                remote.append((out_cp, in_cp))
        for out_cp, in_cp in remote:
            in_cp.wait_recv()
            out_cp.wait_send()
        for cp in local:
            cp.wait()

    hbm = pl.BlockSpec(memory_space=pl.ANY)
    return pl.pallas_call(
        body, name=name, in_specs=[hbm] * ni, out_specs=[hbm] * no, out_shape=list(outputs),
        scratch_shapes=[pltpu.SemaphoreType.DMA((nt * npeer,)), pltpu.SemaphoreType.DMA((nt * npeer,)),
                        pltpu.SemaphoreType.DMA((nt,))],
        compiler_params=pltpu.CompilerParams(has_side_effects=True),
    )(*inputs)


def _peer(p):
    x, y, c = lax.axis_index("x"), lax.axis_index("y"), lax.axis_index("c")
    px = 1 - x if p & 4 else x
    py = 1 - y if p & 2 else y
    pc = 1 - c if p & 1 else c
    return (px, py, pc), 4 * px + 2 * py + pc


def _view(ref, spec, who):
    axis, off, stride, size = spec
    return _slab(ref, axis, off + who * stride, size)


def place_own(inputs, outputs, transfers):
    me = 4 * lax.axis_index("x") + 2 * lax.axis_index("y") + lax.axis_index("c")
    lands = [jnp.zeros(o.shape, o.dtype) for o in outputs]
    for ii, src, oi, dst in transfers:
        axis, off, stride, size = src
        own = inputs[ii] if axis is None else lax.dynamic_slice_in_dim(inputs[ii], off + me * stride, size, axis)
        axis, off, stride, size = dst
        if axis == "lead":
            lands[oi] = lax.dynamic_update_slice_in_dim(lands[oi], own[None], me, 0)
        else:
            lands[oi] = lax.dynamic_update_slice_in_dim(lands[oi], own, off + me * stride, axis)
    return lands


_HBM = pl.BlockSpec(memory_space=pltpu.HBM)
_SEM = pl.BlockSpec(memory_space=pltpu.SEMAPHORE)
_EFFECT = pltpu.SideEffectType.DATAFLOW_SIDE_EFFECTING


def _remote_copies(ins, lands, transfers, send, recv):
    npeer = N_DEV - 1
    me = 4 * lax.axis_index("x") + 2 * lax.axis_index("y") + lax.axis_index("c")
    pairs = []
    for p in range(1, N_DEV):
        dev, idx = _peer(p)
        for w, (ii, src, oi, dst) in enumerate(transfers):
            k = w * npeer + p - 1
            common = dict(src_ref=_view(ins[ii], src, idx), send_sem=send.at[k], recv_sem=recv.at[k],
                          device_id=dev, device_id_type=MESH)
            pairs.append((pltpu.make_async_remote_copy(dst_ref=_view(lands[oi], dst, me), **common),
                          pltpu.make_async_remote_copy(dst_ref=_view(lands[oi], dst, idx), **common)))
    return pairs


def exchange_start(inputs, lands, transfers, after, name):
    ni, nl, nsem = len(inputs), len(lands), len(transfers) * (N_DEV - 1)

    def body(*refs):
        ins, lnd = refs[:ni], refs[ni:ni + nl]
        send, recv = refs[ni + nl + 1], refs[ni + nl + 2]
        token = refs[-1]
        for out_cp, _ in _remote_copies(ins, lnd, transfers, send, recv):
            out_cp.start()
        token[...] = jnp.zeros_like(token)

    args = [pltpu.with_memory_space_constraint(a, pltpu.HBM) for a in list(inputs) + list(lands)]
    res = pl.pallas_call(
        body, name=name,
        out_shape=(pltpu.SemaphoreType.DMA((nsem,)), pltpu.SemaphoreType.DMA((nsem,)),
                   *[pltpu.HBM(a.shape, a.dtype) for a in args], jax.ShapeDtypeStruct((8, 128), F32)),
        in_specs=[_HBM] * (ni + nl) + [pl.BlockSpec(memory_space=pl.ANY)],
        out_specs=(_SEM, _SEM, *[_HBM] * (ni + nl), pl.BlockSpec(memory_space=pltpu.VMEM)),
        input_output_aliases={k: k + 2 for k in range(ni + nl)},
        compiler_params=pltpu.CompilerParams(has_side_effects=_EFFECT),
    )(*args, after)
    return res[0], res[1], list(res[2:2 + ni]), list(res[2 + ni:2 + ni + nl]), res[-1]


def exchange_wait(send, recv, inputs, lands, after, transfers, name):
    ni, nl = len(inputs), len(lands)

    def body(*refs):
        ins, lnd = refs[:ni], refs[ni:ni + nl]
        send_r, recv_r = refs[ni + nl], refs[ni + nl + 1]
        for out_cp, in_cp in _remote_copies(ins, lnd, transfers, send_r, recv_r):
            out_cp.wait_send()
            in_cp.wait_recv()

    res = pl.pallas_call(
        body, name=name,
        out_shape=tuple(pltpu.HBM(a.shape, a.dtype) for a in list(inputs) + list(lands)),
        in_specs=[_HBM] * (ni + nl) + [_SEM, _SEM, pl.BlockSpec(memory_space=pl.ANY)],
        out_specs=tuple([_HBM] * (ni + nl)),
        input_output_aliases={k: k for k in range(ni + nl)},
        compiler_params=pltpu.CompilerParams(has_side_effects=_EFFECT),
    )(*inputs, *lands, send, recv, after)
    return list(res[ni:])


def adamw(w, m, v, contribs, name):
    r, c = w.shape
    nc = len(contribs)
    rows = next((r // d for d in (4, 2) if r % d == 0 and (r // d) % 16 == 0), r)
    c1, c2 = 1.0 - ADAM_B1 ** ADAM_STEP, 1.0 - ADAM_B2 ** ADAM_STEP

    def body(wv, mv, vv, *gs):
        g = gs[0].astype(F32)
        for extra in gs[1:]:
            g = g + extra.astype(F32)
        g = g[:, :c]
        m_new = ADAM_B1 * mv + (1.0 - ADAM_B1) * g
        v_new = ADAM_B2 * vv + (1.0 - ADAM_B2) * (g * g)
        delta = -ADAM_LR * ((m_new / c1) / (jnp.sqrt(v_new / c2) + ADAM_EPS) + ADAM_WD * wv)
        return g, delta, m_new, v_new

    assert nc >= 1
    return rowcall(body, [w, m, v] + list(contribs), [], [(c, F32)] * 4, rows=rows, total=r, name=name)


WEIGHTS = ['ffn1_pre_norm', 'ffn1_w_gate', 'ffn1_w_up', 'ffn1_w_down', 'ffn1_post_norm', 'mix_pre_norm', 'w_in',
           'fox_f_bias', 'gdn_conv_w', 'gdn_a_log', 'gdn_dt_bias', 'gdn_out_norm', 'w_out', 'mix_post_norm',
           'mem_pre_norm', 'mem_kv_norm', 'mem_w_q', 'mem_w_kv', 'mem_w_o', 'mem_post_norm', 'ffn2_pre_norm',
           'ffn2_w_gate', 'ffn2_w_up', 'ffn2_w_down', 'ffn2_post_norm']
GAINS = ['ffn1_pre_norm', 'ffn1_post_norm', 'mix_pre_norm', 'mix_post_norm', 'mem_pre_norm', 'mem_kv_norm',
         'mem_post_norm', 'ffn2_pre_norm', 'ffn2_post_norm']
BIG = ['ffn1_w_gate', 'ffn1_w_up', 'ffn1_w_down', 'w_in', 'w_out', 'mem_w_q', 'mem_w_kv', 'mem_w_o',
       'ffn2_w_gate', 'ffn2_w_up', 'ffn2_w_down']
PACK_ROWS = 24
ROW_MISC = len(GAINS)
ROW_CONV = ROW_MISC + 1
COL_FBIAS, COL_ALOG, COL_DTB, COL_ONORM, COL_LOSS = 0, 8, 12, 128, 256
CONV_CH = 3 * GDN_HEADS * GDN_DH


def _pad_to(a, shape):
    return jnp.pad(a, [(0, t - s) for s, t in zip(a.shape, shape)])


def _pack(get, conv=None, loss=None):
    rows = [get(nm) for nm in GAINS]
    misc = jnp.concatenate([get('fox_f_bias'), get('gdn_a_log'), get('gdn_dt_bias'),
                            jnp.zeros((1, COL_ONORM - COL_DTB - 4), F32), get('gdn_out_norm'),
                            jnp.zeros((1, 1), F32) if loss is None else loss.reshape(1, 1)], axis=1)
    rows.append(_pad_to(misc, (1, D_MODEL)))
    rows.append(jnp.zeros((6, D_MODEL), F32) if conv is None else conv.reshape(6, D_MODEL))
    return _pad_to(jnp.concatenate(rows, axis=0), (PACK_ROWS, D_MODEL))


def _unpack(p):
    out = {nm: p[i:i + 1] for i, nm in enumerate(GAINS)}
    misc = p[ROW_MISC:ROW_MISC + 1]
    out['fox_f_bias'] = misc[:, COL_FBIAS:COL_FBIAS + FOX_HEADS]
    out['gdn_a_log'] = misc[:, COL_ALOG:COL_ALOG + GDN_HEADS]
    out['gdn_dt_bias'] = misc[:, COL_DTB:COL_DTB + GDN_HEADS]
    out['gdn_out_norm'] = misc[:, COL_ONORM:COL_ONORM + GDN_DH]
    return out


def _ffn_fwd(h, pre, wgu, wd, tag):
    s = h.shape[0]
    u, = rowcall(_rms, [h], [pre], [(D_MODEL, BF)], rows=512, total=s, name=tag + "_pre")
    gu = mm(u, wgu, out_dtype=BF, name=tag + "_gate_up")
    act, = rowcall(lambda a, b: _silu(a.astype(F32)) * b.astype(F32), [(gu, D_FF_PAD, 0), (gu, D_FF_PAD, 1)], [],
                   [(D_FF_PAD, BF)],
                   rows=256, total=s, name=tag + "_act")
    if callable(wd):
        wd = wd(act)
    f = mm(act, wd, name=tag + "_down")
    return u, gu, act, f


def _half_rms(a, g):
    return 0.5 * _rms(a, g)


def _ffn_bwd(dh_out, h, pre, post, wgu, wd, saved, tag, on_weights=None):
    u, gu, act, f = saved
    s = h.shape[0]

    def b_post(dh, fv, pg):
        return jax.vjp(_half_rms, fv, pg)[1](dh)

    df, dpost = rowcall(b_post, [dh_out, f], [post], [(D_MODEL, BF)], [(1, D_MODEL)], rows=512, total=s,
                        name=tag + "_bwd_post")
    dact = mm(df, wd, tb=True, out_dtype=BF, name=tag + "_bwd_dact")
    dwd = mm(act, df, ta=True, out_dtype=BF, name=tag + "_bwd_dwd")

    def b_act(a, b, da):
        dg, du = jax.vjp(lambda g_, u_: _silu(g_) * u_, a.astype(F32), b.astype(F32))[1](da.astype(F32))
        return jnp.concatenate([dg, du], axis=1)

    dgu, = rowcall(b_act, [(gu, D_FF_PAD, 0), (gu, D_FF_PAD, 1), dact], [], [(2 * D_FF_PAD, BF)], rows=256, total=s,
                   name=tag + "_bwd_act")
    du = mm(dgu, wgu, tb=True, name=tag + "_bwd_du")
    dwgu = mm(u, dgu, ta=True, out_dtype=BF, name=tag + "_bwd_dwgu")
    if on_weights is not None:
        pre = pre + on_weights(dwgu, dwd)

    def b_pre(dh, duv, hv, pg):
        dx, dpre = jax.vjp(_rms, hv, pg)[1](duv)
        return dh + dx, dpre

    dh, dpre = rowcall(b_pre, [dh_out, du, h], [pre], [(D_MODEL, F32)], [(1, D_MODEL)], rows=512, total=s,
                       name=tag + "_bwd_pre")
    return dh, dwgu, dwd, dpre, dpost


def _residual_rms(h, a, g):
    return h + _rms(a, g)


def _bwd_residual(dh, a, g):
    return jax.vjp(_rms, a, g)[1](dh)


def _step(a):
    x, mem = a['x'][0], a['mem'][0]
    s = x.shape[0]
    me = 4 * lax.axis_index("x") + 2 * lax.axis_index("y") + lax.axis_index("c")
    w2 = {nm: a[nm][0] for nm in WEIGHTS}
    m2 = {nm: a['m_' + nm][0] for nm in WEIGHTS}
    v2 = {nm: a['v_' + nm][0] for nm in WEIGHTS}
    small = {nm: w2[nm][None] for nm in WEIGHTS if nm not in BIG and nm != 'gdn_conv_w'}

    def ff_cols(w):
        return _pad_to(w, (D_MODEL, FF_SHARD_PAD)).astype(BF)

    def ff_rows(w):
        return _pad_to(w, (FF_SHARD_PAD, D_MODEL)).astype(BF)

    whole = (None, 0, 0, 0)
    conv_pad = 256
    g_in = [ff_cols(w2['ffn1_w_gate']), ff_cols(w2['ffn1_w_up']), ff_rows(w2['ffn1_w_down']),
            ff_cols(w2['ffn2_w_gate']), ff_cols(w2['ffn2_w_up']), ff_rows(w2['ffn2_w_down']),
            _pad_to(w2['w_in'], (D_MODEL, IN_SHARD_PAD)).astype(BF), w2['w_out'].astype(BF),
            w2['mem_w_q'].astype(BF), w2['mem_w_kv'].astype(BF), w2['mem_w_o'].astype(BF),
            _pad_to(w2['gdn_conv_w'], (8, conv_pad))]
    g_out = [jax.ShapeDtypeStruct((D_MODEL, 2 * D_FF_PAD), BF), jax.ShapeDtypeStruct((D_FF_PAD, D_MODEL), BF),
             jax.ShapeDtypeStruct((D_MODEL, 2 * D_FF_PAD), BF), jax.ShapeDtypeStruct((D_FF_PAD, D_MODEL), BF),
             jax.ShapeDtypeStruct((D_MODEL, N_DEV * IN_SHARD_PAD), BF), jax.ShapeDtypeStruct((D_MODEL, D_MODEL), BF),
             jax.ShapeDtypeStruct((D_MODEL, D_MODEL), BF), jax.ShapeDtypeStruct((D_MODEL, 2 * D_MODEL), BF),
             jax.ShapeDtypeStruct((D_MODEL, D_MODEL), BF), jax.ShapeDtypeStruct((8, N_DEV * conv_pad), F32)]
    sp_, dm = FF_SHARD_PAD, D_MODEL // N_DEV
    g_tr = [(0, whole, 0, (1, 0, sp_, sp_)), (1, whole, 0, (1, D_FF_PAD, sp_, sp_)), (2, whole, 1, (0, 0, sp_, sp_)),
            (3, whole, 2, (1, 0, sp_, sp_)), (4, whole, 2, (1, D_FF_PAD, sp_, sp_)), (5, whole, 3, (0, 0, sp_, sp_)),
            (6, whole, 4, (1, 0, IN_SHARD_PAD, IN_SHARD_PAD)), (7, whole, 5, (0, 0, dm, dm)),
            (8, whole, 6, (0, 0, dm, dm)), (9, whole, 7, (1, 0, 2 * dm, 2 * dm)), (10, whole, 8, (0, 0, dm, dm)),
            (11, whole, 9, (1, 0, conv_pad, conv_pad))]
    def pick(idx):
        ins = sorted({g_tr[k][0] for k in idx})
        outs = sorted({g_tr[k][2] for k in idx})
        tr = [(ins.index(g_tr[k][0]), g_tr[k][1], outs.index(g_tr[k][2]), g_tr[k][3]) for k in idx]
        return [g_in[i] for i in ins], [g_out[o] for o in outs], tr

    gu_in, gu_out, gu_tr = pick([0, 1])
    wgu1, = exchange(gu_in, gu_out, gu_tr, "gather_gate_up")
    stages, after = [], wgu1
    for nm, idx in (("down", [2]), ("mix", [6, 7, 11]), ("late", [8, 9, 10, 3, 4, 5])):
        st_in, st_out, st_tr = pick(idx)
        st = exchange_start(st_in, place_own(st_in, st_out, st_tr), st_tr, after, "gather_%s_start" % nm)
        stages.append((st, st_tr, "gather_%s_wait" % nm))
        after = st[4]
    g_token = after

    def gather_wait(k, after_):
        (send_, recv_, src_, land_, _), tr_, nm_ = stages[k]
        return exchange_wait(send_, recv_, src_, land_, after_, tr_, nm_)

    bias_row = _pad_to(small['fox_f_bias'], (1, 128))
    gate_prm = _pad_to(jnp.concatenate([_pad_to(small['gdn_a_log'], (1, 128 - SMALL_A)),
                                        _pad_to(small['gdn_dt_bias'], (1, 128 - SMALL_A))], axis=0),
                       (8, 128 - SMALL_A))
    gate_prm = jnp.pad(gate_prm, ((0, 0), (SMALL_A, 0)))
    onorm = small['gdn_out_norm']

    late = {}

    def wd1_when(act):
        late['wd1'], = gather_wait(0, act)
        return late['wd1']

    sv1 = _ffn_fwd(x, small['ffn1_pre_norm'] + g_token[0, 0], wgu1, wd1_when, "ffn1")
    wd1 = late['wd1']
    h1, = rowcall(lambda h, f, g: h + _half_rms(f, g), [x, sv1[3]], [small['ffn1_post_norm']], [(D_MODEL, F32)],
                  rows=512, total=s, name="ffn1_out")
    w_in_g, w_out, conv_g = gather_wait(1, h1)
    w_in = jnp.concatenate([w_in_g[:, j * IN_SHARD_PAD:j * IN_SHARD_PAD + IN_SHARD] for j in range(N_DEV)],
                           axis=1)
    sp = [0, 512, 1024, 1536, 1544, 2056, 2568, 3080, 3592, 3596, 3600]
    fq, fk, fv, ff, gq, gk, gv, gz, gb, ga = [w_in[:, sp[i]:sp[i + 1]] for i in range(10)]
    w_proj = jnp.concatenate([fq, fk, fv, gq, gk, gv, gz, ff, gb, ga,
                              jnp.zeros((D_MODEL, PROJ_W - 3584 - 16), BF)], axis=1)
    conv_w8 = conv_g.reshape(8, N_DEV, conv_pad)[:, :, :CONV_CH // N_DEV].reshape(8, CONV_CH)


    u2, = rowcall(_rms, [h1], [small['mix_pre_norm']], [(D_MODEL, BF)], rows=512, total=s, name="mix_pre")
    proj = mm(u2, w_proj, name="mix_proj")
    f_cum = fox_f_fwd(proj, bias_row)
    f_heads = f_cum[:, :FOX_HEADS]
    qkv_bf = proj[:, :3 * FOX_HEADS * FOX_DH].astype(BF)
    xk, xv = _fox_extras(s, 1.0, -f_heads), _fox_extras(s, 1.0, None)
    fox_flat, lse = fox_fwd(qkv_bf, _fox_extras(s, f_heads, 1.0), xk, xv)
    lse_heads = lse[:, :, :2].transpose(1, 0, 2).reshape(s, FOX_HEADS)
    cqkv = conv_fwd(proj, conv_w8)
    g_l, b_l = rowcall(_gdn_gates, [(proj, 128, SMALL_BLOCK128)], [gate_prm], [(512, F32), (512, F32)],
                       rows=512, total=s, name="gdn_gates")
    gbb = jnp.concatenate([g_l, b_l], axis=1)
    gdn_o, states = gdn_fwd(cqkv, proj, gbb, onorm)
    mixed = jnp.concatenate([fox_flat, gdn_o], axis=1).astype(BF)
    mo = mm(mixed, w_out, name="mix_out")
    h2, = rowcall(_residual_rms, [h1, mo], [small['mix_post_norm']], [(D_MODEL, F32)], rows=512, total=s,
                  name="mix_res")

    hq, = rowcall(_rms, [h2], [small['mem_pre_norm']], [(D_MODEL, BF)], rows=512, total=s, name="mem_pre")
    mn, = rowcall(_rms, [mem], [small['mem_kv_norm']], [(D_MODEL, BF)], rows=256, total=mem.shape[0], name="mem_kvn")
    wgu2, wd2, w_q, w_kv, w_o = gather_wait(2, h2)
    q_mem = mm(hq, w_q, name="mem_q")
    kv_mem = mm(mn, w_kv, name="mem_kv")
    o_mem = xattn_fwd(q_mem, kv_mem)
    c_mem = mm(o_mem, w_o, name="mem_o")
    h3, = rowcall(_residual_rms, [h2, c_mem], [small['mem_post_norm']], [(D_MODEL, F32)], rows=512, total=s,
                  name="mem_res")

    sv2 = _ffn_fwd(h3, small['ffn2_pre_norm'], wgu2, wd2, "ffn2")

    def b_loss(h, f, tgt, g):
        err = h + _half_rms(f, g) - tgt
        part = 0.5 * jnp.sum(jnp.mean(err * err, axis=-1, keepdims=True), axis=0, keepdims=True)
        return err * (1.0 / D_MODEL), jnp.broadcast_to(part, (1, 128))

    dy, loss_acc = rowcall(b_loss, [h3, sv2[3], a['loss_target'][0]], [small['ffn2_post_norm']], [(D_MODEL, F32)],
                           [(1, 128)], rows=512, total=s, name="loss")

    grads = {}
    dh3, dwgu2, dwd2, grads['ffn2_pre_norm'], grads['ffn2_post_norm'] = _ffn_bwd(
        dy, h3, small['ffn2_pre_norm'], small['ffn2_post_norm'], wgu2, wd2, sv2, "ffn2")

    lead = ("lead", 0, 1, 0)

    def land(r, c, dt=BF):
        return jax.ShapeDtypeStruct((N_DEV, r, c), dt)

    ffn_tr = [(0, (1, 0, sp_, sp_), 0, lead), (0, (1, D_FF_PAD, sp_, sp_), 1, lead), (1, (0, 0, sp_, FF_SHARD), 2, lead)]
    ffn_land = [land(D_MODEL, sp_), land(D_MODEL, sp_), land(FF_SHARD, D_MODEL)]
    a_land = place_own([dwgu2, dwd2], ffn_land, ffn_tr)
    a_send, a_recv, a_src, a_land, a_token = exchange_start([dwgu2, dwd2], a_land, ffn_tr, dh3, "reduce_ffn2_start")

    dc, grads['mem_post_norm'] = rowcall(_bwd_residual, [dh3, c_mem], [small['mem_post_norm'] + a_token[0, 0]],
                                         [(D_MODEL, BF)],
                                         [(1, D_MODEL)], rows=512, total=s, name="mem_bwd_res")
    d_o = mm(dc, w_o, tb=True, name="mem_bwd_do")
    dw_o = mm(o_mem, dc, ta=True, out_dtype=BF, name="mem_bwd_dwo")
    dq_mem, dkv = xattn_bwd(q_mem, kv_mem, d_o)
    dhq = mm(dq_mem, w_q, tb=True, name="mem_bwd_dhq")
    dw_q = mm(hq, dq_mem, ta=True, out_dtype=BF, name="mem_bwd_dwq")
    dmn = mm(dkv, w_kv, tb=True, name="mem_bwd_dmn")
    dw_kv = mm(mn, dkv, ta=True, out_dtype=BF, name="mem_bwd_dwkv")
    _, grads['mem_kv_norm'] = rowcall(lambda d, mv, g: jax.vjp(_rms, mv, g)[1](d), [dmn, mem],
                                      [small['mem_kv_norm']], [(D_MODEL, F32)], [(1, D_MODEL)], rows=256,
                                      total=mem.shape[0], name="mem_bwd_kvn")

    def b_pre(dh, duv, hv, pg):
        dx, dpre = jax.vjp(_rms, hv, pg)[1](duv)
        return dh + dx, dpre

    dh2, grads['mem_pre_norm'] = rowcall(b_pre, [dh3, dhq, h2], [small['mem_pre_norm']], [(D_MODEL, F32)],
                                         [(1, D_MODEL)], rows=512, total=s, name="mem_bwd_pre")

    dmo, grads['mix_post_norm'] = rowcall(_bwd_residual, [dh2, mo], [small['mix_post_norm']], [(D_MODEL, BF)],
                                          [(1, D_MODEL)], rows=512, total=s, name="mix_bwd_res")
    d_mixed = mm(dmo, w_out, tb=True, name="mix_bwd_dmixed")
    dw_out = mm(mixed, dmo, ta=True, out_dtype=BF, name="mix_bwd_dwout")
    def b_delta(do, o):
        sel = (_iota2((512, 128), 0) // FOX_DH == _iota2((512, 128), 1)).astype(F32)
        return hdot(do * o, sel)

    delta, = rowcall(b_delta, [(d_mixed, 512, 0), fox_flat], [], [(128, F32)], rows=512, total=s, name="fox_delta")
    dqf, dkf, dvf = fox_bwd(qkv_bf, d_mixed[:, :512].astype(BF), xk, xv, _fox_extras(s, f_heads - lse_heads, 1.0),
                            _fox_extras(s, -delta[:, :FOX_HEADS], None))
    own = (jnp.arange(2)[:, None] == jnp.arange(2 * FOX_DH)[None, :] // FOX_DH)
    dqf, dkf = dqf.reshape(s, FOX_PAIRS, 2, 2 * FOX_DH), dkf.reshape(s, FOX_PAIRS, 2, 2 * FOX_DH)
    dfox_q = jnp.sum(jnp.where(own, dqf, 0.0), axis=2).reshape(s, 512) * FOX_SCALE
    dfox_k = jnp.sum(jnp.where(own, dkf, 0.0), axis=2).reshape(s, 512)
    df_q = jnp.stack([dqf[:, :, e, _extra_lane(e, 0)] for e in range(2)], axis=-1).reshape(s, FOX_HEADS)
    df_k = jnp.stack([dkf[:, :, e, _extra_lane(e, 3)] for e in range(2)], axis=-1).reshape(s, FOX_HEADS)
    d_f = _pad_to(df_q - df_k, (s, 128))
    dsmall_f, dbias = fox_f_bwd(proj, bias_row, d_f)
    grads['fox_f_bias'] = dbias[:, :FOX_HEADS]
    dcqkv, dz, dgb, grads['gdn_out_norm'] = gdn_bwd(cqkv, proj, gbb, onorm, states, d_mixed)

    def b_gates(sm, dsf, dg, db, prm):
        dsm, dprm = jax.vjp(_gdn_gates, sm, prm)[1]((dg, db))
        return dsm + dsf, dprm

    dsmall, dprm = rowcall(b_gates, [(proj, 128, SMALL_BLOCK128), dsmall_f, (dgb, 512, 0), (dgb, 512, 1)], [gate_prm],
                           [(128, F32)],
                           [(8, 128)], rows=512, total=s, name="gdn_bwd_gates")
    grads['gdn_a_log'] = dprm[0:1, SMALL_A:SMALL_A + GDN_HEADS]
    grads['gdn_dt_bias'] = dprm[1:2, SMALL_A:SMALL_A + GDN_HEADS]
    dqkv_pre, dconv8 = conv_bwd(proj, conv_w8, dcqkv)
    dproj = jnp.concatenate([dfox_q, dfox_k, dvf, dqkv_pre, dz, dsmall,
                             jnp.zeros((s, PROJ_W - 3584 - 128), F32)], axis=1).astype(BF)
    du2 = mm(dproj, w_proj, tb=True, name="mix_bwd_du")
    dw_proj = mm(u2, dproj, ta=True, out_dtype=BF, name="mix_bwd_dwproj")
    dh1, grads['mix_pre_norm'] = rowcall(b_pre, [dh2, du2, h1], [small['mix_pre_norm']], [(D_MODEL, F32)],
                                         [(1, D_MODEL)], rows=512, total=s, name="mix_bwd_pre")

    dw_in = jnp.concatenate([dw_proj[:, :1536], dw_proj[:, 3584:3592], dw_proj[:, 1536:3584],
                             dw_proj[:, 3592:3600]], axis=1)
    gap = jnp.zeros((D_MODEL, IN_SHARD_PAD - IN_SHARD), BF)
    dw_in = jnp.concatenate([piece for j in range(N_DEV) for piece in (dw_in[:, j * IN_SHARD:(j + 1) * IN_SHARD], gap)],
                            axis=1)
    b_in = [dw_in, dw_out, dw_q, dw_kv, dw_o]
    b_tr = [(0, (1, 0, IN_SHARD_PAD, IN_SHARD_PAD), 0, lead), (1, (0, 0, dm, dm), 1, lead), (2, (0, 0, dm, dm), 2, lead),
            (3, (1, 0, 2 * dm, 2 * dm), 3, lead), (4, (0, 0, dm, dm), 4, lead)]
    b_shapes = [land(D_MODEL, IN_SHARD_PAD), land(dm, D_MODEL), land(dm, D_MODEL), land(D_MODEL, 2 * dm),
                land(dm, D_MODEL)]
    b_land = place_own(b_in, b_shapes, b_tr)
    b_send, b_recv, b_src, b_land, b_token = exchange_start(b_in, b_land, b_tr, dh1, "reduce_mix_start")

    def start_ffn1_reduce(dwgu, dwd):
        c_land = place_own([dwgu, dwd], ffn_land, ffn_tr)
        late['c'] = exchange_start([dwgu, dwd], c_land, ffn_tr, dwd, "reduce_ffn1_start")
        return late['c'][4][0, 0]

    grad_x, dwgu1, dwd1, grads['ffn1_pre_norm'], grads['ffn1_post_norm'] = _ffn_bwd(
        dh1, x, small['ffn1_pre_norm'], small['ffn1_post_norm'] + b_token[0, 0], wgu1, wd1, sv1, "ffn1",
        on_weights=start_ffn1_reduce)

    gpack = _pack(lambda nm: grads[nm], conv=dconv8[:CONV_W], loss=loss_acc[:, :1])
    gsum_parts, = exchange([gpack], [land(PACK_ROWS, D_MODEL, F32)], [(0, whole, 0, lead)], "reduce_small")
    a_got = exchange_wait(a_send, a_recv, a_src, a_land, gsum_parts, ffn_tr, "reduce_ffn2_wait")
    b_got = exchange_wait(b_send, b_recv, b_src, b_land, gsum_parts, b_tr, "reduce_mix_wait")
    recv = dict(zip(['ffn2_w_gate', 'ffn2_w_up', 'ffn2_w_down', 'w_in', 'w_out', 'mem_w_q', 'mem_w_kv', 'mem_w_o'],
                    a_got + b_got))

    out_g, out_d, out_m, out_v = {}, {}, {}, {}

    def update(nm):
        r = recv[nm]
        res = adamw(w2[nm], m2[nm], v2[nm], [(r, r.shape[2], 0, d) for d in range(N_DEV)], "adamw_" + nm)
        out_g[nm], out_d[nm], out_m[nm], out_v[nm] = res

    for nm in recv:
        update(nm)
    wp = _pack(lambda nm: small[nm])
    mp = _pack(lambda nm: m2[nm][None])
    vp = _pack(lambda nm: v2[nm][None])
    pg, pd, pm, pv = adamw(wp, mp, vp, [(gsum_parts, D_MODEL, 0, d) for d in range(N_DEV)], "adamw_small")
    for dst, p in ((out_g, pg), (out_d, pd), (out_m, pm), (out_v, pv)):
        dst.update({k: val[0] for k, val in _unpack(p).items()})
    loss = pg[ROW_MISC, COL_LOSS]
    conv_g = lax.dynamic_slice_in_dim(pg[ROW_CONV:ROW_CONV + 6].reshape(CONV_W, CONV_CH), me * (CONV_CH // N_DEV),
                                      CONV_CH // N_DEV, axis=1)
    res = adamw(w2['gdn_conv_w'], m2['gdn_conv_w'], v2['gdn_conv_w'], [conv_g], "adamw_conv")
    out_g['gdn_conv_w'], out_d['gdn_conv_w'], out_m['gdn_conv_w'], out_v['gdn_conv_w'] = res

    done = sum(out_d[nm][0, 0] for nm in recv) + out_d['gdn_conv_w'][0, 0] + pd[0, 0]
    c_send, c_recv, c_src, c_land, _ = late['c']
    c_got = exchange_wait(c_send, c_recv, c_src, c_land, jnp.zeros((8, 128), F32) + done, ffn_tr, "reduce_ffn1_wait")
    recv = dict(zip(['ffn1_w_gate', 'ffn1_w_up', 'ffn1_w_down'], c_got))
    for nm in recv:
        update(nm)

    def depth(t):
        return t[None]

    return (loss, grad_x[None], *[depth(out_g[nm]) for nm in WEIGHTS], *[depth(out_d[nm]) for nm in WEIGHTS],
            *[depth(out_m[nm]) for nm in WEIGHTS], *[depth(out_v[nm]) for nm in WEIGHTS])


def kernel(x, mem, ffn1_pre_norm, ffn1_w_gate, ffn1_w_up, ffn1_w_down, ffn1_post_norm, mix_pre_norm, w_in, fox_f_bias, gdn_conv_w, gdn_a_log, gdn_dt_bias, gdn_out_norm, w_out, mix_post_norm, mem_pre_norm, mem_kv_norm, mem_w_q, mem_w_kv, mem_w_o, mem_post_norm, ffn2_pre_norm, ffn2_w_gate, ffn2_w_up, ffn2_w_down, ffn2_post_norm, loss_target, m_ffn1_pre_norm, m_ffn1_w_gate, m_ffn1_w_up, m_ffn1_w_down, m_ffn1_post_norm, m_mix_pre_norm, m_w_in, m_fox_f_bias, m_gdn_conv_w, m_gdn_a_log, m_gdn_dt_bias, m_gdn_out_norm, m_w_out, m_mix_post_norm, m_mem_pre_norm, m_mem_kv_norm, m_mem_w_q, m_mem_w_kv, m_mem_w_o, m_mem_post_norm, m_ffn2_pre_norm, m_ffn2_w_gate, m_ffn2_w_up, m_ffn2_w_down, m_ffn2_post_norm, v_ffn1_pre_norm, v_ffn1_w_gate, v_ffn1_w_up, v_ffn1_w_down, v_ffn1_post_norm, v_mix_pre_norm, v_w_in, v_fox_f_bias, v_gdn_conv_w, v_gdn_a_log, v_gdn_dt_bias, v_gdn_out_norm, v_w_out, v_mix_post_norm, v_mem_pre_norm, v_mem_kv_norm, v_mem_w_q, v_mem_w_kv, v_mem_w_o, v_mem_post_norm, v_ffn2_pre_norm, v_ffn2_w_gate, v_ffn2_w_up, v_ffn2_w_down, v_ffn2_post_norm):
    return _step(dict(locals()))
```

```python
import functools

import jax
import jax.numpy as jnp
from jax import lax
from jax.experimental import pallas as pl
from jax.experimental.pallas import tpu as pltpu

F32 = jnp.float32
BF = jnp.bfloat16
HI = lax.Precision.HIGHEST
MESH = pl.DeviceIdType.MESH

N_DEV = 8
EPS = 1e-6
D_MODEL = 1024
D_FF = 2816
FF_SHARD = D_FF // N_DEV
FF_SHARD_PAD = 384
D_FF_PAD = FF_SHARD_PAD * N_DEV
FOX_HEADS, FOX_DH = 8, 64
GDN_HEADS, GDN_DH = 4, 128
GDN_CHUNK = 64
CONV_W = 4
MEM_HEADS, MEM_DH = 4, 256
IN_W = 3600
IN_SHARD = IN_W // N_DEV
IN_SHARD_PAD = 512
PROJ_W = 4096
SMALL_F, SMALL_B, SMALL_A = 0, 8, 12

ADAM_LR, ADAM_B1, ADAM_B2, ADAM_EPS, ADAM_WD, ADAM_STEP = 0.001, 0.9, 0.999, 1e-08, 0.01, 10

VMEM_LIMIT = 56 * 1024 * 1024


def _params(sem=None):
    return pltpu.CompilerParams(dimension_semantics=sem, vmem_limit_bytes=VMEM_LIMIT)


def _tile(n, pref, unit=128):
    if n <= pref:
        return n
    t = (pref // unit) * unit
    while t > unit and n % t:
        t -= unit
    assert n % t == 0, (n, pref)
    return t


@functools.partial(jax.custom_vjp, nondiff_argnums=(2, 3))
def bdot(a, b, ca, cb):
    return lax.dot_general(a.astype(BF), b.astype(BF), (((ca,), (cb,)), ((), ())), preferred_element_type=F32)


def _bdot_fwd(a, b, ca, cb):
    return bdot(a, b, ca, cb), (a, b)


def _bdot_bwd(ca, cb, res, g):
    a, b = res
    da = bdot(g, b, 1, 1 - cb) if ca == 1 else bdot(b, g, 1 - cb, 1)
    db = bdot(a, g, 1 - ca, 0) if cb == 0 else bdot(g, a, 0, 1 - ca)
    return da, db


bdot.defvjp(_bdot_fwd, _bdot_bwd)


def hdot(a, b):
    return jnp.dot(a, b, precision=HI, preferred_element_type=F32)


def mdot(a, b):
    return jnp.dot(a, b, precision=lax.Precision.HIGH, preferred_element_type=F32)


def _iota2(shape, dim):
    return lax.broadcasted_iota(jnp.int32, shape, dim)


def _sigmoid(x):
    return 1.0 / (1.0 + jnp.exp(-x))


def _silu(x):
    return x * _sigmoid(x)


def _softplus(x):
    return jnp.maximum(x, 0.0) + jnp.log(1.0 + jnp.exp(-jnp.abs(x)))


def _rms(x, gain):
    return x * lax.rsqrt(jnp.mean(x * x, axis=-1, keepdims=True) + EPS) * gain


def mm(a, b, *, name, ta=False, tb=False, out_dtype=F32, tm=1024, tn=1024, tk=1024, token=None):
    m, k = (a.shape[1], a.shape[0]) if ta else a.shape
    n = b.shape[0] if tb else b.shape[1]
    assert k == (b.shape[1] if tb else b.shape[0]), (a.shape, b.shape, ta, tb)
    tm, tn, tk = _tile(m, tm), _tile(n, tn), _tile(k, tk)
    nk = k // tk
    dims = (((0 if ta else 1,), (1 if tb else 0,)), ((), ()))

    def kern(a_ref, b_ref, *rest):
        o_ref, scratch = (rest[1], rest[2:]) if token is not None else (rest[0], rest[1:])

        def part():
            return lax.dot_general(a_ref[...].astype(BF), b_ref[...].astype(BF), dims, preferred_element_type=F32)

        if nk == 1:
            o_ref[...] = part().astype(o_ref.dtype)
            return
        acc_ref, = scratch
        kk = pl.program_id(2)

        @pl.when(kk == 0)
        def _():
            acc_ref[...] = part()

        @pl.when(kk > 0)
        def _():
            acc_ref[...] += part()

        @pl.when(kk == nk - 1)
        def _():
            o_ref[...] = acc_ref[...].astype(o_ref.dtype)

    a_spec = pl.BlockSpec((tk, tm), lambda i, j, kk: (kk, i)) if ta else pl.BlockSpec((tm, tk), lambda i, j, kk: (i, kk))
    b_spec = pl.BlockSpec((tn, tk), lambda i, j, kk: (j, kk)) if tb else pl.BlockSpec((tk, tn), lambda i, j, kk: (kk, j))
    return pl.pallas_call(
        kern, name=name, grid=(m // tm, n // tn, nk),
        in_specs=[a_spec, b_spec] + ([pl.BlockSpec((8, 128), lambda i, j, kk: (0, 0))] if token is not None else []),
        out_specs=pl.BlockSpec((tm, tn), lambda i, j, kk: (i, j)),
        out_shape=jax.ShapeDtypeStruct((m, n), out_dtype),
        scratch_shapes=[pltpu.VMEM((tm, tn), F32)] if nk > 1 else [],
        compiler_params=_params(("parallel", "parallel", "arbitrary")),
    )(*((a, b) if token is None else (a, b, token)))


def _row_spec(item, rows):
    if not isinstance(item, tuple):
        return item, pl.BlockSpec((rows, item.shape[1]), lambda i: (i, 0))
    if len(item) == 3:
        arr, w, c = item
        return arr, pl.BlockSpec((rows, w), lambda i: (i, c))
    arr, w, c, lead = item
    return arr, pl.BlockSpec((None, rows, w), lambda i: (lead, i, c))


def _whole_spec(item):
    if not isinstance(item, tuple):
        return item, pl.BlockSpec(item.shape, lambda i: (0,) * item.ndim)
    arr, w, c = item
    return arr, pl.BlockSpec((arr.shape[0], w), lambda i: (0, c))


def rowcall(body, tiled, whole, outs, accs=(), *, rows, total, name):
    rows = min(rows, total)
    assert total % rows == 0
    t_arr, t_spec = zip(*[_row_spec(t, rows) for t in tiled])
    w_arr, w_spec = zip(*[_whole_spec(w) for w in whole]) if whole else ((), ())
    nt, nw, no, na = len(t_arr), len(w_arr), len(outs), len(accs)

    def kern(*refs):
        vals = [r[...] for r in refs[:nt + nw]]
        res = body(*vals)
        if not isinstance(res, (tuple, list)):
            res = (res,)
        assert len(res) == no + na, (name, len(res), no, na)
        for r, v in zip(refs[nt + nw:nt + nw + no], res[:no]):
            r[...] = v.astype(r.dtype)
        if na:
            acc_refs = refs[nt + nw + no:]

            @pl.when(pl.program_id(0) == 0)
            def _():
                for r in acc_refs:
                    r[...] = jnp.zeros_like(r)

            for r, v in zip(acc_refs, res[no:]):
                r[...] += v

    out_shape = [jax.ShapeDtypeStruct((total, w), d) for w, d in outs] + [jax.ShapeDtypeStruct(s, F32) for s in accs]
    out_specs = [pl.BlockSpec((rows, w), lambda i: (i, 0)) for w, _ in outs] + \
                [pl.BlockSpec(s, lambda i: (0, 0)) for s in accs]
    res = pl.pallas_call(
        kern, name=name, grid=(total // rows,),
        in_specs=list(t_spec) + list(w_spec), out_specs=out_specs, out_shape=out_shape,
        compiler_params=_params(("arbitrary",) if na else ("parallel",)),
    )(*t_arr, *w_arr)
    return res


def _colsum(x):
    return jnp.sum(x, axis=0, keepdims=True)


def _gdn_chunk(q, k, v, z, gb, bb, state, gain):
    c = GDN_CHUNK
    nh = len(q)
    hs = range(nh)
    r64, c64 = _iota2((c, c), 0), _iota2((c, c), 1)
    incl = r64 >= c64
    strict = r64 > c64
    ltri = incl.astype(F32)
    utri = (r64 <= c64).astype(F32)
    eye = (r64 == c64).astype(F32)
    ones = jnp.ones((c, c), F32)
    pick = (_iota2((GDN_DH, c), 0) == _iota2((GDN_DH, c), 1)).astype(F32)
    last = (_iota2((c, GDN_DH), 0) == c - 1).astype(F32)

    qn = [q[h] * lax.rsqrt(jnp.sum(q[h] * q[h], axis=-1, keepdims=True) + EPS) * (GDN_DH ** -0.5) for h in hs]
    kn = [k[h] * lax.rsqrt(jnp.sum(k[h] * k[h], axis=-1, keepdims=True) + EPS) for h in hs]
    gc = [mdot(ltri, gb[h]) for h in hs]
    g64 = [mdot(gb[h], pick) for h in hs]
    gcol = [mdot(ltri, g64[h]) for h in hs]
    grow = [mdot(ones, g64[h] * utri) for h in hs]
    dec = [jnp.exp(jnp.where(incl, gcol[h] - grow[h], -1e30)) for h in hs]
    kb = [kn[h] * bb[h] for h in hs]
    vb = [v[h] * bb[h] for h in hs]
    kk = [bdot(kb[h], kn[h], 1, 1) for h in hs]
    p = [-jnp.where(strict, kk[h] * dec[h], 0.0) for h in hs]
    tinv = [eye + p[h] for h in hs]
    for _ in range(5):
        p = [mdot(p[h], p[h]) for h in hs]
        tinv = [tinv[h] + mdot(tinv[h], p[h]) for h in hs]
    egc = [jnp.exp(gc[h]) for h in hs]
    u = [mdot(tinv[h], vb[h]) for h in hs]
    w = [mdot(tinv[h], kb[h] * egc[h]) for h in hs]
    attn = [bdot(qn[h], kn[h], 1, 1) * dec[h] for h in hs]
    qd = [qn[h] * egc[h] for h in hs]
    gl = [jnp.sum(gc[h] * last, axis=0, keepdims=True) for h in hs]
    kt = [kn[h] * jnp.exp(gl[h] - gc[h]) for h in hs]
    ws = [bdot(w[h], state[h], 1, 0) for h in hs]
    qs = [bdot(qd[h], state[h], 1, 0) for h in hs]
    v_new = [u[h] - ws[h] for h in hs]
    av = [bdot(attn[h], v_new[h], 1, 0) for h in hs]
    kv = [bdot(kt[h], v_new[h], 0, 0) for h in hs]
    new_state = tuple(state[h] * jnp.exp(gl[h]) + kv[h] for h in hs)
    o = tuple(_rms(qs[h] + av[h], gain) * _silu(z[h]) for h in hs)
    return o, new_state


GDN_ROWS = 512
GDN_W = GDN_HEADS * GDN_DH


def gdn_fwd(cqkv, proj, gbb, gain):
    s = cqkv.shape[0]
    nb, cpb = s // GDN_ROWS, GDN_ROWS // GDN_CHUNK
    h4 = GDN_HEADS

    def kern(qkv_ref, z_ref, gb_ref, gain_ref, o_ref, st_ref, state):
        @pl.when(pl.program_id(0) == 0)
        def _():
            state[...] = jnp.zeros_like(state)

        gain_v = gain_ref[...]

        def step(ci, carry):
            sl = pl.ds(pl.multiple_of(ci * GDN_CHUNK, GDN_CHUNK), GDN_CHUNK)
            ins = []
            for h in range(h4):
                ln = lambda base, h=h: slice(base + h * GDN_DH, base + (h + 1) * GDN_DH)
                ins.append((qkv_ref[sl, ln(0)], qkv_ref[sl, ln(GDN_W)], qkv_ref[sl, ln(2 * GDN_W)], z_ref[sl, ln(0)],
                            gb_ref[sl, ln(0)], gb_ref[sl, ln(GDN_W)], state[h]))
            cols = [tuple(col) for col in zip(*ins)]
            o, new = _gdn_chunk(*cols[:7], gain_v)
            for h in range(h4):
                st_ref[h, ci] = ins[h][6]
                o_ref[sl, h * GDN_DH:(h + 1) * GDN_DH] = o[h]
                state[h] = new[h]
            return carry

        lax.fori_loop(0, cpb, step, 0)

    return pl.pallas_call(
        kern, name="gdn_fwd", grid=(nb,),
        in_specs=[pl.BlockSpec((GDN_ROWS, 3 * GDN_W), lambda i: (i, 0)),
                  pl.BlockSpec((GDN_ROWS, GDN_W), lambda i: (i, 6)),
                  pl.BlockSpec((GDN_ROWS, 2 * GDN_W), lambda i: (i, 0)),
                  pl.BlockSpec((1, GDN_DH), lambda i: (0, 0))],
        out_specs=[pl.BlockSpec((GDN_ROWS, GDN_W), lambda i: (i, 0)),
                   pl.BlockSpec((h4, cpb, GDN_DH, GDN_DH), lambda i: (0, i, 0, 0))],
        out_shape=[jax.ShapeDtypeStruct((s, GDN_W), F32),
                   jax.ShapeDtypeStruct((h4, s // GDN_CHUNK, GDN_DH, GDN_DH), F32)],
        scratch_shapes=[pltpu.VMEM((h4, GDN_DH, GDN_DH), F32)],
        compiler_params=_params(("arbitrary",)),
    )(cqkv, proj, gbb, gain)


def gdn_bwd(cqkv, proj, gbb, gain, states, d_mixed):
    s = cqkv.shape[0]
    nb, cpb = s // GDN_ROWS, GDN_ROWS // GDN_CHUNK
    h4 = GDN_HEADS

    def kern(qkv_ref, z_ref, gb_ref, gain_ref, st_ref, do_ref, dqkv_ref, dz_ref, dgb_ref, dgain_ref, dstate):
        @pl.when(pl.program_id(0) == 0)
        def _():
            dgain_ref[...] = jnp.zeros_like(dgain_ref)
            dstate[...] = jnp.zeros_like(dstate)

        gain_v = gain_ref[...]

        def step(t, carry):
            ci = cpb - 1 - t
            sl = pl.ds(pl.multiple_of(ci * GDN_CHUNK, GDN_CHUNK), GDN_CHUNK)
            prim, cot, dst_in = [], [], []
            for h in range(h4):
                ln = lambda base, h=h: slice(base + h * GDN_DH, base + (h + 1) * GDN_DH)
                prim.append((qkv_ref[sl, ln(0)], qkv_ref[sl, ln(GDN_W)], qkv_ref[sl, ln(2 * GDN_W)], z_ref[sl, ln(0)],
                             gb_ref[sl, ln(0)], gb_ref[sl, ln(GDN_W)], st_ref[h, ci]))
                cot.append(do_ref[sl, ln(0)])
                dst_in.append(dstate[h])
            cols = [tuple(col) for col in zip(*prim)]
            vjp = jax.vjp(_gdn_chunk, *cols, gain_v)[1]
            dq, dk, dv, dz, dg, db, dst, dgn = vjp((tuple(cot), tuple(dst_in)))
            for h in range(h4):
                ln = lambda base, h=h: slice(base + h * GDN_DH, base + (h + 1) * GDN_DH)
                dqkv_ref[sl, ln(0)] = dq[h]
                dqkv_ref[sl, ln(GDN_W)] = dk[h]
                dqkv_ref[sl, ln(2 * GDN_W)] = dv[h]
                dz_ref[sl, ln(0)] = dz[h]
                dgb_ref[sl, ln(0)] = dg[h]
                dgb_ref[sl, ln(GDN_W)] = db[h]
                dstate[h] = dst[h]
            dgain_ref[...] += dgn
            return carry

        lax.fori_loop(0, cpb, step, 0)

    def rev(width, cblock=0):
        return pl.BlockSpec((GDN_ROWS, width), lambda i: (nb - 1 - i, cblock))

    return pl.pallas_call(
        kern, name="gdn_bwd", grid=(nb,),
        in_specs=[rev(3 * GDN_W), rev(GDN_W, 6), rev(2 * GDN_W), pl.BlockSpec((1, GDN_DH), lambda i: (0, 0)),
                  pl.BlockSpec((h4, cpb, GDN_DH, GDN_DH), lambda i: (0, nb - 1 - i, 0, 0)), rev(GDN_W, 1)],
        out_specs=[rev(3 * GDN_W), rev(GDN_W), rev(2 * GDN_W), pl.BlockSpec((1, GDN_DH), lambda i: (0, 0))],
        out_shape=[jax.ShapeDtypeStruct((s, 3 * GDN_W), F32), jax.ShapeDtypeStruct((s, GDN_W), F32),
                   jax.ShapeDtypeStruct((s, 2 * GDN_W), F32), jax.ShapeDtypeStruct((1, GDN_DH), F32)],
        scratch_shapes=[pltpu.VMEM((h4, GDN_DH, GDN_DH), F32)],
        compiler_params=_params(("arbitrary",)),
    )(cqkv, proj, gbb, gain, states, d_mixed)


def _gdn_gates(small, prm):
    w = GDN_HEADS * GDN_DH
    lane, head = _iota2((128, w), 0), _iota2((128, w), 1) // GDN_DH
    sel_b = (lane == SMALL_B + head).astype(F32)
    sel_a = (lane == SMALL_A + head).astype(F32)
    prow = _iota2((8, 128), 0)
    a_log = jnp.sum(prm * (prow == 0).astype(F32), axis=0, keepdims=True)
    dt_b = jnp.sum(prm * (prow == 1).astype(F32), axis=0, keepdims=True)
    beta = _sigmoid(hdot(small, sel_b))
    g = hdot(-jnp.exp(a_log) * _softplus(small + dt_b), sel_a)
    return g, beta


CONV_ROWS = 1024
CONV_COLS = 128
CONV_BLOCK0 = 1536 // CONV_COLS


def _shift_down(prev8, cur, s):
    ext = jnp.concatenate([prev8, cur], axis=0)
    return pltpu.roll(ext, s, 0)[8:]


def _shift_up(cur, next8, s):
    n = cur.shape[0]
    ext = jnp.concatenate([cur, next8], axis=0)
    return pltpu.roll(ext, n + 8 - s, 0)[:n]


def _conv_pre(x_ref, w, ci, nchunk):
    r0 = pl.multiple_of(ci * CONV_ROWS, CONV_ROWS)
    cur = x_ref[pl.ds(r0, CONV_ROWS), :]
    prev = x_ref[pl.ds(pl.multiple_of(jnp.maximum(r0 - 8, 0), 8), 8), :]
    prev = jnp.where(ci > 0, prev, 0.0)
    shifted = [cur] + [_shift_down(prev, cur, s) for s in range(1, CONV_W)]
    pre = w[CONV_W - 1:CONV_W, :] * cur
    for s in range(1, CONV_W):
        pre = pre + w[CONV_W - 1 - s:CONV_W - s, :] * shifted[s]
    return r0, pre, shifted


def conv_fwd(proj, conv_w8):
    s = proj.shape[0]
    nchunk = s // CONV_ROWS
    ncol = 3 * GDN_HEADS * GDN_DH // CONV_COLS

    def kern(x_ref, w_ref, y_ref):
        w = w_ref[...]

        def step(ci, carry):
            r0, pre, _ = _conv_pre(x_ref, w, ci, nchunk)
            y_ref[pl.ds(r0, CONV_ROWS), :] = _silu(pre)
            return carry

        lax.fori_loop(0, nchunk, step, 0)

    return pl.pallas_call(
        kern, name="conv_fwd", grid=(ncol,),
        in_specs=[pl.BlockSpec((s, CONV_COLS), lambda j: (0, CONV_BLOCK0 + j)),
                  pl.BlockSpec((8, CONV_COLS), lambda j: (0, j))],
        out_specs=pl.BlockSpec((s, CONV_COLS), lambda j: (0, j)),
        out_shape=jax.ShapeDtypeStruct((s, ncol * CONV_COLS), F32),
        compiler_params=_params(("parallel",)),
    )(proj, conv_w8)


def conv_bwd(proj, conv_w8, dy):
    s = proj.shape[0]
    nchunk = s // CONV_ROWS
    per = 3 * GDN_HEADS * GDN_DH // CONV_COLS
    outs = []
    for part in range(1):
        def kern(x_ref, w_ref, dy_ref, dx_ref, dw_ref, dpre_ref):
            w = w_ref[...]
            rows8 = _iota2((8, CONV_COLS), 0)

            def step1(ci, dw):
                r0, pre, shifted = _conv_pre(x_ref, w, ci, nchunk)
                sg = _sigmoid(pre)
                dpre = dy_ref[pl.ds(r0, CONV_ROWS), :] * sg * (1.0 + pre * (1.0 - sg))
                dpre_ref[pl.ds(r0, CONV_ROWS), :] = dpre
                for sh in range(CONV_W):
                    dw = dw + jnp.where(rows8 == CONV_W - 1 - sh, _colsum(dpre * shifted[sh]), 0.0)
                return dw

            dw_ref[...] = lax.fori_loop(0, nchunk, step1, jnp.zeros((8, CONV_COLS), F32))

            def step2(ci, carry):
                r0 = pl.multiple_of(ci * CONV_ROWS, CONV_ROWS)
                cur = dpre_ref[pl.ds(r0, CONV_ROWS), :]
                nxt = dpre_ref[pl.ds(pl.multiple_of(jnp.minimum(r0 + CONV_ROWS, s - 8), 8), 8), :]
                nxt = jnp.where(ci < nchunk - 1, nxt, 0.0)
                dx = w[CONV_W - 1:CONV_W, :] * cur
                for sh in range(1, CONV_W):
                    dx = dx + w[CONV_W - 1 - sh:CONV_W - sh, :] * _shift_up(cur, nxt, sh)
                dx_ref[pl.ds(r0, CONV_ROWS), :] = dx
                return carry

            lax.fori_loop(0, nchunk, step2, 0)

        outs.append(pl.pallas_call(
            kern, name=f"conv_bwd{part}", grid=(per,),
            in_specs=[pl.BlockSpec((s, CONV_COLS), lambda j, part=part: (0, CONV_BLOCK0 + part * per + j)),
                      pl.BlockSpec((8, CONV_COLS), lambda j, part=part: (0, part * per + j)),
                      pl.BlockSpec((s, CONV_COLS), lambda j: (0, j))],
            out_specs=[pl.BlockSpec((s, CONV_COLS), lambda j: (0, j)),
                       pl.BlockSpec((8, CONV_COLS), lambda j: (0, j))],
            out_shape=[jax.ShapeDtypeStruct((s, per * CONV_COLS), F32),
                       jax.ShapeDtypeStruct((8, per * CONV_COLS), F32)],
            scratch_shapes=[pltpu.VMEM((s, CONV_COLS), F32)],
            compiler_params=_params(("parallel",)),
        )(proj, conv_w8, dy))
    dx = jnp.concatenate([o[0] for o in outs], axis=1)
    dw = jnp.concatenate([o[1] for o in outs], axis=1)
    return dx, dw


FOXF_ROWS = 512
SMALL_BLOCK128 = 3584 // 128


def _log_sigmoid(x):
    return jnp.minimum(x, 0.0) - jnp.log(1.0 + jnp.exp(-jnp.abs(x)))


def fox_f_fwd(proj, bias_row):
    s = proj.shape[0]
    n = s // FOXF_ROWS

    def kern(x_ref, b_ref, f_ref, carry):
        @pl.when(pl.program_id(0) == 0)
        def _():
            carry[...] = jnp.zeros_like(carry)

        heads = _iota2((FOXF_ROWS, 128), 1) < FOX_HEADS
        lf = jnp.where(heads, _log_sigmoid(x_ref[...] + b_ref[...]), 0.0)
        ltri = (_iota2((FOXF_ROWS, FOXF_ROWS), 0) >= _iota2((FOXF_ROWS, FOXF_ROWS), 1)).astype(F32)
        c = hdot(ltri, lf) + carry[...]
        f_ref[...] = c
        carry[...] = c[FOXF_ROWS - 1:FOXF_ROWS, :]

    return pl.pallas_call(
        kern, name="fox_f_fwd", grid=(n,),
        in_specs=[pl.BlockSpec((FOXF_ROWS, 128), lambda i: (i, SMALL_BLOCK128)),
                  pl.BlockSpec((1, 128), lambda i: (0, 0))],
        out_specs=pl.BlockSpec((FOXF_ROWS, 128), lambda i: (i, 0)),
        out_shape=jax.ShapeDtypeStruct((s, 128), F32),
        scratch_shapes=[pltpu.VMEM((1, 128), F32)],
        compiler_params=_params(("arbitrary",)),
    )(proj, bias_row)


def fox_f_bwd(proj, bias_row, d_f):
    s = proj.shape[0]
    n = s // FOXF_ROWS

    def kern(x_ref, b_ref, df_ref, dx_ref, db_ref, carry):
        @pl.when(pl.program_id(0) == 0)
        def _():
            carry[...] = jnp.zeros_like(carry)
            db_ref[...] = jnp.zeros_like(db_ref)

        heads = _iota2((FOXF_ROWS, 128), 1) < FOX_HEADS
        utri = (_iota2((FOXF_ROWS, FOXF_ROWS), 0) <= _iota2((FOXF_ROWS, FOXF_ROWS), 1)).astype(F32)
        rc = hdot(utri, df_ref[...]) + carry[...]
        carry[...] = rc[0:1, :]
        dx = jnp.where(heads, rc * _sigmoid(-(x_ref[...] + b_ref[...])), 0.0)
        dx_ref[...] = dx
        db_ref[...] += _colsum(dx)

    return pl.pallas_call(
        kern, name="fox_f_bwd", grid=(n,),
        in_specs=[pl.BlockSpec((FOXF_ROWS, 128), lambda i: (n - 1 - i, SMALL_BLOCK128)),
                  pl.BlockSpec((1, 128), lambda i: (0, 0)),
                  pl.BlockSpec((FOXF_ROWS, 128), lambda i: (n - 1 - i, 0))],
        out_specs=[pl.BlockSpec((FOXF_ROWS, 128), lambda i: (n - 1 - i, 0)),
                   pl.BlockSpec((1, 128), lambda i: (0, 0))],
        out_shape=[jax.ShapeDtypeStruct((s, 128), F32), jax.ShapeDtypeStruct((1, 128), F32)],
        scratch_shapes=[pltpu.VMEM((1, 128), F32)],
        compiler_params=_params(("arbitrary",)),
    )(proj, bias_row, d_f)


FOX_T = 512
FOX_SCALE = FOX_DH ** -0.5
FOX_PAIRS = FOX_HEADS // 2
NEG = -1e30
_NT = (((1,), (1,)), ((), ()))


def _split3(x):
    def bf(v):
        return lax.reduce_precision(v, exponent_bits=8, mantissa_bits=7)

    hi = bf(x)
    mid = bf(x - hi)
    lo = bf(x - hi - mid)
    return jnp.stack([hi, mid, lo], axis=-1)


def _fox_extras(s, first, second):
    def part(v):
        if v is None:
            return jnp.zeros((s, FOX_HEADS, 3), F32)
        if isinstance(v, float):
            return jnp.full((s, FOX_HEADS, 3), v, F32)
        return _split3(v)

    cols = jnp.concatenate([part(first), part(second)], axis=-1)
    cols = _pad_to(cols, (s, FOX_HEADS, FOX_DH)).reshape(s, FOX_PAIRS, 2, FOX_DH)
    cols = cols[:, :, ::-1, :].reshape(s, FOX_PAIRS, 2 * FOX_DH)
    return cols.transpose(1, 0, 2).astype(BF)


def _head_masks(rows):
    lane = _iota2((rows, 2 * FOX_DH), 1)
    return lane < FOX_DH, lane >= FOX_DH


def _extra_lane(e, slot):
    return (FOX_DH if e == 0 else 0) + slot


def fox_fwd(qkv, xq, xk, xv):
    s = qkv.shape[0]
    t = min(FOX_T, s)
    n = s // t

    def kern(q_ref, k_ref, v_ref, xq_ref, xk_ref, xv_ref, o_ref, lse_ref):
        i = pl.program_id(1)
        masks = _head_masks(t)
        q_pair, x_pair = q_ref[...] * FOX_SCALE, xq_ref[...]
        q_ops = [jnp.where(mk, q_pair, x_pair) for mk in masks]

        def step(j, carry, masked):
            sl = pl.ds(pl.multiple_of(j * t, t), t)
            k_pair, xk_pair, v_pair, xv_pair = k_ref[sl, :], xk_ref[sl, :], v_ref[sl, :], xv_ref[sl, :]
            k_ops = [jnp.where(mk, k_pair, xk_pair) for mk in masks]
            v_ops = [jnp.where(mk, v_pair, xv_pair) for mk in masks]
            sc = [lax.dot_general(q_ops[e], k_ops[e], _NT, preferred_element_type=F32) for e in range(2)]
            if masked:
                keep = _iota2((t, t), 0) >= _iota2((t, t), 1)
                sc = [jnp.where(keep, x, NEG) for x in sc]
            m_new = [jnp.maximum(carry[e][0], jnp.max(sc[e], axis=1, keepdims=True)) for e in range(2)]
            p = [jnp.exp(sc[e] - m_new[e]).astype(BF) for e in range(2)]
            pv = [jnp.dot(p[e], v_ops[e], preferred_element_type=F32) for e in range(2)]
            return tuple((m_new[e], jnp.exp(carry[e][0] - m_new[e]) * carry[e][1] + pv[e]) for e in range(2))

        init = tuple((jnp.full((t, 1), NEG, F32), jnp.zeros((t, 2 * FOX_DH), F32)) for _ in range(2))
        carry = lax.fori_loop(0, i, lambda j, c: step(j, c, False), init)
        carry = step(i, carry, True)
        lane = _iota2((t, 2 * FOX_DH), 1)
        outs, lses = [], []
        for e in range(2):
            m, acc = carry[e]
            l = jnp.sum(jnp.where(lane == _extra_lane(e, 0), acc, 0.0), axis=1, keepdims=True)
            outs.append(acc / l)
            lses.append(m + jnp.log(l))
        o_ref[...] = jnp.where(masks[0], outs[0], outs[1])
        lse_ref[...] = jnp.where(lane == 0, lses[0], jnp.where(lane == 1, lses[1], 0.0))

    pr = FOX_PAIRS
    return pl.pallas_call(
        kern, name="fox_fwd", grid=(pr, n),
        in_specs=[pl.BlockSpec((t, 128), lambda p, i: (i, p)),
                  pl.BlockSpec((s, 128), lambda p, i: (0, pr + p)),
                  pl.BlockSpec((s, 128), lambda p, i: (0, 2 * pr + p)),
                  pl.BlockSpec((None, t, 128), lambda p, i: (p, i, 0)),
                  pl.BlockSpec((None, s, 128), lambda p, i: (p, 0, 0)),
                  pl.BlockSpec((None, s, 128), lambda p, i: (p, 0, 0))],
        out_specs=[pl.BlockSpec((t, 128), lambda p, i: (i, p)),
                   pl.BlockSpec((None, t, 128), lambda p, i: (p, i, 0))],
        out_shape=[jax.ShapeDtypeStruct((s, FOX_HEADS * FOX_DH), F32), jax.ShapeDtypeStruct((pr, s, 128), F32)],
        compiler_params=_params(("parallel", "parallel")),
    )(qkv, qkv, qkv, xq, xk, xv)


def fox_bwd(qkv, d_o, xk, xv, xqb, xdo):
    s = qkv.shape[0]
    t = min(FOX_T, s)
    n = s // t
    w = 2 * FOX_DH

    def both(blocks, slot):
        lane = _iota2(blocks[0].shape, 1)
        own = jnp.where(lane < FOX_DH, blocks[0], blocks[1])
        sums = [jnp.sum(jnp.where(lane == _extra_lane(e, slot), blocks[e], 0.0), axis=1, keepdims=True)
                for e in range(2)]
        return own, jnp.where(lane == 0, sums[0], jnp.where(lane == 1, sums[1], 0.0))

    def kern(k_ref, v_ref, xk_ref, xv_ref, q_ref, do_ref, xq_ref, xd_ref,
             dq_ref, dk_ref, dv_ref, sq_ref, sk_ref, dq_acc):
        j = pl.program_id(1)

        @pl.when(j == 0)
        def _():
            dq_acc[...] = jnp.zeros_like(dq_acc)

        masks = _head_masks(t)
        k_ops = [jnp.where(mk, k_ref[...], xk_ref[...]) for mk in masks]
        v_ops = [jnp.where(mk, v_ref[...], xv_ref[...]) for mk in masks]

        def step(i, carry, masked):
            dk, dv = carry
            sl = pl.ds(pl.multiple_of(i * t, t), t)
            q_pair, xq_pair, do_pair, xd_pair = q_ref[sl, :] * FOX_SCALE, xq_ref[sl, :], do_ref[sl, :], xd_ref[sl, :]
            q_ops = [jnp.where(mk, q_pair, xq_pair) for mk in masks]
            do_ops = [jnp.where(mk, do_pair, xd_pair) for mk in masks]
            do_own = [jnp.where(mk, do_pair, 0).astype(BF) for mk in masks]
            st = [lax.dot_general(k_ops[e], q_ops[e], _NT, preferred_element_type=F32) for e in range(2)]
            dp = [lax.dot_general(v_ops[e], do_ops[e], _NT, preferred_element_type=F32) for e in range(2)]
            if masked:
                keep = _iota2((t, t), 0) <= _iota2((t, t), 1)
                st = [jnp.where(keep, x, NEG) for x in st]
            pt = [jnp.exp(x) for x in st]
            dsb = [(pt[e] * dp[e]).astype(BF) for e in range(2)]
            dv = dv + sum(jnp.dot(pt[e].astype(BF), do_own[e], preferred_element_type=F32) for e in range(2))
            dk = tuple(dk[e] + jnp.dot(dsb[e], q_ops[e], preferred_element_type=F32) for e in range(2))
            for e in range(2):
                dq_acc[sl, e * w:(e + 1) * w] += lax.dot_general(dsb[e], k_ops[e], (((0,), (0,)), ((), ())),
                                                                 preferred_element_type=F32)
            return dk, dv

        init = ((jnp.zeros((t, w), F32), jnp.zeros((t, w), F32)), jnp.zeros((t, w), F32))
        carry = step(j, init, True)
        dk, dv = lax.fori_loop(j + 1, n, lambda i, c: step(i, c, False), carry)
        dk_ref[...], sk_ref[...] = both(dk, 3)
        dv_ref[...] = dv

        @pl.when(j == n - 1)
        def _():
            def out(r, carry):
                sl = pl.ds(pl.multiple_of(r * t, t), t)
                own, sums = both([dq_acc[sl, e * w:(e + 1) * w] for e in range(2)], 0)
                dq_ref[sl, :] = own * FOX_SCALE
                sq_ref[sl, :] = sums
                return carry

            lax.fori_loop(0, n, out, 0)

    pr = FOX_PAIRS
    flat = jax.ShapeDtypeStruct((s, FOX_HEADS * FOX_DH), F32)
    tile = pl.BlockSpec((t, 128), lambda p, j: (j, p))
    whole = pl.BlockSpec((s, 128), lambda p, j: (0, p))
    return pl.pallas_call(
        kern, name="fox_bwd", grid=(pr, n),
        in_specs=[pl.BlockSpec((t, 128), lambda p, j: (j, pr + p)),
                  pl.BlockSpec((t, 128), lambda p, j: (j, 2 * pr + p)),
                  pl.BlockSpec((None, t, 128), lambda p, j: (p, j, 0)),
                  pl.BlockSpec((None, t, 128), lambda p, j: (p, j, 0)),
                  whole, whole,
                  pl.BlockSpec((None, s, 128), lambda p, j: (p, 0, 0)),
                  pl.BlockSpec((None, s, 128), lambda p, j: (p, 0, 0))],
        out_specs=[whole, tile, tile, whole, tile],
        out_shape=[flat] * 5,
        scratch_shapes=[pltpu.VMEM((s, 2 * w), F32)],
        compiler_params=_params(("parallel", "arbitrary")),
    )(qkv, qkv, xk, xv, qkv, d_o, xqb, xdo)


def _xattn_head(q, k, v):
    sc = bdot(q, k, 1, 1) * (MEM_DH ** -0.5)
    e = jnp.exp(sc - lax.stop_gradient(jnp.max(sc, axis=-1, keepdims=True)))
    p = e / jnp.sum(e, axis=-1, keepdims=True)
    return bdot(p, v, 1, 0)


def xattn_fwd(q, kv):
    s = q.shape[0]
    hh = MEM_HEADS

    def body(*vals):
        qs, ks, vs = vals[:hh], vals[hh:2 * hh], vals[2 * hh:]
        return jnp.concatenate([_xattn_head(qs[a], ks[a], vs[a]) for a in range(hh)], axis=1)

    return rowcall(body, [(q, MEM_DH, a) for a in range(hh)],
                   [(kv, MEM_DH, a) for a in range(2 * hh)],
                   [(hh * MEM_DH, BF)], rows=512, total=s, name="xattn_fwd")[0]


def xattn_bwd(q, kv, d_o):
    s = q.shape[0]
    hh = MEM_HEADS

    def body(*vals):
        qs, dos = vals[:hh], vals[hh:2 * hh]
        ks, vs = vals[2 * hh:3 * hh], vals[3 * hh:]
        dqs, dks, dvs = [], [], []
        for a in range(hh):
            _, vjp = jax.vjp(_xattn_head, qs[a], ks[a], vs[a])
            dq, dk, dv = vjp(dos[a])
            dqs.append(dq)
            dks.append(dk)
            dvs.append(dv)
        return jnp.concatenate(dqs, axis=1), jnp.concatenate(dks + dvs, axis=1)

    return rowcall(body, [(q, MEM_DH, a) for a in range(hh)] + [(d_o, MEM_DH, a) for a in range(hh)],
                   [(kv, MEM_DH, a) for a in range(2 * hh)],
                   [(hh * MEM_DH, BF)], [kv.shape], rows=512, total=s, name="xattn_bwd")


def _slab(ref, axis, start, size):
    if axis is None:
        return ref
    if axis == "lead":
        return ref.at[start]
    idx = pl.ds(pl.multiple_of(start, 128 if axis == 1 else 16), size)
    return ref.at[idx] if axis == 0 else ref.at[:, idx]


def exchange(inputs, outputs, transfers, name):
    ni, no, nt = len(inputs), len(outputs), len(transfers)
    npeer = N_DEV - 1

    def body(*refs):
        ins, outs = refs[:ni], refs[ni:ni + no]
        send, recv, loc = refs[ni + no:]
        x, y, c = lax.axis_index("x"), lax.axis_index("y"), lax.axis_index("c")
        me = 4 * x + 2 * y + c

        def peer(p):
            px = 1 - x if p & 4 else x
            py = 1 - y if p & 2 else y
            pc = 1 - c if p & 1 else c
            return (px, py, pc), 4 * px + 2 * py + pc

        def view(ref, spec, who):
            axis, off, stride, size = spec
            return _slab(ref, axis, off + who * stride, size)

        local, remote = [], []
        for w, (ii, src, oi, dst) in enumerate(transfers):
            cp = pltpu.make_async_copy(view(ins[ii], src, me), view(outs[oi], dst, me), loc.at[w])
            cp.start()
            local.append(cp)
        for p in range(1, N_DEV):
            dev, idx = peer(p)
            for w, (ii, src, oi, dst) in enumerate(transfers):
                k = w * npeer + p - 1
                out_cp = pltpu.make_async_remote_copy(
                    src_ref=view(ins[ii], src, idx), dst_ref=view(outs[oi], dst, me), send_sem=send.at[k],
                    recv_sem=recv.at[k], device_id=dev, device_id_type=MESH)
                out_cp.start()
                in_cp = pltpu.make_async_remote_copy(
                    src_ref=view(ins[ii], src, idx), dst_ref=view(outs[oi], dst, idx), send_sem=send.at[k],
                    recv_sem=recv.at[k], device_id=dev, device_id_type=MESH)
                remote.append((out_cp, in_cp))
        for out_cp, in_cp in remote:
            in_cp.wait_recv()
            out_cp.wait_send()
        for cp in local:
            cp.wait()

    hbm = pl.BlockSpec(memory_space=pl.ANY)
    return pl.pallas_call(
        body, name=name, in_specs=[hbm] * ni, out_specs=[hbm] * no, out_shape=list(outputs),
        scratch_shapes=[pltpu.SemaphoreType.DMA((nt * npeer,)), pltpu.SemaphoreType.DMA((nt * npeer,)),
                        pltpu.SemaphoreType.DMA((nt,))],
        compiler_params=pltpu.CompilerParams(has_side_effects=True),
    )(*inputs)


def _peer(p):
    x, y, c = lax.axis_index("x"), lax.axis_index("y"), lax.axis_index("c")
    px = 1 - x if p & 4 else x
    py = 1 - y if p & 2 else y
    pc = 1 - c if p & 1 else c
    return (px, py, pc), 4 * px + 2 * py + pc


def _view(ref, spec, who):
    axis, off, stride, size = spec
    return _slab(ref, axis, off + who * stride, size)


def place_own(inputs, outputs, transfers):
    me = 4 * lax.axis_index("x") + 2 * lax.axis_index("y") + lax.axis_index("c")
    lands = [lax.empty(o.shape, o.dtype) for o in outputs]
    for ii, src, oi, dst in transfers:
        axis, off, stride, size = src
        own = inputs[ii] if axis is None else lax.dynamic_slice_in_dim(inputs[ii], off + me * stride, size, axis)
        axis, off, stride, size = dst
        if axis == "lead":
            lands[oi] = lax.dynamic_update_slice_in_dim(lands[oi], own[None], me, 0)
        else:
            lands[oi] = lax.dynamic_update_slice_in_dim(lands[oi], own, off + me * stride, axis)
    return lands


_HBM = pl.BlockSpec(memory_space=pltpu.HBM)
_SEM = pl.BlockSpec(memory_space=pltpu.SEMAPHORE)
_EFFECT = pltpu.SideEffectType.DATAFLOW_SIDE_EFFECTING


def _remote_copies(ins, lands, transfers, send, recv):
    npeer = N_DEV - 1
    me = 4 * lax.axis_index("x") + 2 * lax.axis_index("y") + lax.axis_index("c")
    pairs = []
    for p in range(1, N_DEV):
        dev, idx = _peer(p)
        for w, (ii, src, oi, dst) in enumerate(transfers):
            k = w * npeer + p - 1
            common = dict(src_ref=_view(ins[ii], src, idx), send_sem=send.at[k], recv_sem=recv.at[k],
                          device_id=dev, device_id_type=MESH)
            pairs.append((pltpu.make_async_remote_copy(dst_ref=_view(lands[oi], dst, me), **common),
                          pltpu.make_async_remote_copy(dst_ref=_view(lands[oi], dst, idx), **common)))
    return pairs


def exchange_start(inputs, lands, transfers, after, name):
    ni, nl, nsem = len(inputs), len(lands), len(transfers) * (N_DEV - 1)

    def body(*refs):
        ins, lnd = refs[:ni], refs[ni:ni + nl]
        send, recv = refs[ni + nl + 1], refs[ni + nl + 2]
        token = refs[-1]
        for out_cp, _ in _remote_copies(ins, lnd, transfers, send, recv):
            out_cp.start()
        token[...] = jnp.zeros_like(token)

    args = [pltpu.with_memory_space_constraint(a, pltpu.HBM) for a in list(inputs) + list(lands)]
    res = pl.pallas_call(
        body, name=name,
        out_shape=(pltpu.SemaphoreType.DMA((nsem,)), pltpu.SemaphoreType.DMA((nsem,)),
                   *[pltpu.HBM(a.shape, a.dtype) for a in args], jax.ShapeDtypeStruct((8, 128), F32)),
        in_specs=[_HBM] * (ni + nl) + [pl.BlockSpec(memory_space=pl.ANY)],
        out_specs=(_SEM, _SEM, *[_HBM] * (ni + nl), pl.BlockSpec(memory_space=pltpu.VMEM)),
        input_output_aliases={k: k + 2 for k in range(ni + nl)},
        compiler_params=pltpu.CompilerParams(has_side_effects=_EFFECT),
    )(*args, after)
    return res[0], res[1], list(res[2:2 + ni]), list(res[2 + ni:2 + ni + nl]), res[-1]


def exchange_wait(send, recv, inputs, lands, after, transfers, name):
    ni, nl = len(inputs), len(lands)

    def body(*refs):
        ins, lnd = refs[:ni], refs[ni:ni + nl]
        send_r, recv_r = refs[ni + nl], refs[ni + nl + 1]
        for out_cp, in_cp in _remote_copies(ins, lnd, transfers, send_r, recv_r):
            out_cp.wait_send()
            in_cp.wait_recv()

    res = pl.pallas_call(
        body, name=name,
        out_shape=tuple(pltpu.HBM(a.shape, a.dtype) for a in list(inputs) + list(lands)),
        in_specs=[_HBM] * (ni + nl) + [_SEM, _SEM, pl.BlockSpec(memory_space=pl.ANY)],
        out_specs=tuple([_HBM] * (ni + nl)),
        input_output_aliases={k: k for k in range(ni + nl)},
        compiler_params=pltpu.CompilerParams(has_side_effects=_EFFECT),
    )(*inputs, *lands, send, recv, after)
    return list(res[ni:])


def adamw(w, m, v, contribs, name):
    r, c = w.shape
    nc = len(contribs)
    rows = next((r // d for d in (4, 2) if r % d == 0 and (r // d) % 16 == 0), r)
    c1, c2 = 1.0 - ADAM_B1 ** ADAM_STEP, 1.0 - ADAM_B2 ** ADAM_STEP

    def body(wv, mv, vv, *gs):
        g = gs[0].astype(F32)
        for extra in gs[1:]:
            g = g + extra.astype(F32)
        g = g[:, :c]
        m_new = ADAM_B1 * mv + (1.0 - ADAM_B1) * g
        v_new = ADAM_B2 * vv + (1.0 - ADAM_B2) * (g * g)
        delta = -ADAM_LR * ((m_new / c1) / (jnp.sqrt(v_new / c2) + ADAM_EPS) + ADAM_WD * wv)
        return g, delta, m_new, v_new

    assert nc >= 1
    return rowcall(body, [w, m, v] + list(contribs), [], [(c, F32)] * 4, rows=rows, total=r, name=name)


WEIGHTS = ['ffn1_pre_norm', 'ffn1_w_gate', 'ffn1_w_up', 'ffn1_w_down', 'ffn1_post_norm', 'mix_pre_norm', 'w_in',
           'fox_f_bias', 'gdn_conv_w', 'gdn_a_log', 'gdn_dt_bias', 'gdn_out_norm', 'w_out', 'mix_post_norm',
           'mem_pre_norm', 'mem_kv_norm', 'mem_w_q', 'mem_w_kv', 'mem_w_o', 'mem_post_norm', 'ffn2_pre_norm',
           'ffn2_w_gate', 'ffn2_w_up', 'ffn2_w_down', 'ffn2_post_norm']
GAINS = ['ffn1_pre_norm', 'ffn1_post_norm', 'mix_pre_norm', 'mix_post_norm', 'mem_pre_norm', 'mem_kv_norm',
         'mem_post_norm', 'ffn2_pre_norm', 'ffn2_post_norm']
BIG = ['ffn1_w_gate', 'ffn1_w_up', 'ffn1_w_down', 'w_in', 'w_out', 'mem_w_q', 'mem_w_kv', 'mem_w_o',
       'ffn2_w_gate', 'ffn2_w_up', 'ffn2_w_down']
PACK_ROWS = 24
ROW_MISC = len(GAINS)
ROW_CONV = ROW_MISC + 1
COL_FBIAS, COL_ALOG, COL_DTB, COL_ONORM, COL_LOSS = 0, 8, 12, 128, 256
CONV_CH = 3 * GDN_HEADS * GDN_DH


def _pad_to(a, shape):
    return jnp.pad(a, [(0, t - s) for s, t in zip(a.shape, shape)])


def _pack(get, conv=None, loss=None):
    rows = [get(nm) for nm in GAINS]
    misc = jnp.concatenate([get('fox_f_bias'), get('gdn_a_log'), get('gdn_dt_bias'),
                            jnp.zeros((1, COL_ONORM - COL_DTB - 4), F32), get('gdn_out_norm'),
                            jnp.zeros((1, 1), F32) if loss is None else loss.reshape(1, 1)], axis=1)
    rows.append(_pad_to(misc, (1, D_MODEL)))
    rows.append(jnp.zeros((6, D_MODEL), F32) if conv is None else conv.reshape(6, D_MODEL))
    return _pad_to(jnp.concatenate(rows, axis=0), (PACK_ROWS, D_MODEL))


def _unpack(p):
    out = {nm: p[i:i + 1] for i, nm in enumerate(GAINS)}
    misc = p[ROW_MISC:ROW_MISC + 1]
    out['fox_f_bias'] = misc[:, COL_FBIAS:COL_FBIAS + FOX_HEADS]
    out['gdn_a_log'] = misc[:, COL_ALOG:COL_ALOG + GDN_HEADS]
    out['gdn_dt_bias'] = misc[:, COL_DTB:COL_DTB + GDN_HEADS]
    out['gdn_out_norm'] = misc[:, COL_ONORM:COL_ONORM + GDN_DH]
    return out


def _ffn_fwd(h, pre, wgu, wd, tag):
    s = h.shape[0]
    u, = rowcall(_rms, [h], [pre], [(D_MODEL, BF)], rows=512, total=s, name=tag + "_pre")
    gu = mm(u, wgu, out_dtype=BF, name=tag + "_gate_up")
    act, = rowcall(lambda a, b: _silu(a.astype(F32)) * b.astype(F32), [(gu, D_FF_PAD, 0), (gu, D_FF_PAD, 1)], [],
                   [(D_FF_PAD, BF)],
                   rows=256, total=s, name=tag + "_act")
    if callable(wd):
        wd = wd(act)
    f = mm(act, wd, name=tag + "_down")
    return u, gu, act, f


def _half_rms(a, g):
    return 0.5 * _rms(a, g)


def _ffn_bwd(dh_out, h, pre, post, wgu, wd, saved, tag, on_dwd=None, on_dwgu=None):
    u, gu, act, f = saved
    s = h.shape[0]

    def b_post(dh, fv, pg):
        return jax.vjp(_half_rms, fv, pg)[1](dh)

    df, dpost = rowcall(b_post, [dh_out, f], [post], [(D_MODEL, BF)], [(1, D_MODEL)], rows=512, total=s,
                        name=tag + "_bwd_post")
    dwd = mm(act, df, ta=True, out_dtype=BF, name=tag + "_bwd_dwd")
    dact = mm(df, wd, tb=True, out_dtype=BF, name=tag + "_bwd_dact", token=on_dwd(dwd) if on_dwd else None)

    def b_act(a, b, da):
        dg, du = jax.vjp(lambda g_, u_: _silu(g_) * u_, a.astype(F32), b.astype(F32))[1](da.astype(F32))
        return jnp.concatenate([dg, du], axis=1)

    dgu, = rowcall(b_act, [(gu, D_FF_PAD, 0), (gu, D_FF_PAD, 1), dact], [], [(2 * D_FF_PAD, BF)], rows=256, total=s,
                   name=tag + "_bwd_act")
    dwgu = mm(u, dgu, ta=True, out_dtype=BF, name=tag + "_bwd_dwgu")
    du = mm(dgu, wgu, tb=True, name=tag + "_bwd_du", token=on_dwgu(dwgu) if on_dwgu else None)

    def b_pre(dh, duv, hv, pg):
        dx, dpre = jax.vjp(_rms, hv, pg)[1](duv)
        return dh + dx, dpre

    dh, dpre = rowcall(b_pre, [dh_out, du, h], [pre], [(D_MODEL, F32)], [(1, D_MODEL)], rows=512, total=s,
                       name=tag + "_bwd_pre")
    return dh, dwgu, dwd, dpre, dpost


def _residual_rms(h, a, g):
    return h + _rms(a, g)


def _bwd_residual(dh, a, g):
    return jax.vjp(_rms, a, g)[1](dh)


def _step(a):
    x, mem = a['x'][0], a['mem'][0]
    s = x.shape[0]
    me = 4 * lax.axis_index("x") + 2 * lax.axis_index("y") + lax.axis_index("c")
    w2 = {nm: a[nm][0] for nm in WEIGHTS}
    m2 = {nm: a['m_' + nm][0] for nm in WEIGHTS}
    v2 = {nm: a['v_' + nm][0] for nm in WEIGHTS}
    small = {nm: w2[nm][None] for nm in WEIGHTS if nm not in BIG and nm != 'gdn_conv_w'}

    def ff_cols(w):
        return _pad_to(w, (D_MODEL, FF_SHARD_PAD)).astype(BF)

    def ff_rows(w):
        return _pad_to(w, (FF_SHARD_PAD, D_MODEL)).astype(BF)

    whole = (None, 0, 0, 0)
    conv_pad = 256
    g_in = [ff_cols(w2['ffn1_w_gate']), ff_cols(w2['ffn1_w_up']), ff_rows(w2['ffn1_w_down']),
            ff_cols(w2['ffn2_w_gate']), ff_cols(w2['ffn2_w_up']), ff_rows(w2['ffn2_w_down']),
            _pad_to(w2['w_in'], (D_MODEL, IN_SHARD_PAD)).astype(BF), w2['w_out'].astype(BF),
            w2['mem_w_q'].astype(BF), w2['mem_w_kv'].astype(BF), w2['mem_w_o'].astype(BF),
            _pad_to(w2['gdn_conv_w'], (8, conv_pad))]
    g_out = [jax.ShapeDtypeStruct((D_MODEL, 2 * D_FF_PAD), BF), jax.ShapeDtypeStruct((D_FF_PAD, D_MODEL), BF),
             jax.ShapeDtypeStruct((D_MODEL, 2 * D_FF_PAD), BF), jax.ShapeDtypeStruct((D_FF_PAD, D_MODEL), BF),
             jax.ShapeDtypeStruct((D_MODEL, N_DEV * IN_SHARD_PAD), BF), jax.ShapeDtypeStruct((D_MODEL, D_MODEL), BF),
             jax.ShapeDtypeStruct((D_MODEL, D_MODEL), BF), jax.ShapeDtypeStruct((D_MODEL, 2 * D_MODEL), BF),
             jax.ShapeDtypeStruct((D_MODEL, D_MODEL), BF), jax.ShapeDtypeStruct((8, N_DEV * conv_pad), F32)]
    sp_, dm = FF_SHARD_PAD, D_MODEL // N_DEV
    g_tr = [(0, whole, 0, (1, 0, sp_, sp_)), (1, whole, 0, (1, D_FF_PAD, sp_, sp_)), (2, whole, 1, (0, 0, sp_, sp_)),
            (3, whole, 2, (1, 0, sp_, sp_)), (4, whole, 2, (1, D_FF_PAD, sp_, sp_)), (5, whole, 3, (0, 0, sp_, sp_)),
            (6, whole, 4, (1, 0, IN_SHARD_PAD, IN_SHARD_PAD)), (7, whole, 5, (0, 0, dm, dm)),
            (8, whole, 6, (0, 0, dm, dm)), (9, whole, 7, (1, 0, 2 * dm, 2 * dm)), (10, whole, 8, (0, 0, dm, dm)),
            (11, whole, 9, (1, 0, conv_pad, conv_pad))]
    def pick(idx):
        ins = sorted({g_tr[k][0] for k in idx})
        outs = sorted({g_tr[k][2] for k in idx})
        tr = [(ins.index(g_tr[k][0]), g_tr[k][1], outs.index(g_tr[k][2]), g_tr[k][3]) for k in idx]
        return [g_in[i] for i in ins], [g_out[o] for o in outs], tr

    gu_in, gu_out, gu_tr = pick([0, 1])
    wgu1, = exchange(gu_in, gu_out, gu_tr, "gather_gate_up")
    stages, after = [], wgu1
    for nm, idx in (("down", [2]), ("mix", [6, 7, 11]), ("late", [8, 9, 10, 3, 4, 5])):
        st_in, st_out, st_tr = pick(idx)
        st = exchange_start(st_in, place_own(st_in, st_out, st_tr), st_tr, after, "gather_%s_start" % nm)
        stages.append((st, st_tr, "gather_%s_wait" % nm))
        after = st[4]
    g_token = after

    def gather_wait(k, after_):
        (send_, recv_, src_, land_, _), tr_, nm_ = stages[k]
        return exchange_wait(send_, recv_, src_, land_, after_, tr_, nm_)

    bias_row = _pad_to(small['fox_f_bias'], (1, 128))
    gate_prm = _pad_to(jnp.concatenate([_pad_to(small['gdn_a_log'], (1, 128 - SMALL_A)),
                                        _pad_to(small['gdn_dt_bias'], (1, 128 - SMALL_A))], axis=0),
                       (8, 128 - SMALL_A))
    gate_prm = jnp.pad(gate_prm, ((0, 0), (SMALL_A, 0)))
    onorm = small['gdn_out_norm']

    late = {}

    def wd1_when(act):
        late['wd1'], = gather_wait(0, act)
        return late['wd1']

    sv1 = _ffn_fwd(x, small['ffn1_pre_norm'] + g_token[0, 0], wgu1, wd1_when, "ffn1")
    wd1 = late['wd1']
    h1, = rowcall(lambda h, f, g: h + _half_rms(f, g), [x, sv1[3]], [small['ffn1_post_norm']], [(D_MODEL, F32)],
                  rows=512, total=s, name="ffn1_out")
    w_in_g, w_out, conv_g = gather_wait(1, h1)
    w_in = jnp.concatenate([w_in_g[:, j * IN_SHARD_PAD:j * IN_SHARD_PAD + IN_SHARD] for j in range(N_DEV)],
                           axis=1)
    sp = [0, 512, 1024, 1536, 1544, 2056, 2568, 3080, 3592, 3596, 3600]
    fq, fk, fv, ff, gq, gk, gv, gz, gb, ga = [w_in[:, sp[i]:sp[i + 1]] for i in range(10)]
    w_proj = jnp.concatenate([fq, fk, fv, gq, gk, gv, gz, ff, gb, ga,
                              jnp.zeros((D_MODEL, PROJ_W - 3584 - 16), BF)], axis=1)
    conv_w8 = conv_g.reshape(8, N_DEV, conv_pad)[:, :, :CONV_CH // N_DEV].reshape(8, CONV_CH)


    u2, = rowcall(_rms, [h1], [small['mix_pre_norm']], [(D_MODEL, BF)], rows=512, total=s, name="mix_pre")
    proj = mm(u2, w_proj, name="mix_proj")
    f_cum = fox_f_fwd(proj, bias_row)
    f_heads = f_cum[:, :FOX_HEADS]
    qkv_bf = proj[:, :3 * FOX_HEADS * FOX_DH].astype(BF)
    xk, xv = _fox_extras(s, 1.0, -f_heads), _fox_extras(s, 1.0, None)
    fox_flat, lse = fox_fwd(qkv_bf, _fox_extras(s, f_heads, 1.0), xk, xv)
    lse_heads = lse[:, :, :2].transpose(1, 0, 2).reshape(s, FOX_HEADS)
    cqkv = conv_fwd(proj, conv_w8)
    g_l, b_l = rowcall(_gdn_gates, [(proj, 128, SMALL_BLOCK128)], [gate_prm], [(512, F32), (512, F32)],
                       rows=512, total=s, name="gdn_gates")
    gbb = jnp.concatenate([g_l, b_l], axis=1)
    gdn_o, states = gdn_fwd(cqkv, proj, gbb, onorm)
    mixed = jnp.concatenate([fox_flat, gdn_o], axis=1).astype(BF)
    mo = mm(mixed, w_out, name="mix_out")
    h2, = rowcall(_residual_rms, [h1, mo], [small['mix_post_norm']], [(D_MODEL, F32)], rows=512, total=s,
                  name="mix_res")

    hq, = rowcall(_rms, [h2], [small['mem_pre_norm']], [(D_MODEL, BF)], rows=512, total=s, name="mem_pre")
    mn, = rowcall(_rms, [mem], [small['mem_kv_norm']], [(D_MODEL, BF)], rows=256, total=mem.shape[0], name="mem_kvn")
    wgu2, wd2, w_q, w_kv, w_o = gather_wait(2, h2)
    q_mem = mm(hq, w_q, name="mem_q")
    kv_mem = mm(mn, w_kv, name="mem_kv")
    o_mem = xattn_fwd(q_mem, kv_mem)
    c_mem = mm(o_mem, w_o, name="mem_o")
    h3, = rowcall(_residual_rms, [h2, c_mem], [small['mem_post_norm']], [(D_MODEL, F32)], rows=512, total=s,
                  name="mem_res")

    sv2 = _ffn_fwd(h3, small['ffn2_pre_norm'], wgu2, wd2, "ffn2")

    def b_loss(h, f, tgt, g):
        err = h + _half_rms(f, g) - tgt
        part = 0.5 * jnp.sum(jnp.mean(err * err, axis=-1, keepdims=True), axis=0, keepdims=True)
        return err * (1.0 / D_MODEL), jnp.broadcast_to(part, (1, 128))

    dy, loss_acc = rowcall(b_loss, [h3, sv2[3], a['loss_target'][0]], [small['ffn2_post_norm']], [(D_MODEL, F32)],
                           [(1, 128)], rows=512, total=s, name="loss")

    grads = {}
    dh3, dwgu2, dwd2, grads['ffn2_pre_norm'], grads['ffn2_post_norm'] = _ffn_bwd(
        dy, h3, small['ffn2_pre_norm'], small['ffn2_post_norm'], wgu2, wd2, sv2, "ffn2")

    lead = ("lead", 0, 1, 0)

    def land(r, c, dt=BF):
        return jax.ShapeDtypeStruct((N_DEV, r, c), dt)

    ffn_tr = [(0, (1, 0, sp_, sp_), 0, lead), (0, (1, D_FF_PAD, sp_, sp_), 1, lead), (1, (0, 0, sp_, FF_SHARD), 2, lead)]
    ffn_land = [land(D_MODEL, sp_), land(D_MODEL, sp_), land(FF_SHARD, D_MODEL)]
    a_land = place_own([dwgu2, dwd2], ffn_land, ffn_tr)
    a_send, a_recv, a_src, a_land, a_token = exchange_start([dwgu2, dwd2], a_land, ffn_tr, dh3, "reduce_ffn2_start")

    dc, grads['mem_post_norm'] = rowcall(_bwd_residual, [dh3, c_mem], [small['mem_post_norm'] + a_token[0, 0]],
                                         [(D_MODEL, BF)],
                                         [(1, D_MODEL)], rows=512, total=s, name="mem_bwd_res")
    d_o = mm(dc, w_o, tb=True, name="mem_bwd_do")
    dw_o = mm(o_mem, dc, ta=True, out_dtype=BF, name="mem_bwd_dwo")
    dq_mem, dkv = xattn_bwd(q_mem, kv_mem, d_o)
    dhq = mm(dq_mem, w_q, tb=True, name="mem_bwd_dhq")
    dw_q = mm(hq, dq_mem, ta=True, out_dtype=BF, name="mem_bwd_dwq")
    dmn = mm(dkv, w_kv, tb=True, name="mem_bwd_dmn")
    dw_kv = mm(mn, dkv, ta=True, out_dtype=BF, name="mem_bwd_dwkv")
    _, grads['mem_kv_norm'] = rowcall(lambda d, mv, g: jax.vjp(_rms, mv, g)[1](d), [dmn, mem],
                                      [small['mem_kv_norm']], [(D_MODEL, F32)], [(1, D_MODEL)], rows=256,
                                      total=mem.shape[0], name="mem_bwd_kvn")

    def b_pre(dh, duv, hv, pg):
        dx, dpre = jax.vjp(_rms, hv, pg)[1](duv)
        return dh + dx, dpre

    dh2, grads['mem_pre_norm'] = rowcall(b_pre, [dh3, dhq, h2], [small['mem_pre_norm']], [(D_MODEL, F32)],
                                         [(1, D_MODEL)], rows=512, total=s, name="mem_bwd_pre")

    dmo, grads['mix_post_norm'] = rowcall(_bwd_residual, [dh2, mo], [small['mix_post_norm']], [(D_MODEL, BF)],
                                          [(1, D_MODEL)], rows=512, total=s, name="mix_bwd_res")
    d_mixed = mm(dmo, w_out, tb=True, name="mix_bwd_dmixed")
    dw_out = mm(mixed, dmo, ta=True, out_dtype=BF, name="mix_bwd_dwout")
    def b_delta(do, o):
        sel = (_iota2((512, 128), 0) // FOX_DH == _iota2((512, 128), 1)).astype(F32)
        return hdot(do * o, sel)

    delta, = rowcall(b_delta, [(d_mixed, 512, 0), fox_flat], [], [(128, F32)], rows=512, total=s, name="fox_delta")
    dfox_q, dfox_k, dvf, sum_q, sum_k = fox_bwd(qkv_bf, d_mixed[:, :512].astype(BF), xk, xv,
                                                _fox_extras(s, f_heads - lse_heads, 1.0),
                                                _fox_extras(s, -delta[:, :FOX_HEADS], None))
    per_head = lambda a: a.reshape(s, FOX_PAIRS, 2 * FOX_DH)[:, :, :2].reshape(s, FOX_HEADS)
    d_f = _pad_to(per_head(sum_q) - per_head(sum_k), (s, 128))
    dsmall_f, dbias = fox_f_bwd(proj, bias_row, d_f)
    grads['fox_f_bias'] = dbias[:, :FOX_HEADS]
    dcqkv, dz, dgb, grads['gdn_out_norm'] = gdn_bwd(cqkv, proj, gbb, onorm, states, d_mixed)

    def b_gates(sm, dsf, dg, db, prm):
        dsm, dprm = jax.vjp(_gdn_gates, sm, prm)[1]((dg, db))
        return dsm + dsf, dprm

    dsmall, dprm = rowcall(b_gates, [(proj, 128, SMALL_BLOCK128), dsmall_f, (dgb, 512, 0), (dgb, 512, 1)], [gate_prm],
                           [(128, F32)],
                           [(8, 128)], rows=512, total=s, name="gdn_bwd_gates")
    grads['gdn_a_log'] = dprm[0:1, SMALL_A:SMALL_A + GDN_HEADS]
    grads['gdn_dt_bias'] = dprm[1:2, SMALL_A:SMALL_A + GDN_HEADS]
    dqkv_pre, dconv8 = conv_bwd(proj, conv_w8, dcqkv)
    dproj = jnp.concatenate([dfox_q, dfox_k, dvf, dqkv_pre, dz, dsmall,
                             jnp.zeros((s, PROJ_W - 3584 - 128), F32)], axis=1).astype(BF)
    du2 = mm(dproj, w_proj, tb=True, name="mix_bwd_du")
    dw_proj = mm(u2, dproj, ta=True, out_dtype=BF, name="mix_bwd_dwproj")
    dh1, grads['mix_pre_norm'] = rowcall(b_pre, [dh2, du2, h1], [small['mix_pre_norm']], [(D_MODEL, F32)],
                                         [(1, D_MODEL)], rows=512, total=s, name="mix_bwd_pre")

    dw_in = jnp.concatenate([dw_proj[:, :1536], dw_proj[:, 3584:3592], dw_proj[:, 1536:3584],
                             dw_proj[:, 3592:3600]], axis=1)
    gap = jnp.zeros((D_MODEL, IN_SHARD_PAD - IN_SHARD), BF)
    dw_in = jnp.concatenate([piece for j in range(N_DEV) for piece in (dw_in[:, j * IN_SHARD:(j + 1) * IN_SHARD], gap)],
                            axis=1)
    b_in = [dw_in, dw_out, dw_q, dw_kv, dw_o]
    b_tr = [(0, (1, 0, IN_SHARD_PAD, IN_SHARD_PAD), 0, lead), (1, (0, 0, dm, dm), 1, lead), (2, (0, 0, dm, dm), 2, lead),
            (3, (1, 0, 2 * dm, 2 * dm), 3, lead), (4, (0, 0, dm, dm), 4, lead)]
    b_shapes = [land(D_MODEL, IN_SHARD_PAD), land(dm, D_MODEL), land(dm, D_MODEL), land(D_MODEL, 2 * dm),
                land(dm, D_MODEL)]
    b_land = place_own(b_in, b_shapes, b_tr)
    b_send, b_recv, b_src, b_land, b_token = exchange_start(b_in, b_land, b_tr, dh1, "reduce_mix_start")

    def start_down_reduce(dwd):
        tr = ffn_tr[2:]
        tr = [(0, tr[0][1], 0, tr[0][3])]
        late['c_down'] = (exchange_start([dwd], place_own([dwd], ffn_land[2:], tr), tr, dwd, "reduce_ffn1_down_start"), tr)
        return late['c_down'][0][4]

    def start_gate_up_reduce(dwgu):
        tr = ffn_tr[:2]
        late['c_gu'] = (exchange_start([dwgu], place_own([dwgu], ffn_land[:2], tr), tr, dwgu, "reduce_ffn1_gu_start"), tr)
        return late['c_gu'][0][4]

    grad_x, dwgu1, dwd1, grads['ffn1_pre_norm'], grads['ffn1_post_norm'] = _ffn_bwd(
        dh1, x, small['ffn1_pre_norm'], small['ffn1_post_norm'] + b_token[0, 0], wgu1, wd1, sv1, "ffn1",
        on_dwd=start_down_reduce, on_dwgu=start_gate_up_reduce)

    gpack = _pack(lambda nm: grads[nm], conv=dconv8[:CONV_W], loss=loss_acc[:, :1])
    gsum_parts, = exchange([gpack], [land(PACK_ROWS, D_MODEL, F32)], [(0, whole, 0, lead)], "reduce_small")
    a_got = exchange_wait(a_send, a_recv, a_src, a_land, gsum_parts, ffn_tr, "reduce_ffn2_wait")
    b_got = exchange_wait(b_send, b_recv, b_src, b_land, gsum_parts, b_tr, "reduce_mix_wait")
    recv = dict(zip(['ffn2_w_gate', 'ffn2_w_up', 'ffn2_w_down', 'w_in', 'w_out', 'mem_w_q', 'mem_w_kv', 'mem_w_o'],
                    a_got + b_got))

    out_g, out_d, out_m, out_v = {}, {}, {}, {}

    def update(nm):
        r = recv[nm]
        res = adamw(w2[nm], m2[nm], v2[nm], [(r, r.shape[2], 0, d) for d in range(N_DEV)], "adamw_" + nm)
        out_g[nm], out_d[nm], out_m[nm], out_v[nm] = res

    for nm in recv:
        update(nm)
    wp = _pack(lambda nm: small[nm])
    mp = _pack(lambda nm: m2[nm][None])
    vp = _pack(lambda nm: v2[nm][None])
    pg, pd, pm, pv = adamw(wp, mp, vp, [(gsum_parts, D_MODEL, 0, d) for d in range(N_DEV)], "adamw_small")
    for dst, p in ((out_g, pg), (out_d, pd), (out_m, pm), (out_v, pv)):
        dst.update({k: val[0] for k, val in _unpack(p).items()})
    loss = pg[ROW_MISC, COL_LOSS]
    conv_g = lax.dynamic_slice_in_dim(pg[ROW_CONV:ROW_CONV + 6].reshape(CONV_W, CONV_CH), me * (CONV_CH // N_DEV),
                                      CONV_CH // N_DEV, axis=1)
    res = adamw(w2['gdn_conv_w'], m2['gdn_conv_w'], v2['gdn_conv_w'], [conv_g], "adamw_conv")
    out_g['gdn_conv_w'], out_d['gdn_conv_w'], out_m['gdn_conv_w'], out_v['gdn_conv_w'] = res

    done = sum(out_d[nm][0, 0] for nm in recv) + out_d['gdn_conv_w'][0, 0] + pd[0, 0]
    after = jnp.zeros((8, 128), F32) + done
    c_got = []
    for key, nm in (('c_gu', "reduce_ffn1_gu_wait"), ('c_down', "reduce_ffn1_down_wait")):
        (c_send, c_recv, c_src, c_land, _), tr = late[key]
        c_got += exchange_wait(c_send, c_recv, c_src, c_land, after, tr, nm)
    recv = dict(zip(['ffn1_w_gate', 'ffn1_w_up', 'ffn1_w_down'], c_got))
    for nm in recv:
        update(nm)

    def depth(t):
        return t[None]

    return (loss, grad_x[None], *[depth(out_g[nm]) for nm in WEIGHTS], *[depth(out_d[nm]) for nm in WEIGHTS],
            *[depth(out_m[nm]) for nm in WEIGHTS], *[depth(out_v[nm]) for nm in WEIGHTS])


def kernel(x, mem, ffn1_pre_norm, ffn1_w_gate, ffn1_w_up, ffn1_w_down, ffn1_post_norm, mix_pre_norm, w_in, fox_f_bias, gdn_conv_w, gdn_a_log, gdn_dt_bias, gdn_out_norm, w_out, mix_post_norm, mem_pre_norm, mem_kv_norm, mem_w_q, mem_w_kv, mem_w_o, mem_post_norm, ffn2_pre_norm, ffn2_w_gate, ffn2_w_up, ffn2_w_down, ffn2_post_norm, loss_target, m_ffn1_pre_norm, m_ffn1_w_gate, m_ffn1_w_up, m_ffn1_w_down, m_ffn1_post_norm, m_mix_pre_norm, m_w_in, m_fox_f_bias, m_gdn_conv_w, m_gdn_a_log, m_gdn_dt_bias, m_gdn_out_norm, m_w_out, m_mix_post_norm, m_mem_pre_norm, m_mem_kv_norm, m_mem_w_q, m_mem_w_kv, m_mem_w_o, m_mem_post_norm, m_ffn2_pre_norm, m_ffn2_w_gate, m_ffn2_w_up, m_ffn2_w_down, m_ffn2_post_norm, v_ffn1_pre_norm, v_ffn1_w_gate, v_ffn1_w_up, v_ffn1_w_down, v_ffn1_post_norm, v_mix_pre_norm, v_w_in, v_fox_f_bias, v_gdn_conv_w, v_gdn_a_log, v_gdn_dt_bias, v_gdn_out_norm, v_w_out, v_mix_post_norm, v_mem_pre_norm, v_mem_kv_norm, v_mem_w_q, v_mem_w_kv, v_mem_w_o, v_mem_post_norm, v_ffn2_pre_norm, v_ffn2_w_gate, v_ffn2_w_up, v_ffn2_w_down, v_ffn2_post_norm):
    return _step(dict(locals()))
```

```python
import functools

import jax
import jax.numpy as jnp
from jax import lax
from jax.experimental import pallas as pl
from jax.experimental.pallas import tpu as pltpu

F32 = jnp.float32
BF = jnp.bfloat16
HI = lax.Precision.HIGHEST
MESH = pl.DeviceIdType.MESH

N_DEV = 8
EPS = 1e-6
D_MODEL = 1024
D_FF = 2816
FF_SHARD = D_FF // N_DEV
FF_SHARD_PAD = 384
D_FF_PAD = FF_SHARD_PAD * N_DEV
FOX_HEADS, FOX_DH = 8, 64
GDN_HEADS, GDN_DH = 4, 128
GDN_CHUNK = 64
CONV_W = 4
MEM_HEADS, MEM_DH = 4, 256
IN_W = 3600
IN_SHARD = IN_W // N_DEV
IN_SHARD_PAD = 512
PROJ_W = 4096
SMALL_F, SMALL_B, SMALL_A = 0, 8, 12

ADAM_LR, ADAM_B1, ADAM_B2, ADAM_EPS, ADAM_WD, ADAM_STEP = 0.001, 0.9, 0.999, 1e-08, 0.01, 10

VMEM_LIMIT = 56 * 1024 * 1024


def _params(sem=None):
    return pltpu.CompilerParams(dimension_semantics=sem, vmem_limit_bytes=VMEM_LIMIT)


def _tile(n, pref, unit=128):
    if n <= pref:
        return n
    t = (pref // unit) * unit
    while t > unit and n % t:
        t -= unit
    assert n % t == 0, (n, pref)
    return t


@functools.partial(jax.custom_vjp, nondiff_argnums=(2, 3))
def bdot(a, b, ca, cb):
    return lax.dot_general(a.astype(BF), b.astype(BF), (((ca,), (cb,)), ((), ())), preferred_element_type=F32)


def _bdot_fwd(a, b, ca, cb):
    return bdot(a, b, ca, cb), (a, b)


def _bdot_bwd(ca, cb, res, g):
    a, b = res
    da = bdot(g, b, 1, 1 - cb) if ca == 1 else bdot(b, g, 1 - cb, 1)
    db = bdot(a, g, 1 - ca, 0) if cb == 0 else bdot(g, a, 0, 1 - ca)
    return da, db


bdot.defvjp(_bdot_fwd, _bdot_bwd)


def hdot(a, b):
    return jnp.dot(a, b, precision=HI, preferred_element_type=F32)


def mdot(a, b):
    return jnp.dot(a, b, precision=lax.Precision.HIGH, preferred_element_type=F32)


def _iota2(shape, dim):
    return lax.broadcasted_iota(jnp.int32, shape, dim)


def _sigmoid(x):
    return 1.0 / (1.0 + jnp.exp(-x))


def _silu(x):
    return x * _sigmoid(x)


def _softplus(x):
    return jnp.maximum(x, 0.0) + jnp.log(1.0 + jnp.exp(-jnp.abs(x)))


def _rms(x, gain):
    return x * lax.rsqrt(jnp.mean(x * x, axis=-1, keepdims=True) + EPS) * gain


def mm(a, b, *, name, ta=False, tb=False, out_dtype=F32, tm=1024, tn=1024, tk=1024, token=None):
    m, k = (a.shape[1], a.shape[0]) if ta else a.shape
    n = b.shape[0] if tb else b.shape[1]
    assert k == (b.shape[1] if tb else b.shape[0]), (a.shape, b.shape, ta, tb)
    tm, tn, tk = _tile(m, tm), _tile(n, tn), _tile(k, tk)
    nk = k // tk
    dims = (((0 if ta else 1,), (1 if tb else 0,)), ((), ()))

    def kern(a_ref, b_ref, *rest):
        o_ref, scratch = (rest[1], rest[2:]) if token is not None else (rest[0], rest[1:])

        def part():
            return lax.dot_general(a_ref[...].astype(BF), b_ref[...].astype(BF), dims, preferred_element_type=F32)

        if nk == 1:
            o_ref[...] = part().astype(o_ref.dtype)
            return
        acc_ref, = scratch
        kk = pl.program_id(2)

        @pl.when(kk == 0)
        def _():
            acc_ref[...] = part()

        @pl.when(kk > 0)
        def _():
            acc_ref[...] += part()

        @pl.when(kk == nk - 1)
        def _():
            o_ref[...] = acc_ref[...].astype(o_ref.dtype)

    a_spec = pl.BlockSpec((tk, tm), lambda i, j, kk: (kk, i)) if ta else pl.BlockSpec((tm, tk), lambda i, j, kk: (i, kk))
    b_spec = pl.BlockSpec((tn, tk), lambda i, j, kk: (j, kk)) if tb else pl.BlockSpec((tk, tn), lambda i, j, kk: (kk, j))
    return pl.pallas_call(
        kern, name=name, grid=(m // tm, n // tn, nk),
        in_specs=[a_spec, b_spec] + ([pl.BlockSpec((8, 128), lambda i, j, kk: (0, 0))] if token is not None else []),
        out_specs=pl.BlockSpec((tm, tn), lambda i, j, kk: (i, j)),
        out_shape=jax.ShapeDtypeStruct((m, n), out_dtype),
        scratch_shapes=[pltpu.VMEM((tm, tn), F32)] if nk > 1 else [],
        compiler_params=_params(("parallel", "parallel", "arbitrary")),
    )(*((a, b) if token is None else (a, b, token)))


def _row_spec(item, rows):
    if not isinstance(item, tuple):
        return item, pl.BlockSpec((rows, item.shape[1]), lambda i: (i, 0))
    if len(item) == 3:
        arr, w, c = item
        return arr, pl.BlockSpec((rows, w), lambda i: (i, c))
    arr, w, c, lead = item
    return arr, pl.BlockSpec((None, rows, w), lambda i: (lead, i, c))


def _whole_spec(item):
    if not isinstance(item, tuple):
        return item, pl.BlockSpec(item.shape, lambda i: (0,) * item.ndim)
    arr, w, c = item
    return arr, pl.BlockSpec((arr.shape[0], w), lambda i: (0, c))


def rowcall(body, tiled, whole, outs, accs=(), *, rows, total, name):
    rows = min(rows, total)
    assert total % rows == 0
    t_arr, t_spec = zip(*[_row_spec(t, rows) for t in tiled])
    w_arr, w_spec = zip(*[_whole_spec(w) for w in whole]) if whole else ((), ())
    nt, nw, no, na = len(t_arr), len(w_arr), len(outs), len(accs)

    def kern(*refs):
        vals = [r[...] for r in refs[:nt + nw]]
        res = body(*vals)
        if not isinstance(res, (tuple, list)):
            res = (res,)
        assert len(res) == no + na, (name, len(res), no, na)
        for r, v in zip(refs[nt + nw:nt + nw + no], res[:no]):
            r[...] = v.astype(r.dtype)
        if na:
            acc_refs = refs[nt + nw + no:]

            @pl.when(pl.program_id(0) == 0)
            def _():
                for r in acc_refs:
                    r[...] = jnp.zeros_like(r)

            for r, v in zip(acc_refs, res[no:]):
                r[...] += v

    out_shape = [jax.ShapeDtypeStruct((total, w), d) for w, d in outs] + [jax.ShapeDtypeStruct(s, F32) for s in accs]
    out_specs = [pl.BlockSpec((rows, w), lambda i: (i, 0)) for w, _ in outs] + \
                [pl.BlockSpec(s, lambda i: (0, 0)) for s in accs]
    res = pl.pallas_call(
        kern, name=name, grid=(total // rows,),
        in_specs=list(t_spec) + list(w_spec), out_specs=out_specs, out_shape=out_shape,
        compiler_params=_params(("arbitrary",) if na else ("parallel",)),
    )(*t_arr, *w_arr)
    return res


def _colsum(x):
    return jnp.sum(x, axis=0, keepdims=True)


def _gdn_chunk(q, k, v, z, gb, bb, state, gain):
    c = GDN_CHUNK
    nh = len(q)
    hs = range(nh)
    r64, c64 = _iota2((c, c), 0), _iota2((c, c), 1)
    incl = r64 >= c64
    strict = r64 > c64
    ltri = incl.astype(F32)
    utri = (r64 <= c64).astype(F32)
    eye = (r64 == c64).astype(F32)
    ones = jnp.ones((c, c), F32)
    pick = (_iota2((GDN_DH, c), 0) == _iota2((GDN_DH, c), 1)).astype(F32)
    last = (_iota2((c, GDN_DH), 0) == c - 1).astype(F32)

    qn = [q[h] * lax.rsqrt(jnp.sum(q[h] * q[h], axis=-1, keepdims=True) + EPS) * (GDN_DH ** -0.5) for h in hs]
    kn = [k[h] * lax.rsqrt(jnp.sum(k[h] * k[h], axis=-1, keepdims=True) + EPS) for h in hs]
    gc = [mdot(ltri, gb[h]) for h in hs]
    g64 = [mdot(gb[h], pick) for h in hs]
    gcol = [mdot(ltri, g64[h]) for h in hs]
    grow = [mdot(ones, g64[h] * utri) for h in hs]
    dec = [jnp.exp(jnp.where(incl, gcol[h] - grow[h], -1e30)) for h in hs]
    kb = [kn[h] * bb[h] for h in hs]
    vb = [v[h] * bb[h] for h in hs]
    kk = [bdot(kb[h], kn[h], 1, 1) for h in hs]
    p = [-jnp.where(strict, kk[h] * dec[h], 0.0) for h in hs]
    tinv = [eye + p[h] for h in hs]
    for level in range(5):
        dot = mdot if level < 2 else (lambda a, b: bdot(a, b, 1, 0))
        p = [dot(p[h], p[h]) for h in hs]
        tinv = [tinv[h] + dot(tinv[h], p[h]) for h in hs]
    egc = [jnp.exp(gc[h]) for h in hs]
    u = [mdot(tinv[h], vb[h]) for h in hs]
    w = [mdot(tinv[h], kb[h] * egc[h]) for h in hs]
    attn = [bdot(qn[h], kn[h], 1, 1) * dec[h] for h in hs]
    qd = [qn[h] * egc[h] for h in hs]
    gl = [jnp.sum(gc[h] * last, axis=0, keepdims=True) for h in hs]
    kt = [kn[h] * jnp.exp(gl[h] - gc[h]) for h in hs]
    ws = [bdot(w[h], state[h], 1, 0) for h in hs]
    qs = [bdot(qd[h], state[h], 1, 0) for h in hs]
    v_new = [u[h] - ws[h] for h in hs]
    av = [bdot(attn[h], v_new[h], 1, 0) for h in hs]
    kv = [bdot(kt[h], v_new[h], 0, 0) for h in hs]
    new_state = tuple(state[h] * jnp.exp(gl[h]) + kv[h] for h in hs)
    o = tuple(_rms(qs[h] + av[h], gain) * _silu(z[h]) for h in hs)
    return o, new_state


GDN_ROWS = 512
GDN_W = GDN_HEADS * GDN_DH


def gdn_fwd(cqkv, proj, gbb, gain):
    s = cqkv.shape[0]
    nb, cpb = s // GDN_ROWS, GDN_ROWS // GDN_CHUNK
    h4 = GDN_HEADS

    def kern(qkv_ref, z_ref, gb_ref, gain_ref, o_ref, st_ref, state):
        @pl.when(pl.program_id(0) == 0)
        def _():
            state[...] = jnp.zeros_like(state)

        gain_v = gain_ref[...]

        def step(ci, carry):
            sl = pl.ds(pl.multiple_of(ci * GDN_CHUNK, GDN_CHUNK), GDN_CHUNK)
            ins = []
            for h in range(h4):
                ln = lambda base, h=h: slice(base + h * GDN_DH, base + (h + 1) * GDN_DH)
                ins.append((qkv_ref[sl, ln(0)], qkv_ref[sl, ln(GDN_W)], qkv_ref[sl, ln(2 * GDN_W)], z_ref[sl, ln(0)],
                            gb_ref[sl, ln(0)], gb_ref[sl, ln(GDN_W)], state[h]))
            cols = [tuple(col) for col in zip(*ins)]
            o, new = _gdn_chunk(*cols[:7], gain_v)
            for h in range(h4):
                st_ref[h, ci] = ins[h][6]
                o_ref[sl, h * GDN_DH:(h + 1) * GDN_DH] = o[h]
                state[h] = new[h]
            return carry

        lax.fori_loop(0, cpb, step, 0)

    return pl.pallas_call(
        kern, name="gdn_fwd", grid=(nb,),
        in_specs=[pl.BlockSpec((GDN_ROWS, 3 * GDN_W), lambda i: (i, 0)),
                  pl.BlockSpec((GDN_ROWS, GDN_W), lambda i: (i, 6)),
                  pl.BlockSpec((GDN_ROWS, 2 * GDN_W), lambda i: (i, 0)),
                  pl.BlockSpec((1, GDN_DH), lambda i: (0, 0))],
        out_specs=[pl.BlockSpec((GDN_ROWS, GDN_W), lambda i: (i, 0)),
                   pl.BlockSpec((h4, cpb, GDN_DH, GDN_DH), lambda i: (0, i, 0, 0))],
        out_shape=[jax.ShapeDtypeStruct((s, GDN_W), F32),
                   jax.ShapeDtypeStruct((h4, s // GDN_CHUNK, GDN_DH, GDN_DH), F32)],
        scratch_shapes=[pltpu.VMEM((h4, GDN_DH, GDN_DH), F32)],
        compiler_params=_params(("arbitrary",)),
    )(cqkv, proj, gbb, gain)


def gdn_bwd(cqkv, proj, gbb, gain, states, d_mixed):
    s = cqkv.shape[0]
    nb, cpb = s // GDN_ROWS, GDN_ROWS // GDN_CHUNK
    h4 = GDN_HEADS

    def kern(qkv_ref, z_ref, gb_ref, gain_ref, st_ref, do_ref, dqkv_ref, dz_ref, dgb_ref, dgain_ref, dstate):
        @pl.when(pl.program_id(0) == 0)
        def _():
            dgain_ref[...] = jnp.zeros_like(dgain_ref)
            dstate[...] = jnp.zeros_like(dstate)

        gain_v = gain_ref[...]

        def step(t, carry):
            ci = cpb - 1 - t
            sl = pl.ds(pl.multiple_of(ci * GDN_CHUNK, GDN_CHUNK), GDN_CHUNK)
            prim, cot, dst_in = [], [], []
            for h in range(h4):
                ln = lambda base, h=h: slice(base + h * GDN_DH, base + (h + 1) * GDN_DH)
                prim.append((qkv_ref[sl, ln(0)], qkv_ref[sl, ln(GDN_W)], qkv_ref[sl, ln(2 * GDN_W)], z_ref[sl, ln(0)],
                             gb_ref[sl, ln(0)], gb_ref[sl, ln(GDN_W)], st_ref[h, ci]))
                cot.append(do_ref[sl, ln(0)])
                dst_in.append(dstate[h])
            cols = [tuple(col) for col in zip(*prim)]
            vjp = jax.vjp(_gdn_chunk, *cols, gain_v)[1]
            dq, dk, dv, dz, dg, db, dst, dgn = vjp((tuple(cot), tuple(dst_in)))
            for h in range(h4):
                ln = lambda base, h=h: slice(base + h * GDN_DH, base + (h + 1) * GDN_DH)
                dqkv_ref[sl, ln(0)] = dq[h]
                dqkv_ref[sl, ln(GDN_W)] = dk[h]
                dqkv_ref[sl, ln(2 * GDN_W)] = dv[h]
                dz_ref[sl, ln(0)] = dz[h]
                dgb_ref[sl, ln(0)] = dg[h]
                dgb_ref[sl, ln(GDN_W)] = db[h]
                dstate[h] = dst[h]
            dgain_ref[...] += dgn
            return carry

        lax.fori_loop(0, cpb, step, 0)

    def rev(width, cblock=0):
        return pl.BlockSpec((GDN_ROWS, width), lambda i: (nb - 1 - i, cblock))

    return pl.pallas_call(
        kern, name="gdn_bwd", grid=(nb,),
        in_specs=[rev(3 * GDN_W), rev(GDN_W, 6), rev(2 * GDN_W), pl.BlockSpec((1, GDN_DH), lambda i: (0, 0)),
                  pl.BlockSpec((h4, cpb, GDN_DH, GDN_DH), lambda i: (0, nb - 1 - i, 0, 0)), rev(GDN_W, 1)],
        out_specs=[rev(3 * GDN_W), rev(GDN_W), rev(2 * GDN_W), pl.BlockSpec((1, GDN_DH), lambda i: (0, 0))],
        out_shape=[jax.ShapeDtypeStruct((s, 3 * GDN_W), F32), jax.ShapeDtypeStruct((s, GDN_W), F32),
                   jax.ShapeDtypeStruct((s, 2 * GDN_W), F32), jax.ShapeDtypeStruct((1, GDN_DH), F32)],
        scratch_shapes=[pltpu.VMEM((h4, GDN_DH, GDN_DH), F32)],
        compiler_params=_params(("arbitrary",)),
    )(cqkv, proj, gbb, gain, states, d_mixed)


def _gdn_gates(small, prm):
    w = GDN_HEADS * GDN_DH
    lane, head = _iota2((128, w), 0), _iota2((128, w), 1) // GDN_DH
    sel_b = (lane == SMALL_B + head).astype(F32)
    sel_a = (lane == SMALL_A + head).astype(F32)
    prow = _iota2((8, 128), 0)
    a_log = jnp.sum(prm * (prow == 0).astype(F32), axis=0, keepdims=True)
    dt_b = jnp.sum(prm * (prow == 1).astype(F32), axis=0, keepdims=True)
    beta = _sigmoid(hdot(small, sel_b))
    g = hdot(-jnp.exp(a_log) * _softplus(small + dt_b), sel_a)
    return g, beta


CONV_ROWS = 1024
CONV_COLS = 128
CONV_BLOCK0 = 1536 // CONV_COLS


def _shift_down(prev8, cur, s):
    ext = jnp.concatenate([prev8, cur], axis=0)
    return pltpu.roll(ext, s, 0)[8:]


def _shift_up(cur, next8, s):
    n = cur.shape[0]
    ext = jnp.concatenate([cur, next8], axis=0)
    return pltpu.roll(ext, n + 8 - s, 0)[:n]


def _conv_pre(x_ref, w, ci, nchunk):
    r0 = pl.multiple_of(ci * CONV_ROWS, CONV_ROWS)
    cur = x_ref[pl.ds(r0, CONV_ROWS), :]
    prev = x_ref[pl.ds(pl.multiple_of(jnp.maximum(r0 - 8, 0), 8), 8), :]
    prev = jnp.where(ci > 0, prev, 0.0)
    shifted = [cur] + [_shift_down(prev, cur, s) for s in range(1, CONV_W)]
    pre = w[CONV_W - 1:CONV_W, :] * cur
    for s in range(1, CONV_W):
        pre = pre + w[CONV_W - 1 - s:CONV_W - s, :] * shifted[s]
    return r0, pre, shifted


def conv_fwd(proj, conv_w8):
    s = proj.shape[0]
    nchunk = s // CONV_ROWS
    ncol = 3 * GDN_HEADS * GDN_DH // CONV_COLS

    def kern(x_ref, w_ref, y_ref):
        w = w_ref[...]

        def step(ci, carry):
            r0, pre, _ = _conv_pre(x_ref, w, ci, nchunk)
            y_ref[pl.ds(r0, CONV_ROWS), :] = _silu(pre)
            return carry

        lax.fori_loop(0, nchunk, step, 0)

    return pl.pallas_call(
        kern, name="conv_fwd", grid=(ncol,),
        in_specs=[pl.BlockSpec((s, CONV_COLS), lambda j: (0, CONV_BLOCK0 + j)),
                  pl.BlockSpec((8, CONV_COLS), lambda j: (0, j))],
        out_specs=pl.BlockSpec((s, CONV_COLS), lambda j: (0, j)),
        out_shape=jax.ShapeDtypeStruct((s, ncol * CONV_COLS), F32),
        compiler_params=_params(("parallel",)),
    )(proj, conv_w8)


def conv_bwd(proj, conv_w8, dy):
    s = proj.shape[0]
    nchunk = s // CONV_ROWS
    per = 3 * GDN_HEADS * GDN_DH // CONV_COLS
    outs = []
    for part in range(1):
        def kern(x_ref, w_ref, dy_ref, dx_ref, dw_ref, dpre_ref):
            w = w_ref[...]
            rows8 = _iota2((8, CONV_COLS), 0)

            def step1(ci, dw):
                r0, pre, shifted = _conv_pre(x_ref, w, ci, nchunk)
                sg = _sigmoid(pre)
                dpre = dy_ref[pl.ds(r0, CONV_ROWS), :] * sg * (1.0 + pre * (1.0 - sg))
                dpre_ref[pl.ds(r0, CONV_ROWS), :] = dpre
                for sh in range(CONV_W):
                    dw = dw + jnp.where(rows8 == CONV_W - 1 - sh, _colsum(dpre * shifted[sh]), 0.0)
                return dw

            dw_ref[...] = lax.fori_loop(0, nchunk, step1, jnp.zeros((8, CONV_COLS), F32))

            def step2(ci, carry):
                r0 = pl.multiple_of(ci * CONV_ROWS, CONV_ROWS)
                cur = dpre_ref[pl.ds(r0, CONV_ROWS), :]
                nxt = dpre_ref[pl.ds(pl.multiple_of(jnp.minimum(r0 + CONV_ROWS, s - 8), 8), 8), :]
                nxt = jnp.where(ci < nchunk - 1, nxt, 0.0)
                dx = w[CONV_W - 1:CONV_W, :] * cur
                for sh in range(1, CONV_W):
                    dx = dx + w[CONV_W - 1 - sh:CONV_W - sh, :] * _shift_up(cur, nxt, sh)
                dx_ref[pl.ds(r0, CONV_ROWS), :] = dx
                return carry

            lax.fori_loop(0, nchunk, step2, 0)

        outs.append(pl.pallas_call(
            kern, name=f"conv_bwd{part}", grid=(per,),
            in_specs=[pl.BlockSpec((s, CONV_COLS), lambda j, part=part: (0, CONV_BLOCK0 + part * per + j)),
                      pl.BlockSpec((8, CONV_COLS), lambda j, part=part: (0, part * per + j)),
                      pl.BlockSpec((s, CONV_COLS), lambda j: (0, j))],
            out_specs=[pl.BlockSpec((s, CONV_COLS), lambda j: (0, j)),
                       pl.BlockSpec((8, CONV_COLS), lambda j: (0, j))],
            out_shape=[jax.ShapeDtypeStruct((s, per * CONV_COLS), F32),
                       jax.ShapeDtypeStruct((8, per * CONV_COLS), F32)],
            scratch_shapes=[pltpu.VMEM((s, CONV_COLS), F32)],
            compiler_params=_params(("parallel",)),
        )(proj, conv_w8, dy))
    dx = jnp.concatenate([o[0] for o in outs], axis=1)
    dw = jnp.concatenate([o[1] for o in outs], axis=1)
    return dx, dw


FOXF_ROWS = 512
SMALL_BLOCK128 = 3584 // 128


def _log_sigmoid(x):
    return jnp.minimum(x, 0.0) - jnp.log(1.0 + jnp.exp(-jnp.abs(x)))


def fox_f_fwd(proj, bias_row):
    s = proj.shape[0]
    n = s // FOXF_ROWS

    def kern(x_ref, b_ref, f_ref, carry):
        @pl.when(pl.program_id(0) == 0)
        def _():
            carry[...] = jnp.zeros_like(carry)

        heads = _iota2((FOXF_ROWS, 128), 1) < FOX_HEADS
        lf = jnp.where(heads, _log_sigmoid(x_ref[...] + b_ref[...]), 0.0)
        ltri = (_iota2((FOXF_ROWS, FOXF_ROWS), 0) >= _iota2((FOXF_ROWS, FOXF_ROWS), 1)).astype(F32)
        c = hdot(ltri, lf) + carry[...]
        f_ref[...] = c
        carry[...] = c[FOXF_ROWS - 1:FOXF_ROWS, :]

    return pl.pallas_call(
        kern, name="fox_f_fwd", grid=(n,),
        in_specs=[pl.BlockSpec((FOXF_ROWS, 128), lambda i: (i, SMALL_BLOCK128)),
                  pl.BlockSpec((1, 128), lambda i: (0, 0))],
        out_specs=pl.BlockSpec((FOXF_ROWS, 128), lambda i: (i, 0)),
        out_shape=jax.ShapeDtypeStruct((s, 128), F32),
        scratch_shapes=[pltpu.VMEM((1, 128), F32)],
        compiler_params=_params(("arbitrary",)),
    )(proj, bias_row)


def fox_f_bwd(proj, bias_row, d_f):
    s = proj.shape[0]
    n = s // FOXF_ROWS

    def kern(x_ref, b_ref, df_ref, dx_ref, db_ref, carry):
        @pl.when(pl.program_id(0) == 0)
        def _():
            carry[...] = jnp.zeros_like(carry)
            db_ref[...] = jnp.zeros_like(db_ref)

        heads = _iota2((FOXF_ROWS, 128), 1) < FOX_HEADS
        utri = (_iota2((FOXF_ROWS, FOXF_ROWS), 0) <= _iota2((FOXF_ROWS, FOXF_ROWS), 1)).astype(F32)
        rc = hdot(utri, df_ref[...]) + carry[...]
        carry[...] = rc[0:1, :]
        dx = jnp.where(heads, rc * _sigmoid(-(x_ref[...] + b_ref[...])), 0.0)
        dx_ref[...] = dx
        db_ref[...] += _colsum(dx)

    return pl.pallas_call(
        kern, name="fox_f_bwd", grid=(n,),
        in_specs=[pl.BlockSpec((FOXF_ROWS, 128), lambda i: (n - 1 - i, SMALL_BLOCK128)),
                  pl.BlockSpec((1, 128), lambda i: (0, 0)),
                  pl.BlockSpec((FOXF_ROWS, 128), lambda i: (n - 1 - i, 0))],
        out_specs=[pl.BlockSpec((FOXF_ROWS, 128), lambda i: (n - 1 - i, 0)),
                   pl.BlockSpec((1, 128), lambda i: (0, 0))],
        out_shape=[jax.ShapeDtypeStruct((s, 128), F32), jax.ShapeDtypeStruct((1, 128), F32)],
        scratch_shapes=[pltpu.VMEM((1, 128), F32)],
        compiler_params=_params(("arbitrary",)),
    )(proj, bias_row, d_f)


FOX_T = 512
FOX_SCALE = FOX_DH ** -0.5
FOX_PAIRS = FOX_HEADS // 2
NEG = -1e30
_NT = (((1,), (1,)), ((), ()))


def _split3(x):
    def bf(v):
        return lax.reduce_precision(v, exponent_bits=8, mantissa_bits=7)

    hi = bf(x)
    mid = bf(x - hi)
    lo = bf(x - hi - mid)
    return jnp.stack([hi, mid, lo], axis=-1)


def _fox_extras(s, first, second):
    def part(v):
        if v is None:
            return jnp.zeros((s, FOX_HEADS, 3), F32)
        if isinstance(v, float):
            return jnp.full((s, FOX_HEADS, 3), v, F32)
        pairs = v.reshape(s, FOX_PAIRS, 2)
        return _split3(jnp.stack([pairs[:, :, 1], pairs[:, :, 0]], axis=-1).reshape(s, FOX_HEADS))

    cols = jnp.concatenate([part(first), part(second)], axis=-1)
    cols = _pad_to(cols, (s, FOX_HEADS, FOX_DH)).reshape(s, FOX_PAIRS, 2 * FOX_DH)
    return cols.transpose(1, 0, 2).astype(BF)


def _head_masks(rows):
    lane = _iota2((rows, 2 * FOX_DH), 1)
    return lane < FOX_DH, lane >= FOX_DH


def _extra_lane(e, slot):
    return (FOX_DH if e == 0 else 0) + slot


def fox_fwd(qkv, xq, xk, xv):
    s = qkv.shape[0]
    t = min(FOX_T, s)
    n = s // t

    def kern(q_ref, k_ref, v_ref, xq_ref, xk_ref, xv_ref, o_ref, lse_ref):
        i = pl.program_id(1)
        masks = _head_masks(t)
        q_pair, x_pair = q_ref[...] * FOX_SCALE, xq_ref[...]
        q_ops = [jnp.where(mk, q_pair, x_pair) for mk in masks]

        def step(j, carry, masked):
            sl = pl.ds(pl.multiple_of(j * t, t), t)
            k_pair, xk_pair, v_pair, xv_pair = k_ref[sl, :], xk_ref[sl, :], v_ref[sl, :], xv_ref[sl, :]
            k_ops = [jnp.where(mk, k_pair, xk_pair) for mk in masks]
            v_ops = [jnp.where(mk, v_pair, xv_pair) for mk in masks]
            sc = [lax.dot_general(q_ops[e], k_ops[e], _NT, preferred_element_type=F32) for e in range(2)]
            if masked:
                keep = _iota2((t, t), 0) >= _iota2((t, t), 1)
                sc = [jnp.where(keep, x, NEG) for x in sc]
            m_new = [jnp.maximum(carry[e][0], jnp.max(sc[e], axis=1, keepdims=True)) for e in range(2)]
            p = [jnp.exp(sc[e] - m_new[e]).astype(BF) for e in range(2)]
            pv = [jnp.dot(p[e], v_ops[e], preferred_element_type=F32) for e in range(2)]
            return tuple((m_new[e], jnp.exp(carry[e][0] - m_new[e]) * carry[e][1] + pv[e]) for e in range(2))

        init = tuple((jnp.full((t, 1), NEG, F32), jnp.zeros((t, 2 * FOX_DH), F32)) for _ in range(2))
        carry = lax.fori_loop(0, i, lambda j, c: step(j, c, False), init)
        carry = step(i, carry, True)
        lane = _iota2((t, 2 * FOX_DH), 1)
        outs, lses = [], []
        for e in range(2):
            m, acc = carry[e]
            l = jnp.sum(jnp.where(lane == _extra_lane(e, 0), acc, 0.0), axis=1, keepdims=True)
            outs.append(acc / l)
            lses.append(m + jnp.log(l))
        o_ref[...] = jnp.where(masks[0], outs[0], outs[1])
        lse_ref[...] = jnp.where(lane == 0, lses[0], jnp.where(lane == 1, lses[1], 0.0))

    pr = FOX_PAIRS
    return pl.pallas_call(
        kern, name="fox_fwd", grid=(pr, n),
        in_specs=[pl.BlockSpec((t, 128), lambda p, i: (i, p)),
                  pl.BlockSpec((s, 128), lambda p, i: (0, pr + p)),
                  pl.BlockSpec((s, 128), lambda p, i: (0, 2 * pr + p)),
                  pl.BlockSpec((None, t, 128), lambda p, i: (p, i, 0)),
                  pl.BlockSpec((None, s, 128), lambda p, i: (p, 0, 0)),
                  pl.BlockSpec((None, s, 128), lambda p, i: (p, 0, 0))],
        out_specs=[pl.BlockSpec((t, 128), lambda p, i: (i, p)),
                   pl.BlockSpec((None, t, 128), lambda p, i: (p, i, 0))],
        out_shape=[jax.ShapeDtypeStruct((s, FOX_HEADS * FOX_DH), F32), jax.ShapeDtypeStruct((pr, s, 128), F32)],
        compiler_params=_params(("parallel", "parallel")),
    )(qkv, qkv, qkv, xq, xk, xv)


def fox_bwd(qkv, d_o, xk, xv, xqb, xdo):
    s = qkv.shape[0]
    t = min(FOX_T, s)
    n = s // t
    w = 2 * FOX_DH

    def both(blocks, slot):
        lane = _iota2(blocks[0].shape, 1)
        own = jnp.where(lane < FOX_DH, blocks[0], blocks[1])
        sums = [jnp.sum(jnp.where(lane == _extra_lane(e, slot), blocks[e], 0.0), axis=1, keepdims=True)
                for e in range(2)]
        return own, jnp.where(lane == 0, sums[0], jnp.where(lane == 1, sums[1], 0.0))

    def kern(k_ref, v_ref, xk_ref, xv_ref, q_ref, do_ref, xq_ref, xd_ref,
             dq_ref, dk_ref, dv_ref, sq_ref, sk_ref, dq_acc):
        j = pl.program_id(1)

        @pl.when(j == 0)
        def _():
            dq_acc[...] = jnp.zeros_like(dq_acc)

        masks = _head_masks(t)
        k_ops = [jnp.where(mk, k_ref[...], xk_ref[...]) for mk in masks]
        v_ops = [jnp.where(mk, v_ref[...], xv_ref[...]) for mk in masks]

        def step(i, carry, masked):
            dk, dv = carry
            sl = pl.ds(pl.multiple_of(i * t, t), t)
            q_pair, xq_pair, do_pair, xd_pair = q_ref[sl, :] * FOX_SCALE, xq_ref[sl, :], do_ref[sl, :], xd_ref[sl, :]
            q_ops = [jnp.where(mk, q_pair, xq_pair) for mk in masks]
            do_ops = [jnp.where(mk, do_pair, xd_pair) for mk in masks]
            do_own = [jnp.where(mk, do_pair, 0).astype(BF) for mk in masks]
            st = [lax.dot_general(k_ops[e], q_ops[e], _NT, preferred_element_type=F32) for e in range(2)]
            dp = [lax.dot_general(v_ops[e], do_ops[e], _NT, preferred_element_type=F32) for e in range(2)]
            if masked:
                keep = _iota2((t, t), 0) <= _iota2((t, t), 1)
                st = [jnp.where(keep, x, NEG) for x in st]
            pt = [jnp.exp(x) for x in st]
            dsb = [(pt[e] * dp[e]).astype(BF) for e in range(2)]
            dv = dv + sum(jnp.dot(pt[e].astype(BF), do_own[e], preferred_element_type=F32) for e in range(2))
            dk = tuple(dk[e] + jnp.dot(dsb[e], q_ops[e], preferred_element_type=F32) for e in range(2))
            for e in range(2):
                dq_acc[sl, e * w:(e + 1) * w] += lax.dot_general(dsb[e], k_ops[e], (((0,), (0,)), ((), ())),
                                                                 preferred_element_type=F32)
            return dk, dv

        init = ((jnp.zeros((t, w), F32), jnp.zeros((t, w), F32)), jnp.zeros((t, w), F32))
        carry = step(j, init, True)
        dk, dv = lax.fori_loop(j + 1, n, lambda i, c: step(i, c, False), carry)
        dk_ref[...], sk_ref[...] = both(dk, 3)
        dv_ref[...] = dv

        @pl.when(j == n - 1)
        def _():
            def out(r, carry):
                sl = pl.ds(pl.multiple_of(r * t, t), t)
                own, sums = both([dq_acc[sl, e * w:(e + 1) * w] for e in range(2)], 0)
                dq_ref[sl, :] = own * FOX_SCALE
                sq_ref[sl, :] = sums
                return carry

            lax.fori_loop(0, n, out, 0)

    pr = FOX_PAIRS
    flat = jax.ShapeDtypeStruct((s, FOX_HEADS * FOX_DH), F32)
    tile = pl.BlockSpec((t, 128), lambda p, j: (j, p))
    whole = pl.BlockSpec((s, 128), lambda p, j: (0, p))
    return pl.pallas_call(
        kern, name="fox_bwd", grid=(pr, n),
        in_specs=[pl.BlockSpec((t, 128), lambda p, j: (j, pr + p)),
                  pl.BlockSpec((t, 128), lambda p, j: (j, 2 * pr + p)),
                  pl.BlockSpec((None, t, 128), lambda p, j: (p, j, 0)),
                  pl.BlockSpec((None, t, 128), lambda p, j: (p, j, 0)),
                  whole, whole,
                  pl.BlockSpec((None, s, 128), lambda p, j: (p, 0, 0)),
                  pl.BlockSpec((None, s, 128), lambda p, j: (p, 0, 0))],
        out_specs=[whole, tile, tile, whole, tile],
        out_shape=[flat] * 5,
        scratch_shapes=[pltpu.VMEM((s, 2 * w), F32)],
        compiler_params=_params(("parallel", "arbitrary")),
    )(qkv, qkv, xk, xv, qkv, d_o, xqb, xdo)


def _xattn_head(q, k, v):
    sc = bdot(q, k, 1, 1) * (MEM_DH ** -0.5)
    e = jnp.exp(sc - lax.stop_gradient(jnp.max(sc, axis=-1, keepdims=True)))
    p = e / jnp.sum(e, axis=-1, keepdims=True)
    return bdot(p, v, 1, 0)


def xattn_fwd(q, kv):
    s = q.shape[0]
    hh = MEM_HEADS

    def body(*vals):
        qs, ks, vs = vals[:hh], vals[hh:2 * hh], vals[2 * hh:]
        return jnp.concatenate([_xattn_head(qs[a], ks[a], vs[a]) for a in range(hh)], axis=1)

    return rowcall(body, [(q, MEM_DH, a) for a in range(hh)],
                   [(kv, MEM_DH, a) for a in range(2 * hh)],
                   [(hh * MEM_DH, BF)], rows=512, total=s, name="xattn_fwd")[0]


def xattn_bwd(q, kv, d_o):
    s = q.shape[0]
    hh = MEM_HEADS

    def body(*vals):
        qs, dos = vals[:hh], vals[hh:2 * hh]
        ks, vs = vals[2 * hh:3 * hh], vals[3 * hh:]
        dqs, dks, dvs = [], [], []
        for a in range(hh):
            _, vjp = jax.vjp(_xattn_head, qs[a], ks[a], vs[a])
            dq, dk, dv = vjp(dos[a])
            dqs.append(dq)
            dks.append(dk)
            dvs.append(dv)
        return jnp.concatenate(dqs, axis=1), jnp.concatenate(dks + dvs, axis=1)

    return rowcall(body, [(q, MEM_DH, a) for a in range(hh)] + [(d_o, MEM_DH, a) for a in range(hh)],
                   [(kv, MEM_DH, a) for a in range(2 * hh)],
                   [(hh * MEM_DH, BF)], [kv.shape], rows=512, total=s, name="xattn_bwd")


def _slab(ref, axis, start, size):
    if axis is None:
        return ref
    if axis == "lead":
        return ref.at[start]
    idx = pl.ds(pl.multiple_of(start, 128 if axis == 1 else 16), size)
    return ref.at[idx] if axis == 0 else ref.at[:, idx]


def exchange(inputs, outputs, transfers, name):
    ni, no, nt = len(inputs), len(outputs), len(transfers)
    npeer = N_DEV - 1

    def body(*refs):
        ins, outs = refs[:ni], refs[ni:ni + no]
        send, recv, loc = refs[ni + no:]
        x, y, c = lax.axis_index("x"), lax.axis_index("y"), lax.axis_index("c")
        me = 4 * x + 2 * y + c

        def peer(p):
            px = 1 - x if p & 4 else x
            py = 1 - y if p & 2 else y
            pc = 1 - c if p & 1 else c
            return (px, py, pc), 4 * px + 2 * py + pc

        def view(ref, spec, who):
            axis, off, stride, size = spec
            return _slab(ref, axis, off + who * stride, size)

        local, remote = [], []
        for w, (ii, src, oi, dst) in enumerate(transfers):
            cp = pltpu.make_async_copy(view(ins[ii], src, me), view(outs[oi], dst, me), loc.at[w])
            cp.start()
            local.append(cp)
        for p in range(1, N_DEV):
            dev, idx = peer(p)
            for w, (ii, src, oi, dst) in enumerate(transfers):
                k = w * npeer + p - 1
                out_cp = pltpu.make_async_remote_copy(
                    src_ref=view(ins[ii], src, idx), dst_ref=view(outs[oi], dst, me), send_sem=send.at[k],
                    recv_sem=recv.at[k], device_id=dev, device_id_type=MESH)
                out_cp.start()
                in_cp = pltpu.make_async_remote_copy(
                    src_ref=view(ins[ii], src, idx), dst_ref=view(outs[oi], dst, idx), send_sem=send.at[k],
                    recv_sem=recv.at[k], device_id=dev, device_id_type=MESH)
                remote.append((out_cp, in_cp))
        for out_cp, in_cp in remote:
            in_cp.wait_recv()
            out_cp.wait_send()
        for cp in local:
            cp.wait()

    hbm = pl.BlockSpec(memory_space=pl.ANY)
    return pl.pallas_call(
        body, name=name, in_specs=[hbm] * ni, out_specs=[hbm] * no, out_shape=list(outputs),
        scratch_shapes=[pltpu.SemaphoreType.DMA((nt * npeer,)), pltpu.SemaphoreType.DMA((nt * npeer,)),
                        pltpu.SemaphoreType.DMA((nt,))],
        compiler_params=pltpu.CompilerParams(has_side_effects=True),
    )(*inputs)


def _peer(p):
    x, y, c = lax.axis_index("x"), lax.axis_index("y"), lax.axis_index("c")
    px = 1 - x if p & 4 else x
    py = 1 - y if p & 2 else y
    pc = 1 - c if p & 1 else c
    return (px, py, pc), 4 * px + 2 * py + pc


def _view(ref, spec, who):
    axis, off, stride, size = spec
    return _slab(ref, axis, off + who * stride, size)


def place_own(inputs, outputs, transfers):
    me = 4 * lax.axis_index("x") + 2 * lax.axis_index("y") + lax.axis_index("c")
    lands = [lax.empty(o.shape, o.dtype) for o in outputs]
    for ii, src, oi, dst in transfers:
        axis, off, stride, size = src
        own = inputs[ii] if axis is None else lax.dynamic_slice_in_dim(inputs[ii], off + me * stride, size, axis)
        axis, off, stride, size = dst
        if axis == "lead":
            lands[oi] = lax.dynamic_update_slice_in_dim(lands[oi], own[None], me, 0)
        else:
            lands[oi] = lax.dynamic_update_slice_in_dim(lands[oi], own, off + me * stride, axis)
    return lands


_HBM = pl.BlockSpec(memory_space=pltpu.HBM)
_SEM = pl.BlockSpec(memory_space=pltpu.SEMAPHORE)
_EFFECT = pltpu.SideEffectType.DATAFLOW_SIDE_EFFECTING


def _remote_copies(ins, lands, transfers, send, recv):
    npeer = N_DEV - 1
    me = 4 * lax.axis_index("x") + 2 * lax.axis_index("y") + lax.axis_index("c")
    pairs = []
    for p in range(1, N_DEV):
        dev, idx = _peer(p)
        for w, (ii, src, oi, dst) in enumerate(transfers):
            k = w * npeer + p - 1
            common = dict(src_ref=_view(ins[ii], src, idx), send_sem=send.at[k], recv_sem=recv.at[k],
                          device_id=dev, device_id_type=MESH)
            pairs.append((pltpu.make_async_remote_copy(dst_ref=_view(lands[oi], dst, me), **common),
                          pltpu.make_async_remote_copy(dst_ref=_view(lands[oi], dst, idx), **common)))
    return pairs


def exchange_start(inputs, lands, transfers, after, name):
    ni, nl, nsem = len(inputs), len(lands), len(transfers) * (N_DEV - 1)

    def body(*refs):
        ins, lnd = refs[:ni], refs[ni:ni + nl]
        send, recv = refs[ni + nl + 1], refs[ni + nl + 2]
        token = refs[-1]
        for out_cp, _ in _remote_copies(ins, lnd, transfers, send, recv):
            out_cp.start()
        token[...] = jnp.zeros_like(token)

    args = [pltpu.with_memory_space_constraint(a, pltpu.HBM) for a in list(inputs) + list(lands)]
    res = pl.pallas_call(
        body, name=name,
        out_shape=(pltpu.SemaphoreType.DMA((nsem,)), pltpu.SemaphoreType.DMA((nsem,)),
                   *[pltpu.HBM(a.shape, a.dtype) for a in args], jax.ShapeDtypeStruct((8, 128), F32)),
        in_specs=[_HBM] * (ni + nl) + [pl.BlockSpec(memory_space=pl.ANY)],
        out_specs=(_SEM, _SEM, *[_HBM] * (ni + nl), pl.BlockSpec(memory_space=pltpu.VMEM)),
        input_output_aliases={k: k + 2 for k in range(ni + nl)},
        compiler_params=pltpu.CompilerParams(has_side_effects=_EFFECT),
    )(*args, after)
    return res[0], res[1], list(res[2:2 + ni]), list(res[2 + ni:2 + ni + nl]), res[-1]


def exchange_wait(send, recv, inputs, lands, after, transfers, name):
    ni, nl = len(inputs), len(lands)

    def body(*refs):
        ins, lnd = refs[:ni], refs[ni:ni + nl]
        send_r, recv_r = refs[ni + nl], refs[ni + nl + 1]
        for out_cp, in_cp in _remote_copies(ins, lnd, transfers, send_r, recv_r):
            out_cp.wait_send()
            in_cp.wait_recv()

    res = pl.pallas_call(
        body, name=name,
        out_shape=tuple(pltpu.HBM(a.shape, a.dtype) for a in list(inputs) + list(lands)),
        in_specs=[_HBM] * (ni + nl) + [_SEM, _SEM, pl.BlockSpec(memory_space=pl.ANY)],
        out_specs=tuple([_HBM] * (ni + nl)),
        input_output_aliases={k: k for k in range(ni + nl)},
        compiler_params=pltpu.CompilerParams(has_side_effects=_EFFECT),
    )(*inputs, *lands, send, recv, after)
    return list(res[ni:])


def adamw(w, m, v, contribs, name):
    r, c = w.shape
    nc = len(contribs)
    rows = next((r // d for d in (4, 2) if r % d == 0 and (r // d) % 16 == 0), r)
    c1, c2 = 1.0 - ADAM_B1 ** ADAM_STEP, 1.0 - ADAM_B2 ** ADAM_STEP

    def body(wv, mv, vv, *gs):
        g = gs[0].astype(F32)
        for extra in gs[1:]:
            g = g + extra.astype(F32)
        g = g[:, :c]
        m_new = ADAM_B1 * mv + (1.0 - ADAM_B1) * g
        v_new = ADAM_B2 * vv + (1.0 - ADAM_B2) * (g * g)
        delta = -ADAM_LR * ((m_new / c1) / (jnp.sqrt(v_new / c2) + ADAM_EPS) + ADAM_WD * wv)
        return g, delta, m_new, v_new

    assert nc >= 1
    return rowcall(body, [w, m, v] + list(contribs), [], [(c, F32)] * 4, rows=rows, total=r, name=name)


WEIGHTS = ['ffn1_pre_norm', 'ffn1_w_gate', 'ffn1_w_up', 'ffn1_w_down', 'ffn1_post_norm', 'mix_pre_norm', 'w_in',
           'fox_f_bias', 'gdn_conv_w', 'gdn_a_log', 'gdn_dt_bias', 'gdn_out_norm', 'w_out', 'mix_post_norm',
           'mem_pre_norm', 'mem_kv_norm', 'mem_w_q', 'mem_w_kv', 'mem_w_o', 'mem_post_norm', 'ffn2_pre_norm',
           'ffn2_w_gate', 'ffn2_w_up', 'ffn2_w_down', 'ffn2_post_norm']
GAINS = ['ffn1_pre_norm', 'ffn1_post_norm', 'mix_pre_norm', 'mix_post_norm', 'mem_pre_norm', 'mem_kv_norm',
         'mem_post_norm', 'ffn2_pre_norm', 'ffn2_post_norm']
BIG = ['ffn1_w_gate', 'ffn1_w_up', 'ffn1_w_down', 'w_in', 'w_out', 'mem_w_q', 'mem_w_kv', 'mem_w_o',
       'ffn2_w_gate', 'ffn2_w_up', 'ffn2_w_down']
PACK_ROWS = 24
ROW_MISC = len(GAINS)
ROW_CONV = ROW_MISC + 1
COL_FBIAS, COL_ALOG, COL_DTB, COL_ONORM, COL_LOSS = 0, 8, 12, 128, 256
CONV_CH = 3 * GDN_HEADS * GDN_DH


def _pad_to(a, shape):
    return jnp.pad(a, [(0, t - s) for s, t in zip(a.shape, shape)])


def _pack(get, conv=None, loss=None):
    rows = [get(nm) for nm in GAINS]
    misc = jnp.concatenate([get('fox_f_bias'), get('gdn_a_log'), get('gdn_dt_bias'),
                            jnp.zeros((1, COL_ONORM - COL_DTB - 4), F32), get('gdn_out_norm'),
                            jnp.zeros((1, 1), F32) if loss is None else loss.reshape(1, 1)], axis=1)
    rows.append(_pad_to(misc, (1, D_MODEL)))
    rows.append(jnp.zeros((6, D_MODEL), F32) if conv is None else conv.reshape(6, D_MODEL))
    return _pad_to(jnp.concatenate(rows, axis=0), (PACK_ROWS, D_MODEL))


def _unpack(p):
    out = {nm: p[i:i + 1] for i, nm in enumerate(GAINS)}
    misc = p[ROW_MISC:ROW_MISC + 1]
    out['fox_f_bias'] = misc[:, COL_FBIAS:COL_FBIAS + FOX_HEADS]
    out['gdn_a_log'] = misc[:, COL_ALOG:COL_ALOG + GDN_HEADS]
    out['gdn_dt_bias'] = misc[:, COL_DTB:COL_DTB + GDN_HEADS]
    out['gdn_out_norm'] = misc[:, COL_ONORM:COL_ONORM + GDN_DH]
    return out


def _ffn_fwd(h, pre, wgu, wd, tag):
    s = h.shape[0]
    u, = rowcall(_rms, [h], [pre], [(D_MODEL, BF)], rows=512, total=s, name=tag + "_pre")
    gu = mm(u, wgu, out_dtype=BF, name=tag + "_gate_up")
    act, = rowcall(lambda a, b: _silu(a.astype(F32)) * b.astype(F32), [(gu, D_FF_PAD, 0), (gu, D_FF_PAD, 1)], [],
                   [(D_FF_PAD, BF)],
                   rows=256, total=s, name=tag + "_act")
    if callable(wd):
        wd = wd(act)
    f = mm(act, wd, name=tag + "_down")
    return u, gu, act, f


def _half_rms(a, g):
    return 0.5 * _rms(a, g)


def _ffn_bwd(dh_out, h, pre, post, wgu, wd, saved, tag, on_dwd=None, on_dwgu=None):
    u, gu, act, f = saved
    s = h.shape[0]

    def b_post(dh, fv, pg):
        return jax.vjp(_half_rms, fv, pg)[1](dh)

    df, dpost = rowcall(b_post, [dh_out, f], [post], [(D_MODEL, BF)], [(1, D_MODEL)], rows=512, total=s,
                        name=tag + "_bwd_post")
    dwd = mm(act, df, ta=True, out_dtype=BF, name=tag + "_bwd_dwd")
    dact = mm(df, wd, tb=True, out_dtype=BF, name=tag + "_bwd_dact", token=on_dwd(dwd) if on_dwd else None)

    def b_act(a, b, da):
        dg, du = jax.vjp(lambda g_, u_: _silu(g_) * u_, a.astype(F32), b.astype(F32))[1](da.astype(F32))
        return jnp.concatenate([dg, du], axis=1)

    dgu, = rowcall(b_act, [(gu, D_FF_PAD, 0), (gu, D_FF_PAD, 1), dact], [], [(2 * D_FF_PAD, BF)], rows=256, total=s,
                   name=tag + "_bwd_act")
    dwgu = mm(u, dgu, ta=True, out_dtype=BF, name=tag + "_bwd_dwgu")
    du = mm(dgu, wgu, tb=True, name=tag + "_bwd_du", token=on_dwgu(dwgu) if on_dwgu else None)

    def b_pre(dh, duv, hv, pg):
        dx, dpre = jax.vjp(_rms, hv, pg)[1](duv)
        return dh + dx, dpre

    dh, dpre = rowcall(b_pre, [dh_out, du, h], [pre], [(D_MODEL, F32)], [(1, D_MODEL)], rows=512, total=s,
                       name=tag + "_bwd_pre")
    return dh, dwgu, dwd, dpre, dpost


def _residual_rms(h, a, g):
    return h + _rms(a, g)


def _bwd_residual(dh, a, g):
    return jax.vjp(_rms, a, g)[1](dh)


def _step(a):
    x, mem = a['x'][0], a['mem'][0]
    s = x.shape[0]
    me = 4 * lax.axis_index("x") + 2 * lax.axis_index("y") + lax.axis_index("c")
    w2 = {nm: a[nm][0] for nm in WEIGHTS}
    m2 = {nm: a['m_' + nm][0] for nm in WEIGHTS}
    v2 = {nm: a['v_' + nm][0] for nm in WEIGHTS}
    small = {nm: w2[nm][None] for nm in WEIGHTS if nm not in BIG and nm != 'gdn_conv_w'}

    def ff_cols(w):
        return _pad_to(w, (D_MODEL, FF_SHARD_PAD)).astype(BF)

    def ff_rows(w):
        return _pad_to(w, (FF_SHARD_PAD, D_MODEL)).astype(BF)

    whole = (None, 0, 0, 0)
    conv_pad = 256
    g_in = [ff_cols(w2['ffn1_w_gate']), ff_cols(w2['ffn1_w_up']), ff_rows(w2['ffn1_w_down']),
            ff_cols(w2['ffn2_w_gate']), ff_cols(w2['ffn2_w_up']), ff_rows(w2['ffn2_w_down']),
            _pad_to(w2['w_in'], (D_MODEL, IN_SHARD_PAD)).astype(BF), w2['w_out'].astype(BF),
            w2['mem_w_q'].astype(BF), w2['mem_w_kv'].astype(BF), w2['mem_w_o'].astype(BF),
            _pad_to(w2['gdn_conv_w'], (8, conv_pad))]
    g_out = [jax.ShapeDtypeStruct((D_MODEL, 2 * D_FF_PAD), BF), jax.ShapeDtypeStruct((D_FF_PAD, D_MODEL), BF),
             jax.ShapeDtypeStruct((D_MODEL, 2 * D_FF_PAD), BF), jax.ShapeDtypeStruct((D_FF_PAD, D_MODEL), BF),
             jax.ShapeDtypeStruct((D_MODEL, N_DEV * IN_SHARD_PAD), BF), jax.ShapeDtypeStruct((D_MODEL, D_MODEL), BF),
             jax.ShapeDtypeStruct((D_MODEL, D_MODEL), BF), jax.ShapeDtypeStruct((D_MODEL, 2 * D_MODEL), BF),
             jax.ShapeDtypeStruct((D_MODEL, D_MODEL), BF), jax.ShapeDtypeStruct((8, N_DEV * conv_pad), F32)]
    sp_, dm = FF_SHARD_PAD, D_MODEL // N_DEV
    g_tr = [(0, whole, 0, (1, 0, sp_, sp_)), (1, whole, 0, (1, D_FF_PAD, sp_, sp_)), (2, whole, 1, (0, 0, sp_, sp_)),
            (3, whole, 2, (1, 0, sp_, sp_)), (4, whole, 2, (1, D_FF_PAD, sp_, sp_)), (5, whole, 3, (0, 0, sp_, sp_)),
            (6, whole, 4, (1, 0, IN_SHARD_PAD, IN_SHARD_PAD)), (7, whole, 5, (0, 0, dm, dm)),
            (8, whole, 6, (0, 0, dm, dm)), (9, whole, 7, (1, 0, 2 * dm, 2 * dm)), (10, whole, 8, (0, 0, dm, dm)),
            (11, whole, 9, (1, 0, conv_pad, conv_pad))]
    def pick(idx):
        ins = sorted({g_tr[k][0] for k in idx})
        outs = sorted({g_tr[k][2] for k in idx})
        tr = [(ins.index(g_tr[k][0]), g_tr[k][1], outs.index(g_tr[k][2]), g_tr[k][3]) for k in idx]
        return [g_in[i] for i in ins], [g_out[o] for o in outs], tr

    gu_in, gu_out, gu_tr = pick([0, 1])
    wgu1, = exchange(gu_in, gu_out, gu_tr, "gather_gate_up")
    stages, after = [], wgu1
    for nm, idx in (("down", [2]), ("mix", [6, 7, 11]), ("late", [8, 9, 10, 3, 4, 5])):
        st_in, st_out, st_tr = pick(idx)
        st = exchange_start(st_in, place_own(st_in, st_out, st_tr), st_tr, after, "gather_%s_start" % nm)
        stages.append((st, st_tr, "gather_%s_wait" % nm))
        after = st[4]
    g_token = after

    def gather_wait(k, after_):
        (send_, recv_, src_, land_, _), tr_, nm_ = stages[k]
        return exchange_wait(send_, recv_, src_, land_, after_, tr_, nm_)

    bias_row = _pad_to(small['fox_f_bias'], (1, 128))
    gate_prm = _pad_to(jnp.concatenate([_pad_to(small['gdn_a_log'], (1, 128 - SMALL_A)),
                                        _pad_to(small['gdn_dt_bias'], (1, 128 - SMALL_A))], axis=0),
                       (8, 128 - SMALL_A))
    gate_prm = jnp.pad(gate_prm, ((0, 0), (SMALL_A, 0)))
    onorm = small['gdn_out_norm']

    late = {}

    def wd1_when(act):
        late['wd1'], = gather_wait(0, act)
        return late['wd1']

    sv1 = _ffn_fwd(x, small['ffn1_pre_norm'] + g_token[0, 0], wgu1, wd1_when, "ffn1")
    wd1 = late['wd1']
    h1, = rowcall(lambda h, f, g: h + _half_rms(f, g), [x, sv1[3]], [small['ffn1_post_norm']], [(D_MODEL, F32)],
                  rows=512, total=s, name="ffn1_out")
    w_in_g, w_out, conv_g = gather_wait(1, h1)
    w_in = jnp.concatenate([w_in_g[:, j * IN_SHARD_PAD:j * IN_SHARD_PAD + IN_SHARD] for j in range(N_DEV)],
                           axis=1)
    sp = [0, 512, 1024, 1536, 1544, 2056, 2568, 3080, 3592, 3596, 3600]
    fq, fk, fv, ff, gq, gk, gv, gz, gb, ga = [w_in[:, sp[i]:sp[i + 1]] for i in range(10)]
    w_proj = jnp.concatenate([fq, fk, fv, gq, gk, gv, gz, ff, gb, ga,
                              jnp.zeros((D_MODEL, PROJ_W - 3584 - 16), BF)], axis=1)
    conv_w8 = conv_g.reshape(8, N_DEV, conv_pad)[:, :, :CONV_CH // N_DEV].reshape(8, CONV_CH)


    u2, = rowcall(_rms, [h1], [small['mix_pre_norm']], [(D_MODEL, BF)], rows=512, total=s, name="mix_pre")
    proj = mm(u2, w_proj, name="mix_proj")
    f_cum = fox_f_fwd(proj, bias_row)
    f_heads = f_cum[:, :FOX_HEADS]
    qkv_bf = proj[:, :3 * FOX_HEADS * FOX_DH].astype(BF)
    xk, xv = _fox_extras(s, 1.0, -f_heads), _fox_extras(s, 1.0, None)
    fox_flat, lse = fox_fwd(qkv_bf, _fox_extras(s, f_heads, 1.0), xk, xv)
    lse_heads = lse[:, :, :2].transpose(1, 0, 2).reshape(s, FOX_HEADS)
    cqkv = conv_fwd(proj, conv_w8)
    g_l, b_l = rowcall(_gdn_gates, [(proj, 128, SMALL_BLOCK128)], [gate_prm], [(512, F32), (512, F32)],
                       rows=512, total=s, name="gdn_gates")
    gbb = jnp.concatenate([g_l, b_l], axis=1)
    gdn_o, states = gdn_fwd(cqkv, proj, gbb, onorm)
    mixed = jnp.concatenate([fox_flat, gdn_o], axis=1).astype(BF)
    mo = mm(mixed, w_out, name="mix_out")
    h2, = rowcall(_residual_rms, [h1, mo], [small['mix_post_norm']], [(D_MODEL, F32)], rows=512, total=s,
                  name="mix_res")

    hq, = rowcall(_rms, [h2], [small['mem_pre_norm']], [(D_MODEL, BF)], rows=512, total=s, name="mem_pre")
    mn, = rowcall(_rms, [mem], [small['mem_kv_norm']], [(D_MODEL, BF)], rows=256, total=mem.shape[0], name="mem_kvn")
    wgu2, wd2, w_q, w_kv, w_o = gather_wait(2, h2)
    q_mem = mm(hq, w_q, name="mem_q")
    kv_mem = mm(mn, w_kv, name="mem_kv")
    o_mem = xattn_fwd(q_mem, kv_mem)
    c_mem = mm(o_mem, w_o, name="mem_o")
    h3, = rowcall(_residual_rms, [h2, c_mem], [small['mem_post_norm']], [(D_MODEL, F32)], rows=512, total=s,
                  name="mem_res")

    sv2 = _ffn_fwd(h3, small['ffn2_pre_norm'], wgu2, wd2, "ffn2")

    def b_loss(h, f, tgt, g):
        err = h + _half_rms(f, g) - tgt
        part = 0.5 * jnp.sum(jnp.mean(err * err, axis=-1, keepdims=True), axis=0, keepdims=True)
        return err * (1.0 / D_MODEL), jnp.broadcast_to(part, (1, 128))

    dy, loss_acc = rowcall(b_loss, [h3, sv2[3], a['loss_target'][0]], [small['ffn2_post_norm']], [(D_MODEL, F32)],
                           [(1, 128)], rows=512, total=s, name="loss")

    grads = {}
    dh3, dwgu2, dwd2, grads['ffn2_pre_norm'], grads['ffn2_post_norm'] = _ffn_bwd(
        dy, h3, small['ffn2_pre_norm'], small['ffn2_post_norm'], wgu2, wd2, sv2, "ffn2")

    lead = ("lead", 0, 1, 0)

    def land(r, c, dt=BF):
        return jax.ShapeDtypeStruct((N_DEV, r, c), dt)

    ffn_tr = [(0, (1, 0, sp_, sp_), 0, lead), (0, (1, D_FF_PAD, sp_, sp_), 1, lead), (1, (0, 0, sp_, FF_SHARD), 2, lead)]
    ffn_land = [land(D_MODEL, sp_), land(D_MODEL, sp_), land(FF_SHARD, D_MODEL)]
    a_land = place_own([dwgu2, dwd2], ffn_land, ffn_tr)
    a_send, a_recv, a_src, a_land, a_token = exchange_start([dwgu2, dwd2], a_land, ffn_tr, dh3, "reduce_ffn2_start")

    dc, grads['mem_post_norm'] = rowcall(_bwd_residual, [dh3, c_mem], [small['mem_post_norm'] + a_token[0, 0]],
                                         [(D_MODEL, BF)],
                                         [(1, D_MODEL)], rows=512, total=s, name="mem_bwd_res")
    d_o = mm(dc, w_o, tb=True, name="mem_bwd_do")
    dw_o = mm(o_mem, dc, ta=True, out_dtype=BF, name="mem_bwd_dwo")
    dq_mem, dkv = xattn_bwd(q_mem, kv_mem, d_o)
    dhq = mm(dq_mem, w_q, tb=True, name="mem_bwd_dhq")
    dw_q = mm(hq, dq_mem, ta=True, out_dtype=BF, name="mem_bwd_dwq")
    dmn = mm(dkv, w_kv, tb=True, name="mem_bwd_dmn")
    dw_kv = mm(mn, dkv, ta=True, out_dtype=BF, name="mem_bwd_dwkv")
    _, grads['mem_kv_norm'] = rowcall(lambda d, mv, g: jax.vjp(_rms, mv, g)[1](d), [dmn, mem],
                                      [small['mem_kv_norm']], [(D_MODEL, F32)], [(1, D_MODEL)], rows=256,
                                      total=mem.shape[0], name="mem_bwd_kvn")

    def b_pre(dh, duv, hv, pg):
        dx, dpre = jax.vjp(_rms, hv, pg)[1](duv)
        return dh + dx, dpre

    dh2, grads['mem_pre_norm'] = rowcall(b_pre, [dh3, dhq, h2], [small['mem_pre_norm']], [(D_MODEL, F32)],
                                         [(1, D_MODEL)], rows=512, total=s, name="mem_bwd_pre")

    dmo, grads['mix_post_norm'] = rowcall(_bwd_residual, [dh2, mo], [small['mix_post_norm']], [(D_MODEL, BF)],
                                          [(1, D_MODEL)], rows=512, total=s, name="mix_bwd_res")
    d_mixed = mm(dmo, w_out, tb=True, name="mix_bwd_dmixed")
    dw_out = mm(mixed, dmo, ta=True, out_dtype=BF, name="mix_bwd_dwout")
    def b_delta(do, o):
        sel = (_iota2((512, 128), 0) // FOX_DH == _iota2((512, 128), 1)).astype(F32)
        return hdot(do * o, sel)

    delta, = rowcall(b_delta, [(d_mixed, 512, 0), fox_flat], [], [(128, F32)], rows=512, total=s, name="fox_delta")
    dfox_q, dfox_k, dvf, sum_q, sum_k = fox_bwd(qkv_bf, d_mixed[:, :512].astype(BF), xk, xv,
                                                _fox_extras(s, f_heads - lse_heads, 1.0),
                                                _fox_extras(s, -delta[:, :FOX_HEADS], None))
    per_head = lambda a: jnp.stack([a[:, 0::2 * FOX_DH], a[:, 1::2 * FOX_DH]], axis=-1).reshape(s, FOX_HEADS)
    d_f = _pad_to(per_head(sum_q) - per_head(sum_k), (s, 128))
    dsmall_f, dbias = fox_f_bwd(proj, bias_row, d_f)
    grads['fox_f_bias'] = dbias[:, :FOX_HEADS]
    dcqkv, dz, dgb, grads['gdn_out_norm'] = gdn_bwd(cqkv, proj, gbb, onorm, states, d_mixed)

    def b_gates(sm, dsf, dg, db, prm):
        dsm, dprm = jax.vjp(_gdn_gates, sm, prm)[1]((dg, db))
        return dsm + dsf, dprm

    dsmall, dprm = rowcall(b_gates, [(proj, 128, SMALL_BLOCK128), dsmall_f, (dgb, 512, 0), (dgb, 512, 1)], [gate_prm],
                           [(128, F32)],
                           [(8, 128)], rows=512, total=s, name="gdn_bwd_gates")
    grads['gdn_a_log'] = dprm[0:1, SMALL_A:SMALL_A + GDN_HEADS]
    grads['gdn_dt_bias'] = dprm[1:2, SMALL_A:SMALL_A + GDN_HEADS]
    dqkv_pre, dconv8 = conv_bwd(proj, conv_w8, dcqkv)
    dproj = jnp.concatenate([dfox_q, dfox_k, dvf, dqkv_pre, dz, dsmall,
                             jnp.zeros((s, PROJ_W - 3584 - 128), F32)], axis=1).astype(BF)
    du2 = mm(dproj, w_proj, tb=True, name="mix_bwd_du")
    dw_proj = mm(u2, dproj, ta=True, out_dtype=BF, name="mix_bwd_dwproj")
    dh1, grads['mix_pre_norm'] = rowcall(b_pre, [dh2, du2, h1], [small['mix_pre_norm']], [(D_MODEL, F32)],
                                         [(1, D_MODEL)], rows=512, total=s, name="mix_bwd_pre")

    dw_in = jnp.concatenate([dw_proj[:, :1536], dw_proj[:, 3584:3592], dw_proj[:, 1536:3584],
                             dw_proj[:, 3592:3600]], axis=1)
    gap = jnp.zeros((D_MODEL, IN_SHARD_PAD - IN_SHARD), BF)
    dw_in = jnp.concatenate([piece for j in range(N_DEV) for piece in (dw_in[:, j * IN_SHARD:(j + 1) * IN_SHARD], gap)],
                            axis=1)
    b_in = [dw_in, dw_out, dw_q, dw_kv, dw_o]
    b_tr = [(0, (1, 0, IN_SHARD_PAD, IN_SHARD_PAD), 0, lead), (1, (0, 0, dm, dm), 1, lead), (2, (0, 0, dm, dm), 2, lead),
            (3, (1, 0, 2 * dm, 2 * dm), 3, lead), (4, (0, 0, dm, dm), 4, lead)]
    b_shapes = [land(D_MODEL, IN_SHARD_PAD), land(dm, D_MODEL), land(dm, D_MODEL), land(D_MODEL, 2 * dm),
                land(dm, D_MODEL)]
    b_land = place_own(b_in, b_shapes, b_tr)
    b_send, b_recv, b_src, b_land, b_token = exchange_start(b_in, b_land, b_tr, dh1, "reduce_mix_start")

    def start_down_reduce(dwd):
        tr = ffn_tr[2:]
        tr = [(0, tr[0][1], 0, tr[0][3])]
        late['c_down'] = (exchange_start([dwd], place_own([dwd], ffn_land[2:], tr), tr, dwd, "reduce_ffn1_down_start"), tr)
        return late['c_down'][0][4]

    def start_gate_up_reduce(dwgu):
        tr = ffn_tr[:2]
        late['c_gu'] = (exchange_start([dwgu], place_own([dwgu], ffn_land[:2], tr), tr, dwgu, "reduce_ffn1_gu_start"), tr)
        return late['c_gu'][0][4]

    grad_x, dwgu1, dwd1, grads['ffn1_pre_norm'], grads['ffn1_post_norm'] = _ffn_bwd(
        dh1, x, small['ffn1_pre_norm'], small['ffn1_post_norm'] + b_token[0, 0], wgu1, wd1, sv1, "ffn1",
        on_dwd=start_down_reduce, on_dwgu=start_gate_up_reduce)

    gpack = _pack(lambda nm: grads[nm], conv=dconv8[:CONV_W], loss=loss_acc[:, :1])
    gsum_parts, = exchange([gpack], [land(PACK_ROWS, D_MODEL, F32)], [(0, whole, 0, lead)], "reduce_small")
    a_got = exchange_wait(a_send, a_recv, a_src, a_land, gsum_parts, ffn_tr, "reduce_ffn2_wait")
    b_got = exchange_wait(b_send, b_recv, b_src, b_land, gsum_parts, b_tr, "reduce_mix_wait")
    recv = dict(zip(['ffn2_w_gate', 'ffn2_w_up', 'ffn2_w_down', 'w_in', 'w_out', 'mem_w_q', 'mem_w_kv', 'mem_w_o'],
                    a_got + b_got))

    out_g, out_d, out_m, out_v = {}, {}, {}, {}

    def update(nm):
        r = recv[nm]
        res = adamw(w2[nm], m2[nm], v2[nm], [(r, r.shape[2], 0, d) for d in range(N_DEV)], "adamw_" + nm)
        out_g[nm], out_d[nm], out_m[nm], out_v[nm] = res

    for nm in recv:
        update(nm)
    wp = _pack(lambda nm: small[nm])
    mp = _pack(lambda nm: m2[nm][None])
    vp = _pack(lambda nm: v2[nm][None])
    pg, pd, pm, pv = adamw(wp, mp, vp, [(gsum_parts, D_MODEL, 0, d) for d in range(N_DEV)], "adamw_small")
    for dst, p in ((out_g, pg), (out_d, pd), (out_m, pm), (out_v, pv)):
        dst.update({k: val[0] for k, val in _unpack(p).items()})
    loss = pg[ROW_MISC, COL_LOSS]
    conv_g = lax.dynamic_slice_in_dim(pg[ROW_CONV:ROW_CONV + 6].reshape(CONV_W, CONV_CH), me * (CONV_CH // N_DEV),
                                      CONV_CH // N_DEV, axis=1)
    res = adamw(w2['gdn_conv_w'], m2['gdn_conv_w'], v2['gdn_conv_w'], [conv_g], "adamw_conv")
    out_g['gdn_conv_w'], out_d['gdn_conv_w'], out_m['gdn_conv_w'], out_v['gdn_conv_w'] = res

    done = sum(out_d[nm][0, 0] for nm in recv) + out_d['gdn_conv_w'][0, 0] + pd[0, 0]
    after = jnp.zeros((8, 128), F32) + done
    c_got = []
    for key, nm in (('c_gu', "reduce_ffn1_gu_wait"), ('c_down', "reduce_ffn1_down_wait")):
        (c_send, c_recv, c_src, c_land, _), tr = late[key]
        c_got += exchange_wait(c_send, c_recv, c_src, c_land, after, tr, nm)
    recv = dict(zip(['ffn1_w_gate', 'ffn1_w_up', 'ffn1_w_down'], c_got))
    for nm in recv:
        update(nm)

    def depth(t):
        return t[None]

    return (loss, grad_x[None], *[depth(out_g[nm]) for nm in WEIGHTS], *[depth(out_d[nm]) for nm in WEIGHTS],
            *[depth(out_m[nm]) for nm in WEIGHTS], *[depth(out_v[nm]) for nm in WEIGHTS])


def kernel(x, mem, ffn1_pre_norm, ffn1_w_gate, ffn1_w_up, ffn1_w_down, ffn1_post_norm, mix_pre_norm, w_in, fox_f_bias, gdn_conv_w, gdn_a_log, gdn_dt_bias, gdn_out_norm, w_out, mix_post_norm, mem_pre_norm, mem_kv_norm, mem_w_q, mem_w_kv, mem_w_o, mem_post_norm, ffn2_pre_norm, ffn2_w_gate, ffn2_w_up, ffn2_w_down, ffn2_post_norm, loss_target, m_ffn1_pre_norm, m_ffn1_w_gate, m_ffn1_w_up, m_ffn1_w_down, m_ffn1_post_norm, m_mix_pre_norm, m_w_in, m_fox_f_bias, m_gdn_conv_w, m_gdn_a_log, m_gdn_dt_bias, m_gdn_out_norm, m_w_out, m_mix_post_norm, m_mem_pre_norm, m_mem_kv_norm, m_mem_w_q, m_mem_w_kv, m_mem_w_o, m_mem_post_norm, m_ffn2_pre_norm, m_ffn2_w_gate, m_ffn2_w_up, m_ffn2_w_down, m_ffn2_post_norm, v_ffn1_pre_norm, v_ffn1_w_gate, v_ffn1_w_up, v_ffn1_w_down, v_ffn1_post_norm, v_mix_pre_norm, v_w_in, v_fox_f_bias, v_gdn_conv_w, v_gdn_a_log, v_gdn_dt_bias, v_gdn_out_norm, v_w_out, v_mix_post_norm, v_mem_pre_norm, v_mem_kv_norm, v_mem_w_q, v_mem_w_kv, v_mem_w_o, v_mem_post_norm, v_ffn2_pre_norm, v_ffn2_w_gate, v_ffn2_w_up, v_ffn2_w_down, v_ffn2_post_norm):
    return _step(dict(locals()))
```

```python
import functools

import jax
import jax.numpy as jnp
from jax import lax
from jax.experimental import pallas as pl
from jax.experimental.pallas import tpu as pltpu

F32 = jnp.float32
BF = jnp.bfloat16
HI = lax.Precision.HIGHEST
MESH = pl.DeviceIdType.MESH

N_DEV = 8
EPS = 1e-6
D_MODEL = 1024
D_FF = 2816
FF_SHARD = D_FF // N_DEV
FF_SHARD_PAD = 384
D_FF_PAD = FF_SHARD_PAD * N_DEV
FOX_HEADS, FOX_DH = 8, 64
GDN_HEADS, GDN_DH = 4, 128
GDN_CHUNK = 64
CONV_W = 4
MEM_HEADS, MEM_DH = 4, 256
IN_W = 3600
IN_SHARD = IN_W // N_DEV
IN_SHARD_PAD = 512
PROJ_W = 4096
SMALL_F, SMALL_B, SMALL_A = 0, 8, 12

ADAM_LR, ADAM_B1, ADAM_B2, ADAM_EPS, ADAM_WD, ADAM_STEP = 0.001, 0.9, 0.999, 1e-08, 0.01, 10

VMEM_LIMIT = 56 * 1024 * 1024


def _params(sem=None):
    return pltpu.CompilerParams(dimension_semantics=sem, vmem_limit_bytes=VMEM_LIMIT)


def _tile(n, pref, unit=128):
    if n <= pref:
        return n
    t = (pref // unit) * unit
    while t > unit and n % t:
        t -= unit
    assert n % t == 0, (n, pref)
    return t


@functools.partial(jax.custom_vjp, nondiff_argnums=(2, 3))
def bdot(a, b, ca, cb):
    return lax.dot_general(a.astype(BF), b.astype(BF), (((ca,), (cb,)), ((), ())), preferred_element_type=F32)


def _bdot_fwd(a, b, ca, cb):
    return bdot(a, b, ca, cb), (a, b)


def _bdot_bwd(ca, cb, res, g):
    a, b = res
    da = bdot(g, b, 1, 1 - cb) if ca == 1 else bdot(b, g, 1 - cb, 1)
    db = bdot(a, g, 1 - ca, 0) if cb == 0 else bdot(g, a, 0, 1 - ca)
    return da, db


bdot.defvjp(_bdot_fwd, _bdot_bwd)


def hdot(a, b):
    return jnp.dot(a, b, precision=HI, preferred_element_type=F32)


def mdot(a, b):
    return jnp.dot(a, b, precision=lax.Precision.HIGH, preferred_element_type=F32)


def _iota2(shape, dim):
    return lax.broadcasted_iota(jnp.int32, shape, dim)


def _sigmoid(x):
    return 1.0 / (1.0 + jnp.exp(-x))


def _silu(x):
    return x * _sigmoid(x)


def _softplus(x):
    return jnp.maximum(x, 0.0) + jnp.log(1.0 + jnp.exp(-jnp.abs(x)))


def _rms(x, gain):
    return x * lax.rsqrt(jnp.mean(x * x, axis=-1, keepdims=True) + EPS) * gain


def mm(a, b, *, name, ta=False, tb=False, out_dtype=F32, tm=1024, tn=1024, tk=1024, token=None):
    m, k = (a.shape[1], a.shape[0]) if ta else a.shape
    n = b.shape[0] if tb else b.shape[1]
    assert k == (b.shape[1] if tb else b.shape[0]), (a.shape, b.shape, ta, tb)
    tm, tn, tk = _tile(m, tm), _tile(n, tn), _tile(k, tk)
    nk = k // tk
    dims = (((0 if ta else 1,), (1 if tb else 0,)), ((), ()))

    def kern(a_ref, b_ref, *rest):
        o_ref, scratch = (rest[1], rest[2:]) if token is not None else (rest[0], rest[1:])

        def part():
            return lax.dot_general(a_ref[...].astype(BF), b_ref[...].astype(BF), dims, preferred_element_type=F32)

        if nk == 1:
            o_ref[...] = part().astype(o_ref.dtype)
            return
        acc_ref, = scratch
        kk = pl.program_id(2)

        @pl.when(kk == 0)
        def _():
            acc_ref[...] = part()

        @pl.when(kk > 0)
        def _():
            acc_ref[...] += part()

        @pl.when(kk == nk - 1)
        def _():
            o_ref[...] = acc_ref[...].astype(o_ref.dtype)

    a_spec = pl.BlockSpec((tk, tm), lambda i, j, kk: (kk, i)) if ta else pl.BlockSpec((tm, tk), lambda i, j, kk: (i, kk))
    b_spec = pl.BlockSpec((tn, tk), lambda i, j, kk: (j, kk)) if tb else pl.BlockSpec((tk, tn), lambda i, j, kk: (kk, j))
    return pl.pallas_call(
        kern, name=name, grid=(m // tm, n // tn, nk),
        in_specs=[a_spec, b_spec] + ([pl.BlockSpec((8, 128), lambda i, j, kk: (0, 0))] if token is not None else []),
        out_specs=pl.BlockSpec((tm, tn), lambda i, j, kk: (i, j)),
        out_shape=jax.ShapeDtypeStruct((m, n), out_dtype),
        scratch_shapes=[pltpu.VMEM((tm, tn), F32)] if nk > 1 else [],
        compiler_params=_params(("parallel", "parallel", "arbitrary")),
    )(*((a, b) if token is None else (a, b, token)))


def mm_swiglu(a, wgu, *, name):
    m, k = a.shape
    nh = wgu.shape[1] // 2
    tm, tn = _tile(m, 1024), _tile(nh, 512)
    nj = nh // tn

    def kern(a_ref, bg_ref, bu_ref, g_ref, u_ref, act_ref):
        av = a_ref[...]
        g = jnp.dot(av, bg_ref[...], preferred_element_type=F32).astype(BF)
        u = jnp.dot(av, bu_ref[...], preferred_element_type=F32).astype(BF)
        g_ref[...] = g
        u_ref[...] = u
        act_ref[...] = (_silu(g.astype(F32)) * u.astype(F32)).astype(BF)

    tile = pl.BlockSpec((tm, tn), lambda i, j: (i, j))
    out = jax.ShapeDtypeStruct((m, nh), BF)
    return pl.pallas_call(
        kern, name=name, grid=(m // tm, nj),
        in_specs=[pl.BlockSpec((tm, k), lambda i, j: (i, 0)), pl.BlockSpec((k, tn), lambda i, j: (0, j)),
                  pl.BlockSpec((k, tn), lambda i, j: (0, j + nj))],
        out_specs=[tile, tile, tile], out_shape=[out, out, out],
        compiler_params=_params(("parallel", "parallel")),
    )(a, wgu, wgu)


def mm_dswiglu(df, wd, gate, up, *, name, token=None):
    m, k = df.shape
    nh = wd.shape[0]
    tm, tn = _tile(m, 1024), _tile(nh, 512)

    def kern(df_ref, wd_ref, g_ref, u_ref, *rest):
        dg_ref, du_ref = rest[-2:]
        da = lax.dot_general(df_ref[...], wd_ref[...], (((1,), (1,)), ((), ())), preferred_element_type=F32)
        g, u = g_ref[...].astype(F32), u_ref[...].astype(F32)
        sg = _sigmoid(g)
        dg_ref[...] = (da * u * (sg * (1.0 + g * (1.0 - sg)))).astype(BF)
        du_ref[...] = (da * (g * sg)).astype(BF)

    tile = pl.BlockSpec((tm, tn), lambda i, j: (i, j))
    out = jax.ShapeDtypeStruct((m, nh), BF)
    extra = [pl.BlockSpec((8, 128), lambda i, j: (0, 0))] if token is not None else []
    return pl.pallas_call(
        kern, name=name, grid=(m // tm, nh // tn),
        in_specs=[pl.BlockSpec((tm, k), lambda i, j: (i, 0)), pl.BlockSpec((tn, k), lambda i, j: (j, 0)), tile, tile]
        + extra,
        out_specs=[tile, tile], out_shape=[out, out],
        compiler_params=_params(("parallel", "parallel")),
    )(*((df, wd, gate, up) if token is None else (df, wd, gate, up, token)))


def mm_pair(a1, a2, wgu, *, name, token=None):
    m, nh = a1.shape
    n = wgu.shape[0]
    tm, tn, tk = _tile(m, 1024), _tile(n, 1024), _tile(nh, 1024)
    nk = nh // tk
    nt = (((1,), (1,)), ((), ()))

    def kern(a1_ref, a2_ref, b1_ref, b2_ref, *rest):
        o_ref, acc_ref = rest[-2:]
        kk = pl.program_id(2)

        def part():
            return (lax.dot_general(a1_ref[...], b1_ref[...], nt, preferred_element_type=F32)
                    + lax.dot_general(a2_ref[...], b2_ref[...], nt, preferred_element_type=F32))

        @pl.when(kk == 0)
        def _():
            acc_ref[...] = part()

        @pl.when(kk > 0)
        def _():
            acc_ref[...] += part()

        @pl.when(kk == nk - 1)
        def _():
            o_ref[...] = acc_ref[...]

    a_spec = pl.BlockSpec((tm, tk), lambda i, j, kk: (i, kk))
    extra = [pl.BlockSpec((8, 128), lambda i, j, kk: (0, 0))] if token is not None else []
    return pl.pallas_call(
        kern, name=name, grid=(m // tm, n // tn, nk),
        in_specs=[a_spec, a_spec, pl.BlockSpec((tn, tk), lambda i, j, kk: (j, kk)),
                  pl.BlockSpec((tn, tk), lambda i, j, kk: (j, kk + nk))] + extra,
        out_specs=pl.BlockSpec((tm, tn), lambda i, j, kk: (i, j)),
        out_shape=jax.ShapeDtypeStruct((m, n), F32),
        scratch_shapes=[pltpu.VMEM((tm, tn), F32)],
        compiler_params=_params(("parallel", "parallel", "arbitrary")),
    )(*((a1, a2, wgu, wgu) if token is None else (a1, a2, wgu, wgu, token)))


def _row_spec(item, rows):
    if not isinstance(item, tuple):
        return item, pl.BlockSpec((rows, item.shape[1]), lambda i: (i, 0))
    if len(item) == 3:
        arr, w, c = item
        return arr, pl.BlockSpec((rows, w), lambda i: (i, c))
    arr, w, c, lead = item
    return arr, pl.BlockSpec((None, rows, w), lambda i: (lead, i, c))


def _whole_spec(item):
    if not isinstance(item, tuple):
        return item, pl.BlockSpec(item.shape, lambda i: (0,) * item.ndim)
    arr, w, c = item
    return arr, pl.BlockSpec((arr.shape[0], w), lambda i: (0, c))


def rowcall(body, tiled, whole, outs, accs=(), *, rows, total, name):
    rows = min(rows, total)
    assert total % rows == 0
    t_arr, t_spec = zip(*[_row_spec(t, rows) for t in tiled])
    w_arr, w_spec = zip(*[_whole_spec(w) for w in whole]) if whole else ((), ())
    nt, nw, no, na = len(t_arr), len(w_arr), len(outs), len(accs)

    def kern(*refs):
        vals = [r[...] for r in refs[:nt + nw]]
        res = body(*vals)
        if not isinstance(res, (tuple, list)):
            res = (res,)
        assert len(res) == no + na, (name, len(res), no, na)
        for r, v in zip(refs[nt + nw:nt + nw + no], res[:no]):
            r[...] = v.astype(r.dtype)
        if na:
            acc_refs = refs[nt + nw + no:]

            @pl.when(pl.program_id(0) == 0)
            def _():
                for r in acc_refs:
                    r[...] = jnp.zeros_like(r)

            for r, v in zip(acc_refs, res[no:]):
                r[...] += v

    out_shape = [jax.ShapeDtypeStruct((total, w), d) for w, d in outs] + [jax.ShapeDtypeStruct(s, F32) for s in accs]
    out_specs = [pl.BlockSpec((rows, w), lambda i: (i, 0)) for w, _ in outs] + \
                [pl.BlockSpec(s, lambda i: (0, 0)) for s in accs]
    res = pl.pallas_call(
        kern, name=name, grid=(total // rows,),
        in_specs=list(t_spec) + list(w_spec), out_specs=out_specs, out_shape=out_shape,
        compiler_params=_params(("arbitrary",) if na else ("parallel",)),
    )(*t_arr, *w_arr)
    return res


def _colsum(x):
    return jnp.sum(x, axis=0, keepdims=True)


def _gdn_chunk(q, k, v, z, gb, bb, state, gain):
    c = GDN_CHUNK
    nh = len(q)
    hs = range(nh)
    r64, c64 = _iota2((c, c), 0), _iota2((c, c), 1)
    incl = r64 >= c64
    strict = r64 > c64
    ltri = incl.astype(F32)
    utri = (r64 <= c64).astype(F32)
    eye = (r64 == c64).astype(F32)
    ones = jnp.ones((c, c), F32)
    pick = (_iota2((GDN_DH, c), 0) == _iota2((GDN_DH, c), 1)).astype(F32)
    last = (_iota2((c, GDN_DH), 0) == c - 1).astype(F32)

    qn = [q[h] * lax.rsqrt(jnp.sum(q[h] * q[h], axis=-1, keepdims=True) + EPS) * (GDN_DH ** -0.5) for h in hs]
    kn = [k[h] * lax.rsqrt(jnp.sum(k[h] * k[h], axis=-1, keepdims=True) + EPS) for h in hs]
    gc = [mdot(ltri, gb[h]) for h in hs]
    g64 = [mdot(gb[h], pick) for h in hs]
    gcol = [mdot(ltri, g64[h]) for h in hs]
    grow = [mdot(ones, g64[h] * utri) for h in hs]
    dec = [jnp.exp(jnp.where(incl, gcol[h] - grow[h], -1e30)) for h in hs]
    kb = [kn[h] * bb[h] for h in hs]
    vb = [v[h] * bb[h] for h in hs]
    kk = [bdot(kb[h], kn[h], 1, 1) for h in hs]
    p = [-jnp.where(strict, kk[h] * dec[h], 0.0) for h in hs]
    tinv = [eye + p[h] for h in hs]
    for level in range(5):
        dot = mdot if level < 2 else (lambda a, b: bdot(a, b, 1, 0))
        p = [dot(p[h], p[h]) for h in hs]
        tinv = [tinv[h] + dot(tinv[h], p[h]) for h in hs]
    egc = [jnp.exp(gc[h]) for h in hs]
    u = [mdot(tinv[h], vb[h]) for h in hs]
    w = [mdot(tinv[h], kb[h] * egc[h]) for h in hs]
    attn = [bdot(qn[h], kn[h], 1, 1) * dec[h] for h in hs]
    qd = [qn[h] * egc[h] for h in hs]
    gl = [jnp.sum(gc[h] * last, axis=0, keepdims=True) for h in hs]
    kt = [kn[h] * jnp.exp(gl[h] - gc[h]) for h in hs]
    ws = [bdot(w[h], state[h], 1, 0) for h in hs]
    qs = [bdot(qd[h], state[h], 1, 0) for h in hs]
    v_new = [u[h] - ws[h] for h in hs]
    av = [bdot(attn[h], v_new[h], 1, 0) for h in hs]
    kv = [bdot(kt[h], v_new[h], 0, 0) for h in hs]
    new_state = tuple(state[h] * jnp.exp(gl[h]) + kv[h] for h in hs)
    o = tuple(_rms(qs[h] + av[h], gain) * _silu(z[h]) for h in hs)
    return o, new_state


GDN_ROWS = 512
GDN_W = GDN_HEADS * GDN_DH


def gdn_fwd(cqkv, proj, gbb, gain):
    s = cqkv.shape[0]
    nb, cpb = s // GDN_ROWS, GDN_ROWS // GDN_CHUNK
    h4 = GDN_HEADS

    def kern(qkv_ref, z_ref, gb_ref, gain_ref, o_ref, st_ref, state):
        @pl.when(pl.program_id(0) == 0)
        def _():
            state[...] = jnp.zeros_like(state)

        gain_v = gain_ref[...]

        def step(ci, carry):
            sl = pl.ds(pl.multiple_of(ci * GDN_CHUNK, GDN_CHUNK), GDN_CHUNK)
            ins = []
            for h in range(h4):
                ln = lambda base, h=h: slice(base + h * GDN_DH, base + (h + 1) * GDN_DH)
                ins.append((qkv_ref[sl, ln(0)], qkv_ref[sl, ln(GDN_W)], qkv_ref[sl, ln(2 * GDN_W)], z_ref[sl, ln(0)],
                            gb_ref[sl, ln(0)], gb_ref[sl, ln(GDN_W)], state[h]))
            cols = [tuple(col) for col in zip(*ins)]
            o, new = _gdn_chunk(*cols[:7], gain_v)
            for h in range(h4):
                st_ref[h, ci] = ins[h][6]
                o_ref[sl, h * GDN_DH:(h + 1) * GDN_DH] = o[h]
                state[h] = new[h]
            return carry

        lax.fori_loop(0, cpb, step, 0)

    return pl.pallas_call(
        kern, name="gdn_fwd", grid=(nb,),
        in_specs=[pl.BlockSpec((GDN_ROWS, 3 * GDN_W), lambda i: (i, 0)),
                  pl.BlockSpec((GDN_ROWS, GDN_W), lambda i: (i, 6)),
                  pl.BlockSpec((GDN_ROWS, 2 * GDN_W), lambda i: (i, 0)),
                  pl.BlockSpec((1, GDN_DH), lambda i: (0, 0))],
        out_specs=[pl.BlockSpec((GDN_ROWS, GDN_W), lambda i: (i, 0)),
                   pl.BlockSpec((h4, cpb, GDN_DH, GDN_DH), lambda i: (0, i, 0, 0))],
        out_shape=[jax.ShapeDtypeStruct((s, GDN_W), F32),
                   jax.ShapeDtypeStruct((h4, s // GDN_CHUNK, GDN_DH, GDN_DH), F32)],
        scratch_shapes=[pltpu.VMEM((h4, GDN_DH, GDN_DH), F32)],
        compiler_params=_params(("arbitrary",)),
    )(cqkv, proj, gbb, gain)


def gdn_bwd(cqkv, proj, gbb, gain, states, d_mixed):
    s = cqkv.shape[0]
    nb, cpb = s // GDN_ROWS, GDN_ROWS // GDN_CHUNK
    h4 = GDN_HEADS

    def kern(qkv_ref, z_ref, gb_ref, gain_ref, st_ref, do_ref, dqkv_ref, dz_ref, dgb_ref, dgain_ref, dstate):
        @pl.when(pl.program_id(0) == 0)
        def _():
            dgain_ref[...] = jnp.zeros_like(dgain_ref)
            dstate[...] = jnp.zeros_like(dstate)

        gain_v = gain_ref[...]

        def step(t, carry):
            ci = cpb - 1 - t
            sl = pl.ds(pl.multiple_of(ci * GDN_CHUNK, GDN_CHUNK), GDN_CHUNK)
            prim, cot, dst_in = [], [], []
            for h in range(h4):
                ln = lambda base, h=h: slice(base + h * GDN_DH, base + (h + 1) * GDN_DH)
                prim.append((qkv_ref[sl, ln(0)], qkv_ref[sl, ln(GDN_W)], qkv_ref[sl, ln(2 * GDN_W)], z_ref[sl, ln(0)],
                             gb_ref[sl, ln(0)], gb_ref[sl, ln(GDN_W)], st_ref[h, ci]))
                cot.append(do_ref[sl, ln(0)])
                dst_in.append(dstate[h])
            cols = [tuple(col) for col in zip(*prim)]
            vjp = jax.vjp(_gdn_chunk, *cols, gain_v)[1]
            dq, dk, dv, dz, dg, db, dst, dgn = vjp((tuple(cot), tuple(dst_in)))
            for h in range(h4):
                ln = lambda base, h=h: slice(base + h * GDN_DH, base + (h + 1) * GDN_DH)
                dqkv_ref[sl, ln(0)] = dq[h]
                dqkv_ref[sl, ln(GDN_W)] = dk[h]
                dqkv_ref[sl, ln(2 * GDN_W)] = dv[h]
                dz_ref[sl, ln(0)] = dz[h]
                dgb_ref[sl, ln(0)] = dg[h]
                dgb_ref[sl, ln(GDN_W)] = db[h]
                dstate[h] = dst[h]
            dgain_ref[...] += dgn
            return carry

        lax.fori_loop(0, cpb, step, 0)

    def rev(width, cblock=0):
        return pl.BlockSpec((GDN_ROWS, width), lambda i: (nb - 1 - i, cblock))

    return pl.pallas_call(
        kern, name="gdn_bwd", grid=(nb,),
        in_specs=[rev(3 * GDN_W), rev(GDN_W, 6), rev(2 * GDN_W), pl.BlockSpec((1, GDN_DH), lambda i: (0, 0)),
                  pl.BlockSpec((h4, cpb, GDN_DH, GDN_DH), lambda i: (0, nb - 1 - i, 0, 0)), rev(GDN_W, 1)],
        out_specs=[rev(3 * GDN_W), rev(GDN_W), rev(2 * GDN_W), pl.BlockSpec((1, GDN_DH), lambda i: (0, 0))],
        out_shape=[jax.ShapeDtypeStruct((s, 3 * GDN_W), F32), jax.ShapeDtypeStruct((s, GDN_W), F32),
                   jax.ShapeDtypeStruct((s, 2 * GDN_W), F32), jax.ShapeDtypeStruct((1, GDN_DH), F32)],
        scratch_shapes=[pltpu.VMEM((h4, GDN_DH, GDN_DH), F32)],
        compiler_params=_params(("arbitrary",)),
    )(cqkv, proj, gbb, gain, states, d_mixed)


def _gdn_gates(small, prm):
    w = GDN_HEADS * GDN_DH
    lane, head = _iota2((128, w), 0), _iota2((128, w), 1) // GDN_DH
    sel_b = (lane == SMALL_B + head).astype(F32)
    sel_a = (lane == SMALL_A + head).astype(F32)
    prow = _iota2((8, 128), 0)
    a_log = jnp.sum(prm * (prow == 0).astype(F32), axis=0, keepdims=True)
    dt_b = jnp.sum(prm * (prow == 1).astype(F32), axis=0, keepdims=True)
    beta = _sigmoid(hdot(small, sel_b))
    g = hdot(-jnp.exp(a_log) * _softplus(small + dt_b), sel_a)
    return g, beta


CONV_ROWS = 1024
CONV_COLS = 128
CONV_BLOCK0 = 1536 // CONV_COLS


def _shift_down(prev8, cur, s):
    ext = jnp.concatenate([prev8, cur], axis=0)
    return pltpu.roll(ext, s, 0)[8:]


def _shift_up(cur, next8, s):
    n = cur.shape[0]
    ext = jnp.concatenate([cur, next8], axis=0)
    return pltpu.roll(ext, n + 8 - s, 0)[:n]


def _conv_pre(x_ref, w, ci, nchunk):
    r0 = pl.multiple_of(ci * CONV_ROWS, CONV_ROWS)
    cur = x_ref[pl.ds(r0, CONV_ROWS), :]
    prev = x_ref[pl.ds(pl.multiple_of(jnp.maximum(r0 - 8, 0), 8), 8), :]
    prev = jnp.where(ci > 0, prev, 0.0)
    shifted = [cur] + [_shift_down(prev, cur, s) for s in range(1, CONV_W)]
    pre = w[CONV_W - 1:CONV_W, :] * cur
    for s in range(1, CONV_W):
        pre = pre + w[CONV_W - 1 - s:CONV_W - s, :] * shifted[s]
    return r0, pre, shifted


def conv_fwd(proj, conv_w8):
    s = proj.shape[0]
    nchunk = s // CONV_ROWS
    ncol = 3 * GDN_HEADS * GDN_DH // CONV_COLS

    def kern(x_ref, w_ref, y_ref):
        w = w_ref[...]

        def step(ci, carry):
            r0, pre, _ = _conv_pre(x_ref, w, ci, nchunk)
            y_ref[pl.ds(r0, CONV_ROWS), :] = _silu(pre)
            return carry

        lax.fori_loop(0, nchunk, step, 0)

    return pl.pallas_call(
        kern, name="conv_fwd", grid=(ncol,),
        in_specs=[pl.BlockSpec((s, CONV_COLS), lambda j: (0, CONV_BLOCK0 + j)),
                  pl.BlockSpec((8, CONV_COLS), lambda j: (0, j))],
        out_specs=pl.BlockSpec((s, CONV_COLS), lambda j: (0, j)),
        out_shape=jax.ShapeDtypeStruct((s, ncol * CONV_COLS), F32),
        compiler_params=_params(("parallel",)),
    )(proj, conv_w8)


def conv_bwd(proj, conv_w8, dy):
    s = proj.shape[0]
    nchunk = s // CONV_ROWS
    per = 3 * GDN_HEADS * GDN_DH // CONV_COLS
    outs = []
    for part in range(1):
        def kern(x_ref, w_ref, dy_ref, dx_ref, dw_ref, dpre_ref):
            w = w_ref[...]
            rows8 = _iota2((8, CONV_COLS), 0)

            def step1(ci, dw):
                r0, pre, shifted = _conv_pre(x_ref, w, ci, nchunk)
                sg = _sigmoid(pre)
                dpre = dy_ref[pl.ds(r0, CONV_ROWS), :] * sg * (1.0 + pre * (1.0 - sg))
                dpre_ref[pl.ds(r0, CONV_ROWS), :] = dpre
                for sh in range(CONV_W):
                    dw = dw + jnp.where(rows8 == CONV_W - 1 - sh, _colsum(dpre * shifted[sh]), 0.0)
                return dw

            dw_ref[...] = lax.fori_loop(0, nchunk, step1, jnp.zeros((8, CONV_COLS), F32))

            def step2(ci, carry):
                r0 = pl.multiple_of(ci * CONV_ROWS, CONV_ROWS)
                cur = dpre_ref[pl.ds(r0, CONV_ROWS), :]
                nxt = dpre_ref[pl.ds(pl.multiple_of(jnp.minimum(r0 + CONV_ROWS, s - 8), 8), 8), :]
                nxt = jnp.where(ci < nchunk - 1, nxt, 0.0)
                dx = w[CONV_W - 1:CONV_W, :] * cur
                for sh in range(1, CONV_W):
                    dx = dx + w[CONV_W - 1 - sh:CONV_W - sh, :] * _shift_up(cur, nxt, sh)
                dx_ref[pl.ds(r0, CONV_ROWS), :] = dx
                return carry

            lax.fori_loop(0, nchunk, step2, 0)

        outs.append(pl.pallas_call(
            kern, name=f"conv_bwd{part}", grid=(per,),
            in_specs=[pl.BlockSpec((s, CONV_COLS), lambda j, part=part: (0, CONV_BLOCK0 + part * per + j)),
                      pl.BlockSpec((8, CONV_COLS), lambda j, part=part: (0, part * per + j)),
                      pl.BlockSpec((s, CONV_COLS), lambda j: (0, j))],
            out_specs=[pl.BlockSpec((s, CONV_COLS), lambda j: (0, j)),
                       pl.BlockSpec((8, CONV_COLS), lambda j: (0, j))],
            out_shape=[jax.ShapeDtypeStruct((s, per * CONV_COLS), F32),
                       jax.ShapeDtypeStruct((8, per * CONV_COLS), F32)],
            scratch_shapes=[pltpu.VMEM((s, CONV_COLS), F32)],
            compiler_params=_params(("parallel",)),
        )(proj, conv_w8, dy))
    dx = jnp.concatenate([o[0] for o in outs], axis=1)
    dw = jnp.concatenate([o[1] for o in outs], axis=1)
    return dx, dw


FOXF_ROWS = 512
SMALL_BLOCK128 = 3584 // 128


def _log_sigmoid(x):
    return jnp.minimum(x, 0.0) - jnp.log(1.0 + jnp.exp(-jnp.abs(x)))


def fox_f_fwd(proj, bias_row):
    s = proj.shape[0]
    n = s // FOXF_ROWS

    def kern(x_ref, b_ref, f_ref, carry):
        @pl.when(pl.program_id(0) == 0)
        def _():
            carry[...] = jnp.zeros_like(carry)

        heads = _iota2((FOXF_ROWS, 128), 1) < FOX_HEADS
        lf = jnp.where(heads, _log_sigmoid(x_ref[...] + b_ref[...]), 0.0)
        ltri = (_iota2((FOXF_ROWS, FOXF_ROWS), 0) >= _iota2((FOXF_ROWS, FOXF_ROWS), 1)).astype(F32)
        c = hdot(ltri, lf) + carry[...]
        f_ref[...] = c
        carry[...] = c[FOXF_ROWS - 1:FOXF_ROWS, :]

    return pl.pallas_call(
        kern, name="fox_f_fwd", grid=(n,),
        in_specs=[pl.BlockSpec((FOXF_ROWS, 128), lambda i: (i, SMALL_BLOCK128)),
                  pl.BlockSpec((1, 128), lambda i: (0, 0))],
        out_specs=pl.BlockSpec((FOXF_ROWS, 128), lambda i: (i, 0)),
        out_shape=jax.ShapeDtypeStruct((s, 128), F32),
        scratch_shapes=[pltpu.VMEM((1, 128), F32)],
        compiler_params=_params(("arbitrary",)),
    )(proj, bias_row)


def fox_f_bwd(proj, bias_row, d_f):
    s = proj.shape[0]
    n = s // FOXF_ROWS

    def kern(x_ref, b_ref, df_ref, dx_ref, db_ref, carry):
        @pl.when(pl.program_id(0) == 0)
        def _():
            carry[...] = jnp.zeros_like(carry)
            db_ref[...] = jnp.zeros_like(db_ref)

        heads = _iota2((FOXF_ROWS, 128), 1) < FOX_HEADS
        utri = (_iota2((FOXF_ROWS, FOXF_ROWS), 0) <= _iota2((FOXF_ROWS, FOXF_ROWS), 1)).astype(F32)
        rc = hdot(utri, df_ref[...]) + carry[...]
        carry[...] = rc[0:1, :]
        dx = jnp.where(heads, rc * _sigmoid(-(x_ref[...] + b_ref[...])), 0.0)
        dx_ref[...] = dx
        db_ref[...] += _colsum(dx)

    return pl.pallas_call(
        kern, name="fox_f_bwd", grid=(n,),
        in_specs=[pl.BlockSpec((FOXF_ROWS, 128), lambda i: (n - 1 - i, SMALL_BLOCK128)),
                  pl.BlockSpec((1, 128), lambda i: (0, 0)),
                  pl.BlockSpec((FOXF_ROWS, 128), lambda i: (n - 1 - i, 0))],
        out_specs=[pl.BlockSpec((FOXF_ROWS, 128), lambda i: (n - 1 - i, 0)),
                   pl.BlockSpec((1, 128), lambda i: (0, 0))],
        out_shape=[jax.ShapeDtypeStruct((s, 128), F32), jax.ShapeDtypeStruct((1, 128), F32)],
        scratch_shapes=[pltpu.VMEM((1, 128), F32)],
        compiler_params=_params(("arbitrary",)),
    )(proj, bias_row, d_f)


FOX_T = 512
FOX_SCALE = FOX_DH ** -0.5
FOX_PAIRS = FOX_HEADS // 2
NEG = -1e30
_NT = (((1,), (1,)), ((), ()))


def _split3(x):
    def bf(v):
        return lax.reduce_precision(v, exponent_bits=8, mantissa_bits=7)

    hi = bf(x)
    mid = bf(x - hi)
    lo = bf(x - hi - mid)
    return jnp.stack([hi, mid, lo], axis=-1)


def _fox_extras(s, first, second):
    def part(v):
        if v is None:
            return jnp.zeros((s, FOX_HEADS, 3), F32)
        if isinstance(v, float):
            return jnp.full((s, FOX_HEADS, 3), v, F32)
        pairs = v.reshape(s, FOX_PAIRS, 2)
        return _split3(jnp.stack([pairs[:, :, 1], pairs[:, :, 0]], axis=-1).reshape(s, FOX_HEADS))

    cols = jnp.concatenate([part(first), part(second)], axis=-1)
    cols = _pad_to(cols, (s, FOX_HEADS, FOX_DH)).reshape(s, FOX_PAIRS, 2 * FOX_DH)
    return cols.transpose(1, 0, 2).astype(BF)


def _head_masks(rows):
    lane = _iota2((rows, 2 * FOX_DH), 1)
    return lane < FOX_DH, lane >= FOX_DH


def _extra_lane(e, slot):
    return (FOX_DH if e == 0 else 0) + slot


def fox_fwd(qkv, xq, xk, xv):
    s = qkv.shape[0]
    t = min(FOX_T, s)
    n = s // t

    def kern(q_ref, k_ref, v_ref, xq_ref, xk_ref, xv_ref, o_ref, lse_ref):
        i = pl.program_id(1)
        masks = _head_masks(t)
        q_pair, x_pair = q_ref[...] * FOX_SCALE, xq_ref[...]
        q_ops = [jnp.where(mk, q_pair, x_pair) for mk in masks]

        def step(j, carry, masked):
            sl = pl.ds(pl.multiple_of(j * t, t), t)
            k_pair, xk_pair, v_pair, xv_pair = k_ref[sl, :], xk_ref[sl, :], v_ref[sl, :], xv_ref[sl, :]
            k_ops = [jnp.where(mk, k_pair, xk_pair) for mk in masks]
            v_ops = [jnp.where(mk, v_pair, xv_pair) for mk in masks]
            sc = [lax.dot_general(q_ops[e], k_ops[e], _NT, preferred_element_type=F32) for e in range(2)]
            if masked:
                keep = _iota2((t, t), 0) >= _iota2((t, t), 1)
                sc = [jnp.where(keep, x, NEG) for x in sc]
            m_new = [jnp.maximum(carry[e][0], jnp.max(sc[e], axis=1, keepdims=True)) for e in range(2)]
            p = [jnp.exp(sc[e] - m_new[e]).astype(BF) for e in range(2)]
            pv = [jnp.dot(p[e], v_ops[e], preferred_element_type=F32) for e in range(2)]
            return tuple((m_new[e], jnp.exp(carry[e][0] - m_new[e]) * carry[e][1] + pv[e]) for e in range(2))

        init = tuple((jnp.full((t, 1), NEG, F32), jnp.zeros((t, 2 * FOX_DH), F32)) for _ in range(2))
        carry = lax.fori_loop(0, i, lambda j, c: step(j, c, False), init)
        carry = step(i, carry, True)
        lane = _iota2((t, 2 * FOX_DH), 1)
        outs, lses = [], []
        for e in range(2):
            m, acc = carry[e]
            l = jnp.sum(jnp.where(lane == _extra_lane(e, 0), acc, 0.0), axis=1, keepdims=True)
            outs.append(acc / l)
            lses.append(m + jnp.log(l))
        o_ref[...] = jnp.where(masks[0], outs[0], outs[1])
        lse_ref[...] = jnp.where(lane == 0, lses[0], jnp.where(lane == 1, lses[1], 0.0))

    pr = FOX_PAIRS
    return pl.pallas_call(
        kern, name="fox_fwd", grid=(pr, n),
        in_specs=[pl.BlockSpec((t, 128), lambda p, i: (i, p)),
                  pl.BlockSpec((s, 128), lambda p, i: (0, pr + p)),
                  pl.BlockSpec((s, 128), lambda p, i: (0, 2 * pr + p)),
                  pl.BlockSpec((None, t, 128), lambda p, i: (p, i, 0)),
                  pl.BlockSpec((None, s, 128), lambda p, i: (p, 0, 0)),
                  pl.BlockSpec((None, s, 128), lambda p, i: (p, 0, 0))],
        out_specs=[pl.BlockSpec((t, 128), lambda p, i: (i, p)),
                   pl.BlockSpec((None, t, 128), lambda p, i: (p, i, 0))],
        out_shape=[jax.ShapeDtypeStruct((s, FOX_HEADS * FOX_DH), F32), jax.ShapeDtypeStruct((pr, s, 128), F32)],
        compiler_params=_params(("parallel", "parallel")),
    )(qkv, qkv, qkv, xq, xk, xv)


def fox_bwd(qkv, d_o, xk, xv, xqb, xdo):
    s = qkv.shape[0]
    t = min(FOX_T, s)
    n = s // t
    w = 2 * FOX_DH

    def both(blocks, slot):
        lane = _iota2(blocks[0].shape, 1)
        own = jnp.where(lane < FOX_DH, blocks[0], blocks[1])
        sums = [jnp.sum(jnp.where(lane == _extra_lane(e, slot), blocks[e], 0.0), axis=1, keepdims=True)
                for e in range(2)]
        return own, jnp.where(lane == 0, sums[0], jnp.where(lane == 1, sums[1], 0.0))

    def kern(k_ref, v_ref, xk_ref, xv_ref, q_ref, do_ref, xq_ref, xd_ref,
             dq_ref, dk_ref, dv_ref, sq_ref, sk_ref, dq_acc):
        j = pl.program_id(1)

        @pl.when(j == 0)
        def _():
            dq_acc[...] = jnp.zeros_like(dq_acc)

        masks = _head_masks(t)
        k_ops = [jnp.where(mk, k_ref[...], xk_ref[...]) for mk in masks]
        v_ops = [jnp.where(mk, v_ref[...], xv_ref[...]) for mk in masks]

        def step(i, carry, masked):
            dk, dv = carry
            sl = pl.ds(pl.multiple_of(i * t, t), t)
            q_pair, xq_pair, do_pair, xd_pair = q_ref[sl, :] * FOX_SCALE, xq_ref[sl, :], do_ref[sl, :], xd_ref[sl, :]
            q_ops = [jnp.where(mk, q_pair, xq_pair) for mk in masks]
            do_ops = [jnp.where(mk, do_pair, xd_pair) for mk in masks]
            do_own = [jnp.where(mk, do_pair, 0).astype(BF) for mk in masks]
            st = [lax.dot_general(k_ops[e], q_ops[e], _NT, preferred_element_type=F32) for e in range(2)]
            dp = [lax.dot_general(v_ops[e], do_ops[e], _NT, preferred_element_type=F32) for e in range(2)]
            if masked:
                keep = _iota2((t, t), 0) <= _iota2((t, t), 1)
                st = [jnp.where(keep, x, NEG) for x in st]
            pt = [jnp.exp(x) for x in st]
            dsb = [(pt[e] * dp[e]).astype(BF) for e in range(2)]
            dv = dv + sum(jnp.dot(pt[e].astype(BF), do_own[e], preferred_element_type=F32) for e in range(2))
            dk = tuple(dk[e] + jnp.dot(dsb[e], q_ops[e], preferred_element_type=F32) for e in range(2))
            for e in range(2):
                dq_acc[sl, e * w:(e + 1) * w] += lax.dot_general(dsb[e], k_ops[e], (((0,), (0,)), ((), ())),
                                                                 preferred_element_type=F32)
            return dk, dv

        init = ((jnp.zeros((t, w), F32), jnp.zeros((t, w), F32)), jnp.zeros((t, w), F32))
        carry = step(j, init, True)
        dk, dv = lax.fori_loop(j + 1, n, lambda i, c: step(i, c, False), carry)
        dk_ref[...], sk_ref[...] = both(dk, 3)
        dv_ref[...] = dv

        @pl.when(j == n - 1)
        def _():
            def out(r, carry):
                sl = pl.ds(pl.multiple_of(r * t, t), t)
                own, sums = both([dq_acc[sl, e * w:(e + 1) * w] for e in range(2)], 0)
                dq_ref[sl, :] = own * FOX_SCALE
                sq_ref[sl, :] = sums
                return carry

            lax.fori_loop(0, n, out, 0)

    pr = FOX_PAIRS
    flat = jax.ShapeDtypeStruct((s, FOX_HEADS * FOX_DH), F32)
    tile = pl.BlockSpec((t, 128), lambda p, j: (j, p))
    whole = pl.BlockSpec((s, 128), lambda p, j: (0, p))
    return pl.pallas_call(
        kern, name="fox_bwd", grid=(pr, n),
        in_specs=[pl.BlockSpec((t, 128), lambda p, j: (j, pr + p)),
                  pl.BlockSpec((t, 128), lambda p, j: (j, 2 * pr + p)),
                  pl.BlockSpec((None, t, 128), lambda p, j: (p, j, 0)),
                  pl.BlockSpec((None, t, 128), lambda p, j: (p, j, 0)),
                  whole, whole,
                  pl.BlockSpec((None, s, 128), lambda p, j: (p, 0, 0)),
                  pl.BlockSpec((None, s, 128), lambda p, j: (p, 0, 0))],
        out_specs=[whole, tile, tile, whole, tile],
        out_shape=[flat] * 5,
        scratch_shapes=[pltpu.VMEM((s, 2 * w), F32)],
        compiler_params=_params(("parallel", "arbitrary")),
    )(qkv, qkv, xk, xv, qkv, d_o, xqb, xdo)


def _xattn_head(q, k, v):
    sc = bdot(q, k, 1, 1) * (MEM_DH ** -0.5)
    e = jnp.exp(sc - lax.stop_gradient(jnp.max(sc, axis=-1, keepdims=True)))
    p = e / jnp.sum(e, axis=-1, keepdims=True)
    return bdot(p, v, 1, 0)


def xattn_fwd(q, kv):
    s = q.shape[0]
    hh = MEM_HEADS

    def body(*vals):
        qs, ks, vs = vals[:hh], vals[hh:2 * hh], vals[2 * hh:]
        return jnp.concatenate([_xattn_head(qs[a], ks[a], vs[a]) for a in range(hh)], axis=1)

    return rowcall(body, [(q, MEM_DH, a) for a in range(hh)],
                   [(kv, MEM_DH, a) for a in range(2 * hh)],
                   [(hh * MEM_DH, BF)], rows=512, total=s, name="xattn_fwd")[0]


def xattn_bwd(q, kv, d_o):
    s = q.shape[0]
    hh = MEM_HEADS

    def body(*vals):
        qs, dos = vals[:hh], vals[hh:2 * hh]
        ks, vs = vals[2 * hh:3 * hh], vals[3 * hh:]
        dqs, dks, dvs = [], [], []
        for a in range(hh):
            _, vjp = jax.vjp(_xattn_head, qs[a], ks[a], vs[a])
            dq, dk, dv = vjp(dos[a])
            dqs.append(dq)
            dks.append(dk)
            dvs.append(dv)
        return jnp.concatenate(dqs, axis=1), jnp.concatenate(dks + dvs, axis=1)

    return rowcall(body, [(q, MEM_DH, a) for a in range(hh)] + [(d_o, MEM_DH, a) for a in range(hh)],
                   [(kv, MEM_DH, a) for a in range(2 * hh)],
                   [(hh * MEM_DH, BF)], [kv.shape], rows=512, total=s, name="xattn_bwd")


def _slab(ref, axis, start, size):
    if axis is None:
        return ref
    if axis == "lead":
        return ref.at[start]
    idx = pl.ds(pl.multiple_of(start, 128 if axis == 1 else 16), size)
    return ref.at[idx] if axis == 0 else ref.at[:, idx]


def exchange(inputs, outputs, transfers, name):
    ni, no, nt = len(inputs), len(outputs), len(transfers)
    npeer = N_DEV - 1

    def body(*refs):
        ins, outs = refs[:ni], refs[ni:ni + no]
        send, recv, loc = refs[ni + no:]
        x, y, c = lax.axis_index("x"), lax.axis_index("y"), lax.axis_index("c")
        me = 4 * x + 2 * y + c

        def peer(p):
            px = 1 - x if p & 4 else x
            py = 1 - y if p & 2 else y
            pc = 1 - c if p & 1 else c
            return (px, py, pc), 4 * px + 2 * py + pc

        def view(ref, spec, who):
            axis, off, stride, size = spec
            return _slab(ref, axis, off + who * stride, size)

        local, remote = [], []
        for w, (ii, src, oi, dst) in enumerate(transfers):
            cp = pltpu.make_async_copy(view(ins[ii], src, me), view(outs[oi], dst, me), loc.at[w])
            cp.start()
            local.append(cp)
        for p in range(1, N_DEV):
            dev, idx = peer(p)
            for w, (ii, src, oi, dst) in enumerate(transfers):
                k = w * npeer + p - 1
                out_cp = pltpu.make_async_remote_copy(
                    src_ref=view(ins[ii], src, idx), dst_ref=view(outs[oi], dst, me), send_sem=send.at[k],
                    recv_sem=recv.at[k], device_id=dev, device_id_type=MESH)
                out_cp.start()
                in_cp = pltpu.make_async_remote_copy(
                    src_ref=view(ins[ii], src, idx), dst_ref=view(outs[oi], dst, idx), send_sem=send.at[k],
                    recv_sem=recv.at[k], device_id=dev, device_id_type=MESH)
                remote.append((out_cp, in_cp))
        for out_cp, in_cp in remote:
            in_cp.wait_recv()
            out_cp.wait_send()
        for cp in local:
            cp.wait()

    hbm = pl.BlockSpec(memory_space=pl.ANY)
    return pl.pallas_call(
        body, name=name, in_specs=[hbm] * ni, out_specs=[hbm] * no, out_shape=list(outputs),
        scratch_shapes=[pltpu.SemaphoreType.DMA((nt * npeer,)), pltpu.SemaphoreType.DMA((nt * npeer,)),
                        pltpu.SemaphoreType.DMA((nt,))],
        compiler_params=pltpu.CompilerParams(has_side_effects=True),
    )(*inputs)


def _peer(p):
    x, y, c = lax.axis_index("x"), lax.axis_index("y"), lax.axis_index("c")
    px = 1 - x if p & 4 else x
    py = 1 - y if p & 2 else y
    pc = 1 - c if p & 1 else c
    return (px, py, pc), 4 * px + 2 * py + pc


def _view(ref, spec, who):
    axis, off, stride, size = spec
    return _slab(ref, axis, off + who * stride, size)


def place_own(inputs, outputs, transfers):
    me = 4 * lax.axis_index("x") + 2 * lax.axis_index("y") + lax.axis_index("c")
    lands = [lax.empty(o.shape, o.dtype) for o in outputs]
    for ii, src, oi, dst in transfers:
        axis, off, stride, size = src
        own = inputs[ii] if axis is None else lax.dynamic_slice_in_dim(inputs[ii], off + me * stride, size, axis)
        axis, off, stride, size = dst
        if axis == "lead":
            lands[oi] = lax.dynamic_update_slice_in_dim(lands[oi], own[None], me, 0)
        else:
            lands[oi] = lax.dynamic_update_slice_in_dim(lands[oi], own, off + me * stride, axis)
    return lands


_HBM = pl.BlockSpec(memory_space=pltpu.HBM)
_SEM = pl.BlockSpec(memory_space=pltpu.SEMAPHORE)
_EFFECT = pltpu.SideEffectType.DATAFLOW_SIDE_EFFECTING


def _remote_copies(ins, lands, transfers, send, recv):
    npeer = N_DEV - 1
    me = 4 * lax.axis_index("x") + 2 * lax.axis_index("y") + lax.axis_index("c")
    pairs = []
    for p in range(1, N_DEV):
        dev, idx = _peer(p)
        for w, (ii, src, oi, dst) in enumerate(transfers):
            k = w * npeer + p - 1
            common = dict(src_ref=_view(ins[ii], src, idx), send_sem=send.at[k], recv_sem=recv.at[k],
                          device_id=dev, device_id_type=MESH)
            pairs.append((pltpu.make_async_remote_copy(dst_ref=_view(lands[oi], dst, me), **common),
                          pltpu.make_async_remote_copy(dst_ref=_view(lands[oi], dst, idx), **common)))
    return pairs


def exchange_start(inputs, lands, transfers, after, name):
    ni, nl, nsem = len(inputs), len(lands), len(transfers) * (N_DEV - 1)

    def body(*refs):
        ins, lnd = refs[:ni], refs[ni:ni + nl]
        send, recv = refs[ni + nl + 1], refs[ni + nl + 2]
        token = refs[-1]
        for out_cp, _ in _remote_copies(ins, lnd, transfers, send, recv):
            out_cp.start()
        token[...] = jnp.zeros_like(token)

    args = [pltpu.with_memory_space_constraint(a, pltpu.HBM) for a in list(inputs) + list(lands)]
    res = pl.pallas_call(
        body, name=name,
        out_shape=(pltpu.SemaphoreType.DMA((nsem,)), pltpu.SemaphoreType.DMA((nsem,)),
                   *[pltpu.HBM(a.shape, a.dtype) for a in args], jax.ShapeDtypeStruct((8, 128), F32)),
        in_specs=[_HBM] * (ni + nl) + [pl.BlockSpec(memory_space=pl.ANY)],
        out_specs=(_SEM, _SEM, *[_HBM] * (ni + nl), pl.BlockSpec(memory_space=pltpu.VMEM)),
        input_output_aliases={k: k + 2 for k in range(ni + nl)},
        compiler_params=pltpu.CompilerParams(has_side_effects=_EFFECT),
    )(*args, after)
    return res[0], res[1], list(res[2:2 + ni]), list(res[2 + ni:2 + ni + nl]), res[-1]


def exchange_wait(send, recv, inputs, lands, after, transfers, name):
    ni, nl = len(inputs), len(lands)

    def body(*refs):
        ins, lnd = refs[:ni], refs[ni:ni + nl]
        send_r, recv_r = refs[ni + nl], refs[ni + nl + 1]
        for out_cp, in_cp in _remote_copies(ins, lnd, transfers, send_r, recv_r):
            out_cp.wait_send()
            in_cp.wait_recv()

    res = pl.pallas_call(
        body, name=name,
        out_shape=tuple(pltpu.HBM(a.shape, a.dtype) for a in list(inputs) + list(lands)),
        in_specs=[_HBM] * (ni + nl) + [_SEM, _SEM, pl.BlockSpec(memory_space=pl.ANY)],
        out_specs=tuple([_HBM] * (ni + nl)),
        input_output_aliases={k: k for k in range(ni + nl)},
        compiler_params=pltpu.CompilerParams(has_side_effects=_EFFECT),
    )(*inputs, *lands, send, recv, after)
    return list(res[ni:])


def adamw(w, m, v, contribs, name):
    r, c = w.shape
    nc = len(contribs)
    rows = next((r // d for d in (4, 2) if r % d == 0 and (r // d) % 16 == 0), r)
    c1, c2 = 1.0 - ADAM_B1 ** ADAM_STEP, 1.0 - ADAM_B2 ** ADAM_STEP

    def body(wv, mv, vv, *gs):
        g = gs[0].astype(F32)
        for extra in gs[1:]:
            g = g + extra.astype(F32)
        g = g[:, :c]
        m_new = ADAM_B1 * mv + (1.0 - ADAM_B1) * g
        v_new = ADAM_B2 * vv + (1.0 - ADAM_B2) * (g * g)
        delta = -ADAM_LR * ((m_new / c1) / (jnp.sqrt(v_new / c2) + ADAM_EPS) + ADAM_WD * wv)
        return g, delta, m_new, v_new

    assert nc >= 1
    return rowcall(body, [w, m, v] + list(contribs), [], [(c, F32)] * 4, rows=rows, total=r, name=name)


WEIGHTS = ['ffn1_pre_norm', 'ffn1_w_gate', 'ffn1_w_up', 'ffn1_w_down', 'ffn1_post_norm', 'mix_pre_norm', 'w_in',
           'fox_f_bias', 'gdn_conv_w', 'gdn_a_log', 'gdn_dt_bias', 'gdn_out_norm', 'w_out', 'mix_post_norm',
           'mem_pre_norm', 'mem_kv_norm', 'mem_w_q', 'mem_w_kv', 'mem_w_o', 'mem_post_norm', 'ffn2_pre_norm',
           'ffn2_w_gate', 'ffn2_w_up', 'ffn2_w_down', 'ffn2_post_norm']
GAINS = ['ffn1_pre_norm', 'ffn1_post_norm', 'mix_pre_norm', 'mix_post_norm', 'mem_pre_norm', 'mem_kv_norm',
         'mem_post_norm', 'ffn2_pre_norm', 'ffn2_post_norm']
BIG = ['ffn1_w_gate', 'ffn1_w_up', 'ffn1_w_down', 'w_in', 'w_out', 'mem_w_q', 'mem_w_kv', 'mem_w_o',
       'ffn2_w_gate', 'ffn2_w_up', 'ffn2_w_down']
PACK_ROWS = 24
ROW_MISC = len(GAINS)
ROW_CONV = ROW_MISC + 1
COL_FBIAS, COL_ALOG, COL_DTB, COL_ONORM, COL_LOSS = 0, 8, 12, 128, 256
CONV_CH = 3 * GDN_HEADS * GDN_DH


def _pad_to(a, shape):
    return jnp.pad(a, [(0, t - s) for s, t in zip(a.shape, shape)])


def _pack(get, conv=None, loss=None):
    rows = [get(nm) for nm in GAINS]
    misc = jnp.concatenate([get('fox_f_bias'), get('gdn_a_log'), get('gdn_dt_bias'),
                            jnp.zeros((1, COL_ONORM - COL_DTB - 4), F32), get('gdn_out_norm'),
                            jnp.zeros((1, 1), F32) if loss is None else loss.reshape(1, 1)], axis=1)
    rows.append(_pad_to(misc, (1, D_MODEL)))
    rows.append(jnp.zeros((6, D_MODEL), F32) if conv is None else conv.reshape(6, D_MODEL))
    return _pad_to(jnp.concatenate(rows, axis=0), (PACK_ROWS, D_MODEL))


def _unpack(p):
    out = {nm: p[i:i + 1] for i, nm in enumerate(GAINS)}
    misc = p[ROW_MISC:ROW_MISC + 1]
    out['fox_f_bias'] = misc[:, COL_FBIAS:COL_FBIAS + FOX_HEADS]
    out['gdn_a_log'] = misc[:, COL_ALOG:COL_ALOG + GDN_HEADS]
    out['gdn_dt_bias'] = misc[:, COL_DTB:COL_DTB + GDN_HEADS]
    out['gdn_out_norm'] = misc[:, COL_ONORM:COL_ONORM + GDN_DH]
    return out


def _ffn_fwd(h, pre, wgu, wd, tag):
    s = h.shape[0]
    u, = rowcall(_rms, [h], [pre], [(D_MODEL, BF)], rows=512, total=s, name=tag + "_pre")
    gate, up, act = mm_swiglu(u, wgu, name=tag + "_gate_up")
    if callable(wd):
        wd = wd(act)
    f = mm(act, wd, name=tag + "_down")
    return u, gate, up, act, f


def _half_rms(a, g):
    return 0.5 * _rms(a, g)


def _ffn_bwd(dh_out, h, pre, post, wgu, wd, saved, tag, on_dwd=None, on_dwgu=None):
    u, gate, up, act, f = saved
    s = h.shape[0]

    def b_post(dh, fv, pg):
        return jax.vjp(_half_rms, fv, pg)[1](dh)

    df, dpost = rowcall(b_post, [dh_out, f], [post], [(D_MODEL, BF)], [(1, D_MODEL)], rows=512, total=s,
                        name=tag + "_bwd_post")
    dwd = mm(act, df, ta=True, out_dtype=BF, name=tag + "_bwd_dwd")
    dgate, dup = mm_dswiglu(df, wd, gate, up, name=tag + "_bwd_dact", token=on_dwd(dwd) if on_dwd else None)
    dwg = mm(u, dgate, ta=True, out_dtype=BF, name=tag + "_bwd_dwg")
    dwu = mm(u, dup, ta=True, out_dtype=BF, name=tag + "_bwd_dwu")
    du = mm_pair(dgate, dup, wgu, name=tag + "_bwd_du", token=on_dwgu(dwg, dwu) if on_dwgu else None)

    def b_pre(dh, duv, hv, pg):
        dx, dpre = jax.vjp(_rms, hv, pg)[1](duv)
        return dh + dx, dpre

    dh, dpre = rowcall(b_pre, [dh_out, du, h], [pre], [(D_MODEL, F32)], [(1, D_MODEL)], rows=512, total=s,
                       name=tag + "_bwd_pre")
    return dh, dwg, dwu, dwd, dpre, dpost


def _residual_rms(h, a, g):
    return h + _rms(a, g)


def _bwd_residual(dh, a, g):
    return jax.vjp(_rms, a, g)[1](dh)


def _step(a):
    x, mem = a['x'][0], a['mem'][0]
    s = x.shape[0]
    me = 4 * lax.axis_index("x") + 2 * lax.axis_index("y") + lax.axis_index("c")
    w2 = {nm: a[nm][0] for nm in WEIGHTS}
    m2 = {nm: a['m_' + nm][0] for nm in WEIGHTS}
    v2 = {nm: a['v_' + nm][0] for nm in WEIGHTS}
    small = {nm: w2[nm][None] for nm in WEIGHTS if nm not in BIG and nm != 'gdn_conv_w'}

    def ff_cols(w):
        return _pad_to(w, (D_MODEL, FF_SHARD_PAD)).astype(BF)

    def ff_rows(w):
        return _pad_to(w, (FF_SHARD_PAD, D_MODEL)).astype(BF)

    whole = (None, 0, 0, 0)
    conv_pad = 256
    g_in = [ff_cols(w2['ffn1_w_gate']), ff_cols(w2['ffn1_w_up']), ff_rows(w2['ffn1_w_down']),
            ff_cols(w2['ffn2_w_gate']), ff_cols(w2['ffn2_w_up']), ff_rows(w2['ffn2_w_down']),
            _pad_to(w2['w_in'], (D_MODEL, IN_SHARD_PAD)).astype(BF), w2['w_out'].astype(BF),
            w2['mem_w_q'].astype(BF), w2['mem_w_kv'].astype(BF), w2['mem_w_o'].astype(BF),
            _pad_to(w2['gdn_conv_w'], (8, conv_pad))]
    g_out = [jax.ShapeDtypeStruct((D_MODEL, 2 * D_FF_PAD), BF), jax.ShapeDtypeStruct((D_FF_PAD, D_MODEL), BF),
             jax.ShapeDtypeStruct((D_MODEL, 2 * D_FF_PAD), BF), jax.ShapeDtypeStruct((D_FF_PAD, D_MODEL), BF),
             jax.ShapeDtypeStruct((D_MODEL, N_DEV * IN_SHARD_PAD), BF), jax.ShapeDtypeStruct((D_MODEL, D_MODEL), BF),
             jax.ShapeDtypeStruct((D_MODEL, D_MODEL), BF), jax.ShapeDtypeStruct((D_MODEL, 2 * D_MODEL), BF),
             jax.ShapeDtypeStruct((D_MODEL, D_MODEL), BF), jax.ShapeDtypeStruct((8, N_DEV * conv_pad), F32)]
    sp_, dm = FF_SHARD_PAD, D_MODEL // N_DEV
    g_tr = [(0, whole, 0, (1, 0, sp_, sp_)), (1, whole, 0, (1, D_FF_PAD, sp_, sp_)), (2, whole, 1, (0, 0, sp_, sp_)),
            (3, whole, 2, (1, 0, sp_, sp_)), (4, whole, 2, (1, D_FF_PAD, sp_, sp_)), (5, whole, 3, (0, 0, sp_, sp_)),
            (6, whole, 4, (1, 0, IN_SHARD_PAD, IN_SHARD_PAD)), (7, whole, 5, (0, 0, dm, dm)),
            (8, whole, 6, (0, 0, dm, dm)), (9, whole, 7, (1, 0, 2 * dm, 2 * dm)), (10, whole, 8, (0, 0, dm, dm)),
            (11, whole, 9, (1, 0, conv_pad, conv_pad))]
    def pick(idx):
        ins = sorted({g_tr[k][0] for k in idx})
        outs = sorted({g_tr[k][2] for k in idx})
        tr = [(ins.index(g_tr[k][0]), g_tr[k][1], outs.index(g_tr[k][2]), g_tr[k][3]) for k in idx]
        return [g_in[i] for i in ins], [g_out[o] for o in outs], tr

    gu_in, gu_out, gu_tr = pick([0, 1])
    wgu1, = exchange(gu_in, gu_out, gu_tr, "gather_gate_up")
    stages, after = [], wgu1
    for nm, idx in (("down", [2]), ("mix", [6, 7, 11]), ("late", [8, 9, 10, 3, 4, 5])):
        st_in, st_out, st_tr = pick(idx)
        st = exchange_start(st_in, place_own(st_in, st_out, st_tr), st_tr, after, "gather_%s_start" % nm)
        stages.append((st, st_tr, "gather_%s_wait" % nm))
        after = st[4]
    g_token = after

    def gather_wait(k, after_):
        (send_, recv_, src_, land_, _), tr_, nm_ = stages[k]
        return exchange_wait(send_, recv_, src_, land_, after_, tr_, nm_)

    bias_row = _pad_to(small['fox_f_bias'], (1, 128))
    gate_prm = _pad_to(jnp.concatenate([_pad_to(small['gdn_a_log'], (1, 128 - SMALL_A)),
                                        _pad_to(small['gdn_dt_bias'], (1, 128 - SMALL_A))], axis=0),
                       (8, 128 - SMALL_A))
    gate_prm = jnp.pad(gate_prm, ((0, 0), (SMALL_A, 0)))
    onorm = small['gdn_out_norm']

    late = {}

    def wd1_when(act):
        late['wd1'], = gather_wait(0, act)
        return late['wd1']

    sv1 = _ffn_fwd(x, small['ffn1_pre_norm'] + g_token[0, 0], wgu1, wd1_when, "ffn1")
    wd1 = late['wd1']
    h1, = rowcall(lambda h, f, g: h + _half_rms(f, g), [x, sv1[4]], [small['ffn1_post_norm']], [(D_MODEL, F32)],
                  rows=512, total=s, name="ffn1_out")
    w_in_g, w_out, conv_g = gather_wait(1, h1)
    w_in = jnp.concatenate([w_in_g[:, j * IN_SHARD_PAD:j * IN_SHARD_PAD + IN_SHARD] for j in range(N_DEV)],
                           axis=1)
    sp = [0, 512, 1024, 1536, 1544, 2056, 2568, 3080, 3592, 3596, 3600]
    fq, fk, fv, ff, gq, gk, gv, gz, gb, ga = [w_in[:, sp[i]:sp[i + 1]] for i in range(10)]
    w_proj = jnp.concatenate([fq, fk, fv, gq, gk, gv, gz, ff, gb, ga,
                              jnp.zeros((D_MODEL, PROJ_W - 3584 - 16), BF)], axis=1)
    conv_w8 = conv_g.reshape(8, N_DEV, conv_pad)[:, :, :CONV_CH // N_DEV].reshape(8, CONV_CH)


    u2, = rowcall(_rms, [h1], [small['mix_pre_norm']], [(D_MODEL, BF)], rows=512, total=s, name="mix_pre")
    proj = mm(u2, w_proj, name="mix_proj")
    f_cum = fox_f_fwd(proj, bias_row)
    f_heads = f_cum[:, :FOX_HEADS]
    qkv_bf = proj[:, :3 * FOX_HEADS * FOX_DH].astype(BF)
    xk, xv = _fox_extras(s, 1.0, -f_heads), _fox_extras(s, 1.0, None)
    fox_flat, lse = fox_fwd(qkv_bf, _fox_extras(s, f_heads, 1.0), xk, xv)
    lse_heads = lse[:, :, :2].transpose(1, 0, 2).reshape(s, FOX_HEADS)
    cqkv = conv_fwd(proj, conv_w8)
    g_l, b_l = rowcall(_gdn_gates, [(proj, 128, SMALL_BLOCK128)], [gate_prm], [(512, F32), (512, F32)],
                       rows=512, total=s, name="gdn_gates")
    gbb = jnp.concatenate([g_l, b_l], axis=1)
    gdn_o, states = gdn_fwd(cqkv, proj, gbb, onorm)
    mixed = jnp.concatenate([fox_flat, gdn_o], axis=1).astype(BF)
    mo = mm(mixed, w_out, name="mix_out")
    h2, = rowcall(_residual_rms, [h1, mo], [small['mix_post_norm']], [(D_MODEL, F32)], rows=512, total=s,
                  name="mix_res")

    hq, = rowcall(_rms, [h2], [small['mem_pre_norm']], [(D_MODEL, BF)], rows=512, total=s, name="mem_pre")
    mn, = rowcall(_rms, [mem], [small['mem_kv_norm']], [(D_MODEL, BF)], rows=256, total=mem.shape[0], name="mem_kvn")
    wgu2, wd2, w_q, w_kv, w_o = gather_wait(2, h2)
    q_mem = mm(hq, w_q, name="mem_q")
    kv_mem = mm(mn, w_kv, name="mem_kv")
    o_mem = xattn_fwd(q_mem, kv_mem)
    c_mem = mm(o_mem, w_o, name="mem_o")
    h3, = rowcall(_residual_rms, [h2, c_mem], [small['mem_post_norm']], [(D_MODEL, F32)], rows=512, total=s,
                  name="mem_res")

    sv2 = _ffn_fwd(h3, small['ffn2_pre_norm'], wgu2, wd2, "ffn2")

    def b_loss(h, f, tgt, g):
        err = h + _half_rms(f, g) - tgt
        part = 0.5 * jnp.sum(jnp.mean(err * err, axis=-1, keepdims=True), axis=0, keepdims=True)
        return err * (1.0 / D_MODEL), jnp.broadcast_to(part, (1, 128))

    dy, loss_acc = rowcall(b_loss, [h3, sv2[4], a['loss_target'][0]], [small['ffn2_post_norm']], [(D_MODEL, F32)],
                           [(1, 128)], rows=512, total=s, name="loss")

    grads = {}
    dh3, dwg2, dwu2, dwd2, grads['ffn2_pre_norm'], grads['ffn2_post_norm'] = _ffn_bwd(
        dy, h3, small['ffn2_pre_norm'], small['ffn2_post_norm'], wgu2, wd2, sv2, "ffn2")

    lead = ("lead", 0, 1, 0)

    def land(r, c, dt=BF):
        return jax.ShapeDtypeStruct((N_DEV, r, c), dt)

    ffn_tr = [(0, (1, 0, sp_, sp_), 0, lead), (1, (1, 0, sp_, sp_), 1, lead), (2, (0, 0, sp_, FF_SHARD), 2, lead)]
    ffn_land = [land(D_MODEL, sp_), land(D_MODEL, sp_), land(FF_SHARD, D_MODEL)]
    a_in = [dwg2, dwu2, dwd2]
    a_send, a_recv, a_src, a_land, a_token = exchange_start(a_in, place_own(a_in, ffn_land, ffn_tr), ffn_tr, dh3,
                                                            "reduce_ffn2_start")

    dc, grads['mem_post_norm'] = rowcall(_bwd_residual, [dh3, c_mem], [small['mem_post_norm'] + a_token[0, 0]],
                                         [(D_MODEL, BF)],
                                         [(1, D_MODEL)], rows=512, total=s, name="mem_bwd_res")
    d_o = mm(dc, w_o, tb=True, name="mem_bwd_do")
    dw_o = mm(o_mem, dc, ta=True, out_dtype=BF, name="mem_bwd_dwo")
    dq_mem, dkv = xattn_bwd(q_mem, kv_mem, d_o)
    dhq = mm(dq_mem, w_q, tb=True, name="mem_bwd_dhq")
    dw_q = mm(hq, dq_mem, ta=True, out_dtype=BF, name="mem_bwd_dwq")
    dmn = mm(dkv, w_kv, tb=True, name="mem_bwd_dmn")
    dw_kv = mm(mn, dkv, ta=True, out_dtype=BF, name="mem_bwd_dwkv")
    _, grads['mem_kv_norm'] = rowcall(lambda d, mv, g: jax.vjp(_rms, mv, g)[1](d), [dmn, mem],
                                      [small['mem_kv_norm']], [(D_MODEL, F32)], [(1, D_MODEL)], rows=256,
                                      total=mem.shape[0], name="mem_bwd_kvn")

    def b_pre(dh, duv, hv, pg):
        dx, dpre = jax.vjp(_rms, hv, pg)[1](duv)
        return dh + dx, dpre

    dh2, grads['mem_pre_norm'] = rowcall(b_pre, [dh3, dhq, h2], [small['mem_pre_norm']], [(D_MODEL, F32)],
                                         [(1, D_MODEL)], rows=512, total=s, name="mem_bwd_pre")

    dmo, grads['mix_post_norm'] = rowcall(_bwd_residual, [dh2, mo], [small['mix_post_norm']], [(D_MODEL, BF)],
                                          [(1, D_MODEL)], rows=512, total=s, name="mix_bwd_res")
    d_mixed = mm(dmo, w_out, tb=True, name="mix_bwd_dmixed")
    dw_out = mm(mixed, dmo, ta=True, out_dtype=BF, name="mix_bwd_dwout")
    def b_delta(do, o):
        sel = (_iota2((512, 128), 0) // FOX_DH == _iota2((512, 128), 1)).astype(F32)
        return hdot(do * o, sel)

    delta, = rowcall(b_delta, [(d_mixed, 512, 0), fox_flat], [], [(128, F32)], rows=512, total=s, name="fox_delta")
    dfox_q, dfox_k, dvf, sum_q, sum_k = fox_bwd(qkv_bf, d_mixed[:, :512].astype(BF), xk, xv,
                                                _fox_extras(s, f_heads - lse_heads, 1.0),
                                                _fox_extras(s, -delta[:, :FOX_HEADS], None))
    per_head = lambda a: jnp.stack([a[:, 0::2 * FOX_DH], a[:, 1::2 * FOX_DH]], axis=-1).reshape(s, FOX_HEADS)
    d_f = _pad_to(per_head(sum_q) - per_head(sum_k), (s, 128))
    dsmall_f, dbias = fox_f_bwd(proj, bias_row, d_f)
    grads['fox_f_bias'] = dbias[:, :FOX_HEADS]
    dcqkv, dz, dgb, grads['gdn_out_norm'] = gdn_bwd(cqkv, proj, gbb, onorm, states, d_mixed)

    def b_gates(sm, dsf, dg, db, prm):
        dsm, dprm = jax.vjp(_gdn_gates, sm, prm)[1]((dg, db))
        return dsm + dsf, dprm

    dsmall, dprm = rowcall(b_gates, [(proj, 128, SMALL_BLOCK128), dsmall_f, (dgb, 512, 0), (dgb, 512, 1)], [gate_prm],
                           [(128, F32)],
                           [(8, 128)], rows=512, total=s, name="gdn_bwd_gates")
    grads['gdn_a_log'] = dprm[0:1, SMALL_A:SMALL_A + GDN_HEADS]
    grads['gdn_dt_bias'] = dprm[1:2, SMALL_A:SMALL_A + GDN_HEADS]
    dqkv_pre, dconv8 = conv_bwd(proj, conv_w8, dcqkv)
    dproj = jnp.concatenate([dfox_q, dfox_k, dvf, dqkv_pre, dz, dsmall,
                             jnp.zeros((s, PROJ_W - 3584 - 128), F32)], axis=1).astype(BF)
    du2 = mm(dproj, w_proj, tb=True, name="mix_bwd_du")
    dw_proj = mm(u2, dproj, ta=True, out_dtype=BF, name="mix_bwd_dwproj")
    dh1, grads['mix_pre_norm'] = rowcall(b_pre, [dh2, du2, h1], [small['mix_pre_norm']], [(D_MODEL, F32)],
                                         [(1, D_MODEL)], rows=512, total=s, name="mix_bwd_pre")

    dw_in = jnp.concatenate([dw_proj[:, :1536], dw_proj[:, 3584:3592], dw_proj[:, 1536:3584],
                             dw_proj[:, 3592:3600]], axis=1)
    gap = jnp.zeros((D_MODEL, IN_SHARD_PAD - IN_SHARD), BF)
    dw_in = jnp.concatenate([piece for j in range(N_DEV) for piece in (dw_in[:, j * IN_SHARD:(j + 1) * IN_SHARD], gap)],
                            axis=1)
    b_in = [dw_in, dw_out, dw_q, dw_kv, dw_o]
    b_tr = [(0, (1, 0, IN_SHARD_PAD, IN_SHARD_PAD), 0, lead), (1, (0, 0, dm, dm), 1, lead), (2, (0, 0, dm, dm), 2, lead),
            (3, (1, 0, 2 * dm, 2 * dm), 3, lead), (4, (0, 0, dm, dm), 4, lead)]
    b_shapes = [land(D_MODEL, IN_SHARD_PAD), land(dm, D_MODEL), land(dm, D_MODEL), land(D_MODEL, 2 * dm),
                land(dm, D_MODEL)]
    b_land = place_own(b_in, b_shapes, b_tr)
    b_send, b_recv, b_src, b_land, b_token = exchange_start(b_in, b_land, b_tr, dh1, "reduce_mix_start")

    def start_down_reduce(dwd):
        tr = ffn_tr[2:]
        tr = [(0, tr[0][1], 0, tr[0][3])]
        late['c_down'] = (exchange_start([dwd], place_own([dwd], ffn_land[2:], tr), tr, dwd, "reduce_ffn1_down_start"), tr)
        return late['c_down'][0][4]

    def start_gate_up_reduce(dwg, dwu):
        tr = ffn_tr[:2]
        late['c_gu'] = (exchange_start([dwg, dwu], place_own([dwg, dwu], ffn_land[:2], tr), tr, dwu,
                                       "reduce_ffn1_gu_start"), tr)
        return late['c_gu'][0][4]

    grad_x, _, _, _, grads['ffn1_pre_norm'], grads['ffn1_post_norm'] = _ffn_bwd(
        dh1, x, small['ffn1_pre_norm'], small['ffn1_post_norm'] + b_token[0, 0], wgu1, wd1, sv1, "ffn1",
        on_dwd=start_down_reduce, on_dwgu=start_gate_up_reduce)

    gpack = _pack(lambda nm: grads[nm], conv=dconv8[:CONV_W], loss=loss_acc[:, :1])
    gsum_parts, = exchange([gpack], [land(PACK_ROWS, D_MODEL, F32)], [(0, whole, 0, lead)], "reduce_small")
    a_got = exchange_wait(a_send, a_recv, a_src, a_land, gsum_parts, ffn_tr, "reduce_ffn2_wait")
    b_got = exchange_wait(b_send, b_recv, b_src, b_land, gsum_parts, b_tr, "reduce_mix_wait")
    recv = dict(zip(['ffn2_w_gate', 'ffn2_w_up', 'ffn2_w_down', 'w_in', 'w_out', 'mem_w_q', 'mem_w_kv', 'mem_w_o'],
                    a_got + b_got))

    out_g, out_d, out_m, out_v = {}, {}, {}, {}

    def update(nm):
        r = recv[nm]
        res = adamw(w2[nm], m2[nm], v2[nm], [(r, r.shape[2], 0, d) for d in range(N_DEV)], "adamw_" + nm)
        out_g[nm], out_d[nm], out_m[nm], out_v[nm] = res

    for nm in recv:
        update(nm)
    wp = _pack(lambda nm: small[nm])
    mp = _pack(lambda nm: m2[nm][None])
    vp = _pack(lambda nm: v2[nm][None])
    pg, pd, pm, pv = adamw(wp, mp, vp, [(gsum_parts, D_MODEL, 0, d) for d in range(N_DEV)], "adamw_small")
    for dst, p in ((out_g, pg), (out_d, pd), (out_m, pm), (out_v, pv)):
        dst.update({k: val[0] for k, val in _unpack(p).items()})
    loss = pg[ROW_MISC, COL_LOSS]
    conv_g = lax.dynamic_slice_in_dim(pg[ROW_CONV:ROW_CONV + 6].reshape(CONV_W, CONV_CH), me * (CONV_CH // N_DEV),
                                      CONV_CH // N_DEV, axis=1)
    res = adamw(w2['gdn_conv_w'], m2['gdn_conv_w'], v2['gdn_conv_w'], [conv_g], "adamw_conv")
    out_g['gdn_conv_w'], out_d['gdn_conv_w'], out_m['gdn_conv_w'], out_v['gdn_conv_w'] = res

    done = sum(out_d[nm][0, 0] for nm in recv) + out_d['gdn_conv_w'][0, 0] + pd[0, 0]
    after = jnp.zeros((8, 128), F32) + done
    c_got = []
    for key, nm in (('c_gu', "reduce_ffn1_gu_wait"), ('c_down', "reduce_ffn1_down_wait")):
        (c_send, c_recv, c_src, c_land, _), tr = late[key]
        c_got += exchange_wait(c_send, c_recv, c_src, c_land, after, tr, nm)
    recv = dict(zip(['ffn1_w_gate', 'ffn1_w_up', 'ffn1_w_down'], c_got))
    for nm in recv:
        update(nm)

    def depth(t):
        return t[None]

    return (loss, grad_x[None], *[depth(out_g[nm]) for nm in WEIGHTS], *[depth(out_d[nm]) for nm in WEIGHTS],
            *[depth(out_m[nm]) for nm in WEIGHTS], *[depth(out_v[nm]) for nm in WEIGHTS])


def kernel(x, mem, ffn1_pre_norm, ffn1_w_gate, ffn1_w_up, ffn1_w_down, ffn1_post_norm, mix_pre_norm, w_in, fox_f_bias, gdn_conv_w, gdn_a_log, gdn_dt_bias, gdn_out_norm, w_out, mix_post_norm, mem_pre_norm, mem_kv_norm, mem_w_q, mem_w_kv, mem_w_o, mem_post_norm, ffn2_pre_norm, ffn2_w_gate, ffn2_w_up, ffn2_w_down, ffn2_post_norm, loss_target, m_ffn1_pre_norm, m_ffn1_w_gate, m_ffn1_w_up, m_ffn1_w_down, m_ffn1_post_norm, m_mix_pre_norm, m_w_in, m_fox_f_bias, m_gdn_conv_w, m_gdn_a_log, m_gdn_dt_bias, m_gdn_out_norm, m_w_out, m_mix_post_norm, m_mem_pre_norm, m_mem_kv_norm, m_mem_w_q, m_mem_w_kv, m_mem_w_o, m_mem_post_norm, m_ffn2_pre_norm, m_ffn2_w_gate, m_ffn2_w_up, m_ffn2_w_down, m_ffn2_post_norm, v_ffn1_pre_norm, v_ffn1_w_gate, v_ffn1_w_up, v_ffn1_w_down, v_ffn1_post_norm, v_mix_pre_norm, v_w_in, v_fox_f_bias, v_gdn_conv_w, v_gdn_a_log, v_gdn_dt_bias, v_gdn_out_norm, v_w_out, v_mix_post_norm, v_mem_pre_norm, v_mem_kv_norm, v_mem_w_q, v_mem_w_kv, v_mem_w_o, v_mem_post_norm, v_ffn2_pre_norm, v_ffn2_w_gate, v_ffn2_w_up, v_ffn2_w_down, v_ffn2_post_norm):
    return _step(dict(locals()))
```

```python
import functools

import jax
import jax.numpy as jnp
from jax import lax
from jax.experimental import pallas as pl
from jax.experimental.pallas import tpu as pltpu

F32 = jnp.float32
BF = jnp.bfloat16
HI = lax.Precision.HIGHEST
MESH = pl.DeviceIdType.MESH

N_DEV = 8
EPS = 1e-6
D_MODEL = 1024
D_FF = 2816
FF_SHARD = D_FF // N_DEV
FF_SHARD_PAD = 384
D_FF_PAD = FF_SHARD_PAD * N_DEV
FOX_HEADS, FOX_DH = 8, 64
GDN_HEADS, GDN_DH = 4, 128
GDN_CHUNK = 64
CONV_W = 4
MEM_HEADS, MEM_DH = 4, 256
IN_W = 3600
IN_SHARD = IN_W // N_DEV
IN_SHARD_PAD = 512
PROJ_W = 4096
SMALL_F, SMALL_B, SMALL_A = 0, 8, 12

ADAM_LR, ADAM_B1, ADAM_B2, ADAM_EPS, ADAM_WD, ADAM_STEP = 0.001, 0.9, 0.999, 1e-08, 0.01, 10

VMEM_LIMIT = 56 * 1024 * 1024


def _params(sem=None):
    return pltpu.CompilerParams(dimension_semantics=sem, vmem_limit_bytes=VMEM_LIMIT)


def _tile(n, pref, unit=128):
    if n <= pref:
        return n
    t = (pref // unit) * unit
    while t > unit and n % t:
        t -= unit
    assert n % t == 0, (n, pref)
    return t


@functools.partial(jax.custom_vjp, nondiff_argnums=(2, 3))
def bdot(a, b, ca, cb):
    return lax.dot_general(a.astype(BF), b.astype(BF), (((ca,), (cb,)), ((), ())), preferred_element_type=F32)


def _bdot_fwd(a, b, ca, cb):
    return bdot(a, b, ca, cb), (a, b)


def _bdot_bwd(ca, cb, res, g):
    a, b = res
    da = bdot(g, b, 1, 1 - cb) if ca == 1 else bdot(b, g, 1 - cb, 1)
    db = bdot(a, g, 1 - ca, 0) if cb == 0 else bdot(g, a, 0, 1 - ca)
    return da, db


bdot.defvjp(_bdot_fwd, _bdot_bwd)


def hdot(a, b):
    return jnp.dot(a, b, precision=HI, preferred_element_type=F32)


def mdot(a, b):
    return jnp.dot(a, b, precision=lax.Precision.HIGH, preferred_element_type=F32)


def _iota2(shape, dim):
    return lax.broadcasted_iota(jnp.int32, shape, dim)


def _sigmoid(x):
    return 1.0 / (1.0 + jnp.exp(-x))


def _silu(x):
    return x * _sigmoid(x)


def _softplus(x):
    return jnp.maximum(x, 0.0) + jnp.log(1.0 + jnp.exp(-jnp.abs(x)))


def _rms(x, gain):
    return x * lax.rsqrt(jnp.mean(x * x, axis=-1, keepdims=True) + EPS) * gain


def mm(a, b, *, name, ta=False, tb=False, out_dtype=F32, tm=1024, tn=1024, tk=1024, token=None):
    m, k = (a.shape[1], a.shape[0]) if ta else a.shape
    n = b.shape[0] if tb else b.shape[1]
    assert k == (b.shape[1] if tb else b.shape[0]), (a.shape, b.shape, ta, tb)
    tm, tn, tk = _tile(m, tm), _tile(n, tn), _tile(k, tk)
    nk = k // tk
    dims = (((0 if ta else 1,), (1 if tb else 0,)), ((), ()))

    def kern(a_ref, b_ref, *rest):
        o_ref, scratch = (rest[1], rest[2:]) if token is not None else (rest[0], rest[1:])

        def part():
            return lax.dot_general(a_ref[...].astype(BF), b_ref[...].astype(BF), dims, preferred_element_type=F32)

        if nk == 1:
            o_ref[...] = part().astype(o_ref.dtype)
            return
        acc_ref, = scratch
        kk = pl.program_id(2)

        @pl.when(kk == 0)
        def _():
            acc_ref[...] = part()

        @pl.when(kk > 0)
        def _():
            acc_ref[...] += part()

        @pl.when(kk == nk - 1)
        def _():
            o_ref[...] = acc_ref[...].astype(o_ref.dtype)

    a_spec = pl.BlockSpec((tk, tm), lambda i, j, kk: (kk, i)) if ta else pl.BlockSpec((tm, tk), lambda i, j, kk: (i, kk))
    b_spec = pl.BlockSpec((tn, tk), lambda i, j, kk: (j, kk)) if tb else pl.BlockSpec((tk, tn), lambda i, j, kk: (kk, j))
    return pl.pallas_call(
        kern, name=name, grid=(m // tm, n // tn, nk),
        in_specs=[a_spec, b_spec] + ([pl.BlockSpec((8, 128), lambda i, j, kk: (0, 0))] if token is not None else []),
        out_specs=pl.BlockSpec((tm, tn), lambda i, j, kk: (i, j)),
        out_shape=jax.ShapeDtypeStruct((m, n), out_dtype),
        scratch_shapes=[pltpu.VMEM((tm, tn), F32)] if nk > 1 else [],
        compiler_params=_params(("parallel", "parallel", "arbitrary")),
    )(*((a, b) if token is None else (a, b, token)))


def mm_swiglu(a, wgu, *, name):
    m, k = a.shape
    nh = wgu.shape[1] // 2
    tm, tn = _tile(m, 1024), _tile(nh, 512)
    nj = nh // tn

    def kern(a_ref, bg_ref, bu_ref, g_ref, u_ref, act_ref):
        av = a_ref[...]
        g = jnp.dot(av, bg_ref[...], preferred_element_type=F32).astype(BF)
        u = jnp.dot(av, bu_ref[...], preferred_element_type=F32).astype(BF)
        g_ref[...] = g
        u_ref[...] = u
        act_ref[...] = (_silu(g.astype(F32)) * u.astype(F32)).astype(BF)

    tile = pl.BlockSpec((tm, tn), lambda i, j: (i, j))
    out = jax.ShapeDtypeStruct((m, nh), BF)
    return pl.pallas_call(
        kern, name=name, grid=(m // tm, nj),
        in_specs=[pl.BlockSpec((tm, k), lambda i, j: (i, 0)), pl.BlockSpec((k, tn), lambda i, j: (0, j)),
                  pl.BlockSpec((k, tn), lambda i, j: (0, j + nj))],
        out_specs=[tile, tile, tile], out_shape=[out, out, out],
        compiler_params=_params(("parallel", "parallel")),
    )(a, wgu, wgu)


def mm_dswiglu(df, wd, gate, up, *, name, token=None):
    m, k = df.shape
    nh = wd.shape[0]
    tm, tn = _tile(m, 1024), _tile(nh, 512)

    def kern(df_ref, wd_ref, g_ref, u_ref, *rest):
        dg_ref, du_ref = rest[-2:]
        da = lax.dot_general(df_ref[...], wd_ref[...], (((1,), (1,)), ((), ())), preferred_element_type=F32)
        g, u = g_ref[...].astype(F32), u_ref[...].astype(F32)
        sg = _sigmoid(g)
        dg_ref[...] = (da * u * (sg * (1.0 + g * (1.0 - sg)))).astype(BF)
        du_ref[...] = (da * (g * sg)).astype(BF)

    tile = pl.BlockSpec((tm, tn), lambda i, j: (i, j))
    out = jax.ShapeDtypeStruct((m, nh), BF)
    extra = [pl.BlockSpec((8, 128), lambda i, j: (0, 0))] if token is not None else []
    return pl.pallas_call(
        kern, name=name, grid=(m // tm, nh // tn),
        in_specs=[pl.BlockSpec((tm, k), lambda i, j: (i, 0)), pl.BlockSpec((tn, k), lambda i, j: (j, 0)), tile, tile]
        + extra,
        out_specs=[tile, tile], out_shape=[out, out],
        compiler_params=_params(("parallel", "parallel")),
    )(*((df, wd, gate, up) if token is None else (df, wd, gate, up, token)))


def mm_pair(a1, a2, wgu, *, name, token=None):
    m, nh = a1.shape
    n = wgu.shape[0]
    tm, tn, tk = _tile(m, 1024), _tile(n, 1024), _tile(nh, 1024)
    nk = nh // tk
    nt = (((1,), (1,)), ((), ()))

    def kern(a1_ref, a2_ref, b1_ref, b2_ref, *rest):
        o_ref, acc_ref = rest[-2:]
        kk = pl.program_id(2)

        def part():
            return (lax.dot_general(a1_ref[...], b1_ref[...], nt, preferred_element_type=F32)
                    + lax.dot_general(a2_ref[...], b2_ref[...], nt, preferred_element_type=F32))

        @pl.when(kk == 0)
        def _():
            acc_ref[...] = part()

        @pl.when(kk > 0)
        def _():
            acc_ref[...] += part()

        @pl.when(kk == nk - 1)
        def _():
            o_ref[...] = acc_ref[...]

    a_spec = pl.BlockSpec((tm, tk), lambda i, j, kk: (i, kk))
    extra = [pl.BlockSpec((8, 128), lambda i, j, kk: (0, 0))] if token is not None else []
    return pl.pallas_call(
        kern, name=name, grid=(m // tm, n // tn, nk),
        in_specs=[a_spec, a_spec, pl.BlockSpec((tn, tk), lambda i, j, kk: (j, kk)),
                  pl.BlockSpec((tn, tk), lambda i, j, kk: (j, kk + nk))] + extra,
        out_specs=pl.BlockSpec((tm, tn), lambda i, j, kk: (i, j)),
        out_shape=jax.ShapeDtypeStruct((m, n), F32),
        scratch_shapes=[pltpu.VMEM((tm, tn), F32)],
        compiler_params=_params(("parallel", "parallel", "arbitrary")),
    )(*((a1, a2, wgu, wgu) if token is None else (a1, a2, wgu, wgu, token)))


def _row_spec(item, rows):
    if not isinstance(item, tuple):
        return item, pl.BlockSpec((rows, item.shape[1]), lambda i: (i, 0))
    if len(item) == 3:
        arr, w, c = item
        return arr, pl.BlockSpec((rows, w), lambda i: (i, c))
    arr, w, c, lead = item
    return arr, pl.BlockSpec((None, rows, w), lambda i: (lead, i, c))


def _whole_spec(item):
    if not isinstance(item, tuple):
        return item, pl.BlockSpec(item.shape, lambda i: (0,) * item.ndim)
    arr, w, c = item
    return arr, pl.BlockSpec((arr.shape[0], w), lambda i: (0, c))


def rowcall(body, tiled, whole, outs, accs=(), *, rows, total, name):
    rows = min(rows, total)
    assert total % rows == 0
    t_arr, t_spec = zip(*[_row_spec(t, rows) for t in tiled])
    w_arr, w_spec = zip(*[_whole_spec(w) for w in whole]) if whole else ((), ())
    nt, nw, no, na = len(t_arr), len(w_arr), len(outs), len(accs)

    def kern(*refs):
        vals = [r[...] for r in refs[:nt + nw]]
        res = body(*vals)
        if not isinstance(res, (tuple, list)):
            res = (res,)
        assert len(res) == no + na, (name, len(res), no, na)
        for r, v in zip(refs[nt + nw:nt + nw + no], res[:no]):
            r[...] = v.astype(r.dtype)
        if na:
            acc_refs = refs[nt + nw + no:]

            @pl.when(pl.program_id(0) == 0)
            def _():
                for r in acc_refs:
                    r[...] = jnp.zeros_like(r)

            for r, v in zip(acc_refs, res[no:]):
                r[...] += v

    out_shape = [jax.ShapeDtypeStruct((total, w), d) for w, d in outs] + [jax.ShapeDtypeStruct(s, F32) for s in accs]
    out_specs = [pl.BlockSpec((rows, w), lambda i: (i, 0)) for w, _ in outs] + \
                [pl.BlockSpec(s, lambda i: (0, 0)) for s in accs]
    res = pl.pallas_call(
        kern, name=name, grid=(total // rows,),
        in_specs=list(t_spec) + list(w_spec), out_specs=out_specs, out_shape=out_shape,
        compiler_params=_params(("arbitrary",) if na else ("parallel",)),
    )(*t_arr, *w_arr)
    return res


def _colsum(x):
    return jnp.sum(x, axis=0, keepdims=True)


def _gdn_chunk(q, k, v, z, gb, bb, state, gain):
    c = GDN_CHUNK
    nh = len(q)
    hs = range(nh)
    r64, c64 = _iota2((c, c), 0), _iota2((c, c), 1)
    incl = r64 >= c64
    strict = r64 > c64
    ltri = incl.astype(F32)
    utri = (r64 <= c64).astype(F32)
    eye = (r64 == c64).astype(F32)
    ones = jnp.ones((c, c), F32)
    pick = (_iota2((GDN_DH, c), 0) == _iota2((GDN_DH, c), 1)).astype(F32)
    last = (_iota2((c, GDN_DH), 0) == c - 1).astype(F32)

    qn = [q[h] * lax.rsqrt(jnp.sum(q[h] * q[h], axis=-1, keepdims=True) + EPS) * (GDN_DH ** -0.5) for h in hs]
    kn = [k[h] * lax.rsqrt(jnp.sum(k[h] * k[h], axis=-1, keepdims=True) + EPS) for h in hs]
    gc = [mdot(ltri, gb[h]) for h in hs]
    g64 = [mdot(gb[h], pick) for h in hs]
    gcol = [mdot(ltri, g64[h]) for h in hs]
    grow = [mdot(ones, g64[h] * utri) for h in hs]
    dec = [jnp.exp(jnp.where(incl, gcol[h] - grow[h], -1e30)) for h in hs]
    kb = [kn[h] * bb[h] for h in hs]
    vb = [v[h] * bb[h] for h in hs]
    kk = [bdot(kb[h], kn[h], 1, 1) for h in hs]
    p = [-jnp.where(strict, kk[h] * dec[h], 0.0) for h in hs]
    tinv = [eye + p[h] for h in hs]
    for level in range(5):
        dot = mdot if level < 2 else (lambda a, b: bdot(a, b, 1, 0))
        p = [dot(p[h], p[h]) for h in hs]
        tinv = [tinv[h] + dot(tinv[h], p[h]) for h in hs]
    egc = [jnp.exp(gc[h]) for h in hs]
    u = [mdot(tinv[h], vb[h]) for h in hs]
    w = [mdot(tinv[h], kb[h] * egc[h]) for h in hs]
    attn = [bdot(qn[h], kn[h], 1, 1) * dec[h] for h in hs]
    qd = [qn[h] * egc[h] for h in hs]
    gl = [jnp.sum(gc[h] * last, axis=0, keepdims=True) for h in hs]
    kt = [kn[h] * jnp.exp(gl[h] - gc[h]) for h in hs]
    ws = [bdot(w[h], state[h], 1, 0) for h in hs]
    qs = [bdot(qd[h], state[h], 1, 0) for h in hs]
    v_new = [u[h] - ws[h] for h in hs]
    av = [bdot(attn[h], v_new[h], 1, 0) for h in hs]
    kv = [bdot(kt[h], v_new[h], 0, 0) for h in hs]
    new_state = tuple(state[h] * jnp.exp(gl[h]) + kv[h] for h in hs)
    o = tuple(_rms(qs[h] + av[h], gain) * _silu(z[h]) for h in hs)
    return o, new_state


GDN_ROWS = 512
GDN_W = GDN_HEADS * GDN_DH


def gdn_fwd(cqkv, proj, gbb, gain):
    s = cqkv.shape[0]
    nb, cpb = s // GDN_ROWS, GDN_ROWS // GDN_CHUNK
    h4 = GDN_HEADS

    def kern(qkv_ref, z_ref, gb_ref, gain_ref, o_ref, st_ref, state):
        @pl.when(pl.program_id(0) == 0)
        def _():
            state[...] = jnp.zeros_like(state)

        gain_v = gain_ref[...]

        def step(ci, carry):
            sl = pl.ds(pl.multiple_of(ci * GDN_CHUNK, GDN_CHUNK), GDN_CHUNK)
            ins = []
            for h in range(h4):
                ln = lambda base, h=h: slice(base + h * GDN_DH, base + (h + 1) * GDN_DH)
                ins.append((qkv_ref[sl, ln(0)], qkv_ref[sl, ln(GDN_W)], qkv_ref[sl, ln(2 * GDN_W)], z_ref[sl, ln(0)],
                            gb_ref[sl, ln(0)], gb_ref[sl, ln(GDN_W)], state[h]))
            cols = [tuple(col) for col in zip(*ins)]
            o, new = _gdn_chunk(*cols[:7], gain_v)
            for h in range(h4):
                st_ref[h, ci] = ins[h][6]
                o_ref[sl, h * GDN_DH:(h + 1) * GDN_DH] = o[h]
                state[h] = new[h]
            return carry

        lax.fori_loop(0, cpb, step, 0)

    return pl.pallas_call(
        kern, name="gdn_fwd", grid=(nb,),
        in_specs=[pl.BlockSpec((GDN_ROWS, 3 * GDN_W), lambda i: (i, 0)),
                  pl.BlockSpec((GDN_ROWS, GDN_W), lambda i: (i, 6)),
                  pl.BlockSpec((GDN_ROWS, 2 * GDN_W), lambda i: (i, 0)),
                  pl.BlockSpec((1, GDN_DH), lambda i: (0, 0))],
        out_specs=[pl.BlockSpec((GDN_ROWS, GDN_W), lambda i: (i, 0)),
                   pl.BlockSpec((h4, cpb, GDN_DH, GDN_DH), lambda i: (0, i, 0, 0))],
        out_shape=[jax.ShapeDtypeStruct((s, GDN_W), F32),
                   jax.ShapeDtypeStruct((h4, s // GDN_CHUNK, GDN_DH, GDN_DH), F32)],
        scratch_shapes=[pltpu.VMEM((h4, GDN_DH, GDN_DH), F32)],
        compiler_params=_params(("arbitrary",)),
    )(cqkv, proj, gbb, gain)


def gdn_bwd(cqkv, proj, gbb, gain, states, d_mixed):
    s = cqkv.shape[0]
    nb, cpb = s // GDN_ROWS, GDN_ROWS // GDN_CHUNK
    h4 = GDN_HEADS

    def kern(qkv_ref, z_ref, gb_ref, gain_ref, st_ref, do_ref, dqkv_ref, dz_ref, dgb_ref, dgain_ref, dstate):
        @pl.when(pl.program_id(0) == 0)
        def _():
            dgain_ref[...] = jnp.zeros_like(dgain_ref)
            dstate[...] = jnp.zeros_like(dstate)

        gain_v = gain_ref[...]

        def step(t, carry):
            ci = cpb - 1 - t
            sl = pl.ds(pl.multiple_of(ci * GDN_CHUNK, GDN_CHUNK), GDN_CHUNK)
            prim, cot, dst_in = [], [], []
            for h in range(h4):
                ln = lambda base, h=h: slice(base + h * GDN_DH, base + (h + 1) * GDN_DH)
                prim.append((qkv_ref[sl, ln(0)], qkv_ref[sl, ln(GDN_W)], qkv_ref[sl, ln(2 * GDN_W)], z_ref[sl, ln(0)],
                             gb_ref[sl, ln(0)], gb_ref[sl, ln(GDN_W)], st_ref[h, ci]))
                cot.append(do_ref[sl, ln(0)])
                dst_in.append(dstate[h])
            cols = [tuple(col) for col in zip(*prim)]
            vjp = jax.vjp(_gdn_chunk, *cols, gain_v)[1]
            dq, dk, dv, dz, dg, db, dst, dgn = vjp((tuple(cot), tuple(dst_in)))
            for h in range(h4):
                ln = lambda base, h=h: slice(base + h * GDN_DH, base + (h + 1) * GDN_DH)
                dqkv_ref[sl, ln(0)] = dq[h]
                dqkv_ref[sl, ln(GDN_W)] = dk[h]
                dqkv_ref[sl, ln(2 * GDN_W)] = dv[h]
                dz_ref[sl, ln(0)] = dz[h]
                dgb_ref[sl, ln(0)] = dg[h]
                dgb_ref[sl, ln(GDN_W)] = db[h]
                dstate[h] = dst[h]
            dgain_ref[...] += dgn
            return carry

        lax.fori_loop(0, cpb, step, 0)

    def rev(width, cblock=0):
        return pl.BlockSpec((GDN_ROWS, width), lambda i: (nb - 1 - i, cblock))

    return pl.pallas_call(
        kern, name="gdn_bwd", grid=(nb,),
        in_specs=[rev(3 * GDN_W), rev(GDN_W, 6), rev(2 * GDN_W), pl.BlockSpec((1, GDN_DH), lambda i: (0, 0)),
                  pl.BlockSpec((h4, cpb, GDN_DH, GDN_DH), lambda i: (0, nb - 1 - i, 0, 0)), rev(GDN_W, 1)],
        out_specs=[rev(3 * GDN_W), rev(GDN_W), rev(2 * GDN_W), pl.BlockSpec((1, GDN_DH), lambda i: (0, 0))],
        out_shape=[jax.ShapeDtypeStruct((s, 3 * GDN_W), F32), jax.ShapeDtypeStruct((s, GDN_W), F32),
                   jax.ShapeDtypeStruct((s, 2 * GDN_W), F32), jax.ShapeDtypeStruct((1, GDN_DH), F32)],
        scratch_shapes=[pltpu.VMEM((h4, GDN_DH, GDN_DH), F32)],
        compiler_params=_params(("arbitrary",)),
    )(cqkv, proj, gbb, gain, states, d_mixed)


def _gdn_gates(small, prm):
    w = GDN_HEADS * GDN_DH
    lane, head = _iota2((128, w), 0), _iota2((128, w), 1) // GDN_DH
    sel_b = (lane == SMALL_B + head).astype(F32)
    sel_a = (lane == SMALL_A + head).astype(F32)
    prow = _iota2((8, 128), 0)
    a_log = jnp.sum(prm * (prow == 0).astype(F32), axis=0, keepdims=True)
    dt_b = jnp.sum(prm * (prow == 1).astype(F32), axis=0, keepdims=True)
    beta = _sigmoid(hdot(small, sel_b))
    g = hdot(-jnp.exp(a_log) * _softplus(small + dt_b), sel_a)
    return g, beta


CONV_ROWS = 1024
CONV_COLS = 128
CONV_BLOCK0 = 1536 // CONV_COLS


def _shift_down(prev8, cur, s):
    ext = jnp.concatenate([prev8, cur], axis=0)
    return pltpu.roll(ext, s, 0)[8:]


def _shift_up(cur, next8, s):
    n = cur.shape[0]
    ext = jnp.concatenate([cur, next8], axis=0)
    return pltpu.roll(ext, n + 8 - s, 0)[:n]


def _conv_pre(x_ref, w, ci, nchunk):
    r0 = pl.multiple_of(ci * CONV_ROWS, CONV_ROWS)
    cur = x_ref[pl.ds(r0, CONV_ROWS), :]
    prev = x_ref[pl.ds(pl.multiple_of(jnp.maximum(r0 - 8, 0), 8), 8), :]
    prev = jnp.where(ci > 0, prev, 0.0)
    shifted = [cur] + [_shift_down(prev, cur, s) for s in range(1, CONV_W)]
    pre = w[CONV_W - 1:CONV_W, :] * cur
    for s in range(1, CONV_W):
        pre = pre + w[CONV_W - 1 - s:CONV_W - s, :] * shifted[s]
    return r0, pre, shifted


def conv_fwd(proj, conv_w8):
    s = proj.shape[0]
    nchunk = s // CONV_ROWS
    ncol = 3 * GDN_HEADS * GDN_DH // CONV_COLS

    def kern(x_ref, w_ref, y_ref):
        w = w_ref[...]

        def step(ci, carry):
            r0, pre, _ = _conv_pre(x_ref, w, ci, nchunk)
            y_ref[pl.ds(r0, CONV_ROWS), :] = _silu(pre)
            return carry

        lax.fori_loop(0, nchunk, step, 0)

    return pl.pallas_call(
        kern, name="conv_fwd", grid=(ncol,),
        in_specs=[pl.BlockSpec((s, CONV_COLS), lambda j: (0, CONV_BLOCK0 + j)),
                  pl.BlockSpec((8, CONV_COLS), lambda j: (0, j))],
        out_specs=pl.BlockSpec((s, CONV_COLS), lambda j: (0, j)),
        out_shape=jax.ShapeDtypeStruct((s, ncol * CONV_COLS), F32),
        compiler_params=_params(("parallel",)),
    )(proj, conv_w8)


def conv_bwd(proj, conv_w8, dy):
    s = proj.shape[0]
    nchunk = s // CONV_ROWS
    per = 3 * GDN_HEADS * GDN_DH // CONV_COLS
    outs = []
    for part in range(1):
        def kern(x_ref, w_ref, dy_ref, dx_ref, dw_ref, dpre_ref):
            w = w_ref[...]
            rows8 = _iota2((8, CONV_COLS), 0)

            def step1(ci, dw):
                r0, pre, shifted = _conv_pre(x_ref, w, ci, nchunk)
                sg = _sigmoid(pre)
                dpre = dy_ref[pl.ds(r0, CONV_ROWS), :] * sg * (1.0 + pre * (1.0 - sg))
                dpre_ref[pl.ds(r0, CONV_ROWS), :] = dpre
                for sh in range(CONV_W):
                    dw = dw + jnp.where(rows8 == CONV_W - 1 - sh, _colsum(dpre * shifted[sh]), 0.0)
                return dw

            dw_ref[...] = lax.fori_loop(0, nchunk, step1, jnp.zeros((8, CONV_COLS), F32))

            def step2(ci, carry):
                r0 = pl.multiple_of(ci * CONV_ROWS, CONV_ROWS)
                cur = dpre_ref[pl.ds(r0, CONV_ROWS), :]
                nxt = dpre_ref[pl.ds(pl.multiple_of(jnp.minimum(r0 + CONV_ROWS, s - 8), 8), 8), :]
                nxt = jnp.where(ci < nchunk - 1, nxt, 0.0)
                dx = w[CONV_W - 1:CONV_W, :] * cur
                for sh in range(1, CONV_W):
                    dx = dx + w[CONV_W - 1 - sh:CONV_W - sh, :] * _shift_up(cur, nxt, sh)
                dx_ref[pl.ds(r0, CONV_ROWS), :] = dx
                return carry

            lax.fori_loop(0, nchunk, step2, 0)

        outs.append(pl.pallas_call(
            kern, name=f"conv_bwd{part}", grid=(per,),
            in_specs=[pl.BlockSpec((s, CONV_COLS), lambda j, part=part: (0, CONV_BLOCK0 + part * per + j)),
                      pl.BlockSpec((8, CONV_COLS), lambda j, part=part: (0, part * per + j)),
                      pl.BlockSpec((s, CONV_COLS), lambda j: (0, j))],
            out_specs=[pl.BlockSpec((s, CONV_COLS), lambda j: (0, j)),
                       pl.BlockSpec((8, CONV_COLS), lambda j: (0, j))],
            out_shape=[jax.ShapeDtypeStruct((s, per * CONV_COLS), F32),
                       jax.ShapeDtypeStruct((8, per * CONV_COLS), F32)],
            scratch_shapes=[pltpu.VMEM((s, CONV_COLS), F32)],
            compiler_params=_params(("parallel",)),
        )(proj, conv_w8, dy))
    dx = jnp.concatenate([o[0] for o in outs], axis=1)
    dw = jnp.concatenate([o[1] for o in outs], axis=1)
    return dx, dw


FOXF_ROWS = 512
SMALL_BLOCK128 = 3584 // 128


def _log_sigmoid(x):
    return jnp.minimum(x, 0.0) - jnp.log(1.0 + jnp.exp(-jnp.abs(x)))


def fox_f_fwd(proj, bias_row):
    s = proj.shape[0]
    n = s // FOXF_ROWS

    def kern(x_ref, b_ref, f_ref, carry):
        @pl.when(pl.program_id(0) == 0)
        def _():
            carry[...] = jnp.zeros_like(carry)

        heads = _iota2((FOXF_ROWS, 128), 1) < FOX_HEADS
        lf = jnp.where(heads, _log_sigmoid(x_ref[...] + b_ref[...]), 0.0)
        ltri = (_iota2((FOXF_ROWS, FOXF_ROWS), 0) >= _iota2((FOXF_ROWS, FOXF_ROWS), 1)).astype(F32)
        c = hdot(ltri, lf) + carry[...]
        f_ref[...] = c
        carry[...] = c[FOXF_ROWS - 1:FOXF_ROWS, :]

    return pl.pallas_call(
        kern, name="fox_f_fwd", grid=(n,),
        in_specs=[pl.BlockSpec((FOXF_ROWS, 128), lambda i: (i, SMALL_BLOCK128)),
                  pl.BlockSpec((1, 128), lambda i: (0, 0))],
        out_specs=pl.BlockSpec((FOXF_ROWS, 128), lambda i: (i, 0)),
        out_shape=jax.ShapeDtypeStruct((s, 128), F32),
        scratch_shapes=[pltpu.VMEM((1, 128), F32)],
        compiler_params=_params(("arbitrary",)),
    )(proj, bias_row)


def fox_f_bwd(proj, bias_row, d_f):
    s = proj.shape[0]
    n = s // FOXF_ROWS

    def kern(x_ref, b_ref, df_ref, dx_ref, db_ref, carry):
        @pl.when(pl.program_id(0) == 0)
        def _():
            carry[...] = jnp.zeros_like(carry)
            db_ref[...] = jnp.zeros_like(db_ref)

        heads = _iota2((FOXF_ROWS, 128), 1) < FOX_HEADS
        utri = (_iota2((FOXF_ROWS, FOXF_ROWS), 0) <= _iota2((FOXF_ROWS, FOXF_ROWS), 1)).astype(F32)
        rc = hdot(utri, df_ref[...]) + carry[...]
        carry[...] = rc[0:1, :]
        dx = jnp.where(heads, rc * _sigmoid(-(x_ref[...] + b_ref[...])), 0.0)
        dx_ref[...] = dx
        db_ref[...] += _colsum(dx)

    return pl.pallas_call(
        kern, name="fox_f_bwd", grid=(n,),
        in_specs=[pl.BlockSpec((FOXF_ROWS, 128), lambda i: (n - 1 - i, SMALL_BLOCK128)),
                  pl.BlockSpec((1, 128), lambda i: (0, 0)),
                  pl.BlockSpec((FOXF_ROWS, 128), lambda i: (n - 1 - i, 0))],
        out_specs=[pl.BlockSpec((FOXF_ROWS, 128), lambda i: (n - 1 - i, 0)),
                   pl.BlockSpec((1, 128), lambda i: (0, 0))],
        out_shape=[jax.ShapeDtypeStruct((s, 128), F32), jax.ShapeDtypeStruct((1, 128), F32)],
        scratch_shapes=[pltpu.VMEM((1, 128), F32)],
        compiler_params=_params(("arbitrary",)),
    )(proj, bias_row, d_f)


FOX_T = 512
FOX_SCALE = FOX_DH ** -0.5
FOX_PAIRS = FOX_HEADS // 2
NEG = -1e30
_NT = (((1,), (1,)), ((), ()))


def _split3(x):
    def bf(v):
        return lax.reduce_precision(v, exponent_bits=8, mantissa_bits=7)

    hi = bf(x)
    mid = bf(x - hi)
    lo = bf(x - hi - mid)
    return jnp.stack([hi, mid, lo], axis=-1)


def _fox_extras(s, first, second):
    def part(v):
        if v is None:
            return jnp.zeros((s, FOX_HEADS, 3), F32)
        if isinstance(v, float):
            return jnp.full((s, FOX_HEADS, 3), v, F32)
        pairs = v.reshape(s, FOX_PAIRS, 2)
        return _split3(jnp.stack([pairs[:, :, 1], pairs[:, :, 0]], axis=-1).reshape(s, FOX_HEADS))

    cols = jnp.concatenate([part(first), part(second)], axis=-1)
    cols = _pad_to(cols, (s, FOX_HEADS, FOX_DH)).reshape(s, FOX_PAIRS, 2 * FOX_DH)
    return cols.transpose(1, 0, 2).astype(BF)


def _head_masks(rows):
    lane = _iota2((rows, 2 * FOX_DH), 1)
    return lane < FOX_DH, lane >= FOX_DH


def _extra_lane(e, slot):
    return (FOX_DH if e == 0 else 0) + slot


def fox_fwd(qkv, xq, xk, xv):
    s = qkv.shape[0]
    t = min(FOX_T, s)
    n = s // t

    def kern(q_ref, k_ref, v_ref, xq_ref, xk_ref, xv_ref, o_ref, lse_ref):
        i = pl.program_id(1)
        masks = _head_masks(t)
        q_pair, x_pair = q_ref[...] * FOX_SCALE, xq_ref[...]
        q_ops = [jnp.where(mk, q_pair, x_pair) for mk in masks]

        def step(j, carry, masked):
            sl = pl.ds(pl.multiple_of(j * t, t), t)
            k_pair, xk_pair, v_pair, xv_pair = k_ref[sl, :], xk_ref[sl, :], v_ref[sl, :], xv_ref[sl, :]
            k_ops = [jnp.where(mk, k_pair, xk_pair) for mk in masks]
            v_ops = [jnp.where(mk, v_pair, xv_pair) for mk in masks]
            sc = [lax.dot_general(q_ops[e], k_ops[e], _NT, preferred_element_type=F32) for e in range(2)]
            if masked:
                keep = _iota2((t, t), 0) >= _iota2((t, t), 1)
                sc = [jnp.where(keep, x, NEG) for x in sc]
            m_new = [jnp.maximum(carry[e][0], jnp.max(sc[e], axis=1, keepdims=True)) for e in range(2)]
            p = [jnp.exp(sc[e] - m_new[e]).astype(BF) for e in range(2)]
            pv = [jnp.dot(p[e], v_ops[e], preferred_element_type=F32) for e in range(2)]
            return tuple((m_new[e], jnp.exp(carry[e][0] - m_new[e]) * carry[e][1] + pv[e]) for e in range(2))

        init = tuple((jnp.full((t, 1), NEG, F32), jnp.zeros((t, 2 * FOX_DH), F32)) for _ in range(2))
        carry = lax.fori_loop(0, i, lambda j, c: step(j, c, False), init)
        carry = step(i, carry, True)
        lane = _iota2((t, 2 * FOX_DH), 1)
        outs, lses = [], []
        for e in range(2):
            m, acc = carry[e]
            l = jnp.sum(jnp.where(lane == _extra_lane(e, 0), acc, 0.0), axis=1, keepdims=True)
            outs.append(acc / l)
            lses.append(m + jnp.log(l))
        o_ref[...] = jnp.where(masks[0], outs[0], outs[1])
        lse_ref[...] = jnp.where(lane == 0, lses[0], jnp.where(lane == 1, lses[1], 0.0))

    pr = FOX_PAIRS
    return pl.pallas_call(
        kern, name="fox_fwd", grid=(pr, n),
        in_specs=[pl.BlockSpec((t, 128), lambda p, i: (i, p)),
                  pl.BlockSpec((s, 128), lambda p, i: (0, pr + p)),
                  pl.BlockSpec((s, 128), lambda p, i: (0, 2 * pr + p)),
                  pl.BlockSpec((None, t, 128), lambda p, i: (p, i, 0)),
                  pl.BlockSpec((None, s, 128), lambda p, i: (p, 0, 0)),
                  pl.BlockSpec((None, s, 128), lambda p, i: (p, 0, 0))],
        out_specs=[pl.BlockSpec((t, 128), lambda p, i: (i, p)),
                   pl.BlockSpec((None, t, 128), lambda p, i: (p, i, 0))],
        out_shape=[jax.ShapeDtypeStruct((s, FOX_HEADS * FOX_DH), F32), jax.ShapeDtypeStruct((pr, s, 128), F32)],
        compiler_params=_params(("parallel", "parallel")),
    )(qkv, qkv, qkv, xq, xk, xv)


def fox_bwd(qkv, d_o, xk, xv, xqb, xdo):
    s = qkv.shape[0]
    t = min(FOX_T, s)
    n = s // t
    w = 2 * FOX_DH

    def both(blocks, slot):
        lane = _iota2(blocks[0].shape, 1)
        own = jnp.where(lane < FOX_DH, blocks[0], blocks[1])
        sums = [jnp.sum(jnp.where(lane == _extra_lane(e, slot), blocks[e], 0.0), axis=1, keepdims=True)
                for e in range(2)]
        return own, jnp.where(lane == 0, sums[0], jnp.where(lane == 1, sums[1], 0.0))

    def kern(k_ref, v_ref, xk_ref, xv_ref, q_ref, do_ref, xq_ref, xd_ref,
             dq_ref, dk_ref, dv_ref, sq_ref, sk_ref, dq_acc):
        j = pl.program_id(1)

        @pl.when(j == 0)
        def _():
            dq_acc[...] = jnp.zeros_like(dq_acc)

        masks = _head_masks(t)
        k_ops = [jnp.where(mk, k_ref[...], xk_ref[...]) for mk in masks]
        v_ops = [jnp.where(mk, v_ref[...], xv_ref[...]) for mk in masks]
        k_t = [x.T for x in k_ops]

        def step(i, carry, masked):
            dk, dv = carry
            sl = pl.ds(pl.multiple_of(i * t, t), t)
            q_pair, xq_pair, do_pair, xd_pair = q_ref[sl, :] * FOX_SCALE, xq_ref[sl, :], do_ref[sl, :], xd_ref[sl, :]
            q_ops = [jnp.where(mk, q_pair, xq_pair) for mk in masks]
            do_ops = [jnp.where(mk, do_pair, xd_pair) for mk in masks]
            q_t = [x.T for x in q_ops]
            do_t = [jnp.where(mk, do_pair, 0).astype(BF).T for mk in masks]
            st = [lax.dot_general(k_ops[e], q_ops[e], _NT, preferred_element_type=F32) for e in range(2)]
            dp = [lax.dot_general(v_ops[e], do_ops[e], _NT, preferred_element_type=F32) for e in range(2)]
            if masked:
                keep = _iota2((t, t), 0) <= _iota2((t, t), 1)
                st = [jnp.where(keep, x, NEG) for x in st]
            pt = [jnp.exp(x) for x in st]
            dsb = [(pt[e] * dp[e]).astype(BF) for e in range(2)]
            dv = dv + sum(lax.dot_general(do_t[e], pt[e].astype(BF), _NT, preferred_element_type=F32)
                          for e in range(2))
            dk = tuple(dk[e] + lax.dot_general(q_t[e], dsb[e], _NT, preferred_element_type=F32) for e in range(2))
            for e in range(2):
                dq_acc[i, e * w:(e + 1) * w, :] += jnp.dot(k_t[e], dsb[e], preferred_element_type=F32)
            return dk, dv

        init = ((jnp.zeros((w, t), F32), jnp.zeros((w, t), F32)), jnp.zeros((w, t), F32))
        carry = step(j, init, True)
        dk, dv = lax.fori_loop(j + 1, n, lambda i, c: step(i, c, False), carry)
        dk_ref[...], sk_ref[...] = both([x.T for x in dk], 3)
        dv_ref[...] = dv.T

        @pl.when(j == n - 1)
        def _():
            def out(r, carry):
                sl = pl.ds(pl.multiple_of(r * t, t), t)
                own, sums = both([dq_acc[r, e * w:(e + 1) * w, :].T for e in range(2)], 0)
                dq_ref[sl, :] = own * FOX_SCALE
                sq_ref[sl, :] = sums
                return carry

            lax.fori_loop(0, n, out, 0)

    pr = FOX_PAIRS
    flat = jax.ShapeDtypeStruct((s, FOX_HEADS * FOX_DH), F32)
    tile = pl.BlockSpec((t, 128), lambda p, j: (j, p))
    whole = pl.BlockSpec((s, 128), lambda p, j: (0, p))
    return pl.pallas_call(
        kern, name="fox_bwd", grid=(pr, n),
        in_specs=[pl.BlockSpec((t, 128), lambda p, j: (j, pr + p)),
                  pl.BlockSpec((t, 128), lambda p, j: (j, 2 * pr + p)),
                  pl.BlockSpec((None, t, 128), lambda p, j: (p, j, 0)),
                  pl.BlockSpec((None, t, 128), lambda p, j: (p, j, 0)),
                  whole, whole,
                  pl.BlockSpec((None, s, 128), lambda p, j: (p, 0, 0)),
                  pl.BlockSpec((None, s, 128), lambda p, j: (p, 0, 0))],
        out_specs=[whole, tile, tile, whole, tile],
        out_shape=[flat] * 5,
        scratch_shapes=[pltpu.VMEM((n, 2 * w, t), F32)],
        compiler_params=_params(("parallel", "arbitrary")),
    )(qkv, qkv, xk, xv, qkv, d_o, xqb, xdo)


def _xattn_head(q, k, v):
    sc = bdot(q, k, 1, 1) * (MEM_DH ** -0.5)
    e = jnp.exp(sc - lax.stop_gradient(jnp.max(sc, axis=-1, keepdims=True)))
    p = e / jnp.sum(e, axis=-1, keepdims=True)
    return bdot(p, v, 1, 0)


def xattn_fwd(q, kv):
    s = q.shape[0]
    hh = MEM_HEADS

    def body(*vals):
        qs, ks, vs = vals[:hh], vals[hh:2 * hh], vals[2 * hh:]
        return jnp.concatenate([_xattn_head(qs[a], ks[a], vs[a]) for a in range(hh)], axis=1)

    return rowcall(body, [(q, MEM_DH, a) for a in range(hh)],
                   [(kv, MEM_DH, a) for a in range(2 * hh)],
                   [(hh * MEM_DH, BF)], rows=512, total=s, name="xattn_fwd")[0]


def xattn_bwd(q, kv, d_o):
    s = q.shape[0]
    hh = MEM_HEADS

    def body(*vals):
        qs, dos = vals[:hh], vals[hh:2 * hh]
        ks, vs = vals[2 * hh:3 * hh], vals[3 * hh:]
        dqs, dks, dvs = [], [], []
        for a in range(hh):
            _, vjp = jax.vjp(_xattn_head, qs[a], ks[a], vs[a])
            dq, dk, dv = vjp(dos[a])
            dqs.append(dq)
            dks.append(dk)
            dvs.append(dv)
        return jnp.concatenate(dqs, axis=1), jnp.concatenate(dks + dvs, axis=1)

    return rowcall(body, [(q, MEM_DH, a) for a in range(hh)] + [(d_o, MEM_DH, a) for a in range(hh)],
                   [(kv, MEM_DH, a) for a in range(2 * hh)],
                   [(hh * MEM_DH, BF)], [kv.shape], rows=512, total=s, name="xattn_bwd")


def _slab(ref, axis, start, size):
    if axis is None:
        return ref
    if axis == "lead":
        return ref.at[start]
    idx = pl.ds(pl.multiple_of(start, 128 if axis == 1 else 16), size)
    return ref.at[idx] if axis == 0 else ref.at[:, idx]


def exchange(inputs, outputs, transfers, name):
    ni, no, nt = len(inputs), len(outputs), len(transfers)
    npeer = N_DEV - 1

    def body(*refs):
        ins, outs = refs[:ni], refs[ni:ni + no]
        send, recv, loc = refs[ni + no:]
        x, y, c = lax.axis_index("x"), lax.axis_index("y"), lax.axis_index("c")
        me = 4 * x + 2 * y + c

        def peer(p):
            px = 1 - x if p & 4 else x
            py = 1 - y if p & 2 else y
            pc = 1 - c if p & 1 else c
            return (px, py, pc), 4 * px + 2 * py + pc

        def view(ref, spec, who):
            axis, off, stride, size = spec
            return _slab(ref, axis, off + who * stride, size)

        local, remote = [], []
        for w, (ii, src, oi, dst) in enumerate(transfers):
            cp = pltpu.make_async_copy(view(ins[ii], src, me), view(outs[oi], dst, me), loc.at[w])
            cp.start()
            local.append(cp)
        for p in range(1, N_DEV):
            dev, idx = peer(p)
            for w, (ii, src, oi, dst) in enumerate(transfers):
                k = w * npeer + p - 1
                out_cp = pltpu.make_async_remote_copy(
                    src_ref=view(ins[ii], src, idx), dst_ref=view(outs[oi], dst, me), send_sem=send.at[k],
                    recv_sem=recv.at[k], device_id=dev, device_id_type=MESH)
                out_cp.start()
                in_cp = pltpu.make_async_remote_copy(
                    src_ref=view(ins[ii], src, idx), dst_ref=view(outs[oi], dst, idx), send_sem=send.at[k],
                    recv_sem=recv.at[k], device_id=dev, device_id_type=MESH)
                remote.append((out_cp, in_cp))
        for out_cp, in_cp in remote:
            in_cp.wait_recv()
            out_cp.wait_send()
        for cp in local:
            cp.wait()

    hbm = pl.BlockSpec(memory_space=pl.ANY)
    return pl.pallas_call(
        body, name=name, in_specs=[hbm] * ni, out_specs=[hbm] * no, out_shape=list(outputs),
        scratch_shapes=[pltpu.SemaphoreType.DMA((nt * npeer,)), pltpu.SemaphoreType.DMA((nt * npeer,)),
                        pltpu.SemaphoreType.DMA((nt,))],
        compiler_params=pltpu.CompilerParams(has_side_effects=True),
    )(*inputs)


def _peer(p):
    x, y, c = lax.axis_index("x"), lax.axis_index("y"), lax.axis_index("c")
    px = 1 - x if p & 4 else x
    py = 1 - y if p & 2 else y
    pc = 1 - c if p & 1 else c
    return (px, py, pc), 4 * px + 2 * py + pc


def _view(ref, spec, who):
    axis, off, stride, size = spec
    return _slab(ref, axis, off + who * stride, size)


def place_own(inputs, outputs, transfers):
    me = 4 * lax.axis_index("x") + 2 * lax.axis_index("y") + lax.axis_index("c")
    lands = [lax.empty(o.shape, o.dtype) for o in outputs]
    for ii, src, oi, dst in transfers:
        axis, off, stride, size = src
        own = inputs[ii] if axis is None else lax.dynamic_slice_in_dim(inputs[ii], off + me * stride, size, axis)
        axis, off, stride, size = dst
        if axis == "lead":
            lands[oi] = lax.dynamic_update_slice_in_dim(lands[oi], own[None], me, 0)
        else:
            lands[oi] = lax.dynamic_update_slice_in_dim(lands[oi], own, off + me * stride, axis)
    return lands


_HBM = pl.BlockSpec(memory_space=pltpu.HBM)
_SEM = pl.BlockSpec(memory_space=pltpu.SEMAPHORE)
_EFFECT = pltpu.SideEffectType.DATAFLOW_SIDE_EFFECTING


def _remote_copies(ins, lands, transfers, send, recv):
    npeer = N_DEV - 1
    me = 4 * lax.axis_index("x") + 2 * lax.axis_index("y") + lax.axis_index("c")
    pairs = []
    for p in range(1, N_DEV):
        dev, idx = _peer(p)
        for w, (ii, src, oi, dst) in enumerate(transfers):
            k = w * npeer + p - 1
            common = dict(src_ref=_view(ins[ii], src, idx), send_sem=send.at[k], recv_sem=recv.at[k],
                          device_id=dev, device_id_type=MESH)
            pairs.append((pltpu.make_async_remote_copy(dst_ref=_view(lands[oi], dst, me), **common),
                          pltpu.make_async_remote_copy(dst_ref=_view(lands[oi], dst, idx), **common)))
    return pairs


def exchange_start(inputs, lands, transfers, after, name):
    ni, nl, nsem = len(inputs), len(lands), len(transfers) * (N_DEV - 1)

    def body(*refs):
        ins, lnd = refs[:ni], refs[ni:ni + nl]
        send, recv = refs[ni + nl + 1], refs[ni + nl + 2]
        token = refs[-1]
        for out_cp, _ in _remote_copies(ins, lnd, transfers, send, recv):
            out_cp.start()
        token[...] = jnp.zeros_like(token)

    args = [pltpu.with_memory_space_constraint(a, pltpu.HBM) for a in list(inputs) + list(lands)]
    res = pl.pallas_call(
        body, name=name,
        out_shape=(pltpu.SemaphoreType.DMA((nsem,)), pltpu.SemaphoreType.DMA((nsem,)),
                   *[pltpu.HBM(a.shape, a.dtype) for a in args], jax.ShapeDtypeStruct((8, 128), F32)),
        in_specs=[_HBM] * (ni + nl) + [pl.BlockSpec(memory_space=pl.ANY)],
        out_specs=(_SEM, _SEM, *[_HBM] * (ni + nl), pl.BlockSpec(memory_space=pltpu.VMEM)),
        input_output_aliases={k: k + 2 for k in range(ni + nl)},
        compiler_params=pltpu.CompilerParams(has_side_effects=_EFFECT),
    )(*args, after)
    return res[0], res[1], list(res[2:2 + ni]), list(res[2 + ni:2 + ni + nl]), res[-1]


def exchange_wait(send, recv, inputs, lands, after, transfers, name):
    ni, nl = len(inputs), len(lands)

    def body(*refs):
        ins, lnd = refs[:ni], refs[ni:ni + nl]
        send_r, recv_r = refs[ni + nl], refs[ni + nl + 1]
        for out_cp, in_cp in _remote_copies(ins, lnd, transfers, send_r, recv_r):
            out_cp.wait_send()
            in_cp.wait_recv()

    res = pl.pallas_call(
        body, name=name,
        out_shape=tuple(pltpu.HBM(a.shape, a.dtype) for a in list(inputs) + list(lands)),
        in_specs=[_HBM] * (ni + nl) + [_SEM, _SEM, pl.BlockSpec(memory_space=pl.ANY)],
        out_specs=tuple([_HBM] * (ni + nl)),
        input_output_aliases={k: k for k in range(ni + nl)},
        compiler_params=pltpu.CompilerParams(has_side_effects=_EFFECT),
    )(*inputs, *lands, send, recv, after)
    return list(res[ni:])


def adamw(w, m, v, contribs, name):
    r, c = w.shape
    nc = len(contribs)
    rows = next((r // d for d in (4, 2) if r % d == 0 and (r // d) % 16 == 0), r)
    c1, c2 = 1.0 - ADAM_B1 ** ADAM_STEP, 1.0 - ADAM_B2 ** ADAM_STEP

    def body(wv, mv, vv, *gs):
        g = gs[0].astype(F32)
        for extra in gs[1:]:
            g = g + extra.astype(F32)
        g = g[:, :c]
        m_new = ADAM_B1 * mv + (1.0 - ADAM_B1) * g
        v_new = ADAM_B2 * vv + (1.0 - ADAM_B2) * (g * g)
        delta = -ADAM_LR * ((m_new / c1) / (jnp.sqrt(v_new / c2) + ADAM_EPS) + ADAM_WD * wv)
        return g, delta, m_new, v_new

    assert nc >= 1
    return rowcall(body, [w, m, v] + list(contribs), [], [(c, F32)] * 4, rows=rows, total=r, name=name)


WEIGHTS = ['ffn1_pre_norm', 'ffn1_w_gate', 'ffn1_w_up', 'ffn1_w_down', 'ffn1_post_norm', 'mix_pre_norm', 'w_in',
           'fox_f_bias', 'gdn_conv_w', 'gdn_a_log', 'gdn_dt_bias', 'gdn_out_norm', 'w_out', 'mix_post_norm',
           'mem_pre_norm', 'mem_kv_norm', 'mem_w_q', 'mem_w_kv', 'mem_w_o', 'mem_post_norm', 'ffn2_pre_norm',
           'ffn2_w_gate', 'ffn2_w_up', 'ffn2_w_down', 'ffn2_post_norm']
GAINS = ['ffn1_pre_norm', 'ffn1_post_norm', 'mix_pre_norm', 'mix_post_norm', 'mem_pre_norm', 'mem_kv_norm',
         'mem_post_norm', 'ffn2_pre_norm', 'ffn2_post_norm']
BIG = ['ffn1_w_gate', 'ffn1_w_up', 'ffn1_w_down', 'w_in', 'w_out', 'mem_w_q', 'mem_w_kv', 'mem_w_o',
       'ffn2_w_gate', 'ffn2_w_up', 'ffn2_w_down']
PACK_ROWS = 24
ROW_MISC = len(GAINS)
ROW_CONV = ROW_MISC + 1
COL_FBIAS, COL_ALOG, COL_DTB, COL_ONORM, COL_LOSS = 0, 8, 12, 128, 256
CONV_CH = 3 * GDN_HEADS * GDN_DH


def _pad_to(a, shape):
    return jnp.pad(a, [(0, t - s) for s, t in zip(a.shape, shape)])


def _pack(get, conv=None, loss=None):
    rows = [get(nm) for nm in GAINS]
    misc = jnp.concatenate([get('fox_f_bias'), get('gdn_a_log'), get('gdn_dt_bias'),
                            jnp.zeros((1, COL_ONORM - COL_DTB - 4), F32), get('gdn_out_norm'),
                            jnp.zeros((1, 1), F32) if loss is None else loss.reshape(1, 1)], axis=1)
    rows.append(_pad_to(misc, (1, D_MODEL)))
    rows.append(jnp.zeros((6, D_MODEL), F32) if conv is None else conv.reshape(6, D_MODEL))
    return _pad_to(jnp.concatenate(rows, axis=0), (PACK_ROWS, D_MODEL))


def _unpack(p):
    out = {nm: p[i:i + 1] for i, nm in enumerate(GAINS)}
    misc = p[ROW_MISC:ROW_MISC + 1]
    out['fox_f_bias'] = misc[:, COL_FBIAS:COL_FBIAS + FOX_HEADS]
    out['gdn_a_log'] = misc[:, COL_ALOG:COL_ALOG + GDN_HEADS]
    out['gdn_dt_bias'] = misc[:, COL_DTB:COL_DTB + GDN_HEADS]
    out['gdn_out_norm'] = misc[:, COL_ONORM:COL_ONORM + GDN_DH]
    return out


def _ffn_fwd(h, pre, wgu, wd, tag):
    s = h.shape[0]
    u, = rowcall(_rms, [h], [pre], [(D_MODEL, BF)], rows=512, total=s, name=tag + "_pre")
    gate, up, act = mm_swiglu(u, wgu, name=tag + "_gate_up")
    if callable(wd):
        wd = wd(act)
    f = mm(act, wd, name=tag + "_down")
    return u, gate, up, act, f


def _half_rms(a, g):
    return 0.5 * _rms(a, g)


def _ffn_bwd(dh_out, h, pre, post, wgu, wd, saved, tag, on_dwd=None, on_dwgu=None):
    u, gate, up, act, f = saved
    s = h.shape[0]

    def b_post(dh, fv, pg):
        return jax.vjp(_half_rms, fv, pg)[1](dh)

    df, dpost = rowcall(b_post, [dh_out, f], [post], [(D_MODEL, BF)], [(1, D_MODEL)], rows=512, total=s,
                        name=tag + "_bwd_post")
    dwd = mm(act, df, ta=True, out_dtype=BF, name=tag + "_bwd_dwd")
    dgate, dup = mm_dswiglu(df, wd, gate, up, name=tag + "_bwd_dact", token=on_dwd(dwd) if on_dwd else None)
    dwg = mm(u, dgate, ta=True, out_dtype=BF, name=tag + "_bwd_dwg")
    dwu = mm(u, dup, ta=True, out_dtype=BF, name=tag + "_bwd_dwu")
    du = mm_pair(dgate, dup, wgu, name=tag + "_bwd_du", token=on_dwgu(dwg, dwu) if on_dwgu else None)

    def b_pre(dh, duv, hv, pg):
        dx, dpre = jax.vjp(_rms, hv, pg)[1](duv)
        return dh + dx, dpre

    dh, dpre = rowcall(b_pre, [dh_out, du, h], [pre], [(D_MODEL, F32)], [(1, D_MODEL)], rows=512, total=s,
                       name=tag + "_bwd_pre")
    return dh, dwg, dwu, dwd, dpre, dpost


def _residual_rms(h, a, g):
    return h + _rms(a, g)


def _bwd_residual(dh, a, g):
    return jax.vjp(_rms, a, g)[1](dh)


def _step(a):
    x, mem = a['x'][0], a['mem'][0]
    s = x.shape[0]
    me = 4 * lax.axis_index("x") + 2 * lax.axis_index("y") + lax.axis_index("c")
    w2 = {nm: a[nm][0] for nm in WEIGHTS}
    m2 = {nm: a['m_' + nm][0] for nm in WEIGHTS}
    v2 = {nm: a['v_' + nm][0] for nm in WEIGHTS}
    small = {nm: w2[nm][None] for nm in WEIGHTS if nm not in BIG and nm != 'gdn_conv_w'}

    def ff_cols(w):
        return _pad_to(w, (D_MODEL, FF_SHARD_PAD)).astype(BF)

    def ff_rows(w):
        return _pad_to(w, (FF_SHARD_PAD, D_MODEL)).astype(BF)

    whole = (None, 0, 0, 0)
    conv_pad = 256
    g_in = [ff_cols(w2['ffn1_w_gate']), ff_cols(w2['ffn1_w_up']), ff_rows(w2['ffn1_w_down']),
            ff_cols(w2['ffn2_w_gate']), ff_cols(w2['ffn2_w_up']), ff_rows(w2['ffn2_w_down']),
            _pad_to(w2['w_in'], (D_MODEL, IN_SHARD_PAD)).astype(BF), w2['w_out'].astype(BF),
            w2['mem_w_q'].astype(BF), w2['mem_w_kv'].astype(BF), w2['mem_w_o'].astype(BF),
            _pad_to(w2['gdn_conv_w'], (8, conv_pad))]
    g_out = [jax.ShapeDtypeStruct((D_MODEL, 2 * D_FF_PAD), BF), jax.ShapeDtypeStruct((D_FF_PAD, D_MODEL), BF),
             jax.ShapeDtypeStruct((D_MODEL, 2 * D_FF_PAD), BF), jax.ShapeDtypeStruct((D_FF_PAD, D_MODEL), BF),
             jax.ShapeDtypeStruct((D_MODEL, N_DEV * IN_SHARD_PAD), BF), jax.ShapeDtypeStruct((D_MODEL, D_MODEL), BF),
             jax.ShapeDtypeStruct((D_MODEL, D_MODEL), BF), jax.ShapeDtypeStruct((D_MODEL, 2 * D_MODEL), BF),
             jax.ShapeDtypeStruct((D_MODEL, D_MODEL), BF), jax.ShapeDtypeStruct((8, N_DEV * conv_pad), F32)]
    sp_, dm = FF_SHARD_PAD, D_MODEL // N_DEV
    g_tr = [(0, whole, 0, (1, 0, sp_, sp_)), (1, whole, 0, (1, D_FF_PAD, sp_, sp_)), (2, whole, 1, (0, 0, sp_, sp_)),
            (3, whole, 2, (1, 0, sp_, sp_)), (4, whole, 2, (1, D_FF_PAD, sp_, sp_)), (5, whole, 3, (0, 0, sp_, sp_)),
            (6, whole, 4, (1, 0, IN_SHARD_PAD, IN_SHARD_PAD)), (7, whole, 5, (0, 0, dm, dm)),
            (8, whole, 6, (0, 0, dm, dm)), (9, whole, 7, (1, 0, 2 * dm, 2 * dm)), (10, whole, 8, (0, 0, dm, dm)),
            (11, whole, 9, (1, 0, conv_pad, conv_pad))]
    def pick(idx):
        ins = sorted({g_tr[k][0] for k in idx})
        outs = sorted({g_tr[k][2] for k in idx})
        tr = [(ins.index(g_tr[k][0]), g_tr[k][1], outs.index(g_tr[k][2]), g_tr[k][3]) for k in idx]
        return [g_in[i] for i in ins], [g_out[o] for o in outs], tr

    gu_in, gu_out, gu_tr = pick([0, 1])
    wgu1, = exchange(gu_in, gu_out, gu_tr, "gather_gate_up")
    stages, after = [], wgu1
    for nm, idx in (("down", [2]), ("mix", [6, 7, 11]), ("late", [8, 9, 10, 3, 4, 5])):
        st_in, st_out, st_tr = pick(idx)
        st = exchange_start(st_in, place_own(st_in, st_out, st_tr), st_tr, after, "gather_%s_start" % nm)
        stages.append((st, st_tr, "gather_%s_wait" % nm))
        after = st[4]
    g_token = after

    def gather_wait(k, after_):
        (send_, recv_, src_, land_, _), tr_, nm_ = stages[k]
        return exchange_wait(send_, recv_, src_, land_, after_, tr_, nm_)

    bias_row = _pad_to(small['fox_f_bias'], (1, 128))
    gate_prm = _pad_to(jnp.concatenate([_pad_to(small['gdn_a_log'], (1, 128 - SMALL_A)),
                                        _pad_to(small['gdn_dt_bias'], (1, 128 - SMALL_A))], axis=0),
                       (8, 128 - SMALL_A))
    gate_prm = jnp.pad(gate_prm, ((0, 0), (SMALL_A, 0)))
    onorm = small['gdn_out_norm']

    late = {}

    def wd1_when(act):
        late['wd1'], = gather_wait(0, act)
        return late['wd1']

    sv1 = _ffn_fwd(x, small['ffn1_pre_norm'] + g_token[0, 0], wgu1, wd1_when, "ffn1")
    wd1 = late['wd1']
    h1, = rowcall(lambda h, f, g: h + _half_rms(f, g), [x, sv1[4]], [small['ffn1_post_norm']], [(D_MODEL, F32)],
                  rows=512, total=s, name="ffn1_out")
    w_in_g, w_out, conv_g = gather_wait(1, h1)
    w_in = jnp.concatenate([w_in_g[:, j * IN_SHARD_PAD:j * IN_SHARD_PAD + IN_SHARD] for j in range(N_DEV)],
                           axis=1)
    sp = [0, 512, 1024, 1536, 1544, 2056, 2568, 3080, 3592, 3596, 3600]
    fq, fk, fv, ff, gq, gk, gv, gz, gb, ga = [w_in[:, sp[i]:sp[i + 1]] for i in range(10)]
    w_proj = jnp.concatenate([fq, fk, fv, gq, gk, gv, gz, ff, gb, ga,
                              jnp.zeros((D_MODEL, PROJ_W - 3584 - 16), BF)], axis=1)
    conv_w8 = conv_g.reshape(8, N_DEV, conv_pad)[:, :, :CONV_CH // N_DEV].reshape(8, CONV_CH)


    u2, = rowcall(_rms, [h1], [small['mix_pre_norm']], [(D_MODEL, BF)], rows=512, total=s, name="mix_pre")
    proj = mm(u2, w_proj, name="mix_proj")
    f_cum = fox_f_fwd(proj, bias_row)
    f_heads = f_cum[:, :FOX_HEADS]
    qkv_bf = proj[:, :3 * FOX_HEADS * FOX_DH].astype(BF)
    xk, xv = _fox_extras(s, 1.0, -f_heads), _fox_extras(s, 1.0, None)
    fox_flat, lse = fox_fwd(qkv_bf, _fox_extras(s, f_heads, 1.0), xk, xv)
    lse_heads = lse[:, :, :2].transpose(1, 0, 2).reshape(s, FOX_HEADS)
    cqkv = conv_fwd(proj, conv_w8)
    g_l, b_l = rowcall(_gdn_gates, [(proj, 128, SMALL_BLOCK128)], [gate_prm], [(512, F32), (512, F32)],
                       rows=512, total=s, name="gdn_gates")
    gbb = jnp.concatenate([g_l, b_l], axis=1)
    gdn_o, states = gdn_fwd(cqkv, proj, gbb, onorm)
    mixed = jnp.concatenate([fox_flat, gdn_o], axis=1).astype(BF)
    mo = mm(mixed, w_out, name="mix_out")
    h2, = rowcall(_residual_rms, [h1, mo], [small['mix_post_norm']], [(D_MODEL, F32)], rows=512, total=s,
                  name="mix_res")

    hq, = rowcall(_rms, [h2], [small['mem_pre_norm']], [(D_MODEL, BF)], rows=512, total=s, name="mem_pre")
    mn, = rowcall(_rms, [mem], [small['mem_kv_norm']], [(D_MODEL, BF)], rows=256, total=mem.shape[0], name="mem_kvn")
    wgu2, wd2, w_q, w_kv, w_o = gather_wait(2, h2)
    q_mem = mm(hq, w_q, name="mem_q")
    kv_mem = mm(mn, w_kv, name="mem_kv")
    o_mem = xattn_fwd(q_mem, kv_mem)
    c_mem = mm(o_mem, w_o, name="mem_o")
    h3, = rowcall(_residual_rms, [h2, c_mem], [small['mem_post_norm']], [(D_MODEL, F32)], rows=512, total=s,
                  name="mem_res")

    sv2 = _ffn_fwd(h3, small['ffn2_pre_norm'], wgu2, wd2, "ffn2")

    def b_loss(h, f, tgt, g):
        err = h + _half_rms(f, g) - tgt
        part = 0.5 * jnp.sum(jnp.mean(err * err, axis=-1, keepdims=True), axis=0, keepdims=True)
        return err * (1.0 / D_MODEL), jnp.broadcast_to(part, (1, 128))

    dy, loss_acc = rowcall(b_loss, [h3, sv2[4], a['loss_target'][0]], [small['ffn2_post_norm']], [(D_MODEL, F32)],
                           [(1, 128)], rows=512, total=s, name="loss")

    grads = {}
    dh3, dwg2, dwu2, dwd2, grads['ffn2_pre_norm'], grads['ffn2_post_norm'] = _ffn_bwd(
        dy, h3, small['ffn2_pre_norm'], small['ffn2_post_norm'], wgu2, wd2, sv2, "ffn2")

    lead = ("lead", 0, 1, 0)

    def land(r, c, dt=BF):
        return jax.ShapeDtypeStruct((N_DEV, r, c), dt)

    ffn_tr = [(0, (1, 0, sp_, sp_), 0, lead), (1, (1, 0, sp_, sp_), 1, lead), (2, (0, 0, sp_, FF_SHARD), 2, lead)]
    ffn_land = [land(D_MODEL, sp_), land(D_MODEL, sp_), land(FF_SHARD, D_MODEL)]
    a_in = [dwg2, dwu2, dwd2]
    a_send, a_recv, a_src, a_land, a_token = exchange_start(a_in, place_own(a_in, ffn_land, ffn_tr), ffn_tr, dh3,
                                                            "reduce_ffn2_start")

    dc, grads['mem_post_norm'] = rowcall(_bwd_residual, [dh3, c_mem], [small['mem_post_norm'] + a_token[0, 0]],
                                         [(D_MODEL, BF)],
                                         [(1, D_MODEL)], rows=512, total=s, name="mem_bwd_res")
    d_o = mm(dc, w_o, tb=True, name="mem_bwd_do")
    dw_o = mm(o_mem, dc, ta=True, out_dtype=BF, name="mem_bwd_dwo")
    dq_mem, dkv = xattn_bwd(q_mem, kv_mem, d_o)
    dhq = mm(dq_mem, w_q, tb=True, name="mem_bwd_dhq")
    dw_q = mm(hq, dq_mem, ta=True, out_dtype=BF, name="mem_bwd_dwq")
    dmn = mm(dkv, w_kv, tb=True, name="mem_bwd_dmn")
    dw_kv = mm(mn, dkv, ta=True, out_dtype=BF, name="mem_bwd_dwkv")
    _, grads['mem_kv_norm'] = rowcall(lambda d, mv, g: jax.vjp(_rms, mv, g)[1](d), [dmn, mem],
                                      [small['mem_kv_norm']], [(D_MODEL, F32)], [(1, D_MODEL)], rows=256,
                                      total=mem.shape[0], name="mem_bwd_kvn")

    def b_pre(dh, duv, hv, pg):
        dx, dpre = jax.vjp(_rms, hv, pg)[1](duv)
        return dh + dx, dpre

    dh2, grads['mem_pre_norm'] = rowcall(b_pre, [dh3, dhq, h2], [small['mem_pre_norm']], [(D_MODEL, F32)],
                                         [(1, D_MODEL)], rows=512, total=s, name="mem_bwd_pre")

    dmo, grads['mix_post_norm'] = rowcall(_bwd_residual, [dh2, mo], [small['mix_post_norm']], [(D_MODEL, BF)],
                                          [(1, D_MODEL)], rows=512, total=s, name="mix_bwd_res")
    d_mixed = mm(dmo, w_out, tb=True, name="mix_bwd_dmixed")
    dw_out = mm(mixed, dmo, ta=True, out_dtype=BF, name="mix_bwd_dwout")
    def b_delta(do, o):
        sel = (_iota2((512, 128), 0) // FOX_DH == _iota2((512, 128), 1)).astype(F32)
        return hdot(do * o, sel)

    delta, = rowcall(b_delta, [(d_mixed, 512, 0), fox_flat], [], [(128, F32)], rows=512, total=s, name="fox_delta")
    dfox_q, dfox_k, dvf, sum_q, sum_k = fox_bwd(qkv_bf, d_mixed[:, :512].astype(BF), xk, xv,
                                                _fox_extras(s, f_heads - lse_heads, 1.0),
                                                _fox_extras(s, -delta[:, :FOX_HEADS], None))
    per_head = lambda a: jnp.stack([a[:, 0::2 * FOX_DH], a[:, 1::2 * FOX_DH]], axis=-1).reshape(s, FOX_HEADS)
    d_f = _pad_to(per_head(sum_q) - per_head(sum_k), (s, 128))
    dsmall_f, dbias = fox_f_bwd(proj, bias_row, d_f)
    grads['fox_f_bias'] = dbias[:, :FOX_HEADS]
    dcqkv, dz, dgb, grads['gdn_out_norm'] = gdn_bwd(cqkv, proj, gbb, onorm, states, d_mixed)

    def b_gates(sm, dsf, dg, db, prm):
        dsm, dprm = jax.vjp(_gdn_gates, sm, prm)[1]((dg, db))
        return dsm + dsf, dprm

    dsmall, dprm = rowcall(b_gates, [(proj, 128, SMALL_BLOCK128), dsmall_f, (dgb, 512, 0), (dgb, 512, 1)], [gate_prm],
                           [(128, F32)],
                           [(8, 128)], rows=512, total=s, name="gdn_bwd_gates")
    grads['gdn_a_log'] = dprm[0:1, SMALL_A:SMALL_A + GDN_HEADS]
    grads['gdn_dt_bias'] = dprm[1:2, SMALL_A:SMALL_A + GDN_HEADS]
    dqkv_pre, dconv8 = conv_bwd(proj, conv_w8, dcqkv)
    dproj = jnp.concatenate([dfox_q, dfox_k, dvf, dqkv_pre, dz, dsmall,
                             jnp.zeros((s, PROJ_W - 3584 - 128), F32)], axis=1).astype(BF)
    du2 = mm(dproj, w_proj, tb=True, name="mix_bwd_du")
    dw_proj = mm(u2, dproj, ta=True, out_dtype=BF, name="mix_bwd_dwproj")
    dh1, grads['mix_pre_norm'] = rowcall(b_pre, [dh2, du2, h1], [small['mix_pre_norm']], [(D_MODEL, F32)],
                                         [(1, D_MODEL)], rows=512, total=s, name="mix_bwd_pre")

    dw_in = jnp.concatenate([dw_proj[:, :1536], dw_proj[:, 3584:3592], dw_proj[:, 1536:3584],
                             dw_proj[:, 3592:3600]], axis=1)
    gap = jnp.zeros((D_MODEL, IN_SHARD_PAD - IN_SHARD), BF)
    dw_in = jnp.concatenate([piece for j in range(N_DEV) for piece in (dw_in[:, j * IN_SHARD:(j + 1) * IN_SHARD], gap)],
                            axis=1)
    b_in = [dw_in, dw_out, dw_q, dw_kv, dw_o]
    b_tr = [(0, (1, 0, IN_SHARD_PAD, IN_SHARD_PAD), 0, lead), (1, (0, 0, dm, dm), 1, lead), (2, (0, 0, dm, dm), 2, lead),
            (3, (1, 0, 2 * dm, 2 * dm), 3, lead), (4, (0, 0, dm, dm), 4, lead)]
    b_shapes = [land(D_MODEL, IN_SHARD_PAD), land(dm, D_MODEL), land(dm, D_MODEL), land(D_MODEL, 2 * dm),
                land(dm, D_MODEL)]
    b_land = place_own(b_in, b_shapes, b_tr)
    b_send, b_recv, b_src, b_land, b_token = exchange_start(b_in, b_land, b_tr, dh1, "reduce_mix_start")

    def start_down_reduce(dwd):
        tr = ffn_tr[2:]
        tr = [(0, tr[0][1], 0, tr[0][3])]
        late['c_down'] = (exchange_start([dwd], place_own([dwd], ffn_land[2:], tr), tr, dwd, "reduce_ffn1_down_start"), tr)
        return late['c_down'][0][4]

    def start_gate_up_reduce(dwg, dwu):
        tr = ffn_tr[:2]
        late['c_gu'] = (exchange_start([dwg, dwu], place_own([dwg, dwu], ffn_land[:2], tr), tr, dwu,
                                       "reduce_ffn1_gu_start"), tr)
        return late['c_gu'][0][4]

    grad_x, _, _, _, grads['ffn1_pre_norm'], grads['ffn1_post_norm'] = _ffn_bwd(
        dh1, x, small['ffn1_pre_norm'], small['ffn1_post_norm'] + b_token[0, 0], wgu1, wd1, sv1, "ffn1",
        on_dwd=start_down_reduce, on_dwgu=start_gate_up_reduce)

    gpack = _pack(lambda nm: grads[nm], conv=dconv8[:CONV_W], loss=loss_acc[:, :1])
    gsum_parts, = exchange([gpack], [land(PACK_ROWS, D_MODEL, F32)], [(0, whole, 0, lead)], "reduce_small")
    a_got = exchange_wait(a_send, a_recv, a_src, a_land, gsum_parts, ffn_tr, "reduce_ffn2_wait")
    b_got = exchange_wait(b_send, b_recv, b_src, b_land, gsum_parts, b_tr, "reduce_mix_wait")
    recv = dict(zip(['ffn2_w_gate', 'ffn2_w_up', 'ffn2_w_down', 'w_in', 'w_out', 'mem_w_q', 'mem_w_kv', 'mem_w_o'],
                    a_got + b_got))

    out_g, out_d, out_m, out_v = {}, {}, {}, {}

    def update(nm):
        r = recv[nm]
        res = adamw(w2[nm], m2[nm], v2[nm], [(r, r.shape[2], 0, d) for d in range(N_DEV)], "adamw_" + nm)
        out_g[nm], out_d[nm], out_m[nm], out_v[nm] = res

    for nm in recv:
        update(nm)
    wp = _pack(lambda nm: small[nm])
    mp = _pack(lambda nm: m2[nm][None])
    vp = _pack(lambda nm: v2[nm][None])
    pg, pd, pm, pv = adamw(wp, mp, vp, [(gsum_parts, D_MODEL, 0, d) for d in range(N_DEV)], "adamw_small")
    for dst, p in ((out_g, pg), (out_d, pd), (out_m, pm), (out_v, pv)):
        dst.update({k: val[0] for k, val in _unpack(p).items()})
    loss = pg[ROW_MISC, COL_LOSS]
    conv_g = lax.dynamic_slice_in_dim(pg[ROW_CONV:ROW_CONV + 6].reshape(CONV_W, CONV_CH), me * (CONV_CH // N_DEV),
                                      CONV_CH // N_DEV, axis=1)
    res = adamw(w2['gdn_conv_w'], m2['gdn_conv_w'], v2['gdn_conv_w'], [conv_g], "adamw_conv")
    out_g['gdn_conv_w'], out_d['gdn_conv_w'], out_m['gdn_conv_w'], out_v['gdn_conv_w'] = res

    done = sum(out_d[nm][0, 0] for nm in recv) + out_d['gdn_conv_w'][0, 0] + pd[0, 0]
    after = jnp.zeros((8, 128), F32) + done
    c_got = []
    for key, nm in (('c_gu', "reduce_ffn1_gu_wait"), ('c_down', "reduce_ffn1_down_wait")):
        (c_send, c_recv, c_src, c_land, _), tr = late[key]
        c_got += exchange_wait(c_send, c_recv, c_src, c_land, after, tr, nm)
    recv = dict(zip(['ffn1_w_gate', 'ffn1_w_up', 'ffn1_w_down'], c_got))
    for nm in recv:
        update(nm)

    def depth(t):
        return t[None]

    return (loss, grad_x[None], *[depth(out_g[nm]) for nm in WEIGHTS], *[depth(out_d[nm]) for nm in WEIGHTS],
            *[depth(out_m[nm]) for nm in WEIGHTS], *[depth(out_v[nm]) for nm in WEIGHTS])


def kernel(x, mem, ffn1_pre_norm, ffn1_w_gate, ffn1_w_up, ffn1_w_down, ffn1_post_norm, mix_pre_norm, w_in, fox_f_bias, gdn_conv_w, gdn_a_log, gdn_dt_bias, gdn_out_norm, w_out, mix_post_norm, mem_pre_norm, mem_kv_norm, mem_w_q, mem_w_kv, mem_w_o, mem_post_norm, ffn2_pre_norm, ffn2_w_gate, ffn2_w_up, ffn2_w_down, ffn2_post_norm, loss_target, m_ffn1_pre_norm, m_ffn1_w_gate, m_ffn1_w_up, m_ffn1_w_down, m_ffn1_post_norm, m_mix_pre_norm, m_w_in, m_fox_f_bias, m_gdn_conv_w, m_gdn_a_log, m_gdn_dt_bias, m_gdn_out_norm, m_w_out, m_mix_post_norm, m_mem_pre_norm, m_mem_kv_norm, m_mem_w_q, m_mem_w_kv, m_mem_w_o, m_mem_post_norm, m_ffn2_pre_norm, m_ffn2_w_gate, m_ffn2_w_up, m_ffn2_w_down, m_ffn2_post_norm, v_ffn1_pre_norm, v_ffn1_w_gate, v_ffn1_w_up, v_ffn1_w_down, v_ffn1_post_norm, v_mix_pre_norm, v_w_in, v_fox_f_bias, v_gdn_conv_w, v_gdn_a_log, v_gdn_dt_bias, v_gdn_out_norm, v_w_out, v_mix_post_norm, v_mem_pre_norm, v_mem_kv_norm, v_mem_w_q, v_mem_w_kv, v_mem_w_o, v_mem_post_norm, v_ffn2_pre_norm, v_ffn2_w_gate, v_ffn2_w_up, v_ffn2_w_down, v_ffn2_post_norm):
    return _step(dict(locals()))
```

```python
import functools

import jax
import jax.numpy as jnp
from jax import lax
from jax.experimental import pallas as pl
from jax.experimental.pallas import tpu as pltpu

F32 = jnp.float32
BF = jnp.bfloat16
HI = lax.Precision.HIGHEST
MESH = pl.DeviceIdType.MESH

N_DEV = 8
EPS = 1e-6
D_MODEL = 1024
D_FF = 2816
FF_SHARD = D_FF // N_DEV
FF_SHARD_PAD = 384
D_FF_PAD = FF_SHARD_PAD * N_DEV
FOX_HEADS, FOX_DH = 8, 64
GDN_HEADS, GDN_DH = 4, 128
GDN_CHUNK = 64
CONV_W = 4
MEM_HEADS, MEM_DH = 4, 256
IN_W = 3600
IN_SHARD = IN_W // N_DEV
IN_SHARD_PAD = 512
PROJ_W = 4096
SMALL_F, SMALL_B, SMALL_A = 0, 8, 12

ADAM_LR, ADAM_B1, ADAM_B2, ADAM_EPS, ADAM_WD, ADAM_STEP = 0.001, 0.9, 0.999, 1e-08, 0.01, 10

VMEM_LIMIT = 56 * 1024 * 1024


def _params(sem=None):
    return pltpu.CompilerParams(dimension_semantics=sem, vmem_limit_bytes=VMEM_LIMIT)


def _tile(n, pref, unit=128):
    if n <= pref:
        return n
    t = (pref // unit) * unit
    while t > unit and n % t:
        t -= unit
    assert n % t == 0, (n, pref)
    return t


@functools.partial(jax.custom_vjp, nondiff_argnums=(2, 3))
def bdot(a, b, ca, cb):
    return lax.dot_general(a.astype(BF), b.astype(BF), (((ca,), (cb,)), ((), ())), preferred_element_type=F32)


def _bdot_fwd(a, b, ca, cb):
    return bdot(a, b, ca, cb), (a, b)


def _bdot_bwd(ca, cb, res, g):
    a, b = res
    da = bdot(g, b, 1, 1 - cb) if ca == 1 else bdot(b, g, 1 - cb, 1)
    db = bdot(a, g, 1 - ca, 0) if cb == 0 else bdot(g, a, 0, 1 - ca)
    return da, db


bdot.defvjp(_bdot_fwd, _bdot_bwd)


def hdot(a, b):
    return jnp.dot(a, b, precision=HI, preferred_element_type=F32)


def mdot(a, b):
    return jnp.dot(a, b, precision=lax.Precision.HIGH, preferred_element_type=F32)


def _iota2(shape, dim):
    return lax.broadcasted_iota(jnp.int32, shape, dim)


def _sigmoid(x):
    return 1.0 / (1.0 + jnp.exp(-x))


def _silu(x):
    return x * _sigmoid(x)


def _softplus(x):
    return jnp.maximum(x, 0.0) + jnp.log(1.0 + jnp.exp(-jnp.abs(x)))


def _rms(x, gain):
    return x * lax.rsqrt(jnp.mean(x * x, axis=-1, keepdims=True) + EPS) * gain


def mm(a, b, *, name, ta=False, tb=False, out_dtype=F32, tm=1024, tn=1024, tk=1024, token=None):
    m, k = (a.shape[1], a.shape[0]) if ta else a.shape
    n = b.shape[0] if tb else b.shape[1]
    assert k == (b.shape[1] if tb else b.shape[0]), (a.shape, b.shape, ta, tb)
    tm, tn, tk = _tile(m, tm), _tile(n, tn), _tile(k, tk)
    nk = k // tk
    dims = (((0 if ta else 1,), (1 if tb else 0,)), ((), ()))

    def kern(a_ref, b_ref, *rest):
        o_ref, scratch = (rest[1], rest[2:]) if token is not None else (rest[0], rest[1:])

        def part():
            return lax.dot_general(a_ref[...].astype(BF), b_ref[...].astype(BF), dims, preferred_element_type=F32)

        if nk == 1:
            o_ref[...] = part().astype(o_ref.dtype)
            return
        acc_ref, = scratch
        kk = pl.program_id(2)

        @pl.when(kk == 0)
        def _():
            acc_ref[...] = part()

        @pl.when(kk > 0)
        def _():
            acc_ref[...] += part()

        @pl.when(kk == nk - 1)
        def _():
            o_ref[...] = acc_ref[...].astype(o_ref.dtype)

    a_spec = pl.BlockSpec((tk, tm), lambda i, j, kk: (kk, i)) if ta else pl.BlockSpec((tm, tk), lambda i, j, kk: (i, kk))
    b_spec = pl.BlockSpec((tn, tk), lambda i, j, kk: (j, kk)) if tb else pl.BlockSpec((tk, tn), lambda i, j, kk: (kk, j))
    return pl.pallas_call(
        kern, name=name, grid=(m // tm, n // tn, nk),
        in_specs=[a_spec, b_spec] + ([pl.BlockSpec((8, 128), lambda i, j, kk: (0, 0))] if token is not None else []),
        out_specs=pl.BlockSpec((tm, tn), lambda i, j, kk: (i, j)),
        out_shape=jax.ShapeDtypeStruct((m, n), out_dtype),
        scratch_shapes=[pltpu.VMEM((tm, tn), F32)] if nk > 1 else [],
        compiler_params=_params(("parallel", "parallel", "arbitrary")),
    )(*((a, b) if token is None else (a, b, token)))


def mm_swiglu(a, wgu, *, name):
    m, k = a.shape
    nh = wgu.shape[1] // 2
    tm, tn = _tile(m, 1024), _tile(nh, 512)
    nj = nh // tn

    def kern(a_ref, bg_ref, bu_ref, g_ref, u_ref, act_ref):
        av = a_ref[...]
        g = jnp.dot(av, bg_ref[...], preferred_element_type=F32).astype(BF)
        u = jnp.dot(av, bu_ref[...], preferred_element_type=F32).astype(BF)
        g_ref[...] = g
        u_ref[...] = u
        act_ref[...] = (_silu(g.astype(F32)) * u.astype(F32)).astype(BF)

    tile = pl.BlockSpec((tm, tn), lambda i, j: (i, j))
    out = jax.ShapeDtypeStruct((m, nh), BF)
    return pl.pallas_call(
        kern, name=name, grid=(m // tm, nj),
        in_specs=[pl.BlockSpec((tm, k), lambda i, j: (i, 0)), pl.BlockSpec((k, tn), lambda i, j: (0, j)),
                  pl.BlockSpec((k, tn), lambda i, j: (0, j + nj))],
        out_specs=[tile, tile, tile], out_shape=[out, out, out],
        compiler_params=_params(("parallel", "parallel")),
    )(a, wgu, wgu)


def mm_dswiglu(df, wd, gate, up, *, name, token=None):
    m, k = df.shape
    nh = wd.shape[0]
    tm, tn = _tile(m, 1024), _tile(nh, 512)

    def kern(df_ref, wd_ref, g_ref, u_ref, *rest):
        dg_ref, du_ref = rest[-2:]
        da = lax.dot_general(df_ref[...], wd_ref[...], (((1,), (1,)), ((), ())), preferred_element_type=F32)
        g, u = g_ref[...].astype(F32), u_ref[...].astype(F32)
        sg = _sigmoid(g)
        dg_ref[...] = (da * u * (sg * (1.0 + g * (1.0 - sg)))).astype(BF)
        du_ref[...] = (da * (g * sg)).astype(BF)

    tile = pl.BlockSpec((tm, tn), lambda i, j: (i, j))
    out = jax.ShapeDtypeStruct((m, nh), BF)
    extra = [pl.BlockSpec((8, 128), lambda i, j: (0, 0))] if token is not None else []
    return pl.pallas_call(
        kern, name=name, grid=(m // tm, nh // tn),
        in_specs=[pl.BlockSpec((tm, k), lambda i, j: (i, 0)), pl.BlockSpec((tn, k), lambda i, j: (j, 0)), tile, tile]
        + extra,
        out_specs=[tile, tile], out_shape=[out, out],
        compiler_params=_params(("parallel", "parallel")),
    )(*((df, wd, gate, up) if token is None else (df, wd, gate, up, token)))


def mm_pair(a1, a2, wgu, *, name, token=None):
    m, nh = a1.shape
    n = wgu.shape[0]
    tm, tn, tk = _tile(m, 1024), _tile(n, 1024), _tile(nh, 1024)
    nk = nh // tk
    nt = (((1,), (1,)), ((), ()))

    def kern(a1_ref, a2_ref, b1_ref, b2_ref, *rest):
        o_ref, acc_ref = rest[-2:]
        kk = pl.program_id(2)

        def part():
            return (lax.dot_general(a1_ref[...], b1_ref[...], nt, preferred_element_type=F32)
                    + lax.dot_general(a2_ref[...], b2_ref[...], nt, preferred_element_type=F32))

        @pl.when(kk == 0)
        def _():
            acc_ref[...] = part()

        @pl.when(kk > 0)
        def _():
            acc_ref[...] += part()

        @pl.when(kk == nk - 1)
        def _():
            o_ref[...] = acc_ref[...]

    a_spec = pl.BlockSpec((tm, tk), lambda i, j, kk: (i, kk))
    extra = [pl.BlockSpec((8, 128), lambda i, j, kk: (0, 0))] if token is not None else []
    return pl.pallas_call(
        kern, name=name, grid=(m // tm, n // tn, nk),
        in_specs=[a_spec, a_spec, pl.BlockSpec((tn, tk), lambda i, j, kk: (j, kk)),
                  pl.BlockSpec((tn, tk), lambda i, j, kk: (j, kk + nk))] + extra,
        out_specs=pl.BlockSpec((tm, tn), lambda i, j, kk: (i, j)),
        out_shape=jax.ShapeDtypeStruct((m, n), F32),
        scratch_shapes=[pltpu.VMEM((tm, tn), F32)],
        compiler_params=_params(("parallel", "parallel", "arbitrary")),
    )(*((a1, a2, wgu, wgu) if token is None else (a1, a2, wgu, wgu, token)))


def _row_spec(item, rows):
    if not isinstance(item, tuple):
        return item, pl.BlockSpec((rows, item.shape[1]), lambda i: (i, 0))
    if len(item) == 3:
        arr, w, c = item
        return arr, pl.BlockSpec((rows, w), lambda i: (i, c))
    arr, w, c, lead = item
    return arr, pl.BlockSpec((None, rows, w), lambda i: (lead, i, c))


def _whole_spec(item):
    if not isinstance(item, tuple):
        return item, pl.BlockSpec(item.shape, lambda i: (0,) * item.ndim)
    arr, w, c = item
    return arr, pl.BlockSpec((arr.shape[0], w), lambda i: (0, c))


def rowcall(body, tiled, whole, outs, accs=(), *, rows, total, name):
    rows = min(rows, total)
    assert total % rows == 0
    t_arr, t_spec = zip(*[_row_spec(t, rows) for t in tiled])
    w_arr, w_spec = zip(*[_whole_spec(w) for w in whole]) if whole else ((), ())
    nt, nw, no, na = len(t_arr), len(w_arr), len(outs), len(accs)

    def kern(*refs):
        vals = [r[...] for r in refs[:nt + nw]]
        res = body(*vals)
        if not isinstance(res, (tuple, list)):
            res = (res,)
        assert len(res) == no + na, (name, len(res), no, na)
        for r, v in zip(refs[nt + nw:nt + nw + no], res[:no]):
            r[...] = v.astype(r.dtype)
        if na:
            acc_refs = refs[nt + nw + no:]

            @pl.when(pl.program_id(0) == 0)
            def _():
                for r in acc_refs:
                    r[...] = jnp.zeros_like(r)

            for r, v in zip(acc_refs, res[no:]):
                r[...] += v

    out_shape = [jax.ShapeDtypeStruct((total, w), d) for w, d in outs] + [jax.ShapeDtypeStruct(s, F32) for s in accs]
    out_specs = [pl.BlockSpec((rows, w), lambda i: (i, 0)) for w, _ in outs] + \
                [pl.BlockSpec(s, lambda i: (0, 0)) for s in accs]
    res = pl.pallas_call(
        kern, name=name, grid=(total // rows,),
        in_specs=list(t_spec) + list(w_spec), out_specs=out_specs, out_shape=out_shape,
        compiler_params=_params(("arbitrary",) if na else ("parallel",)),
    )(*t_arr, *w_arr)
    return res


def _colsum(x):
    return jnp.sum(x, axis=0, keepdims=True)


def _gdn_chunk(q, k, v, z, gb, bb, state, gain):
    c = GDN_CHUNK
    nh = len(q)
    hs = range(nh)
    r64, c64 = _iota2((c, c), 0), _iota2((c, c), 1)
    incl = r64 >= c64
    strict = r64 > c64
    ltri = incl.astype(F32)
    eye = (r64 == c64).astype(F32)
    pick = (_iota2((GDN_DH, c), 0) == _iota2((GDN_DH, c), 1)).astype(F32)
    last = (_iota2((c, GDN_DH), 0) == c - 1).astype(F32)

    qn = [q[h] * lax.rsqrt(jnp.sum(q[h] * q[h], axis=-1, keepdims=True) + EPS) * (GDN_DH ** -0.5) for h in hs]
    kn = [k[h] * lax.rsqrt(jnp.sum(k[h] * k[h], axis=-1, keepdims=True) + EPS) for h in hs]
    gc = [mdot(ltri, gb[h]) for h in hs]
    gcol = [mdot(gc[h], pick) for h in hs]
    dec = [jnp.exp(jnp.where(incl, gcol[h] - gcol[h].T, -1e30)) for h in hs]
    kb = [kn[h] * bb[h] for h in hs]
    vb = [v[h] * bb[h] for h in hs]
    kk = [bdot(kb[h], kn[h], 1, 1) for h in hs]
    p = [-jnp.where(strict, kk[h] * dec[h], 0.0) for h in hs]
    tinv = [eye + p[h] for h in hs]
    for level in range(5):
        dot = mdot if level < 2 else (lambda a, b: bdot(a, b, 1, 0))
        p = [dot(p[h], p[h]) for h in hs]
        tinv = [tinv[h] + dot(tinv[h], p[h]) for h in hs]
    egc = [jnp.exp(gc[h]) for h in hs]
    u = [mdot(tinv[h], vb[h]) for h in hs]
    w = [mdot(tinv[h], kb[h] * egc[h]) for h in hs]
    attn = [bdot(qn[h], kn[h], 1, 1) * dec[h] for h in hs]
    qd = [qn[h] * egc[h] for h in hs]
    gl = [jnp.sum(gc[h] * last, axis=0, keepdims=True) for h in hs]
    kt = [kn[h] * jnp.exp(gl[h] - gc[h]) for h in hs]
    ws = [bdot(w[h], state[h], 1, 0) for h in hs]
    qs = [bdot(qd[h], state[h], 1, 0) for h in hs]
    v_new = [u[h] - ws[h] for h in hs]
    av = [bdot(attn[h], v_new[h], 1, 0) for h in hs]
    kv = [bdot(kt[h], v_new[h], 0, 0) for h in hs]
    new_state = tuple(state[h] * jnp.exp(gl[h]) + kv[h] for h in hs)
    o = tuple(_rms(qs[h] + av[h], gain) * _silu(z[h]) for h in hs)
    return o, new_state


GDN_ROWS = 512
GDN_W = GDN_HEADS * GDN_DH


def gdn_fwd(cqkv, proj, gbb, gain):
    s = cqkv.shape[0]
    nb, cpb = s // GDN_ROWS, GDN_ROWS // GDN_CHUNK
    h4 = GDN_HEADS

    def kern(qkv_ref, z_ref, gb_ref, gain_ref, o_ref, st_ref, state):
        @pl.when(pl.program_id(0) == 0)
        def _():
            state[...] = jnp.zeros_like(state)

        gain_v = gain_ref[...]

        def step(ci, carry):
            sl = pl.ds(pl.multiple_of(ci * GDN_CHUNK, GDN_CHUNK), GDN_CHUNK)
            ins = []
            for h in range(h4):
                ln = lambda base, h=h: slice(base + h * GDN_DH, base + (h + 1) * GDN_DH)
                ins.append((qkv_ref[sl, ln(0)], qkv_ref[sl, ln(GDN_W)], qkv_ref[sl, ln(2 * GDN_W)], z_ref[sl, ln(0)],
                            gb_ref[sl, ln(0)], gb_ref[sl, ln(GDN_W)], state[h]))
            cols = [tuple(col) for col in zip(*ins)]
            o, new = _gdn_chunk(*cols[:7], gain_v)
            for h in range(h4):
                st_ref[h, ci] = ins[h][6]
                o_ref[sl, h * GDN_DH:(h + 1) * GDN_DH] = o[h]
                state[h] = new[h]
            return carry

        lax.fori_loop(0, cpb, step, 0)

    return pl.pallas_call(
        kern, name="gdn_fwd", grid=(nb,),
        in_specs=[pl.BlockSpec((GDN_ROWS, 3 * GDN_W), lambda i: (i, 0)),
                  pl.BlockSpec((GDN_ROWS, GDN_W), lambda i: (i, 6)),
                  pl.BlockSpec((GDN_ROWS, 2 * GDN_W), lambda i: (i, 0)),
                  pl.BlockSpec((1, GDN_DH), lambda i: (0, 0))],
        out_specs=[pl.BlockSpec((GDN_ROWS, GDN_W), lambda i: (i, 0)),
                   pl.BlockSpec((h4, cpb, GDN_DH, GDN_DH), lambda i: (0, i, 0, 0))],
        out_shape=[jax.ShapeDtypeStruct((s, GDN_W), F32),
                   jax.ShapeDtypeStruct((h4, s // GDN_CHUNK, GDN_DH, GDN_DH), F32)],
        scratch_shapes=[pltpu.VMEM((h4, GDN_DH, GDN_DH), F32)],
        compiler_params=_params(("arbitrary",)),
    )(cqkv, proj, gbb, gain)


def gdn_bwd(cqkv, proj, gbb, gain, states, d_mixed):
    s = cqkv.shape[0]
    nb, cpb = s // GDN_ROWS, GDN_ROWS // GDN_CHUNK
    h4 = GDN_HEADS

    def kern(qkv_ref, z_ref, gb_ref, gain_ref, st_ref, do_ref, dqkv_ref, dz_ref, dgb_ref, dgain_ref, dstate):
        @pl.when(pl.program_id(0) == 0)
        def _():
            dgain_ref[...] = jnp.zeros_like(dgain_ref)
            dstate[...] = jnp.zeros_like(dstate)

        gain_v = gain_ref[...]

        def step(t, carry):
            ci = cpb - 1 - t
            sl = pl.ds(pl.multiple_of(ci * GDN_CHUNK, GDN_CHUNK), GDN_CHUNK)
            prim, cot, dst_in = [], [], []
            for h in range(h4):
                ln = lambda base, h=h: slice(base + h * GDN_DH, base + (h + 1) * GDN_DH)
                prim.append((qkv_ref[sl, ln(0)], qkv_ref[sl, ln(GDN_W)], qkv_ref[sl, ln(2 * GDN_W)], z_ref[sl, ln(0)],
                             gb_ref[sl, ln(0)], gb_ref[sl, ln(GDN_W)], st_ref[h, ci]))
                cot.append(do_ref[sl, ln(0)])
                dst_in.append(dstate[h])
            cols = [tuple(col) for col in zip(*prim)]
            vjp = jax.vjp(_gdn_chunk, *cols, gain_v)[1]
            dq, dk, dv, dz, dg, db, dst, dgn = vjp((tuple(cot), tuple(dst_in)))
            for h in range(h4):
                ln = lambda base, h=h: slice(base + h * GDN_DH, base + (h + 1) * GDN_DH)
                dqkv_ref[sl, ln(0)] = dq[h]
                dqkv_ref[sl, ln(GDN_W)] = dk[h]
                dqkv_ref[sl, ln(2 * GDN_W)] = dv[h]
                dz_ref[sl, ln(0)] = dz[h]
                dgb_ref[sl, ln(0)] = dg[h]
                dgb_ref[sl, ln(GDN_W)] = db[h]
                dstate[h] = dst[h]
            dgain_ref[...] += dgn
            return carry

        lax.fori_loop(0, cpb, step, 0)

    def rev(width, cblock=0):
        return pl.BlockSpec((GDN_ROWS, width), lambda i: (nb - 1 - i, cblock))

    return pl.pallas_call(
        kern, name="gdn_bwd", grid=(nb,),
        in_specs=[rev(3 * GDN_W), rev(GDN_W, 6), rev(2 * GDN_W), pl.BlockSpec((1, GDN_DH), lambda i: (0, 0)),
                  pl.BlockSpec((h4, cpb, GDN_DH, GDN_DH), lambda i: (0, nb - 1 - i, 0, 0)), rev(GDN_W, 1)],
        out_specs=[rev(3 * GDN_W), rev(GDN_W), rev(2 * GDN_W), pl.BlockSpec((1, GDN_DH), lambda i: (0, 0))],
        out_shape=[jax.ShapeDtypeStruct((s, 3 * GDN_W), F32), jax.ShapeDtypeStruct((s, GDN_W), F32),
                   jax.ShapeDtypeStruct((s, 2 * GDN_W), F32), jax.ShapeDtypeStruct((1, GDN_DH), F32)],
        scratch_shapes=[pltpu.VMEM((h4, GDN_DH, GDN_DH), F32)],
        compiler_params=_params(("arbitrary",)),
    )(cqkv, proj, gbb, gain, states, d_mixed)


def _gdn_gates(small, prm):
    w = GDN_HEADS * GDN_DH
    lane, head = _iota2((128, w), 0), _iota2((128, w), 1) // GDN_DH
    sel_b = (lane == SMALL_B + head).astype(F32)
    sel_a = (lane == SMALL_A + head).astype(F32)
    prow = _iota2((8, 128), 0)
    a_log = jnp.sum(prm * (prow == 0).astype(F32), axis=0, keepdims=True)
    dt_b = jnp.sum(prm * (prow == 1).astype(F32), axis=0, keepdims=True)
    beta = _sigmoid(hdot(small, sel_b))
    g = hdot(-jnp.exp(a_log) * _softplus(small + dt_b), sel_a)
    return g, beta


CONV_ROWS = 1024
CONV_COLS = 128
CONV_BLOCK0 = 1536 // CONV_COLS


def _shift_down(prev8, cur, s):
    ext = jnp.concatenate([prev8, cur], axis=0)
    return pltpu.roll(ext, s, 0)[8:]


def _shift_up(cur, next8, s):
    n = cur.shape[0]
    ext = jnp.concatenate([cur, next8], axis=0)
    return pltpu.roll(ext, n + 8 - s, 0)[:n]


def _conv_pre(x_ref, w, ci, nchunk):
    r0 = pl.multiple_of(ci * CONV_ROWS, CONV_ROWS)
    cur = x_ref[pl.ds(r0, CONV_ROWS), :]
    prev = x_ref[pl.ds(pl.multiple_of(jnp.maximum(r0 - 8, 0), 8), 8), :]
    prev = jnp.where(ci > 0, prev, 0.0)
    shifted = [cur] + [_shift_down(prev, cur, s) for s in range(1, CONV_W)]
    pre = w[CONV_W - 1:CONV_W, :] * cur
    for s in range(1, CONV_W):
        pre = pre + w[CONV_W - 1 - s:CONV_W - s, :] * shifted[s]
    return r0, pre, shifted


def conv_fwd(proj, conv_w8):
    s = proj.shape[0]
    nchunk = s // CONV_ROWS
    ncol = 3 * GDN_HEADS * GDN_DH // CONV_COLS

    def kern(x_ref, w_ref, y_ref):
        w = w_ref[...]

        def step(ci, carry):
            r0, pre, _ = _conv_pre(x_ref, w, ci, nchunk)
            y_ref[pl.ds(r0, CONV_ROWS), :] = _silu(pre)
            return carry

        lax.fori_loop(0, nchunk, step, 0)

    return pl.pallas_call(
        kern, name="conv_fwd", grid=(ncol,),
        in_specs=[pl.BlockSpec((s, CONV_COLS), lambda j: (0, CONV_BLOCK0 + j)),
                  pl.BlockSpec((8, CONV_COLS), lambda j: (0, j))],
        out_specs=pl.BlockSpec((s, CONV_COLS), lambda j: (0, j)),
        out_shape=jax.ShapeDtypeStruct((s, ncol * CONV_COLS), F32),
        compiler_params=_params(("parallel",)),
    )(proj, conv_w8)


def conv_bwd(proj, conv_w8, dy):
    s = proj.shape[0]
    nchunk = s // CONV_ROWS
    per = 3 * GDN_HEADS * GDN_DH // CONV_COLS
    outs = []
    for part in range(1):
        def kern(x_ref, w_ref, dy_ref, dx_ref, dw_ref, dpre_ref):
            w = w_ref[...]
            rows8 = _iota2((8, CONV_COLS), 0)

            def step1(ci, dw):
                r0, pre, shifted = _conv_pre(x_ref, w, ci, nchunk)
                sg = _sigmoid(pre)
                dpre = dy_ref[pl.ds(r0, CONV_ROWS), :] * sg * (1.0 + pre * (1.0 - sg))
                dpre_ref[pl.ds(r0, CONV_ROWS), :] = dpre
                for sh in range(CONV_W):
                    dw = dw + jnp.where(rows8 == CONV_W - 1 - sh, _colsum(dpre * shifted[sh]), 0.0)
                return dw

            dw_ref[...] = lax.fori_loop(0, nchunk, step1, jnp.zeros((8, CONV_COLS), F32))

            def step2(ci, carry):
                r0 = pl.multiple_of(ci * CONV_ROWS, CONV_ROWS)
                cur = dpre_ref[pl.ds(r0, CONV_ROWS), :]
                nxt = dpre_ref[pl.ds(pl.multiple_of(jnp.minimum(r0 + CONV_ROWS, s - 8), 8), 8), :]
                nxt = jnp.where(ci < nchunk - 1, nxt, 0.0)
                dx = w[CONV_W - 1:CONV_W, :] * cur
                for sh in range(1, CONV_W):
                    dx = dx + w[CONV_W - 1 - sh:CONV_W - sh, :] * _shift_up(cur, nxt, sh)
                dx_ref[pl.ds(r0, CONV_ROWS), :] = dx
                return carry

            lax.fori_loop(0, nchunk, step2, 0)

        outs.append(pl.pallas_call(
            kern, name=f"conv_bwd{part}", grid=(per,),
            in_specs=[pl.BlockSpec((s, CONV_COLS), lambda j, part=part: (0, CONV_BLOCK0 + part * per + j)),
                      pl.BlockSpec((8, CONV_COLS), lambda j, part=part: (0, part * per + j)),
                      pl.BlockSpec((s, CONV_COLS), lambda j: (0, j))],
            out_specs=[pl.BlockSpec((s, CONV_COLS), lambda j: (0, j)),
                       pl.BlockSpec((8, CONV_COLS), lambda j: (0, j))],
            out_shape=[jax.ShapeDtypeStruct((s, per * CONV_COLS), F32),
                       jax.ShapeDtypeStruct((8, per * CONV_COLS), F32)],
            scratch_shapes=[pltpu.VMEM((s, CONV_COLS), F32)],
            compiler_params=_params(("parallel",)),
        )(proj, conv_w8, dy))
    dx = jnp.concatenate([o[0] for o in outs], axis=1)
    dw = jnp.concatenate([o[1] for o in outs], axis=1)
    return dx, dw


FOXF_ROWS = 512
SMALL_BLOCK128 = 3584 // 128


def _log_sigmoid(x):
    return jnp.minimum(x, 0.0) - jnp.log(1.0 + jnp.exp(-jnp.abs(x)))


def fox_f_fwd(proj, bias_row):
    s = proj.shape[0]
    n = s // FOXF_ROWS

    def kern(x_ref, b_ref, f_ref, carry):
        @pl.when(pl.program_id(0) == 0)
        def _():
            carry[...] = jnp.zeros_like(carry)

        heads = _iota2((FOXF_ROWS, 128), 1) < FOX_HEADS
        lf = jnp.where(heads, _log_sigmoid(x_ref[...] + b_ref[...]), 0.0)
        ltri = (_iota2((FOXF_ROWS, FOXF_ROWS), 0) >= _iota2((FOXF_ROWS, FOXF_ROWS), 1)).astype(F32)
        c = hdot(ltri, lf) + carry[...]
        f_ref[...] = c
        carry[...] = c[FOXF_ROWS - 1:FOXF_ROWS, :]

    return pl.pallas_call(
        kern, name="fox_f_fwd", grid=(n,),
        in_specs=[pl.BlockSpec((FOXF_ROWS, 128), lambda i: (i, SMALL_BLOCK128)),
                  pl.BlockSpec((1, 128), lambda i: (0, 0))],
        out_specs=pl.BlockSpec((FOXF_ROWS, 128), lambda i: (i, 0)),
        out_shape=jax.ShapeDtypeStruct((s, 128), F32),
        scratch_shapes=[pltpu.VMEM((1, 128), F32)],
        compiler_params=_params(("arbitrary",)),
    )(proj, bias_row)


def fox_f_bwd(proj, bias_row, d_f):
    s = proj.shape[0]
    n = s // FOXF_ROWS

    def kern(x_ref, b_ref, df_ref, dx_ref, db_ref, carry):
        @pl.when(pl.program_id(0) == 0)
        def _():
            carry[...] = jnp.zeros_like(carry)
            db_ref[...] = jnp.zeros_like(db_ref)

        heads = _iota2((FOXF_ROWS, 128), 1) < FOX_HEADS
        utri = (_iota2((FOXF_ROWS, FOXF_ROWS), 0) <= _iota2((FOXF_ROWS, FOXF_ROWS), 1)).astype(F32)
        rc = hdot(utri, df_ref[...]) + carry[...]
        carry[...] = rc[0:1, :]
        dx = jnp.where(heads, rc * _sigmoid(-(x_ref[...] + b_ref[...])), 0.0)
        dx_ref[...] = dx
        db_ref[...] += _colsum(dx)

    return pl.pallas_call(
        kern, name="fox_f_bwd", grid=(n,),
        in_specs=[pl.BlockSpec((FOXF_ROWS, 128), lambda i: (n - 1 - i, SMALL_BLOCK128)),
                  pl.BlockSpec((1, 128), lambda i: (0, 0)),
                  pl.BlockSpec((FOXF_ROWS, 128), lambda i: (n - 1 - i, 0))],
        out_specs=[pl.BlockSpec((FOXF_ROWS, 128), lambda i: (n - 1 - i, 0)),
                   pl.BlockSpec((1, 128), lambda i: (0, 0))],
        out_shape=[jax.ShapeDtypeStruct((s, 128), F32), jax.ShapeDtypeStruct((1, 128), F32)],
        scratch_shapes=[pltpu.VMEM((1, 128), F32)],
        compiler_params=_params(("arbitrary",)),
    )(proj, bias_row, d_f)


FOX_T = 512
FOX_SCALE = FOX_DH ** -0.5
FOX_PAIRS = FOX_HEADS // 2
NEG = -1e30
_NT = (((1,), (1,)), ((), ()))


def _split3(x):
    def bf(v):
        return lax.reduce_precision(v, exponent_bits=8, mantissa_bits=7)

    hi = bf(x)
    mid = bf(x - hi)
    lo = bf(x - hi - mid)
    return jnp.stack([hi, mid, lo], axis=-1)


def _fox_extras(s, first, second):
    def part(v):
        if v is None:
            return jnp.zeros((s, FOX_HEADS, 3), F32)
        if isinstance(v, float):
            return jnp.full((s, FOX_HEADS, 3), v, F32)
        pairs = v.reshape(s, FOX_PAIRS, 2)
        return _split3(jnp.stack([pairs[:, :, 1], pairs[:, :, 0]], axis=-1).reshape(s, FOX_HEADS))

    cols = jnp.concatenate([part(first), part(second)], axis=-1)
    cols = _pad_to(cols, (s, FOX_HEADS, FOX_DH)).reshape(s, FOX_PAIRS, 2 * FOX_DH)
    return cols.transpose(1, 0, 2).astype(BF)


def _head_masks(rows):
    lane = _iota2((rows, 2 * FOX_DH), 1)
    return lane < FOX_DH, lane >= FOX_DH


def _extra_lane(e, slot):
    return (FOX_DH if e == 0 else 0) + slot


def fox_fwd(qkv, xq, xk, xv):
    s = qkv.shape[0]
    t = min(FOX_T, s)
    n = s // t

    def kern(q_ref, k_ref, v_ref, xq_ref, xk_ref, xv_ref, o_ref, lse_ref):
        i = pl.program_id(1)
        masks = _head_masks(t)
        q_pair, x_pair = q_ref[...] * FOX_SCALE, xq_ref[...]
        q_ops = [jnp.where(mk, q_pair, x_pair) for mk in masks]

        def step(j, carry, masked):
            sl = pl.ds(pl.multiple_of(j * t, t), t)
            k_pair, xk_pair, v_pair, xv_pair = k_ref[sl, :], xk_ref[sl, :], v_ref[sl, :], xv_ref[sl, :]
            k_ops = [jnp.where(mk, k_pair, xk_pair) for mk in masks]
            v_ops = [jnp.where(mk, v_pair, xv_pair) for mk in masks]
            sc = [lax.dot_general(q_ops[e], k_ops[e], _NT, preferred_element_type=F32) for e in range(2)]
            if masked:
                keep = _iota2((t, t), 0) >= _iota2((t, t), 1)
                sc = [jnp.where(keep, x, NEG) for x in sc]
            m_new = [jnp.maximum(carry[e][0], jnp.max(sc[e], axis=1, keepdims=True)) for e in range(2)]
            p = [jnp.exp(sc[e] - m_new[e]).astype(BF) for e in range(2)]
            pv = [jnp.dot(p[e], v_ops[e], preferred_element_type=F32) for e in range(2)]
            return tuple((m_new[e], jnp.exp(carry[e][0] - m_new[e]) * carry[e][1] + pv[e]) for e in range(2))

        init = tuple((jnp.full((t, 1), NEG, F32), jnp.zeros((t, 2 * FOX_DH), F32)) for _ in range(2))
        carry = lax.fori_loop(0, i, lambda j, c: step(j, c, False), init)
        carry = step(i, carry, True)
        lane = _iota2((t, 2 * FOX_DH), 1)
        outs, lses = [], []
        for e in range(2):
            m, acc = carry[e]
            l = jnp.sum(jnp.where(lane == _extra_lane(e, 0), acc, 0.0), axis=1, keepdims=True)
            outs.append(acc / l)
            lses.append(m + jnp.log(l))
        o_ref[...] = jnp.where(masks[0], outs[0], outs[1])
        lse_ref[...] = jnp.where(lane == 0, lses[0], jnp.where(lane == 1, lses[1], 0.0))

    pr = FOX_PAIRS
    return pl.pallas_call(
        kern, name="fox_fwd", grid=(pr, n),
        in_specs=[pl.BlockSpec((t, 128), lambda p, i: (i, p)),
                  pl.BlockSpec((s, 128), lambda p, i: (0, pr + p)),
                  pl.BlockSpec((s, 128), lambda p, i: (0, 2 * pr + p)),
                  pl.BlockSpec((None, t, 128), lambda p, i: (p, i, 0)),
                  pl.BlockSpec((None, s, 128), lambda p, i: (p, 0, 0)),
                  pl.BlockSpec((None, s, 128), lambda p, i: (p, 0, 0))],
        out_specs=[pl.BlockSpec((t, 128), lambda p, i: (i, p)),
                   pl.BlockSpec((None, t, 128), lambda p, i: (p, i, 0))],
        out_shape=[jax.ShapeDtypeStruct((s, FOX_HEADS * FOX_DH), F32), jax.ShapeDtypeStruct((pr, s, 128), F32)],
        compiler_params=_params(("parallel", "parallel")),
    )(qkv, qkv, qkv, xq, xk, xv)


def fox_bwd(qkv, d_o, xk, xv, xqb, xdo):
    s = qkv.shape[0]
    t = min(FOX_T, s)
    n = s // t
    w = 2 * FOX_DH

    def both(blocks, slot):
        lane = _iota2(blocks[0].shape, 1)
        own = jnp.where(lane < FOX_DH, blocks[0], blocks[1])
        sums = [jnp.sum(jnp.where(lane == _extra_lane(e, slot), blocks[e], 0.0), axis=1, keepdims=True)
                for e in range(2)]
        return own, jnp.where(lane == 0, sums[0], jnp.where(lane == 1, sums[1], 0.0))

    def kern(k_ref, v_ref, xk_ref, xv_ref, q_ref, do_ref, xq_ref, xd_ref,
             dq_ref, dk_ref, dv_ref, sq_ref, sk_ref, dq_acc):
        j = pl.program_id(1)

        @pl.when(j == 0)
        def _():
            dq_acc[...] = jnp.zeros_like(dq_acc)

        masks = _head_masks(t)
        k_ops = [jnp.where(mk, k_ref[...], xk_ref[...]) for mk in masks]
        v_ops = [jnp.where(mk, v_ref[...], xv_ref[...]) for mk in masks]
        k_t = [x.T for x in k_ops]

        def step(i, carry, masked):
            dk, dv = carry
            sl = pl.ds(pl.multiple_of(i * t, t), t)
            q_pair, xq_pair, do_pair, xd_pair = q_ref[sl, :] * FOX_SCALE, xq_ref[sl, :], do_ref[sl, :], xd_ref[sl, :]
            q_ops = [jnp.where(mk, q_pair, xq_pair) for mk in masks]
            do_ops = [jnp.where(mk, do_pair, xd_pair) for mk in masks]
            q_t = [x.T for x in q_ops]
            do_t = [jnp.where(mk, do_pair, 0).astype(BF).T for mk in masks]
            st = [lax.dot_general(k_ops[e], q_ops[e], _NT, preferred_element_type=F32) for e in range(2)]
            dp = [lax.dot_general(v_ops[e], do_ops[e], _NT, preferred_element_type=F32) for e in range(2)]
            if masked:
                keep = _iota2((t, t), 0) <= _iota2((t, t), 1)
                st = [jnp.where(keep, x, NEG) for x in st]
            pt = [jnp.exp(x) for x in st]
            dsb = [(pt[e] * dp[e]).astype(BF) for e in range(2)]
            dv = dv + sum(lax.dot_general(do_t[e], pt[e].astype(BF), _NT, preferred_element_type=F32)
                          for e in range(2))
            dk = tuple(dk[e] + lax.dot_general(q_t[e], dsb[e], _NT, preferred_element_type=F32) for e in range(2))
            for e in range(2):
                dq_acc[i, e * w:(e + 1) * w, :] += jnp.dot(k_t[e], dsb[e], preferred_element_type=F32)
            return dk, dv

        init = ((jnp.zeros((w, t), F32), jnp.zeros((w, t), F32)), jnp.zeros((w, t), F32))
        carry = step(j, init, True)
        dk, dv = lax.fori_loop(j + 1, n, lambda i, c: step(i, c, False), carry)
        dk_ref[...], sk_ref[...] = both([x.T for x in dk], 3)
        dv_ref[...] = dv.T

        @pl.when(j == n - 1)
        def _():
            def out(r, carry):
                sl = pl.ds(pl.multiple_of(r * t, t), t)
                own, sums = both([dq_acc[r, e * w:(e + 1) * w, :].T for e in range(2)], 0)
                dq_ref[sl, :] = own * FOX_SCALE
                sq_ref[sl, :] = sums
                return carry

            lax.fori_loop(0, n, out, 0)

    pr = FOX_PAIRS
    flat = jax.ShapeDtypeStruct((s, FOX_HEADS * FOX_DH), F32)
    tile = pl.BlockSpec((t, 128), lambda p, j: (j, p))
    whole = pl.BlockSpec((s, 128), lambda p, j: (0, p))
    return pl.pallas_call(
        kern, name="fox_bwd", grid=(pr, n),
        in_specs=[pl.BlockSpec((t, 128), lambda p, j: (j, pr + p)),
                  pl.BlockSpec((t, 128), lambda p, j: (j, 2 * pr + p)),
                  pl.BlockSpec((None, t, 128), lambda p, j: (p, j, 0)),
                  pl.BlockSpec((None, t, 128), lambda p, j: (p, j, 0)),
                  whole, whole,
                  pl.BlockSpec((None, s, 128), lambda p, j: (p, 0, 0)),
                  pl.BlockSpec((None, s, 128), lambda p, j: (p, 0, 0))],
        out_specs=[whole, tile, tile, whole, tile],
        out_shape=[flat] * 5,
        scratch_shapes=[pltpu.VMEM((n, 2 * w, t), F32)],
        compiler_params=_params(("parallel", "arbitrary")),
    )(qkv, qkv, xk, xv, qkv, d_o, xqb, xdo)


def _xattn_head(q, k, v):
    sc = bdot(q, k, 1, 1) * (MEM_DH ** -0.5)
    e = jnp.exp(sc - lax.stop_gradient(jnp.max(sc, axis=-1, keepdims=True)))
    p = e / jnp.sum(e, axis=-1, keepdims=True)
    return bdot(p, v, 1, 0)


def xattn_fwd(q, kv):
    s = q.shape[0]
    hh = MEM_HEADS

    def body(*vals):
        qs, ks, vs = vals[:hh], vals[hh:2 * hh], vals[2 * hh:]
        return jnp.concatenate([_xattn_head(qs[a], ks[a], vs[a]) for a in range(hh)], axis=1)

    return rowcall(body, [(q, MEM_DH, a) for a in range(hh)],
                   [(kv, MEM_DH, a) for a in range(2 * hh)],
                   [(hh * MEM_DH, BF)], rows=512, total=s, name="xattn_fwd")[0]


def xattn_bwd(q, kv, d_o):
    s = q.shape[0]
    hh = MEM_HEADS

    def body(*vals):
        qs, dos = vals[:hh], vals[hh:2 * hh]
        ks, vs = vals[2 * hh:3 * hh], vals[3 * hh:]
        dqs, dks, dvs = [], [], []
        for a in range(hh):
            _, vjp = jax.vjp(_xattn_head, qs[a], ks[a], vs[a])
            dq, dk, dv = vjp(dos[a])
            dqs.append(dq)
            dks.append(dk)
            dvs.append(dv)
        return jnp.concatenate(dqs, axis=1), jnp.concatenate(dks + dvs, axis=1)

    return rowcall(body, [(q, MEM_DH, a) for a in range(hh)] + [(d_o, MEM_DH, a) for a in range(hh)],
                   [(kv, MEM_DH, a) for a in range(2 * hh)],
                   [(hh * MEM_DH, BF)], [kv.shape], rows=512, total=s, name="xattn_bwd")


def _slab(ref, axis, start, size):
    if axis is None:
        return ref
    if axis == "lead":
        return ref.at[start]
    idx = pl.ds(pl.multiple_of(start, 128 if axis == 1 else 16), size)
    return ref.at[idx] if axis == 0 else ref.at[:, idx]


def exchange(inputs, outputs, transfers, name):
    ni, no, nt = len(inputs), len(outputs), len(transfers)
    npeer = N_DEV - 1

    def body(*refs):
        ins, outs = refs[:ni], refs[ni:ni + no]
        send, recv, loc = refs[ni + no:]
        x, y, c = lax.axis_index("x"), lax.axis_index("y"), lax.axis_index("c")
        me = 4 * x + 2 * y + c

        def peer(p):
            px = 1 - x if p & 4 else x
            py = 1 - y if p & 2 else y
            pc = 1 - c if p & 1 else c
            return (px, py, pc), 4 * px + 2 * py + pc

        def view(ref, spec, who):
            axis, off, stride, size = spec
            return _slab(ref, axis, off + who * stride, size)

        local, remote = [], []
        for w, (ii, src, oi, dst) in enumerate(transfers):
            cp = pltpu.make_async_copy(view(ins[ii], src, me), view(outs[oi], dst, me), loc.at[w])
            cp.start()
            local.append(cp)
        for p in range(1, N_DEV):
            dev, idx = peer(p)
            for w, (ii, src, oi, dst) in enumerate(transfers):
                k = w * npeer + p - 1
                out_cp = pltpu.make_async_remote_copy(
                    src_ref=view(ins[ii], src, idx), dst_ref=view(outs[oi], dst, me), send_sem=send.at[k],
                    recv_sem=recv.at[k], device_id=dev, device_id_type=MESH)
                out_cp.start()
                in_cp = pltpu.make_async_remote_copy(
                    src_ref=view(ins[ii], src, idx), dst_ref=view(outs[oi], dst, idx), send_sem=send.at[k],
                    recv_sem=recv.at[k], device_id=dev, device_id_type=MESH)
                remote.append((out_cp, in_cp))
        for out_cp, in_cp in remote:
            in_cp.wait_recv()
            out_cp.wait_send()
        for cp in local:
            cp.wait()

    hbm = pl.BlockSpec(memory_space=pl.ANY)
    return pl.pallas_call(
        body, name=name, in_specs=[hbm] * ni, out_specs=[hbm] * no, out_shape=list(outputs),
        scratch_shapes=[pltpu.SemaphoreType.DMA((nt * npeer,)), pltpu.SemaphoreType.DMA((nt * npeer,)),
                        pltpu.SemaphoreType.DMA((nt,))],
        compiler_params=pltpu.CompilerParams(has_side_effects=True),
    )(*inputs)


def _peer(p):
    x, y, c = lax.axis_index("x"), lax.axis_index("y"), lax.axis_index("c")
    px = 1 - x if p & 4 else x
    py = 1 - y if p & 2 else y
    pc = 1 - c if p & 1 else c
    return (px, py, pc), 4 * px + 2 * py + pc


def _view(ref, spec, who):
    axis, off, stride, size = spec
    return _slab(ref, axis, off + who * stride, size)


def place_own(inputs, outputs, transfers):
    me = 4 * lax.axis_index("x") + 2 * lax.axis_index("y") + lax.axis_index("c")
    lands = [lax.empty(o.shape, o.dtype) for o in outputs]
    for ii, src, oi, dst in transfers:
        axis, off, stride, size = src
        own = inputs[ii] if axis is None else lax.dynamic_slice_in_dim(inputs[ii], off + me * stride, size, axis)
        axis, off, stride, size = dst
        if axis == "lead":
            lands[oi] = lax.dynamic_update_slice_in_dim(lands[oi], own[None], me, 0)
        else:
            lands[oi] = lax.dynamic_update_slice_in_dim(lands[oi], own, off + me * stride, axis)
    return lands


_HBM = pl.BlockSpec(memory_space=pltpu.HBM)
_SEM = pl.BlockSpec(memory_space=pltpu.SEMAPHORE)
_EFFECT = pltpu.SideEffectType.DATAFLOW_SIDE_EFFECTING


def _remote_copies(ins, lands, transfers, send, recv):
    npeer = N_DEV - 1
    me = 4 * lax.axis_index("x") + 2 * lax.axis_index("y") + lax.axis_index("c")
    pairs = []
    for p in range(1, N_DEV):
        dev, idx = _peer(p)
        for w, (ii, src, oi, dst) in enumerate(transfers):
            k = w * npeer + p - 1
            common = dict(src_ref=_view(ins[ii], src, idx), send_sem=send.at[k], recv_sem=recv.at[k],
                          device_id=dev, device_id_type=MESH)
            pairs.append((pltpu.make_async_remote_copy(dst_ref=_view(lands[oi], dst, me), **common),
                          pltpu.make_async_remote_copy(dst_ref=_view(lands[oi], dst, idx), **common)))
    return pairs


def exchange_start(inputs, lands, transfers, after, name):
    ni, nl, nsem = len(inputs), len(lands), len(transfers) * (N_DEV - 1)

    def body(*refs):
        ins, lnd = refs[:ni], refs[ni:ni + nl]
        send, recv = refs[ni + nl + 1], refs[ni + nl + 2]
        token = refs[-1]
        for out_cp, _ in _remote_copies(ins, lnd, transfers, send, recv):
            out_cp.start()
        token[...] = jnp.zeros_like(token)

    args = [pltpu.with_memory_space_constraint(a, pltpu.HBM) for a in list(inputs) + list(lands)]
    res = pl.pallas_call(
        body, name=name,
        out_shape=(pltpu.SemaphoreType.DMA((nsem,)), pltpu.SemaphoreType.DMA((nsem,)),
                   *[pltpu.HBM(a.shape, a.dtype) for a in args], jax.ShapeDtypeStruct((8, 128), F32)),
        in_specs=[_HBM] * (ni + nl) + [pl.BlockSpec(memory_space=pl.ANY)],
        out_specs=(_SEM, _SEM, *[_HBM] * (ni + nl), pl.BlockSpec(memory_space=pltpu.VMEM)),
        input_output_aliases={k: k + 2 for k in range(ni + nl)},
        compiler_params=pltpu.CompilerParams(has_side_effects=_EFFECT),
    )(*args, after)
    return res[0], res[1], list(res[2:2 + ni]), list(res[2 + ni:2 + ni + nl]), res[-1]


def exchange_wait(send, recv, inputs, lands, after, transfers, name):
    ni, nl = len(inputs), len(lands)

    def body(*refs):
        ins, lnd = refs[:ni], refs[ni:ni + nl]
        send_r, recv_r = refs[ni + nl], refs[ni + nl + 1]
        for out_cp, in_cp in _remote_copies(ins, lnd, transfers, send_r, recv_r):
            out_cp.wait_send()
            in_cp.wait_recv()

    res = pl.pallas_call(
        body, name=name,
        out_shape=tuple(pltpu.HBM(a.shape, a.dtype) for a in list(inputs) + list(lands)),
        in_specs=[_HBM] * (ni + nl) + [_SEM, _SEM, pl.BlockSpec(memory_space=pl.ANY)],
        out_specs=tuple([_HBM] * (ni + nl)),
        input_output_aliases={k: k for k in range(ni + nl)},
        compiler_params=pltpu.CompilerParams(has_side_effects=_EFFECT),
    )(*inputs, *lands, send, recv, after)
    return list(res[ni:])


def adamw(w, m, v, contribs, name):
    r, c = w.shape
    nc = len(contribs)
    rows = next((r // d for d in (4, 2) if r % d == 0 and (r // d) % 16 == 0), r)
    c1, c2 = 1.0 - ADAM_B1 ** ADAM_STEP, 1.0 - ADAM_B2 ** ADAM_STEP

    def body(wv, mv, vv, *gs):
        g = gs[0].astype(F32)
        for extra in gs[1:]:
            g = g + extra.astype(F32)
        g = g[:, :c]
        m_new = ADAM_B1 * mv + (1.0 - ADAM_B1) * g
        v_new = ADAM_B2 * vv + (1.0 - ADAM_B2) * (g * g)
        delta = -ADAM_LR * ((m_new / c1) / (jnp.sqrt(v_new / c2) + ADAM_EPS) + ADAM_WD * wv)
        return g, delta, m_new, v_new

    assert nc >= 1
    return rowcall(body, [w, m, v] + list(contribs), [], [(c, F32)] * 4, rows=rows, total=r, name=name)


WEIGHTS = ['ffn1_pre_norm', 'ffn1_w_gate', 'ffn1_w_up', 'ffn1_w_down', 'ffn1_post_norm', 'mix_pre_norm', 'w_in',
           'fox_f_bias', 'gdn_conv_w', 'gdn_a_log', 'gdn_dt_bias', 'gdn_out_norm', 'w_out', 'mix_post_norm',
           'mem_pre_norm', 'mem_kv_norm', 'mem_w_q', 'mem_w_kv', 'mem_w_o', 'mem_post_norm', 'ffn2_pre_norm',
           'ffn2_w_gate', 'ffn2_w_up', 'ffn2_w_down', 'ffn2_post_norm']
GAINS = ['ffn1_pre_norm', 'ffn1_post_norm', 'mix_pre_norm', 'mix_post_norm', 'mem_pre_norm', 'mem_kv_norm',
         'mem_post_norm', 'ffn2_pre_norm', 'ffn2_post_norm']
BIG = ['ffn1_w_gate', 'ffn1_w_up', 'ffn1_w_down', 'w_in', 'w_out', 'mem_w_q', 'mem_w_kv', 'mem_w_o',
       'ffn2_w_gate', 'ffn2_w_up', 'ffn2_w_down']
PACK_ROWS = 24
ROW_MISC = len(GAINS)
ROW_CONV = ROW_MISC + 1
COL_FBIAS, COL_ALOG, COL_DTB, COL_ONORM, COL_LOSS = 0, 8, 12, 128, 256
CONV_CH = 3 * GDN_HEADS * GDN_DH


def _pad_to(a, shape):
    return jnp.pad(a, [(0, t - s) for s, t in zip(a.shape, shape)])


def _pack(get, conv=None, loss=None):
    rows = [get(nm) for nm in GAINS]
    misc = jnp.concatenate([get('fox_f_bias'), get('gdn_a_log'), get('gdn_dt_bias'),
                            jnp.zeros((1, COL_ONORM - COL_DTB - 4), F32), get('gdn_out_norm'),
                            jnp.zeros((1, 1), F32) if loss is None else loss.reshape(1, 1)], axis=1)
    rows.append(_pad_to(misc, (1, D_MODEL)))
    rows.append(jnp.zeros((6, D_MODEL), F32) if conv is None else conv.reshape(6, D_MODEL))
    return _pad_to(jnp.concatenate(rows, axis=0), (PACK_ROWS, D_MODEL))


def _unpack(p):
    out = {nm: p[i:i + 1] for i, nm in enumerate(GAINS)}
    misc = p[ROW_MISC:ROW_MISC + 1]
    out['fox_f_bias'] = misc[:, COL_FBIAS:COL_FBIAS + FOX_HEADS]
    out['gdn_a_log'] = misc[:, COL_ALOG:COL_ALOG + GDN_HEADS]
    out['gdn_dt_bias'] = misc[:, COL_DTB:COL_DTB + GDN_HEADS]
    out['gdn_out_norm'] = misc[:, COL_ONORM:COL_ONORM + GDN_DH]
    return out


def _ffn_fwd(h, pre, wgu, wd, tag):
    s = h.shape[0]
    u, = rowcall(_rms, [h], [pre], [(D_MODEL, BF)], rows=512, total=s, name=tag + "_pre")
    if callable(wgu):
        wgu = wgu(u)
    gate, up, act = mm_swiglu(u, wgu, name=tag + "_gate_up")
    if callable(wd):
        wd = wd(act)
    f = mm(act, wd, name=tag + "_down")
    return u, gate, up, act, f


def _half_rms(a, g):
    return 0.5 * _rms(a, g)


def _ffn_bwd(dh_out, h, pre, post, wgu, wd, saved, tag, on_dwd=None, on_dwgu=None):
    u, gate, up, act, f = saved
    s = h.shape[0]

    def b_post(dh, fv, pg):
        return jax.vjp(_half_rms, fv, pg)[1](dh)

    df, dpost = rowcall(b_post, [dh_out, f], [post], [(D_MODEL, BF)], [(1, D_MODEL)], rows=512, total=s,
                        name=tag + "_bwd_post")
    dwd = mm(act, df, ta=True, out_dtype=BF, name=tag + "_bwd_dwd")
    dgate, dup = mm_dswiglu(df, wd, gate, up, name=tag + "_bwd_dact", token=on_dwd(dwd) if on_dwd else None)
    dwg = mm(u, dgate, ta=True, out_dtype=BF, name=tag + "_bwd_dwg")
    dwu = mm(u, dup, ta=True, out_dtype=BF, name=tag + "_bwd_dwu")
    du = mm_pair(dgate, dup, wgu, name=tag + "_bwd_du", token=on_dwgu(dwg, dwu) if on_dwgu else None)

    def b_pre(dh, duv, hv, pg):
        dx, dpre = jax.vjp(_rms, hv, pg)[1](duv)
        return dh + dx, dpre

    dh, dpre = rowcall(b_pre, [dh_out, du, h], [pre], [(D_MODEL, F32)], [(1, D_MODEL)], rows=512, total=s,
                       name=tag + "_bwd_pre")
    return dh, dwg, dwu, dwd, dpre, dpost


def _residual_rms(h, a, g):
    return h + _rms(a, g)


def _bwd_residual(dh, a, g):
    return jax.vjp(_rms, a, g)[1](dh)


def _step(a):
    x, mem = a['x'][0], a['mem'][0]
    s = x.shape[0]
    me = 4 * lax.axis_index("x") + 2 * lax.axis_index("y") + lax.axis_index("c")
    w2 = {nm: a[nm][0] for nm in WEIGHTS}
    m2 = {nm: a['m_' + nm][0] for nm in WEIGHTS}
    v2 = {nm: a['v_' + nm][0] for nm in WEIGHTS}
    small = {nm: w2[nm][None] for nm in WEIGHTS if nm not in BIG and nm != 'gdn_conv_w'}

    def ff_cols(w):
        return _pad_to(w, (D_MODEL, FF_SHARD_PAD)).astype(BF)

    def ff_rows(w):
        return _pad_to(w, (FF_SHARD_PAD, D_MODEL)).astype(BF)

    whole = (None, 0, 0, 0)
    conv_pad = 256
    g_in = [ff_cols(w2['ffn1_w_gate']), ff_cols(w2['ffn1_w_up']), ff_rows(w2['ffn1_w_down']),
            ff_cols(w2['ffn2_w_gate']), ff_cols(w2['ffn2_w_up']), ff_rows(w2['ffn2_w_down']),
            _pad_to(w2['w_in'], (D_MODEL, IN_SHARD_PAD)).astype(BF), w2['w_out'].astype(BF),
            w2['mem_w_q'].astype(BF), w2['mem_w_kv'].astype(BF), w2['mem_w_o'].astype(BF),
            _pad_to(w2['gdn_conv_w'], (8, conv_pad))]
    g_out = [jax.ShapeDtypeStruct((D_MODEL, 2 * D_FF_PAD), BF), jax.ShapeDtypeStruct((D_FF_PAD, D_MODEL), BF),
             jax.ShapeDtypeStruct((D_MODEL, 2 * D_FF_PAD), BF), jax.ShapeDtypeStruct((D_FF_PAD, D_MODEL), BF),
             jax.ShapeDtypeStruct((D_MODEL, N_DEV * IN_SHARD_PAD), BF), jax.ShapeDtypeStruct((D_MODEL, D_MODEL), BF),
             jax.ShapeDtypeStruct((D_MODEL, D_MODEL), BF), jax.ShapeDtypeStruct((D_MODEL, 2 * D_MODEL), BF),
             jax.ShapeDtypeStruct((D_MODEL, D_MODEL), BF), jax.ShapeDtypeStruct((8, N_DEV * conv_pad), F32)]
    sp_, dm = FF_SHARD_PAD, D_MODEL // N_DEV
    g_tr = [(0, whole, 0, (1, 0, sp_, sp_)), (1, whole, 0, (1, D_FF_PAD, sp_, sp_)), (2, whole, 1, (0, 0, sp_, sp_)),
            (3, whole, 2, (1, 0, sp_, sp_)), (4, whole, 2, (1, D_FF_PAD, sp_, sp_)), (5, whole, 3, (0, 0, sp_, sp_)),
            (6, whole, 4, (1, 0, IN_SHARD_PAD, IN_SHARD_PAD)), (7, whole, 5, (0, 0, dm, dm)),
            (8, whole, 6, (0, 0, dm, dm)), (9, whole, 7, (1, 0, 2 * dm, 2 * dm)), (10, whole, 8, (0, 0, dm, dm)),
            (11, whole, 9, (1, 0, conv_pad, conv_pad))]
    def pick(idx):
        ins = sorted({g_tr[k][0] for k in idx})
        outs = sorted({g_tr[k][2] for k in idx})
        tr = [(ins.index(g_tr[k][0]), g_tr[k][1], outs.index(g_tr[k][2]), g_tr[k][3]) for k in idx]
        return [g_in[i] for i in ins], [g_out[o] for o in outs], tr

    stages, after = [], g_in[0]
    for nm, idx in (("gate_up", [0, 1]), ("down", [2]), ("mix", [6, 7, 11]), ("late", [8, 9, 10, 3, 4, 5])):
        st_in, st_out, st_tr = pick(idx)
        st = exchange_start(st_in, place_own(st_in, st_out, st_tr), st_tr, after, "gather_%s_start" % nm)
        stages.append((st, st_tr, "gather_%s_wait" % nm))
        after = st[4]
    g_token = after

    def gather_wait(k, after_):
        (send_, recv_, src_, land_, _), tr_, nm_ = stages[k]
        return exchange_wait(send_, recv_, src_, land_, after_, tr_, nm_)

    bias_row = _pad_to(small['fox_f_bias'], (1, 128))
    gate_prm = _pad_to(jnp.concatenate([_pad_to(small['gdn_a_log'], (1, 128 - SMALL_A)),
                                        _pad_to(small['gdn_dt_bias'], (1, 128 - SMALL_A))], axis=0),
                       (8, 128 - SMALL_A))
    gate_prm = jnp.pad(gate_prm, ((0, 0), (SMALL_A, 0)))
    onorm = small['gdn_out_norm']

    late = {}

    def wgu1_when(u):
        late['wgu1'], = gather_wait(0, u)
        return late['wgu1']

    def wd1_when(act):
        late['wd1'], = gather_wait(1, act)
        return late['wd1']

    sv1 = _ffn_fwd(x, small['ffn1_pre_norm'] + g_token[0, 0], wgu1_when, wd1_when, "ffn1")
    wgu1, wd1 = late['wgu1'], late['wd1']
    h1, = rowcall(lambda h, f, g: h + _half_rms(f, g), [x, sv1[4]], [small['ffn1_post_norm']], [(D_MODEL, F32)],
                  rows=512, total=s, name="ffn1_out")
    w_in_g, w_out, conv_g = gather_wait(2, h1)
    w_in = jnp.concatenate([w_in_g[:, j * IN_SHARD_PAD:j * IN_SHARD_PAD + IN_SHARD] for j in range(N_DEV)],
                           axis=1)
    sp = [0, 512, 1024, 1536, 1544, 2056, 2568, 3080, 3592, 3596, 3600]
    fq, fk, fv, ff, gq, gk, gv, gz, gb, ga = [w_in[:, sp[i]:sp[i + 1]] for i in range(10)]
    w_proj = jnp.concatenate([fq, fk, fv, gq, gk, gv, gz, ff, gb, ga,
                              jnp.zeros((D_MODEL, PROJ_W - 3584 - 16), BF)], axis=1)
    conv_w8 = conv_g.reshape(8, N_DEV, conv_pad)[:, :, :CONV_CH // N_DEV].reshape(8, CONV_CH)


    u2, = rowcall(_rms, [h1], [small['mix_pre_norm']], [(D_MODEL, BF)], rows=512, total=s, name="mix_pre")
    proj = mm(u2, w_proj, name="mix_proj")
    f_cum = fox_f_fwd(proj, bias_row)
    f_heads = f_cum[:, :FOX_HEADS]
    qkv_bf = proj[:, :3 * FOX_HEADS * FOX_DH].astype(BF)
    xk, xv = _fox_extras(s, 1.0, -f_heads), _fox_extras(s, 1.0, None)
    fox_flat, lse = fox_fwd(qkv_bf, _fox_extras(s, f_heads, 1.0), xk, xv)
    lse_heads = lse[:, :, :2].transpose(1, 0, 2).reshape(s, FOX_HEADS)
    cqkv = conv_fwd(proj, conv_w8)
    g_l, b_l = rowcall(_gdn_gates, [(proj, 128, SMALL_BLOCK128)], [gate_prm], [(512, F32), (512, F32)],
                       rows=512, total=s, name="gdn_gates")
    gbb = jnp.concatenate([g_l, b_l], axis=1)
    gdn_o, states = gdn_fwd(cqkv, proj, gbb, onorm)
    mixed = jnp.concatenate([fox_flat, gdn_o], axis=1).astype(BF)
    mo = mm(mixed, w_out, name="mix_out")
    h2, = rowcall(_residual_rms, [h1, mo], [small['mix_post_norm']], [(D_MODEL, F32)], rows=512, total=s,
                  name="mix_res")

    hq, = rowcall(_rms, [h2], [small['mem_pre_norm']], [(D_MODEL, BF)], rows=512, total=s, name="mem_pre")
    mn, = rowcall(_rms, [mem], [small['mem_kv_norm']], [(D_MODEL, BF)], rows=256, total=mem.shape[0], name="mem_kvn")
    wgu2, wd2, w_q, w_kv, w_o = gather_wait(3, h2)
    q_mem = mm(hq, w_q, name="mem_q")
    kv_mem = mm(mn, w_kv, name="mem_kv")
    o_mem = xattn_fwd(q_mem, kv_mem)
    c_mem = mm(o_mem, w_o, name="mem_o")
    h3, = rowcall(_residual_rms, [h2, c_mem], [small['mem_post_norm']], [(D_MODEL, F32)], rows=512, total=s,
                  name="mem_res")

    sv2 = _ffn_fwd(h3, small['ffn2_pre_norm'], wgu2, wd2, "ffn2")

    def b_loss(h, f, tgt, g):
        err = h + _half_rms(f, g) - tgt
        part = 0.5 * jnp.sum(jnp.mean(err * err, axis=-1, keepdims=True), axis=0, keepdims=True)
        return err * (1.0 / D_MODEL), jnp.broadcast_to(part, (1, 128))

    dy, loss_acc = rowcall(b_loss, [h3, sv2[4], a['loss_target'][0]], [small['ffn2_post_norm']], [(D_MODEL, F32)],
                           [(1, 128)], rows=512, total=s, name="loss")

    grads = {}
    dh3, dwg2, dwu2, dwd2, grads['ffn2_pre_norm'], grads['ffn2_post_norm'] = _ffn_bwd(
        dy, h3, small['ffn2_pre_norm'], small['ffn2_post_norm'], wgu2, wd2, sv2, "ffn2")

    lead = ("lead", 0, 1, 0)

    def land(r, c, dt=BF):
        return jax.ShapeDtypeStruct((N_DEV, r, c), dt)

    ffn_tr = [(0, (1, 0, sp_, sp_), 0, lead), (1, (1, 0, sp_, sp_), 1, lead), (2, (0, 0, sp_, FF_SHARD), 2, lead)]
    ffn_land = [land(D_MODEL, sp_), land(D_MODEL, sp_), land(FF_SHARD, D_MODEL)]
    a_in = [dwg2, dwu2, dwd2]
    a_send, a_recv, a_src, a_land, a_token = exchange_start(a_in, place_own(a_in, ffn_land, ffn_tr), ffn_tr, dh3,
                                                            "reduce_ffn2_start")

    dc, grads['mem_post_norm'] = rowcall(_bwd_residual, [dh3, c_mem], [small['mem_post_norm'] + a_token[0, 0]],
                                         [(D_MODEL, BF)],
                                         [(1, D_MODEL)], rows=512, total=s, name="mem_bwd_res")
    d_o = mm(dc, w_o, tb=True, name="mem_bwd_do")
    dw_o = mm(o_mem, dc, ta=True, out_dtype=BF, name="mem_bwd_dwo")
    dq_mem, dkv = xattn_bwd(q_mem, kv_mem, d_o)
    dhq = mm(dq_mem, w_q, tb=True, name="mem_bwd_dhq")
    dw_q = mm(hq, dq_mem, ta=True, out_dtype=BF, name="mem_bwd_dwq")
    dmn = mm(dkv, w_kv, tb=True, name="mem_bwd_dmn")
    dw_kv = mm(mn, dkv, ta=True, out_dtype=BF, name="mem_bwd_dwkv")
    _, grads['mem_kv_norm'] = rowcall(lambda d, mv, g: jax.vjp(_rms, mv, g)[1](d), [dmn, mem],
                                      [small['mem_kv_norm']], [(D_MODEL, F32)], [(1, D_MODEL)], rows=256,
                                      total=mem.shape[0], name="mem_bwd_kvn")

    def b_pre(dh, duv, hv, pg):
        dx, dpre = jax.vjp(_rms, hv, pg)[1](duv)
        return dh + dx, dpre

    dh2, grads['mem_pre_norm'] = rowcall(b_pre, [dh3, dhq, h2], [small['mem_pre_norm']], [(D_MODEL, F32)],
                                         [(1, D_MODEL)], rows=512, total=s, name="mem_bwd_pre")

    dmo, grads['mix_post_norm'] = rowcall(_bwd_residual, [dh2, mo], [small['mix_post_norm']], [(D_MODEL, BF)],
                                          [(1, D_MODEL)], rows=512, total=s, name="mix_bwd_res")
    d_mixed = mm(dmo, w_out, tb=True, name="mix_bwd_dmixed")
    dw_out = mm(mixed, dmo, ta=True, out_dtype=BF, name="mix_bwd_dwout")
    def b_delta(do, o):
        sel = (_iota2((512, 128), 0) // FOX_DH == _iota2((512, 128), 1)).astype(F32)
        return hdot(do * o, sel)

    delta, = rowcall(b_delta, [(d_mixed, 512, 0), fox_flat], [], [(128, F32)], rows=512, total=s, name="fox_delta")
    dfox_q, dfox_k, dvf, sum_q, sum_k = fox_bwd(qkv_bf, d_mixed[:, :512].astype(BF), xk, xv,
                                                _fox_extras(s, f_heads - lse_heads, 1.0),
                                                _fox_extras(s, -delta[:, :FOX_HEADS], None))
    per_head = lambda a: jnp.stack([a[:, 0::2 * FOX_DH], a[:, 1::2 * FOX_DH]], axis=-1).reshape(s, FOX_HEADS)
    d_f = _pad_to(per_head(sum_q) - per_head(sum_k), (s, 128))
    dsmall_f, dbias = fox_f_bwd(proj, bias_row, d_f)
    grads['fox_f_bias'] = dbias[:, :FOX_HEADS]
    dcqkv, dz, dgb, grads['gdn_out_norm'] = gdn_bwd(cqkv, proj, gbb, onorm, states, d_mixed)

    def b_gates(sm, dsf, dg, db, prm):
        dsm, dprm = jax.vjp(_gdn_gates, sm, prm)[1]((dg, db))
        return dsm + dsf, dprm

    dsmall, dprm = rowcall(b_gates, [(proj, 128, SMALL_BLOCK128), dsmall_f, (dgb, 512, 0), (dgb, 512, 1)], [gate_prm],
                           [(128, F32)],
                           [(8, 128)], rows=512, total=s, name="gdn_bwd_gates")
    grads['gdn_a_log'] = dprm[0:1, SMALL_A:SMALL_A + GDN_HEADS]
    grads['gdn_dt_bias'] = dprm[1:2, SMALL_A:SMALL_A + GDN_HEADS]
    dqkv_pre, dconv8 = conv_bwd(proj, conv_w8, dcqkv)
    dproj = jnp.concatenate([dfox_q, dfox_k, dvf, dqkv_pre, dz, dsmall,
                             jnp.zeros((s, PROJ_W - 3584 - 128), F32)], axis=1).astype(BF)
    du2 = mm(dproj, w_proj, tb=True, name="mix_bwd_du")
    dw_proj = mm(u2, dproj, ta=True, out_dtype=BF, name="mix_bwd_dwproj")
    dh1, grads['mix_pre_norm'] = rowcall(b_pre, [dh2, du2, h1], [small['mix_pre_norm']], [(D_MODEL, F32)],
                                         [(1, D_MODEL)], rows=512, total=s, name="mix_bwd_pre")

    dw_in = jnp.concatenate([dw_proj[:, :1536], dw_proj[:, 3584:3592], dw_proj[:, 1536:3584],
                             dw_proj[:, 3592:3600]], axis=1)
    gap = jnp.zeros((D_MODEL, IN_SHARD_PAD - IN_SHARD), BF)
    dw_in = jnp.concatenate([piece for j in range(N_DEV) for piece in (dw_in[:, j * IN_SHARD:(j + 1) * IN_SHARD], gap)],
                            axis=1)
    b_in = [dw_in, dw_out, dw_q, dw_kv, dw_o]
    b_tr = [(0, (1, 0, IN_SHARD_PAD, IN_SHARD_PAD), 0, lead), (1, (0, 0, dm, dm), 1, lead), (2, (0, 0, dm, dm), 2, lead),
            (3, (1, 0, 2 * dm, 2 * dm), 3, lead), (4, (0, 0, dm, dm), 4, lead)]
    b_shapes = [land(D_MODEL, IN_SHARD_PAD), land(dm, D_MODEL), land(dm, D_MODEL), land(D_MODEL, 2 * dm),
                land(dm, D_MODEL)]
    b_land = place_own(b_in, b_shapes, b_tr)
    b_send, b_recv, b_src, b_land, b_token = exchange_start(b_in, b_land, b_tr, dh1, "reduce_mix_start")

    def start_down_reduce(dwd):
        tr = ffn_tr[2:]
        tr = [(0, tr[0][1], 0, tr[0][3])]
        late['c_down'] = (exchange_start([dwd], place_own([dwd], ffn_land[2:], tr), tr, dwd, "reduce_ffn1_down_start"), tr)
        return late['c_down'][0][4]

    def start_gate_up_reduce(dwg, dwu):
        tr = ffn_tr[:2]
        late['c_gu'] = (exchange_start([dwg, dwu], place_own([dwg, dwu], ffn_land[:2], tr), tr, dwu,
                                       "reduce_ffn1_gu_start"), tr)
        return late['c_gu'][0][4]

    grad_x, _, _, _, grads['ffn1_pre_norm'], grads['ffn1_post_norm'] = _ffn_bwd(
        dh1, x, small['ffn1_pre_norm'], small['ffn1_post_norm'] + b_token[0, 0], wgu1, wd1, sv1, "ffn1",
        on_dwd=start_down_reduce, on_dwgu=start_gate_up_reduce)

    gpack = _pack(lambda nm: grads[nm], conv=dconv8[:CONV_W], loss=loss_acc[:, :1])
    gsum_parts, = exchange([gpack], [land(PACK_ROWS, D_MODEL, F32)], [(0, whole, 0, lead)], "reduce_small")
    a_got = exchange_wait(a_send, a_recv, a_src, a_land, gsum_parts, ffn_tr, "reduce_ffn2_wait")
    b_got = exchange_wait(b_send, b_recv, b_src, b_land, gsum_parts, b_tr, "reduce_mix_wait")
    recv = dict(zip(['ffn2_w_gate', 'ffn2_w_up', 'ffn2_w_down', 'w_in', 'w_out', 'mem_w_q', 'mem_w_kv', 'mem_w_o'],
                    a_got + b_got))

    out_g, out_d, out_m, out_v = {}, {}, {}, {}

    def update(nm):
        r = recv[nm]
        res = adamw(w2[nm], m2[nm], v2[nm], [(r, r.shape[2], 0, d) for d in range(N_DEV)], "adamw_" + nm)
        out_g[nm], out_d[nm], out_m[nm], out_v[nm] = res

    for nm in recv:
        update(nm)
    wp = _pack(lambda nm: small[nm])
    mp = _pack(lambda nm: m2[nm][None])
    vp = _pack(lambda nm: v2[nm][None])
    pg, pd, pm, pv = adamw(wp, mp, vp, [(gsum_parts, D_MODEL, 0, d) for d in range(N_DEV)], "adamw_small")
    for dst, p in ((out_g, pg), (out_d, pd), (out_m, pm), (out_v, pv)):
        dst.update({k: val[0] for k, val in _unpack(p).items()})
    loss = pg[ROW_MISC, COL_LOSS]
    conv_g = lax.dynamic_slice_in_dim(pg[ROW_CONV:ROW_CONV + 6].reshape(CONV_W, CONV_CH), me * (CONV_CH // N_DEV),
                                      CONV_CH // N_DEV, axis=1)
    res = adamw(w2['gdn_conv_w'], m2['gdn_conv_w'], v2['gdn_conv_w'], [conv_g], "adamw_conv")
    out_g['gdn_conv_w'], out_d['gdn_conv_w'], out_m['gdn_conv_w'], out_v['gdn_conv_w'] = res

    done = sum(out_d[nm][0, 0] for nm in recv) + out_d['gdn_conv_w'][0, 0] + pd[0, 0]
    after = jnp.zeros((8, 128), F32) + done
    c_got = []
    for key, nm in (('c_gu', "reduce_ffn1_gu_wait"), ('c_down', "reduce_ffn1_down_wait")):
        (c_send, c_recv, c_src, c_land, _), tr = late[key]
        c_got += exchange_wait(c_send, c_recv, c_src, c_land, after, tr, nm)
    recv = dict(zip(['ffn1_w_gate', 'ffn1_w_up', 'ffn1_w_down'], c_got))
    for nm in recv:
        update(nm)

    def depth(t):
        return t[None]

    return (loss, grad_x[None], *[depth(out_g[nm]) for nm in WEIGHTS], *[depth(out_d[nm]) for nm in WEIGHTS],
            *[depth(out_m[nm]) for nm in WEIGHTS], *[depth(out_v[nm]) for nm in WEIGHTS])


def kernel(x, mem, ffn1_pre_norm, ffn1_w_gate, ffn1_w_up, ffn1_w_down, ffn1_post_norm, mix_pre_norm, w_in, fox_f_bias, gdn_conv_w, gdn_a_log, gdn_dt_bias, gdn_out_norm, w_out, mix_post_norm, mem_pre_norm, mem_kv_norm, mem_w_q, mem_w_kv, mem_w_o, mem_post_norm, ffn2_pre_norm, ffn2_w_gate, ffn2_w_up, ffn2_w_down, ffn2_post_norm, loss_target, m_ffn1_pre_norm, m_ffn1_w_gate, m_ffn1_w_up, m_ffn1_w_down, m_ffn1_post_norm, m_mix_pre_norm, m_w_in, m_fox_f_bias, m_gdn_conv_w, m_gdn_a_log, m_gdn_dt_bias, m_gdn_out_norm, m_w_out, m_mix_post_norm, m_mem_pre_norm, m_mem_kv_norm, m_mem_w_q, m_mem_w_kv, m_mem_w_o, m_mem_post_norm, m_ffn2_pre_norm, m_ffn2_w_gate, m_ffn2_w_up, m_ffn2_w_down, m_ffn2_post_norm, v_ffn1_pre_norm, v_ffn1_w_gate, v_ffn1_w_up, v_ffn1_w_down, v_ffn1_post_norm, v_mix_pre_norm, v_w_in, v_fox_f_bias, v_gdn_conv_w, v_gdn_a_log, v_gdn_dt_bias, v_gdn_out_norm, v_w_out, v_mix_post_norm, v_mem_pre_norm, v_mem_kv_norm, v_mem_w_q, v_mem_w_kv, v_mem_w_o, v_mem_post_norm, v_ffn2_pre_norm, v_ffn2_w_gate, v_ffn2_w_up, v_ffn2_w_down, v_ffn2_post_norm):
    return _step(dict(locals()))
```

```python
import functools

import jax
import jax.numpy as jnp
from jax import lax
from jax.experimental import pallas as pl
from jax.experimental.pallas import tpu as pltpu

F32 = jnp.float32
BF = jnp.bfloat16
HI = lax.Precision.HIGHEST
MESH = pl.DeviceIdType.MESH

N_DEV = 8
EPS = 1e-6
D_MODEL = 1024
D_FF = 2816
FF_SHARD = D_FF // N_DEV
FF_SHARD_PAD = 384
D_FF_PAD = FF_SHARD_PAD * N_DEV
FOX_HEADS, FOX_DH = 8, 64
GDN_HEADS, GDN_DH = 4, 128
GDN_CHUNK = 64
CONV_W = 4
MEM_HEADS, MEM_DH = 4, 256
IN_W = 3600
IN_SHARD = IN_W // N_DEV
IN_SHARD_PAD = 512
PROJ_W = 4096
SMALL_F, SMALL_B, SMALL_A = 0, 8, 12

ADAM_LR, ADAM_B1, ADAM_B2, ADAM_EPS, ADAM_WD, ADAM_STEP = 0.001, 0.9, 0.999, 1e-08, 0.01, 10

VMEM_LIMIT = 56 * 1024 * 1024


def _params(sem=None):
    return pltpu.CompilerParams(dimension_semantics=sem, vmem_limit_bytes=VMEM_LIMIT)


def _tile(n, pref, unit=128):
    if n <= pref:
        return n
    t = (pref // unit) * unit
    while t > unit and n % t:
        t -= unit
    assert n % t == 0, (n, pref)
    return t


@functools.partial(jax.custom_vjp, nondiff_argnums=(2, 3))
def bdot(a, b, ca, cb):
    return lax.dot_general(a.astype(BF), b.astype(BF), (((ca,), (cb,)), ((), ())), preferred_element_type=F32)


def _bdot_fwd(a, b, ca, cb):
    return bdot(a, b, ca, cb), (a, b)


def _bdot_bwd(ca, cb, res, g):
    a, b = res
    da = bdot(g, b, 1, 1 - cb) if ca == 1 else bdot(b, g, 1 - cb, 1)
    db = bdot(a, g, 1 - ca, 0) if cb == 0 else bdot(g, a, 0, 1 - ca)
    return da, db


bdot.defvjp(_bdot_fwd, _bdot_bwd)


def hdot(a, b):
    return jnp.dot(a, b, precision=HI, preferred_element_type=F32)


def mdot(a, b):
    return jnp.dot(a, b, precision=lax.Precision.HIGH, preferred_element_type=F32)


def _iota2(shape, dim):
    return lax.broadcasted_iota(jnp.int32, shape, dim)


def _sigmoid(x):
    return 1.0 / (1.0 + jnp.exp(-x))


def _silu(x):
    return x * _sigmoid(x)


def _softplus(x):
    return jnp.maximum(x, 0.0) + jnp.log(1.0 + jnp.exp(-jnp.abs(x)))


def _rms(x, gain):
    return x * lax.rsqrt(jnp.mean(x * x, axis=-1, keepdims=True) + EPS) * gain


def mm(a, b, *, name, ta=False, tb=False, out_dtype=F32, tm=1024, tn=1024, tk=1024, token=None):
    m, k = (a.shape[1], a.shape[0]) if ta else a.shape
    n = b.shape[0] if tb else b.shape[1]
    assert k == (b.shape[1] if tb else b.shape[0]), (a.shape, b.shape, ta, tb)
    tm, tn, tk = _tile(m, tm), _tile(n, tn), _tile(k, tk)
    nk = k // tk
    dims = (((0 if ta else 1,), (1 if tb else 0,)), ((), ()))

    def kern(a_ref, b_ref, *rest):
        o_ref, scratch = (rest[1], rest[2:]) if token is not None else (rest[0], rest[1:])

        def part():
            return lax.dot_general(a_ref[...].astype(BF), b_ref[...].astype(BF), dims, preferred_element_type=F32)

        if nk == 1:
            o_ref[...] = part().astype(o_ref.dtype)
            return
        acc_ref, = scratch
        kk = pl.program_id(2)

        @pl.when(kk == 0)
        def _():
            acc_ref[...] = part()

        @pl.when(kk > 0)
        def _():
            acc_ref[...] += part()

        @pl.when(kk == nk - 1)
        def _():
            o_ref[...] = acc_ref[...].astype(o_ref.dtype)

    a_spec = pl.BlockSpec((tk, tm), lambda i, j, kk: (kk, i)) if ta else pl.BlockSpec((tm, tk), lambda i, j, kk: (i, kk))
    b_spec = pl.BlockSpec((tn, tk), lambda i, j, kk: (j, kk)) if tb else pl.BlockSpec((tk, tn), lambda i, j, kk: (kk, j))
    return pl.pallas_call(
        kern, name=name, grid=(m // tm, n // tn, nk),
        in_specs=[a_spec, b_spec] + ([pl.BlockSpec((8, 128), lambda i, j, kk: (0, 0))] if token is not None else []),
        out_specs=pl.BlockSpec((tm, tn), lambda i, j, kk: (i, j)),
        out_shape=jax.ShapeDtypeStruct((m, n), out_dtype),
        scratch_shapes=[pltpu.VMEM((tm, tn), F32)] if nk > 1 else [],
        compiler_params=_params(("parallel", "parallel", "arbitrary")),
    )(*((a, b) if token is None else (a, b, token)))


def mm_swiglu(a, wgu, *, name):
    m, k = a.shape
    nh = wgu.shape[1] // 2
    tm, tn = _tile(m, 1024), _tile(nh, 512)
    nj = nh // tn

    def kern(a_ref, bg_ref, bu_ref, g_ref, u_ref, act_ref):
        av = a_ref[...]
        g = jnp.dot(av, bg_ref[...], preferred_element_type=F32).astype(BF)
        u = jnp.dot(av, bu_ref[...], preferred_element_type=F32).astype(BF)
        g_ref[...] = g
        u_ref[...] = u
        act_ref[...] = (_silu(g.astype(F32)) * u.astype(F32)).astype(BF)

    tile = pl.BlockSpec((tm, tn), lambda i, j: (i, j))
    out = jax.ShapeDtypeStruct((m, nh), BF)
    return pl.pallas_call(
        kern, name=name, grid=(m // tm, nj),
        in_specs=[pl.BlockSpec((tm, k), lambda i, j: (i, 0)), pl.BlockSpec((k, tn), lambda i, j: (0, j)),
                  pl.BlockSpec((k, tn), lambda i, j: (0, j + nj))],
        out_specs=[tile, tile, tile], out_shape=[out, out, out],
        compiler_params=_params(("parallel", "parallel")),
    )(a, wgu, wgu)


def mm_dswiglu(df, wd, gate, up, *, name, token=None):
    m, k = df.shape
    nh = wd.shape[0]
    tm, tn = _tile(m, 1024), _tile(nh, 512)

    def kern(df_ref, wd_ref, g_ref, u_ref, *rest):
        dg_ref, du_ref = rest[-2:]
        da = lax.dot_general(df_ref[...], wd_ref[...], (((1,), (1,)), ((), ())), preferred_element_type=F32)
        g, u = g_ref[...].astype(F32), u_ref[...].astype(F32)
        sg = _sigmoid(g)
        dg_ref[...] = (da * u * (sg * (1.0 + g * (1.0 - sg)))).astype(BF)
        du_ref[...] = (da * (g * sg)).astype(BF)

    tile = pl.BlockSpec((tm, tn), lambda i, j: (i, j))
    out = jax.ShapeDtypeStruct((m, nh), BF)
    extra = [pl.BlockSpec((8, 128), lambda i, j: (0, 0))] if token is not None else []
    return pl.pallas_call(
        kern, name=name, grid=(m // tm, nh // tn),
        in_specs=[pl.BlockSpec((tm, k), lambda i, j: (i, 0)), pl.BlockSpec((tn, k), lambda i, j: (j, 0)), tile, tile]
        + extra,
        out_specs=[tile, tile], out_shape=[out, out],
        compiler_params=_params(("parallel", "parallel")),
    )(*((df, wd, gate, up) if token is None else (df, wd, gate, up, token)))


def mm_pair(a1, a2, wgu, *, name, token=None):
    m, nh = a1.shape
    n = wgu.shape[0]
    tm, tn, tk = _tile(m, 1024), _tile(n, 1024), _tile(nh, 1024)
    nk = nh // tk
    nt = (((1,), (1,)), ((), ()))

    def kern(a1_ref, a2_ref, b1_ref, b2_ref, *rest):
        o_ref, acc_ref = rest[-2:]
        kk = pl.program_id(2)

        def part():
            return (lax.dot_general(a1_ref[...], b1_ref[...], nt, preferred_element_type=F32)
                    + lax.dot_general(a2_ref[...], b2_ref[...], nt, preferred_element_type=F32))

        @pl.when(kk == 0)
        def _():
            acc_ref[...] = part()

        @pl.when(kk > 0)
        def _():
            acc_ref[...] += part()

        @pl.when(kk == nk - 1)
        def _():
            o_ref[...] = acc_ref[...]

    a_spec = pl.BlockSpec((tm, tk), lambda i, j, kk: (i, kk))
    extra = [pl.BlockSpec((8, 128), lambda i, j, kk: (0, 0))] if token is not None else []
    return pl.pallas_call(
        kern, name=name, grid=(m // tm, n // tn, nk),
        in_specs=[a_spec, a_spec, pl.BlockSpec((tn, tk), lambda i, j, kk: (j, kk)),
                  pl.BlockSpec((tn, tk), lambda i, j, kk: (j, kk + nk))] + extra,
        out_specs=pl.BlockSpec((tm, tn), lambda i, j, kk: (i, j)),
        out_shape=jax.ShapeDtypeStruct((m, n), F32),
        scratch_shapes=[pltpu.VMEM((tm, tn), F32)],
        compiler_params=_params(("parallel", "parallel", "arbitrary")),
    )(*((a1, a2, wgu, wgu) if token is None else (a1, a2, wgu, wgu, token)))


def _row_spec(item, rows):
    if not isinstance(item, tuple):
        return item, pl.BlockSpec((rows, item.shape[1]), lambda i: (i, 0))
    if len(item) == 3:
        arr, w, c = item
        return arr, pl.BlockSpec((rows, w), lambda i: (i, c))
    arr, w, c, lead = item
    return arr, pl.BlockSpec((None, rows, w), lambda i: (lead, i, c))


def _whole_spec(item):
    if not isinstance(item, tuple):
        return item, pl.BlockSpec(item.shape, lambda i: (0,) * item.ndim)
    arr, w, c = item
    return arr, pl.BlockSpec((arr.shape[0], w), lambda i: (0, c))


def rowcall(body, tiled, whole, outs, accs=(), *, rows, total, name):
    rows = min(rows, total)
    assert total % rows == 0
    t_arr, t_spec = zip(*[_row_spec(t, rows) for t in tiled])
    w_arr, w_spec = zip(*[_whole_spec(w) for w in whole]) if whole else ((), ())
    nt, nw, no, na = len(t_arr), len(w_arr), len(outs), len(accs)

    def kern(*refs):
        vals = [r[...] for r in refs[:nt + nw]]
        res = body(*vals)
        if not isinstance(res, (tuple, list)):
            res = (res,)
        assert len(res) == no + na, (name, len(res), no, na)
        for r, v in zip(refs[nt + nw:nt + nw + no], res[:no]):
            r[...] = v.astype(r.dtype)
        if na:
            acc_refs = refs[nt + nw + no:]

            @pl.when(pl.program_id(0) == 0)
            def _():
                for r in acc_refs:
                    r[...] = jnp.zeros_like(r)

            for r, v in zip(acc_refs, res[no:]):
                r[...] += v

    out_shape = [jax.ShapeDtypeStruct((total, w), d) for w, d in outs] + [jax.ShapeDtypeStruct(s, F32) for s in accs]
    out_specs = [pl.BlockSpec((rows, w), lambda i: (i, 0)) for w, _ in outs] + \
                [pl.BlockSpec(s, lambda i: (0, 0)) for s in accs]
    res = pl.pallas_call(
        kern, name=name, grid=(total // rows,),
        in_specs=list(t_spec) + list(w_spec), out_specs=out_specs, out_shape=out_shape,
        compiler_params=_params(("arbitrary",) if na else ("parallel",)),
    )(*t_arr, *w_arr)
    return res


def _colsum(x):
    return jnp.sum(x, axis=0, keepdims=True)


def _gdn_chunk(q, k, v, z, gb, bb, state, gain):
    c = GDN_CHUNK
    nh = len(q)
    hs = range(nh)
    r64, c64 = _iota2((c, c), 0), _iota2((c, c), 1)
    incl = r64 >= c64
    strict = r64 > c64
    ltri = incl.astype(F32)
    eye = (r64 == c64).astype(F32)
    pick = (_iota2((GDN_DH, c), 0) == _iota2((GDN_DH, c), 1)).astype(F32)
    last = (_iota2((c, GDN_DH), 0) == c - 1).astype(F32)

    qn = [q[h] * lax.rsqrt(jnp.sum(q[h] * q[h], axis=-1, keepdims=True) + EPS) * (GDN_DH ** -0.5) for h in hs]
    kn = [k[h] * lax.rsqrt(jnp.sum(k[h] * k[h], axis=-1, keepdims=True) + EPS) for h in hs]
    gc = [mdot(ltri, gb[h]) for h in hs]
    gcol = [mdot(gc[h], pick) for h in hs]
    dec = [jnp.exp(jnp.where(incl, gcol[h] - gcol[h].T, -1e30)) for h in hs]
    kb = [kn[h] * bb[h] for h in hs]
    vb = [v[h] * bb[h] for h in hs]
    kk = [bdot(kb[h], kn[h], 1, 1) for h in hs]
    p = [-jnp.where(strict, kk[h] * dec[h], 0.0) for h in hs]
    tinv = [eye + p[h] for h in hs]
    for level in range(5):
        dot = mdot if level < 2 else (lambda a, b: bdot(a, b, 1, 0))
        p = [dot(p[h], p[h]) for h in hs]
        tinv = [tinv[h] + dot(tinv[h], p[h]) for h in hs]
    egc = [jnp.exp(gc[h]) for h in hs]
    u = [mdot(tinv[h], vb[h]) for h in hs]
    w = [mdot(tinv[h], kb[h] * egc[h]) for h in hs]
    attn = [bdot(qn[h], kn[h], 1, 1) * dec[h] for h in hs]
    qd = [qn[h] * egc[h] for h in hs]
    gl = [jnp.sum(gc[h] * last, axis=0, keepdims=True) for h in hs]
    kt = [kn[h] * jnp.exp(gl[h] - gc[h]) for h in hs]
    ws = [bdot(w[h], state[h], 1, 0) for h in hs]
    qs = [bdot(qd[h], state[h], 1, 0) for h in hs]
    v_new = [u[h] - ws[h] for h in hs]
    av = [bdot(attn[h], v_new[h], 1, 0) for h in hs]
    kv = [bdot(kt[h], v_new[h], 0, 0) for h in hs]
    new_state = tuple(state[h] * jnp.exp(gl[h]) + kv[h] for h in hs)
    o = tuple(_rms(qs[h] + av[h], gain) * _silu(z[h]) for h in hs)
    return o, new_state


GDN_ROWS = 512
GDN_W = GDN_HEADS * GDN_DH


def gdn_fwd(cqkv, proj, gbb, gain):
    s = cqkv.shape[0]
    nb, cpb = s // GDN_ROWS, GDN_ROWS // GDN_CHUNK
    h4 = GDN_HEADS

    def kern(qkv_ref, z_ref, gb_ref, gain_ref, o_ref, st_ref, state):
        @pl.when(pl.program_id(0) == 0)
        def _():
            state[...] = jnp.zeros_like(state)

        gain_v = gain_ref[...]

        def step(ci, carry):
            sl = pl.ds(pl.multiple_of(ci * GDN_CHUNK, GDN_CHUNK), GDN_CHUNK)
            ins = []
            for h in range(h4):
                ln = lambda base, h=h: slice(base + h * GDN_DH, base + (h + 1) * GDN_DH)
                ins.append((qkv_ref[sl, ln(0)], qkv_ref[sl, ln(GDN_W)], qkv_ref[sl, ln(2 * GDN_W)], z_ref[sl, ln(0)],
                            gb_ref[sl, ln(0)], gb_ref[sl, ln(GDN_W)], state[h]))
            cols = [tuple(col) for col in zip(*ins)]
            o, new = _gdn_chunk(*cols[:7], gain_v)
            for h in range(h4):
                st_ref[h, ci] = ins[h][6]
                o_ref[sl, h * GDN_DH:(h + 1) * GDN_DH] = o[h]
                state[h] = new[h]
            return carry

        lax.fori_loop(0, cpb, step, 0)

    return pl.pallas_call(
        kern, name="gdn_fwd", grid=(nb,),
        in_specs=[pl.BlockSpec((GDN_ROWS, 3 * GDN_W), lambda i: (i, 0)),
                  pl.BlockSpec((GDN_ROWS, GDN_W), lambda i: (i, 6)),
                  pl.BlockSpec((GDN_ROWS, 2 * GDN_W), lambda i: (i, 0)),
                  pl.BlockSpec((1, GDN_DH), lambda i: (0, 0))],
        out_specs=[pl.BlockSpec((GDN_ROWS, GDN_W), lambda i: (i, 0)),
                   pl.BlockSpec((h4, cpb, GDN_DH, GDN_DH), lambda i: (0, i, 0, 0))],
        out_shape=[jax.ShapeDtypeStruct((s, GDN_W), F32),
                   jax.ShapeDtypeStruct((h4, s // GDN_CHUNK, GDN_DH, GDN_DH), F32)],
        scratch_shapes=[pltpu.VMEM((h4, GDN_DH, GDN_DH), F32)],
        compiler_params=_params(("arbitrary",)),
    )(cqkv, proj, gbb, gain)


def gdn_bwd(cqkv, proj, gbb, gain, states, d_mixed):
    s = cqkv.shape[0]
    nb, cpb = s // GDN_ROWS, GDN_ROWS // GDN_CHUNK
    h4 = GDN_HEADS

    def kern(qkv_ref, z_ref, gb_ref, gain_ref, st_ref, do_ref, dqkv_ref, dz_ref, dgb_ref, dgain_ref, dstate):
        @pl.when(pl.program_id(0) == 0)
        def _():
            dgain_ref[...] = jnp.zeros_like(dgain_ref)
            dstate[...] = jnp.zeros_like(dstate)

        gain_v = gain_ref[...]

        def step(t, carry):
            ci = cpb - 1 - t
            sl = pl.ds(pl.multiple_of(ci * GDN_CHUNK, GDN_CHUNK), GDN_CHUNK)
            prim, cot, dst_in = [], [], []
            for h in range(h4):
                ln = lambda base, h=h: slice(base + h * GDN_DH, base + (h + 1) * GDN_DH)
                prim.append((qkv_ref[sl, ln(0)], qkv_ref[sl, ln(GDN_W)], qkv_ref[sl, ln(2 * GDN_W)], z_ref[sl, ln(0)],
                             gb_ref[sl, ln(0)], gb_ref[sl, ln(GDN_W)], st_ref[h, ci]))
                cot.append(do_ref[sl, ln(0)])
                dst_in.append(dstate[h])
            cols = [tuple(col) for col in zip(*prim)]
            vjp = jax.vjp(_gdn_chunk, *cols, gain_v)[1]
            dq, dk, dv, dz, dg, db, dst, dgn = vjp((tuple(cot), tuple(dst_in)))
            for h in range(h4):
                ln = lambda base, h=h: slice(base + h * GDN_DH, base + (h + 1) * GDN_DH)
                dqkv_ref[sl, ln(0)] = dq[h]
                dqkv_ref[sl, ln(GDN_W)] = dk[h]
                dqkv_ref[sl, ln(2 * GDN_W)] = dv[h]
                dz_ref[sl, ln(0)] = dz[h]
                dgb_ref[sl, ln(0)] = dg[h]
                dgb_ref[sl, ln(GDN_W)] = db[h]
                dstate[h] = dst[h]
            dgain_ref[...] += dgn
            return carry

        lax.fori_loop(0, cpb, step, 0)

    def rev(width, cblock=0):
        return pl.BlockSpec((GDN_ROWS, width), lambda i: (nb - 1 - i, cblock))

    return pl.pallas_call(
        kern, name="gdn_bwd", grid=(nb,),
        in_specs=[rev(3 * GDN_W), rev(GDN_W, 6), rev(2 * GDN_W), pl.BlockSpec((1, GDN_DH), lambda i: (0, 0)),
                  pl.BlockSpec((h4, cpb, GDN_DH, GDN_DH), lambda i: (0, nb - 1 - i, 0, 0)), rev(GDN_W, 1)],
        out_specs=[rev(3 * GDN_W), rev(GDN_W), rev(2 * GDN_W), pl.BlockSpec((1, GDN_DH), lambda i: (0, 0))],
        out_shape=[jax.ShapeDtypeStruct((s, 3 * GDN_W), F32), jax.ShapeDtypeStruct((s, GDN_W), F32),
                   jax.ShapeDtypeStruct((s, 2 * GDN_W), F32), jax.ShapeDtypeStruct((1, GDN_DH), F32)],
        scratch_shapes=[pltpu.VMEM((h4, GDN_DH, GDN_DH), F32)],
        compiler_params=_params(("arbitrary",)),
    )(cqkv, proj, gbb, gain, states, d_mixed)


def _gdn_gates(small, prm):
    w = GDN_HEADS * GDN_DH
    lane, head = _iota2((128, w), 0), _iota2((128, w), 1) // GDN_DH
    sel_b = (lane == SMALL_B + head).astype(F32)
    sel_a = (lane == SMALL_A + head).astype(F32)
    prow = _iota2((8, 128), 0)
    a_log = jnp.sum(prm * (prow == 0).astype(F32), axis=0, keepdims=True)
    dt_b = jnp.sum(prm * (prow == 1).astype(F32), axis=0, keepdims=True)
    beta = _sigmoid(hdot(small, sel_b))
    g = hdot(-jnp.exp(a_log) * _softplus(small + dt_b), sel_a)
    return g, beta


CONV_ROWS = 1024
CONV_COLS = 128
CONV_BLOCK0 = 1536 // CONV_COLS


def _shift_down(prev8, cur, s):
    ext = jnp.concatenate([prev8, cur], axis=0)
    return pltpu.roll(ext, s, 0)[8:]


def _shift_up(cur, next8, s):
    n = cur.shape[0]
    ext = jnp.concatenate([cur, next8], axis=0)
    return pltpu.roll(ext, n + 8 - s, 0)[:n]


def _conv_pre(x_ref, w, ci, nchunk):
    r0 = pl.multiple_of(ci * CONV_ROWS, CONV_ROWS)
    cur = x_ref[pl.ds(r0, CONV_ROWS), :]
    prev = x_ref[pl.ds(pl.multiple_of(jnp.maximum(r0 - 8, 0), 8), 8), :]
    prev = jnp.where(ci > 0, prev, 0.0)
    shifted = [cur] + [_shift_down(prev, cur, s) for s in range(1, CONV_W)]
    pre = w[CONV_W - 1:CONV_W, :] * cur
    for s in range(1, CONV_W):
        pre = pre + w[CONV_W - 1 - s:CONV_W - s, :] * shifted[s]
    return r0, pre, shifted


def conv_fwd(proj, conv_w8):
    s = proj.shape[0]
    nchunk = s // CONV_ROWS
    ncol = 3 * GDN_HEADS * GDN_DH // CONV_COLS

    def kern(x_ref, w_ref, y_ref):
        w = w_ref[...]

        def step(ci, carry):
            r0, pre, _ = _conv_pre(x_ref, w, ci, nchunk)
            y_ref[pl.ds(r0, CONV_ROWS), :] = _silu(pre)
            return carry

        lax.fori_loop(0, nchunk, step, 0)

    return pl.pallas_call(
        kern, name="conv_fwd", grid=(ncol,),
        in_specs=[pl.BlockSpec((s, CONV_COLS), lambda j: (0, CONV_BLOCK0 + j)),
                  pl.BlockSpec((8, CONV_COLS), lambda j: (0, j))],
        out_specs=pl.BlockSpec((s, CONV_COLS), lambda j: (0, j)),
        out_shape=jax.ShapeDtypeStruct((s, ncol * CONV_COLS), F32),
        compiler_params=_params(("parallel",)),
    )(proj, conv_w8)


def conv_bwd(proj, conv_w8, dy):
    s = proj.shape[0]
    nchunk = s // CONV_ROWS
    per = 3 * GDN_HEADS * GDN_DH // CONV_COLS
    outs = []
    for part in range(1):
        def kern(x_ref, w_ref, dy_ref, dx_ref, dw_ref, dpre_ref):
            w = w_ref[...]
            rows8 = _iota2((8, CONV_COLS), 0)

            def step1(ci, dw):
                r0, pre, shifted = _conv_pre(x_ref, w, ci, nchunk)
                sg = _sigmoid(pre)
                dpre = dy_ref[pl.ds(r0, CONV_ROWS), :] * sg * (1.0 + pre * (1.0 - sg))
                dpre_ref[pl.ds(r0, CONV_ROWS), :] = dpre
                for sh in range(CONV_W):
                    dw = dw + jnp.where(rows8 == CONV_W - 1 - sh, _colsum(dpre * shifted[sh]), 0.0)
                return dw

            dw_ref[...] = lax.fori_loop(0, nchunk, step1, jnp.zeros((8, CONV_COLS), F32))

            def step2(ci, carry):
                r0 = pl.multiple_of(ci * CONV_ROWS, CONV_ROWS)
                cur = dpre_ref[pl.ds(r0, CONV_ROWS), :]
                nxt = dpre_ref[pl.ds(pl.multiple_of(jnp.minimum(r0 + CONV_ROWS, s - 8), 8), 8), :]
                nxt = jnp.where(ci < nchunk - 1, nxt, 0.0)
                dx = w[CONV_W - 1:CONV_W, :] * cur
                for sh in range(1, CONV_W):
                    dx = dx + w[CONV_W - 1 - sh:CONV_W - sh, :] * _shift_up(cur, nxt, sh)
                dx_ref[pl.ds(r0, CONV_ROWS), :] = dx
                return carry

            lax.fori_loop(0, nchunk, step2, 0)

        outs.append(pl.pallas_call(
            kern, name=f"conv_bwd{part}", grid=(per,),
            in_specs=[pl.BlockSpec((s, CONV_COLS), lambda j, part=part: (0, CONV_BLOCK0 + part * per + j)),
                      pl.BlockSpec((8, CONV_COLS), lambda j, part=part: (0, part * per + j)),
                      pl.BlockSpec((s, CONV_COLS), lambda j: (0, j))],
            out_specs=[pl.BlockSpec((s, CONV_COLS), lambda j: (0, j)),
                       pl.BlockSpec((8, CONV_COLS), lambda j: (0, j))],
            out_shape=[jax.ShapeDtypeStruct((s, per * CONV_COLS), F32),
                       jax.ShapeDtypeStruct((8, per * CONV_COLS), F32)],
            scratch_shapes=[pltpu.VMEM((s, CONV_COLS), F32)],
            compiler_params=_params(("parallel",)),
        )(proj, conv_w8, dy))
    dx = jnp.concatenate([o[0] for o in outs], axis=1)
    dw = jnp.concatenate([o[1] for o in outs], axis=1)
    return dx, dw


FOXF_ROWS = 512
SMALL_BLOCK128 = 3584 // 128


def _log_sigmoid(x):
    return jnp.minimum(x, 0.0) - jnp.log(1.0 + jnp.exp(-jnp.abs(x)))


def fox_f_fwd(proj, bias_row):
    s = proj.shape[0]
    n = s // FOXF_ROWS

    def kern(x_ref, b_ref, f_ref, carry):
        @pl.when(pl.program_id(0) == 0)
        def _():
            carry[...] = jnp.zeros_like(carry)

        heads = _iota2((FOXF_ROWS, 128), 1) < FOX_HEADS
        lf = jnp.where(heads, _log_sigmoid(x_ref[...] + b_ref[...]), 0.0)
        ltri = (_iota2((FOXF_ROWS, FOXF_ROWS), 0) >= _iota2((FOXF_ROWS, FOXF_ROWS), 1)).astype(F32)
        c = hdot(ltri, lf) + carry[...]
        f_ref[...] = c
        carry[...] = c[FOXF_ROWS - 1:FOXF_ROWS, :]

    return pl.pallas_call(
        kern, name="fox_f_fwd", grid=(n,),
        in_specs=[pl.BlockSpec((FOXF_ROWS, 128), lambda i: (i, SMALL_BLOCK128)),
                  pl.BlockSpec((1, 128), lambda i: (0, 0))],
        out_specs=pl.BlockSpec((FOXF_ROWS, 128), lambda i: (i, 0)),
        out_shape=jax.ShapeDtypeStruct((s, 128), F32),
        scratch_shapes=[pltpu.VMEM((1, 128), F32)],
        compiler_params=_params(("arbitrary",)),
    )(proj, bias_row)


def fox_f_bwd(proj, bias_row, d_f):
    s = proj.shape[0]
    n = s // FOXF_ROWS

    def kern(x_ref, b_ref, df_ref, dx_ref, db_ref, carry):
        @pl.when(pl.program_id(0) == 0)
        def _():
            carry[...] = jnp.zeros_like(carry)
            db_ref[...] = jnp.zeros_like(db_ref)

        heads = _iota2((FOXF_ROWS, 128), 1) < FOX_HEADS
        utri = (_iota2((FOXF_ROWS, FOXF_ROWS), 0) <= _iota2((FOXF_ROWS, FOXF_ROWS), 1)).astype(F32)
        rc = hdot(utri, df_ref[...]) + carry[...]
        carry[...] = rc[0:1, :]
        dx = jnp.where(heads, rc * _sigmoid(-(x_ref[...] + b_ref[...])), 0.0)
        dx_ref[...] = dx
        db_ref[...] += _colsum(dx)

    return pl.pallas_call(
        kern, name="fox_f_bwd", grid=(n,),
        in_specs=[pl.BlockSpec((FOXF_ROWS, 128), lambda i: (n - 1 - i, SMALL_BLOCK128)),
                  pl.BlockSpec((1, 128), lambda i: (0, 0)),
                  pl.BlockSpec((FOXF_ROWS, 128), lambda i: (n - 1 - i, 0))],
        out_specs=[pl.BlockSpec((FOXF_ROWS, 128), lambda i: (n - 1 - i, 0)),
                   pl.BlockSpec((1, 128), lambda i: (0, 0))],
        out_shape=[jax.ShapeDtypeStruct((s, 128), F32), jax.ShapeDtypeStruct((1, 128), F32)],
        scratch_shapes=[pltpu.VMEM((1, 128), F32)],
        compiler_params=_params(("arbitrary",)),
    )(proj, bias_row, d_f)


FOX_T = 512
FOX_SCALE = FOX_DH ** -0.5
FOX_PAIRS = FOX_HEADS // 2
NEG = -1e30
_NT = (((1,), (1,)), ((), ()))


def _split3(x):
    def bf(v):
        return lax.reduce_precision(v, exponent_bits=8, mantissa_bits=7)

    hi = bf(x)
    mid = bf(x - hi)
    lo = bf(x - hi - mid)
    return jnp.stack([hi, mid, lo], axis=-1)


def _fox_extras(s, first, second):
    def part(v):
        if v is None:
            return jnp.zeros((s, FOX_HEADS, 3), F32)
        if isinstance(v, float):
            return jnp.full((s, FOX_HEADS, 3), v, F32)
        pairs = v.reshape(s, FOX_PAIRS, 2)
        return _split3(jnp.stack([pairs[:, :, 1], pairs[:, :, 0]], axis=-1).reshape(s, FOX_HEADS))

    cols = jnp.concatenate([part(first), part(second)], axis=-1)
    cols = _pad_to(cols, (s, FOX_HEADS, FOX_DH)).reshape(s, FOX_PAIRS, 2 * FOX_DH)
    return cols.transpose(1, 0, 2).astype(BF)


def _head_masks(rows):
    lane = _iota2((rows, 2 * FOX_DH), 1)
    return lane < FOX_DH, lane >= FOX_DH


def _extra_lane(e, slot):
    return (FOX_DH if e == 0 else 0) + slot


def fox_fwd(qkv, xq, xk, xv):
    s = qkv.shape[0]
    t = min(FOX_T, s)
    n = s // t

    def kern(q_ref, k_ref, v_ref, xq_ref, xk_ref, xv_ref, o_ref, lse_ref):
        i = pl.program_id(1)
        masks = _head_masks(t)
        q_pair, x_pair = q_ref[...] * FOX_SCALE, xq_ref[...]
        q_ops = [jnp.where(mk, q_pair, x_pair) for mk in masks]

        def step(j, carry, masked):
            sl = pl.ds(pl.multiple_of(j * t, t), t)
            k_pair, xk_pair, v_pair, xv_pair = k_ref[sl, :], xk_ref[sl, :], v_ref[sl, :], xv_ref[sl, :]
            k_ops = [jnp.where(mk, k_pair, xk_pair) for mk in masks]
            v_ops = [jnp.where(mk, v_pair, xv_pair) for mk in masks]
            sc = [lax.dot_general(q_ops[e], k_ops[e], _NT, preferred_element_type=F32) for e in range(2)]
            if masked:
                keep = _iota2((t, t), 0) >= _iota2((t, t), 1)
                sc = [jnp.where(keep, x, NEG) for x in sc]
            m_new = [jnp.maximum(carry[e][0], jnp.max(sc[e], axis=1, keepdims=True)) for e in range(2)]
            p = [jnp.exp(sc[e] - m_new[e]).astype(BF) for e in range(2)]
            pv = [jnp.dot(p[e], v_ops[e], preferred_element_type=F32) for e in range(2)]
            return tuple((m_new[e], jnp.exp(carry[e][0] - m_new[e]) * carry[e][1] + pv[e]) for e in range(2))

        init = tuple((jnp.full((t, 1), NEG, F32), jnp.zeros((t, 2 * FOX_DH), F32)) for _ in range(2))
        carry = lax.fori_loop(0, i, lambda j, c: step(j, c, False), init)
        carry = step(i, carry, True)
        lane = _iota2((t, 2 * FOX_DH), 1)
        outs, lses = [], []
        for e in range(2):
            m, acc = carry[e]
            l = jnp.sum(jnp.where(lane == _extra_lane(e, 0), acc, 0.0), axis=1, keepdims=True)
            outs.append(acc / l)
            lses.append(m + jnp.log(l))
        o_ref[...] = jnp.where(masks[0], outs[0], outs[1])
        head0 = 2 * pl.program_id(0)
        lse_ref[...] = jnp.where(lane == head0, lses[0], jnp.where(lane == head0 + 1, lses[1], 0.0))

    pr = FOX_PAIRS
    return pl.pallas_call(
        kern, name="fox_fwd", grid=(pr, n),
        in_specs=[pl.BlockSpec((t, 128), lambda p, i: (i, p)),
                  pl.BlockSpec((s, 128), lambda p, i: (0, pr + p)),
                  pl.BlockSpec((s, 128), lambda p, i: (0, 2 * pr + p)),
                  pl.BlockSpec((None, t, 128), lambda p, i: (p, i, 0)),
                  pl.BlockSpec((None, s, 128), lambda p, i: (p, 0, 0)),
                  pl.BlockSpec((None, s, 128), lambda p, i: (p, 0, 0))],
        out_specs=[pl.BlockSpec((t, 128), lambda p, i: (i, p)),
                   pl.BlockSpec((None, t, 128), lambda p, i: (p, i, 0))],
        out_shape=[jax.ShapeDtypeStruct((s, FOX_HEADS * FOX_DH), F32), jax.ShapeDtypeStruct((pr, s, 128), F32)],
        compiler_params=_params(("parallel", "parallel")),
    )(qkv, qkv, qkv, xq, xk, xv)


def fox_bwd(qkv, d_o, xk, xv, xqb, xdo):
    s = qkv.shape[0]
    t = min(FOX_T, s)
    n = s // t
    w = 2 * FOX_DH

    def both(blocks, slot):
        lane = _iota2(blocks[0].shape, 1)
        head0 = 2 * pl.program_id(0)
        own = jnp.where(lane < FOX_DH, blocks[0], blocks[1])
        sums = [jnp.sum(jnp.where(lane == _extra_lane(e, slot), blocks[e], 0.0), axis=1, keepdims=True)
                for e in range(2)]
        return own, jnp.where(lane == head0, sums[0], jnp.where(lane == head0 + 1, sums[1], 0.0))

    def kern(k_ref, v_ref, xk_ref, xv_ref, q_ref, do_ref, xq_ref, xd_ref,
             dq_ref, dk_ref, dv_ref, sq_ref, sk_ref, dq_acc):
        j = pl.program_id(1)

        @pl.when(j == 0)
        def _():
            dq_acc[...] = jnp.zeros_like(dq_acc)

        masks = _head_masks(t)
        k_ops = [jnp.where(mk, k_ref[...], xk_ref[...]) for mk in masks]
        v_ops = [jnp.where(mk, v_ref[...], xv_ref[...]) for mk in masks]
        k_t = [x.T for x in k_ops]

        def step(i, carry, masked):
            dk, dv = carry
            sl = pl.ds(pl.multiple_of(i * t, t), t)
            q_pair, xq_pair, do_pair, xd_pair = q_ref[sl, :] * FOX_SCALE, xq_ref[sl, :], do_ref[sl, :], xd_ref[sl, :]
            q_ops = [jnp.where(mk, q_pair, xq_pair) for mk in masks]
            do_ops = [jnp.where(mk, do_pair, xd_pair) for mk in masks]
            q_t = [x.T for x in q_ops]
            do_t = [jnp.where(mk, do_pair, 0).astype(BF).T for mk in masks]
            st = [lax.dot_general(k_ops[e], q_ops[e], _NT, preferred_element_type=F32) for e in range(2)]
            dp = [lax.dot_general(v_ops[e], do_ops[e], _NT, preferred_element_type=F32) for e in range(2)]
            if masked:
                keep = _iota2((t, t), 0) <= _iota2((t, t), 1)
                st = [jnp.where(keep, x, NEG) for x in st]
            pt = [jnp.exp(x) for x in st]
            dsb = [(pt[e] * dp[e]).astype(BF) for e in range(2)]
            dv = dv + sum(lax.dot_general(do_t[e], pt[e].astype(BF), _NT, preferred_element_type=F32)
                          for e in range(2))
            dk = tuple(dk[e] + lax.dot_general(q_t[e], dsb[e], _NT, preferred_element_type=F32) for e in range(2))
            for e in range(2):
                dq_acc[i, e * w:(e + 1) * w, :] += jnp.dot(k_t[e], dsb[e], preferred_element_type=F32)
            return dk, dv

        init = ((jnp.zeros((w, t), F32), jnp.zeros((w, t), F32)), jnp.zeros((w, t), F32))
        carry = step(j, init, True)
        dk, dv = lax.fori_loop(j + 1, n, lambda i, c: step(i, c, False), carry)
        dk_ref[...], sk_ref[...] = both([x.T for x in dk], 3)
        dv_ref[...] = dv.T

        @pl.when(j == n - 1)
        def _():
            def out(r, carry):
                sl = pl.ds(pl.multiple_of(r * t, t), t)
                own, sums = both([dq_acc[r, e * w:(e + 1) * w, :].T for e in range(2)], 0)
                dq_ref[sl, :] = own * FOX_SCALE
                sq_ref[sl, :] = sums
                return carry

            lax.fori_loop(0, n, out, 0)

    pr = FOX_PAIRS
    flat = jax.ShapeDtypeStruct((s, FOX_HEADS * FOX_DH), F32)
    tile = pl.BlockSpec((t, 128), lambda p, j: (j, p))
    whole = pl.BlockSpec((s, 128), lambda p, j: (0, p))
    return pl.pallas_call(
        kern, name="fox_bwd", grid=(pr, n),
        in_specs=[pl.BlockSpec((t, 128), lambda p, j: (j, pr + p)),
                  pl.BlockSpec((t, 128), lambda p, j: (j, 2 * pr + p)),
                  pl.BlockSpec((None, t, 128), lambda p, j: (p, j, 0)),
                  pl.BlockSpec((None, t, 128), lambda p, j: (p, j, 0)),
                  whole, whole,
                  pl.BlockSpec((None, s, 128), lambda p, j: (p, 0, 0)),
                  pl.BlockSpec((None, s, 128), lambda p, j: (p, 0, 0))],
        out_specs=[whole, tile, tile, whole, tile],
        out_shape=[flat] * 5,
        scratch_shapes=[pltpu.VMEM((n, 2 * w, t), F32)],
        compiler_params=_params(("parallel", "arbitrary")),
    )(qkv, qkv, xk, xv, qkv, d_o, xqb, xdo)


def _xattn_head(q, k, v):
    sc = bdot(q, k, 1, 1) * (MEM_DH ** -0.5)
    e = jnp.exp(sc - lax.stop_gradient(jnp.max(sc, axis=-1, keepdims=True)))
    p = e / jnp.sum(e, axis=-1, keepdims=True)
    return bdot(p, v, 1, 0)


def xattn_fwd(q, kv):
    s = q.shape[0]
    hh = MEM_HEADS

    def body(*vals):
        qs, ks, vs = vals[:hh], vals[hh:2 * hh], vals[2 * hh:]
        return jnp.concatenate([_xattn_head(qs[a], ks[a], vs[a]) for a in range(hh)], axis=1)

    return rowcall(body, [(q, MEM_DH, a) for a in range(hh)],
                   [(kv, MEM_DH, a) for a in range(2 * hh)],
                   [(hh * MEM_DH, BF)], rows=512, total=s, name="xattn_fwd")[0]


def xattn_bwd(q, kv, d_o):
    s = q.shape[0]
    hh = MEM_HEADS

    def body(*vals):
        qs, dos = vals[:hh], vals[hh:2 * hh]
        ks, vs = vals[2 * hh:3 * hh], vals[3 * hh:]
        dqs, dks, dvs = [], [], []
        for a in range(hh):
            _, vjp = jax.vjp(_xattn_head, qs[a], ks[a], vs[a])
            dq, dk, dv = vjp(dos[a])
            dqs.append(dq)
            dks.append(dk)
            dvs.append(dv)
        return jnp.concatenate(dqs, axis=1), jnp.concatenate(dks + dvs, axis=1)

    return rowcall(body, [(q, MEM_DH, a) for a in range(hh)] + [(d_o, MEM_DH, a) for a in range(hh)],
                   [(kv, MEM_DH, a) for a in range(2 * hh)],
                   [(hh * MEM_DH, BF)], [kv.shape], rows=512, total=s, name="xattn_bwd")


def _slab(ref, axis, start, size):
    if axis is None:
        return ref
    if axis == "lead":
        return ref.at[start]
    idx = pl.ds(pl.multiple_of(start, 128 if axis == 1 else 16), size)
    return ref.at[idx] if axis == 0 else ref.at[:, idx]


def exchange(inputs, outputs, transfers, name):
    ni, no, nt = len(inputs), len(outputs), len(transfers)
    npeer = N_DEV - 1

    def body(*refs):
        ins, outs = refs[:ni], refs[ni:ni + no]
        send, recv, loc = refs[ni + no:]
        x, y, c = lax.axis_index("x"), lax.axis_index("y"), lax.axis_index("c")
        me = 4 * x + 2 * y + c

        def peer(p):
            px = 1 - x if p & 4 else x
            py = 1 - y if p & 2 else y
            pc = 1 - c if p & 1 else c
            return (px, py, pc), 4 * px + 2 * py + pc

        def view(ref, spec, who):
            axis, off, stride, size = spec
            return _slab(ref, axis, off + who * stride, size)

        local, remote = [], []
        for w, (ii, src, oi, dst) in enumerate(transfers):
            cp = pltpu.make_async_copy(view(ins[ii], src, me), view(outs[oi], dst, me), loc.at[w])
            cp.start()
            local.append(cp)
        for p in range(1, N_DEV):
            dev, idx = peer(p)
            for w, (ii, src, oi, dst) in enumerate(transfers):
                k = w * npeer + p - 1
                out_cp = pltpu.make_async_remote_copy(
                    src_ref=view(ins[ii], src, idx), dst_ref=view(outs[oi], dst, me), send_sem=send.at[k],
                    recv_sem=recv.at[k], device_id=dev, device_id_type=MESH)
                out_cp.start()
                in_cp = pltpu.make_async_remote_copy(
                    src_ref=view(ins[ii], src, idx), dst_ref=view(outs[oi], dst, idx), send_sem=send.at[k],
                    recv_sem=recv.at[k], device_id=dev, device_id_type=MESH)
                remote.append((out_cp, in_cp))
        for out_cp, in_cp in remote:
            in_cp.wait_recv()
            out_cp.wait_send()
        for cp in local:
            cp.wait()

    hbm = pl.BlockSpec(memory_space=pl.ANY)
    return pl.pallas_call(
        body, name=name, in_specs=[hbm] * ni, out_specs=[hbm] * no, out_shape=list(outputs),
        scratch_shapes=[pltpu.SemaphoreType.DMA((nt * npeer,)), pltpu.SemaphoreType.DMA((nt * npeer,)),
                        pltpu.SemaphoreType.DMA((nt,))],
        compiler_params=pltpu.CompilerParams(has_side_effects=True),
    )(*inputs)


def _peer(p):
    x, y, c = lax.axis_index("x"), lax.axis_index("y"), lax.axis_index("c")
    px = 1 - x if p & 4 else x
    py = 1 - y if p & 2 else y
    pc = 1 - c if p & 1 else c
    return (px, py, pc), 4 * px + 2 * py + pc


def _view(ref, spec, who):
    axis, off, stride, size = spec
    return _slab(ref, axis, off + who * stride, size)


def place_own(inputs, outputs, transfers):
    me = 4 * lax.axis_index("x") + 2 * lax.axis_index("y") + lax.axis_index("c")
    lands = [lax.empty(o.shape, o.dtype) for o in outputs]
    for ii, src, oi, dst in transfers:
        axis, off, stride, size = src
        own = inputs[ii] if axis is None else lax.dynamic_slice_in_dim(inputs[ii], off + me * stride, size, axis)
        axis, off, stride, size = dst
        if axis == "lead":
            lands[oi] = lax.dynamic_update_slice_in_dim(lands[oi], own[None], me, 0)
        else:
            lands[oi] = lax.dynamic_update_slice_in_dim(lands[oi], own, off + me * stride, axis)
    return lands


_HBM = pl.BlockSpec(memory_space=pltpu.HBM)
_SEM = pl.BlockSpec(memory_space=pltpu.SEMAPHORE)
_EFFECT = pltpu.SideEffectType.DATAFLOW_SIDE_EFFECTING


def _remote_copies(ins, lands, transfers, send, recv):
    npeer = N_DEV - 1
    me = 4 * lax.axis_index("x") + 2 * lax.axis_index("y") + lax.axis_index("c")
    pairs = []
    for p in range(1, N_DEV):
        dev, idx = _peer(p)
        for w, (ii, src, oi, dst) in enumerate(transfers):
            k = w * npeer + p - 1
            common = dict(src_ref=_view(ins[ii], src, idx), send_sem=send.at[k], recv_sem=recv.at[k],
                          device_id=dev, device_id_type=MESH)
            pairs.append((pltpu.make_async_remote_copy(dst_ref=_view(lands[oi], dst, me), **common),
                          pltpu.make_async_remote_copy(dst_ref=_view(lands[oi], dst, idx), **common)))
    return pairs


def exchange_start(inputs, lands, transfers, after, name):
    ni, nl, nsem = len(inputs), len(lands), len(transfers) * (N_DEV - 1)

    def body(*refs):
        ins, lnd = refs[:ni], refs[ni:ni + nl]
        send, recv = refs[ni + nl + 1], refs[ni + nl + 2]
        token = refs[-1]
        for out_cp, _ in _remote_copies(ins, lnd, transfers, send, recv):
            out_cp.start()
        token[...] = jnp.zeros_like(token)

    args = [pltpu.with_memory_space_constraint(a, pltpu.HBM) for a in list(inputs) + list(lands)]
    res = pl.pallas_call(
        body, name=name,
        out_shape=(pltpu.SemaphoreType.DMA((nsem,)), pltpu.SemaphoreType.DMA((nsem,)),
                   *[pltpu.HBM(a.shape, a.dtype) for a in args], jax.ShapeDtypeStruct((8, 128), F32)),
        in_specs=[_HBM] * (ni + nl) + [pl.BlockSpec(memory_space=pl.ANY)],
        out_specs=(_SEM, _SEM, *[_HBM] * (ni + nl), pl.BlockSpec(memory_space=pltpu.VMEM)),
        input_output_aliases={k: k + 2 for k in range(ni + nl)},
        compiler_params=pltpu.CompilerParams(has_side_effects=_EFFECT),
    )(*args, after)
    return res[0], res[1], list(res[2:2 + ni]), list(res[2 + ni:2 + ni + nl]), res[-1]


def exchange_wait(send, recv, inputs, lands, after, transfers, name):
    ni, nl = len(inputs), len(lands)

    def body(*refs):
        ins, lnd = refs[:ni], refs[ni:ni + nl]
        send_r, recv_r = refs[ni + nl], refs[ni + nl + 1]
        for out_cp, in_cp in _remote_copies(ins, lnd, transfers, send_r, recv_r):
            out_cp.wait_send()
            in_cp.wait_recv()

    res = pl.pallas_call(
        body, name=name,
        out_shape=tuple(pltpu.HBM(a.shape, a.dtype) for a in list(inputs) + list(lands)),
        in_specs=[_HBM] * (ni + nl) + [_SEM, _SEM, pl.BlockSpec(memory_space=pl.ANY)],
        out_specs=tuple([_HBM] * (ni + nl)),
        input_output_aliases={k: k for k in range(ni + nl)},
        compiler_params=pltpu.CompilerParams(has_side_effects=_EFFECT),
    )(*inputs, *lands, send, recv, after)
    return list(res[ni:])


def adamw(w, m, v, contribs, name):
    r, c = w.shape
    nc = len(contribs)
    rows = next((r // d for d in (4, 2) if r % d == 0 and (r // d) % 16 == 0), r)
    c1, c2 = 1.0 - ADAM_B1 ** ADAM_STEP, 1.0 - ADAM_B2 ** ADAM_STEP

    def body(wv, mv, vv, *gs):
        g = gs[0].astype(F32)
        for extra in gs[1:]:
            g = g + extra.astype(F32)
        g = g[:, :c]
        m_new = ADAM_B1 * mv + (1.0 - ADAM_B1) * g
        v_new = ADAM_B2 * vv + (1.0 - ADAM_B2) * (g * g)
        delta = -ADAM_LR * ((m_new / c1) / (jnp.sqrt(v_new / c2) + ADAM_EPS) + ADAM_WD * wv)
        return g, delta, m_new, v_new

    assert nc >= 1
    return rowcall(body, [w, m, v] + list(contribs), [], [(c, F32)] * 4, rows=rows, total=r, name=name)


WEIGHTS = ['ffn1_pre_norm', 'ffn1_w_gate', 'ffn1_w_up', 'ffn1_w_down', 'ffn1_post_norm', 'mix_pre_norm', 'w_in',
           'fox_f_bias', 'gdn_conv_w', 'gdn_a_log', 'gdn_dt_bias', 'gdn_out_norm', 'w_out', 'mix_post_norm',
           'mem_pre_norm', 'mem_kv_norm', 'mem_w_q', 'mem_w_kv', 'mem_w_o', 'mem_post_norm', 'ffn2_pre_norm',
           'ffn2_w_gate', 'ffn2_w_up', 'ffn2_w_down', 'ffn2_post_norm']
GAINS = ['ffn1_pre_norm', 'ffn1_post_norm', 'mix_pre_norm', 'mix_post_norm', 'mem_pre_norm', 'mem_kv_norm',
         'mem_post_norm', 'ffn2_pre_norm', 'ffn2_post_norm']
BIG = ['ffn1_w_gate', 'ffn1_w_up', 'ffn1_w_down', 'w_in', 'w_out', 'mem_w_q', 'mem_w_kv', 'mem_w_o',
       'ffn2_w_gate', 'ffn2_w_up', 'ffn2_w_down']
PACK_ROWS = 24
ROW_MISC = len(GAINS)
ROW_CONV = ROW_MISC + 1
COL_FBIAS, COL_ALOG, COL_DTB, COL_ONORM, COL_LOSS = 0, 8, 12, 128, 256
CONV_CH = 3 * GDN_HEADS * GDN_DH


def _pad_to(a, shape):
    return jnp.pad(a, [(0, t - s) for s, t in zip(a.shape, shape)])


def _pack(get, conv=None, loss=None):
    rows = [get(nm) for nm in GAINS]
    misc = jnp.concatenate([get('fox_f_bias'), get('gdn_a_log'), get('gdn_dt_bias'),
                            jnp.zeros((1, COL_ONORM - COL_DTB - 4), F32), get('gdn_out_norm'),
                            jnp.zeros((1, 1), F32) if loss is None else loss.reshape(1, 1)], axis=1)
    rows.append(_pad_to(misc, (1, D_MODEL)))
    rows.append(jnp.zeros((6, D_MODEL), F32) if conv is None else conv.reshape(6, D_MODEL))
    return _pad_to(jnp.concatenate(rows, axis=0), (PACK_ROWS, D_MODEL))


def _unpack(p):
    out = {nm: p[i:i + 1] for i, nm in enumerate(GAINS)}
    misc = p[ROW_MISC:ROW_MISC + 1]
    out['fox_f_bias'] = misc[:, COL_FBIAS:COL_FBIAS + FOX_HEADS]
    out['gdn_a_log'] = misc[:, COL_ALOG:COL_ALOG + GDN_HEADS]
    out['gdn_dt_bias'] = misc[:, COL_DTB:COL_DTB + GDN_HEADS]
    out['gdn_out_norm'] = misc[:, COL_ONORM:COL_ONORM + GDN_DH]
    return out


def _ffn_fwd(h, pre, wgu, wd, tag):
    s = h.shape[0]
    u, = rowcall(_rms, [h], [pre], [(D_MODEL, BF)], rows=512, total=s, name=tag + "_pre")
    if callable(wgu):
        wgu = wgu(u)
    gate, up, act = mm_swiglu(u, wgu, name=tag + "_gate_up")
    if callable(wd):
        wd = wd(act)
    f = mm(act, wd, name=tag + "_down")
    return u, gate, up, act, f


def _half_rms(a, g):
    return 0.5 * _rms(a, g)


def _ffn_bwd(dh_out, h, pre, post, wgu, wd, saved, tag, on_dwd=None, on_dwgu=None):
    u, gate, up, act, f = saved
    s = h.shape[0]

    def b_post(dh, fv, pg):
        return jax.vjp(_half_rms, fv, pg)[1](dh)

    df, dpost = rowcall(b_post, [dh_out, f], [post], [(D_MODEL, BF)], [(1, D_MODEL)], rows=512, total=s,
                        name=tag + "_bwd_post")
    dwd = mm(act, df, ta=True, out_dtype=BF, name=tag + "_bwd_dwd")
    dgate, dup = mm_dswiglu(df, wd, gate, up, name=tag + "_bwd_dact", token=on_dwd(dwd) if on_dwd else None)
    dwg = mm(u, dgate, ta=True, out_dtype=BF, name=tag + "_bwd_dwg")
    dwu = mm(u, dup, ta=True, out_dtype=BF, name=tag + "_bwd_dwu")
    du = mm_pair(dgate, dup, wgu, name=tag + "_bwd_du", token=on_dwgu(dwg, dwu) if on_dwgu else None)

    def b_pre(dh, duv, hv, pg):
        dx, dpre = jax.vjp(_rms, hv, pg)[1](duv)
        return dh + dx, dpre

    dh, dpre = rowcall(b_pre, [dh_out, du, h], [pre], [(D_MODEL, F32)], [(1, D_MODEL)], rows=512, total=s,
                       name=tag + "_bwd_pre")
    return dh, dwg, dwu, dwd, dpre, dpost


def _residual_rms(h, a, g):
    return h + _rms(a, g)


def _bwd_residual(dh, a, g):
    return jax.vjp(_rms, a, g)[1](dh)


def _step(a):
    x, mem = a['x'][0], a['mem'][0]
    s = x.shape[0]
    me = 4 * lax.axis_index("x") + 2 * lax.axis_index("y") + lax.axis_index("c")
    w2 = {nm: a[nm][0] for nm in WEIGHTS}
    m2 = {nm: a['m_' + nm][0] for nm in WEIGHTS}
    v2 = {nm: a['v_' + nm][0] for nm in WEIGHTS}
    small = {nm: w2[nm][None] for nm in WEIGHTS if nm not in BIG and nm != 'gdn_conv_w'}

    def ff_cols(w):
        return _pad_to(w, (D_MODEL, FF_SHARD_PAD)).astype(BF)

    def ff_rows(w):
        return _pad_to(w, (FF_SHARD_PAD, D_MODEL)).astype(BF)

    whole = (None, 0, 0, 0)
    conv_pad = 256
    g_in = [ff_cols(w2['ffn1_w_gate']), ff_cols(w2['ffn1_w_up']), ff_rows(w2['ffn1_w_down']),
            ff_cols(w2['ffn2_w_gate']), ff_cols(w2['ffn2_w_up']), ff_rows(w2['ffn2_w_down']),
            _pad_to(w2['w_in'], (D_MODEL, IN_SHARD_PAD)).astype(BF), w2['w_out'].astype(BF),
            w2['mem_w_q'].astype(BF), w2['mem_w_kv'].astype(BF), w2['mem_w_o'].astype(BF),
            _pad_to(w2['gdn_conv_w'], (8, conv_pad))]
    g_out = [jax.ShapeDtypeStruct((D_MODEL, 2 * D_FF_PAD), BF), jax.ShapeDtypeStruct((D_FF_PAD, D_MODEL), BF),
             jax.ShapeDtypeStruct((D_MODEL, 2 * D_FF_PAD), BF), jax.ShapeDtypeStruct((D_FF_PAD, D_MODEL), BF),
             jax.ShapeDtypeStruct((D_MODEL, N_DEV * IN_SHARD_PAD), BF), jax.ShapeDtypeStruct((D_MODEL, D_MODEL), BF),
             jax.ShapeDtypeStruct((D_MODEL, D_MODEL), BF), jax.ShapeDtypeStruct((D_MODEL, 2 * D_MODEL), BF),
             jax.ShapeDtypeStruct((D_MODEL, D_MODEL), BF), jax.ShapeDtypeStruct((8, N_DEV * conv_pad), F32)]
    sp_, dm = FF_SHARD_PAD, D_MODEL // N_DEV
    g_tr = [(0, whole, 0, (1, 0, sp_, sp_)), (1, whole, 0, (1, D_FF_PAD, sp_, sp_)), (2, whole, 1, (0, 0, sp_, sp_)),
            (3, whole, 2, (1, 0, sp_, sp_)), (4, whole, 2, (1, D_FF_PAD, sp_, sp_)), (5, whole, 3, (0, 0, sp_, sp_)),
            (6, whole, 4, (1, 0, IN_SHARD_PAD, IN_SHARD_PAD)), (7, whole, 5, (0, 0, dm, dm)),
            (8, whole, 6, (0, 0, dm, dm)), (9, whole, 7, (1, 0, 2 * dm, 2 * dm)), (10, whole, 8, (0, 0, dm, dm)),
            (11, whole, 9, (1, 0, conv_pad, conv_pad))]
    def pick(idx):
        ins = sorted({g_tr[k][0] for k in idx})
        outs = sorted({g_tr[k][2] for k in idx})
        tr = [(ins.index(g_tr[k][0]), g_tr[k][1], outs.index(g_tr[k][2]), g_tr[k][3]) for k in idx]
        return [g_in[i] for i in ins], [g_out[o] for o in outs], tr

    stages, after = [], g_in[0]
    for nm, idx in (("gate_up", [0, 1]), ("down", [2]), ("mix", [6, 7, 11]), ("late", [8, 9, 10, 3, 4, 5])):
        st_in, st_out, st_tr = pick(idx)
        st = exchange_start(st_in, place_own(st_in, st_out, st_tr), st_tr, after, "gather_%s_start" % nm)
        stages.append((st, st_tr, "gather_%s_wait" % nm))
        after = st[4]
    g_token = after

    def gather_wait(k, after_):
        (send_, recv_, src_, land_, _), tr_, nm_ = stages[k]
        return exchange_wait(send_, recv_, src_, land_, after_, tr_, nm_)

    bias_row = _pad_to(small['fox_f_bias'], (1, 128))
    gate_prm = _pad_to(jnp.concatenate([_pad_to(small['gdn_a_log'], (1, 128 - SMALL_A)),
                                        _pad_to(small['gdn_dt_bias'], (1, 128 - SMALL_A))], axis=0),
                       (8, 128 - SMALL_A))
    gate_prm = jnp.pad(gate_prm, ((0, 0), (SMALL_A, 0)))
    onorm = small['gdn_out_norm']

    late = {}

    def wgu1_when(u):
        late['wgu1'], = gather_wait(0, u)
        return late['wgu1']

    def wd1_when(act):
        late['wd1'], = gather_wait(1, act)
        return late['wd1']

    sv1 = _ffn_fwd(x, small['ffn1_pre_norm'] + g_token[0, 0], wgu1_when, wd1_when, "ffn1")
    wgu1, wd1 = late['wgu1'], late['wd1']
    h1, = rowcall(lambda h, f, g: h + _half_rms(f, g), [x, sv1[4]], [small['ffn1_post_norm']], [(D_MODEL, F32)],
                  rows=512, total=s, name="ffn1_out")
    w_in_g, w_out, conv_g = gather_wait(2, h1)
    w_in = jnp.concatenate([w_in_g[:, j * IN_SHARD_PAD:j * IN_SHARD_PAD + IN_SHARD] for j in range(N_DEV)],
                           axis=1)
    sp = [0, 512, 1024, 1536, 1544, 2056, 2568, 3080, 3592, 3596, 3600]
    fq, fk, fv, ff, gq, gk, gv, gz, gb, ga = [w_in[:, sp[i]:sp[i + 1]] for i in range(10)]
    w_proj = jnp.concatenate([fq, fk, fv, gq, gk, gv, gz, ff, gb, ga,
                              jnp.zeros((D_MODEL, PROJ_W - 3584 - 16), BF)], axis=1)
    conv_w8 = conv_g.reshape(8, N_DEV, conv_pad)[:, :, :CONV_CH // N_DEV].reshape(8, CONV_CH)


    u2, = rowcall(_rms, [h1], [small['mix_pre_norm']], [(D_MODEL, BF)], rows=512, total=s, name="mix_pre")
    proj = mm(u2, w_proj, name="mix_proj")
    f_cum = fox_f_fwd(proj, bias_row)
    f_heads = f_cum[:, :FOX_HEADS]
    qkv_bf = proj[:, :3 * FOX_HEADS * FOX_DH].astype(BF)
    xk, xv = _fox_extras(s, 1.0, -f_heads), _fox_extras(s, 1.0, None)
    fox_flat, lse = fox_fwd(qkv_bf, _fox_extras(s, f_heads, 1.0), xk, xv)
    lse_heads = jnp.sum(lse, axis=0)[:, :FOX_HEADS]
    cqkv = conv_fwd(proj, conv_w8)
    g_l, b_l = rowcall(_gdn_gates, [(proj, 128, SMALL_BLOCK128)], [gate_prm], [(512, F32), (512, F32)],
                       rows=512, total=s, name="gdn_gates")
    gbb = jnp.concatenate([g_l, b_l], axis=1)
    gdn_o, states = gdn_fwd(cqkv, proj, gbb, onorm)
    mixed = jnp.concatenate([fox_flat, gdn_o], axis=1).astype(BF)
    mo = mm(mixed, w_out, name="mix_out")
    h2, = rowcall(_residual_rms, [h1, mo], [small['mix_post_norm']], [(D_MODEL, F32)], rows=512, total=s,
                  name="mix_res")

    hq, = rowcall(_rms, [h2], [small['mem_pre_norm']], [(D_MODEL, BF)], rows=512, total=s, name="mem_pre")
    mn, = rowcall(_rms, [mem], [small['mem_kv_norm']], [(D_MODEL, BF)], rows=256, total=mem.shape[0], name="mem_kvn")
    wgu2, wd2, w_q, w_kv, w_o = gather_wait(3, h2)
    q_mem = mm(hq, w_q, name="mem_q")
    kv_mem = mm(mn, w_kv, name="mem_kv")
    o_mem = xattn_fwd(q_mem, kv_mem)
    c_mem = mm(o_mem, w_o, name="mem_o")
    h3, = rowcall(_residual_rms, [h2, c_mem], [small['mem_post_norm']], [(D_MODEL, F32)], rows=512, total=s,
                  name="mem_res")

    sv2 = _ffn_fwd(h3, small['ffn2_pre_norm'], wgu2, wd2, "ffn2")

    def b_loss(h, f, tgt, g):
        err = h + _half_rms(f, g) - tgt
        part = 0.5 * jnp.sum(jnp.mean(err * err, axis=-1, keepdims=True), axis=0, keepdims=True)
        return err * (1.0 / D_MODEL), jnp.broadcast_to(part, (1, 128))

    dy, loss_acc = rowcall(b_loss, [h3, sv2[4], a['loss_target'][0]], [small['ffn2_post_norm']], [(D_MODEL, F32)],
                           [(1, 128)], rows=512, total=s, name="loss")

    grads = {}
    dh3, dwg2, dwu2, dwd2, grads['ffn2_pre_norm'], grads['ffn2_post_norm'] = _ffn_bwd(
        dy, h3, small['ffn2_pre_norm'], small['ffn2_post_norm'], wgu2, wd2, sv2, "ffn2")

    lead = ("lead", 0, 1, 0)

    def land(r, c, dt=BF):
        return jax.ShapeDtypeStruct((N_DEV, r, c), dt)

    ffn_tr = [(0, (1, 0, sp_, sp_), 0, lead), (1, (1, 0, sp_, sp_), 1, lead), (2, (0, 0, sp_, FF_SHARD), 2, lead)]
    ffn_land = [land(D_MODEL, sp_), land(D_MODEL, sp_), land(FF_SHARD, D_MODEL)]
    a_in = [dwg2, dwu2, dwd2]
    a_send, a_recv, a_src, a_land, a_token = exchange_start(a_in, place_own(a_in, ffn_land, ffn_tr), ffn_tr, dh3,
                                                            "reduce_ffn2_start")

    dc, grads['mem_post_norm'] = rowcall(_bwd_residual, [dh3, c_mem], [small['mem_post_norm'] + a_token[0, 0]],
                                         [(D_MODEL, BF)],
                                         [(1, D_MODEL)], rows=512, total=s, name="mem_bwd_res")
    d_o = mm(dc, w_o, tb=True, name="mem_bwd_do")
    dw_o = mm(o_mem, dc, ta=True, out_dtype=BF, name="mem_bwd_dwo")
    dq_mem, dkv = xattn_bwd(q_mem, kv_mem, d_o)
    dhq = mm(dq_mem, w_q, tb=True, name="mem_bwd_dhq")
    dw_q = mm(hq, dq_mem, ta=True, out_dtype=BF, name="mem_bwd_dwq")
    dmn = mm(dkv, w_kv, tb=True, name="mem_bwd_dmn")
    dw_kv = mm(mn, dkv, ta=True, out_dtype=BF, name="mem_bwd_dwkv")
    _, grads['mem_kv_norm'] = rowcall(lambda d, mv, g: jax.vjp(_rms, mv, g)[1](d), [dmn, mem],
                                      [small['mem_kv_norm']], [(D_MODEL, F32)], [(1, D_MODEL)], rows=256,
                                      total=mem.shape[0], name="mem_bwd_kvn")

    def b_pre(dh, duv, hv, pg):
        dx, dpre = jax.vjp(_rms, hv, pg)[1](duv)
        return dh + dx, dpre

    dh2, grads['mem_pre_norm'] = rowcall(b_pre, [dh3, dhq, h2], [small['mem_pre_norm']], [(D_MODEL, F32)],
                                         [(1, D_MODEL)], rows=512, total=s, name="mem_bwd_pre")

    dmo, grads['mix_post_norm'] = rowcall(_bwd_residual, [dh2, mo], [small['mix_post_norm']], [(D_MODEL, BF)],
                                          [(1, D_MODEL)], rows=512, total=s, name="mix_bwd_res")
    d_mixed = mm(dmo, w_out, tb=True, name="mix_bwd_dmixed")
    dw_out = mm(mixed, dmo, ta=True, out_dtype=BF, name="mix_bwd_dwout")
    def b_delta(do, o):
        sel = (_iota2((512, 128), 0) // FOX_DH == _iota2((512, 128), 1)).astype(F32)
        return hdot(do * o, sel)

    delta, = rowcall(b_delta, [(d_mixed, 512, 0), fox_flat], [], [(128, F32)], rows=512, total=s, name="fox_delta")
    dfox_q, dfox_k, dvf, sum_q, sum_k = fox_bwd(qkv_bf, d_mixed[:, :512].astype(BF), xk, xv,
                                                _fox_extras(s, f_heads - lse_heads, 1.0),
                                                _fox_extras(s, -delta[:, :FOX_HEADS], None))
    d_f = jnp.sum((sum_q - sum_k).reshape(s, FOX_PAIRS, 2 * FOX_DH), axis=1)
    dsmall_f, dbias = fox_f_bwd(proj, bias_row, d_f)
    grads['fox_f_bias'] = dbias[:, :FOX_HEADS]
    dcqkv, dz, dgb, grads['gdn_out_norm'] = gdn_bwd(cqkv, proj, gbb, onorm, states, d_mixed)

    def b_gates(sm, dsf, dg, db, prm):
        dsm, dprm = jax.vjp(_gdn_gates, sm, prm)[1]((dg, db))
        return dsm + dsf, dprm

    dsmall, dprm = rowcall(b_gates, [(proj, 128, SMALL_BLOCK128), dsmall_f, (dgb, 512, 0), (dgb, 512, 1)], [gate_prm],
                           [(128, F32)],
                           [(8, 128)], rows=512, total=s, name="gdn_bwd_gates")
    grads['gdn_a_log'] = dprm[0:1, SMALL_A:SMALL_A + GDN_HEADS]
    grads['gdn_dt_bias'] = dprm[1:2, SMALL_A:SMALL_A + GDN_HEADS]
    dqkv_pre, dconv8 = conv_bwd(proj, conv_w8, dcqkv)
    dproj = jnp.concatenate([dfox_q, dfox_k, dvf, dqkv_pre, dz, dsmall,
                             jnp.zeros((s, PROJ_W - 3584 - 128), F32)], axis=1).astype(BF)
    du2 = mm(dproj, w_proj, tb=True, name="mix_bwd_du")
    dw_proj = mm(u2, dproj, ta=True, out_dtype=BF, name="mix_bwd_dwproj")
    dh1, grads['mix_pre_norm'] = rowcall(b_pre, [dh2, du2, h1], [small['mix_pre_norm']], [(D_MODEL, F32)],
                                         [(1, D_MODEL)], rows=512, total=s, name="mix_bwd_pre")

    dw_in = jnp.concatenate([dw_proj[:, :1536], dw_proj[:, 3584:3592], dw_proj[:, 1536:3584],
                             dw_proj[:, 3592:3600]], axis=1)
    gap = jnp.zeros((D_MODEL, IN_SHARD_PAD - IN_SHARD), BF)
    dw_in = jnp.concatenate([piece for j in range(N_DEV) for piece in (dw_in[:, j * IN_SHARD:(j + 1) * IN_SHARD], gap)],
                            axis=1)
    b_in = [dw_in, dw_out, dw_q, dw_kv, dw_o]
    b_tr = [(0, (1, 0, IN_SHARD_PAD, IN_SHARD_PAD), 0, lead), (1, (0, 0, dm, dm), 1, lead), (2, (0, 0, dm, dm), 2, lead),
            (3, (1, 0, 2 * dm, 2 * dm), 3, lead), (4, (0, 0, dm, dm), 4, lead)]
    b_shapes = [land(D_MODEL, IN_SHARD_PAD), land(dm, D_MODEL), land(dm, D_MODEL), land(D_MODEL, 2 * dm),
                land(dm, D_MODEL)]
    b_land = place_own(b_in, b_shapes, b_tr)
    b_send, b_recv, b_src, b_land, b_token = exchange_start(b_in, b_land, b_tr, dh1, "reduce_mix_start")

    def start_down_reduce(dwd):
        tr = ffn_tr[2:]
        tr = [(0, tr[0][1], 0, tr[0][3])]
        late['c_down'] = (exchange_start([dwd], place_own([dwd], ffn_land[2:], tr), tr, dwd, "reduce_ffn1_down_start"), tr)
        return late['c_down'][0][4]

    def start_gate_up_reduce(dwg, dwu):
        tr = ffn_tr[:2]
        late['c_gu'] = (exchange_start([dwg, dwu], place_own([dwg, dwu], ffn_land[:2], tr), tr, dwu,
                                       "reduce_ffn1_gu_start"), tr)
        return late['c_gu'][0][4]

    grad_x, _, _, _, grads['ffn1_pre_norm'], grads['ffn1_post_norm'] = _ffn_bwd(
        dh1, x, small['ffn1_pre_norm'], small['ffn1_post_norm'] + b_token[0, 0], wgu1, wd1, sv1, "ffn1",
        on_dwd=start_down_reduce, on_dwgu=start_gate_up_reduce)

    gpack = _pack(lambda nm: grads[nm], conv=dconv8[:CONV_W], loss=loss_acc[:, :1])
    gsum_parts, = exchange([gpack], [land(PACK_ROWS, D_MODEL, F32)], [(0, whole, 0, lead)], "reduce_small")
    a_got = exchange_wait(a_send, a_recv, a_src, a_land, gsum_parts, ffn_tr, "reduce_ffn2_wait")
    b_got = exchange_wait(b_send, b_recv, b_src, b_land, gsum_parts, b_tr, "reduce_mix_wait")
    recv = dict(zip(['ffn2_w_gate', 'ffn2_w_up', 'ffn2_w_down', 'w_in', 'w_out', 'mem_w_q', 'mem_w_kv', 'mem_w_o'],
                    a_got + b_got))

    out_g, out_d, out_m, out_v = {}, {}, {}, {}

    def update(nm):
        r = recv[nm]
        res = adamw(w2[nm], m2[nm], v2[nm], [(r, r.shape[2], 0, d) for d in range(N_DEV)], "adamw_" + nm)
        out_g[nm], out_d[nm], out_m[nm], out_v[nm] = res

    for nm in recv:
        update(nm)
    wp = _pack(lambda nm: small[nm])
    mp = _pack(lambda nm: m2[nm][None])
    vp = _pack(lambda nm: v2[nm][None])
    pg, pd, pm, pv = adamw(wp, mp, vp, [(gsum_parts, D_MODEL, 0, d) for d in range(N_DEV)], "adamw_small")
    for dst, p in ((out_g, pg), (out_d, pd), (out_m, pm), (out_v, pv)):
        dst.update({k: val[0] for k, val in _unpack(p).items()})
    loss = pg[ROW_MISC, COL_LOSS]
    conv_g = lax.dynamic_slice_in_dim(pg[ROW_CONV:ROW_CONV + 6].reshape(CONV_W, CONV_CH), me * (CONV_CH // N_DEV),
                                      CONV_CH // N_DEV, axis=1)
    res = adamw(w2['gdn_conv_w'], m2['gdn_conv_w'], v2['gdn_conv_w'], [conv_g], "adamw_conv")
    out_g['gdn_conv_w'], out_d['gdn_conv_w'], out_m['gdn_conv_w'], out_v['gdn_conv_w'] = res

    done = sum(out_d[nm][0, 0] for nm in recv) + out_d['gdn_conv_w'][0, 0] + pd[0, 0]
    after = jnp.zeros((8, 128), F32) + done
    c_got = []
    for key, nm in (('c_gu', "reduce_ffn1_gu_wait"), ('c_down', "reduce_ffn1_down_wait")):
        (c_send, c_recv, c_src, c_land, _), tr = late[key]
        c_got += exchange_wait(c_send, c_recv, c_src, c_land, after, tr, nm)
    recv = dict(zip(['ffn1_w_gate', 'ffn1_w_up', 'ffn1_w_down'], c_got))
    for nm in recv:
        update(nm)

    def depth(t):
        return t[None]

    return (loss, grad_x[None], *[depth(out_g[nm]) for nm in WEIGHTS], *[depth(out_d[nm]) for nm in WEIGHTS],
            *[depth(out_m[nm]) for nm in WEIGHTS], *[depth(out_v[nm]) for nm in WEIGHTS])


def kernel(x, mem, ffn1_pre_norm, ffn1_w_gate, ffn1_w_up, ffn1_w_down, ffn1_post_norm, mix_pre_norm, w_in, fox_f_bias, gdn_conv_w, gdn_a_log, gdn_dt_bias, gdn_out_norm, w_out, mix_post_norm, mem_pre_norm, mem_kv_norm, mem_w_q, mem_w_kv, mem_w_o, mem_post_norm, ffn2_pre_norm, ffn2_w_gate, ffn2_w_up, ffn2_w_down, ffn2_post_norm, loss_target, m_ffn1_pre_norm, m_ffn1_w_gate, m_ffn1_w_up, m_ffn1_w_down, m_ffn1_post_norm, m_mix_pre_norm, m_w_in, m_fox_f_bias, m_gdn_conv_w, m_gdn_a_log, m_gdn_dt_bias, m_gdn_out_norm, m_w_out, m_mix_post_norm, m_mem_pre_norm, m_mem_kv_norm, m_mem_w_q, m_mem_w_kv, m_mem_w_o, m_mem_post_norm, m_ffn2_pre_norm, m_ffn2_w_gate, m_ffn2_w_up, m_ffn2_w_down, m_ffn2_post_norm, v_ffn1_pre_norm, v_ffn1_w_gate, v_ffn1_w_up, v_ffn1_w_down, v_ffn1_post_norm, v_mix_pre_norm, v_w_in, v_fox_f_bias, v_gdn_conv_w, v_gdn_a_log, v_gdn_dt_bias, v_gdn_out_norm, v_w_out, v_mix_post_norm, v_mem_pre_norm, v_mem_kv_norm, v_mem_w_q, v_mem_w_kv, v_mem_w_o, v_mem_post_norm, v_ffn2_pre_norm, v_ffn2_w_gate, v_ffn2_w_up, v_ffn2_w_down, v_ffn2_post_norm):
    return _step(dict(locals()))
```

```python
import functools

import jax
import jax.numpy as jnp
from jax import lax
from jax.experimental import pallas as pl
from jax.experimental.pallas import tpu as pltpu

F32 = jnp.float32
BF = jnp.bfloat16
HI = lax.Precision.HIGHEST
MESH = pl.DeviceIdType.MESH

N_DEV = 8
EPS = 1e-6
D_MODEL = 1024
D_FF = 2816
FF_SHARD = D_FF // N_DEV
FF_SHARD_PAD = 384
D_FF_PAD = FF_SHARD_PAD * N_DEV
FOX_HEADS, FOX_DH = 8, 64
GDN_HEADS, GDN_DH = 4, 128
GDN_CHUNK = 64
CONV_W = 4
MEM_HEADS, MEM_DH = 4, 256
IN_W = 3600
IN_SHARD = IN_W // N_DEV
IN_SHARD_PAD = 512
PROJ_W = 4096
SMALL_F, SMALL_B, SMALL_A = 0, 8, 12

ADAM_LR, ADAM_B1, ADAM_B2, ADAM_EPS, ADAM_WD, ADAM_STEP = 0.001, 0.9, 0.999, 1e-08, 0.01, 10

VMEM_LIMIT = 56 * 1024 * 1024


def _params(sem=None):
    return pltpu.CompilerParams(dimension_semantics=sem, vmem_limit_bytes=VMEM_LIMIT)


def _tile(n, pref, unit=128):
    if n <= pref:
        return n
    t = (pref // unit) * unit
    while t > unit and n % t:
        t -= unit
    assert n % t == 0, (n, pref)
    return t


@functools.partial(jax.custom_vjp, nondiff_argnums=(2, 3))
def bdot(a, b, ca, cb):
    return lax.dot_general(a.astype(BF), b.astype(BF), (((ca,), (cb,)), ((), ())), preferred_element_type=F32)


def _bdot_fwd(a, b, ca, cb):
    return bdot(a, b, ca, cb), (a, b)


def _bdot_bwd(ca, cb, res, g):
    a, b = res
    da = bdot(g, b, 1, 1 - cb) if ca == 1 else bdot(b, g, 1 - cb, 1)
    db = bdot(a, g, 1 - ca, 0) if cb == 0 else bdot(g, a, 0, 1 - ca)
    return da, db


bdot.defvjp(_bdot_fwd, _bdot_bwd)


def hdot(a, b):
    return jnp.dot(a, b, precision=HI, preferred_element_type=F32)


def mdot(a, b):
    return jnp.dot(a, b, precision=lax.Precision.HIGH, preferred_element_type=F32)


def _iota2(shape, dim):
    return lax.broadcasted_iota(jnp.int32, shape, dim)


def _sigmoid(x):
    return 1.0 / (1.0 + jnp.exp(-x))


def _silu(x):
    return x * _sigmoid(x)


def _softplus(x):
    return jnp.maximum(x, 0.0) + jnp.log(1.0 + jnp.exp(-jnp.abs(x)))


def _rms(x, gain):
    return x * lax.rsqrt(jnp.mean(x * x, axis=-1, keepdims=True) + EPS) * gain


def mm(a, b, *, name, ta=False, tb=False, out_dtype=F32, tm=1024, tn=1024, tk=1024, token=None):
    m, k = (a.shape[1], a.shape[0]) if ta else a.shape
    n = b.shape[0] if tb else b.shape[1]
    assert k == (b.shape[1] if tb else b.shape[0]), (a.shape, b.shape, ta, tb)
    tm, tn, tk = _tile(m, tm), _tile(n, tn), _tile(k, tk)
    nk = k // tk
    dims = (((0 if ta else 1,), (1 if tb else 0,)), ((), ()))

    def kern(a_ref, b_ref, *rest):
        o_ref, scratch = (rest[1], rest[2:]) if token is not None else (rest[0], rest[1:])

        def part():
            return lax.dot_general(a_ref[...].astype(BF), b_ref[...].astype(BF), dims, preferred_element_type=F32)

        if nk == 1:
            o_ref[...] = part().astype(o_ref.dtype)
            return
        acc_ref, = scratch
        kk = pl.program_id(2)

        @pl.when(kk == 0)
        def _():
            acc_ref[...] = part()

        @pl.when(kk > 0)
        def _():
            acc_ref[...] += part()

        @pl.when(kk == nk - 1)
        def _():
            o_ref[...] = acc_ref[...].astype(o_ref.dtype)

    a_spec = pl.BlockSpec((tk, tm), lambda i, j, kk: (kk, i)) if ta else pl.BlockSpec((tm, tk), lambda i, j, kk: (i, kk))
    b_spec = pl.BlockSpec((tn, tk), lambda i, j, kk: (j, kk)) if tb else pl.BlockSpec((tk, tn), lambda i, j, kk: (kk, j))
    return pl.pallas_call(
        kern, name=name, grid=(m // tm, n // tn, nk),
        in_specs=[a_spec, b_spec] + ([pl.BlockSpec((8, 128), lambda i, j, kk: (0, 0))] if token is not None else []),
        out_specs=pl.BlockSpec((tm, tn), lambda i, j, kk: (i, j)),
        out_shape=jax.ShapeDtypeStruct((m, n), out_dtype),
        scratch_shapes=[pltpu.VMEM((tm, tn), F32)] if nk > 1 else [],
        compiler_params=_params(("parallel", "parallel", "arbitrary")),
    )(*((a, b) if token is None else (a, b, token)))


def mm_swiglu(a, wgu, *, name):
    m, k = a.shape
    nh = wgu.shape[1] // 2
    tm, tn = _tile(m, 1024), _tile(nh, 512)
    nj = nh // tn

    def kern(a_ref, bg_ref, bu_ref, g_ref, u_ref, act_ref):
        av = a_ref[...]
        g = jnp.dot(av, bg_ref[...], preferred_element_type=F32).astype(BF)
        u = jnp.dot(av, bu_ref[...], preferred_element_type=F32).astype(BF)
        g_ref[...] = g
        u_ref[...] = u
        act_ref[...] = (_silu(g.astype(F32)) * u.astype(F32)).astype(BF)

    tile = pl.BlockSpec((tm, tn), lambda i, j: (i, j))
    out = jax.ShapeDtypeStruct((m, nh), BF)
    return pl.pallas_call(
        kern, name=name, grid=(m // tm, nj),
        in_specs=[pl.BlockSpec((tm, k), lambda i, j: (i, 0)), pl.BlockSpec((k, tn), lambda i, j: (0, j)),
                  pl.BlockSpec((k, tn), lambda i, j: (0, j + nj))],
        out_specs=[tile, tile, tile], out_shape=[out, out, out],
        compiler_params=_params(("parallel", "parallel")),
    )(a, wgu, wgu)


def mm_dswiglu(df, wd, gate, up, *, name, token=None):
    m, k = df.shape
    nh = wd.shape[0]
    tm, tn = _tile(m, 1024), _tile(nh, 512)

    def kern(df_ref, wd_ref, g_ref, u_ref, *rest):
        dg_ref, du_ref = rest[-2:]
        da = lax.dot_general(df_ref[...], wd_ref[...], (((1,), (1,)), ((), ())), preferred_element_type=F32)
        g, u = g_ref[...].astype(F32), u_ref[...].astype(F32)
        sg = _sigmoid(g)
        dg_ref[...] = (da * u * (sg * (1.0 + g * (1.0 - sg)))).astype(BF)
        du_ref[...] = (da * (g * sg)).astype(BF)

    tile = pl.BlockSpec((tm, tn), lambda i, j: (i, j))
    out = jax.ShapeDtypeStruct((m, nh), BF)
    extra = [pl.BlockSpec((8, 128), lambda i, j: (0, 0))] if token is not None else []
    return pl.pallas_call(
        kern, name=name, grid=(m // tm, nh // tn),
        in_specs=[pl.BlockSpec((tm, k), lambda i, j: (i, 0)), pl.BlockSpec((tn, k), lambda i, j: (j, 0)), tile, tile]
        + extra,
        out_specs=[tile, tile], out_shape=[out, out],
        compiler_params=_params(("parallel", "parallel")),
    )(*((df, wd, gate, up) if token is None else (df, wd, gate, up, token)))


def mm_pair(a1, a2, wgu, *, name, token=None):
    m, nh = a1.shape
    n = wgu.shape[0]
    tm, tn, tk = _tile(m, 1024), _tile(n, 1024), _tile(nh, 1024)
    nk = nh // tk
    nt = (((1,), (1,)), ((), ()))

    def kern(a1_ref, a2_ref, b1_ref, b2_ref, *rest):
        o_ref, acc_ref = rest[-2:]
        kk = pl.program_id(2)

        def part():
            return (lax.dot_general(a1_ref[...], b1_ref[...], nt, preferred_element_type=F32)
                    + lax.dot_general(a2_ref[...], b2_ref[...], nt, preferred_element_type=F32))

        @pl.when(kk == 0)
        def _():
            acc_ref[...] = part()

        @pl.when(kk > 0)
        def _():
            acc_ref[...] += part()

        @pl.when(kk == nk - 1)
        def _():
            o_ref[...] = acc_ref[...]

    a_spec = pl.BlockSpec((tm, tk), lambda i, j, kk: (i, kk))
    extra = [pl.BlockSpec((8, 128), lambda i, j, kk: (0, 0))] if token is not None else []
    return pl.pallas_call(
        kern, name=name, grid=(m // tm, n // tn, nk),
        in_specs=[a_spec, a_spec, pl.BlockSpec((tn, tk), lambda i, j, kk: (j, kk)),
                  pl.BlockSpec((tn, tk), lambda i, j, kk: (j, kk + nk))] + extra,
        out_specs=pl.BlockSpec((tm, tn), lambda i, j, kk: (i, j)),
        out_shape=jax.ShapeDtypeStruct((m, n), F32),
        scratch_shapes=[pltpu.VMEM((tm, tn), F32)],
        compiler_params=_params(("parallel", "parallel", "arbitrary")),
    )(*((a1, a2, wgu, wgu) if token is None else (a1, a2, wgu, wgu, token)))


def _row_spec(item, rows):
    if not isinstance(item, tuple):
        return item, pl.BlockSpec((rows, item.shape[1]), lambda i: (i, 0))
    if len(item) == 3:
        arr, w, c = item
        return arr, pl.BlockSpec((rows, w), lambda i: (i, c))
    arr, w, c, lead = item
    return arr, pl.BlockSpec((None, rows, w), lambda i: (lead, i, c))


def _whole_spec(item):
    if not isinstance(item, tuple):
        return item, pl.BlockSpec(item.shape, lambda i: (0,) * item.ndim)
    arr, w, c = item
    return arr, pl.BlockSpec((arr.shape[0], w), lambda i: (0, c))


def rowcall(body, tiled, whole, outs, accs=(), *, rows, total, name):
    rows = min(rows, total)
    assert total % rows == 0
    t_arr, t_spec = zip(*[_row_spec(t, rows) for t in tiled])
    w_arr, w_spec = zip(*[_whole_spec(w) for w in whole]) if whole else ((), ())
    nt, nw, no, na = len(t_arr), len(w_arr), len(outs), len(accs)

    def kern(*refs):
        vals = [r[...] for r in refs[:nt + nw]]
        res = body(*vals)
        if not isinstance(res, (tuple, list)):
            res = (res,)
        assert len(res) == no + na, (name, len(res), no, na)
        for r, v in zip(refs[nt + nw:nt + nw + no], res[:no]):
            r[...] = v.astype(r.dtype)
        if na:
            acc_refs = refs[nt + nw + no:]

            @pl.when(pl.program_id(0) == 0)
            def _():
                for r in acc_refs:
                    r[...] = jnp.zeros_like(r)

            for r, v in zip(acc_refs, res[no:]):
                r[...] += v

    out_shape = [jax.ShapeDtypeStruct((total, w), d) for w, d in outs] + [jax.ShapeDtypeStruct(s, F32) for s in accs]
    out_specs = [pl.BlockSpec((rows, w), lambda i: (i, 0)) for w, _ in outs] + \
                [pl.BlockSpec(s, lambda i: (0, 0)) for s in accs]
    res = pl.pallas_call(
        kern, name=name, grid=(total // rows,),
        in_specs=list(t_spec) + list(w_spec), out_specs=out_specs, out_shape=out_shape,
        compiler_params=_params(("arbitrary",) if na else ("parallel",)),
    )(*t_arr, *w_arr)
    return res


def _colsum(x):
    return jnp.sum(x, axis=0, keepdims=True)


GDN_UNROLL = 2


def _gdn_chunk(q, k, v, z, gb, bb, state, gain, with_starts=False):
    c = GDN_CHUNK
    nh = len(q)
    hs = range(nh)
    r64, c64 = _iota2((c, c), 0), _iota2((c, c), 1)
    incl = r64 >= c64
    strict = r64 > c64
    ltri = incl.astype(F32)
    eye = (r64 == c64).astype(F32)
    pick = (_iota2((GDN_DH, c), 0) == _iota2((GDN_DH, c), 1)).astype(F32)
    last = (_iota2((c, GDN_DH), 0) == c - 1).astype(F32)

    qn = [q[h] * lax.rsqrt(jnp.sum(q[h] * q[h], axis=-1, keepdims=True) + EPS) * (GDN_DH ** -0.5) for h in hs]
    kn = [k[h] * lax.rsqrt(jnp.sum(k[h] * k[h], axis=-1, keepdims=True) + EPS) for h in hs]
    gc = [mdot(ltri, gb[h]) for h in hs]
    gcol = [mdot(gc[h], pick) for h in hs]
    dec = [jnp.exp(jnp.where(incl, gcol[h] - gcol[h].T, -1e30)) for h in hs]
    kb = [kn[h] * bb[h] for h in hs]
    vb = [v[h] * bb[h] for h in hs]
    kk = [bdot(kb[h], kn[h], 1, 1) for h in hs]
    p = [-jnp.where(strict, kk[h] * dec[h], 0.0) for h in hs]
    tinv = [eye + p[h] for h in hs]
    for level in range(5):
        dot = mdot if level < 2 else (lambda a, b: bdot(a, b, 1, 0))
        p = [dot(p[h], p[h]) for h in hs]
        tinv = [tinv[h] + dot(tinv[h], p[h]) for h in hs]
    egc = [jnp.exp(gc[h]) for h in hs]
    u = [mdot(tinv[h], vb[h]) for h in hs]
    w = [mdot(tinv[h], kb[h] * egc[h]) for h in hs]
    attn = [bdot(qn[h], kn[h], 1, 1) * dec[h] for h in hs]
    qd = [qn[h] * egc[h] for h in hs]
    gl = [jnp.sum(gc[h] * last, axis=0, keepdims=True) for h in hs]
    kt = [kn[h] * jnp.exp(gl[h] - gc[h]) for h in hs]
    nst = len(state)
    st, o, mids = list(state), [None] * nh, []
    for c0 in range(0, nh, nst):
        us = range(c0, c0 + nst)
        mids.append(tuple(st))
        ws = [bdot(w[h], st[h - c0], 1, 0) for h in us]
        qs = [bdot(qd[h], st[h - c0], 1, 0) for h in us]
        v_new = [u[h] - ws[h - c0] for h in us]
        av = [bdot(attn[h], v_new[h - c0], 1, 0) for h in us]
        kv = [bdot(kt[h], v_new[h - c0], 0, 0) for h in us]
        st = [st[h - c0] * jnp.exp(gl[h]) + kv[h - c0] for h in us]
        for h in us:
            o[h] = _rms(qs[h - c0] + av[h - c0], gain) * _silu(z[h])
    if with_starts:
        return tuple(o), tuple(st), tuple(mids)
    return tuple(o), tuple(st)


GDN_ROWS = 512
GDN_W = GDN_HEADS * GDN_DH


def gdn_fwd(cqkv, proj, gbb, gain):
    s = cqkv.shape[0]
    nb, cpb = s // GDN_ROWS, GDN_ROWS // GDN_CHUNK
    h4 = GDN_HEADS

    def kern(qkv_ref, z_ref, gb_ref, gain_ref, o_ref, st_ref, state):
        @pl.when(pl.program_id(0) == 0)
        def _():
            state[...] = jnp.zeros_like(state)

        gain_v = gain_ref[...]

        def step(ci, carry):
            sls = [pl.ds(pl.multiple_of((ci * GDN_UNROLL + c) * GDN_CHUNK, GDN_CHUNK), GDN_CHUNK)
                   for c in range(GDN_UNROLL)]
            ins = []
            for sl in sls:
                for h in range(h4):
                    ln = lambda base, h=h: slice(base + h * GDN_DH, base + (h + 1) * GDN_DH)
                    ins.append((qkv_ref[sl, ln(0)], qkv_ref[sl, ln(GDN_W)], qkv_ref[sl, ln(2 * GDN_W)],
                                z_ref[sl, ln(0)], gb_ref[sl, ln(0)], gb_ref[sl, ln(GDN_W)]))
            cols = [tuple(col) for col in zip(*ins)]
            o, new, starts = _gdn_chunk(*cols, tuple(state[h] for h in range(h4)), gain_v, with_starts=True)
            for c, sl in enumerate(sls):
                for h in range(h4):
                    st_ref[h, ci * GDN_UNROLL + c] = starts[c][h]
                    o_ref[sl, h * GDN_DH:(h + 1) * GDN_DH] = o[c * h4 + h]
            for h in range(h4):
                state[h] = new[h]
            return carry

        lax.fori_loop(0, cpb // GDN_UNROLL, step, 0)

    return pl.pallas_call(
        kern, name="gdn_fwd", grid=(nb,),
        in_specs=[pl.BlockSpec((GDN_ROWS, 3 * GDN_W), lambda i: (i, 0)),
                  pl.BlockSpec((GDN_ROWS, GDN_W), lambda i: (i, 6)),
                  pl.BlockSpec((GDN_ROWS, 2 * GDN_W), lambda i: (i, 0)),
                  pl.BlockSpec((1, GDN_DH), lambda i: (0, 0))],
        out_specs=[pl.BlockSpec((GDN_ROWS, GDN_W), lambda i: (i, 0)),
                   pl.BlockSpec((h4, cpb, GDN_DH, GDN_DH), lambda i: (0, i, 0, 0))],
        out_shape=[jax.ShapeDtypeStruct((s, GDN_W), F32),
                   jax.ShapeDtypeStruct((h4, s // GDN_CHUNK, GDN_DH, GDN_DH), F32)],
        scratch_shapes=[pltpu.VMEM((h4, GDN_DH, GDN_DH), F32)],
        compiler_params=_params(("arbitrary",)),
    )(cqkv, proj, gbb, gain)


def gdn_bwd(cqkv, proj, gbb, gain, states, d_mixed):
    s = cqkv.shape[0]
    nb, cpb = s // GDN_ROWS, GDN_ROWS // GDN_CHUNK
    h4 = GDN_HEADS

    def kern(qkv_ref, z_ref, gb_ref, gain_ref, st_ref, do_ref, dqkv_ref, dz_ref, dgb_ref, dgain_ref, dstate):
        @pl.when(pl.program_id(0) == 0)
        def _():
            dgain_ref[...] = jnp.zeros_like(dgain_ref)
            dstate[...] = jnp.zeros_like(dstate)

        gain_v = gain_ref[...]

        def step(t, carry):
            first = (cpb // GDN_UNROLL - 1 - t) * GDN_UNROLL
            sls = [pl.ds(pl.multiple_of((first + c) * GDN_CHUNK, GDN_CHUNK), GDN_CHUNK) for c in range(GDN_UNROLL)]
            prim, cot = [], []
            for sl in sls:
                for h in range(h4):
                    ln = lambda base, h=h: slice(base + h * GDN_DH, base + (h + 1) * GDN_DH)
                    prim.append((qkv_ref[sl, ln(0)], qkv_ref[sl, ln(GDN_W)], qkv_ref[sl, ln(2 * GDN_W)],
                                 z_ref[sl, ln(0)], gb_ref[sl, ln(0)], gb_ref[sl, ln(GDN_W)]))
                    cot.append(do_ref[sl, ln(0)])
            cols = [tuple(col) for col in zip(*prim)]
            st_in = tuple(st_ref[h, first] for h in range(h4))
            vjp = jax.vjp(_gdn_chunk, *cols, st_in, gain_v)[1]
            dq, dk, dv, dz, dg, db, dst, dgn = vjp((tuple(cot), tuple(dstate[h] for h in range(h4))))
            for c, sl in enumerate(sls):
                for h in range(h4):
                    ln = lambda base, h=h: slice(base + h * GDN_DH, base + (h + 1) * GDN_DH)
                    unit = c * h4 + h
                    dqkv_ref[sl, ln(0)] = dq[unit]
                    dqkv_ref[sl, ln(GDN_W)] = dk[unit]
                    dqkv_ref[sl, ln(2 * GDN_W)] = dv[unit]
                    dz_ref[sl, ln(0)] = dz[unit]
                    dgb_ref[sl, ln(0)] = dg[unit]
                    dgb_ref[sl, ln(GDN_W)] = db[unit]
            for h in range(h4):
                dstate[h] = dst[h]
            dgain_ref[...] += dgn
            return carry

        lax.fori_loop(0, cpb // GDN_UNROLL, step, 0)

    def rev(width, cblock=0):
        return pl.BlockSpec((GDN_ROWS, width), lambda i: (nb - 1 - i, cblock))

    return pl.pallas_call(
        kern, name="gdn_bwd", grid=(nb,),
        in_specs=[rev(3 * GDN_W), rev(GDN_W, 6), rev(2 * GDN_W), pl.BlockSpec((1, GDN_DH), lambda i: (0, 0)),
                  pl.BlockSpec((h4, cpb, GDN_DH, GDN_DH), lambda i: (0, nb - 1 - i, 0, 0)), rev(GDN_W, 1)],
        out_specs=[rev(3 * GDN_W), rev(GDN_W), rev(2 * GDN_W), pl.BlockSpec((1, GDN_DH), lambda i: (0, 0))],
        out_shape=[jax.ShapeDtypeStruct((s, 3 * GDN_W), F32), jax.ShapeDtypeStruct((s, GDN_W), F32),
                   jax.ShapeDtypeStruct((s, 2 * GDN_W), F32), jax.ShapeDtypeStruct((1, GDN_DH), F32)],
        scratch_shapes=[pltpu.VMEM((h4, GDN_DH, GDN_DH), F32)],
        compiler_params=_params(("arbitrary",)),
    )(cqkv, proj, gbb, gain, states, d_mixed)


def _gdn_gates(small, prm):
    w = GDN_HEADS * GDN_DH
    lane, head = _iota2((128, w), 0), _iota2((128, w), 1) // GDN_DH
    sel_b = (lane == SMALL_B + head).astype(F32)
    sel_a = (lane == SMALL_A + head).astype(F32)
    prow = _iota2((8, 128), 0)
    a_log = jnp.sum(prm * (prow == 0).astype(F32), axis=0, keepdims=True)
    dt_b = jnp.sum(prm * (prow == 1).astype(F32), axis=0, keepdims=True)
    beta = _sigmoid(hdot(small, sel_b))
    g = hdot(-jnp.exp(a_log) * _softplus(small + dt_b), sel_a)
    return g, beta


CONV_ROWS = 1024
CONV_COLS = 128
CONV_BLOCK0 = 1536 // CONV_COLS


def _shift_down(prev8, cur, s):
    ext = jnp.concatenate([prev8, cur], axis=0)
    return pltpu.roll(ext, s, 0)[8:]


def _shift_up(cur, next8, s):
    n = cur.shape[0]
    ext = jnp.concatenate([cur, next8], axis=0)
    return pltpu.roll(ext, n + 8 - s, 0)[:n]


def _conv_pre(x_ref, w, ci, nchunk):
    r0 = pl.multiple_of(ci * CONV_ROWS, CONV_ROWS)
    cur = x_ref[pl.ds(r0, CONV_ROWS), :]
    prev = x_ref[pl.ds(pl.multiple_of(jnp.maximum(r0 - 8, 0), 8), 8), :]
    prev = jnp.where(ci > 0, prev, 0.0)
    shifted = [cur] + [_shift_down(prev, cur, s) for s in range(1, CONV_W)]
    pre = w[CONV_W - 1:CONV_W, :] * cur
    for s in range(1, CONV_W):
        pre = pre + w[CONV_W - 1 - s:CONV_W - s, :] * shifted[s]
    return r0, pre, shifted


def conv_fwd(proj, conv_w8):
    s = proj.shape[0]
    nchunk = s // CONV_ROWS
    ncol = 3 * GDN_HEADS * GDN_DH // CONV_COLS

    def kern(x_ref, w_ref, y_ref):
        w = w_ref[...]

        def step(ci, carry):
            r0, pre, _ = _conv_pre(x_ref, w, ci, nchunk)
            y_ref[pl.ds(r0, CONV_ROWS), :] = _silu(pre)
            return carry

        lax.fori_loop(0, nchunk, step, 0)

    return pl.pallas_call(
        kern, name="conv_fwd", grid=(ncol,),
        in_specs=[pl.BlockSpec((s, CONV_COLS), lambda j: (0, CONV_BLOCK0 + j)),
                  pl.BlockSpec((8, CONV_COLS), lambda j: (0, j))],
        out_specs=pl.BlockSpec((s, CONV_COLS), lambda j: (0, j)),
        out_shape=jax.ShapeDtypeStruct((s, ncol * CONV_COLS), F32),
        compiler_params=_params(("parallel",)),
    )(proj, conv_w8)


def conv_bwd(proj, conv_w8, dy):
    s = proj.shape[0]
    nchunk = s // CONV_ROWS
    per = 3 * GDN_HEADS * GDN_DH // CONV_COLS
    outs = []
    for part in range(1):
        def kern(x_ref, w_ref, dy_ref, dx_ref, dw_ref, dpre_ref):
            w = w_ref[...]
            rows8 = _iota2((8, CONV_COLS), 0)

            def step1(ci, dw):
                r0, pre, shifted = _conv_pre(x_ref, w, ci, nchunk)
                sg = _sigmoid(pre)
                dpre = dy_ref[pl.ds(r0, CONV_ROWS), :] * sg * (1.0 + pre * (1.0 - sg))
                dpre_ref[pl.ds(r0, CONV_ROWS), :] = dpre
                for sh in range(CONV_W):
                    dw = dw + jnp.where(rows8 == CONV_W - 1 - sh, _colsum(dpre * shifted[sh]), 0.0)
                return dw

            dw_ref[...] = lax.fori_loop(0, nchunk, step1, jnp.zeros((8, CONV_COLS), F32))

            def step2(ci, carry):
                r0 = pl.multiple_of(ci * CONV_ROWS, CONV_ROWS)
                cur = dpre_ref[pl.ds(r0, CONV_ROWS), :]
                nxt = dpre_ref[pl.ds(pl.multiple_of(jnp.minimum(r0 + CONV_ROWS, s - 8), 8), 8), :]
                nxt = jnp.where(ci < nchunk - 1, nxt, 0.0)
                dx = w[CONV_W - 1:CONV_W, :] * cur
                for sh in range(1, CONV_W):
                    dx = dx + w[CONV_W - 1 - sh:CONV_W - sh, :] * _shift_up(cur, nxt, sh)
                dx_ref[pl.ds(r0, CONV_ROWS), :] = dx
                return carry

            lax.fori_loop(0, nchunk, step2, 0)

        outs.append(pl.pallas_call(
            kern, name=f"conv_bwd{part}", grid=(per,),
            in_specs=[pl.BlockSpec((s, CONV_COLS), lambda j, part=part: (0, CONV_BLOCK0 + part * per + j)),
                      pl.BlockSpec((8, CONV_COLS), lambda j, part=part: (0, part * per + j)),
                      pl.BlockSpec((s, CONV_COLS), lambda j: (0, j))],
            out_specs=[pl.BlockSpec((s, CONV_COLS), lambda j: (0, j)),
                       pl.BlockSpec((8, CONV_COLS), lambda j: (0, j))],
            out_shape=[jax.ShapeDtypeStruct((s, per * CONV_COLS), F32),
                       jax.ShapeDtypeStruct((8, per * CONV_COLS), F32)],
            scratch_shapes=[pltpu.VMEM((s, CONV_COLS), F32)],
            compiler_params=_params(("parallel",)),
        )(proj, conv_w8, dy))
    dx = jnp.concatenate([o[0] for o in outs], axis=1)
    dw = jnp.concatenate([o[1] for o in outs], axis=1)
    return dx, dw


FOXF_ROWS = 512
SMALL_BLOCK128 = 3584 // 128


def _log_sigmoid(x):
    return jnp.minimum(x, 0.0) - jnp.log(1.0 + jnp.exp(-jnp.abs(x)))


def fox_f_fwd(proj, bias_row):
    s = proj.shape[0]
    n = s // FOXF_ROWS

    def kern(x_ref, b_ref, f_ref, carry):
        @pl.when(pl.program_id(0) == 0)
        def _():
            carry[...] = jnp.zeros_like(carry)

        heads = _iota2((FOXF_ROWS, 128), 1) < FOX_HEADS
        lf = jnp.where(heads, _log_sigmoid(x_ref[...] + b_ref[...]), 0.0)
        ltri = (_iota2((FOXF_ROWS, FOXF_ROWS), 0) >= _iota2((FOXF_ROWS, FOXF_ROWS), 1)).astype(F32)
        c = hdot(ltri, lf) + carry[...]
        f_ref[...] = c
        carry[...] = c[FOXF_ROWS - 1:FOXF_ROWS, :]

    return pl.pallas_call(
        kern, name="fox_f_fwd", grid=(n,),
        in_specs=[pl.BlockSpec((FOXF_ROWS, 128), lambda i: (i, SMALL_BLOCK128)),
                  pl.BlockSpec((1, 128), lambda i: (0, 0))],
        out_specs=pl.BlockSpec((FOXF_ROWS, 128), lambda i: (i, 0)),
        out_shape=jax.ShapeDtypeStruct((s, 128), F32),
        scratch_shapes=[pltpu.VMEM((1, 128), F32)],
        compiler_params=_params(("arbitrary",)),
    )(proj, bias_row)


def fox_f_bwd(proj, bias_row, d_f):
    s = proj.shape[0]
    n = s // FOXF_ROWS

    def kern(x_ref, b_ref, df_ref, dx_ref, db_ref, carry):
        @pl.when(pl.program_id(0) == 0)
        def _():
            carry[...] = jnp.zeros_like(carry)
            db_ref[...] = jnp.zeros_like(db_ref)

        heads = _iota2((FOXF_ROWS, 128), 1) < FOX_HEADS
        utri = (_iota2((FOXF_ROWS, FOXF_ROWS), 0) <= _iota2((FOXF_ROWS, FOXF_ROWS), 1)).astype(F32)
        rc = hdot(utri, df_ref[...]) + carry[...]
        carry[...] = rc[0:1, :]
        dx = jnp.where(heads, rc * _sigmoid(-(x_ref[...] + b_ref[...])), 0.0)
        dx_ref[...] = dx
        db_ref[...] += _colsum(dx)

    return pl.pallas_call(
        kern, name="fox_f_bwd", grid=(n,),
        in_specs=[pl.BlockSpec((FOXF_ROWS, 128), lambda i: (n - 1 - i, SMALL_BLOCK128)),
                  pl.BlockSpec((1, 128), lambda i: (0, 0)),
                  pl.BlockSpec((FOXF_ROWS, 128), lambda i: (n - 1 - i, 0))],
        out_specs=[pl.BlockSpec((FOXF_ROWS, 128), lambda i: (n - 1 - i, 0)),
                   pl.BlockSpec((1, 128), lambda i: (0, 0))],
        out_shape=[jax.ShapeDtypeStruct((s, 128), F32), jax.ShapeDtypeStruct((1, 128), F32)],
        scratch_shapes=[pltpu.VMEM((1, 128), F32)],
        compiler_params=_params(("arbitrary",)),
    )(proj, bias_row, d_f)


FOX_T = 512
FOX_SCALE = FOX_DH ** -0.5
FOX_PAIRS = FOX_HEADS // 2
NEG = -1e30
_NT = (((1,), (1,)), ((), ()))


def _split3(x):
    def bf(v):
        return lax.reduce_precision(v, exponent_bits=8, mantissa_bits=7)

    hi = bf(x)
    mid = bf(x - hi)
    lo = bf(x - hi - mid)
    return jnp.stack([hi, mid, lo], axis=-1)


def _fox_extras(s, first, second):
    def part(v):
        if v is None:
            return jnp.zeros((s, FOX_HEADS, 3), F32)
        if isinstance(v, float):
            return jnp.full((s, FOX_HEADS, 3), v, F32)
        pairs = v.reshape(s, FOX_PAIRS, 2)
        return _split3(jnp.stack([pairs[:, :, 1], pairs[:, :, 0]], axis=-1).reshape(s, FOX_HEADS))

    cols = jnp.concatenate([part(first), part(second)], axis=-1)
    cols = _pad_to(cols, (s, FOX_HEADS, FOX_DH)).reshape(s, FOX_PAIRS, 2 * FOX_DH)
    return cols.transpose(1, 0, 2).astype(BF)


def _head_masks(rows):
    lane = _iota2((rows, 2 * FOX_DH), 1)
    return lane < FOX_DH, lane >= FOX_DH


def _extra_lane(e, slot):
    return (FOX_DH if e == 0 else 0) + slot


def fox_fwd(qkv, xq, xk, xv):
    s = qkv.shape[0]
    t = min(FOX_T, s)
    n = s // t

    def kern(q_ref, k_ref, v_ref, xq_ref, xk_ref, xv_ref, o_ref, lse_ref):
        i = pl.program_id(1)
        masks = _head_masks(t)
        q_pair, x_pair = q_ref[...] * FOX_SCALE, xq_ref[...]
        q_ops = [jnp.where(mk, q_pair, x_pair) for mk in masks]

        def step(j, carry, masked):
            sl = pl.ds(pl.multiple_of(j * t, t), t)
            k_pair, xk_pair, v_pair, xv_pair = k_ref[sl, :], xk_ref[sl, :], v_ref[sl, :], xv_ref[sl, :]
            k_ops = [jnp.where(mk, k_pair, xk_pair) for mk in masks]
            v_ops = [jnp.where(mk, v_pair, xv_pair) for mk in masks]
            sc = [lax.dot_general(q_ops[e], k_ops[e], _NT, preferred_element_type=F32) for e in range(2)]
            if masked:
                keep = _iota2((t, t), 0) >= _iota2((t, t), 1)
                sc = [jnp.where(keep, x, NEG) for x in sc]
            m_new = [jnp.maximum(carry[e][0], jnp.max(sc[e], axis=1, keepdims=True)) for e in range(2)]
            p = [jnp.exp(sc[e] - m_new[e]).astype(BF) for e in range(2)]
            pv = [jnp.dot(p[e], v_ops[e], preferred_element_type=F32) for e in range(2)]
            return tuple((m_new[e], jnp.exp(carry[e][0] - m_new[e]) * carry[e][1] + pv[e]) for e in range(2))

        init = tuple((jnp.full((t, 1), NEG, F32), jnp.zeros((t, 2 * FOX_DH), F32)) for _ in range(2))
        carry = lax.fori_loop(0, i, lambda j, c: step(j, c, False), init)
        carry = step(i, carry, True)
        lane = _iota2((t, 2 * FOX_DH), 1)
        outs, lses = [], []
        for e in range(2):
            m, acc = carry[e]
            l = jnp.sum(jnp.where(lane == _extra_lane(e, 0), acc, 0.0), axis=1, keepdims=True)
            outs.append(acc / l)
            lses.append(m + jnp.log(l))
        o_ref[...] = jnp.where(masks[0], outs[0], outs[1])
        head0 = 2 * pl.program_id(0)
        lse_ref[...] = jnp.where(lane == head0, lses[0], jnp.where(lane == head0 + 1, lses[1], 0.0))

    pr = FOX_PAIRS
    return pl.pallas_call(
        kern, name="fox_fwd", grid=(pr, n),
        in_specs=[pl.BlockSpec((t, 128), lambda p, i: (i, p)),
                  pl.BlockSpec((s, 128), lambda p, i: (0, pr + p)),
                  pl.BlockSpec((s, 128), lambda p, i: (0, 2 * pr + p)),
                  pl.BlockSpec((None, t, 128), lambda p, i: (p, i, 0)),
                  pl.BlockSpec((None, s, 128), lambda p, i: (p, 0, 0)),
                  pl.BlockSpec((None, s, 128), lambda p, i: (p, 0, 0))],
        out_specs=[pl.BlockSpec((t, 128), lambda p, i: (i, p)),
                   pl.BlockSpec((None, t, 128), lambda p, i: (p, i, 0))],
        out_shape=[jax.ShapeDtypeStruct((s, FOX_HEADS * FOX_DH), F32), jax.ShapeDtypeStruct((pr, s, 128), F32)],
        compiler_params=_params(("parallel", "parallel")),
    )(qkv, qkv, qkv, xq, xk, xv)


def fox_bwd(qkv, d_o, xk, xv, xqb, xdo):
    s = qkv.shape[0]
    t = min(FOX_T, s)
    n = s // t
    w = 2 * FOX_DH

    def both(blocks, slot):
        lane = _iota2(blocks[0].shape, 1)
        head0 = 2 * pl.program_id(0)
        own = jnp.where(lane < FOX_DH, blocks[0], blocks[1])
        sums = [jnp.sum(jnp.where(lane == _extra_lane(e, slot), blocks[e], 0.0), axis=1, keepdims=True)
                for e in range(2)]
        return own, jnp.where(lane == head0, sums[0], jnp.where(lane == head0 + 1, sums[1], 0.0))

    def kern(k_ref, v_ref, xk_ref, xv_ref, q_ref, do_ref, xq_ref, xd_ref,
             dq_ref, dk_ref, dv_ref, sq_ref, sk_ref, dq_acc):
        j = pl.program_id(1)

        @pl.when(j == 0)
        def _():
            dq_acc[...] = jnp.zeros_like(dq_acc)

        masks = _head_masks(t)
        k_ops = [jnp.where(mk, k_ref[...], xk_ref[...]) for mk in masks]
        v_ops = [jnp.where(mk, v_ref[...], xv_ref[...]) for mk in masks]
        k_t = [x.T for x in k_ops]

        def step(i, carry, masked):
            dk, dv = carry
            sl = pl.ds(pl.multiple_of(i * t, t), t)
            q_pair, xq_pair, do_pair, xd_pair = q_ref[sl, :] * FOX_SCALE, xq_ref[sl, :], do_ref[sl, :], xd_ref[sl, :]
            q_ops = [jnp.where(mk, q_pair, xq_pair) for mk in masks]
            do_ops = [jnp.where(mk, do_pair, xd_pair) for mk in masks]
            q_t = [x.T for x in q_ops]
            do_t = [jnp.where(mk, do_pair, 0).astype(BF).T for mk in masks]
            st = [lax.dot_general(k_ops[e], q_ops[e], _NT, preferred_element_type=F32) for e in range(2)]
            dp = [lax.dot_general(v_ops[e], do_ops[e], _NT, preferred_element_type=F32) for e in range(2)]
            if masked:
                keep = _iota2((t, t), 0) <= _iota2((t, t), 1)
                st = [jnp.where(keep, x, NEG) for x in st]
            pt = [jnp.exp(x) for x in st]
            dsb = [(pt[e] * dp[e]).astype(BF) for e in range(2)]
            dv = dv + sum(lax.dot_general(do_t[e], pt[e].astype(BF), _NT, preferred_element_type=F32)
                          for e in range(2))
            dk = tuple(dk[e] + lax.dot_general(q_t[e], dsb[e], _NT, preferred_element_type=F32) for e in range(2))
            for e in range(2):
                dq_acc[i, e * w:(e + 1) * w, :] += jnp.dot(k_t[e], dsb[e], preferred_element_type=F32)
            return dk, dv

        init = ((jnp.zeros((w, t), F32), jnp.zeros((w, t), F32)), jnp.zeros((w, t), F32))
        carry = step(j, init, True)
        dk, dv = lax.fori_loop(j + 1, n, lambda i, c: step(i, c, False), carry)
        dk_ref[...], sk_ref[...] = both([x.T for x in dk], 3)
        dv_ref[...] = dv.T

        @pl.when(j == n - 1)
        def _():
            def out(r, carry):
                sl = pl.ds(pl.multiple_of(r * t, t), t)
                own, sums = both([dq_acc[r, e * w:(e + 1) * w, :].T for e in range(2)], 0)
                dq_ref[sl, :] = own * FOX_SCALE
                sq_ref[sl, :] = sums
                return carry

            lax.fori_loop(0, n, out, 0)

    pr = FOX_PAIRS
    flat = jax.ShapeDtypeStruct((s, FOX_HEADS * FOX_DH), F32)
    tile = pl.BlockSpec((t, 128), lambda p, j: (j, p))
    whole = pl.BlockSpec((s, 128), lambda p, j: (0, p))
    return pl.pallas_call(
        kern, name="fox_bwd", grid=(pr, n),
        in_specs=[pl.BlockSpec((t, 128), lambda p, j: (j, pr + p)),
                  pl.BlockSpec((t, 128), lambda p, j: (j, 2 * pr + p)),
                  pl.BlockSpec((None, t, 128), lambda p, j: (p, j, 0)),
                  pl.BlockSpec((None, t, 128), lambda p, j: (p, j, 0)),
                  whole, whole,
                  pl.BlockSpec((None, s, 128), lambda p, j: (p, 0, 0)),
                  pl.BlockSpec((None, s, 128), lambda p, j: (p, 0, 0))],
        out_specs=[whole, tile, tile, whole, tile],
        out_shape=[flat] * 5,
        scratch_shapes=[pltpu.VMEM((n, 2 * w, t), F32)],
        compiler_params=_params(("parallel", "arbitrary")),
    )(qkv, qkv, xk, xv, qkv, d_o, xqb, xdo)


def _xattn_head(q, k, v):
    sc = bdot(q, k, 1, 1) * (MEM_DH ** -0.5)
    e = jnp.exp(sc - lax.stop_gradient(jnp.max(sc, axis=-1, keepdims=True)))
    p = e / jnp.sum(e, axis=-1, keepdims=True)
    return bdot(p, v, 1, 0)


def xattn_fwd(q, kv):
    s = q.shape[0]
    hh = MEM_HEADS

    def body(*vals):
        qs, ks, vs = vals[:hh], vals[hh:2 * hh], vals[2 * hh:]
        return jnp.concatenate([_xattn_head(qs[a], ks[a], vs[a]) for a in range(hh)], axis=1)

    return rowcall(body, [(q, MEM_DH, a) for a in range(hh)],
                   [(kv, MEM_DH, a) for a in range(2 * hh)],
                   [(hh * MEM_DH, BF)], rows=512, total=s, name="xattn_fwd")[0]


def xattn_bwd(q, kv, d_o):
    s = q.shape[0]
    hh = MEM_HEADS

    def body(*vals):
        qs, dos = vals[:hh], vals[hh:2 * hh]
        ks, vs = vals[2 * hh:3 * hh], vals[3 * hh:]
        dqs, dks, dvs = [], [], []
        for a in range(hh):
            _, vjp = jax.vjp(_xattn_head, qs[a], ks[a], vs[a])
            dq, dk, dv = vjp(dos[a])
            dqs.append(dq)
            dks.append(dk)
            dvs.append(dv)
        return jnp.concatenate(dqs, axis=1), jnp.concatenate(dks + dvs, axis=1)

    return rowcall(body, [(q, MEM_DH, a) for a in range(hh)] + [(d_o, MEM_DH, a) for a in range(hh)],
                   [(kv, MEM_DH, a) for a in range(2 * hh)],
                   [(hh * MEM_DH, BF)], [kv.shape], rows=512, total=s, name="xattn_bwd")


def _slab(ref, axis, start, size):
    if axis is None:
        return ref
    if axis == "lead":
        return ref.at[start]
    idx = pl.ds(pl.multiple_of(start, 128 if axis == 1 else 16), size)
    return ref.at[idx] if axis == 0 else ref.at[:, idx]


def exchange(inputs, outputs, transfers, name):
    ni, no, nt = len(inputs), len(outputs), len(transfers)
    npeer = N_DEV - 1

    def body(*refs):
        ins, outs = refs[:ni], refs[ni:ni + no]
        send, recv, loc = refs[ni + no:]
        x, y, c = lax.axis_index("x"), lax.axis_index("y"), lax.axis_index("c")
        me = 4 * x + 2 * y + c

        def peer(p):
            px = 1 - x if p & 4 else x
            py = 1 - y if p & 2 else y
            pc = 1 - c if p & 1 else c
            return (px, py, pc), 4 * px + 2 * py + pc

        def view(ref, spec, who):
            axis, off, stride, size = spec
            return _slab(ref, axis, off + who * stride, size)

        local, remote = [], []
        for w, (ii, src, oi, dst) in enumerate(transfers):
            cp = pltpu.make_async_copy(view(ins[ii], src, me), view(outs[oi], dst, me), loc.at[w])
            cp.start()
            local.append(cp)
        for p in range(1, N_DEV):
            dev, idx = peer(p)
            for w, (ii, src, oi, dst) in enumerate(transfers):
                k = w * npeer + p - 1
                out_cp = pltpu.make_async_remote_copy(
                    src_ref=view(ins[ii], src, idx), dst_ref=view(outs[oi], dst, me), send_sem=send.at[k],
                    recv_sem=recv.at[k], device_id=dev, device_id_type=MESH)
                out_cp.start()
                in_cp = pltpu.make_async_remote_copy(
                    src_ref=view(ins[ii], src, idx), dst_ref=view(outs[oi], dst, idx), send_sem=send.at[k],
                    recv_sem=recv.at[k], device_id=dev, device_id_type=MESH)
                remote.append((out_cp, in_cp))
        for out_cp, in_cp in remote:
            in_cp.wait_recv()
            out_cp.wait_send()
        for cp in local:
            cp.wait()

    hbm = pl.BlockSpec(memory_space=pl.ANY)
    return pl.pallas_call(
        body, name=name, in_specs=[hbm] * ni, out_specs=[hbm] * no, out_shape=list(outputs),
        scratch_shapes=[pltpu.SemaphoreType.DMA((nt * npeer,)), pltpu.SemaphoreType.DMA((nt * npeer,)),
                        pltpu.SemaphoreType.DMA((nt,))],
        compiler_params=pltpu.CompilerParams(has_side_effects=True),
    )(*inputs)


def _peer(p):
    x, y, c = lax.axis_index("x"), lax.axis_index("y"), lax.axis_index("c")
    px = 1 - x if p & 4 else x
    py = 1 - y if p & 2 else y
    pc = 1 - c if p & 1 else c
    return (px, py, pc), 4 * px + 2 * py + pc


def _view(ref, spec, who):
    axis, off, stride, size = spec
    return _slab(ref, axis, off + who * stride, size)


def place_own(inputs, outputs, transfers):
    me = 4 * lax.axis_index("x") + 2 * lax.axis_index("y") + lax.axis_index("c")
    lands = [lax.empty(o.shape, o.dtype) for o in outputs]
    for ii, src, oi, dst in transfers:
        axis, off, stride, size = src
        own = inputs[ii] if axis is None else lax.dynamic_slice_in_dim(inputs[ii], off + me * stride, size, axis)
        axis, off, stride, size = dst
        if axis == "lead":
            lands[oi] = lax.dynamic_update_slice_in_dim(lands[oi], own[None], me, 0)
        else:
            lands[oi] = lax.dynamic_update_slice_in_dim(lands[oi], own, off + me * stride, axis)
    return lands


_HBM = pl.BlockSpec(memory_space=pltpu.HBM)
_SEM = pl.BlockSpec(memory_space=pltpu.SEMAPHORE)
_EFFECT = pltpu.SideEffectType.DATAFLOW_SIDE_EFFECTING


def _remote_copies(ins, lands, transfers, send, recv):
    npeer = N_DEV - 1
    me = 4 * lax.axis_index("x") + 2 * lax.axis_index("y") + lax.axis_index("c")
    pairs = []
    for p in range(1, N_DEV):
        dev, idx = _peer(p)
        for w, (ii, src, oi, dst) in enumerate(transfers):
            k = w * npeer + p - 1
            common = dict(src_ref=_view(ins[ii], src, idx), send_sem=send.at[k], recv_sem=recv.at[k],
                          device_id=dev, device_id_type=MESH)
            pairs.append((pltpu.make_async_remote_copy(dst_ref=_view(lands[oi], dst, me), **common),
                          pltpu.make_async_remote_copy(dst_ref=_view(lands[oi], dst, idx), **common)))
    return pairs


def exchange_start(inputs, lands, transfers, after, name):
    ni, nl, nsem = len(inputs), len(lands), len(transfers) * (N_DEV - 1)

    def body(*refs):
        ins, lnd = refs[:ni], refs[ni:ni + nl]
        send, recv = refs[ni + nl + 1], refs[ni + nl + 2]
        token = refs[-1]
        for out_cp, _ in _remote_copies(ins, lnd, transfers, send, recv):
            out_cp.start()
        token[...] = jnp.zeros_like(token)

    args = [pltpu.with_memory_space_constraint(a, pltpu.HBM) for a in list(inputs) + list(lands)]
    res = pl.pallas_call(
        body, name=name,
        out_shape=(pltpu.SemaphoreType.DMA((nsem,)), pltpu.SemaphoreType.DMA((nsem,)),
                   *[pltpu.HBM(a.shape, a.dtype) for a in args], jax.ShapeDtypeStruct((8, 128), F32)),
        in_specs=[_HBM] * (ni + nl) + [pl.BlockSpec(memory_space=pl.ANY)],
        out_specs=(_SEM, _SEM, *[_HBM] * (ni + nl), pl.BlockSpec(memory_space=pltpu.VMEM)),
        input_output_aliases={k: k + 2 for k in range(ni + nl)},
        compiler_params=pltpu.CompilerParams(has_side_effects=_EFFECT),
    )(*args, after)
    return res[0], res[1], list(res[2:2 + ni]), list(res[2 + ni:2 + ni + nl]), res[-1]


def exchange_wait(send, recv, inputs, lands, after, transfers, name):
    ni, nl = len(inputs), len(lands)

    def body(*refs):
        ins, lnd = refs[:ni], refs[ni:ni + nl]
        send_r, recv_r = refs[ni + nl], refs[ni + nl + 1]
        for out_cp, in_cp in _remote_copies(ins, lnd, transfers, send_r, recv_r):
            out_cp.wait_send()
            in_cp.wait_recv()

    res = pl.pallas_call(
        body, name=name,
        out_shape=tuple(pltpu.HBM(a.shape, a.dtype) for a in list(inputs) + list(lands)),
        in_specs=[_HBM] * (ni + nl) + [_SEM, _SEM, pl.BlockSpec(memory_space=pl.ANY)],
        out_specs=tuple([_HBM] * (ni + nl)),
        input_output_aliases={k: k for k in range(ni + nl)},
        compiler_params=pltpu.CompilerParams(has_side_effects=_EFFECT),
    )(*inputs, *lands, send, recv, after)
    return list(res[ni:])


def adamw(w, m, v, contribs, name):
    r, c = w.shape
    nc = len(contribs)
    rows = next((r // d for d in (4, 2) if r % d == 0 and (r // d) % 16 == 0), r)
    c1, c2 = 1.0 - ADAM_B1 ** ADAM_STEP, 1.0 - ADAM_B2 ** ADAM_STEP

    def body(wv, mv, vv, *gs):
        g = gs[0].astype(F32)
        for extra in gs[1:]:
            g = g + extra.astype(F32)
        g = g[:, :c]
        m_new = ADAM_B1 * mv + (1.0 - ADAM_B1) * g
        v_new = ADAM_B2 * vv + (1.0 - ADAM_B2) * (g * g)
        delta = -ADAM_LR * ((m_new / c1) / (jnp.sqrt(v_new / c2) + ADAM_EPS) + ADAM_WD * wv)
        return g, delta, m_new, v_new

    assert nc >= 1
    return rowcall(body, [w, m, v] + list(contribs), [], [(c, F32)] * 4, rows=rows, total=r, name=name)


WEIGHTS = ['ffn1_pre_norm', 'ffn1_w_gate', 'ffn1_w_up', 'ffn1_w_down', 'ffn1_post_norm', 'mix_pre_norm', 'w_in',
           'fox_f_bias', 'gdn_conv_w', 'gdn_a_log', 'gdn_dt_bias', 'gdn_out_norm', 'w_out', 'mix_post_norm',
           'mem_pre_norm', 'mem_kv_norm', 'mem_w_q', 'mem_w_kv', 'mem_w_o', 'mem_post_norm', 'ffn2_pre_norm',
           'ffn2_w_gate', 'ffn2_w_up', 'ffn2_w_down', 'ffn2_post_norm']
GAINS = ['ffn1_pre_norm', 'ffn1_post_norm', 'mix_pre_norm', 'mix_post_norm', 'mem_pre_norm', 'mem_kv_norm',
         'mem_post_norm', 'ffn2_pre_norm', 'ffn2_post_norm']
BIG = ['ffn1_w_gate', 'ffn1_w_up', 'ffn1_w_down', 'w_in', 'w_out', 'mem_w_q', 'mem_w_kv', 'mem_w_o',
       'ffn2_w_gate', 'ffn2_w_up', 'ffn2_w_down']
PACK_ROWS = 24
ROW_MISC = len(GAINS)
ROW_CONV = ROW_MISC + 1
COL_FBIAS, COL_ALOG, COL_DTB, COL_ONORM, COL_LOSS = 0, 8, 12, 128, 256
CONV_CH = 3 * GDN_HEADS * GDN_DH


def _pad_to(a, shape):
    return jnp.pad(a, [(0, t - s) for s, t in zip(a.shape, shape)])


def _pack(get, conv=None, loss=None):
    rows = [get(nm) for nm in GAINS]
    misc = jnp.concatenate([get('fox_f_bias'), get('gdn_a_log'), get('gdn_dt_bias'),
                            jnp.zeros((1, COL_ONORM - COL_DTB - 4), F32), get('gdn_out_norm'),
                            jnp.zeros((1, 1), F32) if loss is None else loss.reshape(1, 1)], axis=1)
    rows.append(_pad_to(misc, (1, D_MODEL)))
    rows.append(jnp.zeros((6, D_MODEL), F32) if conv is None else conv.reshape(6, D_MODEL))
    return _pad_to(jnp.concatenate(rows, axis=0), (PACK_ROWS, D_MODEL))


def _unpack(p):
    out = {nm: p[i:i + 1] for i, nm in enumerate(GAINS)}
    misc = p[ROW_MISC:ROW_MISC + 1]
    out['fox_f_bias'] = misc[:, COL_FBIAS:COL_FBIAS + FOX_HEADS]
    out['gdn_a_log'] = misc[:, COL_ALOG:COL_ALOG + GDN_HEADS]
    out['gdn_dt_bias'] = misc[:, COL_DTB:COL_DTB + GDN_HEADS]
    out['gdn_out_norm'] = misc[:, COL_ONORM:COL_ONORM + GDN_DH]
    return out


def _ffn_fwd(h, pre, wgu, wd, tag):
    s = h.shape[0]
    u, = rowcall(_rms, [h], [pre], [(D_MODEL, BF)], rows=512, total=s, name=tag + "_pre")
    if callable(wgu):
        wgu = wgu(u)
    gate, up, act = mm_swiglu(u, wgu, name=tag + "_gate_up")
    if callable(wd):
        wd = wd(act)
    f = mm(act, wd, name=tag + "_down")
    return u, gate, up, act, f


def _half_rms(a, g):
    return 0.5 * _rms(a, g)


def _ffn_bwd(dh_out, h, pre, post, wgu, wd, saved, tag, on_dwd=None, on_dwgu=None):
    u, gate, up, act, f = saved
    s = h.shape[0]

    def b_post(dh, fv, pg):
        return jax.vjp(_half_rms, fv, pg)[1](dh)

    df, dpost = rowcall(b_post, [dh_out, f], [post], [(D_MODEL, BF)], [(1, D_MODEL)], rows=512, total=s,
                        name=tag + "_bwd_post")
    dwd = mm(act, df, ta=True, out_dtype=BF, name=tag + "_bwd_dwd")
    dgate, dup = mm_dswiglu(df, wd, gate, up, name=tag + "_bwd_dact", token=on_dwd(dwd) if on_dwd else None)
    dwg = mm(u, dgate, ta=True, out_dtype=BF, name=tag + "_bwd_dwg")
    dwu = mm(u, dup, ta=True, out_dtype=BF, name=tag + "_bwd_dwu")
    du = mm_pair(dgate, dup, wgu, name=tag + "_bwd_du", token=on_dwgu(dwg, dwu) if on_dwgu else None)

    def b_pre(dh, duv, hv, pg):
        dx, dpre = jax.vjp(_rms, hv, pg)[1](duv)
        return dh + dx, dpre

    dh, dpre = rowcall(b_pre, [dh_out, du, h], [pre], [(D_MODEL, F32)], [(1, D_MODEL)], rows=512, total=s,
                       name=tag + "_bwd_pre")
    return dh, dwg, dwu, dwd, dpre, dpost


def _residual_rms(h, a, g):
    return h + _rms(a, g)


def _bwd_residual(dh, a, g):
    return jax.vjp(_rms, a, g)[1](dh)


def _step(a):
    x, mem = a['x'][0], a['mem'][0]
    s = x.shape[0]
    me = 4 * lax.axis_index("x") + 2 * lax.axis_index("y") + lax.axis_index("c")
    w2 = {nm: a[nm][0] for nm in WEIGHTS}
    m2 = {nm: a['m_' + nm][0] for nm in WEIGHTS}
    v2 = {nm: a['v_' + nm][0] for nm in WEIGHTS}
    small = {nm: w2[nm][None] for nm in WEIGHTS if nm not in BIG and nm != 'gdn_conv_w'}

    def ff_cols(w):
        return _pad_to(w, (D_MODEL, FF_SHARD_PAD)).astype(BF)

    def ff_rows(w):
        return _pad_to(w, (FF_SHARD_PAD, D_MODEL)).astype(BF)

    whole = (None, 0, 0, 0)
    conv_pad = 256
    g_in = [ff_cols(w2['ffn1_w_gate']), ff_cols(w2['ffn1_w_up']), ff_rows(w2['ffn1_w_down']),
            ff_cols(w2['ffn2_w_gate']), ff_cols(w2['ffn2_w_up']), ff_rows(w2['ffn2_w_down']),
            _pad_to(w2['w_in'], (D_MODEL, IN_SHARD_PAD)).astype(BF), w2['w_out'].astype(BF),
            w2['mem_w_q'].astype(BF), w2['mem_w_kv'].astype(BF), w2['mem_w_o'].astype(BF),
            _pad_to(w2['gdn_conv_w'], (8, conv_pad))]
    g_out = [jax.ShapeDtypeStruct((D_MODEL, 2 * D_FF_PAD), BF), jax.ShapeDtypeStruct((D_FF_PAD, D_MODEL), BF),
             jax.ShapeDtypeStruct((D_MODEL, 2 * D_FF_PAD), BF), jax.ShapeDtypeStruct((D_FF_PAD, D_MODEL), BF),
             jax.ShapeDtypeStruct((D_MODEL, N_DEV * IN_SHARD_PAD), BF), jax.ShapeDtypeStruct((D_MODEL, D_MODEL), BF),
             jax.ShapeDtypeStruct((D_MODEL, D_MODEL), BF), jax.ShapeDtypeStruct((D_MODEL, 2 * D_MODEL), BF),
             jax.ShapeDtypeStruct((D_MODEL, D_MODEL), BF), jax.ShapeDtypeStruct((8, N_DEV * conv_pad), F32)]
    sp_, dm = FF_SHARD_PAD, D_MODEL // N_DEV
    g_tr = [(0, whole, 0, (1, 0, sp_, sp_)), (1, whole, 0, (1, D_FF_PAD, sp_, sp_)), (2, whole, 1, (0, 0, sp_, sp_)),
            (3, whole, 2, (1, 0, sp_, sp_)), (4, whole, 2, (1, D_FF_PAD, sp_, sp_)), (5, whole, 3, (0, 0, sp_, sp_)),
            (6, whole, 4, (1, 0, IN_SHARD_PAD, IN_SHARD_PAD)), (7, whole, 5, (0, 0, dm, dm)),
            (8, whole, 6, (0, 0, dm, dm)), (9, whole, 7, (1, 0, 2 * dm, 2 * dm)), (10, whole, 8, (0, 0, dm, dm)),
            (11, whole, 9, (1, 0, conv_pad, conv_pad))]
    def pick(idx):
        ins = sorted({g_tr[k][0] for k in idx})
        outs = sorted({g_tr[k][2] for k in idx})
        tr = [(ins.index(g_tr[k][0]), g_tr[k][1], outs.index(g_tr[k][2]), g_tr[k][3]) for k in idx]
        return [g_in[i] for i in ins], [g_out[o] for o in outs], tr

    stages, after = [], g_in[0]
    for nm, idx in (("gate_up", [0, 1]), ("down", [2]), ("mix", [6, 7, 11]), ("late", [8, 9, 10, 3, 4, 5])):
        st_in, st_out, st_tr = pick(idx)
        st = exchange_start(st_in, place_own(st_in, st_out, st_tr), st_tr, after, "gather_%s_start" % nm)
        stages.append((st, st_tr, "gather_%s_wait" % nm))
        after = st[4]
    g_token = after

    def gather_wait(k, after_):
        (send_, recv_, src_, land_, _), tr_, nm_ = stages[k]
        return exchange_wait(send_, recv_, src_, land_, after_, tr_, nm_)

    bias_row = _pad_to(small['fox_f_bias'], (1, 128))
    gate_prm = _pad_to(jnp.concatenate([_pad_to(small['gdn_a_log'], (1, 128 - SMALL_A)),
                                        _pad_to(small['gdn_dt_bias'], (1, 128 - SMALL_A))], axis=0),
                       (8, 128 - SMALL_A))
    gate_prm = jnp.pad(gate_prm, ((0, 0), (SMALL_A, 0)))
    onorm = small['gdn_out_norm']

    late = {}

    def wgu1_when(u):
        late['wgu1'], = gather_wait(0, u)
        return late['wgu1']

    def wd1_when(act):
        late['wd1'], = gather_wait(1, act)
        return late['wd1']

    sv1 = _ffn_fwd(x, small['ffn1_pre_norm'] + g_token[0, 0], wgu1_when, wd1_when, "ffn1")
    wgu1, wd1 = late['wgu1'], late['wd1']
    h1, = rowcall(lambda h, f, g: h + _half_rms(f, g), [x, sv1[4]], [small['ffn1_post_norm']], [(D_MODEL, F32)],
                  rows=512, total=s, name="ffn1_out")
    w_in_g, w_out, conv_g = gather_wait(2, h1)
    w_in = jnp.concatenate([w_in_g[:, j * IN_SHARD_PAD:j * IN_SHARD_PAD + IN_SHARD] for j in range(N_DEV)],
                           axis=1)
    sp = [0, 512, 1024, 1536, 1544, 2056, 2568, 3080, 3592, 3596, 3600]
    fq, fk, fv, ff, gq, gk, gv, gz, gb, ga = [w_in[:, sp[i]:sp[i + 1]] for i in range(10)]
    w_proj = jnp.concatenate([fq, fk, fv, gq, gk, gv, gz, ff, gb, ga,
                              jnp.zeros((D_MODEL, PROJ_W - 3584 - 16), BF)], axis=1)
    conv_w8 = conv_g.reshape(8, N_DEV, conv_pad)[:, :, :CONV_CH // N_DEV].reshape(8, CONV_CH)


    u2, = rowcall(_rms, [h1], [small['mix_pre_norm']], [(D_MODEL, BF)], rows=512, total=s, name="mix_pre")
    proj = mm(u2, w_proj, name="mix_proj")
    f_cum = fox_f_fwd(proj, bias_row)
    f_heads = f_cum[:, :FOX_HEADS]
    qkv_bf = proj[:, :3 * FOX_HEADS * FOX_DH].astype(BF)
    xk, xv = _fox_extras(s, 1.0, -f_heads), _fox_extras(s, 1.0, None)
    fox_flat, lse = fox_fwd(qkv_bf, _fox_extras(s, f_heads, 1.0), xk, xv)
    lse_heads = jnp.sum(lse, axis=0)[:, :FOX_HEADS]
    cqkv = conv_fwd(proj, conv_w8)
    g_l, b_l = rowcall(_gdn_gates, [(proj, 128, SMALL_BLOCK128)], [gate_prm], [(512, F32), (512, F32)],
                       rows=512, total=s, name="gdn_gates")
    gbb = jnp.concatenate([g_l, b_l], axis=1)
    gdn_o, states = gdn_fwd(cqkv, proj, gbb, onorm)
    mixed = jnp.concatenate([fox_flat, gdn_o], axis=1).astype(BF)
    mo = mm(mixed, w_out, name="mix_out")
    h2, = rowcall(_residual_rms, [h1, mo], [small['mix_post_norm']], [(D_MODEL, F32)], rows=512, total=s,
                  name="mix_res")

    hq, = rowcall(_rms, [h2], [small['mem_pre_norm']], [(D_MODEL, BF)], rows=512, total=s, name="mem_pre")
    mn, = rowcall(_rms, [mem], [small['mem_kv_norm']], [(D_MODEL, BF)], rows=256, total=mem.shape[0], name="mem_kvn")
    wgu2, wd2, w_q, w_kv, w_o = gather_wait(3, h2)
    q_mem = mm(hq, w_q, name="mem_q")
    kv_mem = mm(mn, w_kv, name="mem_kv")
    o_mem = xattn_fwd(q_mem, kv_mem)
    c_mem = mm(o_mem, w_o, name="mem_o")
    h3, = rowcall(_residual_rms, [h2, c_mem], [small['mem_post_norm']], [(D_MODEL, F32)], rows=512, total=s,
                  name="mem_res")

    sv2 = _ffn_fwd(h3, small['ffn2_pre_norm'], wgu2, wd2, "ffn2")

    def b_loss(h, f, tgt, g):
        err = h + _half_rms(f, g) - tgt
        part = 0.5 * jnp.sum(jnp.mean(err * err, axis=-1, keepdims=True), axis=0, keepdims=True)
        return err * (1.0 / D_MODEL), jnp.broadcast_to(part, (1, 128))

    dy, loss_acc = rowcall(b_loss, [h3, sv2[4], a['loss_target'][0]], [small['ffn2_post_norm']], [(D_MODEL, F32)],
                           [(1, 128)], rows=512, total=s, name="loss")

    grads = {}
    dh3, dwg2, dwu2, dwd2, grads['ffn2_pre_norm'], grads['ffn2_post_norm'] = _ffn_bwd(
        dy, h3, small['ffn2_pre_norm'], small['ffn2_post_norm'], wgu2, wd2, sv2, "ffn2")

    lead = ("lead", 0, 1, 0)

    def land(r, c, dt=BF):
        return jax.ShapeDtypeStruct((N_DEV, r, c), dt)

    ffn_tr = [(0, (1, 0, sp_, sp_), 0, lead), (1, (1, 0, sp_, sp_), 1, lead), (2, (0, 0, sp_, FF_SHARD), 2, lead)]
    ffn_land = [land(D_MODEL, sp_), land(D_MODEL, sp_), land(FF_SHARD, D_MODEL)]
    a_in = [dwg2, dwu2, dwd2]
    a_send, a_recv, a_src, a_land, a_token = exchange_start(a_in, place_own(a_in, ffn_land, ffn_tr), ffn_tr, dh3,
                                                            "reduce_ffn2_start")

    dc, grads['mem_post_norm'] = rowcall(_bwd_residual, [dh3, c_mem], [small['mem_post_norm'] + a_token[0, 0]],
                                         [(D_MODEL, BF)],
                                         [(1, D_MODEL)], rows=512, total=s, name="mem_bwd_res")
    d_o = mm(dc, w_o, tb=True, name="mem_bwd_do")
    dw_o = mm(o_mem, dc, ta=True, out_dtype=BF, name="mem_bwd_dwo")
    dq_mem, dkv = xattn_bwd(q_mem, kv_mem, d_o)
    dhq = mm(dq_mem, w_q, tb=True, name="mem_bwd_dhq")
    dw_q = mm(hq, dq_mem, ta=True, out_dtype=BF, name="mem_bwd_dwq")
    dmn = mm(dkv, w_kv, tb=True, name="mem_bwd_dmn")
    dw_kv = mm(mn, dkv, ta=True, out_dtype=BF, name="mem_bwd_dwkv")
    _, grads['mem_kv_norm'] = rowcall(lambda d, mv, g: jax.vjp(_rms, mv, g)[1](d), [dmn, mem],
                                      [small['mem_kv_norm']], [(D_MODEL, F32)], [(1, D_MODEL)], rows=256,
                                      total=mem.shape[0], name="mem_bwd_kvn")

    def b_pre(dh, duv, hv, pg):
        dx, dpre = jax.vjp(_rms, hv, pg)[1](duv)
        return dh + dx, dpre

    dh2, grads['mem_pre_norm'] = rowcall(b_pre, [dh3, dhq, h2], [small['mem_pre_norm']], [(D_MODEL, F32)],
                                         [(1, D_MODEL)], rows=512, total=s, name="mem_bwd_pre")

    dmo, grads['mix_post_norm'] = rowcall(_bwd_residual, [dh2, mo], [small['mix_post_norm']], [(D_MODEL, BF)],
                                          [(1, D_MODEL)], rows=512, total=s, name="mix_bwd_res")
    d_mixed = mm(dmo, w_out, tb=True, name="mix_bwd_dmixed")
    dw_out = mm(mixed, dmo, ta=True, out_dtype=BF, name="mix_bwd_dwout")
    def b_delta(do, o):
        sel = (_iota2((512, 128), 0) // FOX_DH == _iota2((512, 128), 1)).astype(F32)
        return hdot(do * o, sel)

    delta, = rowcall(b_delta, [(d_mixed, 512, 0), fox_flat], [], [(128, F32)], rows=512, total=s, name="fox_delta")
    dfox_q, dfox_k, dvf, sum_q, sum_k = fox_bwd(qkv_bf, d_mixed[:, :512].astype(BF), xk, xv,
                                                _fox_extras(s, f_heads - lse_heads, 1.0),
                                                _fox_extras(s, -delta[:, :FOX_HEADS], None))
    d_f = jnp.sum((sum_q - sum_k).reshape(s, FOX_PAIRS, 2 * FOX_DH), axis=1)
    dsmall_f, dbias = fox_f_bwd(proj, bias_row, d_f)
    grads['fox_f_bias'] = dbias[:, :FOX_HEADS]
    dcqkv, dz, dgb, grads['gdn_out_norm'] = gdn_bwd(cqkv, proj, gbb, onorm, states, d_mixed)

    def b_gates(sm, dsf, dg, db, prm):
        dsm, dprm = jax.vjp(_gdn_gates, sm, prm)[1]((dg, db))
        return dsm + dsf, dprm

    dsmall, dprm = rowcall(b_gates, [(proj, 128, SMALL_BLOCK128), dsmall_f, (dgb, 512, 0), (dgb, 512, 1)], [gate_prm],
                           [(128, F32)],
                           [(8, 128)], rows=512, total=s, name="gdn_bwd_gates")
    grads['gdn_a_log'] = dprm[0:1, SMALL_A:SMALL_A + GDN_HEADS]
    grads['gdn_dt_bias'] = dprm[1:2, SMALL_A:SMALL_A + GDN_HEADS]
    dqkv_pre, dconv8 = conv_bwd(proj, conv_w8, dcqkv)
    dproj = jnp.concatenate([dfox_q, dfox_k, dvf, dqkv_pre, dz, dsmall,
                             jnp.zeros((s, PROJ_W - 3584 - 128), F32)], axis=1).astype(BF)
    du2 = mm(dproj, w_proj, tb=True, name="mix_bwd_du")
    dw_proj = mm(u2, dproj, ta=True, out_dtype=BF, name="mix_bwd_dwproj")
    dh1, grads['mix_pre_norm'] = rowcall(b_pre, [dh2, du2, h1], [small['mix_pre_norm']], [(D_MODEL, F32)],
                                         [(1, D_MODEL)], rows=512, total=s, name="mix_bwd_pre")

    dw_in = jnp.concatenate([dw_proj[:, :1536], dw_proj[:, 3584:3592], dw_proj[:, 1536:3584],
                             dw_proj[:, 3592:3600]], axis=1)
    gap = jnp.zeros((D_MODEL, IN_SHARD_PAD - IN_SHARD), BF)
    dw_in = jnp.concatenate([piece for j in range(N_DEV) for piece in (dw_in[:, j * IN_SHARD:(j + 1) * IN_SHARD], gap)],
                            axis=1)
    b_in = [dw_in, dw_out, dw_q, dw_kv, dw_o]
    b_tr = [(0, (1, 0, IN_SHARD_PAD, IN_SHARD_PAD), 0, lead), (1, (0, 0, dm, dm), 1, lead), (2, (0, 0, dm, dm), 2, lead),
            (3, (1, 0, 2 * dm, 2 * dm), 3, lead), (4, (0, 0, dm, dm), 4, lead)]
    b_shapes = [land(D_MODEL, IN_SHARD_PAD), land(dm, D_MODEL), land(dm, D_MODEL), land(D_MODEL, 2 * dm),
                land(dm, D_MODEL)]
    b_land = place_own(b_in, b_shapes, b_tr)
    b_send, b_recv, b_src, b_land, b_token = exchange_start(b_in, b_land, b_tr, dh1, "reduce_mix_start")

    def start_down_reduce(dwd):
        tr = ffn_tr[2:]
        tr = [(0, tr[0][1], 0, tr[0][3])]
        late['c_down'] = (exchange_start([dwd], place_own([dwd], ffn_land[2:], tr), tr, dwd, "reduce_ffn1_down_start"), tr)
        return late['c_down'][0][4]

    def start_gate_up_reduce(dwg, dwu):
        tr = ffn_tr[:2]
        late['c_gu'] = (exchange_start([dwg, dwu], place_own([dwg, dwu], ffn_land[:2], tr), tr, dwu,
                                       "reduce_ffn1_gu_start"), tr)
        return late['c_gu'][0][4]

    grad_x, _, _, _, grads['ffn1_pre_norm'], grads['ffn1_post_norm'] = _ffn_bwd(
        dh1, x, small['ffn1_pre_norm'], small['ffn1_post_norm'] + b_token[0, 0], wgu1, wd1, sv1, "ffn1",
        on_dwd=start_down_reduce, on_dwgu=start_gate_up_reduce)

    gpack = _pack(lambda nm: grads[nm], conv=dconv8[:CONV_W], loss=loss_acc[:, :1])
    gsum_parts, = exchange([gpack], [land(PACK_ROWS, D_MODEL, F32)], [(0, whole, 0, lead)], "reduce_small")
    a_got = exchange_wait(a_send, a_recv, a_src, a_land, gsum_parts, ffn_tr, "reduce_ffn2_wait")
    b_got = exchange_wait(b_send, b_recv, b_src, b_land, gsum_parts, b_tr, "reduce_mix_wait")
    recv = dict(zip(['ffn2_w_gate', 'ffn2_w_up', 'ffn2_w_down', 'w_in', 'w_out', 'mem_w_q', 'mem_w_kv', 'mem_w_o'],
                    a_got + b_got))

    out_g, out_d, out_m, out_v = {}, {}, {}, {}

    def update(nm):
        r = recv[nm]
        res = adamw(w2[nm], m2[nm], v2[nm], [(r, r.shape[2], 0, d) for d in range(N_DEV)], "adamw_" + nm)
        out_g[nm], out_d[nm], out_m[nm], out_v[nm] = res

    for nm in recv:
        update(nm)
    wp = _pack(lambda nm: small[nm])
    mp = _pack(lambda nm: m2[nm][None])
    vp = _pack(lambda nm: v2[nm][None])
    pg, pd, pm, pv = adamw(wp, mp, vp, [(gsum_parts, D_MODEL, 0, d) for d in range(N_DEV)], "adamw_small")
    for dst, p in ((out_g, pg), (out_d, pd), (out_m, pm), (out_v, pv)):
        dst.update({k: val[0] for k, val in _unpack(p).items()})
    loss = pg[ROW_MISC, COL_LOSS]
    conv_g = lax.dynamic_slice_in_dim(pg[ROW_CONV:ROW_CONV + 6].reshape(CONV_W, CONV_CH), me * (CONV_CH // N_DEV),
                                      CONV_CH // N_DEV, axis=1)
    res = adamw(w2['gdn_conv_w'], m2['gdn_conv_w'], v2['gdn_conv_w'], [conv_g], "adamw_conv")
    out_g['gdn_conv_w'], out_d['gdn_conv_w'], out_m['gdn_conv_w'], out_v['gdn_conv_w'] = res

    done = sum(out_d[nm][0, 0] for nm in recv) + out_d['gdn_conv_w'][0, 0] + pd[0, 0]
    after = jnp.zeros((8, 128), F32) + done
    c_got = []
    for key, nm in (('c_gu', "reduce_ffn1_gu_wait"), ('c_down', "reduce_ffn1_down_wait")):
        (c_send, c_recv, c_src, c_land, _), tr = late[key]
        c_got += exchange_wait(c_send, c_recv, c_src, c_land, after, tr, nm)
    recv = dict(zip(['ffn1_w_gate', 'ffn1_w_up', 'ffn1_w_down'], c_got))
    for nm in recv:
        update(nm)

    def depth(t):
        return t[None]

    return (loss, grad_x[None], *[depth(out_g[nm]) for nm in WEIGHTS], *[depth(out_d[nm]) for nm in WEIGHTS],
            *[depth(out_m[nm]) for nm in WEIGHTS], *[depth(out_v[nm]) for nm in WEIGHTS])


def kernel(x, mem, ffn1_pre_norm, ffn1_w_gate, ffn1_w_up, ffn1_w_down, ffn1_post_norm, mix_pre_norm, w_in, fox_f_bias, gdn_conv_w, gdn_a_log, gdn_dt_bias, gdn_out_norm, w_out, mix_post_norm, mem_pre_norm, mem_kv_norm, mem_w_q, mem_w_kv, mem_w_o, mem_post_norm, ffn2_pre_norm, ffn2_w_gate, ffn2_w_up, ffn2_w_down, ffn2_post_norm, loss_target, m_ffn1_pre_norm, m_ffn1_w_gate, m_ffn1_w_up, m_ffn1_w_down, m_ffn1_post_norm, m_mix_pre_norm, m_w_in, m_fox_f_bias, m_gdn_conv_w, m_gdn_a_log, m_gdn_dt_bias, m_gdn_out_norm, m_w_out, m_mix_post_norm, m_mem_pre_norm, m_mem_kv_norm, m_mem_w_q, m_mem_w_kv, m_mem_w_o, m_mem_post_norm, m_ffn2_pre_norm, m_ffn2_w_gate, m_ffn2_w_up, m_ffn2_w_down, m_ffn2_post_norm, v_ffn1_pre_norm, v_ffn1_w_gate, v_ffn1_w_up, v_ffn1_w_down, v_ffn1_post_norm, v_mix_pre_norm, v_w_in, v_fox_f_bias, v_gdn_conv_w, v_gdn_a_log, v_gdn_dt_bias, v_gdn_out_norm, v_w_out, v_mix_post_norm, v_mem_pre_norm, v_mem_kv_norm, v_mem_w_q, v_mem_w_kv, v_mem_w_o, v_mem_post_norm, v_ffn2_pre_norm, v_ffn2_w_gate, v_ffn2_w_up, v_ffn2_w_down, v_ffn2_post_norm):
    return _step(dict(locals()))
```

```python
import functools

import jax
import jax.numpy as jnp
from jax import lax
from jax.experimental import pallas as pl
from jax.experimental.pallas import tpu as pltpu

F32 = jnp.float32
BF = jnp.bfloat16
HI = lax.Precision.HIGHEST
MESH = pl.DeviceIdType.MESH

N_DEV = 8
EPS = 1e-6
D_MODEL = 1024
D_FF = 2816
FF_SHARD = D_FF // N_DEV
FF_SHARD_PAD = 384
D_FF_PAD = FF_SHARD_PAD * N_DEV
FOX_HEADS, FOX_DH = 8, 64
GDN_HEADS, GDN_DH = 4, 128
GDN_CHUNK = 64
CONV_W = 4
MEM_HEADS, MEM_DH = 4, 256
IN_W = 3600
IN_SHARD = IN_W // N_DEV
IN_SHARD_PAD = 512
PROJ_W = 4096
SMALL_F, SMALL_B, SMALL_A = 0, 8, 12

ADAM_LR, ADAM_B1, ADAM_B2, ADAM_EPS, ADAM_WD, ADAM_STEP = 0.001, 0.9, 0.999, 1e-08, 0.01, 10

VMEM_LIMIT = 56 * 1024 * 1024


def _params(sem=None):
    return pltpu.CompilerParams(dimension_semantics=sem, vmem_limit_bytes=VMEM_LIMIT)


def _tile(n, pref, unit=128):
    if n <= pref:
        return n
    t = (pref // unit) * unit
    while t > unit and n % t:
        t -= unit
    assert n % t == 0, (n, pref)
    return t


@functools.partial(jax.custom_vjp, nondiff_argnums=(2, 3))
def bdot(a, b, ca, cb):
    return lax.dot_general(a.astype(BF), b.astype(BF), (((ca,), (cb,)), ((), ())), preferred_element_type=F32)


def _bdot_fwd(a, b, ca, cb):
    return bdot(a, b, ca, cb), (a, b)


def _bdot_bwd(ca, cb, res, g):
    a, b = res
    da = bdot(g, b, 1, 1 - cb) if ca == 1 else bdot(b, g, 1 - cb, 1)
    db = bdot(a, g, 1 - ca, 0) if cb == 0 else bdot(g, a, 0, 1 - ca)
    return da, db


bdot.defvjp(_bdot_fwd, _bdot_bwd)


def hdot(a, b):
    return jnp.dot(a, b, precision=HI, preferred_element_type=F32)


def mdot(a, b):
    return jnp.dot(a, b, precision=lax.Precision.HIGH, preferred_element_type=F32)


def _iota2(shape, dim):
    return lax.broadcasted_iota(jnp.int32, shape, dim)


def _sigmoid(x):
    return 1.0 / (1.0 + jnp.exp(-x))


def _silu(x):
    return x * _sigmoid(x)


def _softplus(x):
    return jnp.maximum(x, 0.0) + jnp.log(1.0 + jnp.exp(-jnp.abs(x)))


def _rms(x, gain):
    return x * lax.rsqrt(jnp.mean(x * x, axis=-1, keepdims=True) + EPS) * gain


def mm(a, b, *, name, ta=False, tb=False, out_dtype=F32, tm=1024, tn=1024, tk=1024, token=None):
    m, k = (a.shape[1], a.shape[0]) if ta else a.shape
    n = b.shape[0] if tb else b.shape[1]
    assert k == (b.shape[1] if tb else b.shape[0]), (a.shape, b.shape, ta, tb)
    tm, tn, tk = _tile(m, tm), _tile(n, tn), _tile(k, tk)
    nk = k // tk
    dims = (((0 if ta else 1,), (1 if tb else 0,)), ((), ()))

    def kern(a_ref, b_ref, *rest):
        o_ref, scratch = (rest[1], rest[2:]) if token is not None else (rest[0], rest[1:])

        def part():
            return lax.dot_general(a_ref[...].astype(BF), b_ref[...].astype(BF), dims, preferred_element_type=F32)

        if nk == 1:
            o_ref[...] = part().astype(o_ref.dtype)
            return
        acc_ref, = scratch
        kk = pl.program_id(2)

        @pl.when(kk == 0)
        def _():
            acc_ref[...] = part()

        @pl.when(kk > 0)
        def _():
            acc_ref[...] += part()

        @pl.when(kk == nk - 1)
        def _():
            o_ref[...] = acc_ref[...].astype(o_ref.dtype)

    a_spec = pl.BlockSpec((tk, tm), lambda i, j, kk: (kk, i)) if ta else pl.BlockSpec((tm, tk), lambda i, j, kk: (i, kk))
    b_spec = pl.BlockSpec((tn, tk), lambda i, j, kk: (j, kk)) if tb else pl.BlockSpec((tk, tn), lambda i, j, kk: (kk, j))
    return pl.pallas_call(
        kern, name=name, grid=(m // tm, n // tn, nk),
        in_specs=[a_spec, b_spec] + ([pl.BlockSpec((8, 128), lambda i, j, kk: (0, 0))] if token is not None else []),
        out_specs=pl.BlockSpec((tm, tn), lambda i, j, kk: (i, j)),
        out_shape=jax.ShapeDtypeStruct((m, n), out_dtype),
        scratch_shapes=[pltpu.VMEM((tm, tn), F32)] if nk > 1 else [],
        compiler_params=_params(("parallel", "parallel", "arbitrary")),
    )(*((a, b) if token is None else (a, b, token)))


def mm_swiglu(a, wgu, *, name):
    m, k = a.shape
    nh = wgu.shape[1] // 2
    tm, tn = _tile(m, 1024), _tile(nh, 512)
    nj = nh // tn

    def kern(a_ref, bg_ref, bu_ref, g_ref, u_ref, act_ref):
        av = a_ref[...]
        g = jnp.dot(av, bg_ref[...], preferred_element_type=F32).astype(BF)
        u = jnp.dot(av, bu_ref[...], preferred_element_type=F32).astype(BF)
        g_ref[...] = g
        u_ref[...] = u
        act_ref[...] = (_silu(g.astype(F32)) * u.astype(F32)).astype(BF)

    tile = pl.BlockSpec((tm, tn), lambda i, j: (i, j))
    out = jax.ShapeDtypeStruct((m, nh), BF)
    return pl.pallas_call(
        kern, name=name, grid=(m // tm, nj),
        in_specs=[pl.BlockSpec((tm, k), lambda i, j: (i, 0)), pl.BlockSpec((k, tn), lambda i, j: (0, j)),
                  pl.BlockSpec((k, tn), lambda i, j: (0, j + nj))],
        out_specs=[tile, tile, tile], out_shape=[out, out, out],
        compiler_params=_params(("parallel", "parallel")),
    )(a, wgu, wgu)


def mm_dswiglu(df, wd, gate, up, *, name, token=None):
    m, k = df.shape
    nh = wd.shape[0]
    tm, tn = _tile(m, 1024), _tile(nh, 512)

    def kern(df_ref, wd_ref, g_ref, u_ref, *rest):
        dg_ref, du_ref = rest[-2:]
        da = lax.dot_general(df_ref[...], wd_ref[...], (((1,), (1,)), ((), ())), preferred_element_type=F32)
        g, u = g_ref[...].astype(F32), u_ref[...].astype(F32)
        sg = _sigmoid(g)
        dg_ref[...] = (da * u * (sg * (1.0 + g * (1.0 - sg)))).astype(BF)
        du_ref[...] = (da * (g * sg)).astype(BF)

    tile = pl.BlockSpec((tm, tn), lambda i, j: (i, j))
    out = jax.ShapeDtypeStruct((m, nh), BF)
    extra = [pl.BlockSpec((8, 128), lambda i, j: (0, 0))] if token is not None else []
    return pl.pallas_call(
        kern, name=name, grid=(m // tm, nh // tn),
        in_specs=[pl.BlockSpec((tm, k), lambda i, j: (i, 0)), pl.BlockSpec((tn, k), lambda i, j: (j, 0)), tile, tile]
        + extra,
        out_specs=[tile, tile], out_shape=[out, out],
        compiler_params=_params(("parallel", "parallel")),
    )(*((df, wd, gate, up) if token is None else (df, wd, gate, up, token)))


def mm_pair(a1, a2, wgu, *, name, token=None):
    m, nh = a1.shape
    n = wgu.shape[0]
    tm, tn, tk = _tile(m, 1024), _tile(n, 1024), _tile(nh, 1024)
    nk = nh // tk
    nt = (((1,), (1,)), ((), ()))

    def kern(a1_ref, a2_ref, b1_ref, b2_ref, *rest):
        o_ref, acc_ref = rest[-2:]
        kk = pl.program_id(2)

        def part():
            return (lax.dot_general(a1_ref[...], b1_ref[...], nt, preferred_element_type=F32)
                    + lax.dot_general(a2_ref[...], b2_ref[...], nt, preferred_element_type=F32))

        @pl.when(kk == 0)
        def _():
            acc_ref[...] = part()

        @pl.when(kk > 0)
        def _():
            acc_ref[...] += part()

        @pl.when(kk == nk - 1)
        def _():
            o_ref[...] = acc_ref[...]

    a_spec = pl.BlockSpec((tm, tk), lambda i, j, kk: (i, kk))
    extra = [pl.BlockSpec((8, 128), lambda i, j, kk: (0, 0))] if token is not None else []
    return pl.pallas_call(
        kern, name=name, grid=(m // tm, n // tn, nk),
        in_specs=[a_spec, a_spec, pl.BlockSpec((tn, tk), lambda i, j, kk: (j, kk)),
                  pl.BlockSpec((tn, tk), lambda i, j, kk: (j, kk + nk))] + extra,
        out_specs=pl.BlockSpec((tm, tn), lambda i, j, kk: (i, j)),
        out_shape=jax.ShapeDtypeStruct((m, n), F32),
        scratch_shapes=[pltpu.VMEM((tm, tn), F32)],
        compiler_params=_params(("parallel", "parallel", "arbitrary")),
    )(*((a1, a2, wgu, wgu) if token is None else (a1, a2, wgu, wgu, token)))


def _row_spec(item, rows):
    if not isinstance(item, tuple):
        return item, pl.BlockSpec((rows, item.shape[1]), lambda i: (i, 0))
    if len(item) == 3:
        arr, w, c = item
        return arr, pl.BlockSpec((rows, w), lambda i: (i, c))
    arr, w, c, lead = item
    return arr, pl.BlockSpec((None, rows, w), lambda i: (lead, i, c))


def _whole_spec(item):
    if not isinstance(item, tuple):
        return item, pl.BlockSpec(item.shape, lambda i: (0,) * item.ndim)
    arr, w, c = item
    return arr, pl.BlockSpec((arr.shape[0], w), lambda i: (0, c))


def rowcall(body, tiled, whole, outs, accs=(), *, rows, total, name):
    rows = min(rows, total)
    assert total % rows == 0
    t_arr, t_spec = zip(*[_row_spec(t, rows) for t in tiled])
    w_arr, w_spec = zip(*[_whole_spec(w) for w in whole]) if whole else ((), ())
    nt, nw, no, na = len(t_arr), len(w_arr), len(outs), len(accs)

    def kern(*refs):
        vals = [r[...] for r in refs[:nt + nw]]
        res = body(*vals)
        if not isinstance(res, (tuple, list)):
            res = (res,)
        assert len(res) == no + na, (name, len(res), no, na)
        for r, v in zip(refs[nt + nw:nt + nw + no], res[:no]):
            r[...] = v.astype(r.dtype)
        if na:
            acc_refs = refs[nt + nw + no:]

            @pl.when(pl.program_id(0) == 0)
            def _():
                for r in acc_refs:
                    r[...] = jnp.zeros_like(r)

            for r, v in zip(acc_refs, res[no:]):
                r[...] += v

    out_shape = [jax.ShapeDtypeStruct((total, w), d) for w, d in outs] + [jax.ShapeDtypeStruct(s, F32) for s in accs]
    out_specs = [pl.BlockSpec((rows, w), lambda i: (i, 0)) for w, _ in outs] + \
                [pl.BlockSpec(s, lambda i: (0, 0)) for s in accs]
    res = pl.pallas_call(
        kern, name=name, grid=(total // rows,),
        in_specs=list(t_spec) + list(w_spec), out_specs=out_specs, out_shape=out_shape,
        compiler_params=_params(("arbitrary",) if na else ("parallel",)),
    )(*t_arr, *w_arr)
    return res


def _colsum(x):
    return jnp.sum(x, axis=0, keepdims=True)


GDN_UNROLL = 4


def _gdn_chunk(q, k, v, z, gb, bb, state, gain, with_starts=False):
    c = GDN_CHUNK
    nh = len(q)
    hs = range(nh)
    r64, c64 = _iota2((c, c), 0), _iota2((c, c), 1)
    incl = r64 >= c64
    strict = r64 > c64
    ltri = incl.astype(F32)
    eye = (r64 == c64).astype(F32)
    pick = (_iota2((GDN_DH, c), 0) == _iota2((GDN_DH, c), 1)).astype(F32)
    last = (_iota2((c, GDN_DH), 0) == c - 1).astype(F32)

    qn = [q[h] * lax.rsqrt(jnp.sum(q[h] * q[h], axis=-1, keepdims=True) + EPS) * (GDN_DH ** -0.5) for h in hs]
    kn = [k[h] * lax.rsqrt(jnp.sum(k[h] * k[h], axis=-1, keepdims=True) + EPS) for h in hs]
    gc = [mdot(ltri, gb[h]) for h in hs]
    gcol = [mdot(gc[h], pick) for h in hs]
    dec = [jnp.exp(jnp.where(incl, gcol[h] - gcol[h].T, -1e30)) for h in hs]
    kb = [kn[h] * bb[h] for h in hs]
    vb = [v[h] * bb[h] for h in hs]
    kk = [bdot(kb[h], kn[h], 1, 1) for h in hs]
    p = [-jnp.where(strict, kk[h] * dec[h], 0.0) for h in hs]
    tinv = [eye + p[h] for h in hs]
    for level in range(5):
        dot = mdot if level < 2 else (lambda a, b: bdot(a, b, 1, 0))
        p = [dot(p[h], p[h]) for h in hs]
        tinv = [tinv[h] + dot(tinv[h], p[h]) for h in hs]
    egc = [jnp.exp(gc[h]) for h in hs]
    u = [mdot(tinv[h], vb[h]) for h in hs]
    w = [mdot(tinv[h], kb[h] * egc[h]) for h in hs]
    attn = [bdot(qn[h], kn[h], 1, 1) * dec[h] for h in hs]
    qd = [qn[h] * egc[h] for h in hs]
    gl = [jnp.sum(gc[h] * last, axis=0, keepdims=True) for h in hs]
    kt = [kn[h] * jnp.exp(gl[h] - gc[h]) for h in hs]
    nst = len(state)
    st, o, mids = list(state), [None] * nh, []
    for c0 in range(0, nh, nst):
        us = range(c0, c0 + nst)
        mids.append(tuple(st))
        ws = [bdot(w[h], st[h - c0], 1, 0) for h in us]
        qs = [bdot(qd[h], st[h - c0], 1, 0) for h in us]
        v_new = [u[h] - ws[h - c0] for h in us]
        av = [bdot(attn[h], v_new[h - c0], 1, 0) for h in us]
        kv = [bdot(kt[h], v_new[h - c0], 0, 0) for h in us]
        st = [st[h - c0] * jnp.exp(gl[h]) + kv[h - c0] for h in us]
        for h in us:
            o[h] = _rms(qs[h - c0] + av[h - c0], gain) * _silu(z[h])
    if with_starts:
        return tuple(o), tuple(st), tuple(mids)
    return tuple(o), tuple(st)


GDN_ROWS = 512
GDN_W = GDN_HEADS * GDN_DH


def gdn_fwd(cqkv, proj, gbb, gain):
    s = cqkv.shape[0]
    nb, cpb = s // GDN_ROWS, GDN_ROWS // GDN_CHUNK
    h4 = GDN_HEADS

    def kern(qkv_ref, z_ref, gb_ref, gain_ref, o_ref, st_ref, state):
        @pl.when(pl.program_id(0) == 0)
        def _():
            state[...] = jnp.zeros_like(state)

        gain_v = gain_ref[...]

        def step(ci, carry):
            sls = [pl.ds(pl.multiple_of((ci * GDN_UNROLL + c) * GDN_CHUNK, GDN_CHUNK), GDN_CHUNK)
                   for c in range(GDN_UNROLL)]
            ins = []
            for sl in sls:
                for h in range(h4):
                    ln = lambda base, h=h: slice(base + h * GDN_DH, base + (h + 1) * GDN_DH)
                    ins.append((qkv_ref[sl, ln(0)], qkv_ref[sl, ln(GDN_W)], qkv_ref[sl, ln(2 * GDN_W)],
                                z_ref[sl, ln(0)], gb_ref[sl, ln(0)], gb_ref[sl, ln(GDN_W)]))
            cols = [tuple(col) for col in zip(*ins)]
            o, new, starts = _gdn_chunk(*cols, tuple(state[h] for h in range(h4)), gain_v, with_starts=True)
            for c, sl in enumerate(sls):
                for h in range(h4):
                    st_ref[h, ci * GDN_UNROLL + c] = starts[c][h]
                    o_ref[sl, h * GDN_DH:(h + 1) * GDN_DH] = o[c * h4 + h]
            for h in range(h4):
                state[h] = new[h]
            return carry

        lax.fori_loop(0, cpb // GDN_UNROLL, step, 0)

    return pl.pallas_call(
        kern, name="gdn_fwd", grid=(nb,),
        in_specs=[pl.BlockSpec((GDN_ROWS, 3 * GDN_W), lambda i: (i, 0)),
                  pl.BlockSpec((GDN_ROWS, GDN_W), lambda i: (i, 6)),
                  pl.BlockSpec((GDN_ROWS, 2 * GDN_W), lambda i: (i, 0)),
                  pl.BlockSpec((1, GDN_DH), lambda i: (0, 0))],
        out_specs=[pl.BlockSpec((GDN_ROWS, GDN_W), lambda i: (i, 0)),
                   pl.BlockSpec((h4, cpb, GDN_DH, GDN_DH), lambda i: (0, i, 0, 0))],
        out_shape=[jax.ShapeDtypeStruct((s, GDN_W), F32),
                   jax.ShapeDtypeStruct((h4, s // GDN_CHUNK, GDN_DH, GDN_DH), F32)],
        scratch_shapes=[pltpu.VMEM((h4, GDN_DH, GDN_DH), F32)],
        compiler_params=_params(("arbitrary",)),
    )(cqkv, proj, gbb, gain)


def gdn_bwd(cqkv, proj, gbb, gain, states, d_mixed):
    s = cqkv.shape[0]
    nb, cpb = s // GDN_ROWS, GDN_ROWS // GDN_CHUNK
    h4 = GDN_HEADS

    def kern(qkv_ref, z_ref, gb_ref, gain_ref, st_ref, do_ref, dqkv_ref, dz_ref, dgb_ref, dgain_ref, dstate):
        @pl.when(pl.program_id(0) == 0)
        def _():
            dgain_ref[...] = jnp.zeros_like(dgain_ref)
            dstate[...] = jnp.zeros_like(dstate)

        gain_v = gain_ref[...]

        def step(t, carry):
            first = (cpb // GDN_UNROLL - 1 - t) * GDN_UNROLL
            sls = [pl.ds(pl.multiple_of((first + c) * GDN_CHUNK, GDN_CHUNK), GDN_CHUNK) for c in range(GDN_UNROLL)]
            prim, cot = [], []
            for sl in sls:
                for h in range(h4):
                    ln = lambda base, h=h: slice(base + h * GDN_DH, base + (h + 1) * GDN_DH)
                    prim.append((qkv_ref[sl, ln(0)], qkv_ref[sl, ln(GDN_W)], qkv_ref[sl, ln(2 * GDN_W)],
                                 z_ref[sl, ln(0)], gb_ref[sl, ln(0)], gb_ref[sl, ln(GDN_W)]))
                    cot.append(do_ref[sl, ln(0)])
            cols = [tuple(col) for col in zip(*prim)]
            st_in = tuple(st_ref[h, first] for h in range(h4))
            vjp = jax.vjp(_gdn_chunk, *cols, st_in, gain_v)[1]
            dq, dk, dv, dz, dg, db, dst, dgn = vjp((tuple(cot), tuple(dstate[h] for h in range(h4))))
            for c, sl in enumerate(sls):
                for h in range(h4):
                    ln = lambda base, h=h: slice(base + h * GDN_DH, base + (h + 1) * GDN_DH)
                    unit = c * h4 + h
                    dqkv_ref[sl, ln(0)] = dq[unit]
                    dqkv_ref[sl, ln(GDN_W)] = dk[unit]
                    dqkv_ref[sl, ln(2 * GDN_W)] = dv[unit]
                    dz_ref[sl, ln(0)] = dz[unit]
                    dgb_ref[sl, ln(0)] = dg[unit]
                    dgb_ref[sl, ln(GDN_W)] = db[unit]
            for h in range(h4):
                dstate[h] = dst[h]
            dgain_ref[...] += dgn
            return carry

        lax.fori_loop(0, cpb // GDN_UNROLL, step, 0)

    def rev(width, cblock=0):
        return pl.BlockSpec((GDN_ROWS, width), lambda i: (nb - 1 - i, cblock))

    return pl.pallas_call(
        kern, name="gdn_bwd", grid=(nb,),
        in_specs=[rev(3 * GDN_W), rev(GDN_W, 6), rev(2 * GDN_W), pl.BlockSpec((1, GDN_DH), lambda i: (0, 0)),
                  pl.BlockSpec((h4, cpb, GDN_DH, GDN_DH), lambda i: (0, nb - 1 - i, 0, 0)), rev(GDN_W, 1)],
        out_specs=[rev(3 * GDN_W), rev(GDN_W), rev(2 * GDN_W), pl.BlockSpec((1, GDN_DH), lambda i: (0, 0))],
        out_shape=[jax.ShapeDtypeStruct((s, 3 * GDN_W), F32), jax.ShapeDtypeStruct((s, GDN_W), F32),
                   jax.ShapeDtypeStruct((s, 2 * GDN_W), F32), jax.ShapeDtypeStruct((1, GDN_DH), F32)],
        scratch_shapes=[pltpu.VMEM((h4, GDN_DH, GDN_DH), F32)],
        compiler_params=_params(("arbitrary",)),
    )(cqkv, proj, gbb, gain, states, d_mixed)


def _gdn_gates(small, prm):
    w = GDN_HEADS * GDN_DH
    lane, head = _iota2((128, w), 0), _iota2((128, w), 1) // GDN_DH
    sel_b = (lane == SMALL_B + head).astype(F32)
    sel_a = (lane == SMALL_A + head).astype(F32)
    prow = _iota2((8, 128), 0)
    a_log = jnp.sum(prm * (prow == 0).astype(F32), axis=0, keepdims=True)
    dt_b = jnp.sum(prm * (prow == 1).astype(F32), axis=0, keepdims=True)
    beta = _sigmoid(hdot(small, sel_b))
    g = hdot(-jnp.exp(a_log) * _softplus(small + dt_b), sel_a)
    return g, beta


CONV_ROWS = 1024
CONV_COLS = 128
CONV_BLOCK0 = 1536 // CONV_COLS


def _shift_down(prev8, cur, s):
    ext = jnp.concatenate([prev8, cur], axis=0)
    return pltpu.roll(ext, s, 0)[8:]


def _shift_up(cur, next8, s):
    n = cur.shape[0]
    ext = jnp.concatenate([cur, next8], axis=0)
    return pltpu.roll(ext, n + 8 - s, 0)[:n]


def _conv_pre(x_ref, w, ci, nchunk):
    r0 = pl.multiple_of(ci * CONV_ROWS, CONV_ROWS)
    cur = x_ref[pl.ds(r0, CONV_ROWS), :]
    prev = x_ref[pl.ds(pl.multiple_of(jnp.maximum(r0 - 8, 0), 8), 8), :]
    prev = jnp.where(ci > 0, prev, 0.0)
    shifted = [cur] + [_shift_down(prev, cur, s) for s in range(1, CONV_W)]
    pre = w[CONV_W - 1:CONV_W, :] * cur
    for s in range(1, CONV_W):
        pre = pre + w[CONV_W - 1 - s:CONV_W - s, :] * shifted[s]
    return r0, pre, shifted


def conv_fwd(proj, conv_w8):
    s = proj.shape[0]
    nchunk = s // CONV_ROWS
    ncol = 3 * GDN_HEADS * GDN_DH // CONV_COLS

    def kern(x_ref, w_ref, y_ref):
        w = w_ref[...]

        def step(ci, carry):
            r0, pre, _ = _conv_pre(x_ref, w, ci, nchunk)
            y_ref[pl.ds(r0, CONV_ROWS), :] = _silu(pre)
            return carry

        lax.fori_loop(0, nchunk, step, 0)

    return pl.pallas_call(
        kern, name="conv_fwd", grid=(ncol,),
        in_specs=[pl.BlockSpec((s, CONV_COLS), lambda j: (0, CONV_BLOCK0 + j)),
                  pl.BlockSpec((8, CONV_COLS), lambda j: (0, j))],
        out_specs=pl.BlockSpec((s, CONV_COLS), lambda j: (0, j)),
        out_shape=jax.ShapeDtypeStruct((s, ncol * CONV_COLS), F32),
        compiler_params=_params(("parallel",)),
    )(proj, conv_w8)


def conv_bwd(proj, conv_w8, dy):
    s = proj.shape[0]
    nchunk = s // CONV_ROWS
    per = 3 * GDN_HEADS * GDN_DH // CONV_COLS
    outs = []
    for part in range(1):
        def kern(x_ref, w_ref, dy_ref, dx_ref, dw_ref, dpre_ref):
            w = w_ref[...]
            rows8 = _iota2((8, CONV_COLS), 0)

            def step1(ci, dw):
                r0, pre, shifted = _conv_pre(x_ref, w, ci, nchunk)
                sg = _sigmoid(pre)
                dpre = dy_ref[pl.ds(r0, CONV_ROWS), :] * sg * (1.0 + pre * (1.0 - sg))
                dpre_ref[pl.ds(r0, CONV_ROWS), :] = dpre
                for sh in range(CONV_W):
                    dw = dw + jnp.where(rows8 == CONV_W - 1 - sh, _colsum(dpre * shifted[sh]), 0.0)
                return dw

            dw_ref[...] = lax.fori_loop(0, nchunk, step1, jnp.zeros((8, CONV_COLS), F32))

            def step2(ci, carry):
                r0 = pl.multiple_of(ci * CONV_ROWS, CONV_ROWS)
                cur = dpre_ref[pl.ds(r0, CONV_ROWS), :]
                nxt = dpre_ref[pl.ds(pl.multiple_of(jnp.minimum(r0 + CONV_ROWS, s - 8), 8), 8), :]
                nxt = jnp.where(ci < nchunk - 1, nxt, 0.0)
                dx = w[CONV_W - 1:CONV_W, :] * cur
                for sh in range(1, CONV_W):
                    dx = dx + w[CONV_W - 1 - sh:CONV_W - sh, :] * _shift_up(cur, nxt, sh)
                dx_ref[pl.ds(r0, CONV_ROWS), :] = dx
                return carry

            lax.fori_loop(0, nchunk, step2, 0)

        outs.append(pl.pallas_call(
            kern, name=f"conv_bwd{part}", grid=(per,),
            in_specs=[pl.BlockSpec((s, CONV_COLS), lambda j, part=part: (0, CONV_BLOCK0 + part * per + j)),
                      pl.BlockSpec((8, CONV_COLS), lambda j, part=part: (0, part * per + j)),
                      pl.BlockSpec((s, CONV_COLS), lambda j: (0, j))],
            out_specs=[pl.BlockSpec((s, CONV_COLS), lambda j: (0, j)),
                       pl.BlockSpec((8, CONV_COLS), lambda j: (0, j))],
            out_shape=[jax.ShapeDtypeStruct((s, per * CONV_COLS), F32),
                       jax.ShapeDtypeStruct((8, per * CONV_COLS), F32)],
            scratch_shapes=[pltpu.VMEM((s, CONV_COLS), F32)],
            compiler_params=_params(("parallel",)),
        )(proj, conv_w8, dy))
    dx = jnp.concatenate([o[0] for o in outs], axis=1)
    dw = jnp.concatenate([o[1] for o in outs], axis=1)
    return dx, dw


FOXF_ROWS = 512
SMALL_BLOCK128 = 3584 // 128


def _log_sigmoid(x):
    return jnp.minimum(x, 0.0) - jnp.log(1.0 + jnp.exp(-jnp.abs(x)))


def fox_f_fwd(proj, bias_row):
    s = proj.shape[0]
    n = s // FOXF_ROWS

    def kern(x_ref, b_ref, f_ref, carry):
        @pl.when(pl.program_id(0) == 0)
        def _():
            carry[...] = jnp.zeros_like(carry)

        heads = _iota2((FOXF_ROWS, 128), 1) < FOX_HEADS
        lf = jnp.where(heads, _log_sigmoid(x_ref[...] + b_ref[...]), 0.0)
        ltri = (_iota2((FOXF_ROWS, FOXF_ROWS), 0) >= _iota2((FOXF_ROWS, FOXF_ROWS), 1)).astype(F32)
        c = hdot(ltri, lf) + carry[...]
        f_ref[...] = c
        carry[...] = c[FOXF_ROWS - 1:FOXF_ROWS, :]

    return pl.pallas_call(
        kern, name="fox_f_fwd", grid=(n,),
        in_specs=[pl.BlockSpec((FOXF_ROWS, 128), lambda i: (i, SMALL_BLOCK128)),
                  pl.BlockSpec((1, 128), lambda i: (0, 0))],
        out_specs=pl.BlockSpec((FOXF_ROWS, 128), lambda i: (i, 0)),
        out_shape=jax.ShapeDtypeStruct((s, 128), F32),
        scratch_shapes=[pltpu.VMEM((1, 128), F32)],
        compiler_params=_params(("arbitrary",)),
    )(proj, bias_row)


def fox_f_bwd(proj, bias_row, d_f):
    s = proj.shape[0]
    n = s // FOXF_ROWS

    def kern(x_ref, b_ref, df_ref, dx_ref, db_ref, carry):
        @pl.when(pl.program_id(0) == 0)
        def _():
            carry[...] = jnp.zeros_like(carry)
            db_ref[...] = jnp.zeros_like(db_ref)

        heads = _iota2((FOXF_ROWS, 128), 1) < FOX_HEADS
        utri = (_iota2((FOXF_ROWS, FOXF_ROWS), 0) <= _iota2((FOXF_ROWS, FOXF_ROWS), 1)).astype(F32)
        rc = hdot(utri, df_ref[...]) + carry[...]
        carry[...] = rc[0:1, :]
        dx = jnp.where(heads, rc * _sigmoid(-(x_ref[...] + b_ref[...])), 0.0)
        dx_ref[...] = dx
        db_ref[...] += _colsum(dx)

    return pl.pallas_call(
        kern, name="fox_f_bwd", grid=(n,),
        in_specs=[pl.BlockSpec((FOXF_ROWS, 128), lambda i: (n - 1 - i, SMALL_BLOCK128)),
                  pl.BlockSpec((1, 128), lambda i: (0, 0)),
                  pl.BlockSpec((FOXF_ROWS, 128), lambda i: (n - 1 - i, 0))],
        out_specs=[pl.BlockSpec((FOXF_ROWS, 128), lambda i: (n - 1 - i, 0)),
                   pl.BlockSpec((1, 128), lambda i: (0, 0))],
        out_shape=[jax.ShapeDtypeStruct((s, 128), F32), jax.ShapeDtypeStruct((1, 128), F32)],
        scratch_shapes=[pltpu.VMEM((1, 128), F32)],
        compiler_params=_params(("arbitrary",)),
    )(proj, bias_row, d_f)


FOX_T = 512
FOX_SCALE = FOX_DH ** -0.5
FOX_PAIRS = FOX_HEADS // 2
NEG = -1e30
_NT = (((1,), (1,)), ((), ()))


def _split3(x):
    def bf(v):
        return lax.reduce_precision(v, exponent_bits=8, mantissa_bits=7)

    hi = bf(x)
    mid = bf(x - hi)
    lo = bf(x - hi - mid)
    return jnp.stack([hi, mid, lo], axis=-1)


def _fox_extras(s, first, second):
    def part(v):
        if v is None:
            return jnp.zeros((s, FOX_HEADS, 3), F32)
        if isinstance(v, float):
            return jnp.full((s, FOX_HEADS, 3), v, F32)
        pairs = v.reshape(s, FOX_PAIRS, 2)
        return _split3(jnp.stack([pairs[:, :, 1], pairs[:, :, 0]], axis=-1).reshape(s, FOX_HEADS))

    cols = jnp.concatenate([part(first), part(second)], axis=-1)
    cols = _pad_to(cols, (s, FOX_HEADS, FOX_DH)).reshape(s, FOX_PAIRS, 2 * FOX_DH)
    return cols.transpose(1, 0, 2).astype(BF)


def _head_masks(rows):
    lane = _iota2((rows, 2 * FOX_DH), 1)
    return lane < FOX_DH, lane >= FOX_DH


def _extra_lane(e, slot):
    return (FOX_DH if e == 0 else 0) + slot


def fox_fwd(qkv, xq, xk, xv):
    s = qkv.shape[0]
    t = min(FOX_T, s)
    n = s // t

    def kern(q_ref, k_ref, v_ref, xq_ref, xk_ref, xv_ref, o_ref, lse_ref):
        i = pl.program_id(1)
        masks = _head_masks(t)
        q_pair, x_pair = q_ref[...] * FOX_SCALE, xq_ref[...]
        q_ops = [jnp.where(mk, q_pair, x_pair) for mk in masks]

        def step(j, carry, masked):
            sl = pl.ds(pl.multiple_of(j * t, t), t)
            k_pair, xk_pair, v_pair, xv_pair = k_ref[sl, :], xk_ref[sl, :], v_ref[sl, :], xv_ref[sl, :]
            k_ops = [jnp.where(mk, k_pair, xk_pair) for mk in masks]
            v_ops = [jnp.where(mk, v_pair, xv_pair) for mk in masks]
            sc = [lax.dot_general(q_ops[e], k_ops[e], _NT, preferred_element_type=F32) for e in range(2)]
            if masked:
                keep = _iota2((t, t), 0) >= _iota2((t, t), 1)
                sc = [jnp.where(keep, x, NEG) for x in sc]
            m_new = [jnp.maximum(carry[e][0], jnp.max(sc[e], axis=1, keepdims=True)) for e in range(2)]
            p = [jnp.exp(sc[e] - m_new[e]).astype(BF) for e in range(2)]
            pv = [jnp.dot(p[e], v_ops[e], preferred_element_type=F32) for e in range(2)]
            return tuple((m_new[e], jnp.exp(carry[e][0] - m_new[e]) * carry[e][1] + pv[e]) for e in range(2))

        init = tuple((jnp.full((t, 1), NEG, F32), jnp.zeros((t, 2 * FOX_DH), F32)) for _ in range(2))
        carry = lax.fori_loop(0, i, lambda j, c: step(j, c, False), init)
        carry = step(i, carry, True)
        lane = _iota2((t, 2 * FOX_DH), 1)
        outs, lses = [], []
        for e in range(2):
            m, acc = carry[e]
            l = jnp.sum(jnp.where(lane == _extra_lane(e, 0), acc, 0.0), axis=1, keepdims=True)
            outs.append(acc / l)
            lses.append(m + jnp.log(l))
        o_ref[...] = jnp.where(masks[0], outs[0], outs[1])
        head0 = 2 * pl.program_id(0)
        lse_ref[...] = jnp.where(lane == head0, lses[0], jnp.where(lane == head0 + 1, lses[1], 0.0))

    pr = FOX_PAIRS
    return pl.pallas_call(
        kern, name="fox_fwd", grid=(pr, n),
        in_specs=[pl.BlockSpec((t, 128), lambda p, i: (i, p)),
                  pl.BlockSpec((s, 128), lambda p, i: (0, pr + p)),
                  pl.BlockSpec((s, 128), lambda p, i: (0, 2 * pr + p)),
                  pl.BlockSpec((None, t, 128), lambda p, i: (p, i, 0)),
                  pl.BlockSpec((None, s, 128), lambda p, i: (p, 0, 0)),
                  pl.BlockSpec((None, s, 128), lambda p, i: (p, 0, 0))],
        out_specs=[pl.BlockSpec((t, 128), lambda p, i: (i, p)),
                   pl.BlockSpec((None, t, 128), lambda p, i: (p, i, 0))],
        out_shape=[jax.ShapeDtypeStruct((s, FOX_HEADS * FOX_DH), F32), jax.ShapeDtypeStruct((pr, s, 128), F32)],
        compiler_params=_params(("parallel", "parallel")),
    )(qkv, qkv, qkv, xq, xk, xv)


def fox_bwd(qkv, d_o, xk, xv, xqb, xdo):
    s = qkv.shape[0]
    t = min(FOX_T, s)
    n = s // t
    w = 2 * FOX_DH

    def both(blocks, slot):
        lane = _iota2(blocks[0].shape, 1)
        head0 = 2 * pl.program_id(0)
        own = jnp.where(lane < FOX_DH, blocks[0], blocks[1])
        sums = [jnp.sum(jnp.where(lane == _extra_lane(e, slot), blocks[e], 0.0), axis=1, keepdims=True)
                for e in range(2)]
        return own, jnp.where(lane == head0, sums[0], jnp.where(lane == head0 + 1, sums[1], 0.0))

    def kern(k_ref, v_ref, xk_ref, xv_ref, q_ref, do_ref, xq_ref, xd_ref,
             dq_ref, dk_ref, dv_ref, sq_ref, sk_ref, dq_acc):
        j = pl.program_id(1)

        @pl.when(j == 0)
        def _():
            dq_acc[...] = jnp.zeros_like(dq_acc)

        masks = _head_masks(t)
        k_ops = [jnp.where(mk, k_ref[...], xk_ref[...]) for mk in masks]
        v_ops = [jnp.where(mk, v_ref[...], xv_ref[...]) for mk in masks]
        k_t = [x.T for x in k_ops]

        def step(i, carry, masked):
            dk, dv = carry
            sl = pl.ds(pl.multiple_of(i * t, t), t)
            q_pair, xq_pair, do_pair, xd_pair = q_ref[sl, :] * FOX_SCALE, xq_ref[sl, :], do_ref[sl, :], xd_ref[sl, :]
            q_ops = [jnp.where(mk, q_pair, xq_pair) for mk in masks]
            do_ops = [jnp.where(mk, do_pair, xd_pair) for mk in masks]
            q_t = [x.T for x in q_ops]
            do_t = [jnp.where(mk, do_pair, 0).astype(BF).T for mk in masks]
            st = [lax.dot_general(k_ops[e], q_ops[e], _NT, preferred_element_type=F32) for e in range(2)]
            dp = [lax.dot_general(v_ops[e], do_ops[e], _NT, preferred_element_type=F32) for e in range(2)]
            if masked:
                keep = _iota2((t, t), 0) <= _iota2((t, t), 1)
                st = [jnp.where(keep, x, NEG) for x in st]
            pt = [jnp.exp(x) for x in st]
            dsb = [(pt[e] * dp[e]).astype(BF) for e in range(2)]
            dv = dv + sum(lax.dot_general(do_t[e], pt[e].astype(BF), _NT, preferred_element_type=F32)
                          for e in range(2))
            dk = tuple(dk[e] + lax.dot_general(q_t[e], dsb[e], _NT, preferred_element_type=F32) for e in range(2))
            for e in range(2):
                dq_acc[i, e * w:(e + 1) * w, :] += jnp.dot(k_t[e], dsb[e], preferred_element_type=F32)
            return dk, dv

        init = ((jnp.zeros((w, t), F32), jnp.zeros((w, t), F32)), jnp.zeros((w, t), F32))
        carry = step(j, init, True)
        dk, dv = lax.fori_loop(j + 1, n, lambda i, c: step(i, c, False), carry)
        dk_ref[...], sk_ref[...] = both([x.T for x in dk], 3)
        dv_ref[...] = dv.T

        @pl.when(j == n - 1)
        def _():
            def out(r, carry):
                sl = pl.ds(pl.multiple_of(r * t, t), t)
                own, sums = both([dq_acc[r, e * w:(e + 1) * w, :].T for e in range(2)], 0)
                dq_ref[sl, :] = own * FOX_SCALE
                sq_ref[sl, :] = sums
                return carry

            lax.fori_loop(0, n, out, 0)

    pr = FOX_PAIRS
    flat = jax.ShapeDtypeStruct((s, FOX_HEADS * FOX_DH), F32)
    tile = pl.BlockSpec((t, 128), lambda p, j: (j, p))
    whole = pl.BlockSpec((s, 128), lambda p, j: (0, p))
    return pl.pallas_call(
        kern, name="fox_bwd", grid=(pr, n),
        in_specs=[pl.BlockSpec((t, 128), lambda p, j: (j, pr + p)),
                  pl.BlockSpec((t, 128), lambda p, j: (j, 2 * pr + p)),
                  pl.BlockSpec((None, t, 128), lambda p, j: (p, j, 0)),
                  pl.BlockSpec((None, t, 128), lambda p, j: (p, j, 0)),
                  whole, whole,
                  pl.BlockSpec((None, s, 128), lambda p, j: (p, 0, 0)),
                  pl.BlockSpec((None, s, 128), lambda p, j: (p, 0, 0))],
        out_specs=[whole, tile, tile, whole, tile],
        out_shape=[flat] * 5,
        scratch_shapes=[pltpu.VMEM((n, 2 * w, t), F32)],
        compiler_params=_params(("parallel", "arbitrary")),
    )(qkv, qkv, xk, xv, qkv, d_o, xqb, xdo)


def _xattn_head(q, k, v):
    sc = bdot(q, k, 1, 1) * (MEM_DH ** -0.5)
    e = jnp.exp(sc - lax.stop_gradient(jnp.max(sc, axis=-1, keepdims=True)))
    p = e / jnp.sum(e, axis=-1, keepdims=True)
    return bdot(p, v, 1, 0)


def xattn_fwd(q, kv):
    s = q.shape[0]
    hh = MEM_HEADS

    def body(*vals):
        qs, ks, vs = vals[:hh], vals[hh:2 * hh], vals[2 * hh:]
        return jnp.concatenate([_xattn_head(qs[a], ks[a], vs[a]) for a in range(hh)], axis=1)

    return rowcall(body, [(q, MEM_DH, a) for a in range(hh)],
                   [(kv, MEM_DH, a) for a in range(2 * hh)],
                   [(hh * MEM_DH, BF)], rows=512, total=s, name="xattn_fwd")[0]


def xattn_bwd(q, kv, d_o):
    s = q.shape[0]
    hh = MEM_HEADS

    def body(*vals):
        qs, dos = vals[:hh], vals[hh:2 * hh]
        ks, vs = vals[2 * hh:3 * hh], vals[3 * hh:]
        dqs, dks, dvs = [], [], []
        for a in range(hh):
            _, vjp = jax.vjp(_xattn_head, qs[a], ks[a], vs[a])
            dq, dk, dv = vjp(dos[a])
            dqs.append(dq)
            dks.append(dk)
            dvs.append(dv)
        return jnp.concatenate(dqs, axis=1), jnp.concatenate(dks + dvs, axis=1)

    return rowcall(body, [(q, MEM_DH, a) for a in range(hh)] + [(d_o, MEM_DH, a) for a in range(hh)],
                   [(kv, MEM_DH, a) for a in range(2 * hh)],
                   [(hh * MEM_DH, BF)], [kv.shape], rows=512, total=s, name="xattn_bwd")


def _slab(ref, axis, start, size):
    if axis is None:
        return ref
    if axis == "lead":
        return ref.at[start]
    idx = pl.ds(pl.multiple_of(start, 128 if axis == 1 else 16), size)
    return ref.at[idx] if axis == 0 else ref.at[:, idx]


def exchange(inputs, outputs, transfers, name):
    ni, no, nt = len(inputs), len(outputs), len(transfers)
    npeer = N_DEV - 1

    def body(*refs):
        ins, outs = refs[:ni], refs[ni:ni + no]
        send, recv, loc = refs[ni + no:]
        x, y, c = lax.axis_index("x"), lax.axis_index("y"), lax.axis_index("c")
        me = 4 * x + 2 * y + c

        def peer(p):
            px = 1 - x if p & 4 else x
            py = 1 - y if p & 2 else y
            pc = 1 - c if p & 1 else c
            return (px, py, pc), 4 * px + 2 * py + pc

        def view(ref, spec, who):
            axis, off, stride, size = spec
            return _slab(ref, axis, off + who * stride, size)

        local, remote = [], []
        for w, (ii, src, oi, dst) in enumerate(transfers):
            cp = pltpu.make_async_copy(view(ins[ii], src, me), view(outs[oi], dst, me), loc.at[w])
            cp.start()
            local.append(cp)
        for p in range(1, N_DEV):
            dev, idx = peer(p)
            for w, (ii, src, oi, dst) in enumerate(transfers):
                k = w * npeer + p - 1
                out_cp = pltpu.make_async_remote_copy(
                    src_ref=view(ins[ii], src, idx), dst_ref=view(outs[oi], dst, me), send_sem=send.at[k],
                    recv_sem=recv.at[k], device_id=dev, device_id_type=MESH)
                out_cp.start()
                in_cp = pltpu.make_async_remote_copy(
                    src_ref=view(ins[ii], src, idx), dst_ref=view(outs[oi], dst, idx), send_sem=send.at[k],
                    recv_sem=recv.at[k], device_id=dev, device_id_type=MESH)
                remote.append((out_cp, in_cp))
        for out_cp, in_cp in remote:
            in_cp.wait_recv()
            out_cp.wait_send()
        for cp in local:
            cp.wait()

    hbm = pl.BlockSpec(memory_space=pl.ANY)
    return pl.pallas_call(
        body, name=name, in_specs=[hbm] * ni, out_specs=[hbm] * no, out_shape=list(outputs),
        scratch_shapes=[pltpu.SemaphoreType.DMA((nt * npeer,)), pltpu.SemaphoreType.DMA((nt * npeer,)),
                        pltpu.SemaphoreType.DMA((nt,))],
        compiler_params=pltpu.CompilerParams(has_side_effects=True),
    )(*inputs)


def _peer(p):
    x, y, c = lax.axis_index("x"), lax.axis_index("y"), lax.axis_index("c")
    px = 1 - x if p & 4 else x
    py = 1 - y if p & 2 else y
    pc = 1 - c if p & 1 else c
    return (px, py, pc), 4 * px + 2 * py + pc


def _view(ref, spec, who):
    axis, off, stride, size = spec
    return _slab(ref, axis, off + who * stride, size)


def place_own(inputs, outputs, transfers):
    me = 4 * lax.axis_index("x") + 2 * lax.axis_index("y") + lax.axis_index("c")
    lands = [lax.empty(o.shape, o.dtype) for o in outputs]
    for ii, src, oi, dst in transfers:
        axis, off, stride, size = src
        own = inputs[ii] if axis is None else lax.dynamic_slice_in_dim(inputs[ii], off + me * stride, size, axis)
        axis, off, stride, size = dst
        if axis == "lead":
            lands[oi] = lax.dynamic_update_slice_in_dim(lands[oi], own[None], me, 0)
        else:
            lands[oi] = lax.dynamic_update_slice_in_dim(lands[oi], own, off + me * stride, axis)
    return lands


_HBM = pl.BlockSpec(memory_space=pltpu.HBM)
_SEM = pl.BlockSpec(memory_space=pltpu.SEMAPHORE)
_EFFECT = pltpu.SideEffectType.DATAFLOW_SIDE_EFFECTING


def _remote_copies(ins, lands, transfers, send, recv):
    npeer = N_DEV - 1
    me = 4 * lax.axis_index("x") + 2 * lax.axis_index("y") + lax.axis_index("c")
    pairs = []
    for p in range(1, N_DEV):
        dev, idx = _peer(p)
        for w, (ii, src, oi, dst) in enumerate(transfers):
            k = w * npeer + p - 1
            common = dict(src_ref=_view(ins[ii], src, idx), send_sem=send.at[k], recv_sem=recv.at[k],
                          device_id=dev, device_id_type=MESH)
            pairs.append((pltpu.make_async_remote_copy(dst_ref=_view(lands[oi], dst, me), **common),
                          pltpu.make_async_remote_copy(dst_ref=_view(lands[oi], dst, idx), **common)))
    return pairs


def exchange_start(inputs, lands, transfers, after, name):
    ni, nl, nsem = len(inputs), len(lands), len(transfers) * (N_DEV - 1)

    def body(*refs):
        ins, lnd = refs[:ni], refs[ni:ni + nl]
        send, recv = refs[ni + nl + 1], refs[ni + nl + 2]
        token = refs[-1]
        for out_cp, _ in _remote_copies(ins, lnd, transfers, send, recv):
            out_cp.start()
        token[...] = jnp.zeros_like(token)

    args = [pltpu.with_memory_space_constraint(a, pltpu.HBM) for a in list(inputs) + list(lands)]
    res = pl.pallas_call(
        body, name=name,
        out_shape=(pltpu.SemaphoreType.DMA((nsem,)), pltpu.SemaphoreType.DMA((nsem,)),
                   *[pltpu.HBM(a.shape, a.dtype) for a in args], jax.ShapeDtypeStruct((8, 128), F32)),
        in_specs=[_HBM] * (ni + nl) + [pl.BlockSpec(memory_space=pl.ANY)],
        out_specs=(_SEM, _SEM, *[_HBM] * (ni + nl), pl.BlockSpec(memory_space=pltpu.VMEM)),
        input_output_aliases={k: k + 2 for k in range(ni + nl)},
        compiler_params=pltpu.CompilerParams(has_side_effects=_EFFECT),
    )(*args, after)
    return res[0], res[1], list(res[2:2 + ni]), list(res[2 + ni:2 + ni + nl]), res[-1]


def exchange_wait(send, recv, inputs, lands, after, transfers, name):
    ni, nl = len(inputs), len(lands)

    def body(*refs):
        ins, lnd = refs[:ni], refs[ni:ni + nl]
        send_r, recv_r = refs[ni + nl], refs[ni + nl + 1]
        for out_cp, in_cp in _remote_copies(ins, lnd, transfers, send_r, recv_r):
            out_cp.wait_send()
            in_cp.wait_recv()

    res = pl.pallas_call(
        body, name=name,
        out_shape=tuple(pltpu.HBM(a.shape, a.dtype) for a in list(inputs) + list(lands)),
        in_specs=[_HBM] * (ni + nl) + [_SEM, _SEM, pl.BlockSpec(memory_space=pl.ANY)],
        out_specs=tuple([_HBM] * (ni + nl)),
        input_output_aliases={k: k for k in range(ni + nl)},
        compiler_params=pltpu.CompilerParams(has_side_effects=_EFFECT),
    )(*inputs, *lands, send, recv, after)
    return list(res[ni:])


def adamw(w, m, v, contribs, name):
    r, c = w.shape
    nc = len(contribs)
    rows = next((r // d for d in (4, 2) if r % d == 0 and (r // d) % 16 == 0), r)
    c1, c2 = 1.0 - ADAM_B1 ** ADAM_STEP, 1.0 - ADAM_B2 ** ADAM_STEP

    def body(wv, mv, vv, *gs):
        g = gs[0].astype(F32)
        for extra in gs[1:]:
            g = g + extra.astype(F32)
        g = g[:, :c]
        m_new = ADAM_B1 * mv + (1.0 - ADAM_B1) * g
        v_new = ADAM_B2 * vv + (1.0 - ADAM_B2) * (g * g)
        delta = -ADAM_LR * ((m_new / c1) / (jnp.sqrt(v_new / c2) + ADAM_EPS) + ADAM_WD * wv)
        return g, delta, m_new, v_new

    assert nc >= 1
    return rowcall(body, [w, m, v] + list(contribs), [], [(c, F32)] * 4, rows=rows, total=r, name=name)


WEIGHTS = ['ffn1_pre_norm', 'ffn1_w_gate', 'ffn1_w_up', 'ffn1_w_down', 'ffn1_post_norm', 'mix_pre_norm', 'w_in',
           'fox_f_bias', 'gdn_conv_w', 'gdn_a_log', 'gdn_dt_bias', 'gdn_out_norm', 'w_out', 'mix_post_norm',
           'mem_pre_norm', 'mem_kv_norm', 'mem_w_q', 'mem_w_kv', 'mem_w_o', 'mem_post_norm', 'ffn2_pre_norm',
           'ffn2_w_gate', 'ffn2_w_up', 'ffn2_w_down', 'ffn2_post_norm']
GAINS = ['ffn1_pre_norm', 'ffn1_post_norm', 'mix_pre_norm', 'mix_post_norm', 'mem_pre_norm', 'mem_kv_norm',
         'mem_post_norm', 'ffn2_pre_norm', 'ffn2_post_norm']
BIG = ['ffn1_w_gate', 'ffn1_w_up', 'ffn1_w_down', 'w_in', 'w_out', 'mem_w_q', 'mem_w_kv', 'mem_w_o',
       'ffn2_w_gate', 'ffn2_w_up', 'ffn2_w_down']
PACK_ROWS = 24
ROW_MISC = len(GAINS)
ROW_CONV = ROW_MISC + 1
COL_FBIAS, COL_ALOG, COL_DTB, COL_ONORM, COL_LOSS = 0, 8, 12, 128, 256
CONV_CH = 3 * GDN_HEADS * GDN_DH


def _pad_to(a, shape):
    return jnp.pad(a, [(0, t - s) for s, t in zip(a.shape, shape)])


def _pack(get, conv=None, loss=None):
    rows = [get(nm) for nm in GAINS]
    misc = jnp.concatenate([get('fox_f_bias'), get('gdn_a_log'), get('gdn_dt_bias'),
                            jnp.zeros((1, COL_ONORM - COL_DTB - 4), F32), get('gdn_out_norm'),
                            jnp.zeros((1, 1), F32) if loss is None else loss.reshape(1, 1)], axis=1)
    rows.append(_pad_to(misc, (1, D_MODEL)))
    rows.append(jnp.zeros((6, D_MODEL), F32) if conv is None else conv.reshape(6, D_MODEL))
    return _pad_to(jnp.concatenate(rows, axis=0), (PACK_ROWS, D_MODEL))


def _unpack(p):
    out = {nm: p[i:i + 1] for i, nm in enumerate(GAINS)}
    misc = p[ROW_MISC:ROW_MISC + 1]
    out['fox_f_bias'] = misc[:, COL_FBIAS:COL_FBIAS + FOX_HEADS]
    out['gdn_a_log'] = misc[:, COL_ALOG:COL_ALOG + GDN_HEADS]
    out['gdn_dt_bias'] = misc[:, COL_DTB:COL_DTB + GDN_HEADS]
    out['gdn_out_norm'] = misc[:, COL_ONORM:COL_ONORM + GDN_DH]
    return out


def _ffn_fwd(h, pre, wgu, wd, tag):
    s = h.shape[0]
    u, = rowcall(_rms, [h], [pre], [(D_MODEL, BF)], rows=512, total=s, name=tag + "_pre")
    if callable(wgu):
        wgu = wgu(u)
    gate, up, act = mm_swiglu(u, wgu, name=tag + "_gate_up")
    if callable(wd):
        wd = wd(act)
    f = mm(act, wd, name=tag + "_down")
    return u, gate, up, act, f


def _half_rms(a, g):
    return 0.5 * _rms(a, g)


def _ffn_bwd(dh_out, h, pre, post, wgu, wd, saved, tag, on_dwd=None, on_dwgu=None):
    u, gate, up, act, f = saved
    s = h.shape[0]

    def b_post(dh, fv, pg):
        return jax.vjp(_half_rms, fv, pg)[1](dh)

    df, dpost = rowcall(b_post, [dh_out, f], [post], [(D_MODEL, BF)], [(1, D_MODEL)], rows=512, total=s,
                        name=tag + "_bwd_post")
    dwd = mm(act, df, ta=True, out_dtype=BF, name=tag + "_bwd_dwd")
    dgate, dup = mm_dswiglu(df, wd, gate, up, name=tag + "_bwd_dact", token=on_dwd(dwd) if on_dwd else None)
    dwg = mm(u, dgate, ta=True, out_dtype=BF, name=tag + "_bwd_dwg")
    dwu = mm(u, dup, ta=True, out_dtype=BF, name=tag + "_bwd_dwu")
    du = mm_pair(dgate, dup, wgu, name=tag + "_bwd_du", token=on_dwgu(dwg, dwu) if on_dwgu else None)

    def b_pre(dh, duv, hv, pg):
        dx, dpre = jax.vjp(_rms, hv, pg)[1](duv)
        return dh + dx, dpre

    dh, dpre = rowcall(b_pre, [dh_out, du, h], [pre], [(D_MODEL, F32)], [(1, D_MODEL)], rows=512, total=s,
                       name=tag + "_bwd_pre")
    return dh, dwg, dwu, dwd, dpre, dpost


def _residual_rms(h, a, g):
    return h + _rms(a, g)


def _bwd_residual(dh, a, g):
    return jax.vjp(_rms, a, g)[1](dh)


def _step(a):
    x, mem = a['x'][0], a['mem'][0]
    s = x.shape[0]
    me = 4 * lax.axis_index("x") + 2 * lax.axis_index("y") + lax.axis_index("c")
    w2 = {nm: a[nm][0] for nm in WEIGHTS}
    m2 = {nm: a['m_' + nm][0] for nm in WEIGHTS}
    v2 = {nm: a['v_' + nm][0] for nm in WEIGHTS}
    small = {nm: w2[nm][None] for nm in WEIGHTS if nm not in BIG and nm != 'gdn_conv_w'}

    def ff_cols(w):
        return _pad_to(w, (D_MODEL, FF_SHARD_PAD)).astype(BF)

    def ff_rows(w):
        return _pad_to(w, (FF_SHARD_PAD, D_MODEL)).astype(BF)

    whole = (None, 0, 0, 0)
    conv_pad = 256
    g_in = [ff_cols(w2['ffn1_w_gate']), ff_cols(w2['ffn1_w_up']), ff_rows(w2['ffn1_w_down']),
            ff_cols(w2['ffn2_w_gate']), ff_cols(w2['ffn2_w_up']), ff_rows(w2['ffn2_w_down']),
            _pad_to(w2['w_in'], (D_MODEL, IN_SHARD_PAD)).astype(BF), w2['w_out'].astype(BF),
            w2['mem_w_q'].astype(BF), w2['mem_w_kv'].astype(BF), w2['mem_w_o'].astype(BF),
            _pad_to(w2['gdn_conv_w'], (8, conv_pad))]
    g_out = [jax.ShapeDtypeStruct((D_MODEL, 2 * D_FF_PAD), BF), jax.ShapeDtypeStruct((D_FF_PAD, D_MODEL), BF),
             jax.ShapeDtypeStruct((D_MODEL, 2 * D_FF_PAD), BF), jax.ShapeDtypeStruct((D_FF_PAD, D_MODEL), BF),
             jax.ShapeDtypeStruct((D_MODEL, N_DEV * IN_SHARD_PAD), BF), jax.ShapeDtypeStruct((D_MODEL, D_MODEL), BF),
             jax.ShapeDtypeStruct((D_MODEL, D_MODEL), BF), jax.ShapeDtypeStruct((D_MODEL, 2 * D_MODEL), BF),
             jax.ShapeDtypeStruct((D_MODEL, D_MODEL), BF), jax.ShapeDtypeStruct((8, N_DEV * conv_pad), F32)]
    sp_, dm = FF_SHARD_PAD, D_MODEL // N_DEV
    g_tr = [(0, whole, 0, (1, 0, sp_, sp_)), (1, whole, 0, (1, D_FF_PAD, sp_, sp_)), (2, whole, 1, (0, 0, sp_, sp_)),
            (3, whole, 2, (1, 0, sp_, sp_)), (4, whole, 2, (1, D_FF_PAD, sp_, sp_)), (5, whole, 3, (0, 0, sp_, sp_)),
            (6, whole, 4, (1, 0, IN_SHARD_PAD, IN_SHARD_PAD)), (7, whole, 5, (0, 0, dm, dm)),
            (8, whole, 6, (0, 0, dm, dm)), (9, whole, 7, (1, 0, 2 * dm, 2 * dm)), (10, whole, 8, (0, 0, dm, dm)),
            (11, whole, 9, (1, 0, conv_pad, conv_pad))]
    def pick(idx):
        ins = sorted({g_tr[k][0] for k in idx})
        outs = sorted({g_tr[k][2] for k in idx})
        tr = [(ins.index(g_tr[k][0]), g_tr[k][1], outs.index(g_tr[k][2]), g_tr[k][3]) for k in idx]
        return [g_in[i] for i in ins], [g_out[o] for o in outs], tr

    stages, after = [], g_in[0]
    for nm, idx in (("gate_up", [0, 1]), ("down", [2]), ("mix", [6, 7, 11]), ("late", [8, 9, 10, 3, 4, 5])):
        st_in, st_out, st_tr = pick(idx)
        st = exchange_start(st_in, place_own(st_in, st_out, st_tr), st_tr, after, "gather_%s_start" % nm)
        stages.append((st, st_tr, "gather_%s_wait" % nm))
        after = st[4]
    g_token = after

    def gather_wait(k, after_):
        (send_, recv_, src_, land_, _), tr_, nm_ = stages[k]
        return exchange_wait(send_, recv_, src_, land_, after_, tr_, nm_)

    bias_row = _pad_to(small['fox_f_bias'], (1, 128))
    gate_prm = _pad_to(jnp.concatenate([_pad_to(small['gdn_a_log'], (1, 128 - SMALL_A)),
                                        _pad_to(small['gdn_dt_bias'], (1, 128 - SMALL_A))], axis=0),
                       (8, 128 - SMALL_A))
    gate_prm = jnp.pad(gate_prm, ((0, 0), (SMALL_A, 0)))
    onorm = small['gdn_out_norm']

    late = {}

    def wgu1_when(u):
        late['wgu1'], = gather_wait(0, u)
        return late['wgu1']

    def wd1_when(act):
        late['wd1'], = gather_wait(1, act)
        return late['wd1']

    sv1 = _ffn_fwd(x, small['ffn1_pre_norm'] + g_token[0, 0], wgu1_when, wd1_when, "ffn1")
    wgu1, wd1 = late['wgu1'], late['wd1']
    h1, = rowcall(lambda h, f, g: h + _half_rms(f, g), [x, sv1[4]], [small['ffn1_post_norm']], [(D_MODEL, F32)],
                  rows=512, total=s, name="ffn1_out")
    w_in_g, w_out, conv_g = gather_wait(2, h1)
    w_in = jnp.concatenate([w_in_g[:, j * IN_SHARD_PAD:j * IN_SHARD_PAD + IN_SHARD] for j in range(N_DEV)],
                           axis=1)
    sp = [0, 512, 1024, 1536, 1544, 2056, 2568, 3080, 3592, 3596, 3600]
    fq, fk, fv, ff, gq, gk, gv, gz, gb, ga = [w_in[:, sp[i]:sp[i + 1]] for i in range(10)]
    w_proj = jnp.concatenate([fq, fk, fv, gq, gk, gv, gz, ff, gb, ga,
                              jnp.zeros((D_MODEL, PROJ_W - 3584 - 16), BF)], axis=1)
    conv_w8 = conv_g.reshape(8, N_DEV, conv_pad)[:, :, :CONV_CH // N_DEV].reshape(8, CONV_CH)


    u2, = rowcall(_rms, [h1], [small['mix_pre_norm']], [(D_MODEL, BF)], rows=512, total=s, name="mix_pre")
    proj = mm(u2, w_proj, name="mix_proj")
    f_cum = fox_f_fwd(proj, bias_row)
    f_heads = f_cum[:, :FOX_HEADS]
    qkv_bf = proj[:, :3 * FOX_HEADS * FOX_DH].astype(BF)
    xk, xv = _fox_extras(s, 1.0, -f_heads), _fox_extras(s, 1.0, None)
    fox_flat, lse = fox_fwd(qkv_bf, _fox_extras(s, f_heads, 1.0), xk, xv)
    lse_heads = jnp.sum(lse, axis=0)[:, :FOX_HEADS]
    cqkv = conv_fwd(proj, conv_w8)
    g_l, b_l = rowcall(_gdn_gates, [(proj, 128, SMALL_BLOCK128)], [gate_prm], [(512, F32), (512, F32)],
                       rows=512, total=s, name="gdn_gates")
    gbb = jnp.concatenate([g_l, b_l], axis=1)
    gdn_o, states = gdn_fwd(cqkv, proj, gbb, onorm)
    mixed = jnp.concatenate([fox_flat, gdn_o], axis=1).astype(BF)
    mo = mm(mixed, w_out, name="mix_out")
    h2, = rowcall(_residual_rms, [h1, mo], [small['mix_post_norm']], [(D_MODEL, F32)], rows=512, total=s,
                  name="mix_res")

    hq, = rowcall(_rms, [h2], [small['mem_pre_norm']], [(D_MODEL, BF)], rows=512, total=s, name="mem_pre")
    mn, = rowcall(_rms, [mem], [small['mem_kv_norm']], [(D_MODEL, BF)], rows=256, total=mem.shape[0], name="mem_kvn")
    wgu2, wd2, w_q, w_kv, w_o = gather_wait(3, h2)
    q_mem = mm(hq, w_q, name="mem_q")
    kv_mem = mm(mn, w_kv, name="mem_kv")
    o_mem = xattn_fwd(q_mem, kv_mem)
    c_mem = mm(o_mem, w_o, name="mem_o")
    h3, = rowcall(_residual_rms, [h2, c_mem], [small['mem_post_norm']], [(D_MODEL, F32)], rows=512, total=s,
                  name="mem_res")

    sv2 = _ffn_fwd(h3, small['ffn2_pre_norm'], wgu2, wd2, "ffn2")

    def b_loss(h, f, tgt, g):
        err = h + _half_rms(f, g) - tgt
        part = 0.5 * jnp.sum(jnp.mean(err * err, axis=-1, keepdims=True), axis=0, keepdims=True)
        return err * (1.0 / D_MODEL), jnp.broadcast_to(part, (1, 128))

    dy, loss_acc = rowcall(b_loss, [h3, sv2[4], a['loss_target'][0]], [small['ffn2_post_norm']], [(D_MODEL, F32)],
                           [(1, 128)], rows=512, total=s, name="loss")

    grads = {}
    dh3, dwg2, dwu2, dwd2, grads['ffn2_pre_norm'], grads['ffn2_post_norm'] = _ffn_bwd(
        dy, h3, small['ffn2_pre_norm'], small['ffn2_post_norm'], wgu2, wd2, sv2, "ffn2")

    lead = ("lead", 0, 1, 0)

    def land(r, c, dt=BF):
        return jax.ShapeDtypeStruct((N_DEV, r, c), dt)

    ffn_tr = [(0, (1, 0, sp_, sp_), 0, lead), (1, (1, 0, sp_, sp_), 1, lead), (2, (0, 0, sp_, FF_SHARD), 2, lead)]
    ffn_land = [land(D_MODEL, sp_), land(D_MODEL, sp_), land(FF_SHARD, D_MODEL)]
    a_in = [dwg2, dwu2, dwd2]
    a_send, a_recv, a_src, a_land, a_token = exchange_start(a_in, place_own(a_in, ffn_land, ffn_tr), ffn_tr, dh3,
                                                            "reduce_ffn2_start")

    dc, grads['mem_post_norm'] = rowcall(_bwd_residual, [dh3, c_mem], [small['mem_post_norm'] + a_token[0, 0]],
                                         [(D_MODEL, BF)],
                                         [(1, D_MODEL)], rows=512, total=s, name="mem_bwd_res")
    d_o = mm(dc, w_o, tb=True, name="mem_bwd_do")
    dw_o = mm(o_mem, dc, ta=True, out_dtype=BF, name="mem_bwd_dwo")
    dq_mem, dkv = xattn_bwd(q_mem, kv_mem, d_o)
    dhq = mm(dq_mem, w_q, tb=True, name="mem_bwd_dhq")
    dw_q = mm(hq, dq_mem, ta=True, out_dtype=BF, name="mem_bwd_dwq")
    dmn = mm(dkv, w_kv, tb=True, name="mem_bwd_dmn")
    dw_kv = mm(mn, dkv, ta=True, out_dtype=BF, name="mem_bwd_dwkv")
    _, grads['mem_kv_norm'] = rowcall(lambda d, mv, g: jax.vjp(_rms, mv, g)[1](d), [dmn, mem],
                                      [small['mem_kv_norm']], [(D_MODEL, F32)], [(1, D_MODEL)], rows=256,
                                      total=mem.shape[0], name="mem_bwd_kvn")

    def b_pre(dh, duv, hv, pg):
        dx, dpre = jax.vjp(_rms, hv, pg)[1](duv)
        return dh + dx, dpre

    dh2, grads['mem_pre_norm'] = rowcall(b_pre, [dh3, dhq, h2], [small['mem_pre_norm']], [(D_MODEL, F32)],
                                         [(1, D_MODEL)], rows=512, total=s, name="mem_bwd_pre")

    dmo, grads['mix_post_norm'] = rowcall(_bwd_residual, [dh2, mo], [small['mix_post_norm']], [(D_MODEL, BF)],
                                          [(1, D_MODEL)], rows=512, total=s, name="mix_bwd_res")
    d_mixed = mm(dmo, w_out, tb=True, name="mix_bwd_dmixed")
    dw_out = mm(mixed, dmo, ta=True, out_dtype=BF, name="mix_bwd_dwout")
    def b_delta(do, o):
        sel = (_iota2((512, 128), 0) // FOX_DH == _iota2((512, 128), 1)).astype(F32)
        return hdot(do * o, sel)

    delta, = rowcall(b_delta, [(d_mixed, 512, 0), fox_flat], [], [(128, F32)], rows=512, total=s, name="fox_delta")
    dfox_q, dfox_k, dvf, sum_q, sum_k = fox_bwd(qkv_bf, d_mixed[:, :512].astype(BF), xk, xv,
                                                _fox_extras(s, f_heads - lse_heads, 1.0),
                                                _fox_extras(s, -delta[:, :FOX_HEADS], None))
    d_f = jnp.sum((sum_q - sum_k).reshape(s, FOX_PAIRS, 2 * FOX_DH), axis=1)
    dsmall_f, dbias = fox_f_bwd(proj, bias_row, d_f)
    grads['fox_f_bias'] = dbias[:, :FOX_HEADS]
    dcqkv, dz, dgb, grads['gdn_out_norm'] = gdn_bwd(cqkv, proj, gbb, onorm, states, d_mixed)

    def b_gates(sm, dsf, dg, db, prm):
        dsm, dprm = jax.vjp(_gdn_gates, sm, prm)[1]((dg, db))
        return dsm + dsf, dprm

    dsmall, dprm = rowcall(b_gates, [(proj, 128, SMALL_BLOCK128), dsmall_f, (dgb, 512, 0), (dgb, 512, 1)], [gate_prm],
                           [(128, F32)],
                           [(8, 128)], rows=512, total=s, name="gdn_bwd_gates")
    grads['gdn_a_log'] = dprm[0:1, SMALL_A:SMALL_A + GDN_HEADS]
    grads['gdn_dt_bias'] = dprm[1:2, SMALL_A:SMALL_A + GDN_HEADS]
    dqkv_pre, dconv8 = conv_bwd(proj, conv_w8, dcqkv)
    dproj = jnp.concatenate([dfox_q, dfox_k, dvf, dqkv_pre, dz, dsmall,
                             jnp.zeros((s, PROJ_W - 3584 - 128), F32)], axis=1).astype(BF)
    du2 = mm(dproj, w_proj, tb=True, name="mix_bwd_du")
    dw_proj = mm(u2, dproj, ta=True, out_dtype=BF, name="mix_bwd_dwproj")
    dh1, grads['mix_pre_norm'] = rowcall(b_pre, [dh2, du2, h1], [small['mix_pre_norm']], [(D_MODEL, F32)],
                                         [(1, D_MODEL)], rows=512, total=s, name="mix_bwd_pre")

    dw_in = jnp.concatenate([dw_proj[:, :1536], dw_proj[:, 3584:3592], dw_proj[:, 1536:3584],
                             dw_proj[:, 3592:3600]], axis=1)
    gap = jnp.zeros((D_MODEL, IN_SHARD_PAD - IN_SHARD), BF)
    dw_in = jnp.concatenate([piece for j in range(N_DEV) for piece in (dw_in[:, j * IN_SHARD:(j + 1) * IN_SHARD], gap)],
                            axis=1)
    b_in = [dw_in, dw_out, dw_q, dw_kv, dw_o]
    b_tr = [(0, (1, 0, IN_SHARD_PAD, IN_SHARD_PAD), 0, lead), (1, (0, 0, dm, dm), 1, lead), (2, (0, 0, dm, dm), 2, lead),
            (3, (1, 0, 2 * dm, 2 * dm), 3, lead), (4, (0, 0, dm, dm), 4, lead)]
    b_shapes = [land(D_MODEL, IN_SHARD_PAD), land(dm, D_MODEL), land(dm, D_MODEL), land(D_MODEL, 2 * dm),
                land(dm, D_MODEL)]
    b_land = place_own(b_in, b_shapes, b_tr)
    b_send, b_recv, b_src, b_land, b_token = exchange_start(b_in, b_land, b_tr, dh1, "reduce_mix_start")

    def start_down_reduce(dwd):
        tr = ffn_tr[2:]
        tr = [(0, tr[0][1], 0, tr[0][3])]
        late['c_down'] = (exchange_start([dwd], place_own([dwd], ffn_land[2:], tr), tr, dwd, "reduce_ffn1_down_start"), tr)
        return late['c_down'][0][4]

    def start_gate_up_reduce(dwg, dwu):
        tr = ffn_tr[:2]
        late['c_gu'] = (exchange_start([dwg, dwu], place_own([dwg, dwu], ffn_land[:2], tr), tr, dwu,
                                       "reduce_ffn1_gu_start"), tr)
        return late['c_gu'][0][4]

    grad_x, _, _, _, grads['ffn1_pre_norm'], grads['ffn1_post_norm'] = _ffn_bwd(
        dh1, x, small['ffn1_pre_norm'], small['ffn1_post_norm'] + b_token[0, 0], wgu1, wd1, sv1, "ffn1",
        on_dwd=start_down_reduce, on_dwgu=start_gate_up_reduce)

    gpack = _pack(lambda nm: grads[nm], conv=dconv8[:CONV_W], loss=loss_acc[:, :1])
    gsum_parts, = exchange([gpack], [land(PACK_ROWS, D_MODEL, F32)], [(0, whole, 0, lead)], "reduce_small")
    a_got = exchange_wait(a_send, a_recv, a_src, a_land, gsum_parts, ffn_tr, "reduce_ffn2_wait")
    b_got = exchange_wait(b_send, b_recv, b_src, b_land, gsum_parts, b_tr, "reduce_mix_wait")
    recv = dict(zip(['ffn2_w_gate', 'ffn2_w_up', 'ffn2_w_down', 'w_in', 'w_out', 'mem_w_q', 'mem_w_kv', 'mem_w_o'],
                    a_got + b_got))

    out_g, out_d, out_m, out_v = {}, {}, {}, {}

    def update(nm):
        r = recv[nm]
        res = adamw(w2[nm], m2[nm], v2[nm], [(r, r.shape[2], 0, d) for d in range(N_DEV)], "adamw_" + nm)
        out_g[nm], out_d[nm], out_m[nm], out_v[nm] = res

    for nm in recv:
        update(nm)
    wp = _pack(lambda nm: small[nm])
    mp = _pack(lambda nm: m2[nm][None])
    vp = _pack(lambda nm: v2[nm][None])
    pg, pd, pm, pv = adamw(wp, mp, vp, [(gsum_parts, D_MODEL, 0, d) for d in range(N_DEV)], "adamw_small")
    for dst, p in ((out_g, pg), (out_d, pd), (out_m, pm), (out_v, pv)):
        dst.update({k: val[0] for k, val in _unpack(p).items()})
    loss = pg[ROW_MISC, COL_LOSS]
    conv_g = lax.dynamic_slice_in_dim(pg[ROW_CONV:ROW_CONV + 6].reshape(CONV_W, CONV_CH), me * (CONV_CH // N_DEV),
                                      CONV_CH // N_DEV, axis=1)
    res = adamw(w2['gdn_conv_w'], m2['gdn_conv_w'], v2['gdn_conv_w'], [conv_g], "adamw_conv")
    out_g['gdn_conv_w'], out_d['gdn_conv_w'], out_m['gdn_conv_w'], out_v['gdn_conv_w'] = res

    done = sum(out_d[nm][0, 0] for nm in recv) + out_d['gdn_conv_w'][0, 0] + pd[0, 0]
    after = jnp.zeros((8, 128), F32) + done
    c_got = []
    for key, nm in (('c_gu', "reduce_ffn1_gu_wait"), ('c_down', "reduce_ffn1_down_wait")):
        (c_send, c_recv, c_src, c_land, _), tr = late[key]
        c_got += exchange_wait(c_send, c_recv, c_src, c_land, after, tr, nm)
    recv = dict(zip(['ffn1_w_gate', 'ffn1_w_up', 'ffn1_w_down'], c_got))
    for nm in recv:
        update(nm)

    def depth(t):
        return t[None]

    return (loss, grad_x[None], *[depth(out_g[nm]) for nm in WEIGHTS], *[depth(out_d[nm]) for nm in WEIGHTS],
            *[depth(out_m[nm]) for nm in WEIGHTS], *[depth(out_v[nm]) for nm in WEIGHTS])


def kernel(x, mem, ffn1_pre_norm, ffn1_w_gate, ffn1_w_up, ffn1_w_down, ffn1_post_norm, mix_pre_norm, w_in, fox_f_bias, gdn_conv_w, gdn_a_log, gdn_dt_bias, gdn_out_norm, w_out, mix_post_norm, mem_pre_norm, mem_kv_norm, mem_w_q, mem_w_kv, mem_w_o, mem_post_norm, ffn2_pre_norm, ffn2_w_gate, ffn2_w_up, ffn2_w_down, ffn2_post_norm, loss_target, m_ffn1_pre_norm, m_ffn1_w_gate, m_ffn1_w_up, m_ffn1_w_down, m_ffn1_post_norm, m_mix_pre_norm, m_w_in, m_fox_f_bias, m_gdn_conv_w, m_gdn_a_log, m_gdn_dt_bias, m_gdn_out_norm, m_w_out, m_mix_post_norm, m_mem_pre_norm, m_mem_kv_norm, m_mem_w_q, m_mem_w_kv, m_mem_w_o, m_mem_post_norm, m_ffn2_pre_norm, m_ffn2_w_gate, m_ffn2_w_up, m_ffn2_w_down, m_ffn2_post_norm, v_ffn1_pre_norm, v_ffn1_w_gate, v_ffn1_w_up, v_ffn1_w_down, v_ffn1_post_norm, v_mix_pre_norm, v_w_in, v_fox_f_bias, v_gdn_conv_w, v_gdn_a_log, v_gdn_dt_bias, v_gdn_out_norm, v_w_out, v_mix_post_norm, v_mem_pre_norm, v_mem_kv_norm, v_mem_w_q, v_mem_w_kv, v_mem_w_o, v_mem_post_norm, v_ffn2_pre_norm, v_ffn2_w_gate, v_ffn2_w_up, v_ffn2_w_down, v_ffn2_post_norm):
    return _step(dict(locals()))
```

```python
import functools

import jax
import jax.numpy as jnp
from jax import lax
from jax.experimental import pallas as pl
from jax.experimental.pallas import tpu as pltpu

F32 = jnp.float32
BF = jnp.bfloat16
HI = lax.Precision.HIGHEST
MESH = pl.DeviceIdType.MESH

N_DEV = 8
EPS = 1e-6
D_MODEL = 1024
D_FF = 2816
FF_SHARD = D_FF // N_DEV
FF_SHARD_PAD = 384
D_FF_PAD = FF_SHARD_PAD * N_DEV
FOX_HEADS, FOX_DH = 8, 64
GDN_HEADS, GDN_DH = 4, 128
GDN_CHUNK = 64
CONV_W = 4
MEM_HEADS, MEM_DH = 4, 256
IN_W = 3600
IN_SHARD = IN_W // N_DEV
IN_SHARD_PAD = 512
PROJ_W = 4096
SMALL_F, SMALL_B, SMALL_A = 0, 8, 12

ADAM_LR, ADAM_B1, ADAM_B2, ADAM_EPS, ADAM_WD, ADAM_STEP = 0.001, 0.9, 0.999, 1e-08, 0.01, 10

VMEM_LIMIT = 56 * 1024 * 1024


def _params(sem=None):
    return pltpu.CompilerParams(dimension_semantics=sem, vmem_limit_bytes=VMEM_LIMIT)


def _tile(n, pref, unit=128):
    if n <= pref:
        return n
    t = (pref // unit) * unit
    while t > unit and n % t:
        t -= unit
    assert n % t == 0, (n, pref)
    return t


@functools.partial(jax.custom_vjp, nondiff_argnums=(2, 3))
def bdot(a, b, ca, cb):
    return lax.dot_general(a.astype(BF), b.astype(BF), (((ca,), (cb,)), ((), ())), preferred_element_type=F32)


def _bdot_fwd(a, b, ca, cb):
    return bdot(a, b, ca, cb), (a, b)


def _bdot_bwd(ca, cb, res, g):
    a, b = res
    da = bdot(g, b, 1, 1 - cb) if ca == 1 else bdot(b, g, 1 - cb, 1)
    db = bdot(a, g, 1 - ca, 0) if cb == 0 else bdot(g, a, 0, 1 - ca)
    return da, db


bdot.defvjp(_bdot_fwd, _bdot_bwd)


def hdot(a, b):
    return jnp.dot(a, b, precision=HI, preferred_element_type=F32)


def mdot(a, b):
    return jnp.dot(a, b, precision=lax.Precision.HIGH, preferred_element_type=F32)


def _iota2(shape, dim):
    return lax.broadcasted_iota(jnp.int32, shape, dim)


def _sigmoid(x):
    return 1.0 / (1.0 + jnp.exp(-x))


def _silu(x):
    return x * _sigmoid(x)


def _softplus(x):
    return jnp.maximum(x, 0.0) + jnp.log(1.0 + jnp.exp(-jnp.abs(x)))


def _rms(x, gain):
    return x * lax.rsqrt(jnp.mean(x * x, axis=-1, keepdims=True) + EPS) * gain


def mm(a, b, *, name, ta=False, tb=False, out_dtype=F32, tm=1024, tn=1024, tk=1024, token=None):
    m, k = (a.shape[1], a.shape[0]) if ta else a.shape
    n = b.shape[0] if tb else b.shape[1]
    assert k == (b.shape[1] if tb else b.shape[0]), (a.shape, b.shape, ta, tb)
    tm, tn, tk = _tile(m, tm), _tile(n, tn), _tile(k, tk)
    nk = k // tk
    dims = (((0 if ta else 1,), (1 if tb else 0,)), ((), ()))

    def kern(a_ref, b_ref, *rest):
        o_ref, scratch = (rest[1], rest[2:]) if token is not None else (rest[0], rest[1:])

        def part():
            return lax.dot_general(a_ref[...].astype(BF), b_ref[...].astype(BF), dims, preferred_element_type=F32)

        if nk == 1:
            o_ref[...] = part().astype(o_ref.dtype)
            return
        acc_ref, = scratch
        kk = pl.program_id(2)

        @pl.when(kk == 0)
        def _():
            acc_ref[...] = part()

        @pl.when(kk > 0)
        def _():
            acc_ref[...] += part()

        @pl.when(kk == nk - 1)
        def _():
            o_ref[...] = acc_ref[...].astype(o_ref.dtype)

    a_spec = pl.BlockSpec((tk, tm), lambda i, j, kk: (kk, i)) if ta else pl.BlockSpec((tm, tk), lambda i, j, kk: (i, kk))
    b_spec = pl.BlockSpec((tn, tk), lambda i, j, kk: (j, kk)) if tb else pl.BlockSpec((tk, tn), lambda i, j, kk: (kk, j))
    return pl.pallas_call(
        kern, name=name, grid=(m // tm, n // tn, nk),
        in_specs=[a_spec, b_spec] + ([pl.BlockSpec((8, 128), lambda i, j, kk: (0, 0))] if token is not None else []),
        out_specs=pl.BlockSpec((tm, tn), lambda i, j, kk: (i, j)),
        out_shape=jax.ShapeDtypeStruct((m, n), out_dtype),
        scratch_shapes=[pltpu.VMEM((tm, tn), F32)] if nk > 1 else [],
        compiler_params=_params(("parallel", "parallel", "arbitrary")),
    )(*((a, b) if token is None else (a, b, token)))


def mm_swiglu(a, wgu, *, name):
    m, k = a.shape
    nh = wgu.shape[1] // 2
    tm, tn = _tile(m, 1024), _tile(nh, 512)
    nj = nh // tn

    def kern(a_ref, bg_ref, bu_ref, g_ref, u_ref, act_ref):
        av = a_ref[...]
        g = jnp.dot(av, bg_ref[...], preferred_element_type=F32).astype(BF)
        u = jnp.dot(av, bu_ref[...], preferred_element_type=F32).astype(BF)
        g_ref[...] = g
        u_ref[...] = u
        act_ref[...] = (_silu(g.astype(F32)) * u.astype(F32)).astype(BF)

    tile = pl.BlockSpec((tm, tn), lambda i, j: (i, j))
    out = jax.ShapeDtypeStruct((m, nh), BF)
    return pl.pallas_call(
        kern, name=name, grid=(m // tm, nj),
        in_specs=[pl.BlockSpec((tm, k), lambda i, j: (i, 0)), pl.BlockSpec((k, tn), lambda i, j: (0, j)),
                  pl.BlockSpec((k, tn), lambda i, j: (0, j + nj))],
        out_specs=[tile, tile, tile], out_shape=[out, out, out],
        compiler_params=_params(("parallel", "parallel")),
    )(a, wgu, wgu)


def mm_dswiglu(df, wd, gate, up, *, name, token=None):
    m, k = df.shape
    nh = wd.shape[0]
    tm, tn = _tile(m, 1024), _tile(nh, 512)

    def kern(df_ref, wd_ref, g_ref, u_ref, *rest):
        dg_ref, du_ref = rest[-2:]
        da = lax.dot_general(df_ref[...], wd_ref[...], (((1,), (1,)), ((), ())), preferred_element_type=F32)
        g, u = g_ref[...].astype(F32), u_ref[...].astype(F32)
        sg = _sigmoid(g)
        dg_ref[...] = (da * u * (sg * (1.0 + g * (1.0 - sg)))).astype(BF)
        du_ref[...] = (da * (g * sg)).astype(BF)

    tile = pl.BlockSpec((tm, tn), lambda i, j: (i, j))
    out = jax.ShapeDtypeStruct((m, nh), BF)
    extra = [pl.BlockSpec((8, 128), lambda i, j: (0, 0))] if token is not None else []
    return pl.pallas_call(
        kern, name=name, grid=(m // tm, nh // tn),
        in_specs=[pl.BlockSpec((tm, k), lambda i, j: (i, 0)), pl.BlockSpec((tn, k), lambda i, j: (j, 0)), tile, tile]
        + extra,
        out_specs=[tile, tile], out_shape=[out, out],
        compiler_params=_params(("parallel", "parallel")),
    )(*((df, wd, gate, up) if token is None else (df, wd, gate, up, token)))


def mm_pair(a1, a2, wgu, *, name, token=None):
    m, nh = a1.shape
    n = wgu.shape[0]
    tm, tn, tk = _tile(m, 1024), _tile(n, 1024), _tile(nh, 1024)
    nk = nh // tk
    nt = (((1,), (1,)), ((), ()))

    def kern(a1_ref, a2_ref, b1_ref, b2_ref, *rest):
        o_ref, acc_ref = rest[-2:]
        kk = pl.program_id(2)

        def part():
            return (lax.dot_general(a1_ref[...], b1_ref[...], nt, preferred_element_type=F32)
                    + lax.dot_general(a2_ref[...], b2_ref[...], nt, preferred_element_type=F32))

        @pl.when(kk == 0)
        def _():
            acc_ref[...] = part()

        @pl.when(kk > 0)
        def _():
            acc_ref[...] += part()

        @pl.when(kk == nk - 1)
        def _():
            o_ref[...] = acc_ref[...]

    a_spec = pl.BlockSpec((tm, tk), lambda i, j, kk: (i, kk))
    extra = [pl.BlockSpec((8, 128), lambda i, j, kk: (0, 0))] if token is not None else []
    return pl.pallas_call(
        kern, name=name, grid=(m // tm, n // tn, nk),
        in_specs=[a_spec, a_spec, pl.BlockSpec((tn, tk), lambda i, j, kk: (j, kk)),
                  pl.BlockSpec((tn, tk), lambda i, j, kk: (j, kk + nk))] + extra,
        out_specs=pl.BlockSpec((tm, tn), lambda i, j, kk: (i, j)),
        out_shape=jax.ShapeDtypeStruct((m, n), F32),
        scratch_shapes=[pltpu.VMEM((tm, tn), F32)],
        compiler_params=_params(("parallel", "parallel", "arbitrary")),
    )(*((a1, a2, wgu, wgu) if token is None else (a1, a2, wgu, wgu, token)))


def _row_spec(item, rows):
    if not isinstance(item, tuple):
        return item, pl.BlockSpec((rows, item.shape[1]), lambda i: (i, 0))
    if len(item) == 3:
        arr, w, c = item
        return arr, pl.BlockSpec((rows, w), lambda i: (i, c))
    arr, w, c, lead = item
    return arr, pl.BlockSpec((None, rows, w), lambda i: (lead, i, c))


def _whole_spec(item):
    if not isinstance(item, tuple):
        return item, pl.BlockSpec(item.shape, lambda i: (0,) * item.ndim)
    arr, w, c = item
    return arr, pl.BlockSpec((arr.shape[0], w), lambda i: (0, c))


def rowcall(body, tiled, whole, outs, accs=(), *, rows, total, name):
    rows = min(rows, total)
    assert total % rows == 0
    t_arr, t_spec = zip(*[_row_spec(t, rows) for t in tiled])
    w_arr, w_spec = zip(*[_whole_spec(w) for w in whole]) if whole else ((), ())
    nt, nw, no, na = len(t_arr), len(w_arr), len(outs), len(accs)

    def kern(*refs):
        vals = [r[...] for r in refs[:nt + nw]]
        res = body(*vals)
        if not isinstance(res, (tuple, list)):
            res = (res,)
        assert len(res) == no + na, (name, len(res), no, na)
        for r, v in zip(refs[nt + nw:nt + nw + no], res[:no]):
            r[...] = v.astype(r.dtype)
        if na:
            acc_refs = refs[nt + nw + no:]

            @pl.when(pl.program_id(0) == 0)
            def _():
                for r in acc_refs:
                    r[...] = jnp.zeros_like(r)

            for r, v in zip(acc_refs, res[no:]):
                r[...] += v

    out_shape = [jax.ShapeDtypeStruct((total, w), d) for w, d in outs] + [jax.ShapeDtypeStruct(s, F32) for s in accs]
    out_specs = [pl.BlockSpec((rows, w), lambda i: (i, 0)) for w, _ in outs] + \
                [pl.BlockSpec(s, lambda i: (0, 0)) for s in accs]
    res = pl.pallas_call(
        kern, name=name, grid=(total // rows,),
        in_specs=list(t_spec) + list(w_spec), out_specs=out_specs, out_shape=out_shape,
        compiler_params=_params(("arbitrary",) if na else ("parallel",)),
    )(*t_arr, *w_arr)
    return res


def _colsum(x):
    return jnp.sum(x, axis=0, keepdims=True)


GDN_UNROLL = 4


def _gdn_chunk(q, k, v, z, gb, bb, state, gain, with_starts=False):
    c = GDN_CHUNK
    nh = len(q)
    hs = range(nh)
    r64, c64 = _iota2((c, c), 0), _iota2((c, c), 1)
    incl = r64 >= c64
    strict = r64 > c64
    ltri = incl.astype(F32)
    eye = (r64 == c64).astype(F32)
    pick = (_iota2((GDN_DH, c), 0) == _iota2((GDN_DH, c), 1)).astype(F32)
    last = (_iota2((c, GDN_DH), 0) == c - 1).astype(F32)

    qn = [q[h] * lax.rsqrt(jnp.sum(q[h] * q[h], axis=-1, keepdims=True) + EPS) * (GDN_DH ** -0.5) for h in hs]
    kn = [k[h] * lax.rsqrt(jnp.sum(k[h] * k[h], axis=-1, keepdims=True) + EPS) for h in hs]
    gc = [mdot(ltri, gb[h]) for h in hs]
    gcol = [mdot(gc[h], pick) for h in hs]
    dec = [jnp.exp(jnp.where(incl, gcol[h] - gcol[h].T, -1e30)) for h in hs]
    kb = [kn[h] * bb[h] for h in hs]
    vb = [v[h] * bb[h] for h in hs]
    kk = [bdot(kb[h], kn[h], 1, 1) for h in hs]
    p = [-jnp.where(strict, kk[h] * dec[h], 0.0) for h in hs]
    tinv = [eye + p[h] for h in hs]
    for level in range(5):
        dot = mdot if level < 2 else (lambda a, b: bdot(a, b, 1, 0))
        p = [dot(p[h], p[h]) for h in hs]
        tinv = [tinv[h] + dot(tinv[h], p[h]) for h in hs]
    egc = [jnp.exp(gc[h]) for h in hs]
    u = [mdot(tinv[h], vb[h]) for h in hs]
    w = [mdot(tinv[h], kb[h] * egc[h]) for h in hs]
    attn = [bdot(qn[h], kn[h], 1, 1) * dec[h] for h in hs]
    qd = [qn[h] * egc[h] for h in hs]
    gl = [jnp.sum(gc[h] * last, axis=0, keepdims=True) for h in hs]
    kt = [kn[h] * jnp.exp(gl[h] - gc[h]) for h in hs]
    nst = len(state)
    st, o, mids = list(state), [None] * nh, []
    for c0 in range(0, nh, nst):
        us = range(c0, c0 + nst)
        mids.append(tuple(st))
        ws = [bdot(w[h], st[h - c0], 1, 0) for h in us]
        qs = [bdot(qd[h], st[h - c0], 1, 0) for h in us]
        v_new = [u[h] - ws[h - c0] for h in us]
        av = [bdot(attn[h], v_new[h - c0], 1, 0) for h in us]
        kv = [bdot(kt[h], v_new[h - c0], 0, 0) for h in us]
        st = [st[h - c0] * jnp.exp(gl[h]) + kv[h - c0] for h in us]
        for h in us:
            o[h] = _rms(qs[h - c0] + av[h - c0], gain) * _silu(z[h])
    if with_starts:
        return tuple(o), tuple(st), tuple(mids)
    return tuple(o), tuple(st)


GDN_ROWS = 512
GDN_W = GDN_HEADS * GDN_DH


def gdn_fwd(cqkv, proj, gbb, gain):
    s = cqkv.shape[0]
    nb, cpb = s // GDN_ROWS, GDN_ROWS // GDN_CHUNK
    h4 = GDN_HEADS

    def kern(qkv_ref, z_ref, gb_ref, gain_ref, o_ref, st_ref, state):
        @pl.when(pl.program_id(0) == 0)
        def _():
            state[...] = jnp.zeros_like(state)

        gain_v = gain_ref[...]

        def step(ci, carry):
            sls = [pl.ds(pl.multiple_of((ci * GDN_UNROLL + c) * GDN_CHUNK, GDN_CHUNK), GDN_CHUNK)
                   for c in range(GDN_UNROLL)]
            ins = []
            for sl in sls:
                for h in range(h4):
                    ln = lambda base, h=h: slice(base + h * GDN_DH, base + (h + 1) * GDN_DH)
                    ins.append((qkv_ref[sl, ln(0)], qkv_ref[sl, ln(GDN_W)], qkv_ref[sl, ln(2 * GDN_W)],
                                z_ref[sl, ln(0)], gb_ref[sl, ln(0)], gb_ref[sl, ln(GDN_W)]))
            cols = [tuple(col) for col in zip(*ins)]
            o, new, starts = _gdn_chunk(*cols, tuple(state[h] for h in range(h4)), gain_v, with_starts=True)
            for c, sl in enumerate(sls):
                for h in range(h4):
                    st_ref[h, ci * GDN_UNROLL + c] = starts[c][h]
                    o_ref[sl, h * GDN_DH:(h + 1) * GDN_DH] = o[c * h4 + h]
            for h in range(h4):
                state[h] = new[h]
            return carry

        lax.fori_loop(0, cpb // GDN_UNROLL, step, 0)

    return pl.pallas_call(
        kern, name="gdn_fwd", grid=(nb,),
        in_specs=[pl.BlockSpec((GDN_ROWS, 3 * GDN_W), lambda i: (i, 0)),
                  pl.BlockSpec((GDN_ROWS, GDN_W), lambda i: (i, 6)),
                  pl.BlockSpec((GDN_ROWS, 2 * GDN_W), lambda i: (i, 0)),
                  pl.BlockSpec((1, GDN_DH), lambda i: (0, 0))],
        out_specs=[pl.BlockSpec((GDN_ROWS, GDN_W), lambda i: (i, 0)),
                   pl.BlockSpec((h4, cpb, GDN_DH, GDN_DH), lambda i: (0, i, 0, 0))],
        out_shape=[jax.ShapeDtypeStruct((s, GDN_W), F32),
                   jax.ShapeDtypeStruct((h4, s // GDN_CHUNK, GDN_DH, GDN_DH), F32)],
        scratch_shapes=[pltpu.VMEM((h4, GDN_DH, GDN_DH), F32)],
        compiler_params=_params(("arbitrary",)),
    )(cqkv, proj, gbb, gain)


def gdn_bwd(cqkv, proj, gbb, gain, states, d_mixed):
    s = cqkv.shape[0]
    nb, cpb = s // GDN_ROWS, GDN_ROWS // GDN_CHUNK
    h4 = GDN_HEADS

    def kern(qkv_ref, z_ref, gb_ref, gain_ref, st_ref, do_ref, dqkv_ref, dz_ref, dgb_ref, dgain_ref, dstate):
        @pl.when(pl.program_id(0) == 0)
        def _():
            dgain_ref[...] = jnp.zeros_like(dgain_ref)
            dstate[...] = jnp.zeros_like(dstate)

        gain_v = gain_ref[...]

        def step(t, carry):
            first = (cpb // GDN_UNROLL - 1 - t) * GDN_UNROLL
            sls = [pl.ds(pl.multiple_of((first + c) * GDN_CHUNK, GDN_CHUNK), GDN_CHUNK) for c in range(GDN_UNROLL)]
            prim, cot = [], []
            for sl in sls:
                for h in range(h4):
                    ln = lambda base, h=h: slice(base + h * GDN_DH, base + (h + 1) * GDN_DH)
                    prim.append((qkv_ref[sl, ln(0)], qkv_ref[sl, ln(GDN_W)], qkv_ref[sl, ln(2 * GDN_W)],
                                 z_ref[sl, ln(0)], gb_ref[sl, ln(0)], gb_ref[sl, ln(GDN_W)]))
                    cot.append(do_ref[sl, ln(0)])
            cols = [tuple(col) for col in zip(*prim)]
            st_in = tuple(st_ref[h, first] for h in range(h4))
            vjp = jax.vjp(_gdn_chunk, *cols, st_in, gain_v)[1]
            dq, dk, dv, dz, dg, db, dst, dgn = vjp((tuple(cot), tuple(dstate[h] for h in range(h4))))
            for c, sl in enumerate(sls):
                for h in range(h4):
                    ln = lambda base, h=h: slice(base + h * GDN_DH, base + (h + 1) * GDN_DH)
                    unit = c * h4 + h
                    dqkv_ref[sl, ln(0)] = dq[unit]
                    dqkv_ref[sl, ln(GDN_W)] = dk[unit]
                    dqkv_ref[sl, ln(2 * GDN_W)] = dv[unit]
                    dz_ref[sl, ln(0)] = dz[unit]
                    dgb_ref[sl, ln(0)] = dg[unit]
                    dgb_ref[sl, ln(GDN_W)] = db[unit]
            for h in range(h4):
                dstate[h] = dst[h]
            dgain_ref[...] += dgn
            return carry

        lax.fori_loop(0, cpb // GDN_UNROLL, step, 0)

    def rev(width, cblock=0):
        return pl.BlockSpec((GDN_ROWS, width), lambda i: (nb - 1 - i, cblock))

    return pl.pallas_call(
        kern, name="gdn_bwd", grid=(nb,),
        in_specs=[rev(3 * GDN_W), rev(GDN_W, 6), rev(2 * GDN_W), pl.BlockSpec((1, GDN_DH), lambda i: (0, 0)),
                  pl.BlockSpec((h4, cpb, GDN_DH, GDN_DH), lambda i: (0, nb - 1 - i, 0, 0)), rev(GDN_W, 1)],
        out_specs=[rev(3 * GDN_W), rev(GDN_W), rev(2 * GDN_W), pl.BlockSpec((1, GDN_DH), lambda i: (0, 0))],
        out_shape=[jax.ShapeDtypeStruct((s, 3 * GDN_W), F32), jax.ShapeDtypeStruct((s, GDN_W), F32),
                   jax.ShapeDtypeStruct((s, 2 * GDN_W), F32), jax.ShapeDtypeStruct((1, GDN_DH), F32)],
        scratch_shapes=[pltpu.VMEM((h4, GDN_DH, GDN_DH), F32)],
        compiler_params=_params(("arbitrary",)),
    )(cqkv, proj, gbb, gain, states, d_mixed)


def _gdn_gates(small, prm):
    w = GDN_HEADS * GDN_DH
    lane, head = _iota2((128, w), 0), _iota2((128, w), 1) // GDN_DH
    sel_b = (lane == SMALL_B + head).astype(F32)
    sel_a = (lane == SMALL_A + head).astype(F32)
    prow = _iota2((8, 128), 0)
    a_log = jnp.sum(prm * (prow == 0).astype(F32), axis=0, keepdims=True)
    dt_b = jnp.sum(prm * (prow == 1).astype(F32), axis=0, keepdims=True)
    beta = _sigmoid(mdot(small, sel_b))
    g = mdot(-jnp.exp(a_log) * _softplus(small + dt_b), sel_a)
    return g, beta


CONV_ROWS = 1024
CONV_COLS = 128
CONV_BLOCK0 = 1536 // CONV_COLS


def _shift_down(prev8, cur, s):
    ext = jnp.concatenate([prev8, cur], axis=0)
    return pltpu.roll(ext, s, 0)[8:]


def _shift_up(cur, next8, s):
    n = cur.shape[0]
    ext = jnp.concatenate([cur, next8], axis=0)
    return pltpu.roll(ext, n + 8 - s, 0)[:n]


def _conv_pre(x_ref, w, ci, nchunk):
    r0 = pl.multiple_of(ci * CONV_ROWS, CONV_ROWS)
    cur = x_ref[pl.ds(r0, CONV_ROWS), :]
    prev = x_ref[pl.ds(pl.multiple_of(jnp.maximum(r0 - 8, 0), 8), 8), :]
    prev = jnp.where(ci > 0, prev, 0.0)
    shifted = [cur] + [_shift_down(prev, cur, s) for s in range(1, CONV_W)]
    pre = w[CONV_W - 1:CONV_W, :] * cur
    for s in range(1, CONV_W):
        pre = pre + w[CONV_W - 1 - s:CONV_W - s, :] * shifted[s]
    return r0, pre, shifted


def conv_fwd(proj, conv_w8):
    s = proj.shape[0]
    nchunk = s // CONV_ROWS
    ncol = 3 * GDN_HEADS * GDN_DH // CONV_COLS

    def kern(x_ref, w_ref, y_ref):
        w = w_ref[...]

        def step(ci, carry):
            r0, pre, _ = _conv_pre(x_ref, w, ci, nchunk)
            y_ref[pl.ds(r0, CONV_ROWS), :] = _silu(pre)
            return carry

        lax.fori_loop(0, nchunk, step, 0)

    return pl.pallas_call(
        kern, name="conv_fwd", grid=(ncol,),
        in_specs=[pl.BlockSpec((s, CONV_COLS), lambda j: (0, CONV_BLOCK0 + j)),
                  pl.BlockSpec((8, CONV_COLS), lambda j: (0, j))],
        out_specs=pl.BlockSpec((s, CONV_COLS), lambda j: (0, j)),
        out_shape=jax.ShapeDtypeStruct((s, ncol * CONV_COLS), F32),
        compiler_params=_params(("parallel",)),
    )(proj, conv_w8)


def conv_bwd(proj, conv_w8, dy):
    s = proj.shape[0]
    nchunk = s // CONV_ROWS
    per = 3 * GDN_HEADS * GDN_DH // CONV_COLS
    outs = []
    for part in range(1):
        def kern(x_ref, w_ref, dy_ref, dx_ref, dw_ref, dpre_ref):
            w = w_ref[...]
            rows8 = _iota2((8, CONV_COLS), 0)

            def step1(ci, dw):
                r0, pre, shifted = _conv_pre(x_ref, w, ci, nchunk)
                sg = _sigmoid(pre)
                dpre = dy_ref[pl.ds(r0, CONV_ROWS), :] * sg * (1.0 + pre * (1.0 - sg))
                dpre_ref[pl.ds(r0, CONV_ROWS), :] = dpre
                for sh in range(CONV_W):
                    dw = dw + jnp.where(rows8 == CONV_W - 1 - sh, _colsum(dpre * shifted[sh]), 0.0)
                return dw

            dw_ref[...] = lax.fori_loop(0, nchunk, step1, jnp.zeros((8, CONV_COLS), F32))

            def step2(ci, carry):
                r0 = pl.multiple_of(ci * CONV_ROWS, CONV_ROWS)
                cur = dpre_ref[pl.ds(r0, CONV_ROWS), :]
                nxt = dpre_ref[pl.ds(pl.multiple_of(jnp.minimum(r0 + CONV_ROWS, s - 8), 8), 8), :]
                nxt = jnp.where(ci < nchunk - 1, nxt, 0.0)
                dx = w[CONV_W - 1:CONV_W, :] * cur
                for sh in range(1, CONV_W):
                    dx = dx + w[CONV_W - 1 - sh:CONV_W - sh, :] * _shift_up(cur, nxt, sh)
                dx_ref[pl.ds(r0, CONV_ROWS), :] = dx
                return carry

            lax.fori_loop(0, nchunk, step2, 0)

        outs.append(pl.pallas_call(
            kern, name=f"conv_bwd{part}", grid=(per,),
            in_specs=[pl.BlockSpec((s, CONV_COLS), lambda j, part=part: (0, CONV_BLOCK0 + part * per + j)),
                      pl.BlockSpec((8, CONV_COLS), lambda j, part=part: (0, part * per + j)),
                      pl.BlockSpec((s, CONV_COLS), lambda j: (0, j))],
            out_specs=[pl.BlockSpec((s, CONV_COLS), lambda j: (0, j)),
                       pl.BlockSpec((8, CONV_COLS), lambda j: (0, j))],
            out_shape=[jax.ShapeDtypeStruct((s, per * CONV_COLS), F32),
                       jax.ShapeDtypeStruct((8, per * CONV_COLS), F32)],
            scratch_shapes=[pltpu.VMEM((s, CONV_COLS), F32)],
            compiler_params=_params(("parallel",)),
        )(proj, conv_w8, dy))
    dx = jnp.concatenate([o[0] for o in outs], axis=1)
    dw = jnp.concatenate([o[1] for o in outs], axis=1)
    return dx, dw


FOXF_ROWS = 512
SMALL_BLOCK128 = 3584 // 128


def _log_sigmoid(x):
    return jnp.minimum(x, 0.0) - jnp.log(1.0 + jnp.exp(-jnp.abs(x)))


def fox_f_fwd(proj, bias_row):
    s = proj.shape[0]
    n = s // FOXF_ROWS

    def kern(x_ref, b_ref, f_ref, carry):
        @pl.when(pl.program_id(0) == 0)
        def _():
            carry[...] = jnp.zeros_like(carry)

        heads = _iota2((FOXF_ROWS, 128), 1) < FOX_HEADS
        lf = jnp.where(heads, _log_sigmoid(x_ref[...] + b_ref[...]), 0.0)
        ltri = (_iota2((FOXF_ROWS, FOXF_ROWS), 0) >= _iota2((FOXF_ROWS, FOXF_ROWS), 1)).astype(F32)
        c = hdot(ltri, lf) + carry[...]
        f_ref[...] = c
        carry[...] = c[FOXF_ROWS - 1:FOXF_ROWS, :]

    return pl.pallas_call(
        kern, name="fox_f_fwd", grid=(n,),
        in_specs=[pl.BlockSpec((FOXF_ROWS, 128), lambda i: (i, SMALL_BLOCK128)),
                  pl.BlockSpec((1, 128), lambda i: (0, 0))],
        out_specs=pl.BlockSpec((FOXF_ROWS, 128), lambda i: (i, 0)),
        out_shape=jax.ShapeDtypeStruct((s, 128), F32),
        scratch_shapes=[pltpu.VMEM((1, 128), F32)],
        compiler_params=_params(("arbitrary",)),
    )(proj, bias_row)


def fox_f_bwd(proj, bias_row, d_f):
    s = proj.shape[0]
    n = s // FOXF_ROWS

    def kern(x_ref, b_ref, df_ref, dx_ref, db_ref, carry):
        @pl.when(pl.program_id(0) == 0)
        def _():
            carry[...] = jnp.zeros_like(carry)
            db_ref[...] = jnp.zeros_like(db_ref)

        heads = _iota2((FOXF_ROWS, 128), 1) < FOX_HEADS
        utri = (_iota2((FOXF_ROWS, FOXF_ROWS), 0) <= _iota2((FOXF_ROWS, FOXF_ROWS), 1)).astype(F32)
        rc = hdot(utri, df_ref[...]) + carry[...]
        carry[...] = rc[0:1, :]
        dx = jnp.where(heads, rc * _sigmoid(-(x_ref[...] + b_ref[...])), 0.0)
        dx_ref[...] = dx
        db_ref[...] += _colsum(dx)

    return pl.pallas_call(
        kern, name="fox_f_bwd", grid=(n,),
        in_specs=[pl.BlockSpec((FOXF_ROWS, 128), lambda i: (n - 1 - i, SMALL_BLOCK128)),
                  pl.BlockSpec((1, 128), lambda i: (0, 0)),
                  pl.BlockSpec((FOXF_ROWS, 128), lambda i: (n - 1 - i, 0))],
        out_specs=[pl.BlockSpec((FOXF_ROWS, 128), lambda i: (n - 1 - i, 0)),
                   pl.BlockSpec((1, 128), lambda i: (0, 0))],
        out_shape=[jax.ShapeDtypeStruct((s, 128), F32), jax.ShapeDtypeStruct((1, 128), F32)],
        scratch_shapes=[pltpu.VMEM((1, 128), F32)],
        compiler_params=_params(("arbitrary",)),
    )(proj, bias_row, d_f)


FOX_T = 512
FOX_SCALE = FOX_DH ** -0.5
FOX_PAIRS = FOX_HEADS // 2
NEG = -1e30
_NT = (((1,), (1,)), ((), ()))


def _split3(x):
    def bf(v):
        return lax.reduce_precision(v, exponent_bits=8, mantissa_bits=7)

    hi = bf(x)
    mid = bf(x - hi)
    lo = bf(x - hi - mid)
    return jnp.stack([hi, mid, lo], axis=-1)


def _fox_extras(s, first, second):
    def part(v):
        if v is None:
            return jnp.zeros((s, FOX_HEADS, 3), F32)
        if isinstance(v, float):
            return jnp.full((s, FOX_HEADS, 3), v, F32)
        pairs = v.reshape(s, FOX_PAIRS, 2)
        return _split3(jnp.stack([pairs[:, :, 1], pairs[:, :, 0]], axis=-1).reshape(s, FOX_HEADS))

    cols = jnp.concatenate([part(first), part(second)], axis=-1)
    cols = _pad_to(cols, (s, FOX_HEADS, FOX_DH)).reshape(s, FOX_PAIRS, 2 * FOX_DH)
    return cols.transpose(1, 0, 2).astype(BF)


def _head_masks(rows):
    lane = _iota2((rows, 2 * FOX_DH), 1)
    return lane < FOX_DH, lane >= FOX_DH


def _extra_lane(e, slot):
    return (FOX_DH if e == 0 else 0) + slot


def fox_fwd(qkv, xq, xk, xv):
    s = qkv.shape[0]
    t = min(FOX_T, s)
    n = s // t

    def kern(q_ref, k_ref, v_ref, xq_ref, xk_ref, xv_ref, o_ref, lse_ref):
        i = pl.program_id(1)
        masks = _head_masks(t)
        q_pair, x_pair = q_ref[...] * FOX_SCALE, xq_ref[...]
        q_ops = [jnp.where(mk, q_pair, x_pair) for mk in masks]

        def step(j, carry, masked):
            sl = pl.ds(pl.multiple_of(j * t, t), t)
            k_pair, xk_pair, v_pair, xv_pair = k_ref[sl, :], xk_ref[sl, :], v_ref[sl, :], xv_ref[sl, :]
            k_ops = [jnp.where(mk, k_pair, xk_pair) for mk in masks]
            v_ops = [jnp.where(mk, v_pair, xv_pair) for mk in masks]
            sc = [lax.dot_general(q_ops[e], k_ops[e], _NT, preferred_element_type=F32) for e in range(2)]
            if masked:
                keep = _iota2((t, t), 0) >= _iota2((t, t), 1)
                sc = [jnp.where(keep, x, NEG) for x in sc]
            m_new = [jnp.maximum(carry[e][0], jnp.max(sc[e], axis=1, keepdims=True)) for e in range(2)]
            p = [jnp.exp(sc[e] - m_new[e]).astype(BF) for e in range(2)]
            pv = [jnp.dot(p[e], v_ops[e], preferred_element_type=F32) for e in range(2)]
            return tuple((m_new[e], jnp.exp(carry[e][0] - m_new[e]) * carry[e][1] + pv[e]) for e in range(2))

        init = tuple((jnp.full((t, 1), NEG, F32), jnp.zeros((t, 2 * FOX_DH), F32)) for _ in range(2))
        carry = lax.fori_loop(0, i, lambda j, c: step(j, c, False), init)
        carry = step(i, carry, True)
        lane = _iota2((t, 2 * FOX_DH), 1)
        outs, lses = [], []
        for e in range(2):
            m, acc = carry[e]
            l = jnp.sum(jnp.where(lane == _extra_lane(e, 0), acc, 0.0), axis=1, keepdims=True)
            outs.append(acc / l)
            lses.append(m + jnp.log(l))
        o_ref[...] = jnp.where(masks[0], outs[0], outs[1])
        head0 = 2 * pl.program_id(0)
        lse_ref[...] = jnp.where(lane == head0, lses[0], jnp.where(lane == head0 + 1, lses[1], 0.0))

    pr = FOX_PAIRS
    return pl.pallas_call(
        kern, name="fox_fwd", grid=(pr, n),
        in_specs=[pl.BlockSpec((t, 128), lambda p, i: (i, p)),
                  pl.BlockSpec((s, 128), lambda p, i: (0, pr + p)),
                  pl.BlockSpec((s, 128), lambda p, i: (0, 2 * pr + p)),
                  pl.BlockSpec((None, t, 128), lambda p, i: (p, i, 0)),
                  pl.BlockSpec((None, s, 128), lambda p, i: (p, 0, 0)),
                  pl.BlockSpec((None, s, 128), lambda p, i: (p, 0, 0))],
        out_specs=[pl.BlockSpec((t, 128), lambda p, i: (i, p)),
                   pl.BlockSpec((None, t, 128), lambda p, i: (p, i, 0))],
        out_shape=[jax.ShapeDtypeStruct((s, FOX_HEADS * FOX_DH), F32), jax.ShapeDtypeStruct((pr, s, 128), F32)],
        compiler_params=_params(("parallel", "parallel")),
    )(qkv, qkv, qkv, xq, xk, xv)


def fox_bwd(qkv, d_o, xk, xv, xqb, xdo):
    s = qkv.shape[0]
    t = min(FOX_T, s)
    n = s // t
    w = 2 * FOX_DH

    def both(blocks, slot):
        lane = _iota2(blocks[0].shape, 1)
        head0 = 2 * pl.program_id(0)
        own = jnp.where(lane < FOX_DH, blocks[0], blocks[1])
        sums = [jnp.sum(jnp.where(lane == _extra_lane(e, slot), blocks[e], 0.0), axis=1, keepdims=True)
                for e in range(2)]
        return own, jnp.where(lane == head0, sums[0], jnp.where(lane == head0 + 1, sums[1], 0.0))

    def kern(k_ref, v_ref, xk_ref, xv_ref, q_ref, do_ref, xq_ref, xd_ref,
             dq_ref, dk_ref, dv_ref, sq_ref, sk_ref, dq_acc):
        j = pl.program_id(1)

        @pl.when(j == 0)
        def _():
            dq_acc[...] = jnp.zeros_like(dq_acc)

        masks = _head_masks(t)
        k_ops = [jnp.where(mk, k_ref[...], xk_ref[...]) for mk in masks]
        v_ops = [jnp.where(mk, v_ref[...], xv_ref[...]) for mk in masks]
        k_t = [x.T for x in k_ops]

        def step(i, carry, masked):
            dk, dv = carry
            sl = pl.ds(pl.multiple_of(i * t, t), t)
            q_pair, xq_pair, do_pair, xd_pair = q_ref[sl, :] * FOX_SCALE, xq_ref[sl, :], do_ref[sl, :], xd_ref[sl, :]
            q_ops = [jnp.where(mk, q_pair, xq_pair) for mk in masks]
            do_ops = [jnp.where(mk, do_pair, xd_pair) for mk in masks]
            q_t = [x.T for x in q_ops]
            do_t = [jnp.where(mk, do_pair, 0).astype(BF).T for mk in masks]
            st = [lax.dot_general(k_ops[e], q_ops[e], _NT, preferred_element_type=F32) for e in range(2)]
            dp = [lax.dot_general(v_ops[e], do_ops[e], _NT, preferred_element_type=F32) for e in range(2)]
            if masked:
                keep = _iota2((t, t), 0) <= _iota2((t, t), 1)
                st = [jnp.where(keep, x, NEG) for x in st]
            pt = [jnp.exp(x) for x in st]
            dsb = [(pt[e] * dp[e]).astype(BF) for e in range(2)]
            dv = dv + sum(lax.dot_general(do_t[e], pt[e].astype(BF), _NT, preferred_element_type=F32)
                          for e in range(2))
            dk = tuple(dk[e] + lax.dot_general(q_t[e], dsb[e], _NT, preferred_element_type=F32) for e in range(2))
            for e in range(2):
                dq_acc[i, e * w:(e + 1) * w, :] += jnp.dot(k_t[e], dsb[e], preferred_element_type=F32)
            return dk, dv

        init = ((jnp.zeros((w, t), F32), jnp.zeros((w, t), F32)), jnp.zeros((w, t), F32))
        carry = step(j, init, True)
        dk, dv = lax.fori_loop(j + 1, n, lambda i, c: step(i, c, False), carry)
        dk_ref[...], sk_ref[...] = both([x.T for x in dk], 3)
        dv_ref[...] = dv.T

        @pl.when(j == n - 1)
        def _():
            def out(r, carry):
                sl = pl.ds(pl.multiple_of(r * t, t), t)
                own, sums = both([dq_acc[r, e * w:(e + 1) * w, :].T for e in range(2)], 0)
                dq_ref[sl, :] = own * FOX_SCALE
                sq_ref[sl, :] = sums
                return carry

            lax.fori_loop(0, n, out, 0)

    pr = FOX_PAIRS
    flat = jax.ShapeDtypeStruct((s, FOX_HEADS * FOX_DH), F32)
    tile = pl.BlockSpec((t, 128), lambda p, j: (j, p))
    whole = pl.BlockSpec((s, 128), lambda p, j: (0, p))
    return pl.pallas_call(
        kern, name="fox_bwd", grid=(pr, n),
        in_specs=[pl.BlockSpec((t, 128), lambda p, j: (j, pr + p)),
                  pl.BlockSpec((t, 128), lambda p, j: (j, 2 * pr + p)),
                  pl.BlockSpec((None, t, 128), lambda p, j: (p, j, 0)),
                  pl.BlockSpec((None, t, 128), lambda p, j: (p, j, 0)),
                  whole, whole,
                  pl.BlockSpec((None, s, 128), lambda p, j: (p, 0, 0)),
                  pl.BlockSpec((None, s, 128), lambda p, j: (p, 0, 0))],
        out_specs=[whole, tile, tile, whole, tile],
        out_shape=[flat] * 5,
        scratch_shapes=[pltpu.VMEM((n, 2 * w, t), F32)],
        compiler_params=_params(("parallel", "arbitrary")),
    )(qkv, qkv, xk, xv, qkv, d_o, xqb, xdo)


def _xattn_head(q, k, v):
    sc = bdot(q, k, 1, 1) * (MEM_DH ** -0.5)
    e = jnp.exp(sc - lax.stop_gradient(jnp.max(sc, axis=-1, keepdims=True)))
    p = e / jnp.sum(e, axis=-1, keepdims=True)
    return bdot(p, v, 1, 0)


def xattn_fwd(q, kv):
    s = q.shape[0]
    hh = MEM_HEADS

    def body(*vals):
        qs, ks, vs = vals[:hh], vals[hh:2 * hh], vals[2 * hh:]
        return jnp.concatenate([_xattn_head(qs[a], ks[a], vs[a]) for a in range(hh)], axis=1)

    return rowcall(body, [(q, MEM_DH, a) for a in range(hh)],
                   [(kv, MEM_DH, a) for a in range(2 * hh)],
                   [(hh * MEM_DH, BF)], rows=512, total=s, name="xattn_fwd")[0]


def xattn_bwd(q, kv, d_o):
    s = q.shape[0]
    hh = MEM_HEADS

    def body(*vals):
        qs, dos = vals[:hh], vals[hh:2 * hh]
        ks, vs = vals[2 * hh:3 * hh], vals[3 * hh:]
        dqs, dks, dvs = [], [], []
        for a in range(hh):
            _, vjp = jax.vjp(_xattn_head, qs[a], ks[a], vs[a])
            dq, dk, dv = vjp(dos[a])
            dqs.append(dq)
            dks.append(dk)
            dvs.append(dv)
        return jnp.concatenate(dqs, axis=1), jnp.concatenate(dks + dvs, axis=1)

    return rowcall(body, [(q, MEM_DH, a) for a in range(hh)] + [(d_o, MEM_DH, a) for a in range(hh)],
                   [(kv, MEM_DH, a) for a in range(2 * hh)],
                   [(hh * MEM_DH, BF)], [kv.shape], rows=512, total=s, name="xattn_bwd")


def _slab(ref, axis, start, size):
    if axis is None:
        return ref
    if axis == "lead":
        return ref.at[start]
    idx = pl.ds(pl.multiple_of(start, 128 if axis == 1 else 16), size)
    return ref.at[idx] if axis == 0 else ref.at[:, idx]


def exchange(inputs, outputs, transfers, name):
    ni, no, nt = len(inputs), len(outputs), len(transfers)
    npeer = N_DEV - 1

    def body(*refs):
        ins, outs = refs[:ni], refs[ni:ni + no]
        send, recv, loc = refs[ni + no:]
        x, y, c = lax.axis_index("x"), lax.axis_index("y"), lax.axis_index("c")
        me = 4 * x + 2 * y + c

        def peer(p):
            px = 1 - x if p & 4 else x
            py = 1 - y if p & 2 else y
            pc = 1 - c if p & 1 else c
            return (px, py, pc), 4 * px + 2 * py + pc

        def view(ref, spec, who):
            axis, off, stride, size = spec
            return _slab(ref, axis, off + who * stride, size)

        local, remote = [], []
        for w, (ii, src, oi, dst) in enumerate(transfers):
            cp = pltpu.make_async_copy(view(ins[ii], src, me), view(outs[oi], dst, me), loc.at[w])
            cp.start()
            local.append(cp)
        for p in range(1, N_DEV):
            dev, idx = peer(p)
            for w, (ii, src, oi, dst) in enumerate(transfers):
                k = w * npeer + p - 1
                out_cp = pltpu.make_async_remote_copy(
                    src_ref=view(ins[ii], src, idx), dst_ref=view(outs[oi], dst, me), send_sem=send.at[k],
                    recv_sem=recv.at[k], device_id=dev, device_id_type=MESH)
                out_cp.start()
                in_cp = pltpu.make_async_remote_copy(
                    src_ref=view(ins[ii], src, idx), dst_ref=view(outs[oi], dst, idx), send_sem=send.at[k],
                    recv_sem=recv.at[k], device_id=dev, device_id_type=MESH)
                remote.append((out_cp, in_cp))
        for out_cp, in_cp in remote:
            in_cp.wait_recv()
            out_cp.wait_send()
        for cp in local:
            cp.wait()

    hbm = pl.BlockSpec(memory_space=pl.ANY)
    return pl.pallas_call(
        body, name=name, in_specs=[hbm] * ni, out_specs=[hbm] * no, out_shape=list(outputs),
        scratch_shapes=[pltpu.SemaphoreType.DMA((nt * npeer,)), pltpu.SemaphoreType.DMA((nt * npeer,)),
                        pltpu.SemaphoreType.DMA((nt,))],
        compiler_params=pltpu.CompilerParams(has_side_effects=True),
    )(*inputs)


def _peer(p):
    x, y, c = lax.axis_index("x"), lax.axis_index("y"), lax.axis_index("c")
    px = 1 - x if p & 4 else x
    py = 1 - y if p & 2 else y
    pc = 1 - c if p & 1 else c
    return (px, py, pc), 4 * px + 2 * py + pc


def _view(ref, spec, who):
    axis, off, stride, size = spec
    return _slab(ref, axis, off + who * stride, size)


def place_own(inputs, outputs, transfers):
    me = 4 * lax.axis_index("x") + 2 * lax.axis_index("y") + lax.axis_index("c")
    lands = [lax.empty(o.shape, o.dtype) for o in outputs]
    for ii, src, oi, dst in transfers:
        axis, off, stride, size = src
        own = inputs[ii] if axis is None else lax.dynamic_slice_in_dim(inputs[ii], off + me * stride, size, axis)
        axis, off, stride, size = dst
        if axis == "lead":
            lands[oi] = lax.dynamic_update_slice_in_dim(lands[oi], own[None], me, 0)
        else:
            lands[oi] = lax.dynamic_update_slice_in_dim(lands[oi], own, off + me * stride, axis)
    return lands


_HBM = pl.BlockSpec(memory_space=pltpu.HBM)
_SEM = pl.BlockSpec(memory_space=pltpu.SEMAPHORE)
_EFFECT = pltpu.SideEffectType.DATAFLOW_SIDE_EFFECTING


def _remote_copies(ins, lands, transfers, send, recv):
    npeer = N_DEV - 1
    me = 4 * lax.axis_index("x") + 2 * lax.axis_index("y") + lax.axis_index("c")
    pairs = []
    for p in range(1, N_DEV):
        dev, idx = _peer(p)
        for w, (ii, src, oi, dst) in enumerate(transfers):
            k = w * npeer + p - 1
            common = dict(src_ref=_view(ins[ii], src, idx), send_sem=send.at[k], recv_sem=recv.at[k],
                          device_id=dev, device_id_type=MESH)
            pairs.append((pltpu.make_async_remote_copy(dst_ref=_view(lands[oi], dst, me), **common),
                          pltpu.make_async_remote_copy(dst_ref=_view(lands[oi], dst, idx), **common)))
    return pairs


def exchange_start(inputs, lands, transfers, after, name):
    ni, nl, nsem = len(inputs), len(lands), len(transfers) * (N_DEV - 1)

    def body(*refs):
        ins, lnd = refs[:ni], refs[ni:ni + nl]
        send, recv = refs[ni + nl + 1], refs[ni + nl + 2]
        token = refs[-1]
        for out_cp, _ in _remote_copies(ins, lnd, transfers, send, recv):
            out_cp.start()
        token[...] = jnp.zeros_like(token)

    args = [pltpu.with_memory_space_constraint(a, pltpu.HBM) for a in list(inputs) + list(lands)]
    res = pl.pallas_call(
        body, name=name,
        out_shape=(pltpu.SemaphoreType.DMA((nsem,)), pltpu.SemaphoreType.DMA((nsem,)),
                   *[pltpu.HBM(a.shape, a.dtype) for a in args], jax.ShapeDtypeStruct((8, 128), F32)),
        in_specs=[_HBM] * (ni + nl) + [pl.BlockSpec(memory_space=pl.ANY)],
        out_specs=(_SEM, _SEM, *[_HBM] * (ni + nl), pl.BlockSpec(memory_space=pltpu.VMEM)),
        input_output_aliases={k: k + 2 for k in range(ni + nl)},
        compiler_params=pltpu.CompilerParams(has_side_effects=_EFFECT),
    )(*args, after)
    return res[0], res[1], list(res[2:2 + ni]), list(res[2 + ni:2 + ni + nl]), res[-1]


def exchange_wait(send, recv, inputs, lands, after, transfers, name):
    ni, nl = len(inputs), len(lands)

    def body(*refs):
        ins, lnd = refs[:ni], refs[ni:ni + nl]
        send_r, recv_r = refs[ni + nl], refs[ni + nl + 1]
        for out_cp, in_cp in _remote_copies(ins, lnd, transfers, send_r, recv_r):
            out_cp.wait_send()
            in_cp.wait_recv()

    res = pl.pallas_call(
        body, name=name,
        out_shape=tuple(pltpu.HBM(a.shape, a.dtype) for a in list(inputs) + list(lands)),
        in_specs=[_HBM] * (ni + nl) + [_SEM, _SEM, pl.BlockSpec(memory_space=pl.ANY)],
        out_specs=tuple([_HBM] * (ni + nl)),
        input_output_aliases={k: k for k in range(ni + nl)},
        compiler_params=pltpu.CompilerParams(has_side_effects=_EFFECT),
    )(*inputs, *lands, send, recv, after)
    return list(res[ni:])


def adamw(w, m, v, contribs, name):
    r, c = w.shape
    nc = len(contribs)
    rows = next((r // d for d in (4, 2) if r % d == 0 and (r // d) % 16 == 0), r)
    c1, c2 = 1.0 - ADAM_B1 ** ADAM_STEP, 1.0 - ADAM_B2 ** ADAM_STEP

    def body(wv, mv, vv, *gs):
        g = gs[0].astype(F32)
        for extra in gs[1:]:
            g = g + extra.astype(F32)
        g = g[:, :c]
        m_new = ADAM_B1 * mv + (1.0 - ADAM_B1) * g
        v_new = ADAM_B2 * vv + (1.0 - ADAM_B2) * (g * g)
        delta = -ADAM_LR * ((m_new / c1) / (jnp.sqrt(v_new / c2) + ADAM_EPS) + ADAM_WD * wv)
        return g, delta, m_new, v_new

    assert nc >= 1
    return rowcall(body, [w, m, v] + list(contribs), [], [(c, F32)] * 4, rows=rows, total=r, name=name)


WEIGHTS = ['ffn1_pre_norm', 'ffn1_w_gate', 'ffn1_w_up', 'ffn1_w_down', 'ffn1_post_norm', 'mix_pre_norm', 'w_in',
           'fox_f_bias', 'gdn_conv_w', 'gdn_a_log', 'gdn_dt_bias', 'gdn_out_norm', 'w_out', 'mix_post_norm',
           'mem_pre_norm', 'mem_kv_norm', 'mem_w_q', 'mem_w_kv', 'mem_w_o', 'mem_post_norm', 'ffn2_pre_norm',
           'ffn2_w_gate', 'ffn2_w_up', 'ffn2_w_down', 'ffn2_post_norm']
GAINS = ['ffn1_pre_norm', 'ffn1_post_norm', 'mix_pre_norm', 'mix_post_norm', 'mem_pre_norm', 'mem_kv_norm',
         'mem_post_norm', 'ffn2_pre_norm', 'ffn2_post_norm']
BIG = ['ffn1_w_gate', 'ffn1_w_up', 'ffn1_w_down', 'w_in', 'w_out', 'mem_w_q', 'mem_w_kv', 'mem_w_o',
       'ffn2_w_gate', 'ffn2_w_up', 'ffn2_w_down']
PACK_ROWS = 24
ROW_MISC = len(GAINS)
ROW_CONV = ROW_MISC + 1
COL_FBIAS, COL_ALOG, COL_DTB, COL_ONORM, COL_LOSS = 0, 8, 12, 128, 256
CONV_CH = 3 * GDN_HEADS * GDN_DH


def _pad_to(a, shape):
    return jnp.pad(a, [(0, t - s) for s, t in zip(a.shape, shape)])


def _pack(get, conv=None, loss=None):
    rows = [get(nm) for nm in GAINS]
    misc = jnp.concatenate([get('fox_f_bias'), get('gdn_a_log'), get('gdn_dt_bias'),
                            jnp.zeros((1, COL_ONORM - COL_DTB - 4), F32), get('gdn_out_norm'),
                            jnp.zeros((1, 1), F32) if loss is None else loss.reshape(1, 1)], axis=1)
    rows.append(_pad_to(misc, (1, D_MODEL)))
    rows.append(jnp.zeros((6, D_MODEL), F32) if conv is None else conv.reshape(6, D_MODEL))
    return _pad_to(jnp.concatenate(rows, axis=0), (PACK_ROWS, D_MODEL))


def _unpack(p):
    out = {nm: p[i:i + 1] for i, nm in enumerate(GAINS)}
    misc = p[ROW_MISC:ROW_MISC + 1]
    out['fox_f_bias'] = misc[:, COL_FBIAS:COL_FBIAS + FOX_HEADS]
    out['gdn_a_log'] = misc[:, COL_ALOG:COL_ALOG + GDN_HEADS]
    out['gdn_dt_bias'] = misc[:, COL_DTB:COL_DTB + GDN_HEADS]
    out['gdn_out_norm'] = misc[:, COL_ONORM:COL_ONORM + GDN_DH]
    return out


def _ffn_fwd(h, pre, wgu, wd, tag):
    s = h.shape[0]
    u, = rowcall(_rms, [h], [pre], [(D_MODEL, BF)], rows=512, total=s, name=tag + "_pre")
    if callable(wgu):
        wgu = wgu(u)
    gate, up, act = mm_swiglu(u, wgu, name=tag + "_gate_up")
    if callable(wd):
        wd = wd(act)
    f = mm(act, wd, name=tag + "_down")
    return u, gate, up, act, f


def _half_rms(a, g):
    return 0.5 * _rms(a, g)


def _ffn_bwd(dh_out, h, pre, post, wgu, wd, saved, tag, on_dwd=None, on_dwgu=None):
    u, gate, up, act, f = saved
    s = h.shape[0]

    def b_post(dh, fv, pg):
        return jax.vjp(_half_rms, fv, pg)[1](dh)

    df, dpost = rowcall(b_post, [dh_out, f], [post], [(D_MODEL, BF)], [(1, D_MODEL)], rows=512, total=s,
                        name=tag + "_bwd_post")
    dwd = mm(act, df, ta=True, out_dtype=BF, name=tag + "_bwd_dwd")
    dgate, dup = mm_dswiglu(df, wd, gate, up, name=tag + "_bwd_dact", token=on_dwd(dwd) if on_dwd else None)
    dwg = mm(u, dgate, ta=True, out_dtype=BF, name=tag + "_bwd_dwg")
    dwu = mm(u, dup, ta=True, out_dtype=BF, name=tag + "_bwd_dwu")
    du = mm_pair(dgate, dup, wgu, name=tag + "_bwd_du", token=on_dwgu(dwg, dwu) if on_dwgu else None)

    def b_pre(dh, duv, hv, pg):
        dx, dpre = jax.vjp(_rms, hv, pg)[1](duv)
        return dh + dx, dpre

    dh, dpre = rowcall(b_pre, [dh_out, du, h], [pre], [(D_MODEL, F32)], [(1, D_MODEL)], rows=512, total=s,
                       name=tag + "_bwd_pre")
    return dh, dwg, dwu, dwd, dpre, dpost


def _residual_rms(h, a, g):
    return h + _rms(a, g)


def _bwd_residual(dh, a, g):
    return jax.vjp(_rms, a, g)[1](dh)


def _step(a):
    x, mem = a['x'][0], a['mem'][0]
    s = x.shape[0]
    me = 4 * lax.axis_index("x") + 2 * lax.axis_index("y") + lax.axis_index("c")
    w2 = {nm: a[nm][0] for nm in WEIGHTS}
    m2 = {nm: a['m_' + nm][0] for nm in WEIGHTS}
    v2 = {nm: a['v_' + nm][0] for nm in WEIGHTS}
    small = {nm: w2[nm][None] for nm in WEIGHTS if nm not in BIG and nm != 'gdn_conv_w'}

    def ff_cols(w):
        return _pad_to(w, (D_MODEL, FF_SHARD_PAD)).astype(BF)

    def ff_rows(w):
        return _pad_to(w, (FF_SHARD_PAD, D_MODEL)).astype(BF)

    whole = (None, 0, 0, 0)
    conv_pad = 256
    g_in = [ff_cols(w2['ffn1_w_gate']), ff_cols(w2['ffn1_w_up']), ff_rows(w2['ffn1_w_down']),
            ff_cols(w2['ffn2_w_gate']), ff_cols(w2['ffn2_w_up']), ff_rows(w2['ffn2_w_down']),
            _pad_to(w2['w_in'], (D_MODEL, IN_SHARD_PAD)).astype(BF), w2['w_out'].astype(BF),
            w2['mem_w_q'].astype(BF), w2['mem_w_kv'].astype(BF), w2['mem_w_o'].astype(BF),
            _pad_to(w2['gdn_conv_w'], (8, conv_pad))]
    g_out = [jax.ShapeDtypeStruct((D_MODEL, 2 * D_FF_PAD), BF), jax.ShapeDtypeStruct((D_FF_PAD, D_MODEL), BF),
             jax.ShapeDtypeStruct((D_MODEL, 2 * D_FF_PAD), BF), jax.ShapeDtypeStruct((D_FF_PAD, D_MODEL), BF),
             jax.ShapeDtypeStruct((D_MODEL, N_DEV * IN_SHARD_PAD), BF), jax.ShapeDtypeStruct((D_MODEL, D_MODEL), BF),
             jax.ShapeDtypeStruct((D_MODEL, D_MODEL), BF), jax.ShapeDtypeStruct((D_MODEL, 2 * D_MODEL), BF),
             jax.ShapeDtypeStruct((D_MODEL, D_MODEL), BF), jax.ShapeDtypeStruct((8, N_DEV * conv_pad), F32)]
    sp_, dm = FF_SHARD_PAD, D_MODEL // N_DEV
    g_tr = [(0, whole, 0, (1, 0, sp_, sp_)), (1, whole, 0, (1, D_FF_PAD, sp_, sp_)), (2, whole, 1, (0, 0, sp_, sp_)),
            (3, whole, 2, (1, 0, sp_, sp_)), (4, whole, 2, (1, D_FF_PAD, sp_, sp_)), (5, whole, 3, (0, 0, sp_, sp_)),
            (6, whole, 4, (1, 0, IN_SHARD_PAD, IN_SHARD_PAD)), (7, whole, 5, (0, 0, dm, dm)),
            (8, whole, 6, (0, 0, dm, dm)), (9, whole, 7, (1, 0, 2 * dm, 2 * dm)), (10, whole, 8, (0, 0, dm, dm)),
            (11, whole, 9, (1, 0, conv_pad, conv_pad))]
    def pick(idx):
        ins = sorted({g_tr[k][0] for k in idx})
        outs = sorted({g_tr[k][2] for k in idx})
        tr = [(ins.index(g_tr[k][0]), g_tr[k][1], outs.index(g_tr[k][2]), g_tr[k][3]) for k in idx]
        return [g_in[i] for i in ins], [g_out[o] for o in outs], tr

    stages, after = [], g_in[0]
    for nm, idx in (("gate_up", [0, 1]), ("down", [2]), ("mix", [6, 7, 11]), ("late", [8, 9, 10, 3, 4, 5])):
        st_in, st_out, st_tr = pick(idx)
        st = exchange_start(st_in, place_own(st_in, st_out, st_tr), st_tr, after, "gather_%s_start" % nm)
        stages.append((st, st_tr, "gather_%s_wait" % nm))
        after = st[4]
    g_token = after

    def gather_wait(k, after_):
        (send_, recv_, src_, land_, _), tr_, nm_ = stages[k]
        return exchange_wait(send_, recv_, src_, land_, after_, tr_, nm_)

    bias_row = _pad_to(small['fox_f_bias'], (1, 128))
    gate_prm = _pad_to(jnp.concatenate([_pad_to(small['gdn_a_log'], (1, 128 - SMALL_A)),
                                        _pad_to(small['gdn_dt_bias'], (1, 128 - SMALL_A))], axis=0),
                       (8, 128 - SMALL_A))
    gate_prm = jnp.pad(gate_prm, ((0, 0), (SMALL_A, 0)))
    onorm = small['gdn_out_norm']

    late = {}

    def wgu1_when(u):
        late['wgu1'], = gather_wait(0, u)
        return late['wgu1']

    def wd1_when(act):
        late['wd1'], = gather_wait(1, act)
        return late['wd1']

    sv1 = _ffn_fwd(x, small['ffn1_pre_norm'] + g_token[0, 0], wgu1_when, wd1_when, "ffn1")
    wgu1, wd1 = late['wgu1'], late['wd1']
    def b_out_pre(h, f, g_post, g_pre):
        hn = h + _half_rms(f, g_post)
        return hn, _rms(hn, g_pre)

    h1, u2 = rowcall(b_out_pre, [x, sv1[4]], [small['ffn1_post_norm'], small['mix_pre_norm']],
                     [(D_MODEL, F32), (D_MODEL, BF)], rows=512, total=s, name="ffn1_out")
    w_in_g, w_out, conv_g = gather_wait(2, h1)
    w_in = jnp.concatenate([w_in_g[:, j * IN_SHARD_PAD:j * IN_SHARD_PAD + IN_SHARD] for j in range(N_DEV)],
                           axis=1)
    sp = [0, 512, 1024, 1536, 1544, 2056, 2568, 3080, 3592, 3596, 3600]
    fq, fk, fv, ff, gq, gk, gv, gz, gb, ga = [w_in[:, sp[i]:sp[i + 1]] for i in range(10)]
    w_proj = jnp.concatenate([fq, fk, fv, gq, gk, gv, gz, ff, gb, ga,
                              jnp.zeros((D_MODEL, PROJ_W - 3584 - 16), BF)], axis=1)
    conv_w8 = conv_g.reshape(8, N_DEV, conv_pad)[:, :, :CONV_CH // N_DEV].reshape(8, CONV_CH)


    proj = mm(u2, w_proj, name="mix_proj")
    f_cum = fox_f_fwd(proj, bias_row)
    f_heads = f_cum[:, :FOX_HEADS]
    qkv_bf = proj[:, :3 * FOX_HEADS * FOX_DH].astype(BF)
    xk, xv = _fox_extras(s, 1.0, -f_heads), _fox_extras(s, 1.0, None)
    fox_flat, lse = fox_fwd(qkv_bf, _fox_extras(s, f_heads, 1.0), xk, xv)
    lse_heads = jnp.sum(lse, axis=0)[:, :FOX_HEADS]
    cqkv = conv_fwd(proj, conv_w8)
    g_l, b_l = rowcall(_gdn_gates, [(proj, 128, SMALL_BLOCK128)], [gate_prm], [(512, F32), (512, F32)],
                       rows=512, total=s, name="gdn_gates")
    gbb = jnp.concatenate([g_l, b_l], axis=1)
    gdn_o, states = gdn_fwd(cqkv, proj, gbb, onorm)
    mixed = jnp.concatenate([fox_flat, gdn_o], axis=1).astype(BF)
    mo = mm(mixed, w_out, name="mix_out")
    def b_res_pre(h, a_, g_post, g_pre):
        hn = h + _rms(a_, g_post)
        return hn, _rms(hn, g_pre)

    h2, hq = rowcall(b_res_pre, [h1, mo], [small['mix_post_norm'], small['mem_pre_norm']],
                     [(D_MODEL, F32), (D_MODEL, BF)], rows=512, total=s, name="mix_res")
    mn, = rowcall(_rms, [mem], [small['mem_kv_norm']], [(D_MODEL, BF)], rows=256, total=mem.shape[0], name="mem_kvn")
    wgu2, wd2, w_q, w_kv, w_o = gather_wait(3, h2)
    q_mem = mm(hq, w_q, name="mem_q")
    kv_mem = mm(mn, w_kv, name="mem_kv")
    o_mem = xattn_fwd(q_mem, kv_mem)
    c_mem = mm(o_mem, w_o, name="mem_o")
    h3, = rowcall(_residual_rms, [h2, c_mem], [small['mem_post_norm']], [(D_MODEL, F32)], rows=512, total=s,
                  name="mem_res")

    sv2 = _ffn_fwd(h3, small['ffn2_pre_norm'], wgu2, wd2, "ffn2")

    def b_loss(h, f, tgt, g):
        err = h + _half_rms(f, g) - tgt
        part = 0.5 * jnp.sum(jnp.mean(err * err, axis=-1, keepdims=True), axis=0, keepdims=True)
        return err * (1.0 / D_MODEL), jnp.broadcast_to(part, (1, 128))

    dy, loss_acc = rowcall(b_loss, [h3, sv2[4], a['loss_target'][0]], [small['ffn2_post_norm']], [(D_MODEL, F32)],
                           [(1, 128)], rows=512, total=s, name="loss")

    grads = {}
    dh3, dwg2, dwu2, dwd2, grads['ffn2_pre_norm'], grads['ffn2_post_norm'] = _ffn_bwd(
        dy, h3, small['ffn2_pre_norm'], small['ffn2_post_norm'], wgu2, wd2, sv2, "ffn2")

    lead = ("lead", 0, 1, 0)

    def land(r, c, dt=BF):
        return jax.ShapeDtypeStruct((N_DEV, r, c), dt)

    ffn_tr = [(0, (1, 0, sp_, sp_), 0, lead), (1, (1, 0, sp_, sp_), 1, lead), (2, (0, 0, sp_, FF_SHARD), 2, lead)]
    ffn_land = [land(D_MODEL, sp_), land(D_MODEL, sp_), land(FF_SHARD, D_MODEL)]
    a_in = [dwg2, dwu2, dwd2]
    a_send, a_recv, a_src, a_land, a_token = exchange_start(a_in, place_own(a_in, ffn_land, ffn_tr), ffn_tr, dh3,
                                                            "reduce_ffn2_start")

    dc, grads['mem_post_norm'] = rowcall(_bwd_residual, [dh3, c_mem], [small['mem_post_norm'] + a_token[0, 0]],
                                         [(D_MODEL, BF)],
                                         [(1, D_MODEL)], rows=512, total=s, name="mem_bwd_res")
    d_o = mm(dc, w_o, tb=True, name="mem_bwd_do")
    dw_o = mm(o_mem, dc, ta=True, out_dtype=BF, name="mem_bwd_dwo")
    dq_mem, dkv = xattn_bwd(q_mem, kv_mem, d_o)
    dhq = mm(dq_mem, w_q, tb=True, name="mem_bwd_dhq")
    dw_q = mm(hq, dq_mem, ta=True, out_dtype=BF, name="mem_bwd_dwq")
    dmn = mm(dkv, w_kv, tb=True, name="mem_bwd_dmn")
    dw_kv = mm(mn, dkv, ta=True, out_dtype=BF, name="mem_bwd_dwkv")
    _, grads['mem_kv_norm'] = rowcall(lambda d, mv, g: jax.vjp(_rms, mv, g)[1](d), [dmn, mem],
                                      [small['mem_kv_norm']], [(D_MODEL, F32)], [(1, D_MODEL)], rows=256,
                                      total=mem.shape[0], name="mem_bwd_kvn")

    def b_pre(dh, duv, hv, pg):
        dx, dpre = jax.vjp(_rms, hv, pg)[1](duv)
        return dh + dx, dpre

    dh2, grads['mem_pre_norm'] = rowcall(b_pre, [dh3, dhq, h2], [small['mem_pre_norm']], [(D_MODEL, F32)],
                                         [(1, D_MODEL)], rows=512, total=s, name="mem_bwd_pre")

    dmo, grads['mix_post_norm'] = rowcall(_bwd_residual, [dh2, mo], [small['mix_post_norm']], [(D_MODEL, BF)],
                                          [(1, D_MODEL)], rows=512, total=s, name="mix_bwd_res")
    d_mixed = mm(dmo, w_out, tb=True, name="mix_bwd_dmixed")
    dw_out = mm(mixed, dmo, ta=True, out_dtype=BF, name="mix_bwd_dwout")
    def b_delta(do, o):
        sel = (_iota2((512, 128), 0) // FOX_DH == _iota2((512, 128), 1)).astype(F32)
        return mdot(do * o, sel)

    delta, = rowcall(b_delta, [(d_mixed, 512, 0), fox_flat], [], [(128, F32)], rows=512, total=s, name="fox_delta")
    dfox_q, dfox_k, dvf, sum_q, sum_k = fox_bwd(qkv_bf, d_mixed[:, :512].astype(BF), xk, xv,
                                                _fox_extras(s, f_heads - lse_heads, 1.0),
                                                _fox_extras(s, -delta[:, :FOX_HEADS], None))
    d_f = jnp.sum((sum_q - sum_k).reshape(s, FOX_PAIRS, 2 * FOX_DH), axis=1)
    dsmall_f, dbias = fox_f_bwd(proj, bias_row, d_f)
    grads['fox_f_bias'] = dbias[:, :FOX_HEADS]
    dcqkv, dz, dgb, grads['gdn_out_norm'] = gdn_bwd(cqkv, proj, gbb, onorm, states, d_mixed)

    def b_gates(sm, dsf, dg, db, prm):
        dsm, dprm = jax.vjp(_gdn_gates, sm, prm)[1]((dg, db))
        return dsm + dsf, dprm

    dsmall, dprm = rowcall(b_gates, [(proj, 128, SMALL_BLOCK128), dsmall_f, (dgb, 512, 0), (dgb, 512, 1)], [gate_prm],
                           [(128, F32)],
                           [(8, 128)], rows=512, total=s, name="gdn_bwd_gates")
    grads['gdn_a_log'] = dprm[0:1, SMALL_A:SMALL_A + GDN_HEADS]
    grads['gdn_dt_bias'] = dprm[1:2, SMALL_A:SMALL_A + GDN_HEADS]
    dqkv_pre, dconv8 = conv_bwd(proj, conv_w8, dcqkv)
    dproj = jnp.concatenate([dfox_q, dfox_k, dvf, dqkv_pre, dz, dsmall,
                             jnp.zeros((s, PROJ_W - 3584 - 128), F32)], axis=1).astype(BF)
    du2 = mm(dproj, w_proj, tb=True, name="mix_bwd_du")
    dw_proj = mm(u2, dproj, ta=True, out_dtype=BF, name="mix_bwd_dwproj")
    dh1, grads['mix_pre_norm'] = rowcall(b_pre, [dh2, du2, h1], [small['mix_pre_norm']], [(D_MODEL, F32)],
                                         [(1, D_MODEL)], rows=512, total=s, name="mix_bwd_pre")

    dw_in = jnp.concatenate([dw_proj[:, :1536], dw_proj[:, 3584:3592], dw_proj[:, 1536:3584],
                             dw_proj[:, 3592:3600]], axis=1)
    gap = jnp.zeros((D_MODEL, IN_SHARD_PAD - IN_SHARD), BF)
    dw_in = jnp.concatenate([piece for j in range(N_DEV) for piece in (dw_in[:, j * IN_SHARD:(j + 1) * IN_SHARD], gap)],
                            axis=1)
    b_in = [dw_in, dw_out, dw_q, dw_kv, dw_o]
    b_tr = [(0, (1, 0, IN_SHARD_PAD, IN_SHARD_PAD), 0, lead), (1, (0, 0, dm, dm), 1, lead), (2, (0, 0, dm, dm), 2, lead),
            (3, (1, 0, 2 * dm, 2 * dm), 3, lead), (4, (0, 0, dm, dm), 4, lead)]
    b_shapes = [land(D_MODEL, IN_SHARD_PAD), land(dm, D_MODEL), land(dm, D_MODEL), land(D_MODEL, 2 * dm),
                land(dm, D_MODEL)]
    b_land = place_own(b_in, b_shapes, b_tr)
    b_send, b_recv, b_src, b_land, b_token = exchange_start(b_in, b_land, b_tr, dh1, "reduce_mix_start")

    def start_down_reduce(dwd):
        tr = ffn_tr[2:]
        tr = [(0, tr[0][1], 0, tr[0][3])]
        late['c_down'] = (exchange_start([dwd], place_own([dwd], ffn_land[2:], tr), tr, dwd, "reduce_ffn1_down_start"), tr)
        return late['c_down'][0][4]

    def start_gate_up_reduce(dwg, dwu):
        tr = ffn_tr[:2]
        late['c_gu'] = (exchange_start([dwg, dwu], place_own([dwg, dwu], ffn_land[:2], tr), tr, dwu,
                                       "reduce_ffn1_gu_start"), tr)
        return late['c_gu'][0][4]

    grad_x, _, _, _, grads['ffn1_pre_norm'], grads['ffn1_post_norm'] = _ffn_bwd(
        dh1, x, small['ffn1_pre_norm'], small['ffn1_post_norm'] + b_token[0, 0], wgu1, wd1, sv1, "ffn1",
        on_dwd=start_down_reduce, on_dwgu=start_gate_up_reduce)

    gpack = _pack(lambda nm: grads[nm], conv=dconv8[:CONV_W], loss=loss_acc[:, :1])
    gsum_parts, = exchange([gpack], [land(PACK_ROWS, D_MODEL, F32)], [(0, whole, 0, lead)], "reduce_small")
    a_got = exchange_wait(a_send, a_recv, a_src, a_land, gsum_parts, ffn_tr, "reduce_ffn2_wait")
    b_got = exchange_wait(b_send, b_recv, b_src, b_land, gsum_parts, b_tr, "reduce_mix_wait")
    recv = dict(zip(['ffn2_w_gate', 'ffn2_w_up', 'ffn2_w_down', 'w_in', 'w_out', 'mem_w_q', 'mem_w_kv', 'mem_w_o'],
                    a_got + b_got))

    out_g, out_d, out_m, out_v = {}, {}, {}, {}

    def update(nm):
        r = recv[nm]
        res = adamw(w2[nm], m2[nm], v2[nm], [(r, r.shape[2], 0, d) for d in range(N_DEV)], "adamw_" + nm)
        out_g[nm], out_d[nm], out_m[nm], out_v[nm] = res

    for nm in recv:
        update(nm)
    wp = _pack(lambda nm: small[nm])
    mp = _pack(lambda nm: m2[nm][None])
    vp = _pack(lambda nm: v2[nm][None])
    pg, pd, pm, pv = adamw(wp, mp, vp, [(gsum_parts, D_MODEL, 0, d) for d in range(N_DEV)], "adamw_small")
    for dst, p in ((out_g, pg), (out_d, pd), (out_m, pm), (out_v, pv)):
        dst.update({k: val[0] for k, val in _unpack(p).items()})
    loss = pg[ROW_MISC, COL_LOSS]
    conv_g = lax.dynamic_slice_in_dim(pg[ROW_CONV:ROW_CONV + 6].reshape(CONV_W, CONV_CH), me * (CONV_CH // N_DEV),
                                      CONV_CH // N_DEV, axis=1)
    res = adamw(w2['gdn_conv_w'], m2['gdn_conv_w'], v2['gdn_conv_w'], [conv_g], "adamw_conv")
    out_g['gdn_conv_w'], out_d['gdn_conv_w'], out_m['gdn_conv_w'], out_v['gdn_conv_w'] = res

    done = sum(out_d[nm][0, 0] for nm in recv) + out_d['gdn_conv_w'][0, 0] + pd[0, 0]
    after = jnp.zeros((8, 128), F32) + done
    c_got = []
    for key, nm in (('c_gu', "reduce_ffn1_gu_wait"), ('c_down', "reduce_ffn1_down_wait")):
        (c_send, c_recv, c_src, c_land, _), tr = late[key]
        c_got += exchange_wait(c_send, c_recv, c_src, c_land, after, tr, nm)
    recv = dict(zip(['ffn1_w_gate', 'ffn1_w_up', 'ffn1_w_down'], c_got))
    for nm in recv:
        update(nm)

    def depth(t):
        return t[None]

    return (loss, grad_x[None], *[depth(out_g[nm]) for nm in WEIGHTS], *[depth(out_d[nm]) for nm in WEIGHTS],
            *[depth(out_m[nm]) for nm in WEIGHTS], *[depth(out_v[nm]) for nm in WEIGHTS])


def kernel(x, mem, ffn1_pre_norm, ffn1_w_gate, ffn1_w_up, ffn1_w_down, ffn1_post_norm, mix_pre_norm, w_in, fox_f_bias, gdn_conv_w, gdn_a_log, gdn_dt_bias, gdn_out_norm, w_out, mix_post_norm, mem_pre_norm, mem_kv_norm, mem_w_q, mem_w_kv, mem_w_o, mem_post_norm, ffn2_pre_norm, ffn2_w_gate, ffn2_w_up, ffn2_w_down, ffn2_post_norm, loss_target, m_ffn1_pre_norm, m_ffn1_w_gate, m_ffn1_w_up, m_ffn1_w_down, m_ffn1_post_norm, m_mix_pre_norm, m_w_in, m_fox_f_bias, m_gdn_conv_w, m_gdn_a_log, m_gdn_dt_bias, m_gdn_out_norm, m_w_out, m_mix_post_norm, m_mem_pre_norm, m_mem_kv_norm, m_mem_w_q, m_mem_w_kv, m_mem_w_o, m_mem_post_norm, m_ffn2_pre_norm, m_ffn2_w_gate, m_ffn2_w_up, m_ffn2_w_down, m_ffn2_post_norm, v_ffn1_pre_norm, v_ffn1_w_gate, v_ffn1_w_up, v_ffn1_w_down, v_ffn1_post_norm, v_mix_pre_norm, v_w_in, v_fox_f_bias, v_gdn_conv_w, v_gdn_a_log, v_gdn_dt_bias, v_gdn_out_norm, v_w_out, v_mix_post_norm, v_mem_pre_norm, v_mem_kv_norm, v_mem_w_q, v_mem_w_kv, v_mem_w_o, v_mem_post_norm, v_ffn2_pre_norm, v_ffn2_w_gate, v_ffn2_w_up, v_ffn2_w_down, v_ffn2_post_norm):
    return _step(dict(locals()))
```

```python
import functools

import jax
import jax.numpy as jnp
from jax import lax
from jax.experimental import pallas as pl
from jax.experimental.pallas import tpu as pltpu

F32 = jnp.float32
BF = jnp.bfloat16
HI = lax.Precision.HIGHEST
MESH = pl.DeviceIdType.MESH

N_DEV = 8
EPS = 1e-6
D_MODEL = 1024
D_FF = 2816
FF_SHARD = D_FF // N_DEV
FF_SHARD_PAD = 384
D_FF_PAD = FF_SHARD_PAD * N_DEV
FOX_HEADS, FOX_DH = 8, 64
GDN_HEADS, GDN_DH = 4, 128
GDN_CHUNK = 64
CONV_W = 4
MEM_HEADS, MEM_DH = 4, 256
IN_W = 3600
IN_SHARD = IN_W // N_DEV
IN_SHARD_PAD = 512
PROJ_W = 4096
SMALL_F, SMALL_B, SMALL_A = 0, 8, 12

ADAM_LR, ADAM_B1, ADAM_B2, ADAM_EPS, ADAM_WD, ADAM_STEP = 0.001, 0.9, 0.999, 1e-08, 0.01, 10

VMEM_LIMIT = 56 * 1024 * 1024


def _params(sem=None):
    return pltpu.CompilerParams(dimension_semantics=sem, vmem_limit_bytes=VMEM_LIMIT)


def _tile(n, pref, unit=128):
    if n <= pref:
        return n
    t = (pref // unit) * unit
    while t > unit and n % t:
        t -= unit
    assert n % t == 0, (n, pref)
    return t


@functools.partial(jax.custom_vjp, nondiff_argnums=(2, 3))
def bdot(a, b, ca, cb):
    return lax.dot_general(a.astype(BF), b.astype(BF), (((ca,), (cb,)), ((), ())), preferred_element_type=F32)


def _bdot_fwd(a, b, ca, cb):
    return bdot(a, b, ca, cb), (a, b)


def _bdot_bwd(ca, cb, res, g):
    a, b = res
    da = bdot(g, b, 1, 1 - cb) if ca == 1 else bdot(b, g, 1 - cb, 1)
    db = bdot(a, g, 1 - ca, 0) if cb == 0 else bdot(g, a, 0, 1 - ca)
    return da, db


bdot.defvjp(_bdot_fwd, _bdot_bwd)


def hdot(a, b):
    return jnp.dot(a, b, precision=HI, preferred_element_type=F32)


def mdot(a, b):
    return jnp.dot(a, b, precision=lax.Precision.HIGH, preferred_element_type=F32)


def _iota2(shape, dim):
    return lax.broadcasted_iota(jnp.int32, shape, dim)


def _sigmoid(x):
    return 1.0 / (1.0 + jnp.exp(-x))


def _silu(x):
    return x * _sigmoid(x)


def _softplus(x):
    return jnp.maximum(x, 0.0) + jnp.log(1.0 + jnp.exp(-jnp.abs(x)))


def _rms(x, gain):
    return x * lax.rsqrt(jnp.mean(x * x, axis=-1, keepdims=True) + EPS) * gain


def mm(a, b, *, name, ta=False, tb=False, out_dtype=F32, tm=1024, tn=1024, tk=1024, token=None):
    m, k = (a.shape[1], a.shape[0]) if ta else a.shape
    n = b.shape[0] if tb else b.shape[1]
    assert k == (b.shape[1] if tb else b.shape[0]), (a.shape, b.shape, ta, tb)
    tm, tn, tk = _tile(m, tm), _tile(n, tn), _tile(k, tk)
    nk = k // tk
    dims = (((0 if ta else 1,), (1 if tb else 0,)), ((), ()))

    def kern(a_ref, b_ref, *rest):
        o_ref, scratch = (rest[1], rest[2:]) if token is not None else (rest[0], rest[1:])

        def part():
            return lax.dot_general(a_ref[...].astype(BF), b_ref[...].astype(BF), dims, preferred_element_type=F32)

        if nk == 1:
            o_ref[...] = part().astype(o_ref.dtype)
            return
        acc_ref, = scratch
        kk = pl.program_id(2)

        @pl.when(kk == 0)
        def _():
            acc_ref[...] = part()

        @pl.when(kk > 0)
        def _():
            acc_ref[...] += part()

        @pl.when(kk == nk - 1)
        def _():
            o_ref[...] = acc_ref[...].astype(o_ref.dtype)

    a_spec = pl.BlockSpec((tk, tm), lambda i, j, kk: (kk, i)) if ta else pl.BlockSpec((tm, tk), lambda i, j, kk: (i, kk))
    b_spec = pl.BlockSpec((tn, tk), lambda i, j, kk: (j, kk)) if tb else pl.BlockSpec((tk, tn), lambda i, j, kk: (kk, j))
    return pl.pallas_call(
        kern, name=name, grid=(m // tm, n // tn, nk),
        in_specs=[a_spec, b_spec] + ([pl.BlockSpec((8, 128), lambda i, j, kk: (0, 0))] if token is not None else []),
        out_specs=pl.BlockSpec((tm, tn), lambda i, j, kk: (i, j)),
        out_shape=jax.ShapeDtypeStruct((m, n), out_dtype),
        scratch_shapes=[pltpu.VMEM((tm, tn), F32)] if nk > 1 else [],
        compiler_params=_params(("parallel", "parallel", "arbitrary")),
    )(*((a, b) if token is None else (a, b, token)))


def mm_swiglu(a, wgu, *, name):
    m, k = a.shape
    nh = wgu.shape[1] // 2
    tm, tn = _tile(m, 1024), _tile(nh, 512)
    nj = nh // tn

    def kern(a_ref, bg_ref, bu_ref, g_ref, u_ref, act_ref):
        av = a_ref[...]
        g = jnp.dot(av, bg_ref[...], preferred_element_type=F32).astype(BF)
        u = jnp.dot(av, bu_ref[...], preferred_element_type=F32).astype(BF)
        g_ref[...] = g
        u_ref[...] = u
        act_ref[...] = (_silu(g.astype(F32)) * u.astype(F32)).astype(BF)

    tile = pl.BlockSpec((tm, tn), lambda i, j: (i, j))
    out = jax.ShapeDtypeStruct((m, nh), BF)
    return pl.pallas_call(
        kern, name=name, grid=(m // tm, nj),
        in_specs=[pl.BlockSpec((tm, k), lambda i, j: (i, 0)), pl.BlockSpec((k, tn), lambda i, j: (0, j)),
                  pl.BlockSpec((k, tn), lambda i, j: (0, j + nj))],
        out_specs=[tile, tile, tile], out_shape=[out, out, out],
        compiler_params=_params(("parallel", "parallel")),
    )(a, wgu, wgu)


def mm_dswiglu(df, wd, gate, up, *, name, token=None):
    m, k = df.shape
    nh = wd.shape[0]
    tm, tn = _tile(m, 1024), _tile(nh, 512)

    def kern(df_ref, wd_ref, g_ref, u_ref, *rest):
        dg_ref, du_ref = rest[-2:]
        da = lax.dot_general(df_ref[...], wd_ref[...], (((1,), (1,)), ((), ())), preferred_element_type=F32)
        g, u = g_ref[...].astype(F32), u_ref[...].astype(F32)
        sg = _sigmoid(g)
        dg_ref[...] = (da * u * (sg * (1.0 + g * (1.0 - sg)))).astype(BF)
        du_ref[...] = (da * (g * sg)).astype(BF)

    tile = pl.BlockSpec((tm, tn), lambda i, j: (i, j))
    out = jax.ShapeDtypeStruct((m, nh), BF)
    extra = [pl.BlockSpec((8, 128), lambda i, j: (0, 0))] if token is not None else []
    return pl.pallas_call(
        kern, name=name, grid=(m // tm, nh // tn),
        in_specs=[pl.BlockSpec((tm, k), lambda i, j: (i, 0)), pl.BlockSpec((tn, k), lambda i, j: (j, 0)), tile, tile]
        + extra,
        out_specs=[tile, tile], out_shape=[out, out],
        compiler_params=_params(("parallel", "parallel")),
    )(*((df, wd, gate, up) if token is None else (df, wd, gate, up, token)))


def mm_pair(a1, a2, wgu, *, name, token=None):
    m, nh = a1.shape
    n = wgu.shape[0]
    tm, tn, tk = _tile(m, 1024), _tile(n, 1024), _tile(nh, 1024)
    nk = nh // tk
    nt = (((1,), (1,)), ((), ()))

    def kern(a1_ref, a2_ref, b1_ref, b2_ref, *rest):
        o_ref, acc_ref = rest[-2:]
        kk = pl.program_id(2)

        def part():
            return (lax.dot_general(a1_ref[...], b1_ref[...], nt, preferred_element_type=F32)
                    + lax.dot_general(a2_ref[...], b2_ref[...], nt, preferred_element_type=F32))

        @pl.when(kk == 0)
        def _():
            acc_ref[...] = part()

        @pl.when(kk > 0)
        def _():
            acc_ref[...] += part()

        @pl.when(kk == nk - 1)
        def _():
            o_ref[...] = acc_ref[...]

    a_spec = pl.BlockSpec((tm, tk), lambda i, j, kk: (i, kk))
    extra = [pl.BlockSpec((8, 128), lambda i, j, kk: (0, 0))] if token is not None else []
    return pl.pallas_call(
        kern, name=name, grid=(m // tm, n // tn, nk),
        in_specs=[a_spec, a_spec, pl.BlockSpec((tn, tk), lambda i, j, kk: (j, kk)),
                  pl.BlockSpec((tn, tk), lambda i, j, kk: (j, kk + nk))] + extra,
        out_specs=pl.BlockSpec((tm, tn), lambda i, j, kk: (i, j)),
        out_shape=jax.ShapeDtypeStruct((m, n), F32),
        scratch_shapes=[pltpu.VMEM((tm, tn), F32)],
        compiler_params=_params(("parallel", "parallel", "arbitrary")),
    )(*((a1, a2, wgu, wgu) if token is None else (a1, a2, wgu, wgu, token)))


def _row_spec(item, rows):
    if not isinstance(item, tuple):
        return item, pl.BlockSpec((rows, item.shape[1]), lambda i: (i, 0))
    if len(item) == 3:
        arr, w, c = item
        return arr, pl.BlockSpec((rows, w), lambda i: (i, c))
    arr, w, c, lead = item
    return arr, pl.BlockSpec((None, rows, w), lambda i: (lead, i, c))


def _whole_spec(item):
    if not isinstance(item, tuple):
        return item, pl.BlockSpec(item.shape, lambda i: (0,) * item.ndim)
    arr, w, c = item
    return arr, pl.BlockSpec((arr.shape[0], w), lambda i: (0, c))


def rowcall(body, tiled, whole, outs, accs=(), *, rows, total, name):
    rows = min(rows, total)
    assert total % rows == 0
    t_arr, t_spec = zip(*[_row_spec(t, rows) for t in tiled])
    w_arr, w_spec = zip(*[_whole_spec(w) for w in whole]) if whole else ((), ())
    nt, nw, no, na = len(t_arr), len(w_arr), len(outs), len(accs)

    def kern(*refs):
        vals = [r[...] for r in refs[:nt + nw]]
        res = body(*vals)
        if not isinstance(res, (tuple, list)):
            res = (res,)
        assert len(res) == no + na, (name, len(res), no, na)
        for r, v in zip(refs[nt + nw:nt + nw + no], res[:no]):
            r[...] = v.astype(r.dtype)
        if na:
            acc_refs = refs[nt + nw + no:]

            @pl.when(pl.program_id(0) == 0)
            def _():
                for r in acc_refs:
                    r[...] = jnp.zeros_like(r)

            for r, v in zip(acc_refs, res[no:]):
                r[...] += v

    out_shape = [jax.ShapeDtypeStruct((total, w), d) for w, d in outs] + [jax.ShapeDtypeStruct(s, F32) for s in accs]
    out_specs = [pl.BlockSpec((rows, w), lambda i: (i, 0)) for w, _ in outs] + \
                [pl.BlockSpec(s, lambda i: (0, 0)) for s in accs]
    res = pl.pallas_call(
        kern, name=name, grid=(total // rows,),
        in_specs=list(t_spec) + list(w_spec), out_specs=out_specs, out_shape=out_shape,
        compiler_params=_params(("arbitrary",) if na else ("parallel",)),
    )(*t_arr, *w_arr)
    return res


def _colsum(x):
    return jnp.sum(x, axis=0, keepdims=True)


GDN_UNROLL = 4


def _gdn_chunk(q, k, v, z, gb, bb, state, gain, with_starts=False):
    c = GDN_CHUNK
    nh = len(q)
    hs = range(nh)
    r64, c64 = _iota2((c, c), 0), _iota2((c, c), 1)
    incl = r64 >= c64
    strict = r64 > c64
    ltri = incl.astype(F32)
    eye = (r64 == c64).astype(F32)
    pick = (_iota2((GDN_DH, c), 0) == _iota2((GDN_DH, c), 1)).astype(F32)
    last = (_iota2((c, GDN_DH), 0) == c - 1).astype(F32)

    qn = [q[h] * lax.rsqrt(jnp.sum(q[h] * q[h], axis=-1, keepdims=True) + EPS) * (GDN_DH ** -0.5) for h in hs]
    kn = [k[h] * lax.rsqrt(jnp.sum(k[h] * k[h], axis=-1, keepdims=True) + EPS) for h in hs]
    gc = [mdot(ltri, gb[h]) for h in hs]
    gcol = [mdot(gc[h], pick) for h in hs]
    dec = [jnp.exp(jnp.where(incl, gcol[h] - gcol[h].T, -1e30)) for h in hs]
    kb = [kn[h] * bb[h] for h in hs]
    vb = [v[h] * bb[h] for h in hs]
    kk = [bdot(kb[h], kn[h], 1, 1) for h in hs]
    p = [-jnp.where(strict, kk[h] * dec[h], 0.0) for h in hs]
    tinv = [eye + p[h] for h in hs]
    for level in range(5):
        dot = mdot if level < 2 else (lambda a, b: bdot(a, b, 1, 0))
        p = [dot(p[h], p[h]) for h in hs]
        tinv = [tinv[h] + dot(tinv[h], p[h]) for h in hs]
    egc = [jnp.exp(gc[h]) for h in hs]
    u = [mdot(tinv[h], vb[h]) for h in hs]
    w = [mdot(tinv[h], kb[h] * egc[h]) for h in hs]
    attn = [bdot(qn[h], kn[h], 1, 1) * dec[h] for h in hs]
    qd = [qn[h] * egc[h] for h in hs]
    gl = [jnp.sum(gc[h] * last, axis=0, keepdims=True) for h in hs]
    kt = [kn[h] * jnp.exp(gl[h] - gc[h]) for h in hs]
    nst = len(state)
    st, o, mids = list(state), [None] * nh, []
    for c0 in range(0, nh, nst):
        us = range(c0, c0 + nst)
        mids.append(tuple(st))
        ws = [bdot(w[h], st[h - c0], 1, 0) for h in us]
        qs = [bdot(qd[h], st[h - c0], 1, 0) for h in us]
        v_new = [u[h] - ws[h - c0] for h in us]
        av = [bdot(attn[h], v_new[h - c0], 1, 0) for h in us]
        kv = [bdot(kt[h], v_new[h - c0], 0, 0) for h in us]
        st = [st[h - c0] * jnp.exp(gl[h]) + kv[h - c0] for h in us]
        for h in us:
            o[h] = _rms(qs[h - c0] + av[h - c0], gain) * _silu(z[h])
    if with_starts:
        return tuple(o), tuple(st), tuple(mids)
    return tuple(o), tuple(st)


GDN_ROWS = 512
GDN_W = GDN_HEADS * GDN_DH


def gdn_fwd(cqkv, proj, gbb, gain):
    s = cqkv.shape[0]
    nb, cpb = s // GDN_ROWS, GDN_ROWS // GDN_CHUNK
    h4 = GDN_HEADS

    def kern(qkv_ref, z_ref, gb_ref, gain_ref, o_ref, st_ref, state):
        @pl.when(pl.program_id(0) == 0)
        def _():
            state[...] = jnp.zeros_like(state)

        gain_v = gain_ref[...]

        def step(ci, carry):
            sls = [pl.ds(pl.multiple_of((ci * GDN_UNROLL + c) * GDN_CHUNK, GDN_CHUNK), GDN_CHUNK)
                   for c in range(GDN_UNROLL)]
            ins = []
            for sl in sls:
                for h in range(h4):
                    ln = lambda base, h=h: slice(base + h * GDN_DH, base + (h + 1) * GDN_DH)
                    ins.append((qkv_ref[sl, ln(0)], qkv_ref[sl, ln(GDN_W)], qkv_ref[sl, ln(2 * GDN_W)],
                                z_ref[sl, ln(0)], gb_ref[sl, ln(0)], gb_ref[sl, ln(GDN_W)]))
            cols = [tuple(col) for col in zip(*ins)]
            o, new, starts = _gdn_chunk(*cols, tuple(state[h] for h in range(h4)), gain_v, with_starts=True)
            for c, sl in enumerate(sls):
                for h in range(h4):
                    st_ref[h, ci * GDN_UNROLL + c] = starts[c][h]
                    o_ref[sl, h * GDN_DH:(h + 1) * GDN_DH] = o[c * h4 + h]
            for h in range(h4):
                state[h] = new[h]
            return carry

        lax.fori_loop(0, cpb // GDN_UNROLL, step, 0)

    return pl.pallas_call(
        kern, name="gdn_fwd", grid=(nb,),
        in_specs=[pl.BlockSpec((GDN_ROWS, 3 * GDN_W), lambda i: (i, 0)),
                  pl.BlockSpec((GDN_ROWS, GDN_W), lambda i: (i, 6)),
                  pl.BlockSpec((GDN_ROWS, 2 * GDN_W), lambda i: (i, 0)),
                  pl.BlockSpec((1, GDN_DH), lambda i: (0, 0))],
        out_specs=[pl.BlockSpec((GDN_ROWS, GDN_W), lambda i: (i, 0)),
                   pl.BlockSpec((h4, cpb, GDN_DH, GDN_DH), lambda i: (0, i, 0, 0))],
        out_shape=[jax.ShapeDtypeStruct((s, GDN_W), F32),
                   jax.ShapeDtypeStruct((h4, s // GDN_CHUNK, GDN_DH, GDN_DH), F32)],
        scratch_shapes=[pltpu.VMEM((h4, GDN_DH, GDN_DH), F32)],
        compiler_params=_params(("arbitrary",)),
    )(cqkv, proj, gbb, gain)


def gdn_bwd(cqkv, proj, gbb, gain, states, d_mixed):
    s = cqkv.shape[0]
    nb, cpb = s // GDN_ROWS, GDN_ROWS // GDN_CHUNK
    h4 = GDN_HEADS

    def kern(qkv_ref, z_ref, gb_ref, gain_ref, st_ref, do_ref, dqkv_ref, dz_ref, dgb_ref, dgain_ref, dstate):
        @pl.when(pl.program_id(0) == 0)
        def _():
            dgain_ref[...] = jnp.zeros_like(dgain_ref)
            dstate[...] = jnp.zeros_like(dstate)

        gain_v = gain_ref[...]

        def step(t, carry):
            first = (cpb // GDN_UNROLL - 1 - t) * GDN_UNROLL
            sls = [pl.ds(pl.multiple_of((first + c) * GDN_CHUNK, GDN_CHUNK), GDN_CHUNK) for c in range(GDN_UNROLL)]
            prim, cot = [], []
            for sl in sls:
                for h in range(h4):
                    ln = lambda base, h=h: slice(base + h * GDN_DH, base + (h + 1) * GDN_DH)
                    prim.append((qkv_ref[sl, ln(0)], qkv_ref[sl, ln(GDN_W)], qkv_ref[sl, ln(2 * GDN_W)],
                                 z_ref[sl, ln(0)], gb_ref[sl, ln(0)], gb_ref[sl, ln(GDN_W)]))
                    cot.append(do_ref[sl, ln(0)])
            cols = [tuple(col) for col in zip(*prim)]
            st_in = tuple(st_ref[h, first] for h in range(h4))
            vjp = jax.vjp(_gdn_chunk, *cols, st_in, gain_v)[1]
            dq, dk, dv, dz, dg, db, dst, dgn = vjp((tuple(cot), tuple(dstate[h] for h in range(h4))))
            for c, sl in enumerate(sls):
                for h in range(h4):
                    ln = lambda base, h=h: slice(base + h * GDN_DH, base + (h + 1) * GDN_DH)
                    unit = c * h4 + h
                    dqkv_ref[sl, ln(0)] = dq[unit]
                    dqkv_ref[sl, ln(GDN_W)] = dk[unit]
                    dqkv_ref[sl, ln(2 * GDN_W)] = dv[unit]
                    dz_ref[sl, ln(0)] = dz[unit]
                    dgb_ref[sl, ln(0)] = dg[unit]
                    dgb_ref[sl, ln(GDN_W)] = db[unit]
            for h in range(h4):
                dstate[h] = dst[h]
            dgain_ref[...] += dgn
            return carry

        lax.fori_loop(0, cpb // GDN_UNROLL, step, 0)

    def rev(width, cblock=0):
        return pl.BlockSpec((GDN_ROWS, width), lambda i: (nb - 1 - i, cblock))

    return pl.pallas_call(
        kern, name="gdn_bwd", grid=(nb,),
        in_specs=[rev(3 * GDN_W), rev(GDN_W, 6), rev(2 * GDN_W), pl.BlockSpec((1, GDN_DH), lambda i: (0, 0)),
                  pl.BlockSpec((h4, cpb, GDN_DH, GDN_DH), lambda i: (0, nb - 1 - i, 0, 0)), rev(GDN_W, 1)],
        out_specs=[rev(3 * GDN_W), rev(GDN_W), rev(2 * GDN_W), pl.BlockSpec((1, GDN_DH), lambda i: (0, 0))],
        out_shape=[jax.ShapeDtypeStruct((s, 3 * GDN_W), F32), jax.ShapeDtypeStruct((s, GDN_W), F32),
                   jax.ShapeDtypeStruct((s, 2 * GDN_W), F32), jax.ShapeDtypeStruct((1, GDN_DH), F32)],
        scratch_shapes=[pltpu.VMEM((h4, GDN_DH, GDN_DH), F32)],
        compiler_params=_params(("arbitrary",)),
    )(cqkv, proj, gbb, gain, states, d_mixed)


def _gdn_gates(small, prm):
    w = GDN_HEADS * GDN_DH
    lane, head = _iota2((128, w), 0), _iota2((128, w), 1) // GDN_DH
    sel_b = (lane == SMALL_B + head).astype(F32)
    sel_a = (lane == SMALL_A + head).astype(F32)
    prow = _iota2((8, 128), 0)
    a_log = jnp.sum(prm * (prow == 0).astype(F32), axis=0, keepdims=True)
    dt_b = jnp.sum(prm * (prow == 1).astype(F32), axis=0, keepdims=True)
    beta = _sigmoid(mdot(small, sel_b))
    g = mdot(-jnp.exp(a_log) * _softplus(small + dt_b), sel_a)
    return g, beta


CONV_ROWS = 1024
CONV_COLS = 128
CONV_BLOCK0 = 1536 // CONV_COLS


def _shift_down(prev8, cur, s):
    ext = jnp.concatenate([prev8, cur], axis=0)
    return pltpu.roll(ext, s, 0)[8:]


def _shift_up(cur, next8, s):
    n = cur.shape[0]
    ext = jnp.concatenate([cur, next8], axis=0)
    return pltpu.roll(ext, n + 8 - s, 0)[:n]


def _conv_pre(x_ref, w, ci, nchunk):
    r0 = pl.multiple_of(ci * CONV_ROWS, CONV_ROWS)
    cur = x_ref[pl.ds(r0, CONV_ROWS), :]
    prev = x_ref[pl.ds(pl.multiple_of(jnp.maximum(r0 - 8, 0), 8), 8), :]
    prev = jnp.where(ci > 0, prev, 0.0)
    shifted = [cur] + [_shift_down(prev, cur, s) for s in range(1, CONV_W)]
    pre = w[CONV_W - 1:CONV_W, :] * cur
    for s in range(1, CONV_W):
        pre = pre + w[CONV_W - 1 - s:CONV_W - s, :] * shifted[s]
    return r0, pre, shifted


def conv_fwd(proj, conv_w8):
    s = proj.shape[0]
    nchunk = s // CONV_ROWS
    ncol = 3 * GDN_HEADS * GDN_DH // CONV_COLS

    def kern(x_ref, w_ref, y_ref):
        w = w_ref[...]

        def step(ci, carry):
            r0, pre, _ = _conv_pre(x_ref, w, ci, nchunk)
            y_ref[pl.ds(r0, CONV_ROWS), :] = _silu(pre)
            return carry

        lax.fori_loop(0, nchunk, step, 0)

    return pl.pallas_call(
        kern, name="conv_fwd", grid=(ncol,),
        in_specs=[pl.BlockSpec((s, CONV_COLS), lambda j: (0, CONV_BLOCK0 + j)),
                  pl.BlockSpec((8, CONV_COLS), lambda j: (0, j))],
        out_specs=pl.BlockSpec((s, CONV_COLS), lambda j: (0, j)),
        out_shape=jax.ShapeDtypeStruct((s, ncol * CONV_COLS), F32),
        compiler_params=_params(("parallel",)),
    )(proj, conv_w8)


def conv_bwd(proj, conv_w8, dy):
    s = proj.shape[0]
    nchunk = s // CONV_ROWS
    per = 3 * GDN_HEADS * GDN_DH // CONV_COLS
    outs = []
    for part in range(1):
        def kern(x_ref, w_ref, dy_ref, dx_ref, dw_ref, dpre_ref):
            w = w_ref[...]
            rows8 = _iota2((8, CONV_COLS), 0)

            def step1(ci, dw):
                r0, pre, shifted = _conv_pre(x_ref, w, ci, nchunk)
                sg = _sigmoid(pre)
                dpre = dy_ref[pl.ds(r0, CONV_ROWS), :] * sg * (1.0 + pre * (1.0 - sg))
                dpre_ref[pl.ds(r0, CONV_ROWS), :] = dpre
                for sh in range(CONV_W):
                    dw = dw + jnp.where(rows8 == CONV_W - 1 - sh, _colsum(dpre * shifted[sh]), 0.0)
                return dw

            dw_ref[...] = lax.fori_loop(0, nchunk, step1, jnp.zeros((8, CONV_COLS), F32))

            def step2(ci, carry):
                r0 = pl.multiple_of(ci * CONV_ROWS, CONV_ROWS)
                cur = dpre_ref[pl.ds(r0, CONV_ROWS), :]
                nxt = dpre_ref[pl.ds(pl.multiple_of(jnp.minimum(r0 + CONV_ROWS, s - 8), 8), 8), :]
                nxt = jnp.where(ci < nchunk - 1, nxt, 0.0)
                dx = w[CONV_W - 1:CONV_W, :] * cur
                for sh in range(1, CONV_W):
                    dx = dx + w[CONV_W - 1 - sh:CONV_W - sh, :] * _shift_up(cur, nxt, sh)
                dx_ref[pl.ds(r0, CONV_ROWS), :] = dx
                return carry

            lax.fori_loop(0, nchunk, step2, 0)

        outs.append(pl.pallas_call(
            kern, name=f"conv_bwd{part}", grid=(per,),
            in_specs=[pl.BlockSpec((s, CONV_COLS), lambda j, part=part: (0, CONV_BLOCK0 + part * per + j)),
                      pl.BlockSpec((8, CONV_COLS), lambda j, part=part: (0, part * per + j)),
                      pl.BlockSpec((s, CONV_COLS), lambda j: (0, j))],
            out_specs=[pl.BlockSpec((s, CONV_COLS), lambda j: (0, j)),
                       pl.BlockSpec((8, CONV_COLS), lambda j: (0, j))],
            out_shape=[jax.ShapeDtypeStruct((s, per * CONV_COLS), F32),
                       jax.ShapeDtypeStruct((8, per * CONV_COLS), F32)],
            scratch_shapes=[pltpu.VMEM((s, CONV_COLS), F32)],
            compiler_params=_params(("parallel",)),
        )(proj, conv_w8, dy))
    dx = jnp.concatenate([o[0] for o in outs], axis=1)
    dw = jnp.concatenate([o[1] for o in outs], axis=1)
    return dx, dw


FOXF_ROWS = 512
SMALL_BLOCK128 = 3584 // 128


def _log_sigmoid(x):
    return jnp.minimum(x, 0.0) - jnp.log(1.0 + jnp.exp(-jnp.abs(x)))


def fox_f_fwd(proj, bias_row):
    s = proj.shape[0]
    n = s // FOXF_ROWS

    def kern(x_ref, b_ref, f_ref, carry):
        @pl.when(pl.program_id(0) == 0)
        def _():
            carry[...] = jnp.zeros_like(carry)

        heads = _iota2((FOXF_ROWS, 128), 1) < FOX_HEADS
        lf = jnp.where(heads, _log_sigmoid(x_ref[...] + b_ref[...]), 0.0)
        ltri = (_iota2((FOXF_ROWS, FOXF_ROWS), 0) >= _iota2((FOXF_ROWS, FOXF_ROWS), 1)).astype(F32)
        c = hdot(ltri, lf) + carry[...]
        f_ref[...] = c
        carry[...] = c[FOXF_ROWS - 1:FOXF_ROWS, :]

    return pl.pallas_call(
        kern, name="fox_f_fwd", grid=(n,),
        in_specs=[pl.BlockSpec((FOXF_ROWS, 128), lambda i: (i, SMALL_BLOCK128)),
                  pl.BlockSpec((1, 128), lambda i: (0, 0))],
        out_specs=pl.BlockSpec((FOXF_ROWS, 128), lambda i: (i, 0)),
        out_shape=jax.ShapeDtypeStruct((s, 128), F32),
        scratch_shapes=[pltpu.VMEM((1, 128), F32)],
        compiler_params=_params(("arbitrary",)),
    )(proj, bias_row)


def fox_f_bwd(proj, bias_row, d_f):
    s = proj.shape[0]
    n = s // FOXF_ROWS

    def kern(x_ref, b_ref, df_ref, dx_ref, db_ref, carry):
        @pl.when(pl.program_id(0) == 0)
        def _():
            carry[...] = jnp.zeros_like(carry)
            db_ref[...] = jnp.zeros_like(db_ref)

        heads = _iota2((FOXF_ROWS, 128), 1) < FOX_HEADS
        utri = (_iota2((FOXF_ROWS, FOXF_ROWS), 0) <= _iota2((FOXF_ROWS, FOXF_ROWS), 1)).astype(F32)
        rc = hdot(utri, df_ref[...]) + carry[...]
        carry[...] = rc[0:1, :]
        dx = jnp.where(heads, rc * _sigmoid(-(x_ref[...] + b_ref[...])), 0.0)
        dx_ref[...] = dx
        db_ref[...] += _colsum(dx)

    return pl.pallas_call(
        kern, name="fox_f_bwd", grid=(n,),
        in_specs=[pl.BlockSpec((FOXF_ROWS, 128), lambda i: (n - 1 - i, SMALL_BLOCK128)),
                  pl.BlockSpec((1, 128), lambda i: (0, 0)),
                  pl.BlockSpec((FOXF_ROWS, 128), lambda i: (n - 1 - i, 0))],
        out_specs=[pl.BlockSpec((FOXF_ROWS, 128), lambda i: (n - 1 - i, 0)),
                   pl.BlockSpec((1, 128), lambda i: (0, 0))],
        out_shape=[jax.ShapeDtypeStruct((s, 128), F32), jax.ShapeDtypeStruct((1, 128), F32)],
        scratch_shapes=[pltpu.VMEM((1, 128), F32)],
        compiler_params=_params(("arbitrary",)),
    )(proj, bias_row, d_f)


FOX_T = 512
FOX_SCALE = FOX_DH ** -0.5
FOX_PAIRS = FOX_HEADS // 2
NEG = -1e30
_NT = (((1,), (1,)), ((), ()))


def _split3(x):
    def bf(v):
        return lax.reduce_precision(v, exponent_bits=8, mantissa_bits=7)

    hi = bf(x)
    mid = bf(x - hi)
    lo = bf(x - hi - mid)
    return jnp.stack([hi, mid, lo], axis=-1)


def _fox_extras(s, first, second):
    def part(v):
        if v is None:
            return jnp.zeros((s, FOX_HEADS, 3), F32)
        if isinstance(v, float):
            return jnp.full((s, FOX_HEADS, 3), v, F32)
        pairs = v.reshape(s, FOX_PAIRS, 2)
        return _split3(jnp.stack([pairs[:, :, 1], pairs[:, :, 0]], axis=-1).reshape(s, FOX_HEADS))

    cols = jnp.concatenate([part(first), part(second)], axis=-1)
    cols = _pad_to(cols, (s, FOX_HEADS, FOX_DH)).reshape(s, FOX_PAIRS, 2 * FOX_DH)
    return cols.transpose(1, 0, 2).astype(BF)


def _head_masks(rows):
    lane = _iota2((rows, 2 * FOX_DH), 1)
    return lane < FOX_DH, lane >= FOX_DH


def _extra_lane(e, slot):
    return (FOX_DH if e == 0 else 0) + slot


def fox_fwd(qkv, xq, xk, xv):
    s = qkv.shape[0]
    t = min(FOX_T, s)
    n = s // t

    def kern(q_ref, k_ref, v_ref, xq_ref, xk_ref, xv_ref, o_ref, lse_ref):
        i = pl.program_id(1)
        masks = _head_masks(t)
        q_pair, x_pair = q_ref[...] * FOX_SCALE, xq_ref[...]
        q_ops = [jnp.where(mk, q_pair, x_pair) for mk in masks]

        def step(j, carry, masked):
            sl = pl.ds(pl.multiple_of(j * t, t), t)
            k_pair, xk_pair, v_pair, xv_pair = k_ref[sl, :], xk_ref[sl, :], v_ref[sl, :], xv_ref[sl, :]
            k_ops = [jnp.where(mk, k_pair, xk_pair) for mk in masks]
            v_ops = [jnp.where(mk, v_pair, xv_pair) for mk in masks]
            sc = [lax.dot_general(q_ops[e], k_ops[e], _NT, preferred_element_type=F32) for e in range(2)]
            if masked:
                keep = _iota2((t, t), 0) >= _iota2((t, t), 1)
                sc = [jnp.where(keep, x, NEG) for x in sc]
            m_new = [jnp.maximum(carry[e][0], jnp.max(sc[e], axis=1, keepdims=True)) for e in range(2)]
            p = [jnp.exp(sc[e] - m_new[e]).astype(BF) for e in range(2)]
            pv = [jnp.dot(p[e], v_ops[e], preferred_element_type=F32) for e in range(2)]
            return tuple((m_new[e], jnp.exp(carry[e][0] - m_new[e]) * carry[e][1] + pv[e]) for e in range(2))

        init = tuple((jnp.full((t, 1), NEG, F32), jnp.zeros((t, 2 * FOX_DH), F32)) for _ in range(2))
        carry = lax.fori_loop(0, i, lambda j, c: step(j, c, False), init)
        carry = step(i, carry, True)
        lane = _iota2((t, 2 * FOX_DH), 1)
        outs, lses = [], []
        for e in range(2):
            m, acc = carry[e]
            l = jnp.sum(jnp.where(lane == _extra_lane(e, 0), acc, 0.0), axis=1, keepdims=True)
            outs.append(acc / l)
            lses.append(m + jnp.log(l))
        o_ref[...] = jnp.where(masks[0], outs[0], outs[1])
        head0 = 2 * pl.program_id(0)
        lse_ref[...] = jnp.where(lane == head0, lses[0], jnp.where(lane == head0 + 1, lses[1], 0.0))

    pr = FOX_PAIRS
    return pl.pallas_call(
        kern, name="fox_fwd", grid=(pr, n),
        in_specs=[pl.BlockSpec((t, 128), lambda p, i: (i, p)),
                  pl.BlockSpec((s, 128), lambda p, i: (0, pr + p)),
                  pl.BlockSpec((s, 128), lambda p, i: (0, 2 * pr + p)),
                  pl.BlockSpec((None, t, 128), lambda p, i: (p, i, 0)),
                  pl.BlockSpec((None, s, 128), lambda p, i: (p, 0, 0)),
                  pl.BlockSpec((None, s, 128), lambda p, i: (p, 0, 0))],
        out_specs=[pl.BlockSpec((t, 128), lambda p, i: (i, p)),
                   pl.BlockSpec((None, t, 128), lambda p, i: (p, i, 0))],
        out_shape=[jax.ShapeDtypeStruct((s, FOX_HEADS * FOX_DH), F32), jax.ShapeDtypeStruct((pr, s, 128), F32)],
        compiler_params=_params(("parallel", "parallel")),
    )(qkv, qkv, qkv, xq, xk, xv)


def fox_bwd(qkv, d_o, xk, xv, xqb, xdo):
    s = qkv.shape[0]
    t = min(FOX_T, s)
    n = s // t
    w = 2 * FOX_DH

    def both(blocks, slot):
        lane = _iota2(blocks[0].shape, 1)
        head0 = 2 * pl.program_id(0)
        own = jnp.where(lane < FOX_DH, blocks[0], blocks[1])
        sums = [jnp.sum(jnp.where(lane == _extra_lane(e, slot), blocks[e], 0.0), axis=1, keepdims=True)
                for e in range(2)]
        return own, jnp.where(lane == head0, sums[0], jnp.where(lane == head0 + 1, sums[1], 0.0))

    def kern(k_ref, v_ref, xk_ref, xv_ref, q_ref, do_ref, xq_ref, xd_ref,
             dq_ref, dk_ref, dv_ref, sq_ref, sk_ref, dq_acc):
        j = pl.program_id(1)

        @pl.when(j == 0)
        def _():
            dq_acc[...] = jnp.zeros_like(dq_acc)

        masks = _head_masks(t)
        k_ops = [jnp.where(mk, k_ref[...], xk_ref[...]) for mk in masks]
        v_ops = [jnp.where(mk, v_ref[...], xv_ref[...]) for mk in masks]
        k_t = [x.T for x in k_ops]

        def step(i, carry, masked):
            dk, dv = carry
            sl = pl.ds(pl.multiple_of(i * t, t), t)
            q_pair, xq_pair, do_pair, xd_pair = q_ref[sl, :] * FOX_SCALE, xq_ref[sl, :], do_ref[sl, :], xd_ref[sl, :]
            q_ops = [jnp.where(mk, q_pair, xq_pair) for mk in masks]
            do_ops = [jnp.where(mk, do_pair, xd_pair) for mk in masks]
            q_t = [x.T for x in q_ops]
            do_t = [jnp.where(mk, do_pair, 0).astype(BF).T for mk in masks]
            st = [lax.dot_general(k_ops[e], q_ops[e], _NT, preferred_element_type=F32) for e in range(2)]
            dp = [lax.dot_general(v_ops[e], do_ops[e], _NT, preferred_element_type=F32) for e in range(2)]
            if masked:
                keep = _iota2((t, t), 0) <= _iota2((t, t), 1)
                st = [jnp.where(keep, x, NEG) for x in st]
            pt = [jnp.exp(x) for x in st]
            dsb = [(pt[e] * dp[e]).astype(BF) for e in range(2)]
            dv = dv + sum(lax.dot_general(do_t[e], pt[e].astype(BF), _NT, preferred_element_type=F32)
                          for e in range(2))
            dk = tuple(dk[e] + lax.dot_general(q_t[e], dsb[e], _NT, preferred_element_type=F32) for e in range(2))
            for e in range(2):
                dq_acc[i, e * w:(e + 1) * w, :] += jnp.dot(k_t[e], dsb[e], preferred_element_type=F32)
            return dk, dv

        init = ((jnp.zeros((w, t), F32), jnp.zeros((w, t), F32)), jnp.zeros((w, t), F32))
        carry = step(j, init, True)
        dk, dv = lax.fori_loop(j + 1, n, lambda i, c: step(i, c, False), carry)
        dk_ref[...], sk_ref[...] = both([x.T for x in dk], 3)
        dv_ref[...] = dv.T

        @pl.when(j == n - 1)
        def _():
            def out(r, carry):
                sl = pl.ds(pl.multiple_of(r * t, t), t)
                own, sums = both([dq_acc[r, e * w:(e + 1) * w, :].T for e in range(2)], 0)
                dq_ref[sl, :] = own * FOX_SCALE
                sq_ref[sl, :] = sums
                return carry

            lax.fori_loop(0, n, out, 0)

    pr = FOX_PAIRS
    flat = jax.ShapeDtypeStruct((s, FOX_HEADS * FOX_DH), F32)
    tile = pl.BlockSpec((t, 128), lambda p, j: (j, p))
    whole = pl.BlockSpec((s, 128), lambda p, j: (0, p))
    return pl.pallas_call(
        kern, name="fox_bwd", grid=(pr, n),
        in_specs=[pl.BlockSpec((t, 128), lambda p, j: (j, pr + p)),
                  pl.BlockSpec((t, 128), lambda p, j: (j, 2 * pr + p)),
                  pl.BlockSpec((None, t, 128), lambda p, j: (p, j, 0)),
                  pl.BlockSpec((None, t, 128), lambda p, j: (p, j, 0)),
                  whole, whole,
                  pl.BlockSpec((None, s, 128), lambda p, j: (p, 0, 0)),
                  pl.BlockSpec((None, s, 128), lambda p, j: (p, 0, 0))],
        out_specs=[whole, tile, tile, whole, tile],
        out_shape=[flat] * 5,
        scratch_shapes=[pltpu.VMEM((n, 2 * w, t), F32)],
        compiler_params=_params(("parallel", "arbitrary")),
    )(qkv, qkv, xk, xv, qkv, d_o, xqb, xdo)


def _xattn_head(q, k, v):
    sc = bdot(q, k, 1, 1) * (MEM_DH ** -0.5)
    e = jnp.exp(sc - lax.stop_gradient(jnp.max(sc, axis=-1, keepdims=True)))
    p = e / jnp.sum(e, axis=-1, keepdims=True)
    return bdot(p, v, 1, 0)


def xattn_fwd(q, kv):
    s = q.shape[0]
    hh = MEM_HEADS

    def body(*vals):
        qs, ks, vs = vals[:hh], vals[hh:2 * hh], vals[2 * hh:]
        return jnp.concatenate([_xattn_head(qs[a], ks[a], vs[a]) for a in range(hh)], axis=1)

    return rowcall(body, [(q, MEM_DH, a) for a in range(hh)],
                   [(kv, MEM_DH, a) for a in range(2 * hh)],
                   [(hh * MEM_DH, BF)], rows=512, total=s, name="xattn_fwd")[0]


def xattn_bwd(q, kv, d_o):
    s = q.shape[0]
    hh = MEM_HEADS

    def body(*vals):
        qs, dos = vals[:hh], vals[hh:2 * hh]
        ks, vs = vals[2 * hh:3 * hh], vals[3 * hh:]
        dqs, dks, dvs = [], [], []
        for a in range(hh):
            _, vjp = jax.vjp(_xattn_head, qs[a], ks[a], vs[a])
            dq, dk, dv = vjp(dos[a])
            dqs.append(dq)
            dks.append(dk)
            dvs.append(dv)
        return jnp.concatenate(dqs, axis=1), jnp.concatenate(dks + dvs, axis=1)

    return rowcall(body, [(q, MEM_DH, a) for a in range(hh)] + [(d_o, MEM_DH, a) for a in range(hh)],
                   [(kv, MEM_DH, a) for a in range(2 * hh)],
                   [(hh * MEM_DH, BF)], [kv.shape], rows=512, total=s, name="xattn_bwd")


def _slab(ref, axis, start, size):
    if axis is None:
        return ref
    if axis == "lead":
        return ref.at[start]
    idx = pl.ds(pl.multiple_of(start, 128 if axis == 1 else 16), size)
    return ref.at[idx] if axis == 0 else ref.at[:, idx]


def exchange(inputs, outputs, transfers, name):
    ni, no, nt = len(inputs), len(outputs), len(transfers)
    npeer = N_DEV - 1

    def body(*refs):
        ins, outs = refs[:ni], refs[ni:ni + no]
        send, recv, loc = refs[ni + no:]
        x, y, c = lax.axis_index("x"), lax.axis_index("y"), lax.axis_index("c")
        me = 4 * x + 2 * y + c

        def peer(p):
            px = 1 - x if p & 4 else x
            py = 1 - y if p & 2 else y
            pc = 1 - c if p & 1 else c
            return (px, py, pc), 4 * px + 2 * py + pc

        def view(ref, spec, who):
            axis, off, stride, size = spec
            return _slab(ref, axis, off + who * stride, size)

        local, remote = [], []
        for w, (ii, src, oi, dst) in enumerate(transfers):
            cp = pltpu.make_async_copy(view(ins[ii], src, me), view(outs[oi], dst, me), loc.at[w])
            cp.start()
            local.append(cp)
        for p in range(1, N_DEV):
            dev, idx = peer(p)
            for w, (ii, src, oi, dst) in enumerate(transfers):
                k = w * npeer + p - 1
                out_cp = pltpu.make_async_remote_copy(
                    src_ref=view(ins[ii], src, idx), dst_ref=view(outs[oi], dst, me), send_sem=send.at[k],
                    recv_sem=recv.at[k], device_id=dev, device_id_type=MESH)
                out_cp.start()
                in_cp = pltpu.make_async_remote_copy(
                    src_ref=view(ins[ii], src, idx), dst_ref=view(outs[oi], dst, idx), send_sem=send.at[k],
                    recv_sem=recv.at[k], device_id=dev, device_id_type=MESH)
                remote.append((out_cp, in_cp))
        for out_cp, in_cp in remote:
            in_cp.wait_recv()
            out_cp.wait_send()
        for cp in local:
            cp.wait()

    hbm = pl.BlockSpec(memory_space=pl.ANY)
    return pl.pallas_call(
        body, name=name, in_specs=[hbm] * ni, out_specs=[hbm] * no, out_shape=list(outputs),
        scratch_shapes=[pltpu.SemaphoreType.DMA((nt * npeer,)), pltpu.SemaphoreType.DMA((nt * npeer,)),
                        pltpu.SemaphoreType.DMA((nt,))],
        compiler_params=pltpu.CompilerParams(has_side_effects=True),
    )(*inputs)


def _peer(p):
    x, y, c = lax.axis_index("x"), lax.axis_index("y"), lax.axis_index("c")
    px = 1 - x if p & 4 else x
    py = 1 - y if p & 2 else y
    pc = 1 - c if p & 1 else c
    return (px, py, pc), 4 * px + 2 * py + pc


def _view(ref, spec, who):
    axis, off, stride, size = spec
    return _slab(ref, axis, off + who * stride, size)


def place_own(inputs, outputs, transfers):
    me = 4 * lax.axis_index("x") + 2 * lax.axis_index("y") + lax.axis_index("c")
    lands = [lax.empty(o.shape, o.dtype) for o in outputs]
    for ii, src, oi, dst in transfers:
        axis, off, stride, size = src
        own = inputs[ii] if axis is None else lax.dynamic_slice_in_dim(inputs[ii], off + me * stride, size, axis)
        axis, off, stride, size = dst
        if axis == "lead":
            lands[oi] = lax.dynamic_update_slice_in_dim(lands[oi], own[None], me, 0)
        else:
            lands[oi] = lax.dynamic_update_slice_in_dim(lands[oi], own, off + me * stride, axis)
    return lands


_HBM = pl.BlockSpec(memory_space=pltpu.HBM)
_SEM = pl.BlockSpec(memory_space=pltpu.SEMAPHORE)
_EFFECT = pltpu.SideEffectType.DATAFLOW_SIDE_EFFECTING


def _remote_copies(ins, lands, transfers, send, recv):
    npeer = N_DEV - 1
    me = 4 * lax.axis_index("x") + 2 * lax.axis_index("y") + lax.axis_index("c")
    pairs = []
    for p in range(1, N_DEV):
        dev, idx = _peer(p)
        for w, (ii, src, oi, dst) in enumerate(transfers):
            k = w * npeer + p - 1
            common = dict(src_ref=_view(ins[ii], src, idx), send_sem=send.at[k], recv_sem=recv.at[k],
                          device_id=dev, device_id_type=MESH)
            pairs.append((pltpu.make_async_remote_copy(dst_ref=_view(lands[oi], dst, me), **common),
                          pltpu.make_async_remote_copy(dst_ref=_view(lands[oi], dst, idx), **common)))
    return pairs


def exchange_start(inputs, lands, transfers, after, name):
    ni, nl, nsem = len(inputs), len(lands), len(transfers) * (N_DEV - 1)

    def body(*refs):
        ins, lnd = refs[:ni], refs[ni:ni + nl]
        send, recv = refs[ni + nl + 1], refs[ni + nl + 2]
        token = refs[-1]
        for out_cp, _ in _remote_copies(ins, lnd, transfers, send, recv):
            out_cp.start()
        token[...] = jnp.zeros_like(token)

    args = [pltpu.with_memory_space_constraint(a, pltpu.HBM) for a in list(inputs) + list(lands)]
    res = pl.pallas_call(
        body, name=name,
        out_shape=(pltpu.SemaphoreType.DMA((nsem,)), pltpu.SemaphoreType.DMA((nsem,)),
                   *[pltpu.HBM(a.shape, a.dtype) for a in args], jax.ShapeDtypeStruct((8, 128), F32)),
        in_specs=[_HBM] * (ni + nl) + [pl.BlockSpec(memory_space=pl.ANY)],
        out_specs=(_SEM, _SEM, *[_HBM] * (ni + nl), pl.BlockSpec(memory_space=pltpu.VMEM)),
        input_output_aliases={k: k + 2 for k in range(ni + nl)},
        compiler_params=pltpu.CompilerParams(has_side_effects=_EFFECT),
    )(*args, after)
    return res[0], res[1], list(res[2:2 + ni]), list(res[2 + ni:2 + ni + nl]), res[-1]


def exchange_wait(send, recv, inputs, lands, after, transfers, name):
    ni, nl = len(inputs), len(lands)

    def body(*refs):
        ins, lnd = refs[:ni], refs[ni:ni + nl]
        send_r, recv_r = refs[ni + nl], refs[ni + nl + 1]
        for out_cp, in_cp in _remote_copies(ins, lnd, transfers, send_r, recv_r):
            out_cp.wait_send()
            in_cp.wait_recv()

    res = pl.pallas_call(
        body, name=name,
        out_shape=tuple(pltpu.HBM(a.shape, a.dtype) for a in list(inputs) + list(lands)),
        in_specs=[_HBM] * (ni + nl) + [_SEM, _SEM, pl.BlockSpec(memory_space=pl.ANY)],
        out_specs=tuple([_HBM] * (ni + nl)),
        input_output_aliases={k: k for k in range(ni + nl)},
        compiler_params=pltpu.CompilerParams(has_side_effects=_EFFECT),
    )(*inputs, *lands, send, recv, after)
    return list(res[ni:])


def adamw(w, m, v, contribs, name):
    r, c = w.shape
    nc = len(contribs)
    rows = next((r // d for d in (4, 2) if r % d == 0 and (r // d) % 16 == 0), r)
    c1, c2 = 1.0 - ADAM_B1 ** ADAM_STEP, 1.0 - ADAM_B2 ** ADAM_STEP

    def body(wv, mv, vv, *gs):
        g = gs[0].astype(F32)
        for extra in gs[1:]:
            g = g + extra.astype(F32)
        g = g[:, :c]
        m_new = ADAM_B1 * mv + (1.0 - ADAM_B1) * g
        v_new = ADAM_B2 * vv + (1.0 - ADAM_B2) * (g * g)
        delta = -ADAM_LR * ((m_new / c1) / (jnp.sqrt(v_new / c2) + ADAM_EPS) + ADAM_WD * wv)
        return g, delta, m_new, v_new

    assert nc >= 1
    return rowcall(body, [w, m, v] + list(contribs), [], [(c, F32)] * 4, rows=rows, total=r, name=name)


WEIGHTS = ['ffn1_pre_norm', 'ffn1_w_gate', 'ffn1_w_up', 'ffn1_w_down', 'ffn1_post_norm', 'mix_pre_norm', 'w_in',
           'fox_f_bias', 'gdn_conv_w', 'gdn_a_log', 'gdn_dt_bias', 'gdn_out_norm', 'w_out', 'mix_post_norm',
           'mem_pre_norm', 'mem_kv_norm', 'mem_w_q', 'mem_w_kv', 'mem_w_o', 'mem_post_norm', 'ffn2_pre_norm',
           'ffn2_w_gate', 'ffn2_w_up', 'ffn2_w_down', 'ffn2_post_norm']
GAINS = ['ffn1_pre_norm', 'ffn1_post_norm', 'mix_pre_norm', 'mix_post_norm', 'mem_pre_norm', 'mem_kv_norm',
         'mem_post_norm', 'ffn2_pre_norm', 'ffn2_post_norm']
BIG = ['ffn1_w_gate', 'ffn1_w_up', 'ffn1_w_down', 'w_in', 'w_out', 'mem_w_q', 'mem_w_kv', 'mem_w_o',
       'ffn2_w_gate', 'ffn2_w_up', 'ffn2_w_down']
PACK_ROWS = 24
ROW_MISC = len(GAINS)
ROW_CONV = ROW_MISC + 1
COL_FBIAS, COL_ALOG, COL_DTB, COL_ONORM, COL_LOSS = 0, 8, 12, 128, 256
CONV_CH = 3 * GDN_HEADS * GDN_DH


def _pad_to(a, shape):
    return jnp.pad(a, [(0, t - s) for s, t in zip(a.shape, shape)])


def _pack(get, conv=None, loss=None):
    rows = [get(nm) for nm in GAINS]
    misc = jnp.concatenate([get('fox_f_bias'), get('gdn_a_log'), get('gdn_dt_bias'),
                            jnp.zeros((1, COL_ONORM - COL_DTB - 4), F32), get('gdn_out_norm'),
                            jnp.zeros((1, 1), F32) if loss is None else loss.reshape(1, 1)], axis=1)
    rows.append(_pad_to(misc, (1, D_MODEL)))
    rows.append(jnp.zeros((6, D_MODEL), F32) if conv is None else conv.reshape(6, D_MODEL))
    return _pad_to(jnp.concatenate(rows, axis=0), (PACK_ROWS, D_MODEL))


def _unpack(p):
    out = {nm: p[i:i + 1] for i, nm in enumerate(GAINS)}
    misc = p[ROW_MISC:ROW_MISC + 1]
    out['fox_f_bias'] = misc[:, COL_FBIAS:COL_FBIAS + FOX_HEADS]
    out['gdn_a_log'] = misc[:, COL_ALOG:COL_ALOG + GDN_HEADS]
    out['gdn_dt_bias'] = misc[:, COL_DTB:COL_DTB + GDN_HEADS]
    out['gdn_out_norm'] = misc[:, COL_ONORM:COL_ONORM + GDN_DH]
    return out


def _ffn_fwd(h, pre, wgu, wd, tag, u=None):
    s = h.shape[0]
    if u is None:
        u, = rowcall(_rms, [h], [pre], [(D_MODEL, BF)], rows=512, total=s, name=tag + "_pre")
    if callable(wgu):
        wgu = wgu(u)
    gate, up, act = mm_swiglu(u, wgu, name=tag + "_gate_up")
    if callable(wd):
        wd = wd(act)
    f = mm(act, wd, name=tag + "_down")
    return u, gate, up, act, f


def _half_rms(a, g):
    return 0.5 * _rms(a, g)


def _ffn_bwd(dh_out, h, pre, post, wgu, wd, saved, tag, on_dwd=None, on_dwgu=None, post_done=None):
    u, gate, up, act, f = saved
    s = h.shape[0]

    def b_post(dh, fv, pg):
        return jax.vjp(_half_rms, fv, pg)[1](dh)

    if post_done is not None:
        df, dpost = post_done
    else:
        df, dpost = rowcall(b_post, [dh_out, f], [post], [(D_MODEL, BF)], [(1, D_MODEL)], rows=512, total=s,
                            name=tag + "_bwd_post")
    dwd = mm(act, df, ta=True, out_dtype=BF, name=tag + "_bwd_dwd")
    dgate, dup = mm_dswiglu(df, wd, gate, up, name=tag + "_bwd_dact", token=on_dwd(dwd) if on_dwd else None)
    dwg = mm(u, dgate, ta=True, out_dtype=BF, name=tag + "_bwd_dwg")
    dwu = mm(u, dup, ta=True, out_dtype=BF, name=tag + "_bwd_dwu")
    du = mm_pair(dgate, dup, wgu, name=tag + "_bwd_du", token=on_dwgu(dwg, dwu) if on_dwgu else None)

    def b_pre(dh, duv, hv, pg):
        dx, dpre = jax.vjp(_rms, hv, pg)[1](duv)
        return dh + dx, dpre

    dh, dpre = rowcall(b_pre, [dh_out, du, h], [pre], [(D_MODEL, F32)], [(1, D_MODEL)], rows=512, total=s,
                       name=tag + "_bwd_pre")
    return dh, dwg, dwu, dwd, dpre, dpost


def _residual_rms(h, a, g):
    return h + _rms(a, g)


def _bwd_residual(dh, a, g):
    return jax.vjp(_rms, a, g)[1](dh)


def _step(a):
    x, mem = a['x'][0], a['mem'][0]
    s = x.shape[0]
    me = 4 * lax.axis_index("x") + 2 * lax.axis_index("y") + lax.axis_index("c")
    w2 = {nm: a[nm][0] for nm in WEIGHTS}
    m2 = {nm: a['m_' + nm][0] for nm in WEIGHTS}
    v2 = {nm: a['v_' + nm][0] for nm in WEIGHTS}
    small = {nm: w2[nm][None] for nm in WEIGHTS if nm not in BIG and nm != 'gdn_conv_w'}

    def ff_cols(w):
        return _pad_to(w, (D_MODEL, FF_SHARD_PAD)).astype(BF)

    def ff_rows(w):
        return _pad_to(w, (FF_SHARD_PAD, D_MODEL)).astype(BF)

    whole = (None, 0, 0, 0)
    conv_pad = 256
    g_in = [ff_cols(w2['ffn1_w_gate']), ff_cols(w2['ffn1_w_up']), ff_rows(w2['ffn1_w_down']),
            ff_cols(w2['ffn2_w_gate']), ff_cols(w2['ffn2_w_up']), ff_rows(w2['ffn2_w_down']),
            _pad_to(w2['w_in'], (D_MODEL, IN_SHARD_PAD)).astype(BF), w2['w_out'].astype(BF),
            w2['mem_w_q'].astype(BF), w2['mem_w_kv'].astype(BF), w2['mem_w_o'].astype(BF),
            _pad_to(w2['gdn_conv_w'], (8, conv_pad))]
    g_out = [jax.ShapeDtypeStruct((D_MODEL, 2 * D_FF_PAD), BF), jax.ShapeDtypeStruct((D_FF_PAD, D_MODEL), BF),
             jax.ShapeDtypeStruct((D_MODEL, 2 * D_FF_PAD), BF), jax.ShapeDtypeStruct((D_FF_PAD, D_MODEL), BF),
             jax.ShapeDtypeStruct((D_MODEL, N_DEV * IN_SHARD_PAD), BF), jax.ShapeDtypeStruct((D_MODEL, D_MODEL), BF),
             jax.ShapeDtypeStruct((D_MODEL, D_MODEL), BF), jax.ShapeDtypeStruct((D_MODEL, 2 * D_MODEL), BF),
             jax.ShapeDtypeStruct((D_MODEL, D_MODEL), BF), jax.ShapeDtypeStruct((8, N_DEV * conv_pad), F32)]
    sp_, dm = FF_SHARD_PAD, D_MODEL // N_DEV
    g_tr = [(0, whole, 0, (1, 0, sp_, sp_)), (1, whole, 0, (1, D_FF_PAD, sp_, sp_)), (2, whole, 1, (0, 0, sp_, sp_)),
            (3, whole, 2, (1, 0, sp_, sp_)), (4, whole, 2, (1, D_FF_PAD, sp_, sp_)), (5, whole, 3, (0, 0, sp_, sp_)),
            (6, whole, 4, (1, 0, IN_SHARD_PAD, IN_SHARD_PAD)), (7, whole, 5, (0, 0, dm, dm)),
            (8, whole, 6, (0, 0, dm, dm)), (9, whole, 7, (1, 0, 2 * dm, 2 * dm)), (10, whole, 8, (0, 0, dm, dm)),
            (11, whole, 9, (1, 0, conv_pad, conv_pad))]
    def pick(idx):
        ins = sorted({g_tr[k][0] for k in idx})
        outs = sorted({g_tr[k][2] for k in idx})
        tr = [(ins.index(g_tr[k][0]), g_tr[k][1], outs.index(g_tr[k][2]), g_tr[k][3]) for k in idx]
        return [g_in[i] for i in ins], [g_out[o] for o in outs], tr

    stages, after = [], g_in[0]
    for nm, idx in (("gate_up", [0, 1]), ("down", [2]), ("mix", [6, 7, 11]), ("late", [8, 9, 10, 3, 4, 5])):
        st_in, st_out, st_tr = pick(idx)
        st = exchange_start(st_in, place_own(st_in, st_out, st_tr), st_tr, after, "gather_%s_start" % nm)
        stages.append((st, st_tr, "gather_%s_wait" % nm))
        after = st[4]
    g_token = after

    def gather_wait(k, after_):
        (send_, recv_, src_, land_, _), tr_, nm_ = stages[k]
        return exchange_wait(send_, recv_, src_, land_, after_, tr_, nm_)

    bias_row = _pad_to(small['fox_f_bias'], (1, 128))
    gate_prm = _pad_to(jnp.concatenate([_pad_to(small['gdn_a_log'], (1, 128 - SMALL_A)),
                                        _pad_to(small['gdn_dt_bias'], (1, 128 - SMALL_A))], axis=0),
                       (8, 128 - SMALL_A))
    gate_prm = jnp.pad(gate_prm, ((0, 0), (SMALL_A, 0)))
    onorm = small['gdn_out_norm']

    late = {}

    def wgu1_when(u):
        late['wgu1'], = gather_wait(0, u)
        return late['wgu1']

    def wd1_when(act):
        late['wd1'], = gather_wait(1, act)
        return late['wd1']

    sv1 = _ffn_fwd(x, small['ffn1_pre_norm'] + g_token[0, 0], wgu1_when, wd1_when, "ffn1")
    wgu1, wd1 = late['wgu1'], late['wd1']
    def b_out_pre(h, f, g_post, g_pre):
        hn = h + _half_rms(f, g_post)
        return hn, _rms(hn, g_pre)

    h1, u2 = rowcall(b_out_pre, [x, sv1[4]], [small['ffn1_post_norm'], small['mix_pre_norm']],
                     [(D_MODEL, F32), (D_MODEL, BF)], rows=512, total=s, name="ffn1_out")
    w_in_g, w_out, conv_g = gather_wait(2, h1)
    w_in = jnp.concatenate([w_in_g[:, j * IN_SHARD_PAD:j * IN_SHARD_PAD + IN_SHARD] for j in range(N_DEV)],
                           axis=1)
    sp = [0, 512, 1024, 1536, 1544, 2056, 2568, 3080, 3592, 3596, 3600]
    fq, fk, fv, ff, gq, gk, gv, gz, gb, ga = [w_in[:, sp[i]:sp[i + 1]] for i in range(10)]
    w_proj = jnp.concatenate([fq, fk, fv, gq, gk, gv, gz, ff, gb, ga,
                              jnp.zeros((D_MODEL, PROJ_W - 3584 - 16), BF)], axis=1)
    conv_w8 = conv_g.reshape(8, N_DEV, conv_pad)[:, :, :CONV_CH // N_DEV].reshape(8, CONV_CH)


    proj = mm(u2, w_proj, name="mix_proj")
    f_cum = fox_f_fwd(proj, bias_row)
    f_heads = f_cum[:, :FOX_HEADS]
    qkv_bf = proj[:, :3 * FOX_HEADS * FOX_DH].astype(BF)
    xk, xv = _fox_extras(s, 1.0, -f_heads), _fox_extras(s, 1.0, None)
    fox_flat, lse = fox_fwd(qkv_bf, _fox_extras(s, f_heads, 1.0), xk, xv)
    lse_heads = jnp.sum(lse, axis=0)[:, :FOX_HEADS]
    cqkv = conv_fwd(proj, conv_w8)
    g_l, b_l = rowcall(_gdn_gates, [(proj, 128, SMALL_BLOCK128)], [gate_prm], [(512, F32), (512, F32)],
                       rows=512, total=s, name="gdn_gates")
    gbb = jnp.concatenate([g_l, b_l], axis=1)
    gdn_o, states = gdn_fwd(cqkv, proj, gbb, onorm)
    mixed = jnp.concatenate([fox_flat, gdn_o], axis=1).astype(BF)
    mo = mm(mixed, w_out, name="mix_out")
    def b_res_pre(h, a_, g_post, g_pre):
        hn = h + _rms(a_, g_post)
        return hn, _rms(hn, g_pre)

    h2, hq = rowcall(b_res_pre, [h1, mo], [small['mix_post_norm'], small['mem_pre_norm']],
                     [(D_MODEL, F32), (D_MODEL, BF)], rows=512, total=s, name="mix_res")
    mn, = rowcall(_rms, [mem], [small['mem_kv_norm']], [(D_MODEL, BF)], rows=256, total=mem.shape[0], name="mem_kvn")
    wgu2, wd2, w_q, w_kv, w_o = gather_wait(3, h2)
    q_mem = mm(hq, w_q, name="mem_q")
    kv_mem = mm(mn, w_kv, name="mem_kv")
    o_mem = xattn_fwd(q_mem, kv_mem)
    c_mem = mm(o_mem, w_o, name="mem_o")
    h3, u3 = rowcall(b_res_pre, [h2, c_mem], [small['mem_post_norm'], small['ffn2_pre_norm']],
                     [(D_MODEL, F32), (D_MODEL, BF)], rows=512, total=s, name="mem_res")

    sv2 = _ffn_fwd(h3, small['ffn2_pre_norm'], wgu2, wd2, "ffn2", u=u3)

    def b_loss(h, f, tgt, g):
        err = h + _half_rms(f, g) - tgt
        part = 0.5 * jnp.sum(jnp.mean(err * err, axis=-1, keepdims=True), axis=0, keepdims=True)
        dyv = err * (1.0 / D_MODEL)
        dfv, dpost = jax.vjp(_half_rms, f, g)[1](dyv)
        return dyv, dfv, jnp.broadcast_to(part, (1, 128)), dpost

    dy, df2, loss_acc, dpost2 = rowcall(b_loss, [h3, sv2[4], a['loss_target'][0]], [small['ffn2_post_norm']],
                                        [(D_MODEL, F32), (D_MODEL, BF)], [(1, 128), (1, D_MODEL)], rows=512, total=s,
                                        name="loss")

    grads = {}
    dh3, dwg2, dwu2, dwd2, grads['ffn2_pre_norm'], grads['ffn2_post_norm'] = _ffn_bwd(
        dy, h3, small['ffn2_pre_norm'], small['ffn2_post_norm'], wgu2, wd2, sv2, "ffn2", post_done=(df2, dpost2))

    lead = ("lead", 0, 1, 0)

    def land(r, c, dt=BF):
        return jax.ShapeDtypeStruct((N_DEV, r, c), dt)

    ffn_tr = [(0, (1, 0, sp_, sp_), 0, lead), (1, (1, 0, sp_, sp_), 1, lead), (2, (0, 0, sp_, FF_SHARD), 2, lead)]
    ffn_land = [land(D_MODEL, sp_), land(D_MODEL, sp_), land(FF_SHARD, D_MODEL)]
    a_in = [dwg2, dwu2, dwd2]
    a_send, a_recv, a_src, a_land, a_token = exchange_start(a_in, place_own(a_in, ffn_land, ffn_tr), ffn_tr, dh3,
                                                            "reduce_ffn2_start")

    dc, grads['mem_post_norm'] = rowcall(_bwd_residual, [dh3, c_mem], [small['mem_post_norm'] + a_token[0, 0]],
                                         [(D_MODEL, BF)],
                                         [(1, D_MODEL)], rows=512, total=s, name="mem_bwd_res")
    d_o = mm(dc, w_o, tb=True, name="mem_bwd_do")
    dw_o = mm(o_mem, dc, ta=True, out_dtype=BF, name="mem_bwd_dwo")
    dq_mem, dkv = xattn_bwd(q_mem, kv_mem, d_o)
    dhq = mm(dq_mem, w_q, tb=True, name="mem_bwd_dhq")
    dw_q = mm(hq, dq_mem, ta=True, out_dtype=BF, name="mem_bwd_dwq")
    dmn = mm(dkv, w_kv, tb=True, name="mem_bwd_dmn")
    dw_kv = mm(mn, dkv, ta=True, out_dtype=BF, name="mem_bwd_dwkv")
    _, grads['mem_kv_norm'] = rowcall(lambda d, mv, g: jax.vjp(_rms, mv, g)[1](d), [dmn, mem],
                                      [small['mem_kv_norm']], [(D_MODEL, F32)], [(1, D_MODEL)], rows=256,
                                      total=mem.shape[0], name="mem_bwd_kvn")

    def b_pre(dh, duv, hv, pg):
        dx, dpre = jax.vjp(_rms, hv, pg)[1](duv)
        return dh + dx, dpre

    def b_pre_res(dh, duv, hv, mov, g_pre, g_post):
        dx, dpre = jax.vjp(_rms, hv, g_pre)[1](duv)
        dhn = dh + dx
        dmov, dpost = jax.vjp(_rms, mov, g_post)[1](dhn)
        return dhn, dmov, dpre, dpost

    dh2, dmo, grads['mem_pre_norm'], grads['mix_post_norm'] = rowcall(
        b_pre_res, [dh3, dhq, h2, mo], [small['mem_pre_norm'], small['mix_post_norm']],
        [(D_MODEL, F32), (D_MODEL, BF)], [(1, D_MODEL), (1, D_MODEL)], rows=512, total=s, name="mem_bwd_pre")
    d_mixed = mm(dmo, w_out, tb=True, name="mix_bwd_dmixed")
    dw_out = mm(mixed, dmo, ta=True, out_dtype=BF, name="mix_bwd_dwout")
    def b_delta(do, o):
        sel = (_iota2((512, 128), 0) // FOX_DH == _iota2((512, 128), 1)).astype(F32)
        return mdot(do * o, sel)

    delta, = rowcall(b_delta, [(d_mixed, 512, 0), fox_flat], [], [(128, F32)], rows=512, total=s, name="fox_delta")
    dfox_q, dfox_k, dvf, sum_q, sum_k = fox_bwd(qkv_bf, d_mixed[:, :512].astype(BF), xk, xv,
                                                _fox_extras(s, f_heads - lse_heads, 1.0),
                                                _fox_extras(s, -delta[:, :FOX_HEADS], None))
    d_f = jnp.sum((sum_q - sum_k).reshape(s, FOX_PAIRS, 2 * FOX_DH), axis=1)
    dsmall_f, dbias = fox_f_bwd(proj, bias_row, d_f)
    grads['fox_f_bias'] = dbias[:, :FOX_HEADS]
    dcqkv, dz, dgb, grads['gdn_out_norm'] = gdn_bwd(cqkv, proj, gbb, onorm, states, d_mixed)

    def b_gates(sm, dsf, dg, db, prm):
        dsm, dprm = jax.vjp(_gdn_gates, sm, prm)[1]((dg, db))
        return dsm + dsf, dprm

    dsmall, dprm = rowcall(b_gates, [(proj, 128, SMALL_BLOCK128), dsmall_f, (dgb, 512, 0), (dgb, 512, 1)], [gate_prm],
                           [(128, F32)],
                           [(8, 128)], rows=512, total=s, name="gdn_bwd_gates")
    grads['gdn_a_log'] = dprm[0:1, SMALL_A:SMALL_A + GDN_HEADS]
    grads['gdn_dt_bias'] = dprm[1:2, SMALL_A:SMALL_A + GDN_HEADS]
    dqkv_pre, dconv8 = conv_bwd(proj, conv_w8, dcqkv)
    dproj = jnp.concatenate([dfox_q, dfox_k, dvf, dqkv_pre, dz, dsmall,
                             jnp.zeros((s, PROJ_W - 3584 - 128), F32)], axis=1).astype(BF)
    du2 = mm(dproj, w_proj, tb=True, name="mix_bwd_du")
    dw_proj = mm(u2, dproj, ta=True, out_dtype=BF, name="mix_bwd_dwproj")
    dh1, grads['mix_pre_norm'] = rowcall(b_pre, [dh2, du2, h1], [small['mix_pre_norm']], [(D_MODEL, F32)],
                                         [(1, D_MODEL)], rows=512, total=s, name="mix_bwd_pre")

    dw_in = jnp.concatenate([dw_proj[:, :1536], dw_proj[:, 3584:3592], dw_proj[:, 1536:3584],
                             dw_proj[:, 3592:3600]], axis=1)
    gap = jnp.zeros((D_MODEL, IN_SHARD_PAD - IN_SHARD), BF)
    dw_in = jnp.concatenate([piece for j in range(N_DEV) for piece in (dw_in[:, j * IN_SHARD:(j + 1) * IN_SHARD], gap)],
                            axis=1)
    b_in = [dw_in, dw_out, dw_q, dw_kv, dw_o]
    b_tr = [(0, (1, 0, IN_SHARD_PAD, IN_SHARD_PAD), 0, lead), (1, (0, 0, dm, dm), 1, lead), (2, (0, 0, dm, dm), 2, lead),
            (3, (1, 0, 2 * dm, 2 * dm), 3, lead), (4, (0, 0, dm, dm), 4, lead)]
    b_shapes = [land(D_MODEL, IN_SHARD_PAD), land(dm, D_MODEL), land(dm, D_MODEL), land(D_MODEL, 2 * dm),
                land(dm, D_MODEL)]
    b_land = place_own(b_in, b_shapes, b_tr)
    b_send, b_recv, b_src, b_land, b_token = exchange_start(b_in, b_land, b_tr, dh1, "reduce_mix_start")

    def start_down_reduce(dwd):
        tr = ffn_tr[2:]
        tr = [(0, tr[0][1], 0, tr[0][3])]
        late['c_down'] = (exchange_start([dwd], place_own([dwd], ffn_land[2:], tr), tr, dwd, "reduce_ffn1_down_start"), tr)
        return late['c_down'][0][4]

    def start_gate_up_reduce(dwg, dwu):
        tr = ffn_tr[:2]
        late['c_gu'] = (exchange_start([dwg, dwu], place_own([dwg, dwu], ffn_land[:2], tr), tr, dwu,
                                       "reduce_ffn1_gu_start"), tr)
        return late['c_gu'][0][4]

    grad_x, _, _, _, grads['ffn1_pre_norm'], grads['ffn1_post_norm'] = _ffn_bwd(
        dh1, x, small['ffn1_pre_norm'], small['ffn1_post_norm'] + b_token[0, 0], wgu1, wd1, sv1, "ffn1",
        on_dwd=start_down_reduce, on_dwgu=start_gate_up_reduce)

    gpack = _pack(lambda nm: grads[nm], conv=dconv8[:CONV_W], loss=loss_acc[:, :1])
    gsum_parts, = exchange([gpack], [land(PACK_ROWS, D_MODEL, F32)], [(0, whole, 0, lead)], "reduce_small")
    a_got = exchange_wait(a_send, a_recv, a_src, a_land, gsum_parts, ffn_tr, "reduce_ffn2_wait")
    b_got = exchange_wait(b_send, b_recv, b_src, b_land, gsum_parts, b_tr, "reduce_mix_wait")
    recv = dict(zip(['ffn2_w_gate', 'ffn2_w_up', 'ffn2_w_down', 'w_in', 'w_out', 'mem_w_q', 'mem_w_kv', 'mem_w_o'],
                    a_got + b_got))

    out_g, out_d, out_m, out_v = {}, {}, {}, {}

    def update(nm):
        r = recv[nm]
        res = adamw(w2[nm], m2[nm], v2[nm], [(r, r.shape[2], 0, d) for d in range(N_DEV)], "adamw_" + nm)
        out_g[nm], out_d[nm], out_m[nm], out_v[nm] = res

    for nm in recv:
        update(nm)
    wp = _pack(lambda nm: small[nm])
    mp = _pack(lambda nm: m2[nm][None])
    vp = _pack(lambda nm: v2[nm][None])
    pg, pd, pm, pv = adamw(wp, mp, vp, [(gsum_parts, D_MODEL, 0, d) for d in range(N_DEV)], "adamw_small")
    for dst, p in ((out_g, pg), (out_d, pd), (out_m, pm), (out_v, pv)):
        dst.update({k: val[0] for k, val in _unpack(p).items()})
    loss = pg[ROW_MISC, COL_LOSS]
    conv_g = lax.dynamic_slice_in_dim(pg[ROW_CONV:ROW_CONV + 6].reshape(CONV_W, CONV_CH), me * (CONV_CH // N_DEV),
                                      CONV_CH // N_DEV, axis=1)
    res = adamw(w2['gdn_conv_w'], m2['gdn_conv_w'], v2['gdn_conv_w'], [conv_g], "adamw_conv")
    out_g['gdn_conv_w'], out_d['gdn_conv_w'], out_m['gdn_conv_w'], out_v['gdn_conv_w'] = res

    done = sum(out_d[nm][0, 0] for nm in recv) + out_d['gdn_conv_w'][0, 0] + pd[0, 0]
    after = jnp.zeros((8, 128), F32) + done
    c_got = []
    for key, nm in (('c_gu', "reduce_ffn1_gu_wait"), ('c_down', "reduce_ffn1_down_wait")):
        (c_send, c_recv, c_src, c_land, _), tr = late[key]
        c_got += exchange_wait(c_send, c_recv, c_src, c_land, after, tr, nm)
    recv = dict(zip(['ffn1_w_gate', 'ffn1_w_up', 'ffn1_w_down'], c_got))
    for nm in recv:
        update(nm)

    def depth(t):
        return t[None]

    return (loss, grad_x[None], *[depth(out_g[nm]) for nm in WEIGHTS], *[depth(out_d[nm]) for nm in WEIGHTS],
            *[depth(out_m[nm]) for nm in WEIGHTS], *[depth(out_v[nm]) for nm in WEIGHTS])


def kernel(x, mem, ffn1_pre_norm, ffn1_w_gate, ffn1_w_up, ffn1_w_down, ffn1_post_norm, mix_pre_norm, w_in, fox_f_bias, gdn_conv_w, gdn_a_log, gdn_dt_bias, gdn_out_norm, w_out, mix_post_norm, mem_pre_norm, mem_kv_norm, mem_w_q, mem_w_kv, mem_w_o, mem_post_norm, ffn2_pre_norm, ffn2_w_gate, ffn2_w_up, ffn2_w_down, ffn2_post_norm, loss_target, m_ffn1_pre_norm, m_ffn1_w_gate, m_ffn1_w_up, m_ffn1_w_down, m_ffn1_post_norm, m_mix_pre_norm, m_w_in, m_fox_f_bias, m_gdn_conv_w, m_gdn_a_log, m_gdn_dt_bias, m_gdn_out_norm, m_w_out, m_mix_post_norm, m_mem_pre_norm, m_mem_kv_norm, m_mem_w_q, m_mem_w_kv, m_mem_w_o, m_mem_post_norm, m_ffn2_pre_norm, m_ffn2_w_gate, m_ffn2_w_up, m_ffn2_w_down, m_ffn2_post_norm, v_ffn1_pre_norm, v_ffn1_w_gate, v_ffn1_w_up, v_ffn1_w_down, v_ffn1_post_norm, v_mix_pre_norm, v_w_in, v_fox_f_bias, v_gdn_conv_w, v_gdn_a_log, v_gdn_dt_bias, v_gdn_out_norm, v_w_out, v_mix_post_norm, v_mem_pre_norm, v_mem_kv_norm, v_mem_w_q, v_mem_w_kv, v_mem_w_o, v_mem_post_norm, v_ffn2_pre_norm, v_ffn2_w_gate, v_ffn2_w_up, v_ffn2_w_down, v_ffn2_post_norm):
    return _step(dict(locals()))
```

```python
import functools

import jax
import jax.numpy as jnp
from jax import lax
from jax.experimental import pallas as pl
from jax.experimental.pallas import tpu as pltpu

F32 = jnp.float32
BF = jnp.bfloat16
HI = lax.Precision.HIGHEST
MESH = pl.DeviceIdType.MESH

N_DEV = 8
EPS = 1e-6
D_MODEL = 1024
D_FF = 2816
FF_SHARD = D_FF // N_DEV
FF_SHARD_PAD = 384
D_FF_PAD = FF_SHARD_PAD * N_DEV
FOX_HEADS, FOX_DH = 8, 64
GDN_HEADS, GDN_DH = 4, 128
GDN_CHUNK = 64
CONV_W = 4
MEM_HEADS, MEM_DH = 4, 256
IN_W = 3600
IN_SHARD = IN_W // N_DEV
IN_SHARD_PAD = 512
PROJ_W = 4096
SMALL_F, SMALL_B, SMALL_A = 0, 8, 12

ADAM_LR, ADAM_B1, ADAM_B2, ADAM_EPS, ADAM_WD, ADAM_STEP = 0.001, 0.9, 0.999, 1e-08, 0.01, 10

VMEM_LIMIT = 56 * 1024 * 1024


def _params(sem=None):
    return pltpu.CompilerParams(dimension_semantics=sem, vmem_limit_bytes=VMEM_LIMIT)


def _tile(n, pref, unit=128):
    if n <= pref:
        return n
    t = (pref // unit) * unit
    while t > unit and n % t:
        t -= unit
    assert n % t == 0, (n, pref)
    return t


@functools.partial(jax.custom_vjp, nondiff_argnums=(2, 3))
def bdot(a, b, ca, cb):
    return lax.dot_general(a.astype(BF), b.astype(BF), (((ca,), (cb,)), ((), ())), preferred_element_type=F32)


def _bdot_fwd(a, b, ca, cb):
    return bdot(a, b, ca, cb), (a, b)


def _bdot_bwd(ca, cb, res, g):
    a, b = res
    da = bdot(g, b, 1, 1 - cb) if ca == 1 else bdot(b, g, 1 - cb, 1)
    db = bdot(a, g, 1 - ca, 0) if cb == 0 else bdot(g, a, 0, 1 - ca)
    return da, db


bdot.defvjp(_bdot_fwd, _bdot_bwd)


def hdot(a, b):
    return jnp.dot(a, b, precision=HI, preferred_element_type=F32)


def mdot(a, b):
    return jnp.dot(a, b, precision=lax.Precision.HIGH, preferred_element_type=F32)


def _iota2(shape, dim):
    return lax.broadcasted_iota(jnp.int32, shape, dim)


def _sigmoid(x):
    return 1.0 / (1.0 + jnp.exp(-x))


def _silu(x):
    return x * _sigmoid(x)


def _softplus(x):
    return jnp.maximum(x, 0.0) + jnp.log(1.0 + jnp.exp(-jnp.abs(x)))


def _rms(x, gain):
    return x * lax.rsqrt(jnp.mean(x * x, axis=-1, keepdims=True) + EPS) * gain


def mm(a, b, *, name, ta=False, tb=False, out_dtype=F32, tm=1024, tn=1024, tk=1024, token=None):
    m, k = (a.shape[1], a.shape[0]) if ta else a.shape
    n = b.shape[0] if tb else b.shape[1]
    assert k == (b.shape[1] if tb else b.shape[0]), (a.shape, b.shape, ta, tb)
    tm, tn, tk = _tile(m, tm), _tile(n, tn), _tile(k, tk)
    nk = k // tk
    dims = (((0 if ta else 1,), (1 if tb else 0,)), ((), ()))

    def kern(a_ref, b_ref, *rest):
        o_ref, scratch = (rest[1], rest[2:]) if token is not None else (rest[0], rest[1:])

        def part():
            return lax.dot_general(a_ref[...].astype(BF), b_ref[...].astype(BF), dims, preferred_element_type=F32)

        if nk == 1:
            o_ref[...] = part().astype(o_ref.dtype)
            return
        acc_ref, = scratch
        kk = pl.program_id(2)

        @pl.when(kk == 0)
        def _():
            acc_ref[...] = part()

        @pl.when(kk > 0)
        def _():
            acc_ref[...] += part()

        @pl.when(kk == nk - 1)
        def _():
            o_ref[...] = acc_ref[...].astype(o_ref.dtype)

    a_spec = pl.BlockSpec((tk, tm), lambda i, j, kk: (kk, i)) if ta else pl.BlockSpec((tm, tk), lambda i, j, kk: (i, kk))
    b_spec = pl.BlockSpec((tn, tk), lambda i, j, kk: (j, kk)) if tb else pl.BlockSpec((tk, tn), lambda i, j, kk: (kk, j))
    return pl.pallas_call(
        kern, name=name, grid=(m // tm, n // tn, nk),
        in_specs=[a_spec, b_spec] + ([pl.BlockSpec((8, 128), lambda i, j, kk: (0, 0))] if token is not None else []),
        out_specs=pl.BlockSpec((tm, tn), lambda i, j, kk: (i, j)),
        out_shape=jax.ShapeDtypeStruct((m, n), out_dtype),
        scratch_shapes=[pltpu.VMEM((tm, tn), F32)] if nk > 1 else [],
        compiler_params=_params(("parallel", "parallel", "arbitrary")),
    )(*((a, b) if token is None else (a, b, token)))


def mm_swiglu(a, wgu, *, name):
    m, k = a.shape
    nh = wgu.shape[1] // 2
    tm, tn = _tile(m, 1024), _tile(nh, 512)
    nj = nh // tn

    def kern(a_ref, bg_ref, bu_ref, g_ref, u_ref, act_ref):
        av = a_ref[...]
        g = jnp.dot(av, bg_ref[...], preferred_element_type=F32).astype(BF)
        u = jnp.dot(av, bu_ref[...], preferred_element_type=F32).astype(BF)
        g_ref[...] = g
        u_ref[...] = u
        act_ref[...] = (_silu(g.astype(F32)) * u.astype(F32)).astype(BF)

    tile = pl.BlockSpec((tm, tn), lambda i, j: (i, j))
    out = jax.ShapeDtypeStruct((m, nh), BF)
    return pl.pallas_call(
        kern, name=name, grid=(m // tm, nj),
        in_specs=[pl.BlockSpec((tm, k), lambda i, j: (i, 0)), pl.BlockSpec((k, tn), lambda i, j: (0, j)),
                  pl.BlockSpec((k, tn), lambda i, j: (0, j + nj))],
        out_specs=[tile, tile, tile], out_shape=[out, out, out],
        compiler_params=_params(("parallel", "parallel")),
    )(a, wgu, wgu)


def mm_dswiglu(df, wd, gate, up, *, name, token=None):
    m, k = df.shape
    nh = wd.shape[0]
    tm, tn = _tile(m, 1024), _tile(nh, 512)

    def kern(df_ref, wd_ref, g_ref, u_ref, *rest):
        dg_ref, du_ref = rest[-2:]
        da = lax.dot_general(df_ref[...], wd_ref[...], (((1,), (1,)), ((), ())), preferred_element_type=F32)
        g, u = g_ref[...].astype(F32), u_ref[...].astype(F32)
        sg = _sigmoid(g)
        dg_ref[...] = (da * u * (sg * (1.0 + g * (1.0 - sg)))).astype(BF)
        du_ref[...] = (da * (g * sg)).astype(BF)

    tile = pl.BlockSpec((tm, tn), lambda i, j: (i, j))
    out = jax.ShapeDtypeStruct((m, nh), BF)
    extra = [pl.BlockSpec((8, 128), lambda i, j: (0, 0))] if token is not None else []
    return pl.pallas_call(
        kern, name=name, grid=(m // tm, nh // tn),
        in_specs=[pl.BlockSpec((tm, k), lambda i, j: (i, 0)), pl.BlockSpec((tn, k), lambda i, j: (j, 0)), tile, tile]
        + extra,
        out_specs=[tile, tile], out_shape=[out, out],
        compiler_params=_params(("parallel", "parallel")),
    )(*((df, wd, gate, up) if token is None else (df, wd, gate, up, token)))


def mm_pair(a1, a2, wgu, *, name, token=None):
    m, nh = a1.shape
    n = wgu.shape[0]
    tm, tn, tk = _tile(m, 1024), _tile(n, 1024), _tile(nh, 1024)
    nk = nh // tk
    nt = (((1,), (1,)), ((), ()))

    def kern(a1_ref, a2_ref, b1_ref, b2_ref, *rest):
        o_ref, acc_ref = rest[-2:]
        kk = pl.program_id(2)

        def part():
            return (lax.dot_general(a1_ref[...], b1_ref[...], nt, preferred_element_type=F32)
                    + lax.dot_general(a2_ref[...], b2_ref[...], nt, preferred_element_type=F32))

        @pl.when(kk == 0)
        def _():
            acc_ref[...] = part()

        @pl.when(kk > 0)
        def _():
            acc_ref[...] += part()

        @pl.when(kk == nk - 1)
        def _():
            o_ref[...] = acc_ref[...]

    a_spec = pl.BlockSpec((tm, tk), lambda i, j, kk: (i, kk))
    extra = [pl.BlockSpec((8, 128), lambda i, j, kk: (0, 0))] if token is not None else []
    return pl.pallas_call(
        kern, name=name, grid=(m // tm, n // tn, nk),
        in_specs=[a_spec, a_spec, pl.BlockSpec((tn, tk), lambda i, j, kk: (j, kk)),
                  pl.BlockSpec((tn, tk), lambda i, j, kk: (j, kk + nk))] + extra,
        out_specs=pl.BlockSpec((tm, tn), lambda i, j, kk: (i, j)),
        out_shape=jax.ShapeDtypeStruct((m, n), F32),
        scratch_shapes=[pltpu.VMEM((tm, tn), F32)],
        compiler_params=_params(("parallel", "parallel", "arbitrary")),
    )(*((a1, a2, wgu, wgu) if token is None else (a1, a2, wgu, wgu, token)))


def _row_spec(item, rows):
    if not isinstance(item, tuple):
        return item, pl.BlockSpec((rows, item.shape[1]), lambda i: (i, 0))
    if len(item) == 3:
        arr, w, c = item
        return arr, pl.BlockSpec((rows, w), lambda i: (i, c))
    arr, w, c, lead = item
    return arr, pl.BlockSpec((None, rows, w), lambda i: (lead, i, c))


def _whole_spec(item):
    if not isinstance(item, tuple):
        return item, pl.BlockSpec(item.shape, lambda i: (0,) * item.ndim)
    arr, w, c = item
    return arr, pl.BlockSpec((arr.shape[0], w), lambda i: (0, c))


def rowcall(body, tiled, whole, outs, accs=(), *, rows, total, name):
    rows = min(rows, total)
    assert total % rows == 0
    t_arr, t_spec = zip(*[_row_spec(t, rows) for t in tiled])
    w_arr, w_spec = zip(*[_whole_spec(w) for w in whole]) if whole else ((), ())
    nt, nw, no, na = len(t_arr), len(w_arr), len(outs), len(accs)

    def kern(*refs):
        vals = [r[...] for r in refs[:nt + nw]]
        res = body(*vals)
        if not isinstance(res, (tuple, list)):
            res = (res,)
        assert len(res) == no + na, (name, len(res), no, na)
        for r, v in zip(refs[nt + nw:nt + nw + no], res[:no]):
            r[...] = v.astype(r.dtype)
        if na:
            acc_refs = refs[nt + nw + no:]

            @pl.when(pl.program_id(0) == 0)
            def _():
                for r in acc_refs:
                    r[...] = jnp.zeros_like(r)

            for r, v in zip(acc_refs, res[no:]):
                r[...] += v

    out_shape = [jax.ShapeDtypeStruct((total, w), d) for w, d in outs] + [jax.ShapeDtypeStruct(s, F32) for s in accs]
    out_specs = [pl.BlockSpec((rows, w), lambda i: (i, 0)) for w, _ in outs] + \
                [pl.BlockSpec(s, lambda i: (0, 0)) for s in accs]
    res = pl.pallas_call(
        kern, name=name, grid=(total // rows,),
        in_specs=list(t_spec) + list(w_spec), out_specs=out_specs, out_shape=out_shape,
        compiler_params=_params(("arbitrary",) if na else ("parallel",)),
    )(*t_arr, *w_arr)
    return res


def _colsum(x):
    return jnp.sum(x, axis=0, keepdims=True)


GDN_UNROLL = 4


def _gdn_chunk(q, k, v, z, gb, bb, state, gain, with_starts=False):
    c = GDN_CHUNK
    nh = len(q)
    hs = range(nh)
    r64, c64 = _iota2((c, c), 0), _iota2((c, c), 1)
    incl = r64 >= c64
    strict = r64 > c64
    ltri = incl.astype(F32)
    eye = (r64 == c64).astype(F32)
    pick = (_iota2((GDN_DH, c), 0) == _iota2((GDN_DH, c), 1)).astype(F32)
    last = (_iota2((c, GDN_DH), 0) == c - 1).astype(F32)

    qn = [q[h] * lax.rsqrt(jnp.sum(q[h] * q[h], axis=-1, keepdims=True) + EPS) * (GDN_DH ** -0.5) for h in hs]
    kn = [k[h] * lax.rsqrt(jnp.sum(k[h] * k[h], axis=-1, keepdims=True) + EPS) for h in hs]
    gc = [mdot(ltri, gb[h]) for h in hs]
    gcol = [mdot(gc[h], pick) for h in hs]
    dec = [jnp.exp(jnp.where(incl, gcol[h] - gcol[h].T, -1e30)) for h in hs]
    kb = [kn[h] * bb[h] for h in hs]
    vb = [v[h] * bb[h] for h in hs]
    kk = [bdot(kb[h], kn[h], 1, 1) for h in hs]
    p = [-jnp.where(strict, kk[h] * dec[h], 0.0) for h in hs]
    tinv = [eye + p[h] for h in hs]
    for level in range(5):
        dot = mdot if level < 2 else (lambda a, b: bdot(a, b, 1, 0))
        p = [dot(p[h], p[h]) for h in hs]
        tinv = [tinv[h] + dot(tinv[h], p[h]) for h in hs]
    egc = [jnp.exp(gc[h]) for h in hs]
    u = [mdot(tinv[h], vb[h]) for h in hs]
    w = [mdot(tinv[h], kb[h] * egc[h]) for h in hs]
    attn = [bdot(qn[h], kn[h], 1, 1) * dec[h] for h in hs]
    qd = [qn[h] * egc[h] for h in hs]
    gl = [jnp.sum(gc[h] * last, axis=0, keepdims=True) for h in hs]
    kt = [kn[h] * jnp.exp(gl[h] - gc[h]) for h in hs]
    nst = len(state)
    st, o, mids = list(state), [None] * nh, []
    for c0 in range(0, nh, nst):
        us = range(c0, c0 + nst)
        mids.append(tuple(st))
        ws = [bdot(w[h], st[h - c0], 1, 0) for h in us]
        qs = [bdot(qd[h], st[h - c0], 1, 0) for h in us]
        v_new = [u[h] - ws[h - c0] for h in us]
        av = [bdot(attn[h], v_new[h - c0], 1, 0) for h in us]
        kv = [bdot(kt[h], v_new[h - c0], 0, 0) for h in us]
        st = [st[h - c0] * jnp.exp(gl[h]) + kv[h - c0] for h in us]
        for h in us:
            o[h] = _rms(qs[h - c0] + av[h - c0], gain) * _silu(z[h])
    if with_starts:
        return tuple(o), tuple(st), tuple(mids)
    return tuple(o), tuple(st)


GDN_ROWS = 512
GDN_W = GDN_HEADS * GDN_DH


def gdn_fwd(cqkv, proj, gbb, gain):
    s = cqkv.shape[0]
    nb, cpb = s // GDN_ROWS, GDN_ROWS // GDN_CHUNK
    h4 = GDN_HEADS

    def kern(qkv_ref, z_ref, gb_ref, gain_ref, o_ref, st_ref, state):
        @pl.when(pl.program_id(0) == 0)
        def _():
            state[...] = jnp.zeros_like(state)

        gain_v = gain_ref[...]

        def step(ci, carry):
            sls = [pl.ds(pl.multiple_of((ci * GDN_UNROLL + c) * GDN_CHUNK, GDN_CHUNK), GDN_CHUNK)
                   for c in range(GDN_UNROLL)]
            ins = []
            for sl in sls:
                for h in range(h4):
                    ln = lambda base, h=h: slice(base + h * GDN_DH, base + (h + 1) * GDN_DH)
                    ins.append((qkv_ref[sl, ln(0)], qkv_ref[sl, ln(GDN_W)], qkv_ref[sl, ln(2 * GDN_W)],
                                z_ref[sl, ln(0)], gb_ref[sl, ln(0)], gb_ref[sl, ln(GDN_W)]))
            cols = [tuple(col) for col in zip(*ins)]
            o, new, starts = _gdn_chunk(*cols, tuple(state[h] for h in range(h4)), gain_v, with_starts=True)
            for c, sl in enumerate(sls):
                for h in range(h4):
                    st_ref[h, ci * GDN_UNROLL + c] = starts[c][h]
                    o_ref[sl, h * GDN_DH:(h + 1) * GDN_DH] = o[c * h4 + h]
            for h in range(h4):
                state[h] = new[h]
            return carry

        lax.fori_loop(0, cpb // GDN_UNROLL, step, 0)

    return pl.pallas_call(
        kern, name="gdn_fwd", grid=(nb,),
        in_specs=[pl.BlockSpec((GDN_ROWS, 3 * GDN_W), lambda i: (i, 0)),
                  pl.BlockSpec((GDN_ROWS, GDN_W), lambda i: (i, 6)),
                  pl.BlockSpec((GDN_ROWS, 2 * GDN_W), lambda i: (i, 0)),
                  pl.BlockSpec((1, GDN_DH), lambda i: (0, 0))],
        out_specs=[pl.BlockSpec((GDN_ROWS, GDN_W), lambda i: (i, 0)),
                   pl.BlockSpec((h4, cpb, GDN_DH, GDN_DH), lambda i: (0, i, 0, 0))],
        out_shape=[jax.ShapeDtypeStruct((s, GDN_W), F32),
                   jax.ShapeDtypeStruct((h4, s // GDN_CHUNK, GDN_DH, GDN_DH), F32)],
        scratch_shapes=[pltpu.VMEM((h4, GDN_DH, GDN_DH), F32)],
        compiler_params=_params(("arbitrary",)),
    )(cqkv, proj, gbb, gain)


def gdn_bwd(cqkv, proj, gbb, gain, states, d_mixed):
    s = cqkv.shape[0]
    nb, cpb = s // GDN_ROWS, GDN_ROWS // GDN_CHUNK
    h4 = GDN_HEADS

    def kern(qkv_ref, z_ref, gb_ref, gain_ref, st_ref, do_ref, dqkv_ref, dz_ref, dgb_ref, dgain_ref, dstate):
        @pl.when(pl.program_id(0) == 0)
        def _():
            dgain_ref[...] = jnp.zeros_like(dgain_ref)
            dstate[...] = jnp.zeros_like(dstate)

        gain_v = gain_ref[...]

        def step(t, carry):
            first = (cpb // GDN_UNROLL - 1 - t) * GDN_UNROLL
            sls = [pl.ds(pl.multiple_of((first + c) * GDN_CHUNK, GDN_CHUNK), GDN_CHUNK) for c in range(GDN_UNROLL)]
            prim, cot = [], []
            for sl in sls:
                for h in range(h4):
                    ln = lambda base, h=h: slice(base + h * GDN_DH, base + (h + 1) * GDN_DH)
                    prim.append((qkv_ref[sl, ln(0)], qkv_ref[sl, ln(GDN_W)], qkv_ref[sl, ln(2 * GDN_W)],
                                 z_ref[sl, ln(0)], gb_ref[sl, ln(0)], gb_ref[sl, ln(GDN_W)]))
                    cot.append(do_ref[sl, ln(0)])
            cols = [tuple(col) for col in zip(*prim)]
            st_in = tuple(st_ref[h, first] for h in range(h4))
            vjp = jax.vjp(_gdn_chunk, *cols, st_in, gain_v)[1]
            dq, dk, dv, dz, dg, db, dst, dgn = vjp((tuple(cot), tuple(dstate[h] for h in range(h4))))
            for c, sl in enumerate(sls):
                for h in range(h4):
                    ln = lambda base, h=h: slice(base + h * GDN_DH, base + (h + 1) * GDN_DH)
                    unit = c * h4 + h
                    dqkv_ref[sl, ln(0)] = dq[unit]
                    dqkv_ref[sl, ln(GDN_W)] = dk[unit]
                    dqkv_ref[sl, ln(2 * GDN_W)] = dv[unit]
                    dz_ref[sl, ln(0)] = dz[unit]
                    dgb_ref[sl, ln(0)] = dg[unit]
                    dgb_ref[sl, ln(GDN_W)] = db[unit]
            for h in range(h4):
                dstate[h] = dst[h]
            dgain_ref[...] += dgn
            return carry

        lax.fori_loop(0, cpb // GDN_UNROLL, step, 0)

    def rev(width, cblock=0):
        return pl.BlockSpec((GDN_ROWS, width), lambda i: (nb - 1 - i, cblock))

    return pl.pallas_call(
        kern, name="gdn_bwd", grid=(nb,),
        in_specs=[rev(3 * GDN_W), rev(GDN_W, 6), rev(2 * GDN_W), pl.BlockSpec((1, GDN_DH), lambda i: (0, 0)),
                  pl.BlockSpec((h4, cpb, GDN_DH, GDN_DH), lambda i: (0, nb - 1 - i, 0, 0)), rev(GDN_W, 1)],
        out_specs=[rev(3 * GDN_W), rev(GDN_W), rev(2 * GDN_W), pl.BlockSpec((1, GDN_DH), lambda i: (0, 0))],
        out_shape=[jax.ShapeDtypeStruct((s, 3 * GDN_W), F32), jax.ShapeDtypeStruct((s, GDN_W), F32),
                   jax.ShapeDtypeStruct((s, 2 * GDN_W), F32), jax.ShapeDtypeStruct((1, GDN_DH), F32)],
        scratch_shapes=[pltpu.VMEM((h4, GDN_DH, GDN_DH), F32)],
        compiler_params=_params(("arbitrary",)),
    )(cqkv, proj, gbb, gain, states, d_mixed)


def _gdn_gates(small, prm):
    w = GDN_HEADS * GDN_DH
    lane, head = _iota2((128, w), 0), _iota2((128, w), 1) // GDN_DH
    sel_b = (lane == SMALL_B + head).astype(F32)
    sel_a = (lane == SMALL_A + head).astype(F32)
    prow = _iota2((8, 128), 0)
    a_log = jnp.sum(prm * (prow == 0).astype(F32), axis=0, keepdims=True)
    dt_b = jnp.sum(prm * (prow == 1).astype(F32), axis=0, keepdims=True)
    beta = _sigmoid(mdot(small, sel_b))
    g = mdot(-jnp.exp(a_log) * _softplus(small + dt_b), sel_a)
    return g, beta


CONV_ROWS = 1024
CONV_COLS = 128
CONV_BLOCK0 = 1536 // CONV_COLS


def _shift_down(prev8, cur, s):
    ext = jnp.concatenate([prev8, cur], axis=0)
    return pltpu.roll(ext, s, 0)[8:]


def _shift_up(cur, next8, s):
    n = cur.shape[0]
    ext = jnp.concatenate([cur, next8], axis=0)
    return pltpu.roll(ext, n + 8 - s, 0)[:n]


def _conv_pre(x_ref, w, ci, nchunk):
    r0 = pl.multiple_of(ci * CONV_ROWS, CONV_ROWS)
    cur = x_ref[pl.ds(r0, CONV_ROWS), :]
    prev = x_ref[pl.ds(pl.multiple_of(jnp.maximum(r0 - 8, 0), 8), 8), :]
    prev = jnp.where(ci > 0, prev, 0.0)
    shifted = [cur] + [_shift_down(prev, cur, s) for s in range(1, CONV_W)]
    pre = w[CONV_W - 1:CONV_W, :] * cur
    for s in range(1, CONV_W):
        pre = pre + w[CONV_W - 1 - s:CONV_W - s, :] * shifted[s]
    return r0, pre, shifted


def conv_fwd(proj, conv_w8):
    s = proj.shape[0]
    nchunk = s // CONV_ROWS
    ncol = 3 * GDN_HEADS * GDN_DH // CONV_COLS

    def kern(x_ref, w_ref, y_ref):
        w = w_ref[...]

        def step(ci, carry):
            r0, pre, _ = _conv_pre(x_ref, w, ci, nchunk)
            y_ref[pl.ds(r0, CONV_ROWS), :] = _silu(pre)
            return carry

        lax.fori_loop(0, nchunk, step, 0)

    return pl.pallas_call(
        kern, name="conv_fwd", grid=(ncol,),
        in_specs=[pl.BlockSpec((s, CONV_COLS), lambda j: (0, CONV_BLOCK0 + j)),
                  pl.BlockSpec((8, CONV_COLS), lambda j: (0, j))],
        out_specs=pl.BlockSpec((s, CONV_COLS), lambda j: (0, j)),
        out_shape=jax.ShapeDtypeStruct((s, ncol * CONV_COLS), F32),
        compiler_params=_params(("parallel",)),
    )(proj, conv_w8)


def conv_bwd(proj, conv_w8, dy):
    s = proj.shape[0]
    nchunk = s // CONV_ROWS
    per = 3 * GDN_HEADS * GDN_DH // CONV_COLS
    outs = []
    for part in range(1):
        def kern(x_ref, w_ref, dy_ref, dx_ref, dw_ref, dpre_ref):
            w = w_ref[...]
            rows8 = _iota2((8, CONV_COLS), 0)

            def step1(ci, dw):
                r0, pre, shifted = _conv_pre(x_ref, w, ci, nchunk)
                sg = _sigmoid(pre)
                dpre = dy_ref[pl.ds(r0, CONV_ROWS), :] * sg * (1.0 + pre * (1.0 - sg))
                dpre_ref[pl.ds(r0, CONV_ROWS), :] = dpre
                for sh in range(CONV_W):
                    dw = dw + jnp.where(rows8 == CONV_W - 1 - sh, _colsum(dpre * shifted[sh]), 0.0)
                return dw

            dw_ref[...] = lax.fori_loop(0, nchunk, step1, jnp.zeros((8, CONV_COLS), F32))

            def step2(ci, carry):
                r0 = pl.multiple_of(ci * CONV_ROWS, CONV_ROWS)
                cur = dpre_ref[pl.ds(r0, CONV_ROWS), :]
                nxt = dpre_ref[pl.ds(pl.multiple_of(jnp.minimum(r0 + CONV_ROWS, s - 8), 8), 8), :]
                nxt = jnp.where(ci < nchunk - 1, nxt, 0.0)
                dx = w[CONV_W - 1:CONV_W, :] * cur
                for sh in range(1, CONV_W):
                    dx = dx + w[CONV_W - 1 - sh:CONV_W - sh, :] * _shift_up(cur, nxt, sh)
                dx_ref[pl.ds(r0, CONV_ROWS), :] = dx
                return carry

            lax.fori_loop(0, nchunk, step2, 0)

        outs.append(pl.pallas_call(
            kern, name=f"conv_bwd{part}", grid=(per,),
            in_specs=[pl.BlockSpec((s, CONV_COLS), lambda j, part=part: (0, CONV_BLOCK0 + part * per + j)),
                      pl.BlockSpec((8, CONV_COLS), lambda j, part=part: (0, part * per + j)),
                      pl.BlockSpec((s, CONV_COLS), lambda j: (0, j))],
            out_specs=[pl.BlockSpec((s, CONV_COLS), lambda j: (0, j)),
                       pl.BlockSpec((8, CONV_COLS), lambda j: (0, j))],
            out_shape=[jax.ShapeDtypeStruct((s, per * CONV_COLS), F32),
                       jax.ShapeDtypeStruct((8, per * CONV_COLS), F32)],
            scratch_shapes=[pltpu.VMEM((s, CONV_COLS), F32)],
            compiler_params=_params(("parallel",)),
        )(proj, conv_w8, dy))
    dx = jnp.concatenate([o[0] for o in outs], axis=1)
    dw = jnp.concatenate([o[1] for o in outs], axis=1)
    return dx, dw


FOXF_ROWS = 512
SMALL_BLOCK128 = 3584 // 128


def _log_sigmoid(x):
    return jnp.minimum(x, 0.0) - jnp.log(1.0 + jnp.exp(-jnp.abs(x)))


def fox_f_fwd(proj, bias_row):
    s = proj.shape[0]
    n = s // FOXF_ROWS

    def kern(x_ref, b_ref, f_ref, carry):
        @pl.when(pl.program_id(0) == 0)
        def _():
            carry[...] = jnp.zeros_like(carry)

        heads = _iota2((FOXF_ROWS, 128), 1) < FOX_HEADS
        lf = jnp.where(heads, _log_sigmoid(x_ref[...] + b_ref[...]), 0.0)
        ltri = (_iota2((FOXF_ROWS, FOXF_ROWS), 0) >= _iota2((FOXF_ROWS, FOXF_ROWS), 1)).astype(F32)
        c = hdot(ltri, lf) + carry[...]
        f_ref[...] = c
        carry[...] = c[FOXF_ROWS - 1:FOXF_ROWS, :]

    return pl.pallas_call(
        kern, name="fox_f_fwd", grid=(n,),
        in_specs=[pl.BlockSpec((FOXF_ROWS, 128), lambda i: (i, SMALL_BLOCK128)),
                  pl.BlockSpec((1, 128), lambda i: (0, 0))],
        out_specs=pl.BlockSpec((FOXF_ROWS, 128), lambda i: (i, 0)),
        out_shape=jax.ShapeDtypeStruct((s, 128), F32),
        scratch_shapes=[pltpu.VMEM((1, 128), F32)],
        compiler_params=_params(("arbitrary",)),
    )(proj, bias_row)


def fox_f_bwd(proj, bias_row, d_f):
    s = proj.shape[0]
    n = s // FOXF_ROWS

    def kern(x_ref, b_ref, df_ref, dx_ref, db_ref, carry):
        @pl.when(pl.program_id(0) == 0)
        def _():
            carry[...] = jnp.zeros_like(carry)
            db_ref[...] = jnp.zeros_like(db_ref)

        heads = _iota2((FOXF_ROWS, 128), 1) < FOX_HEADS
        utri = (_iota2((FOXF_ROWS, FOXF_ROWS), 0) <= _iota2((FOXF_ROWS, FOXF_ROWS), 1)).astype(F32)
        rc = hdot(utri, df_ref[...]) + carry[...]
        carry[...] = rc[0:1, :]
        dx = jnp.where(heads, rc * _sigmoid(-(x_ref[...] + b_ref[...])), 0.0)
        dx_ref[...] = dx
        db_ref[...] += _colsum(dx)

    return pl.pallas_call(
        kern, name="fox_f_bwd", grid=(n,),
        in_specs=[pl.BlockSpec((FOXF_ROWS, 128), lambda i: (n - 1 - i, SMALL_BLOCK128)),
                  pl.BlockSpec((1, 128), lambda i: (0, 0)),
                  pl.BlockSpec((FOXF_ROWS, 128), lambda i: (n - 1 - i, 0))],
        out_specs=[pl.BlockSpec((FOXF_ROWS, 128), lambda i: (n - 1 - i, 0)),
                   pl.BlockSpec((1, 128), lambda i: (0, 0))],
        out_shape=[jax.ShapeDtypeStruct((s, 128), F32), jax.ShapeDtypeStruct((1, 128), F32)],
        scratch_shapes=[pltpu.VMEM((1, 128), F32)],
        compiler_params=_params(("arbitrary",)),
    )(proj, bias_row, d_f)


FOX_T = 512
FOX_SCALE = FOX_DH ** -0.5
FOX_PAIRS = FOX_HEADS // 2
NEG = -1e30
_NT = (((1,), (1,)), ((), ()))


def _split3(x):
    def bf(v):
        return lax.reduce_precision(v, exponent_bits=8, mantissa_bits=7)

    hi = bf(x)
    mid = bf(x - hi)
    lo = bf(x - hi - mid)
    return jnp.stack([hi, mid, lo], axis=-1)


def _fox_extras(s, first, second):
    def part(v):
        if v is None:
            return jnp.zeros((s, FOX_HEADS, 3), F32)
        if isinstance(v, float):
            return jnp.full((s, FOX_HEADS, 3), v, F32)
        pairs = v.reshape(s, FOX_PAIRS, 2)
        return _split3(jnp.stack([pairs[:, :, 1], pairs[:, :, 0]], axis=-1).reshape(s, FOX_HEADS))

    cols = jnp.concatenate([part(first), part(second)], axis=-1)
    cols = _pad_to(cols, (s, FOX_HEADS, FOX_DH)).reshape(s, FOX_PAIRS, 2 * FOX_DH)
    return cols.transpose(1, 0, 2).astype(BF)


def _head_masks(rows):
    lane = _iota2((rows, 2 * FOX_DH), 1)
    return lane < FOX_DH, lane >= FOX_DH


def _extra_lane(e, slot):
    return (FOX_DH if e == 0 else 0) + slot


def fox_fwd(qkv, xq, xk, xv):
    s = qkv.shape[0]
    t = min(FOX_T, s)
    n = s // t

    def kern(q_ref, k_ref, v_ref, xq_ref, xk_ref, xv_ref, o_ref, lse_ref):
        i = pl.program_id(1)
        masks = _head_masks(t)
        q_pair, x_pair = q_ref[...] * FOX_SCALE, xq_ref[...]
        q_ops = [jnp.where(mk, q_pair, x_pair) for mk in masks]

        def step(j, carry, masked):
            sl = pl.ds(pl.multiple_of(j * t, t), t)
            k_pair, xk_pair, v_pair, xv_pair = k_ref[sl, :], xk_ref[sl, :], v_ref[sl, :], xv_ref[sl, :]
            k_ops = [jnp.where(mk, k_pair, xk_pair) for mk in masks]
            v_ops = [jnp.where(mk, v_pair, xv_pair) for mk in masks]
            sc = [lax.dot_general(q_ops[e], k_ops[e], _NT, preferred_element_type=F32) for e in range(2)]
            if masked:
                keep = _iota2((t, t), 0) >= _iota2((t, t), 1)
                sc = [jnp.where(keep, x, NEG) for x in sc]
            m_new = [jnp.maximum(carry[e][0], jnp.max(sc[e], axis=1, keepdims=True)) for e in range(2)]
            p = [jnp.exp(sc[e] - m_new[e]).astype(BF) for e in range(2)]
            pv = [jnp.dot(p[e], v_ops[e], preferred_element_type=F32) for e in range(2)]
            return tuple((m_new[e], jnp.exp(carry[e][0] - m_new[e]) * carry[e][1] + pv[e]) for e in range(2))

        init = tuple((jnp.full((t, 1), NEG, F32), jnp.zeros((t, 2 * FOX_DH), F32)) for _ in range(2))
        carry = lax.fori_loop(0, i, lambda j, c: step(j, c, False), init)
        carry = step(i, carry, True)
        lane = _iota2((t, 2 * FOX_DH), 1)
        outs, lses = [], []
        for e in range(2):
            m, acc = carry[e]
            l = jnp.sum(jnp.where(lane == _extra_lane(e, 0), acc, 0.0), axis=1, keepdims=True)
            outs.append(acc / l)
            lses.append(m + jnp.log(l))
        o_ref[...] = jnp.where(masks[0], outs[0], outs[1])
        head0 = 2 * pl.program_id(0)
        lse_ref[...] = jnp.where(lane == head0, lses[0], jnp.where(lane == head0 + 1, lses[1], 0.0))

    pr = FOX_PAIRS
    return pl.pallas_call(
        kern, name="fox_fwd", grid=(pr, n),
        in_specs=[pl.BlockSpec((t, 128), lambda p, i: (i, p)),
                  pl.BlockSpec((s, 128), lambda p, i: (0, pr + p)),
                  pl.BlockSpec((s, 128), lambda p, i: (0, 2 * pr + p)),
                  pl.BlockSpec((None, t, 128), lambda p, i: (p, i, 0)),
                  pl.BlockSpec((None, s, 128), lambda p, i: (p, 0, 0)),
                  pl.BlockSpec((None, s, 128), lambda p, i: (p, 0, 0))],
        out_specs=[pl.BlockSpec((t, 128), lambda p, i: (i, p)),
                   pl.BlockSpec((None, t, 128), lambda p, i: (p, i, 0))],
        out_shape=[jax.ShapeDtypeStruct((s, FOX_HEADS * FOX_DH), F32), jax.ShapeDtypeStruct((pr, s, 128), F32)],
        compiler_params=_params(("parallel", "parallel")),
    )(qkv, qkv, qkv, xq, xk, xv)


def fox_bwd(qkv, d_o, xk, xv, xqb, xdo):
    s = qkv.shape[0]
    t = min(FOX_T, s)
    n = s // t
    w = 2 * FOX_DH

    def both(blocks, slot):
        lane = _iota2(blocks[0].shape, 1)
        head0 = 2 * pl.program_id(0)
        own = jnp.where(lane < FOX_DH, blocks[0], blocks[1])
        sums = [jnp.sum(jnp.where(lane == _extra_lane(e, slot), blocks[e], 0.0), axis=1, keepdims=True)
                for e in range(2)]
        return own, jnp.where(lane == head0, sums[0], jnp.where(lane == head0 + 1, sums[1], 0.0))

    def kern(k_ref, v_ref, xk_ref, xv_ref, q_ref, do_ref, xq_ref, xd_ref,
             dq_ref, dk_ref, dv_ref, sq_ref, sk_ref, dq_acc):
        j = pl.program_id(1)

        @pl.when(j == 0)
        def _():
            dq_acc[...] = jnp.zeros_like(dq_acc)

        masks = _head_masks(t)
        k_ops = [jnp.where(mk, k_ref[...], xk_ref[...]) for mk in masks]
        v_ops = [jnp.where(mk, v_ref[...], xv_ref[...]) for mk in masks]
        k_t = [x.T for x in k_ops]

        def step(i, carry, masked):
            dk, dv = carry
            sl = pl.ds(pl.multiple_of(i * t, t), t)
            q_pair, xq_pair, do_pair, xd_pair = q_ref[sl, :] * FOX_SCALE, xq_ref[sl, :], do_ref[sl, :], xd_ref[sl, :]
            q_ops = [jnp.where(mk, q_pair, xq_pair) for mk in masks]
            do_ops = [jnp.where(mk, do_pair, xd_pair) for mk in masks]
            q_t = [x.T for x in q_ops]
            do_t = [jnp.where(mk, do_pair, 0).astype(BF).T for mk in masks]
            st = [lax.dot_general(k_ops[e], q_ops[e], _NT, preferred_element_type=F32) for e in range(2)]
            dp = [lax.dot_general(v_ops[e], do_ops[e], _NT, preferred_element_type=F32) for e in range(2)]
            if masked:
                keep = _iota2((t, t), 0) <= _iota2((t, t), 1)
                st = [jnp.where(keep, x, NEG) for x in st]
            pt = [jnp.exp(x) for x in st]
            dsb = [(pt[e] * dp[e]).astype(BF) for e in range(2)]
            dv = dv + sum(lax.dot_general(do_t[e], pt[e].astype(BF), _NT, preferred_element_type=F32)
                          for e in range(2))
            dk = tuple(dk[e] + lax.dot_general(q_t[e], dsb[e], _NT, preferred_element_type=F32) for e in range(2))
            for e in range(2):
                dq_acc[i, e * w:(e + 1) * w, :] += jnp.dot(k_t[e], dsb[e], preferred_element_type=F32)
            return dk, dv

        init = ((jnp.zeros((w, t), F32), jnp.zeros((w, t), F32)), jnp.zeros((w, t), F32))
        carry = step(j, init, True)
        dk, dv = lax.fori_loop(j + 1, n, lambda i, c: step(i, c, False), carry)
        dk_ref[...], sk_ref[...] = both([x.T for x in dk], 3)
        dv_ref[...] = dv.T

        @pl.when(j == n - 1)
        def _():
            def out(r, carry):
                sl = pl.ds(pl.multiple_of(r * t, t), t)
                own, sums = both([dq_acc[r, e * w:(e + 1) * w, :].T for e in range(2)], 0)
                dq_ref[sl, :] = own * FOX_SCALE
                sq_ref[sl, :] = sums
                return carry

            lax.fori_loop(0, n, out, 0)

    pr = FOX_PAIRS
    flat = jax.ShapeDtypeStruct((s, FOX_HEADS * FOX_DH), F32)
    tile = pl.BlockSpec((t, 128), lambda p, j: (j, p))
    whole = pl.BlockSpec((s, 128), lambda p, j: (0, p))
    return pl.pallas_call(
        kern, name="fox_bwd", grid=(pr, n),
        in_specs=[pl.BlockSpec((t, 128), lambda p, j: (j, pr + p)),
                  pl.BlockSpec((t, 128), lambda p, j: (j, 2 * pr + p)),
                  pl.BlockSpec((None, t, 128), lambda p, j: (p, j, 0)),
                  pl.BlockSpec((None, t, 128), lambda p, j: (p, j, 0)),
                  whole, whole,
                  pl.BlockSpec((None, s, 128), lambda p, j: (p, 0, 0)),
                  pl.BlockSpec((None, s, 128), lambda p, j: (p, 0, 0))],
        out_specs=[whole, tile, tile, whole, tile],
        out_shape=[flat] * 5,
        scratch_shapes=[pltpu.VMEM((n, 2 * w, t), F32)],
        compiler_params=_params(("parallel", "arbitrary")),
    )(qkv, qkv, xk, xv, qkv, d_o, xqb, xdo)


def _xattn_head(q, k, v):
    sc = bdot(q, k, 1, 1) * (MEM_DH ** -0.5)
    e = jnp.exp(sc - lax.stop_gradient(jnp.max(sc, axis=-1, keepdims=True)))
    p = e / jnp.sum(e, axis=-1, keepdims=True)
    return bdot(p, v, 1, 0)


def xattn_fwd(q, kv):
    s = q.shape[0]
    hh = MEM_HEADS

    def body(*vals):
        qs, ks, vs = vals[:hh], vals[hh:2 * hh], vals[2 * hh:]
        return jnp.concatenate([_xattn_head(qs[a], ks[a], vs[a]) for a in range(hh)], axis=1)

    return rowcall(body, [(q, MEM_DH, a) for a in range(hh)],
                   [(kv, MEM_DH, a) for a in range(2 * hh)],
                   [(hh * MEM_DH, BF)], rows=512, total=s, name="xattn_fwd")[0]


def xattn_bwd(q, kv, d_o):
    s = q.shape[0]
    hh = MEM_HEADS

    def body(*vals):
        qs, dos = vals[:hh], vals[hh:2 * hh]
        ks, vs = vals[2 * hh:3 * hh], vals[3 * hh:]
        dqs, dks, dvs = [], [], []
        for a in range(hh):
            _, vjp = jax.vjp(_xattn_head, qs[a], ks[a], vs[a])
            dq, dk, dv = vjp(dos[a])
            dqs.append(dq)
            dks.append(dk)
            dvs.append(dv)
        return jnp.concatenate(dqs, axis=1), jnp.concatenate(dks + dvs, axis=1)

    return rowcall(body, [(q, MEM_DH, a) for a in range(hh)] + [(d_o, MEM_DH, a) for a in range(hh)],
                   [(kv, MEM_DH, a) for a in range(2 * hh)],
                   [(hh * MEM_DH, BF)], [kv.shape], rows=512, total=s, name="xattn_bwd")


def _slab(ref, axis, start, size):
    if axis is None:
        return ref
    if axis == "lead":
        return ref.at[start]
    idx = pl.ds(pl.multiple_of(start, 128 if axis == 1 else 16), size)
    return ref.at[idx] if axis == 0 else ref.at[:, idx]


def exchange(inputs, outputs, transfers, name):
    ni, no, nt = len(inputs), len(outputs), len(transfers)
    npeer = N_DEV - 1

    def body(*refs):
        ins, outs = refs[:ni], refs[ni:ni + no]
        send, recv, loc = refs[ni + no:]
        x, y, c = lax.axis_index("x"), lax.axis_index("y"), lax.axis_index("c")
        me = 4 * x + 2 * y + c

        def peer(p):
            px = 1 - x if p & 4 else x
            py = 1 - y if p & 2 else y
            pc = 1 - c if p & 1 else c
            return (px, py, pc), 4 * px + 2 * py + pc

        def view(ref, spec, who):
            axis, off, stride, size = spec
            return _slab(ref, axis, off + who * stride, size)

        local, remote = [], []
        for w, (ii, src, oi, dst) in enumerate(transfers):
            cp = pltpu.make_async_copy(view(ins[ii], src, me), view(outs[oi], dst, me), loc.at[w])
            cp.start()
            local.append(cp)
        for p in range(1, N_DEV):
            dev, idx = peer(p)
            for w, (ii, src, oi, dst) in enumerate(transfers):
                k = w * npeer + p - 1
                out_cp = pltpu.make_async_remote_copy(
                    src_ref=view(ins[ii], src, idx), dst_ref=view(outs[oi], dst, me), send_sem=send.at[k],
                    recv_sem=recv.at[k], device_id=dev, device_id_type=MESH)
                out_cp.start()
                in_cp = pltpu.make_async_remote_copy(
                    src_ref=view(ins[ii], src, idx), dst_ref=view(outs[oi], dst, idx), send_sem=send.at[k],
                    recv_sem=recv.at[k], device_id=dev, device_id_type=MESH)
                remote.append((out_cp, in_cp))
        for out_cp, in_cp in remote:
            in_cp.wait_recv()
            out_cp.wait_send()
        for cp in local:
            cp.wait()

    hbm = pl.BlockSpec(memory_space=pl.ANY)
    return pl.pallas_call(
        body, name=name, in_specs=[hbm] * ni, out_specs=[hbm] * no, out_shape=list(outputs),
        scratch_shapes=[pltpu.SemaphoreType.DMA((nt * npeer,)), pltpu.SemaphoreType.DMA((nt * npeer,)),
                        pltpu.SemaphoreType.DMA((nt,))],
        compiler_params=pltpu.CompilerParams(has_side_effects=True),
    )(*inputs)


def _peer(p):
    x, y, c = lax.axis_index("x"), lax.axis_index("y"), lax.axis_index("c")
    px = 1 - x if p & 4 else x
    py = 1 - y if p & 2 else y
    pc = 1 - c if p & 1 else c
    return (px, py, pc), 4 * px + 2 * py + pc


def _view(ref, spec, who):
    axis, off, stride, size = spec
    return _slab(ref, axis, off + who * stride, size)


def place_own(inputs, outputs, transfers):
    me = 4 * lax.axis_index("x") + 2 * lax.axis_index("y") + lax.axis_index("c")
    lands = [lax.empty(o.shape, o.dtype) for o in outputs]
    for ii, src, oi, dst in transfers:
        axis, off, stride, size = src
        own = inputs[ii] if axis is None else lax.dynamic_slice_in_dim(inputs[ii], off + me * stride, size, axis)
        axis, off, stride, size = dst
        if axis == "lead":
            lands[oi] = lax.dynamic_update_slice_in_dim(lands[oi], own[None], me, 0)
        else:
            lands[oi] = lax.dynamic_update_slice_in_dim(lands[oi], own, off + me * stride, axis)
    return lands


_HBM = pl.BlockSpec(memory_space=pltpu.HBM)
_SEM = pl.BlockSpec(memory_space=pltpu.SEMAPHORE)
_EFFECT = pltpu.SideEffectType.DATAFLOW_SIDE_EFFECTING


def _remote_copies(ins, lands, transfers, send, recv):
    npeer = N_DEV - 1
    me = 4 * lax.axis_index("x") + 2 * lax.axis_index("y") + lax.axis_index("c")
    pairs = []
    for p in range(1, N_DEV):
        dev, idx = _peer(p)
        for w, (ii, src, oi, dst) in enumerate(transfers):
            k = w * npeer + p - 1
            common = dict(src_ref=_view(ins[ii], src, idx), send_sem=send.at[k], recv_sem=recv.at[k],
                          device_id=dev, device_id_type=MESH)
            pairs.append((pltpu.make_async_remote_copy(dst_ref=_view(lands[oi], dst, me), **common),
                          pltpu.make_async_remote_copy(dst_ref=_view(lands[oi], dst, idx), **common)))
    return pairs


def exchange_start(inputs, lands, transfers, after, name):
    ni, nl, nsem = len(inputs), len(lands), len(transfers) * (N_DEV - 1)

    def body(*refs):
        ins, lnd = refs[:ni], refs[ni:ni + nl]
        send, recv = refs[ni + nl + 1], refs[ni + nl + 2]
        token = refs[-1]
        for out_cp, _ in _remote_copies(ins, lnd, transfers, send, recv):
            out_cp.start()
        token[...] = jnp.zeros_like(token)

    args = [pltpu.with_memory_space_constraint(a, pltpu.HBM) for a in list(inputs) + list(lands)]
    res = pl.pallas_call(
        body, name=name,
        out_shape=(pltpu.SemaphoreType.DMA((nsem,)), pltpu.SemaphoreType.DMA((nsem,)),
                   *[pltpu.HBM(a.shape, a.dtype) for a in args], jax.ShapeDtypeStruct((8, 128), F32)),
        in_specs=[_HBM] * (ni + nl) + [pl.BlockSpec(memory_space=pl.ANY)],
        out_specs=(_SEM, _SEM, *[_HBM] * (ni + nl), pl.BlockSpec(memory_space=pltpu.VMEM)),
        input_output_aliases={k: k + 2 for k in range(ni + nl)},
        compiler_params=pltpu.CompilerParams(has_side_effects=_EFFECT),
    )(*args, after)
    return res[0], res[1], list(res[2:2 + ni]), list(res[2 + ni:2 + ni + nl]), res[-1]


def exchange_wait(send, recv, inputs, lands, after, transfers, name):
    ni, nl = len(inputs), len(lands)

    def body(*refs):
        ins, lnd = refs[:ni], refs[ni:ni + nl]
        send_r, recv_r = refs[ni + nl], refs[ni + nl + 1]
        for out_cp, in_cp in _remote_copies(ins, lnd, transfers, send_r, recv_r):
            out_cp.wait_send()
            in_cp.wait_recv()

    res = pl.pallas_call(
        body, name=name,
        out_shape=tuple(pltpu.HBM(a.shape, a.dtype) for a in list(inputs) + list(lands)),
        in_specs=[_HBM] * (ni + nl) + [_SEM, _SEM, pl.BlockSpec(memory_space=pl.ANY)],
        out_specs=tuple([_HBM] * (ni + nl)),
        input_output_aliases={k: k for k in range(ni + nl)},
        compiler_params=pltpu.CompilerParams(has_side_effects=_EFFECT),
    )(*inputs, *lands, send, recv, after)
    return list(res[ni:])


def adamw(w, m, v, contribs, name):
    r, c = w.shape
    nc = len(contribs)
    rows = next((r // d for d in (4, 2) if r % d == 0 and (r // d) % 16 == 0), r)
    c1, c2 = 1.0 - ADAM_B1 ** ADAM_STEP, 1.0 - ADAM_B2 ** ADAM_STEP

    def body(wv, mv, vv, *gs):
        g = gs[0].astype(F32)
        for extra in gs[1:]:
            g = g + extra.astype(F32)
        g = g[:, :c]
        m_new = ADAM_B1 * mv + (1.0 - ADAM_B1) * g
        v_new = ADAM_B2 * vv + (1.0 - ADAM_B2) * (g * g)
        delta = -ADAM_LR * ((m_new / c1) / (jnp.sqrt(v_new / c2) + ADAM_EPS) + ADAM_WD * wv)
        return g, delta, m_new, v_new

    assert nc >= 1
    return rowcall(body, [w, m, v] + list(contribs), [], [(c, F32)] * 4, rows=rows, total=r, name=name)


WEIGHTS = ['ffn1_pre_norm', 'ffn1_w_gate', 'ffn1_w_up', 'ffn1_w_down', 'ffn1_post_norm', 'mix_pre_norm', 'w_in',
           'fox_f_bias', 'gdn_conv_w', 'gdn_a_log', 'gdn_dt_bias', 'gdn_out_norm', 'w_out', 'mix_post_norm',
           'mem_pre_norm', 'mem_kv_norm', 'mem_w_q', 'mem_w_kv', 'mem_w_o', 'mem_post_norm', 'ffn2_pre_norm',
           'ffn2_w_gate', 'ffn2_w_up', 'ffn2_w_down', 'ffn2_post_norm']
GAINS = ['ffn1_pre_norm', 'ffn1_post_norm', 'mix_pre_norm', 'mix_post_norm', 'mem_pre_norm', 'mem_kv_norm',
         'mem_post_norm', 'ffn2_pre_norm', 'ffn2_post_norm']
BIG = ['ffn1_w_gate', 'ffn1_w_up', 'ffn1_w_down', 'w_in', 'w_out', 'mem_w_q', 'mem_w_kv', 'mem_w_o',
       'ffn2_w_gate', 'ffn2_w_up', 'ffn2_w_down']
PACK_ROWS = 24
ROW_MISC = len(GAINS)
ROW_CONV = ROW_MISC + 1
COL_FBIAS, COL_ALOG, COL_DTB, COL_ONORM, COL_LOSS = 0, 8, 12, 128, 256
CONV_CH = 3 * GDN_HEADS * GDN_DH


def _pad_to(a, shape):
    return jnp.pad(a, [(0, t - s) for s, t in zip(a.shape, shape)])


def _pack(get, conv=None, loss=None):
    rows = [get(nm) for nm in GAINS]
    misc = jnp.concatenate([get('fox_f_bias'), get('gdn_a_log'), get('gdn_dt_bias'),
                            jnp.zeros((1, COL_ONORM - COL_DTB - 4), F32), get('gdn_out_norm'),
                            jnp.zeros((1, 1), F32) if loss is None else loss.reshape(1, 1)], axis=1)
    rows.append(_pad_to(misc, (1, D_MODEL)))
    rows.append(jnp.zeros((6, D_MODEL), F32) if conv is None else conv.reshape(6, D_MODEL))
    return _pad_to(jnp.concatenate(rows, axis=0), (PACK_ROWS, D_MODEL))


def _unpack(p):
    out = {nm: p[i:i + 1] for i, nm in enumerate(GAINS)}
    misc = p[ROW_MISC:ROW_MISC + 1]
    out['fox_f_bias'] = misc[:, COL_FBIAS:COL_FBIAS + FOX_HEADS]
    out['gdn_a_log'] = misc[:, COL_ALOG:COL_ALOG + GDN_HEADS]
    out['gdn_dt_bias'] = misc[:, COL_DTB:COL_DTB + GDN_HEADS]
    out['gdn_out_norm'] = misc[:, COL_ONORM:COL_ONORM + GDN_DH]
    return out


def _ffn_fwd(h, pre, wgu, wd, tag, u=None):
    s = h.shape[0]
    if u is None:
        u, = rowcall(_rms, [h], [pre], [(D_MODEL, BF)], rows=512, total=s, name=tag + "_pre")
    if callable(wgu):
        wgu = wgu(u)
    gate, up, act = mm_swiglu(u, wgu, name=tag + "_gate_up")
    if callable(wd):
        wd = wd(act)
    f = mm(act, wd, name=tag + "_down")
    return u, gate, up, act, f


def _half_rms(a, g):
    return 0.5 * _rms(a, g)


def _ffn_bwd(dh_out, h, pre, post, wgu, wd, saved, tag, on_dwd=None, on_dwgu=None, post_done=None, first_token=None):
    u, gate, up, act, f = saved
    s = h.shape[0]

    def b_post(dh, fv, pg):
        return jax.vjp(_half_rms, fv, pg)[1](dh)

    if post_done is not None:
        df, dpost = post_done
    else:
        df, dpost = rowcall(b_post, [dh_out, f], [post], [(D_MODEL, BF)], [(1, D_MODEL)], rows=512, total=s,
                            name=tag + "_bwd_post")
    dwd = mm(act, df, ta=True, out_dtype=BF, name=tag + "_bwd_dwd", token=first_token)
    dgate, dup = mm_dswiglu(df, wd, gate, up, name=tag + "_bwd_dact", token=on_dwd(dwd) if on_dwd else None)
    dwg = mm(u, dgate, ta=True, out_dtype=BF, name=tag + "_bwd_dwg")
    dwu = mm(u, dup, ta=True, out_dtype=BF, name=tag + "_bwd_dwu")
    du = mm_pair(dgate, dup, wgu, name=tag + "_bwd_du", token=on_dwgu(dwg, dwu) if on_dwgu else None)

    def b_pre(dh, duv, hv, pg):
        dx, dpre = jax.vjp(_rms, hv, pg)[1](duv)
        return dh + dx, dpre

    dh, dpre = rowcall(b_pre, [dh_out, du, h], [pre], [(D_MODEL, F32)], [(1, D_MODEL)], rows=512, total=s,
                       name=tag + "_bwd_pre")
    return dh, dwg, dwu, dwd, dpre, dpost


def _bwd_residual(dh, a, g):
    return jax.vjp(_rms, a, g)[1](dh)


def _step(a):
    x, mem = a['x'][0], a['mem'][0]
    s = x.shape[0]
    me = 4 * lax.axis_index("x") + 2 * lax.axis_index("y") + lax.axis_index("c")
    w2 = {nm: a[nm][0] for nm in WEIGHTS}
    m2 = {nm: a['m_' + nm][0] for nm in WEIGHTS}
    v2 = {nm: a['v_' + nm][0] for nm in WEIGHTS}
    small = {nm: w2[nm][None] for nm in WEIGHTS if nm not in BIG and nm != 'gdn_conv_w'}

    def ff_cols(w):
        return _pad_to(w, (D_MODEL, FF_SHARD_PAD)).astype(BF)

    def ff_rows(w):
        return _pad_to(w, (FF_SHARD_PAD, D_MODEL)).astype(BF)

    whole = (None, 0, 0, 0)
    conv_pad = 256
    g_in = [ff_cols(w2['ffn1_w_gate']), ff_cols(w2['ffn1_w_up']), ff_rows(w2['ffn1_w_down']),
            ff_cols(w2['ffn2_w_gate']), ff_cols(w2['ffn2_w_up']), ff_rows(w2['ffn2_w_down']),
            _pad_to(w2['w_in'], (D_MODEL, IN_SHARD_PAD)).astype(BF), w2['w_out'].astype(BF),
            w2['mem_w_q'].astype(BF), w2['mem_w_kv'].astype(BF), w2['mem_w_o'].astype(BF),
            _pad_to(w2['gdn_conv_w'], (8, conv_pad))]
    g_out = [jax.ShapeDtypeStruct((D_MODEL, 2 * D_FF_PAD), BF), jax.ShapeDtypeStruct((D_FF_PAD, D_MODEL), BF),
             jax.ShapeDtypeStruct((D_MODEL, 2 * D_FF_PAD), BF), jax.ShapeDtypeStruct((D_FF_PAD, D_MODEL), BF),
             jax.ShapeDtypeStruct((D_MODEL, N_DEV * IN_SHARD_PAD), BF), jax.ShapeDtypeStruct((D_MODEL, D_MODEL), BF),
             jax.ShapeDtypeStruct((D_MODEL, D_MODEL), BF), jax.ShapeDtypeStruct((D_MODEL, 2 * D_MODEL), BF),
             jax.ShapeDtypeStruct((D_MODEL, D_MODEL), BF), jax.ShapeDtypeStruct((8, N_DEV * conv_pad), F32)]
    sp_, dm = FF_SHARD_PAD, D_MODEL // N_DEV
    g_tr = [(0, whole, 0, (1, 0, sp_, sp_)), (1, whole, 0, (1, D_FF_PAD, sp_, sp_)), (2, whole, 1, (0, 0, sp_, sp_)),
            (3, whole, 2, (1, 0, sp_, sp_)), (4, whole, 2, (1, D_FF_PAD, sp_, sp_)), (5, whole, 3, (0, 0, sp_, sp_)),
            (6, whole, 4, (1, 0, IN_SHARD_PAD, IN_SHARD_PAD)), (7, whole, 5, (0, 0, dm, dm)),
            (8, whole, 6, (0, 0, dm, dm)), (9, whole, 7, (1, 0, 2 * dm, 2 * dm)), (10, whole, 8, (0, 0, dm, dm)),
            (11, whole, 9, (1, 0, conv_pad, conv_pad))]
    def pick(idx):
        ins = sorted({g_tr[k][0] for k in idx})
        outs = sorted({g_tr[k][2] for k in idx})
        tr = [(ins.index(g_tr[k][0]), g_tr[k][1], outs.index(g_tr[k][2]), g_tr[k][3]) for k in idx]
        return [g_in[i] for i in ins], [g_out[o] for o in outs], tr

    stages, after = [], g_in[0]
    for nm, idx in (("gate_up", [0, 1]), ("down", [2]), ("mix", [6, 7, 11]), ("late", [8, 9, 10, 3, 4, 5])):
        st_in, st_out, st_tr = pick(idx)
        st = exchange_start(st_in, place_own(st_in, st_out, st_tr), st_tr, after, "gather_%s_start" % nm)
        stages.append((st, st_tr, "gather_%s_wait" % nm))
        after = st[4]
    g_token = after

    def gather_wait(k, after_):
        (send_, recv_, src_, land_, _), tr_, nm_ = stages[k]
        return exchange_wait(send_, recv_, src_, land_, after_, tr_, nm_)

    bias_row = _pad_to(small['fox_f_bias'], (1, 128))
    gate_prm = _pad_to(jnp.concatenate([_pad_to(small['gdn_a_log'], (1, 128 - SMALL_A)),
                                        _pad_to(small['gdn_dt_bias'], (1, 128 - SMALL_A))], axis=0),
                       (8, 128 - SMALL_A))
    gate_prm = jnp.pad(gate_prm, ((0, 0), (SMALL_A, 0)))
    onorm = small['gdn_out_norm']

    late = {}

    def wgu1_when(u):
        late['wgu1'], = gather_wait(0, u)
        return late['wgu1']

    def wd1_when(act):
        late['wd1'], = gather_wait(1, act)
        return late['wd1']

    sv1 = _ffn_fwd(x, small['ffn1_pre_norm'] + g_token[0, 0], wgu1_when, wd1_when, "ffn1")
    wgu1, wd1 = late['wgu1'], late['wd1']
    def b_out_pre(h, f, g_post, g_pre):
        hn = h + _half_rms(f, g_post)
        return hn, _rms(hn, g_pre)

    h1, u2 = rowcall(b_out_pre, [x, sv1[4]], [small['ffn1_post_norm'], small['mix_pre_norm']],
                     [(D_MODEL, F32), (D_MODEL, BF)], rows=512, total=s, name="ffn1_out")
    w_in_g, w_out, conv_g = gather_wait(2, h1)
    w_in = jnp.concatenate([w_in_g[:, j * IN_SHARD_PAD:j * IN_SHARD_PAD + IN_SHARD] for j in range(N_DEV)],
                           axis=1)
    sp = [0, 512, 1024, 1536, 1544, 2056, 2568, 3080, 3592, 3596, 3600]
    fq, fk, fv, ff, gq, gk, gv, gz, gb, ga = [w_in[:, sp[i]:sp[i + 1]] for i in range(10)]
    w_proj = jnp.concatenate([fq, fk, fv, gq, gk, gv, gz, ff, gb, ga,
                              jnp.zeros((D_MODEL, PROJ_W - 3584 - 16), BF)], axis=1)
    conv_w8 = conv_g.reshape(8, N_DEV, conv_pad)[:, :, :CONV_CH // N_DEV].reshape(8, CONV_CH)


    proj = mm(u2, w_proj, name="mix_proj")
    f_cum = fox_f_fwd(proj, bias_row)
    f_heads = f_cum[:, :FOX_HEADS]
    qkv_bf = proj[:, :3 * FOX_HEADS * FOX_DH].astype(BF)
    xk, xv = _fox_extras(s, 1.0, -f_heads), _fox_extras(s, 1.0, None)
    fox_flat, lse = fox_fwd(qkv_bf, _fox_extras(s, f_heads, 1.0), xk, xv)
    lse_heads = jnp.sum(lse, axis=0)[:, :FOX_HEADS]
    cqkv = conv_fwd(proj, conv_w8)
    g_l, b_l = rowcall(_gdn_gates, [(proj, 128, SMALL_BLOCK128)], [gate_prm], [(512, F32), (512, F32)],
                       rows=512, total=s, name="gdn_gates")
    gbb = jnp.concatenate([g_l, b_l], axis=1)
    gdn_o, states = gdn_fwd(cqkv, proj, gbb, onorm)
    mixed = jnp.concatenate([fox_flat, gdn_o], axis=1).astype(BF)
    mo = mm(mixed, w_out, name="mix_out")
    def b_res_pre(h, a_, g_post, g_pre):
        hn = h + _rms(a_, g_post)
        return hn, _rms(hn, g_pre)

    h2, hq = rowcall(b_res_pre, [h1, mo], [small['mix_post_norm'], small['mem_pre_norm']],
                     [(D_MODEL, F32), (D_MODEL, BF)], rows=512, total=s, name="mix_res")
    mn, = rowcall(_rms, [mem], [small['mem_kv_norm']], [(D_MODEL, BF)], rows=256, total=mem.shape[0], name="mem_kvn")
    wgu2, wd2, w_q, w_kv, w_o = gather_wait(3, h2)
    q_mem = mm(hq, w_q, name="mem_q")
    kv_mem = mm(mn, w_kv, name="mem_kv")
    o_mem = xattn_fwd(q_mem, kv_mem)
    c_mem = mm(o_mem, w_o, name="mem_o")
    h3, u3 = rowcall(b_res_pre, [h2, c_mem], [small['mem_post_norm'], small['ffn2_pre_norm']],
                     [(D_MODEL, F32), (D_MODEL, BF)], rows=512, total=s, name="mem_res")

    sv2 = _ffn_fwd(h3, small['ffn2_pre_norm'], wgu2, wd2, "ffn2", u=u3)

    def b_loss(h, f, tgt, g):
        err = h + _half_rms(f, g) - tgt
        part = 0.5 * jnp.sum(jnp.mean(err * err, axis=-1, keepdims=True), axis=0, keepdims=True)
        dyv = err * (1.0 / D_MODEL)
        dfv, dpost = jax.vjp(_half_rms, f, g)[1](dyv)
        return dyv, dfv, jnp.broadcast_to(part, (1, 128)), dpost

    dy, df2, loss_acc, dpost2 = rowcall(b_loss, [h3, sv2[4], a['loss_target'][0]], [small['ffn2_post_norm']],
                                        [(D_MODEL, F32), (D_MODEL, BF)], [(1, 128), (1, D_MODEL)], rows=512, total=s,
                                        name="loss")

    grads = {}
    dh3, dwg2, dwu2, dwd2, grads['ffn2_pre_norm'], grads['ffn2_post_norm'] = _ffn_bwd(
        dy, h3, small['ffn2_pre_norm'], small['ffn2_post_norm'], wgu2, wd2, sv2, "ffn2", post_done=(df2, dpost2))

    lead = ("lead", 0, 1, 0)

    def land(r, c, dt=BF):
        return jax.ShapeDtypeStruct((N_DEV, r, c), dt)

    ffn_tr = [(0, (1, 0, sp_, sp_), 0, lead), (1, (1, 0, sp_, sp_), 1, lead), (2, (0, 0, sp_, FF_SHARD), 2, lead)]
    ffn_land = [land(D_MODEL, sp_), land(D_MODEL, sp_), land(FF_SHARD, D_MODEL)]
    a_in = [dwg2, dwu2, dwd2]
    a_send, a_recv, a_src, a_land, a_token = exchange_start(a_in, place_own(a_in, ffn_land, ffn_tr), ffn_tr, dh3,
                                                            "reduce_ffn2_start")

    dc, grads['mem_post_norm'] = rowcall(_bwd_residual, [dh3, c_mem], [small['mem_post_norm'] + a_token[0, 0]],
                                         [(D_MODEL, BF)],
                                         [(1, D_MODEL)], rows=512, total=s, name="mem_bwd_res")
    d_o = mm(dc, w_o, tb=True, name="mem_bwd_do")
    dw_o = mm(o_mem, dc, ta=True, out_dtype=BF, name="mem_bwd_dwo")
    dq_mem, dkv = xattn_bwd(q_mem, kv_mem, d_o)
    dhq = mm(dq_mem, w_q, tb=True, name="mem_bwd_dhq")
    dw_q = mm(hq, dq_mem, ta=True, out_dtype=BF, name="mem_bwd_dwq")
    dmn = mm(dkv, w_kv, tb=True, name="mem_bwd_dmn")
    dw_kv = mm(mn, dkv, ta=True, out_dtype=BF, name="mem_bwd_dwkv")
    _, grads['mem_kv_norm'] = rowcall(lambda d, mv, g: jax.vjp(_rms, mv, g)[1](d), [dmn, mem],
                                      [small['mem_kv_norm']], [(D_MODEL, F32)], [(1, D_MODEL)], rows=256,
                                      total=mem.shape[0], name="mem_bwd_kvn")

    def b_pre(dh, duv, hv, pg):
        dx, dpre = jax.vjp(_rms, hv, pg)[1](duv)
        return dh + dx, dpre

    def b_pre_res(dh, duv, hv, mov, g_pre, g_post):
        dx, dpre = jax.vjp(_rms, hv, g_pre)[1](duv)
        dhn = dh + dx
        dmov, dpost = jax.vjp(_rms, mov, g_post)[1](dhn)
        return dhn, dmov, dpre, dpost

    dh2, dmo, grads['mem_pre_norm'], grads['mix_post_norm'] = rowcall(
        b_pre_res, [dh3, dhq, h2, mo], [small['mem_pre_norm'], small['mix_post_norm']],
        [(D_MODEL, F32), (D_MODEL, BF)], [(1, D_MODEL), (1, D_MODEL)], rows=512, total=s, name="mem_bwd_pre")
    d_mixed = mm(dmo, w_out, tb=True, name="mix_bwd_dmixed")
    dw_out = mm(mixed, dmo, ta=True, out_dtype=BF, name="mix_bwd_dwout")
    def b_delta(do, o):
        sel = (_iota2((512, 128), 0) // FOX_DH == _iota2((512, 128), 1)).astype(F32)
        return mdot(do * o, sel)

    delta, = rowcall(b_delta, [(d_mixed, 512, 0), fox_flat], [], [(128, F32)], rows=512, total=s, name="fox_delta")
    dfox_q, dfox_k, dvf, sum_q, sum_k = fox_bwd(qkv_bf, d_mixed[:, :512].astype(BF), xk, xv,
                                                _fox_extras(s, f_heads - lse_heads, 1.0),
                                                _fox_extras(s, -delta[:, :FOX_HEADS], None))
    d_f = jnp.sum((sum_q - sum_k).reshape(s, FOX_PAIRS, 2 * FOX_DH), axis=1)
    dsmall_f, dbias = fox_f_bwd(proj, bias_row, d_f)
    grads['fox_f_bias'] = dbias[:, :FOX_HEADS]
    dcqkv, dz, dgb, grads['gdn_out_norm'] = gdn_bwd(cqkv, proj, gbb, onorm, states, d_mixed)

    def b_gates(sm, dsf, dg, db, prm):
        dsm, dprm = jax.vjp(_gdn_gates, sm, prm)[1]((dg, db))
        return dsm + dsf, dprm

    dsmall, dprm = rowcall(b_gates, [(proj, 128, SMALL_BLOCK128), dsmall_f, (dgb, 512, 0), (dgb, 512, 1)], [gate_prm],
                           [(128, F32)],
                           [(8, 128)], rows=512, total=s, name="gdn_bwd_gates")
    grads['gdn_a_log'] = dprm[0:1, SMALL_A:SMALL_A + GDN_HEADS]
    grads['gdn_dt_bias'] = dprm[1:2, SMALL_A:SMALL_A + GDN_HEADS]
    dqkv_pre, dconv8 = conv_bwd(proj, conv_w8, dcqkv)
    dproj = jnp.concatenate([dfox_q, dfox_k, dvf, dqkv_pre, dz, dsmall,
                             jnp.zeros((s, PROJ_W - 3584 - 128), F32)], axis=1).astype(BF)
    du2 = mm(dproj, w_proj, tb=True, name="mix_bwd_du")
    dw_proj = mm(u2, dproj, ta=True, out_dtype=BF, name="mix_bwd_dwproj")
    def b_pre_post(dh, duv, hv, fv, g_pre, g_post):
        dx, dpre = jax.vjp(_rms, hv, g_pre)[1](duv)
        dhn = dh + dx
        dfv, dpost = jax.vjp(_half_rms, fv, g_post)[1](dhn)
        return dhn, dfv, dpre, dpost

    dh1, df1, grads['mix_pre_norm'], dpost1 = rowcall(
        b_pre_post, [dh2, du2, h1, sv1[4]], [small['mix_pre_norm'], small['ffn1_post_norm']],
        [(D_MODEL, F32), (D_MODEL, BF)], [(1, D_MODEL), (1, D_MODEL)], rows=512, total=s, name="mix_bwd_pre")

    dw_in = jnp.concatenate([dw_proj[:, :1536], dw_proj[:, 3584:3592], dw_proj[:, 1536:3584],
                             dw_proj[:, 3592:3600]], axis=1)
    gap = jnp.zeros((D_MODEL, IN_SHARD_PAD - IN_SHARD), BF)
    dw_in = jnp.concatenate([piece for j in range(N_DEV) for piece in (dw_in[:, j * IN_SHARD:(j + 1) * IN_SHARD], gap)],
                            axis=1)
    b_in = [dw_in, dw_out, dw_q, dw_kv, dw_o]
    b_tr = [(0, (1, 0, IN_SHARD_PAD, IN_SHARD_PAD), 0, lead), (1, (0, 0, dm, dm), 1, lead), (2, (0, 0, dm, dm), 2, lead),
            (3, (1, 0, 2 * dm, 2 * dm), 3, lead), (4, (0, 0, dm, dm), 4, lead)]
    b_shapes = [land(D_MODEL, IN_SHARD_PAD), land(dm, D_MODEL), land(dm, D_MODEL), land(D_MODEL, 2 * dm),
                land(dm, D_MODEL)]
    b_land = place_own(b_in, b_shapes, b_tr)
    b_send, b_recv, b_src, b_land, b_token = exchange_start(b_in, b_land, b_tr, dh1, "reduce_mix_start")

    def start_down_reduce(dwd):
        tr = ffn_tr[2:]
        tr = [(0, tr[0][1], 0, tr[0][3])]
        late['c_down'] = (exchange_start([dwd], place_own([dwd], ffn_land[2:], tr), tr, dwd, "reduce_ffn1_down_start"), tr)
        return late['c_down'][0][4]

    def start_gate_up_reduce(dwg, dwu):
        tr = ffn_tr[:2]
        late['c_gu'] = (exchange_start([dwg, dwu], place_own([dwg, dwu], ffn_land[:2], tr), tr, dwu,
                                       "reduce_ffn1_gu_start"), tr)
        return late['c_gu'][0][4]

    grad_x, _, _, _, grads['ffn1_pre_norm'], grads['ffn1_post_norm'] = _ffn_bwd(
        dh1, x, small['ffn1_pre_norm'], small['ffn1_post_norm'], wgu1, wd1, sv1, "ffn1",
        on_dwd=start_down_reduce, on_dwgu=start_gate_up_reduce, post_done=(df1, dpost1), first_token=b_token)

    gpack = _pack(lambda nm: grads[nm], conv=dconv8[:CONV_W], loss=loss_acc[:, :1])
    gsum_parts, = exchange([gpack], [land(PACK_ROWS, D_MODEL, F32)], [(0, whole, 0, lead)], "reduce_small")
    a_got = exchange_wait(a_send, a_recv, a_src, a_land, gsum_parts, ffn_tr, "reduce_ffn2_wait")
    b_got = exchange_wait(b_send, b_recv, b_src, b_land, gsum_parts, b_tr, "reduce_mix_wait")
    recv = dict(zip(['ffn2_w_gate', 'ffn2_w_up', 'ffn2_w_down', 'w_in', 'w_out', 'mem_w_q', 'mem_w_kv', 'mem_w_o'],
                    a_got + b_got))

    out_g, out_d, out_m, out_v = {}, {}, {}, {}

    def update(nm):
        r = recv[nm]
        res = adamw(w2[nm], m2[nm], v2[nm], [(r, r.shape[2], 0, d) for d in range(N_DEV)], "adamw_" + nm)
        out_g[nm], out_d[nm], out_m[nm], out_v[nm] = res

    for nm in recv:
        update(nm)
    wp = _pack(lambda nm: small[nm])
    mp = _pack(lambda nm: m2[nm][None])
    vp = _pack(lambda nm: v2[nm][None])
    pg, pd, pm, pv = adamw(wp, mp, vp, [(gsum_parts, D_MODEL, 0, d) for d in range(N_DEV)], "adamw_small")
    for dst, p in ((out_g, pg), (out_d, pd), (out_m, pm), (out_v, pv)):
        dst.update({k: val[0] for k, val in _unpack(p).items()})
    loss = pg[ROW_MISC, COL_LOSS]
    conv_g = lax.dynamic_slice_in_dim(pg[ROW_CONV:ROW_CONV + 6].reshape(CONV_W, CONV_CH), me * (CONV_CH // N_DEV),
                                      CONV_CH // N_DEV, axis=1)
    res = adamw(w2['gdn_conv_w'], m2['gdn_conv_w'], v2['gdn_conv_w'], [conv_g], "adamw_conv")
    out_g['gdn_conv_w'], out_d['gdn_conv_w'], out_m['gdn_conv_w'], out_v['gdn_conv_w'] = res

    done = sum(out_d[nm][0, 0] for nm in recv) + out_d['gdn_conv_w'][0, 0] + pd[0, 0]
    after = jnp.zeros((8, 128), F32) + done
    c_got = []
    for key, nm in (('c_gu', "reduce_ffn1_gu_wait"), ('c_down', "reduce_ffn1_down_wait")):
        (c_send, c_recv, c_src, c_land, _), tr = late[key]
        c_got += exchange_wait(c_send, c_recv, c_src, c_land, after, tr, nm)
    recv = dict(zip(['ffn1_w_gate', 'ffn1_w_up', 'ffn1_w_down'], c_got))
    for nm in recv:
        update(nm)

    def depth(t):
        return t[None]

    return (loss, grad_x[None], *[depth(out_g[nm]) for nm in WEIGHTS], *[depth(out_d[nm]) for nm in WEIGHTS],
            *[depth(out_m[nm]) for nm in WEIGHTS], *[depth(out_v[nm]) for nm in WEIGHTS])


def kernel(x, mem, ffn1_pre_norm, ffn1_w_gate, ffn1_w_up, ffn1_w_down, ffn1_post_norm, mix_pre_norm, w_in, fox_f_bias, gdn_conv_w, gdn_a_log, gdn_dt_bias, gdn_out_norm, w_out, mix_post_norm, mem_pre_norm, mem_kv_norm, mem_w_q, mem_w_kv, mem_w_o, mem_post_norm, ffn2_pre_norm, ffn2_w_gate, ffn2_w_up, ffn2_w_down, ffn2_post_norm, loss_target, m_ffn1_pre_norm, m_ffn1_w_gate, m_ffn1_w_up, m_ffn1_w_down, m_ffn1_post_norm, m_mix_pre_norm, m_w_in, m_fox_f_bias, m_gdn_conv_w, m_gdn_a_log, m_gdn_dt_bias, m_gdn_out_norm, m_w_out, m_mix_post_norm, m_mem_pre_norm, m_mem_kv_norm, m_mem_w_q, m_mem_w_kv, m_mem_w_o, m_mem_post_norm, m_ffn2_pre_norm, m_ffn2_w_gate, m_ffn2_w_up, m_ffn2_w_down, m_ffn2_post_norm, v_ffn1_pre_norm, v_ffn1_w_gate, v_ffn1_w_up, v_ffn1_w_down, v_ffn1_post_norm, v_mix_pre_norm, v_w_in, v_fox_f_bias, v_gdn_conv_w, v_gdn_a_log, v_gdn_dt_bias, v_gdn_out_norm, v_w_out, v_mix_post_norm, v_mem_pre_norm, v_mem_kv_norm, v_mem_w_q, v_mem_w_kv, v_mem_w_o, v_mem_post_norm, v_ffn2_pre_norm, v_ffn2_w_gate, v_ffn2_w_up, v_ffn2_w_down, v_ffn2_post_norm):
    return _step(dict(locals()))
```

```python
import functools

import jax
import jax.numpy as jnp
from jax import lax
from jax.experimental import pallas as pl
from jax.experimental.pallas import tpu as pltpu

F32 = jnp.float32
BF = jnp.bfloat16
HI = lax.Precision.HIGHEST
MESH = pl.DeviceIdType.MESH

N_DEV = 8
EPS = 1e-6
D_MODEL = 1024
D_FF = 2816
FF_SHARD = D_FF // N_DEV
FF_SHARD_PAD = 384
D_FF_PAD = FF_SHARD_PAD * N_DEV
FOX_HEADS, FOX_DH = 8, 64
GDN_HEADS, GDN_DH = 4, 128
GDN_CHUNK = 64
CONV_W = 4
MEM_HEADS, MEM_DH = 4, 256
IN_W = 3600
IN_SHARD = IN_W // N_DEV
IN_SHARD_PAD = 512
PROJ_W = 4096
SMALL_F, SMALL_B, SMALL_A = 0, 8, 12

ADAM_LR, ADAM_B1, ADAM_B2, ADAM_EPS, ADAM_WD, ADAM_STEP = 0.001, 0.9, 0.999, 1e-08, 0.01, 10

VMEM_LIMIT = 56 * 1024 * 1024


def _params(sem=None):
    return pltpu.CompilerParams(dimension_semantics=sem, vmem_limit_bytes=VMEM_LIMIT)


def _tile(n, pref, unit=128):
    if n <= pref:
        return n
    t = (pref // unit) * unit
    while t > unit and n % t:
        t -= unit
    assert n % t == 0, (n, pref)
    return t


@functools.partial(jax.custom_vjp, nondiff_argnums=(2, 3))
def bdot(a, b, ca, cb):
    return lax.dot_general(a.astype(BF), b.astype(BF), (((ca,), (cb,)), ((), ())), preferred_element_type=F32)


def _bdot_fwd(a, b, ca, cb):
    return bdot(a, b, ca, cb), (a, b)


def _bdot_bwd(ca, cb, res, g):
    a, b = res
    da = bdot(g, b, 1, 1 - cb) if ca == 1 else bdot(b, g, 1 - cb, 1)
    db = bdot(a, g, 1 - ca, 0) if cb == 0 else bdot(g, a, 0, 1 - ca)
    return da, db


bdot.defvjp(_bdot_fwd, _bdot_bwd)


def hdot(a, b):
    return jnp.dot(a, b, precision=HI, preferred_element_type=F32)


def mdot(a, b):
    return jnp.dot(a, b, precision=lax.Precision.HIGH, preferred_element_type=F32)


def _iota2(shape, dim):
    return lax.broadcasted_iota(jnp.int32, shape, dim)


def _sigmoid(x):
    return 1.0 / (1.0 + jnp.exp(-x))


def _silu(x):
    return x * _sigmoid(x)


def _softplus(x):
    return jnp.maximum(x, 0.0) + jnp.log(1.0 + jnp.exp(-jnp.abs(x)))


def _rms(x, gain):
    return x * lax.rsqrt(jnp.mean(x * x, axis=-1, keepdims=True) + EPS) * gain


def mm(a, b, *, name, ta=False, tb=False, out_dtype=F32, tm=1024, tn=1024, tk=1024, token=None):
    m, k = (a.shape[1], a.shape[0]) if ta else a.shape
    n = b.shape[0] if tb else b.shape[1]
    assert k == (b.shape[1] if tb else b.shape[0]), (a.shape, b.shape, ta, tb)
    tm, tn, tk = _tile(m, tm), _tile(n, tn), _tile(k, tk)
    nk = k // tk
    dims = (((0 if ta else 1,), (1 if tb else 0,)), ((), ()))

    def kern(a_ref, b_ref, *rest):
        o_ref, scratch = (rest[1], rest[2:]) if token is not None else (rest[0], rest[1:])

        def part():
            return lax.dot_general(a_ref[...].astype(BF), b_ref[...].astype(BF), dims, preferred_element_type=F32)

        if nk == 1:
            o_ref[...] = part().astype(o_ref.dtype)
            return
        acc_ref, = scratch
        kk = pl.program_id(2)

        @pl.when(kk == 0)
        def _():
            acc_ref[...] = part()

        @pl.when(kk > 0)
        def _():
            acc_ref[...] += part()

        @pl.when(kk == nk - 1)
        def _():
            o_ref[...] = acc_ref[...].astype(o_ref.dtype)

    a_spec = pl.BlockSpec((tk, tm), lambda i, j, kk: (kk, i)) if ta else pl.BlockSpec((tm, tk), lambda i, j, kk: (i, kk))
    b_spec = pl.BlockSpec((tn, tk), lambda i, j, kk: (j, kk)) if tb else pl.BlockSpec((tk, tn), lambda i, j, kk: (kk, j))
    return pl.pallas_call(
        kern, name=name, grid=(m // tm, n // tn, nk),
        in_specs=[a_spec, b_spec] + ([pl.BlockSpec((8, 128), lambda i, j, kk: (0, 0))] if token is not None else []),
        out_specs=pl.BlockSpec((tm, tn), lambda i, j, kk: (i, j)),
        out_shape=jax.ShapeDtypeStruct((m, n), out_dtype),
        scratch_shapes=[pltpu.VMEM((tm, tn), F32)] if nk > 1 else [],
        compiler_params=_params(("parallel", "parallel", "arbitrary")),
    )(*((a, b) if token is None else (a, b, token)))


def mm_swiglu(a, wgu, *, name):
    m, k = a.shape
    nh = wgu.shape[1] // 2
    tm, tn = _tile(m, 1024), _tile(nh, 512)
    nj = nh // tn

    def kern(a_ref, bg_ref, bu_ref, g_ref, u_ref, act_ref):
        av = a_ref[...]
        g = jnp.dot(av, bg_ref[...], preferred_element_type=F32).astype(BF)
        u = jnp.dot(av, bu_ref[...], preferred_element_type=F32).astype(BF)
        g_ref[...] = g
        u_ref[...] = u
        act_ref[...] = (_silu(g.astype(F32)) * u.astype(F32)).astype(BF)

    tile = pl.BlockSpec((tm, tn), lambda i, j: (i, j))
    out = jax.ShapeDtypeStruct((m, nh), BF)
    return pl.pallas_call(
        kern, name=name, grid=(m // tm, nj),
        in_specs=[pl.BlockSpec((tm, k), lambda i, j: (i, 0)), pl.BlockSpec((k, tn), lambda i, j: (0, j)),
                  pl.BlockSpec((k, tn), lambda i, j: (0, j + nj))],
        out_specs=[tile, tile, tile], out_shape=[out, out, out],
        compiler_params=_params(("parallel", "parallel")),
    )(a, wgu, wgu)


def mm_dswiglu(df, wd, gate, up, *, name, token=None):
    m, k = df.shape
    nh = wd.shape[0]
    tm, tn = _tile(m, 1024), _tile(nh, 512)

    def kern(df_ref, wd_ref, g_ref, u_ref, *rest):
        dg_ref, du_ref = rest[-2:]
        da = lax.dot_general(df_ref[...], wd_ref[...], (((1,), (1,)), ((), ())), preferred_element_type=F32)
        g, u = g_ref[...].astype(F32), u_ref[...].astype(F32)
        sg = _sigmoid(g)
        dg_ref[...] = (da * u * (sg * (1.0 + g * (1.0 - sg)))).astype(BF)
        du_ref[...] = (da * (g * sg)).astype(BF)

    tile = pl.BlockSpec((tm, tn), lambda i, j: (i, j))
    out = jax.ShapeDtypeStruct((m, nh), BF)
    extra = [pl.BlockSpec((8, 128), lambda i, j: (0, 0))] if token is not None else []
    return pl.pallas_call(
        kern, name=name, grid=(m // tm, nh // tn),
        in_specs=[pl.BlockSpec((tm, k), lambda i, j: (i, 0)), pl.BlockSpec((tn, k), lambda i, j: (j, 0)), tile, tile]
        + extra,
        out_specs=[tile, tile], out_shape=[out, out],
        compiler_params=_params(("parallel", "parallel")),
    )(*((df, wd, gate, up) if token is None else (df, wd, gate, up, token)))


def mm_pair(a1, a2, wgu, *, name, token=None):
    m, nh = a1.shape
    n = wgu.shape[0]
    tm, tn, tk = _tile(m, 1024), _tile(n, 1024), _tile(nh, 1024)
    nk = nh // tk
    nt = (((1,), (1,)), ((), ()))

    def kern(a1_ref, a2_ref, b1_ref, b2_ref, *rest):
        o_ref, acc_ref = rest[-2:]
        kk = pl.program_id(2)

        def part():
            return (lax.dot_general(a1_ref[...], b1_ref[...], nt, preferred_element_type=F32)
                    + lax.dot_general(a2_ref[...], b2_ref[...], nt, preferred_element_type=F32))

        @pl.when(kk == 0)
        def _():
            acc_ref[...] = part()

        @pl.when(kk > 0)
        def _():
            acc_ref[...] += part()

        @pl.when(kk == nk - 1)
        def _():
            o_ref[...] = acc_ref[...]

    a_spec = pl.BlockSpec((tm, tk), lambda i, j, kk: (i, kk))
    extra = [pl.BlockSpec((8, 128), lambda i, j, kk: (0, 0))] if token is not None else []
    return pl.pallas_call(
        kern, name=name, grid=(m // tm, n // tn, nk),
        in_specs=[a_spec, a_spec, pl.BlockSpec((tn, tk), lambda i, j, kk: (j, kk)),
                  pl.BlockSpec((tn, tk), lambda i, j, kk: (j, kk + nk))] + extra,
        out_specs=pl.BlockSpec((tm, tn), lambda i, j, kk: (i, j)),
        out_shape=jax.ShapeDtypeStruct((m, n), F32),
        scratch_shapes=[pltpu.VMEM((tm, tn), F32)],
        compiler_params=_params(("parallel", "parallel", "arbitrary")),
    )(*((a1, a2, wgu, wgu) if token is None else (a1, a2, wgu, wgu, token)))


def _row_spec(item, rows):
    if not isinstance(item, tuple):
        return item, pl.BlockSpec((rows, item.shape[1]), lambda i: (i, 0))
    if len(item) == 3:
        arr, w, c = item
        return arr, pl.BlockSpec((rows, w), lambda i: (i, c))
    arr, w, c, lead = item
    return arr, pl.BlockSpec((None, rows, w), lambda i: (lead, i, c))


def _whole_spec(item):
    if not isinstance(item, tuple):
        return item, pl.BlockSpec(item.shape, lambda i: (0,) * item.ndim)
    arr, w, c = item
    return arr, pl.BlockSpec((arr.shape[0], w), lambda i: (0, c))


def rowcall(body, tiled, whole, outs, accs=(), *, rows, total, name):
    rows = min(rows, total)
    assert total % rows == 0
    t_arr, t_spec = zip(*[_row_spec(t, rows) for t in tiled])
    w_arr, w_spec = zip(*[_whole_spec(w) for w in whole]) if whole else ((), ())
    nt, nw, no, na = len(t_arr), len(w_arr), len(outs), len(accs)

    def kern(*refs):
        vals = [r[...] for r in refs[:nt + nw]]
        res = body(*vals)
        if not isinstance(res, (tuple, list)):
            res = (res,)
        assert len(res) == no + na, (name, len(res), no, na)
        for r, v in zip(refs[nt + nw:nt + nw + no], res[:no]):
            r[...] = v.astype(r.dtype)
        if na:
            acc_refs = refs[nt + nw + no:]

            @pl.when(pl.program_id(0) == 0)
            def _():
                for r in acc_refs:
                    r[...] = jnp.zeros_like(r)

            for r, v in zip(acc_refs, res[no:]):
                r[...] += v

    out_shape = [jax.ShapeDtypeStruct((total, w), d) for w, d in outs] + [jax.ShapeDtypeStruct(s, F32) for s in accs]
    out_specs = [pl.BlockSpec((rows, w), lambda i: (i, 0)) for w, _ in outs] + \
                [pl.BlockSpec(s, lambda i: (0, 0)) for s in accs]
    res = pl.pallas_call(
        kern, name=name, grid=(total // rows,),
        in_specs=list(t_spec) + list(w_spec), out_specs=out_specs, out_shape=out_shape,
        compiler_params=_params(("arbitrary",) if na else ("parallel",)),
    )(*t_arr, *w_arr)
    return res


def _colsum(x):
    return jnp.sum(x, axis=0, keepdims=True)


GDN_UNROLL = 4


def _gdn_chunk(q, k, v, z, gb, bb, state, gain, with_starts=False):
    c = GDN_CHUNK
    nh = len(q)
    hs = range(nh)
    r64, c64 = _iota2((c, c), 0), _iota2((c, c), 1)
    incl = r64 >= c64
    strict = r64 > c64
    ltri = incl.astype(F32)
    eye = (r64 == c64).astype(F32)
    pick = (_iota2((GDN_DH, c), 0) == _iota2((GDN_DH, c), 1)).astype(F32)
    last = (_iota2((c, GDN_DH), 0) == c - 1).astype(F32)

    qn = [q[h] * lax.rsqrt(jnp.sum(q[h] * q[h], axis=-1, keepdims=True) + EPS) * (GDN_DH ** -0.5) for h in hs]
    kn = [k[h] * lax.rsqrt(jnp.sum(k[h] * k[h], axis=-1, keepdims=True) + EPS) for h in hs]
    gc = [mdot(ltri, gb[h]) for h in hs]
    gcol = [mdot(gc[h], pick) for h in hs]
    dec = [jnp.exp(jnp.where(incl, gcol[h] - gcol[h].T, -1e30)) for h in hs]
    kb = [kn[h] * bb[h] for h in hs]
    vb = [v[h] * bb[h] for h in hs]
    kk = [bdot(kb[h], kn[h], 1, 1) for h in hs]
    p = [-jnp.where(strict, kk[h] * dec[h], 0.0) for h in hs]
    tinv = [eye + p[h] for h in hs]
    for level in range(5):
        dot = mdot if level < 2 else (lambda a, b: bdot(a, b, 1, 0))
        p = [dot(p[h], p[h]) for h in hs]
        tinv = [tinv[h] + dot(tinv[h], p[h]) for h in hs]
    egc = [jnp.exp(gc[h]) for h in hs]
    u = [mdot(tinv[h], vb[h]) for h in hs]
    w = [mdot(tinv[h], kb[h] * egc[h]) for h in hs]
    attn = [bdot(qn[h], kn[h], 1, 1) * dec[h] for h in hs]
    qd = [qn[h] * egc[h] for h in hs]
    gl = [jnp.sum(gc[h] * last, axis=0, keepdims=True) for h in hs]
    kt = [kn[h] * jnp.exp(gl[h] - gc[h]) for h in hs]
    nst = len(state)
    st, o, mids = list(state), [None] * nh, []
    for c0 in range(0, nh, nst):
        us = range(c0, c0 + nst)
        mids.append(tuple(st))
        ws = [bdot(w[h], st[h - c0], 1, 0) for h in us]
        qs = [bdot(qd[h], st[h - c0], 1, 0) for h in us]
        v_new = [u[h] - ws[h - c0] for h in us]
        av = [bdot(attn[h], v_new[h - c0], 1, 0) for h in us]
        kv = [bdot(kt[h], v_new[h - c0], 0, 0) for h in us]
        st = [st[h - c0] * jnp.exp(gl[h]) + kv[h - c0] for h in us]
        for h in us:
            o[h] = _rms(qs[h - c0] + av[h - c0], gain) * _silu(z[h])
    if with_starts:
        return tuple(o), tuple(st), tuple(mids)
    return tuple(o), tuple(st)


GDN_ROWS = 512
GDN_W = GDN_HEADS * GDN_DH


def gdn_fwd(cqkv, proj, gbb, gain):
    s = cqkv.shape[0]
    nb, cpb = s // GDN_ROWS, GDN_ROWS // GDN_CHUNK
    h4 = GDN_HEADS

    def kern(qkv_ref, z_ref, gb_ref, gain_ref, o_ref, st_ref, state):
        @pl.when(pl.program_id(0) == 0)
        def _():
            state[...] = jnp.zeros_like(state)

        gain_v = gain_ref[...]

        def step(ci, carry):
            sls = [pl.ds(pl.multiple_of((ci * GDN_UNROLL + c) * GDN_CHUNK, GDN_CHUNK), GDN_CHUNK)
                   for c in range(GDN_UNROLL)]
            ins = []
            for sl in sls:
                for h in range(h4):
                    ln = lambda base, h=h: slice(base + h * GDN_DH, base + (h + 1) * GDN_DH)
                    ins.append((qkv_ref[sl, ln(0)], qkv_ref[sl, ln(GDN_W)], qkv_ref[sl, ln(2 * GDN_W)],
                                z_ref[sl, ln(0)], gb_ref[sl, ln(0)], gb_ref[sl, ln(GDN_W)]))
            cols = [tuple(col) for col in zip(*ins)]
            o, new, starts = _gdn_chunk(*cols, tuple(state[h] for h in range(h4)), gain_v, with_starts=True)
            for c, sl in enumerate(sls):
                for h in range(h4):
                    st_ref[h, ci * GDN_UNROLL + c] = starts[c][h]
                    o_ref[sl, h * GDN_DH:(h + 1) * GDN_DH] = o[c * h4 + h]
            for h in range(h4):
                state[h] = new[h]
            return carry

        lax.fori_loop(0, cpb // GDN_UNROLL, step, 0)

    return pl.pallas_call(
        kern, name="gdn_fwd", grid=(nb,),
        in_specs=[pl.BlockSpec((GDN_ROWS, 3 * GDN_W), lambda i: (i, 0)),
                  pl.BlockSpec((GDN_ROWS, GDN_W), lambda i: (i, 6)),
                  pl.BlockSpec((GDN_ROWS, 2 * GDN_W), lambda i: (i, 0)),
                  pl.BlockSpec((1, GDN_DH), lambda i: (0, 0))],
        out_specs=[pl.BlockSpec((GDN_ROWS, GDN_W), lambda i: (i, 0)),
                   pl.BlockSpec((h4, cpb, GDN_DH, GDN_DH), lambda i: (0, i, 0, 0))],
        out_shape=[jax.ShapeDtypeStruct((s, GDN_W), F32),
                   jax.ShapeDtypeStruct((h4, s // GDN_CHUNK, GDN_DH, GDN_DH), F32)],
        scratch_shapes=[pltpu.VMEM((h4, GDN_DH, GDN_DH), F32)],
        compiler_params=_params(("arbitrary",)),
    )(cqkv, proj, gbb, gain)


def gdn_bwd(cqkv, proj, gbb, gain, states, d_mixed):
    s = cqkv.shape[0]
    nb, cpb = s // GDN_ROWS, GDN_ROWS // GDN_CHUNK
    h4 = GDN_HEADS

    def kern(qkv_ref, z_ref, gb_ref, gain_ref, st_ref, do_ref, dqkv_ref, dz_ref, dgb_ref, dgain_ref, dstate):
        @pl.when(pl.program_id(0) == 0)
        def _():
            dgain_ref[...] = jnp.zeros_like(dgain_ref)
            dstate[...] = jnp.zeros_like(dstate)

        gain_v = gain_ref[...]

        def step(t, carry):
            first = (cpb // GDN_UNROLL - 1 - t) * GDN_UNROLL
            sls = [pl.ds(pl.multiple_of((first + c) * GDN_CHUNK, GDN_CHUNK), GDN_CHUNK) for c in range(GDN_UNROLL)]
            prim, cot = [], []
            for sl in sls:
                for h in range(h4):
                    ln = lambda base, h=h: slice(base + h * GDN_DH, base + (h + 1) * GDN_DH)
                    prim.append((qkv_ref[sl, ln(0)], qkv_ref[sl, ln(GDN_W)], qkv_ref[sl, ln(2 * GDN_W)],
                                 z_ref[sl, ln(0)], gb_ref[sl, ln(0)], gb_ref[sl, ln(GDN_W)]))
                    cot.append(do_ref[sl, ln(0)])
            cols = [tuple(col) for col in zip(*prim)]
            st_in = tuple(st_ref[h, first] for h in range(h4))
            vjp = jax.vjp(_gdn_chunk, *cols, st_in, gain_v)[1]
            dq, dk, dv, dz, dg, db, dst, dgn = vjp((tuple(cot), tuple(dstate[h] for h in range(h4))))
            for c, sl in enumerate(sls):
                for h in range(h4):
                    ln = lambda base, h=h: slice(base + h * GDN_DH, base + (h + 1) * GDN_DH)
                    unit = c * h4 + h
                    dqkv_ref[sl, ln(0)] = dq[unit]
                    dqkv_ref[sl, ln(GDN_W)] = dk[unit]
                    dqkv_ref[sl, ln(2 * GDN_W)] = dv[unit]
                    dz_ref[sl, ln(0)] = dz[unit]
                    dgb_ref[sl, ln(0)] = dg[unit]
                    dgb_ref[sl, ln(GDN_W)] = db[unit]
            for h in range(h4):
                dstate[h] = dst[h]
            dgain_ref[...] += dgn
            return carry

        lax.fori_loop(0, cpb // GDN_UNROLL, step, 0)

    def rev(width, cblock=0):
        return pl.BlockSpec((GDN_ROWS, width), lambda i: (nb - 1 - i, cblock))

    return pl.pallas_call(
        kern, name="gdn_bwd", grid=(nb,),
        in_specs=[rev(3 * GDN_W), rev(GDN_W, 6), rev(2 * GDN_W), pl.BlockSpec((1, GDN_DH), lambda i: (0, 0)),
                  pl.BlockSpec((h4, cpb, GDN_DH, GDN_DH), lambda i: (0, nb - 1 - i, 0, 0)), rev(GDN_W, 1)],
        out_specs=[rev(3 * GDN_W), rev(GDN_W), rev(2 * GDN_W), pl.BlockSpec((1, GDN_DH), lambda i: (0, 0))],
        out_shape=[jax.ShapeDtypeStruct((s, 3 * GDN_W), F32), jax.ShapeDtypeStruct((s, GDN_W), F32),
                   jax.ShapeDtypeStruct((s, 2 * GDN_W), F32), jax.ShapeDtypeStruct((1, GDN_DH), F32)],
        scratch_shapes=[pltpu.VMEM((h4, GDN_DH, GDN_DH), F32)],
        compiler_params=_params(("arbitrary",)),
    )(cqkv, proj, gbb, gain, states, d_mixed)


def _gdn_gates(small, prm):
    w = GDN_HEADS * GDN_DH
    lane, head = _iota2((128, w), 0), _iota2((128, w), 1) // GDN_DH
    sel_b = (lane == SMALL_B + head).astype(F32)
    sel_a = (lane == SMALL_A + head).astype(F32)
    prow = _iota2((8, 128), 0)
    a_log = jnp.sum(prm * (prow == 0).astype(F32), axis=0, keepdims=True)
    dt_b = jnp.sum(prm * (prow == 1).astype(F32), axis=0, keepdims=True)
    beta = _sigmoid(mdot(small, sel_b))
    g = mdot(-jnp.exp(a_log) * _softplus(small + dt_b), sel_a)
    return g, beta


CONV_ROWS = 1024
CONV_COLS = 128
CONV_BLOCK0 = 1536 // CONV_COLS


def _shift_down(prev8, cur, s):
    ext = jnp.concatenate([prev8, cur], axis=0)
    return pltpu.roll(ext, s, 0)[8:]


def _shift_up(cur, next8, s):
    n = cur.shape[0]
    ext = jnp.concatenate([cur, next8], axis=0)
    return pltpu.roll(ext, n + 8 - s, 0)[:n]


def _conv_pre(x_ref, w, ci, nchunk):
    r0 = pl.multiple_of(ci * CONV_ROWS, CONV_ROWS)
    cur = x_ref[pl.ds(r0, CONV_ROWS), :]
    prev = x_ref[pl.ds(pl.multiple_of(jnp.maximum(r0 - 8, 0), 8), 8), :]
    prev = jnp.where(ci > 0, prev, 0.0)
    shifted = [cur] + [_shift_down(prev, cur, s) for s in range(1, CONV_W)]
    pre = w[CONV_W - 1:CONV_W, :] * cur
    for s in range(1, CONV_W):
        pre = pre + w[CONV_W - 1 - s:CONV_W - s, :] * shifted[s]
    return r0, pre, shifted


def conv_fwd(proj, conv_w8):
    s = proj.shape[0]
    nchunk = s // CONV_ROWS
    ncol = 3 * GDN_HEADS * GDN_DH // CONV_COLS

    def kern(x_ref, w_ref, y_ref):
        w = w_ref[...]

        def step(ci, carry):
            r0, pre, _ = _conv_pre(x_ref, w, ci, nchunk)
            y_ref[pl.ds(r0, CONV_ROWS), :] = _silu(pre)
            return carry

        lax.fori_loop(0, nchunk, step, 0)

    return pl.pallas_call(
        kern, name="conv_fwd", grid=(ncol,),
        in_specs=[pl.BlockSpec((s, CONV_COLS), lambda j: (0, CONV_BLOCK0 + j)),
                  pl.BlockSpec((8, CONV_COLS), lambda j: (0, j))],
        out_specs=pl.BlockSpec((s, CONV_COLS), lambda j: (0, j)),
        out_shape=jax.ShapeDtypeStruct((s, ncol * CONV_COLS), F32),
        compiler_params=_params(("parallel",)),
    )(proj, conv_w8)


def conv_bwd(proj, conv_w8, dy):
    s = proj.shape[0]
    nchunk = s // CONV_ROWS
    per = 3 * GDN_HEADS * GDN_DH // CONV_COLS
    outs = []
    for part in range(1):
        def kern(x_ref, w_ref, dy_ref, dx_ref, dw_ref, dpre_ref):
            w = w_ref[...]
            rows8 = _iota2((8, CONV_COLS), 0)

            def step1(ci, dw):
                r0, pre, shifted = _conv_pre(x_ref, w, ci, nchunk)
                sg = _sigmoid(pre)
                dpre = dy_ref[pl.ds(r0, CONV_ROWS), :] * sg * (1.0 + pre * (1.0 - sg))
                dpre_ref[pl.ds(r0, CONV_ROWS), :] = dpre
                for sh in range(CONV_W):
                    dw = dw + jnp.where(rows8 == CONV_W - 1 - sh, _colsum(dpre * shifted[sh]), 0.0)
                return dw

            dw_ref[...] = lax.fori_loop(0, nchunk, step1, jnp.zeros((8, CONV_COLS), F32))

            def step2(ci, carry):
                r0 = pl.multiple_of(ci * CONV_ROWS, CONV_ROWS)
                cur = dpre_ref[pl.ds(r0, CONV_ROWS), :]
                nxt = dpre_ref[pl.ds(pl.multiple_of(jnp.minimum(r0 + CONV_ROWS, s - 8), 8), 8), :]
                nxt = jnp.where(ci < nchunk - 1, nxt, 0.0)
                dx = w[CONV_W - 1:CONV_W, :] * cur
                for sh in range(1, CONV_W):
                    dx = dx + w[CONV_W - 1 - sh:CONV_W - sh, :] * _shift_up(cur, nxt, sh)
                dx_ref[pl.ds(r0, CONV_ROWS), :] = dx
                return carry

            lax.fori_loop(0, nchunk, step2, 0)

        outs.append(pl.pallas_call(
            kern, name=f"conv_bwd{part}", grid=(per,),
            in_specs=[pl.BlockSpec((s, CONV_COLS), lambda j, part=part: (0, CONV_BLOCK0 + part * per + j)),
                      pl.BlockSpec((8, CONV_COLS), lambda j, part=part: (0, part * per + j)),
                      pl.BlockSpec((s, CONV_COLS), lambda j: (0, j))],
            out_specs=[pl.BlockSpec((s, CONV_COLS), lambda j: (0, j)),
                       pl.BlockSpec((8, CONV_COLS), lambda j: (0, j))],
            out_shape=[jax.ShapeDtypeStruct((s, per * CONV_COLS), F32),
                       jax.ShapeDtypeStruct((8, per * CONV_COLS), F32)],
            scratch_shapes=[pltpu.VMEM((s, CONV_COLS), F32)],
            compiler_params=_params(("parallel",)),
        )(proj, conv_w8, dy))
    dx = jnp.concatenate([o[0] for o in outs], axis=1)
    dw = jnp.concatenate([o[1] for o in outs], axis=1)
    return dx, dw


FOXF_ROWS = 512
SMALL_BLOCK128 = 3584 // 128


def _log_sigmoid(x):
    return jnp.minimum(x, 0.0) - jnp.log(1.0 + jnp.exp(-jnp.abs(x)))


def fox_f_fwd(proj, bias_row):
    s = proj.shape[0]
    n = s // FOXF_ROWS

    def kern(x_ref, b_ref, f_ref, carry):
        @pl.when(pl.program_id(0) == 0)
        def _():
            carry[...] = jnp.zeros_like(carry)

        heads = _iota2((FOXF_ROWS, 128), 1) < FOX_HEADS
        lf = jnp.where(heads, _log_sigmoid(x_ref[...] + b_ref[...]), 0.0)
        ltri = (_iota2((FOXF_ROWS, FOXF_ROWS), 0) >= _iota2((FOXF_ROWS, FOXF_ROWS), 1)).astype(F32)
        c = hdot(ltri, lf) + carry[...]
        f_ref[...] = c
        carry[...] = c[FOXF_ROWS - 1:FOXF_ROWS, :]

    return pl.pallas_call(
        kern, name="fox_f_fwd", grid=(n,),
        in_specs=[pl.BlockSpec((FOXF_ROWS, 128), lambda i: (i, SMALL_BLOCK128)),
                  pl.BlockSpec((1, 128), lambda i: (0, 0))],
        out_specs=pl.BlockSpec((FOXF_ROWS, 128), lambda i: (i, 0)),
        out_shape=jax.ShapeDtypeStruct((s, 128), F32),
        scratch_shapes=[pltpu.VMEM((1, 128), F32)],
        compiler_params=_params(("arbitrary",)),
    )(proj, bias_row)


def fox_f_bwd(proj, bias_row, d_f):
    s = proj.shape[0]
    n = s // FOXF_ROWS

    def kern(x_ref, b_ref, df_ref, dx_ref, db_ref, carry):
        @pl.when(pl.program_id(0) == 0)
        def _():
            carry[...] = jnp.zeros_like(carry)
            db_ref[...] = jnp.zeros_like(db_ref)

        heads = _iota2((FOXF_ROWS, 128), 1) < FOX_HEADS
        utri = (_iota2((FOXF_ROWS, FOXF_ROWS), 0) <= _iota2((FOXF_ROWS, FOXF_ROWS), 1)).astype(F32)
        rc = hdot(utri, df_ref[...]) + carry[...]
        carry[...] = rc[0:1, :]
        dx = jnp.where(heads, rc * _sigmoid(-(x_ref[...] + b_ref[...])), 0.0)
        dx_ref[...] = dx
        db_ref[...] += _colsum(dx)

    return pl.pallas_call(
        kern, name="fox_f_bwd", grid=(n,),
        in_specs=[pl.BlockSpec((FOXF_ROWS, 128), lambda i: (n - 1 - i, SMALL_BLOCK128)),
                  pl.BlockSpec((1, 128), lambda i: (0, 0)),
                  pl.BlockSpec((FOXF_ROWS, 128), lambda i: (n - 1 - i, 0))],
        out_specs=[pl.BlockSpec((FOXF_ROWS, 128), lambda i: (n - 1 - i, 0)),
                   pl.BlockSpec((1, 128), lambda i: (0, 0))],
        out_shape=[jax.ShapeDtypeStruct((s, 128), F32), jax.ShapeDtypeStruct((1, 128), F32)],
        scratch_shapes=[pltpu.VMEM((1, 128), F32)],
        compiler_params=_params(("arbitrary",)),
    )(proj, bias_row, d_f)


FOX_T = 512
FOX_SCALE = FOX_DH ** -0.5
FOX_PAIRS = FOX_HEADS // 2
NEG = -1e30
_NT = (((1,), (1,)), ((), ()))


def _split3(x):
    def bf(v):
        return lax.reduce_precision(v, exponent_bits=8, mantissa_bits=7)

    hi = bf(x)
    mid = bf(x - hi)
    lo = bf(x - hi - mid)
    return jnp.stack([hi, mid, lo], axis=-1)


def _fox_extras(s, first, second):
    def part(v):
        if v is None:
            return jnp.zeros((s, FOX_HEADS, 3), F32)
        if isinstance(v, float):
            return jnp.full((s, FOX_HEADS, 3), v, F32)
        pairs = v.reshape(s, FOX_PAIRS, 2)
        return _split3(jnp.stack([pairs[:, :, 1], pairs[:, :, 0]], axis=-1).reshape(s, FOX_HEADS))

    cols = jnp.concatenate([part(first), part(second)], axis=-1)
    cols = _pad_to(cols, (s, FOX_HEADS, FOX_DH)).reshape(s, FOX_PAIRS, 2 * FOX_DH)
    return cols.transpose(1, 0, 2).astype(BF)


def _head_masks(rows):
    lane = _iota2((rows, 2 * FOX_DH), 1)
    return lane < FOX_DH, lane >= FOX_DH


def _extra_lane(e, slot):
    return (FOX_DH if e == 0 else 0) + slot


def fox_fwd(qkv, xq, xk, xv):
    s = qkv.shape[0]
    t = min(FOX_T, s)
    n = s // t

    def kern(q_ref, k_ref, v_ref, xq_ref, xk_ref, xv_ref, o_ref, lse_ref):
        i = pl.program_id(1)
        masks = _head_masks(t)
        q_pair, x_pair = q_ref[...] * FOX_SCALE, xq_ref[...]
        q_ops = [jnp.where(mk, q_pair, x_pair) for mk in masks]

        def step(j, carry, masked):
            sl = pl.ds(pl.multiple_of(j * t, t), t)
            k_pair, xk_pair, v_pair, xv_pair = k_ref[sl, :], xk_ref[sl, :], v_ref[sl, :], xv_ref[sl, :]
            k_ops = [jnp.where(mk, k_pair, xk_pair) for mk in masks]
            v_ops = [jnp.where(mk, v_pair, xv_pair) for mk in masks]
            sc = [lax.dot_general(q_ops[e], k_ops[e], _NT, preferred_element_type=F32) for e in range(2)]
            if masked:
                keep = _iota2((t, t), 0) >= _iota2((t, t), 1)
                sc = [jnp.where(keep, x, NEG) for x in sc]
            m_new = [jnp.maximum(carry[e][0], jnp.max(sc[e], axis=1, keepdims=True)) for e in range(2)]
            p = [jnp.exp(sc[e] - m_new[e]).astype(BF) for e in range(2)]
            pv = [jnp.dot(p[e], v_ops[e], preferred_element_type=F32) for e in range(2)]
            return tuple((m_new[e], jnp.exp(carry[e][0] - m_new[e]) * carry[e][1] + pv[e]) for e in range(2))

        init = tuple((jnp.full((t, 1), NEG, F32), jnp.zeros((t, 2 * FOX_DH), F32)) for _ in range(2))
        carry = lax.fori_loop(0, i, lambda j, c: step(j, c, False), init)
        carry = step(i, carry, True)
        lane = _iota2((t, 2 * FOX_DH), 1)
        outs, lses = [], []
        for e in range(2):
            m, acc = carry[e]
            l = jnp.sum(jnp.where(lane == _extra_lane(e, 0), acc, 0.0), axis=1, keepdims=True)
            outs.append(acc / l)
            lses.append(m + jnp.log(l))
        o_ref[...] = jnp.where(masks[0], outs[0], outs[1])
        head0 = 2 * pl.program_id(0)
        lse_ref[...] = jnp.where(lane == head0, lses[0], jnp.where(lane == head0 + 1, lses[1], 0.0))

    pr = FOX_PAIRS
    return pl.pallas_call(
        kern, name="fox_fwd", grid=(pr, n),
        in_specs=[pl.BlockSpec((t, 128), lambda p, i: (i, p)),
                  pl.BlockSpec((s, 128), lambda p, i: (0, pr + p)),
                  pl.BlockSpec((s, 128), lambda p, i: (0, 2 * pr + p)),
                  pl.BlockSpec((None, t, 128), lambda p, i: (p, i, 0)),
                  pl.BlockSpec((None, s, 128), lambda p, i: (p, 0, 0)),
                  pl.BlockSpec((None, s, 128), lambda p, i: (p, 0, 0))],
        out_specs=[pl.BlockSpec((t, 128), lambda p, i: (i, p)),
                   pl.BlockSpec((None, t, 128), lambda p, i: (p, i, 0))],
        out_shape=[jax.ShapeDtypeStruct((s, FOX_HEADS * FOX_DH), F32), jax.ShapeDtypeStruct((pr, s, 128), F32)],
        compiler_params=_params(("parallel", "parallel")),
    )(qkv, qkv, qkv, xq, xk, xv)


def fox_bwd(qkv, d_o, xk, xv, xqb, xdo):
    s = qkv.shape[0]
    t = min(FOX_T, s)
    n = s // t
    w = 2 * FOX_DH

    def both(blocks, slot):
        lane = _iota2(blocks[0].shape, 1)
        head0 = 2 * pl.program_id(0)
        own = jnp.where(lane < FOX_DH, blocks[0], blocks[1])
        sums = [jnp.sum(jnp.where(lane == _extra_lane(e, slot), blocks[e], 0.0), axis=1, keepdims=True)
                for e in range(2)]
        return own, jnp.where(lane == head0, sums[0], jnp.where(lane == head0 + 1, sums[1], 0.0))

    def kern(k_ref, v_ref, xk_ref, xv_ref, q_ref, do_ref, xq_ref, xd_ref,
             dq_ref, dk_ref, dv_ref, sq_ref, sk_ref, dq_acc):
        j = pl.program_id(1)

        @pl.when(j == 0)
        def _():
            dq_acc[...] = jnp.zeros_like(dq_acc)

        masks = _head_masks(t)
        k_ops = [jnp.where(mk, k_ref[...], xk_ref[...]) for mk in masks]
        v_ops = [jnp.where(mk, v_ref[...], xv_ref[...]) for mk in masks]
        k_t = [x.T for x in k_ops]

        def step(i, carry, masked):
            dk, dv = carry
            sl = pl.ds(pl.multiple_of(i * t, t), t)
            q_pair, xq_pair, do_pair, xd_pair = q_ref[sl, :] * FOX_SCALE, xq_ref[sl, :], do_ref[sl, :], xd_ref[sl, :]
            q_ops = [jnp.where(mk, q_pair, xq_pair) for mk in masks]
            do_ops = [jnp.where(mk, do_pair, xd_pair) for mk in masks]
            q_t = [x.T for x in q_ops]
            do_t = [jnp.where(mk, do_pair, 0).astype(BF).T for mk in masks]
            st = [lax.dot_general(k_ops[e], q_ops[e], _NT, preferred_element_type=F32) for e in range(2)]
            dp = [lax.dot_general(v_ops[e], do_ops[e], _NT, preferred_element_type=F32) for e in range(2)]
            if masked:
                keep = _iota2((t, t), 0) <= _iota2((t, t), 1)
                st = [jnp.where(keep, x, NEG) for x in st]
            pt = [jnp.exp(x) for x in st]
            dsb = [(pt[e] * dp[e]).astype(BF) for e in range(2)]
            dv = dv + sum(lax.dot_general(do_t[e], pt[e].astype(BF), _NT, preferred_element_type=F32)
                          for e in range(2))
            dk = tuple(dk[e] + lax.dot_general(q_t[e], dsb[e], _NT, preferred_element_type=F32) for e in range(2))
            for e in range(2):
                dq_acc[i, e * w:(e + 1) * w, :] += jnp.dot(k_t[e], dsb[e], preferred_element_type=F32)
            return dk, dv

        init = ((jnp.zeros((w, t), F32), jnp.zeros((w, t), F32)), jnp.zeros((w, t), F32))
        carry = step(j, init, True)
        dk, dv = lax.fori_loop(j + 1, n, lambda i, c: step(i, c, False), carry)
        dk_ref[...], sk_ref[...] = both([x.T for x in dk], 3)
        dv_ref[...] = dv.T

        @pl.when(j == n - 1)
        def _():
            def out(r, carry):
                sl = pl.ds(pl.multiple_of(r * t, t), t)
                own, sums = both([dq_acc[r, e * w:(e + 1) * w, :].T for e in range(2)], 0)
                dq_ref[sl, :] = own * FOX_SCALE
                sq_ref[sl, :] = sums
                return carry

            lax.fori_loop(0, n, out, 0)

    pr = FOX_PAIRS
    flat = jax.ShapeDtypeStruct((s, FOX_HEADS * FOX_DH), F32)
    tile = pl.BlockSpec((t, 128), lambda p, j: (j, p))
    whole = pl.BlockSpec((s, 128), lambda p, j: (0, p))
    return pl.pallas_call(
        kern, name="fox_bwd", grid=(pr, n),
        in_specs=[pl.BlockSpec((t, 128), lambda p, j: (j, pr + p)),
                  pl.BlockSpec((t, 128), lambda p, j: (j, 2 * pr + p)),
                  pl.BlockSpec((None, t, 128), lambda p, j: (p, j, 0)),
                  pl.BlockSpec((None, t, 128), lambda p, j: (p, j, 0)),
                  whole, whole,
                  pl.BlockSpec((None, s, 128), lambda p, j: (p, 0, 0)),
                  pl.BlockSpec((None, s, 128), lambda p, j: (p, 0, 0))],
        out_specs=[whole, tile, tile, whole, tile],
        out_shape=[flat] * 5,
        scratch_shapes=[pltpu.VMEM((n, 2 * w, t), F32)],
        compiler_params=_params(("parallel", "arbitrary")),
    )(qkv, qkv, xk, xv, qkv, d_o, xqb, xdo)


def _xattn_head(q, k, v):
    sc = bdot(q, k, 1, 1) * (MEM_DH ** -0.5)
    e = jnp.exp(sc - lax.stop_gradient(jnp.max(sc, axis=-1, keepdims=True)))
    p = e / jnp.sum(e, axis=-1, keepdims=True)
    return bdot(p, v, 1, 0)


def xattn_fwd(q, kv):
    s = q.shape[0]
    hh = MEM_HEADS

    def body(*vals):
        qs, ks, vs = vals[:hh], vals[hh:2 * hh], vals[2 * hh:]
        return jnp.concatenate([_xattn_head(qs[a], ks[a], vs[a]) for a in range(hh)], axis=1)

    return rowcall(body, [(q, MEM_DH, a) for a in range(hh)],
                   [(kv, MEM_DH, a) for a in range(2 * hh)],
                   [(hh * MEM_DH, BF)], rows=512, total=s, name="xattn_fwd")[0]


def xattn_bwd(q, kv, d_o):
    s = q.shape[0]
    hh = MEM_HEADS

    def body(*vals):
        qs, dos = vals[:hh], vals[hh:2 * hh]
        ks, vs = vals[2 * hh:3 * hh], vals[3 * hh:]
        dqs, dks, dvs = [], [], []
        for a in range(hh):
            _, vjp = jax.vjp(_xattn_head, qs[a], ks[a], vs[a])
            dq, dk, dv = vjp(dos[a])
            dqs.append(dq)
            dks.append(dk)
            dvs.append(dv)
        return jnp.concatenate(dqs, axis=1), jnp.concatenate(dks + dvs, axis=1)

    return rowcall(body, [(q, MEM_DH, a) for a in range(hh)] + [(d_o, MEM_DH, a) for a in range(hh)],
                   [(kv, MEM_DH, a) for a in range(2 * hh)],
                   [(hh * MEM_DH, BF)], [kv.shape], rows=512, total=s, name="xattn_bwd")


def _slab(ref, axis, start, size):
    if axis is None:
        return ref
    if axis == "lead":
        return ref.at[start]
    idx = pl.ds(pl.multiple_of(start, 128 if axis == 1 else 16), size)
    return ref.at[idx] if axis == 0 else ref.at[:, idx]


def exchange(inputs, outputs, transfers, name):
    ni, no, nt = len(inputs), len(outputs), len(transfers)
    npeer = N_DEV - 1

    def body(*refs):
        ins, outs = refs[:ni], refs[ni:ni + no]
        send, recv, loc = refs[ni + no:]
        x, y, c = lax.axis_index("x"), lax.axis_index("y"), lax.axis_index("c")
        me = 4 * x + 2 * y + c

        def peer(p):
            px = 1 - x if p & 4 else x
            py = 1 - y if p & 2 else y
            pc = 1 - c if p & 1 else c
            return (px, py, pc), 4 * px + 2 * py + pc

        def view(ref, spec, who):
            axis, off, stride, size = spec
            return _slab(ref, axis, off + who * stride, size)

        local, remote = [], []
        for w, (ii, src, oi, dst) in enumerate(transfers):
            cp = pltpu.make_async_copy(view(ins[ii], src, me), view(outs[oi], dst, me), loc.at[w])
            cp.start()
            local.append(cp)
        for p in range(1, N_DEV):
            dev, idx = peer(p)
            for w, (ii, src, oi, dst) in enumerate(transfers):
                k = w * npeer + p - 1
                out_cp = pltpu.make_async_remote_copy(
                    src_ref=view(ins[ii], src, idx), dst_ref=view(outs[oi], dst, me), send_sem=send.at[k],
                    recv_sem=recv.at[k], device_id=dev, device_id_type=MESH)
                out_cp.start()
                in_cp = pltpu.make_async_remote_copy(
                    src_ref=view(ins[ii], src, idx), dst_ref=view(outs[oi], dst, idx), send_sem=send.at[k],
                    recv_sem=recv.at[k], device_id=dev, device_id_type=MESH)
                remote.append((out_cp, in_cp))
        for out_cp, in_cp in remote:
            in_cp.wait_recv()
            out_cp.wait_send()
        for cp in local:
            cp.wait()

    hbm = pl.BlockSpec(memory_space=pl.ANY)
    return pl.pallas_call(
        body, name=name, in_specs=[hbm] * ni, out_specs=[hbm] * no, out_shape=list(outputs),
        scratch_shapes=[pltpu.SemaphoreType.DMA((nt * npeer,)), pltpu.SemaphoreType.DMA((nt * npeer,)),
                        pltpu.SemaphoreType.DMA((nt,))],
        compiler_params=pltpu.CompilerParams(has_side_effects=True),
    )(*inputs)


def _peer(p):
    x, y, c = lax.axis_index("x"), lax.axis_index("y"), lax.axis_index("c")
    px = 1 - x if p & 4 else x
    py = 1 - y if p & 2 else y
    pc = 1 - c if p & 1 else c
    return (px, py, pc), 4 * px + 2 * py + pc


def _view(ref, spec, who):
    axis, off, stride, size = spec
    return _slab(ref, axis, off + who * stride, size)


def place_own(inputs, outputs, transfers):
    me = 4 * lax.axis_index("x") + 2 * lax.axis_index("y") + lax.axis_index("c")
    lands = [lax.empty(o.shape, o.dtype) for o in outputs]
    for ii, src, oi, dst in transfers:
        axis, off, stride, size = src
        own = inputs[ii] if axis is None else lax.dynamic_slice_in_dim(inputs[ii], off + me * stride, size, axis)
        axis, off, stride, size = dst
        if axis == "lead":
            lands[oi] = lax.dynamic_update_slice_in_dim(lands[oi], own[None], me, 0)
        else:
            lands[oi] = lax.dynamic_update_slice_in_dim(lands[oi], own, off + me * stride, axis)
    return lands


_HBM = pl.BlockSpec(memory_space=pltpu.HBM)
_SEM = pl.BlockSpec(memory_space=pltpu.SEMAPHORE)
_EFFECT = pltpu.SideEffectType.DATAFLOW_SIDE_EFFECTING


def _remote_copies(ins, lands, transfers, send, recv):
    npeer = N_DEV - 1
    me = 4 * lax.axis_index("x") + 2 * lax.axis_index("y") + lax.axis_index("c")
    pairs = []
    for p in range(1, N_DEV):
        dev, idx = _peer(p)
        for w, (ii, src, oi, dst) in enumerate(transfers):
            k = w * npeer + p - 1
            common = dict(src_ref=_view(ins[ii], src, idx), send_sem=send.at[k], recv_sem=recv.at[k],
                          device_id=dev, device_id_type=MESH)
            pairs.append((pltpu.make_async_remote_copy(dst_ref=_view(lands[oi], dst, me), **common),
                          pltpu.make_async_remote_copy(dst_ref=_view(lands[oi], dst, idx), **common)))
    return pairs


def exchange_start(inputs, lands, transfers, after, name):
    ni, nl, nsem = len(inputs), len(lands), len(transfers) * (N_DEV - 1)

    def body(*refs):
        ins, lnd = refs[:ni], refs[ni:ni + nl]
        send, recv = refs[ni + nl + 1], refs[ni + nl + 2]
        token = refs[-1]
        for out_cp, _ in _remote_copies(ins, lnd, transfers, send, recv):
            out_cp.start()
        token[...] = jnp.zeros_like(token)

    args = [pltpu.with_memory_space_constraint(a, pltpu.HBM) for a in list(inputs) + list(lands)]
    res = pl.pallas_call(
        body, name=name,
        out_shape=(pltpu.SemaphoreType.DMA((nsem,)), pltpu.SemaphoreType.DMA((nsem,)),
                   *[pltpu.HBM(a.shape, a.dtype) for a in args], jax.ShapeDtypeStruct((8, 128), F32)),
        in_specs=[_HBM] * (ni + nl) + [pl.BlockSpec(memory_space=pl.ANY)],
        out_specs=(_SEM, _SEM, *[_HBM] * (ni + nl), pl.BlockSpec(memory_space=pltpu.VMEM)),
        input_output_aliases={k: k + 2 for k in range(ni + nl)},
        compiler_params=pltpu.CompilerParams(has_side_effects=_EFFECT),
    )(*args, after)
    return res[0], res[1], list(res[2:2 + ni]), list(res[2 + ni:2 + ni + nl]), res[-1]


def exchange_wait(send, recv, inputs, lands, after, transfers, name):
    ni, nl = len(inputs), len(lands)

    def body(*refs):
        ins, lnd = refs[:ni], refs[ni:ni + nl]
        send_r, recv_r = refs[ni + nl], refs[ni + nl + 1]
        for out_cp, in_cp in _remote_copies(ins, lnd, transfers, send_r, recv_r):
            out_cp.wait_send()
            in_cp.wait_recv()

    res = pl.pallas_call(
        body, name=name,
        out_shape=tuple(pltpu.HBM(a.shape, a.dtype) for a in list(inputs) + list(lands)),
        in_specs=[_HBM] * (ni + nl) + [_SEM, _SEM, pl.BlockSpec(memory_space=pl.ANY)],
        out_specs=tuple([_HBM] * (ni + nl)),
        input_output_aliases={k: k for k in range(ni + nl)},
        compiler_params=pltpu.CompilerParams(has_side_effects=_EFFECT),
    )(*inputs, *lands, send, recv, after)
    return list(res[ni:])


def adamw(w, m, v, contribs, name):
    r, c = w.shape
    nc = len(contribs)
    rows = next((r // d for d in (4, 2) if r % d == 0 and (r // d) % 16 == 0), r)
    c1, c2 = 1.0 - ADAM_B1 ** ADAM_STEP, 1.0 - ADAM_B2 ** ADAM_STEP

    def body(wv, mv, vv, *gs):
        g = gs[0].astype(F32)
        for extra in gs[1:]:
            g = g + extra.astype(F32)
        g = g[:, :c]
        m_new = ADAM_B1 * mv + (1.0 - ADAM_B1) * g
        v_new = ADAM_B2 * vv + (1.0 - ADAM_B2) * (g * g)
        delta = -ADAM_LR * ((m_new / c1) / (jnp.sqrt(v_new / c2) + ADAM_EPS) + ADAM_WD * wv)
        return g, delta, m_new, v_new

    assert nc >= 1
    return rowcall(body, [w, m, v] + list(contribs), [], [(c, F32)] * 4, rows=rows, total=r, name=name)


WEIGHTS = ['ffn1_pre_norm', 'ffn1_w_gate', 'ffn1_w_up', 'ffn1_w_down', 'ffn1_post_norm', 'mix_pre_norm', 'w_in',
           'fox_f_bias', 'gdn_conv_w', 'gdn_a_log', 'gdn_dt_bias', 'gdn_out_norm', 'w_out', 'mix_post_norm',
           'mem_pre_norm', 'mem_kv_norm', 'mem_w_q', 'mem_w_kv', 'mem_w_o', 'mem_post_norm', 'ffn2_pre_norm',
           'ffn2_w_gate', 'ffn2_w_up', 'ffn2_w_down', 'ffn2_post_norm']
GAINS = ['ffn1_pre_norm', 'ffn1_post_norm', 'mix_pre_norm', 'mix_post_norm', 'mem_pre_norm', 'mem_kv_norm',
         'mem_post_norm', 'ffn2_pre_norm', 'ffn2_post_norm']
BIG = ['ffn1_w_gate', 'ffn1_w_up', 'ffn1_w_down', 'w_in', 'w_out', 'mem_w_q', 'mem_w_kv', 'mem_w_o',
       'ffn2_w_gate', 'ffn2_w_up', 'ffn2_w_down']
PACK_ROWS = 24
ROW_MISC = len(GAINS)
ROW_CONV = ROW_MISC + 1
COL_FBIAS, COL_ALOG, COL_DTB, COL_ONORM, COL_LOSS = 0, 8, 12, 128, 256
CONV_CH = 3 * GDN_HEADS * GDN_DH


def _pad_to(a, shape):
    return jnp.pad(a, [(0, t - s) for s, t in zip(a.shape, shape)])


def _pack(get, conv=None, loss=None):
    rows = [get(nm) for nm in GAINS]
    misc = jnp.concatenate([get('fox_f_bias'), get('gdn_a_log'), get('gdn_dt_bias'),
                            jnp.zeros((1, COL_ONORM - COL_DTB - 4), F32), get('gdn_out_norm'),
                            jnp.zeros((1, 1), F32) if loss is None else loss.reshape(1, 1)], axis=1)
    rows.append(_pad_to(misc, (1, D_MODEL)))
    rows.append(jnp.zeros((6, D_MODEL), F32) if conv is None else conv.reshape(6, D_MODEL))
    return _pad_to(jnp.concatenate(rows, axis=0), (PACK_ROWS, D_MODEL))


def _unpack(p):
    out = {nm: p[i:i + 1] for i, nm in enumerate(GAINS)}
    misc = p[ROW_MISC:ROW_MISC + 1]
    out['fox_f_bias'] = misc[:, COL_FBIAS:COL_FBIAS + FOX_HEADS]
    out['gdn_a_log'] = misc[:, COL_ALOG:COL_ALOG + GDN_HEADS]
    out['gdn_dt_bias'] = misc[:, COL_DTB:COL_DTB + GDN_HEADS]
    out['gdn_out_norm'] = misc[:, COL_ONORM:COL_ONORM + GDN_DH]
    return out


def _ffn_fwd(h, pre, wgu, wd, tag, u=None):
    s = h.shape[0]
    if u is None:
        u, = rowcall(_rms, [h], [pre], [(D_MODEL, BF)], rows=512, total=s, name=tag + "_pre")
    if callable(wgu):
        wgu = wgu(u)
    gate, up, act = mm_swiglu(u, wgu, name=tag + "_gate_up")
    if callable(wd):
        wd = wd(act)
    f = mm(act, wd, name=tag + "_down")
    return u, gate, up, act, f


def _half_rms(a, g):
    return 0.5 * _rms(a, g)


def _ffn_bwd(dh_out, h, pre, post, wgu, wd, saved, tag, on_dwd=None, on_dwgu=None, post_done=None, first_token=None,
             next_res=None):
    u, gate, up, act, f = saved
    s = h.shape[0]

    def b_post(dh, fv, pg):
        return jax.vjp(_half_rms, fv, pg)[1](dh)

    if post_done is not None:
        df, dpost = post_done
    else:
        df, dpost = rowcall(b_post, [dh_out, f], [post], [(D_MODEL, BF)], [(1, D_MODEL)], rows=512, total=s,
                            name=tag + "_bwd_post")
    dwd = mm(act, df, ta=True, out_dtype=BF, name=tag + "_bwd_dwd", token=first_token)
    dgate, dup = mm_dswiglu(df, wd, gate, up, name=tag + "_bwd_dact", token=on_dwd(dwd) if on_dwd else None)
    dwg = mm(u, dgate, ta=True, out_dtype=BF, name=tag + "_bwd_dwg")
    dwu = mm(u, dup, ta=True, out_dtype=BF, name=tag + "_bwd_dwu")
    du = mm_pair(dgate, dup, wgu, name=tag + "_bwd_du", token=on_dwgu(dwg, dwu) if on_dwgu else None)

    def b_pre(dh, duv, hv, pg):
        dx, dpre = jax.vjp(_rms, hv, pg)[1](duv)
        return dh + dx, dpre

    if next_res is None:
        dh, dpre = rowcall(b_pre, [dh_out, du, h], [pre], [(D_MODEL, F32)], [(1, D_MODEL)], rows=512, total=s,
                           name=tag + "_bwd_pre")
        return dh, dwg, dwu, dwd, dpre, dpost

    def b_pre_next(dh, duv, hv, av, pg, gn):
        dx, dpre_ = jax.vjp(_rms, hv, pg)[1](duv)
        dhn = dh + dx
        dav, dgn = jax.vjp(_rms, av, gn)[1](dhn)
        return dhn, dav, dpre_, dgn

    dh, d_next, dpre, dg_next = rowcall(b_pre_next, [dh_out, du, h, next_res[0]], [pre, next_res[1]],
                                        [(D_MODEL, F32), (D_MODEL, BF)], [(1, D_MODEL), (1, D_MODEL)], rows=512,
                                        total=s, name=tag + "_bwd_pre")
    return dh, dwg, dwu, dwd, dpre, dpost, d_next, dg_next


def _step(a):
    x, mem = a['x'][0], a['mem'][0]
    s = x.shape[0]
    me = 4 * lax.axis_index("x") + 2 * lax.axis_index("y") + lax.axis_index("c")
    w2 = {nm: a[nm][0] for nm in WEIGHTS}
    m2 = {nm: a['m_' + nm][0] for nm in WEIGHTS}
    v2 = {nm: a['v_' + nm][0] for nm in WEIGHTS}
    small = {nm: w2[nm][None] for nm in WEIGHTS if nm not in BIG and nm != 'gdn_conv_w'}

    def ff_cols(w):
        return _pad_to(w, (D_MODEL, FF_SHARD_PAD)).astype(BF)

    def ff_rows(w):
        return _pad_to(w, (FF_SHARD_PAD, D_MODEL)).astype(BF)

    whole = (None, 0, 0, 0)
    conv_pad = 256
    g_in = [ff_cols(w2['ffn1_w_gate']), ff_cols(w2['ffn1_w_up']), ff_rows(w2['ffn1_w_down']),
            ff_cols(w2['ffn2_w_gate']), ff_cols(w2['ffn2_w_up']), ff_rows(w2['ffn2_w_down']),
            _pad_to(w2['w_in'], (D_MODEL, IN_SHARD_PAD)).astype(BF), w2['w_out'].astype(BF),
            w2['mem_w_q'].astype(BF), w2['mem_w_kv'].astype(BF), w2['mem_w_o'].astype(BF),
            _pad_to(w2['gdn_conv_w'], (8, conv_pad))]
    g_out = [jax.ShapeDtypeStruct((D_MODEL, 2 * D_FF_PAD), BF), jax.ShapeDtypeStruct((D_FF_PAD, D_MODEL), BF),
             jax.ShapeDtypeStruct((D_MODEL, 2 * D_FF_PAD), BF), jax.ShapeDtypeStruct((D_FF_PAD, D_MODEL), BF),
             jax.ShapeDtypeStruct((D_MODEL, N_DEV * IN_SHARD_PAD), BF), jax.ShapeDtypeStruct((D_MODEL, D_MODEL), BF),
             jax.ShapeDtypeStruct((D_MODEL, D_MODEL), BF), jax.ShapeDtypeStruct((D_MODEL, 2 * D_MODEL), BF),
             jax.ShapeDtypeStruct((D_MODEL, D_MODEL), BF), jax.ShapeDtypeStruct((8, N_DEV * conv_pad), F32)]
    sp_, dm = FF_SHARD_PAD, D_MODEL // N_DEV
    g_tr = [(0, whole, 0, (1, 0, sp_, sp_)), (1, whole, 0, (1, D_FF_PAD, sp_, sp_)), (2, whole, 1, (0, 0, sp_, sp_)),
            (3, whole, 2, (1, 0, sp_, sp_)), (4, whole, 2, (1, D_FF_PAD, sp_, sp_)), (5, whole, 3, (0, 0, sp_, sp_)),
            (6, whole, 4, (1, 0, IN_SHARD_PAD, IN_SHARD_PAD)), (7, whole, 5, (0, 0, dm, dm)),
            (8, whole, 6, (0, 0, dm, dm)), (9, whole, 7, (1, 0, 2 * dm, 2 * dm)), (10, whole, 8, (0, 0, dm, dm)),
            (11, whole, 9, (1, 0, conv_pad, conv_pad))]
    def pick(idx):
        ins = sorted({g_tr[k][0] for k in idx})
        outs = sorted({g_tr[k][2] for k in idx})
        tr = [(ins.index(g_tr[k][0]), g_tr[k][1], outs.index(g_tr[k][2]), g_tr[k][3]) for k in idx]
        return [g_in[i] for i in ins], [g_out[o] for o in outs], tr

    stages, after = [], g_in[0]
    for nm, idx in (("gate_up", [0, 1]), ("down", [2]), ("mix", [6, 7, 11]), ("late", [8, 9, 10, 3, 4, 5])):
        st_in, st_out, st_tr = pick(idx)
        st = exchange_start(st_in, place_own(st_in, st_out, st_tr), st_tr, after, "gather_%s_start" % nm)
        stages.append((st, st_tr, "gather_%s_wait" % nm))
        after = st[4]
    g_token = after

    def gather_wait(k, after_):
        (send_, recv_, src_, land_, _), tr_, nm_ = stages[k]
        return exchange_wait(send_, recv_, src_, land_, after_, tr_, nm_)

    bias_row = _pad_to(small['fox_f_bias'], (1, 128))
    gate_prm = _pad_to(jnp.concatenate([_pad_to(small['gdn_a_log'], (1, 128 - SMALL_A)),
                                        _pad_to(small['gdn_dt_bias'], (1, 128 - SMALL_A))], axis=0),
                       (8, 128 - SMALL_A))
    gate_prm = jnp.pad(gate_prm, ((0, 0), (SMALL_A, 0)))
    onorm = small['gdn_out_norm']

    late = {}

    def wgu1_when(u):
        late['wgu1'], = gather_wait(0, u)
        return late['wgu1']

    def wd1_when(act):
        late['wd1'], = gather_wait(1, act)
        return late['wd1']

    sv1 = _ffn_fwd(x, small['ffn1_pre_norm'] + g_token[0, 0], wgu1_when, wd1_when, "ffn1")
    wgu1, wd1 = late['wgu1'], late['wd1']
    def b_out_pre(h, f, g_post, g_pre):
        hn = h + _half_rms(f, g_post)
        return hn, _rms(hn, g_pre)

    h1, u2 = rowcall(b_out_pre, [x, sv1[4]], [small['ffn1_post_norm'], small['mix_pre_norm']],
                     [(D_MODEL, F32), (D_MODEL, BF)], rows=512, total=s, name="ffn1_out")
    w_in_g, w_out, conv_g = gather_wait(2, h1)
    w_in = jnp.concatenate([w_in_g[:, j * IN_SHARD_PAD:j * IN_SHARD_PAD + IN_SHARD] for j in range(N_DEV)],
                           axis=1)
    sp = [0, 512, 1024, 1536, 1544, 2056, 2568, 3080, 3592, 3596, 3600]
    fq, fk, fv, ff, gq, gk, gv, gz, gb, ga = [w_in[:, sp[i]:sp[i + 1]] for i in range(10)]
    w_proj = jnp.concatenate([fq, fk, fv, gq, gk, gv, gz, ff, gb, ga,
                              jnp.zeros((D_MODEL, PROJ_W - 3584 - 16), BF)], axis=1)
    conv_w8 = conv_g.reshape(8, N_DEV, conv_pad)[:, :, :CONV_CH // N_DEV].reshape(8, CONV_CH)


    proj = mm(u2, w_proj, name="mix_proj")
    f_cum = fox_f_fwd(proj, bias_row)
    f_heads = f_cum[:, :FOX_HEADS]
    qkv_bf = proj[:, :3 * FOX_HEADS * FOX_DH].astype(BF)
    xk, xv = _fox_extras(s, 1.0, -f_heads), _fox_extras(s, 1.0, None)
    fox_flat, lse = fox_fwd(qkv_bf, _fox_extras(s, f_heads, 1.0), xk, xv)
    lse_heads = jnp.sum(lse, axis=0)[:, :FOX_HEADS]
    cqkv = conv_fwd(proj, conv_w8)
    g_l, b_l = rowcall(_gdn_gates, [(proj, 128, SMALL_BLOCK128)], [gate_prm], [(512, F32), (512, F32)],
                       rows=512, total=s, name="gdn_gates")
    gbb = jnp.concatenate([g_l, b_l], axis=1)
    gdn_o, states = gdn_fwd(cqkv, proj, gbb, onorm)
    mixed = jnp.concatenate([fox_flat, gdn_o], axis=1).astype(BF)
    mo = mm(mixed, w_out, name="mix_out")
    def b_res_pre(h, a_, g_post, g_pre):
        hn = h + _rms(a_, g_post)
        return hn, _rms(hn, g_pre)

    h2, hq = rowcall(b_res_pre, [h1, mo], [small['mix_post_norm'], small['mem_pre_norm']],
                     [(D_MODEL, F32), (D_MODEL, BF)], rows=512, total=s, name="mix_res")
    mn, = rowcall(_rms, [mem], [small['mem_kv_norm']], [(D_MODEL, BF)], rows=256, total=mem.shape[0], name="mem_kvn")
    wgu2, wd2, w_q, w_kv, w_o = gather_wait(3, h2)
    q_mem = mm(hq, w_q, name="mem_q")
    kv_mem = mm(mn, w_kv, name="mem_kv")
    o_mem = xattn_fwd(q_mem, kv_mem)
    c_mem = mm(o_mem, w_o, name="mem_o")
    h3, u3 = rowcall(b_res_pre, [h2, c_mem], [small['mem_post_norm'], small['ffn2_pre_norm']],
                     [(D_MODEL, F32), (D_MODEL, BF)], rows=512, total=s, name="mem_res")

    sv2 = _ffn_fwd(h3, small['ffn2_pre_norm'], wgu2, wd2, "ffn2", u=u3)

    def b_loss(h, f, tgt, g):
        err = h + _half_rms(f, g) - tgt
        part = 0.5 * jnp.sum(jnp.mean(err * err, axis=-1, keepdims=True), axis=0, keepdims=True)
        dyv = err * (1.0 / D_MODEL)
        dfv, dpost = jax.vjp(_half_rms, f, g)[1](dyv)
        return dyv, dfv, jnp.broadcast_to(part, (1, 128)), dpost

    dy, df2, loss_acc, dpost2 = rowcall(b_loss, [h3, sv2[4], a['loss_target'][0]], [small['ffn2_post_norm']],
                                        [(D_MODEL, F32), (D_MODEL, BF)], [(1, 128), (1, D_MODEL)], rows=512, total=s,
                                        name="loss")

    grads = {}
    lead = ("lead", 0, 1, 0)

    def land(r, c, dt=BF):
        return jax.ShapeDtypeStruct((N_DEV, r, c), dt)

    ffn_tr = [(0, (1, 0, sp_, sp_), 0, lead), (1, (1, 0, sp_, sp_), 1, lead), (2, (0, 0, sp_, FF_SHARD), 2, lead)]
    ffn_land = [land(D_MODEL, sp_), land(D_MODEL, sp_), land(FF_SHARD, D_MODEL)]

    def keep_ffn2_down(dwd):
        late['dwd2'] = dwd

    def start_ffn2_reduce(dwg, dwu):
        a_in = [dwg, dwu, late['dwd2']]
        late['a'] = exchange_start(a_in, place_own(a_in, ffn_land, ffn_tr), ffn_tr, dwu, "reduce_ffn2_start")
        return late['a'][4]

    dh3, _, _, _, grads['ffn2_pre_norm'], grads['ffn2_post_norm'], dc, grads['mem_post_norm'] = _ffn_bwd(
        dy, h3, small['ffn2_pre_norm'], small['ffn2_post_norm'], wgu2, wd2, sv2, "ffn2", post_done=(df2, dpost2),
        on_dwd=keep_ffn2_down, on_dwgu=start_ffn2_reduce, next_res=(c_mem, small['mem_post_norm']))
    a_send, a_recv, a_src, a_land, _ = late['a']

    d_o = mm(dc, w_o, tb=True, name="mem_bwd_do")
    dw_o = mm(o_mem, dc, ta=True, out_dtype=BF, name="mem_bwd_dwo")
    dq_mem, dkv = xattn_bwd(q_mem, kv_mem, d_o)
    dhq = mm(dq_mem, w_q, tb=True, name="mem_bwd_dhq")
    dw_q = mm(hq, dq_mem, ta=True, out_dtype=BF, name="mem_bwd_dwq")
    dmn = mm(dkv, w_kv, tb=True, name="mem_bwd_dmn")
    dw_kv = mm(mn, dkv, ta=True, out_dtype=BF, name="mem_bwd_dwkv")
    _, grads['mem_kv_norm'] = rowcall(lambda d, mv, g: jax.vjp(_rms, mv, g)[1](d), [dmn, mem],
                                      [small['mem_kv_norm']], [(D_MODEL, F32)], [(1, D_MODEL)], rows=256,
                                      total=mem.shape[0], name="mem_bwd_kvn")

    def b_pre(dh, duv, hv, pg):
        dx, dpre = jax.vjp(_rms, hv, pg)[1](duv)
        return dh + dx, dpre

    def b_pre_res(dh, duv, hv, mov, g_pre, g_post):
        dx, dpre = jax.vjp(_rms, hv, g_pre)[1](duv)
        dhn = dh + dx
        dmov, dpost = jax.vjp(_rms, mov, g_post)[1](dhn)
        return dhn, dmov, dpre, dpost

    dh2, dmo, grads['mem_pre_norm'], grads['mix_post_norm'] = rowcall(
        b_pre_res, [dh3, dhq, h2, mo], [small['mem_pre_norm'], small['mix_post_norm']],
        [(D_MODEL, F32), (D_MODEL, BF)], [(1, D_MODEL), (1, D_MODEL)], rows=512, total=s, name="mem_bwd_pre")
    d_mixed = mm(dmo, w_out, tb=True, name="mix_bwd_dmixed")
    dw_out = mm(mixed, dmo, ta=True, out_dtype=BF, name="mix_bwd_dwout")
    def b_delta(do, o):
        sel = (_iota2((512, 128), 0) // FOX_DH == _iota2((512, 128), 1)).astype(F32)
        return mdot(do * o, sel)

    delta, = rowcall(b_delta, [(d_mixed, 512, 0), fox_flat], [], [(128, F32)], rows=512, total=s, name="fox_delta")
    dfox_q, dfox_k, dvf, sum_q, sum_k = fox_bwd(qkv_bf, d_mixed[:, :512].astype(BF), xk, xv,
                                                _fox_extras(s, f_heads - lse_heads, 1.0),
                                                _fox_extras(s, -delta[:, :FOX_HEADS], None))
    d_f = jnp.sum((sum_q - sum_k).reshape(s, FOX_PAIRS, 2 * FOX_DH), axis=1)
    dsmall_f, dbias = fox_f_bwd(proj, bias_row, d_f)
    grads['fox_f_bias'] = dbias[:, :FOX_HEADS]
    dcqkv, dz, dgb, grads['gdn_out_norm'] = gdn_bwd(cqkv, proj, gbb, onorm, states, d_mixed)

    def b_gates(sm, dsf, dg, db, prm):
        dsm, dprm = jax.vjp(_gdn_gates, sm, prm)[1]((dg, db))
        return dsm + dsf, dprm

    dsmall, dprm = rowcall(b_gates, [(proj, 128, SMALL_BLOCK128), dsmall_f, (dgb, 512, 0), (dgb, 512, 1)], [gate_prm],
                           [(128, F32)],
                           [(8, 128)], rows=512, total=s, name="gdn_bwd_gates")
    grads['gdn_a_log'] = dprm[0:1, SMALL_A:SMALL_A + GDN_HEADS]
    grads['gdn_dt_bias'] = dprm[1:2, SMALL_A:SMALL_A + GDN_HEADS]
    dqkv_pre, dconv8 = conv_bwd(proj, conv_w8, dcqkv)
    dproj = jnp.concatenate([dfox_q, dfox_k, dvf, dqkv_pre, dz, dsmall,
                             jnp.zeros((s, PROJ_W - 3584 - 128), F32)], axis=1).astype(BF)
    du2 = mm(dproj, w_proj, tb=True, name="mix_bwd_du")
    dw_proj = mm(u2, dproj, ta=True, out_dtype=BF, name="mix_bwd_dwproj")
    def b_pre_post(dh, duv, hv, fv, g_pre, g_post):
        dx, dpre = jax.vjp(_rms, hv, g_pre)[1](duv)
        dhn = dh + dx
        dfv, dpost = jax.vjp(_half_rms, fv, g_post)[1](dhn)
        return dhn, dfv, dpre, dpost

    dh1, df1, grads['mix_pre_norm'], dpost1 = rowcall(
        b_pre_post, [dh2, du2, h1, sv1[4]], [small['mix_pre_norm'], small['ffn1_post_norm']],
        [(D_MODEL, F32), (D_MODEL, BF)], [(1, D_MODEL), (1, D_MODEL)], rows=512, total=s, name="mix_bwd_pre")

    dw_in = jnp.concatenate([dw_proj[:, :1536], dw_proj[:, 3584:3592], dw_proj[:, 1536:3584],
                             dw_proj[:, 3592:3600]], axis=1)
    gap = jnp.zeros((D_MODEL, IN_SHARD_PAD - IN_SHARD), BF)
    dw_in = jnp.concatenate([piece for j in range(N_DEV) for piece in (dw_in[:, j * IN_SHARD:(j + 1) * IN_SHARD], gap)],
                            axis=1)
    b_in = [dw_in, dw_out, dw_q, dw_kv, dw_o]
    b_tr = [(0, (1, 0, IN_SHARD_PAD, IN_SHARD_PAD), 0, lead), (1, (0, 0, dm, dm), 1, lead), (2, (0, 0, dm, dm), 2, lead),
            (3, (1, 0, 2 * dm, 2 * dm), 3, lead), (4, (0, 0, dm, dm), 4, lead)]
    b_shapes = [land(D_MODEL, IN_SHARD_PAD), land(dm, D_MODEL), land(dm, D_MODEL), land(D_MODEL, 2 * dm),
                land(dm, D_MODEL)]
    b_land = place_own(b_in, b_shapes, b_tr)
    b_send, b_recv, b_src, b_land, b_token = exchange_start(b_in, b_land, b_tr, dh1, "reduce_mix_start")

    def start_down_reduce(dwd):
        tr = ffn_tr[2:]
        tr = [(0, tr[0][1], 0, tr[0][3])]
        late['c_down'] = (exchange_start([dwd], place_own([dwd], ffn_land[2:], tr), tr, dwd, "reduce_ffn1_down_start"), tr)
        return late['c_down'][0][4]

    def start_gate_up_reduce(dwg, dwu):
        tr = ffn_tr[:2]
        late['c_gu'] = (exchange_start([dwg, dwu], place_own([dwg, dwu], ffn_land[:2], tr), tr, dwu,
                                       "reduce_ffn1_gu_start"), tr)
        return late['c_gu'][0][4]

    grad_x, _, _, _, grads['ffn1_pre_norm'], grads['ffn1_post_norm'] = _ffn_bwd(
        dh1, x, small['ffn1_pre_norm'], small['ffn1_post_norm'], wgu1, wd1, sv1, "ffn1",
        on_dwd=start_down_reduce, on_dwgu=start_gate_up_reduce, post_done=(df1, dpost1), first_token=b_token)

    gpack = _pack(lambda nm: grads[nm], conv=dconv8[:CONV_W], loss=loss_acc[:, :1])
    gsum_parts, = exchange([gpack], [land(PACK_ROWS, D_MODEL, F32)], [(0, whole, 0, lead)], "reduce_small")
    a_got = exchange_wait(a_send, a_recv, a_src, a_land, gsum_parts, ffn_tr, "reduce_ffn2_wait")
    b_got = exchange_wait(b_send, b_recv, b_src, b_land, gsum_parts, b_tr, "reduce_mix_wait")
    recv = dict(zip(['ffn2_w_gate', 'ffn2_w_up', 'ffn2_w_down', 'w_in', 'w_out', 'mem_w_q', 'mem_w_kv', 'mem_w_o'],
                    a_got + b_got))

    out_g, out_d, out_m, out_v = {}, {}, {}, {}

    def update(nm):
        r = recv[nm]
        res = adamw(w2[nm], m2[nm], v2[nm], [(r, r.shape[2], 0, d) for d in range(N_DEV)], "adamw_" + nm)
        out_g[nm], out_d[nm], out_m[nm], out_v[nm] = res

    for nm in recv:
        update(nm)
    wp = _pack(lambda nm: small[nm])
    mp = _pack(lambda nm: m2[nm][None])
    vp = _pack(lambda nm: v2[nm][None])
    pg, pd, pm, pv = adamw(wp, mp, vp, [(gsum_parts, D_MODEL, 0, d) for d in range(N_DEV)], "adamw_small")
    for dst, p in ((out_g, pg), (out_d, pd), (out_m, pm), (out_v, pv)):
        dst.update({k: val[0] for k, val in _unpack(p).items()})
    loss = pg[ROW_MISC, COL_LOSS]
    conv_g = lax.dynamic_slice_in_dim(pg[ROW_CONV:ROW_CONV + 6].reshape(CONV_W, CONV_CH), me * (CONV_CH // N_DEV),
                                      CONV_CH // N_DEV, axis=1)
    res = adamw(w2['gdn_conv_w'], m2['gdn_conv_w'], v2['gdn_conv_w'], [conv_g], "adamw_conv")
    out_g['gdn_conv_w'], out_d['gdn_conv_w'], out_m['gdn_conv_w'], out_v['gdn_conv_w'] = res

    done = sum(out_d[nm][0, 0] for nm in recv) + out_d['gdn_conv_w'][0, 0] + pd[0, 0]
    after = jnp.zeros((8, 128), F32) + done
    c_got = []
    for key, nm in (('c_gu', "reduce_ffn1_gu_wait"), ('c_down', "reduce_ffn1_down_wait")):
        (c_send, c_recv, c_src, c_land, _), tr = late[key]
        c_got += exchange_wait(c_send, c_recv, c_src, c_land, after, tr, nm)
    recv = dict(zip(['ffn1_w_gate', 'ffn1_w_up', 'ffn1_w_down'], c_got))
    for nm in recv:
        update(nm)

    def depth(t):
        return t[None]

    return (loss, grad_x[None], *[depth(out_g[nm]) for nm in WEIGHTS], *[depth(out_d[nm]) for nm in WEIGHTS],
            *[depth(out_m[nm]) for nm in WEIGHTS], *[depth(out_v[nm]) for nm in WEIGHTS])


def kernel(x, mem, ffn1_pre_norm, ffn1_w_gate, ffn1_w_up, ffn1_w_down, ffn1_post_norm, mix_pre_norm, w_in, fox_f_bias, gdn_conv_w, gdn_a_log, gdn_dt_bias, gdn_out_norm, w_out, mix_post_norm, mem_pre_norm, mem_kv_norm, mem_w_q, mem_w_kv, mem_w_o, mem_post_norm, ffn2_pre_norm, ffn2_w_gate, ffn2_w_up, ffn2_w_down, ffn2_post_norm, loss_target, m_ffn1_pre_norm, m_ffn1_w_gate, m_ffn1_w_up, m_ffn1_w_down, m_ffn1_post_norm, m_mix_pre_norm, m_w_in, m_fox_f_bias, m_gdn_conv_w, m_gdn_a_log, m_gdn_dt_bias, m_gdn_out_norm, m_w_out, m_mix_post_norm, m_mem_pre_norm, m_mem_kv_norm, m_mem_w_q, m_mem_w_kv, m_mem_w_o, m_mem_post_norm, m_ffn2_pre_norm, m_ffn2_w_gate, m_ffn2_w_up, m_ffn2_w_down, m_ffn2_post_norm, v_ffn1_pre_norm, v_ffn1_w_gate, v_ffn1_w_up, v_ffn1_w_down, v_ffn1_post_norm, v_mix_pre_norm, v_w_in, v_fox_f_bias, v_gdn_conv_w, v_gdn_a_log, v_gdn_dt_bias, v_gdn_out_norm, v_w_out, v_mix_post_norm, v_mem_pre_norm, v_mem_kv_norm, v_mem_w_q, v_mem_w_kv, v_mem_w_o, v_mem_post_norm, v_ffn2_pre_norm, v_ffn2_w_gate, v_ffn2_w_up, v_ffn2_w_down, v_ffn2_post_norm):
    return _step(dict(locals()))
```
